```python
import math
import jax, jax.numpy as jnp
from jax import lax
import numpy as np

D_MODEL = 1024
BATCH = 8
SEQ = 8192
DEPTH = 1

D_FF = 2816
D_CONV = D_MODEL
CONV_WIDTH = 31
HEAD_DIM = 64
N_HEADS = D_MODEL // HEAD_DIM
N_KV_HEADS = 4
GROUP = N_HEADS // N_KV_HEADS
WINDOW = 128
ROPE_THETA = 10000.0
EPS = 1e-6
LN_EPS = 1e-5
NEG_INF = -1e30

SPLITS = (D_CONV, D_CONV, N_HEADS * HEAD_DIM, N_KV_HEADS * HEAD_DIM, N_KV_HEADS * HEAD_DIM, D_MODEL, D_MODEL)
D_IN = sum(SPLITS)

kernel_name = "hybrid_macaron_conv_swa_gated_block"


def rmsnorm(x, g):
    xf = x.astype(jnp.float32)
    y = xf * lax.rsqrt(jnp.mean(xf * xf, axis=-1, keepdims=True) + EPS)
    return (y * g.astype(jnp.float32)).astype(x.dtype)


def layernorm(x, g, b):
    xf = x.astype(jnp.float32)
    mu = jnp.mean(xf, axis=-1, keepdims=True)
    var = jnp.mean(jnp.square(xf - mu), axis=-1, keepdims=True)
    y = (xf - mu) * lax.rsqrt(var + LN_EPS)
    return (y * g.astype(jnp.float32) + b.astype(jnp.float32)).astype(x.dtype)


def swiglu(x, w_gate, w_up, w_down):
    return (jax.nn.silu(x @ w_gate) * (x @ w_up)) @ w_down


def rope(x, positions):
    half = HEAD_DIM // 2
    inv_freq = ROPE_THETA ** (-jnp.arange(half, dtype=jnp.float32) / half)
    ang = positions.astype(jnp.float32)[..., None] * inv_freq
    cos = jnp.cos(ang)[:, :, None, :]
    sin = jnp.sin(ang)[:, :, None, :]
    xf = x.astype(jnp.float32)
    x1, x2 = xf[..., :half], xf[..., half:]
    out = jnp.concatenate([x1 * cos - x2 * sin, x2 * cos + x1 * sin], axis=-1)
    return out.astype(x.dtype)


def causal_depthwise_conv(u, w, b):
    out = lax.conv_general_dilated(
        u, w[:, None, :].astype(u.dtype), window_strides=(1,),
        padding=((CONV_WIDTH - 1, 0),),
        dimension_numbers=("NWC", "WIO", "NWC"),
        feature_group_count=u.shape[-1])
    return out + b.astype(u.dtype)


def conformer_conv_branch(glu_a, glu_b, dw_w, dw_b, ln_g, ln_b, w_proj):
    u = glu_a * jax.nn.sigmoid(glu_b)
    u = causal_depthwise_conv(u, dw_w, dw_b)
    u = jax.nn.silu(layernorm(u, ln_g, ln_b))
    return u @ w_proj


def band(t):
    B, S = t.shape[:2]
    nb = S // WINDOW
    tb = t.reshape(B, nb, WINDOW, N_KV_HEADS, HEAD_DIM)
    prev = jnp.pad(tb[:, :-1], ((0, 0), (1, 0), (0, 0), (0, 0), (0, 0)))
    return jnp.concatenate([prev, tb], axis=2)


def sliding_window_gqa_sinks(q, k, v, sinks):
    B, S = q.shape[:2]
    nb = S // WINDOW
    qb = q.reshape(B, nb, WINDOW, N_KV_HEADS, GROUP, HEAD_DIM)
    kb, vb = band(k), band(v)
    scores = jnp.einsum("bnqkgd,bnskd->bnkgqs", qb, kb).astype(jnp.float32) * (HEAD_DIM ** -0.5)
    qi = jnp.arange(WINDOW)[:, None]
    sj = jnp.arange(2 * WINDOW)[None, :] - WINDOW
    rel = qi - sj
    allowed = (rel >= 0) & (rel < WINDOW)
    blk = jnp.arange(nb)[:, None, None]
    allowed = allowed[None] & ((blk > 0) | (sj[None] >= 0))
    scores = jnp.where(allowed[None, :, None, None], scores, NEG_INF)
    sink = sinks.astype(jnp.float32).reshape(N_KV_HEADS, GROUP)[None, None, :, :, None, None]
    m = jnp.maximum(jnp.max(scores, axis=-1, keepdims=True), sink)
    p = jnp.exp(scores - m)
    probs = p / (jnp.sum(p, axis=-1, keepdims=True) + jnp.exp(sink - m))
    out = jnp.einsum("bnkgqs,bnskd->bnqkgd", probs.astype(v.dtype), vb)
    return out.reshape(B, S, N_HEADS * HEAD_DIM)


def _fwd_setup_inputs(seed: int = 0) -> dict:
    key = jax.random.key(seed)
    ks = jax.random.split(key, 24)
    f32 = jnp.float32

    def w(k, shape, fan_in):
        return jax.random.normal(k, shape, f32) * (fan_in ** -0.5)

    def gain(k, n):
        return 1.0 + 0.02 * jax.random.normal(k, (DEPTH, n), f32)

    def small(k, shape):
        return 0.02 * jax.random.normal(k, shape, f32)

    x = jax.random.normal(ks[0], (BATCH, SEQ, D_MODEL), f32)
    positions = jnp.broadcast_to(jnp.arange(SEQ, dtype=jnp.int32)[None, :], (BATCH, SEQ))
    return {
        "x": x,
        "positions": positions,
        "ffn1_norm": gain(ks[1], D_MODEL),
        "ffn1_w_gate": w(ks[2], (DEPTH, D_MODEL, D_FF), D_MODEL),
        "ffn1_w_up": w(ks[3], (DEPTH, D_MODEL, D_FF), D_MODEL),
        "ffn1_w_down": w(ks[4], (DEPTH, D_FF, D_MODEL), D_FF),
        "mix_norm": gain(ks[5], D_MODEL),
        "w_in": w(ks[6], (DEPTH, D_MODEL, D_IN), D_MODEL),
        "conv_dw_w": w(ks[7], (DEPTH, CONV_WIDTH, D_CONV), CONV_WIDTH),
        "conv_dw_b": small(ks[8], (DEPTH, D_CONV)),
        "conv_ln_g": gain(ks[9], D_CONV),
        "conv_ln_b": small(ks[10], (DEPTH, D_CONV)),
        "conv_w_proj": w(ks[11], (DEPTH, D_CONV, D_MODEL), D_CONV),
        "attn_sinks": 0.5 * jax.random.normal(ks[12], (DEPTH, N_HEADS), f32),
        "attn_w_o": w(ks[13], (DEPTH, N_HEADS * HEAD_DIM, D_MODEL), N_HEADS * HEAD_DIM),
        "gate_b": small(ks[14], (DEPTH, 2 * D_MODEL)),
        "w_out": w(ks[15], (DEPTH, D_MODEL, D_MODEL), D_MODEL),
        "ffn2_norm": gain(ks[16], D_MODEL),
        "ffn2_w_gate": w(ks[17], (DEPTH, D_MODEL, D_FF), D_MODEL),
        "ffn2_w_up": w(ks[18], (DEPTH, D_MODEL, D_FF), D_MODEL),
        "ffn2_w_down": w(ks[19], (DEPTH, D_FF, D_MODEL), D_FF),
        "final_norm": 1.0 + 0.02 * jax.random.normal(ks[20], (D_MODEL,), f32),
    }


def _fwd_reference(x, positions, ffn1_norm, ffn1_w_gate, ffn1_w_up, ffn1_w_down, mix_norm, w_in,
              conv_dw_w, conv_dw_b, conv_ln_g, conv_ln_b, conv_w_proj, attn_sinks, attn_w_o,
              gate_b, w_out, ffn2_norm, ffn2_w_gate, ffn2_w_up, ffn2_w_down, final_norm):
    B, S, _ = x.shape
    bounds = np.cumsum(SPLITS)[:-1].tolist()
    for l in range(DEPTH):
        x = x + 0.5 * swiglu(rmsnorm(x, ffn1_norm[l]), ffn1_w_gate[l], ffn1_w_up[l], ffn1_w_down[l])

        h = rmsnorm(x, mix_norm[l])
        proj = h @ w_in[l]
        glu_a, glu_b, q, k, v, g_conv, g_attn = jnp.split(proj, bounds, axis=-1)

        conv_out = conformer_conv_branch(glu_a, glu_b, conv_dw_w[l], conv_dw_b[l],
                                         conv_ln_g[l], conv_ln_b[l], conv_w_proj[l])

        q = rope(q.reshape(B, S, N_HEADS, HEAD_DIM), positions)
        k = rope(k.reshape(B, S, N_KV_HEADS, HEAD_DIM), positions)
        v = v.reshape(B, S, N_KV_HEADS, HEAD_DIM)
        attn_out = sliding_window_gqa_sinks(q, k, v, attn_sinks[l]) @ attn_w_o[l]

        gb_conv, gb_attn = jnp.split(gate_b[l], 2)
        merged = (jax.nn.sigmoid(g_conv + gb_conv) * conv_out
                  + jax.nn.sigmoid(g_attn + gb_attn) * attn_out)
        x = x + merged @ w_out[l]

        x = x + 0.5 * swiglu(rmsnorm(x, ffn2_norm[l]), ffn2_w_gate[l], ffn2_w_up[l], ffn2_w_down[l])
    return rmsnorm(x, final_norm)


import jax as _jax
import jax.numpy as _jnp

TWIN_FORMAT = 'train_step'
FWD_PARAMS = ['x', 'positions', 'ffn1_norm', 'ffn1_w_gate', 'ffn1_w_up', 'ffn1_w_down', 'mix_norm', 'w_in', 'conv_dw_w', 'conv_dw_b', 'conv_ln_g', 'conv_ln_b', 'conv_w_proj', 'attn_sinks', 'attn_w_o', 'gate_b', 'w_out', 'ffn2_norm', 'ffn2_w_gate', 'ffn2_w_up', 'ffn2_w_down', 'final_norm']
TWIN_WEIGHTS = ['ffn1_norm', 'ffn1_w_gate', 'ffn1_w_up', 'ffn1_w_down', 'mix_norm', 'w_in', 'conv_dw_w', 'conv_dw_b', 'conv_ln_g', 'conv_ln_b', 'conv_w_proj', 'attn_sinks', 'attn_w_o', 'gate_b', 'w_out', 'ffn2_norm', 'ffn2_w_gate', 'ffn2_w_up', 'ffn2_w_down', 'final_norm']
TWIN_DIFF_INPUT = 'x'
TWIN_INPUTS = ['x', 'positions', 'ffn1_norm', 'ffn1_w_gate', 'ffn1_w_up', 'ffn1_w_down', 'mix_norm', 'w_in', 'conv_dw_w', 'conv_dw_b', 'conv_ln_g', 'conv_ln_b', 'conv_w_proj', 'attn_sinks', 'attn_w_o', 'gate_b', 'w_out', 'ffn2_norm', 'ffn2_w_gate', 'ffn2_w_up', 'ffn2_w_down', 'final_norm', 'loss_target', 'm_ffn1_norm', 'm_ffn1_w_gate', 'm_ffn1_w_up', 'm_ffn1_w_down', 'm_mix_norm', 'm_w_in', 'm_conv_dw_w', 'm_conv_dw_b', 'm_conv_ln_g', 'm_conv_ln_b', 'm_conv_w_proj', 'm_attn_sinks', 'm_attn_w_o', 'm_gate_b', 'm_w_out', 'm_ffn2_norm', 'm_ffn2_w_gate', 'm_ffn2_w_up', 'm_ffn2_w_down', 'm_final_norm', 'v_ffn1_norm', 'v_ffn1_w_gate', 'v_ffn1_w_up', 'v_ffn1_w_down', 'v_mix_norm', 'v_w_in', 'v_conv_dw_w', 'v_conv_dw_b', 'v_conv_ln_g', 'v_conv_ln_b', 'v_conv_w_proj', 'v_attn_sinks', 'v_attn_w_o', 'v_gate_b', 'v_w_out', 'v_ffn2_norm', 'v_ffn2_w_gate', 'v_ffn2_w_up', 'v_ffn2_w_down', 'v_final_norm']
TWIN_OUTPUTS = ['loss', 'grad_x', 'grad_ffn1_norm', 'grad_ffn1_w_gate', 'grad_ffn1_w_up', 'grad_ffn1_w_down', 'grad_mix_norm', 'grad_w_in', 'grad_conv_dw_w', 'grad_conv_dw_b', 'grad_conv_ln_g', 'grad_conv_ln_b', 'grad_conv_w_proj', 'grad_attn_sinks', 'grad_attn_w_o', 'grad_gate_b', 'grad_w_out', 'grad_ffn2_norm', 'grad_ffn2_w_gate', 'grad_ffn2_w_up', 'grad_ffn2_w_down', 'grad_final_norm', 'delta_ffn1_norm', 'delta_ffn1_w_gate', 'delta_ffn1_w_up', 'delta_ffn1_w_down', 'delta_mix_norm', 'delta_w_in', 'delta_conv_dw_w', 'delta_conv_dw_b', 'delta_conv_ln_g', 'delta_conv_ln_b', 'delta_conv_w_proj', 'delta_attn_sinks', 'delta_attn_w_o', 'delta_gate_b', 'delta_w_out', 'delta_ffn2_norm', 'delta_ffn2_w_gate', 'delta_ffn2_w_up', 'delta_ffn2_w_down', 'delta_final_norm', 'new_m_ffn1_norm', 'new_m_ffn1_w_gate', 'new_m_ffn1_w_up', 'new_m_ffn1_w_down', 'new_m_mix_norm', 'new_m_w_in', 'new_m_conv_dw_w', 'new_m_conv_dw_b', 'new_m_conv_ln_g', 'new_m_conv_ln_b', 'new_m_conv_w_proj', 'new_m_attn_sinks', 'new_m_attn_w_o', 'new_m_gate_b', 'new_m_w_out', 'new_m_ffn2_norm', 'new_m_ffn2_w_gate', 'new_m_ffn2_w_up', 'new_m_ffn2_w_down', 'new_m_final_norm', 'new_v_ffn1_norm', 'new_v_ffn1_w_gate', 'new_v_ffn1_w_up', 'new_v_ffn1_w_down', 'new_v_mix_norm', 'new_v_w_in', 'new_v_conv_dw_w', 'new_v_conv_dw_b', 'new_v_conv_ln_g', 'new_v_conv_ln_b', 'new_v_conv_w_proj', 'new_v_attn_sinks', 'new_v_attn_w_o', 'new_v_gate_b', 'new_v_w_out', 'new_v_ffn2_norm', 'new_v_ffn2_w_gate', 'new_v_ffn2_w_up', 'new_v_ffn2_w_down', 'new_v_final_norm']
TWIN_LEAF_KINDS = {'loss': 'loss', 'grad_x': 'grad_x', 'grad_ffn1_norm': 'grad_w', 'grad_ffn1_w_gate': 'grad_w', 'grad_ffn1_w_up': 'grad_w', 'grad_ffn1_w_down': 'grad_w', 'grad_mix_norm': 'grad_w', 'grad_w_in': 'grad_w', 'grad_conv_dw_w': 'grad_w', 'grad_conv_dw_b': 'grad_w', 'grad_conv_ln_g': 'grad_w', 'grad_conv_ln_b': 'grad_w', 'grad_conv_w_proj': 'grad_w', 'grad_attn_sinks': 'grad_w', 'grad_attn_w_o': 'grad_w', 'grad_gate_b': 'grad_w', 'grad_w_out': 'grad_w', 'grad_ffn2_norm': 'grad_w', 'grad_ffn2_w_gate': 'grad_w', 'grad_ffn2_w_up': 'grad_w', 'grad_ffn2_w_down': 'grad_w', 'grad_final_norm': 'grad_w', 'delta_ffn1_norm': 'delta_w', 'delta_ffn1_w_gate': 'delta_w', 'delta_ffn1_w_up': 'delta_w', 'delta_ffn1_w_down': 'delta_w', 'delta_mix_norm': 'delta_w', 'delta_w_in': 'delta_w', 'delta_conv_dw_w': 'delta_w', 'delta_conv_dw_b': 'delta_w', 'delta_conv_ln_g': 'delta_w', 'delta_conv_ln_b': 'delta_w', 'delta_conv_w_proj': 'delta_w', 'delta_attn_sinks': 'delta_w', 'delta_attn_w_o': 'delta_w', 'delta_gate_b': 'delta_w', 'delta_w_out': 'delta_w', 'delta_ffn2_norm': 'delta_w', 'delta_ffn2_w_gate': 'delta_w', 'delta_ffn2_w_up': 'delta_w', 'delta_ffn2_w_down': 'delta_w', 'delta_final_norm': 'delta_w', 'new_m_ffn1_norm': 'new_m', 'new_m_ffn1_w_gate': 'new_m', 'new_m_ffn1_w_up': 'new_m', 'new_m_ffn1_w_down': 'new_m', 'new_m_mix_norm': 'new_m', 'new_m_w_in': 'new_m', 'new_m_conv_dw_w': 'new_m', 'new_m_conv_dw_b': 'new_m', 'new_m_conv_ln_g': 'new_m', 'new_m_conv_ln_b': 'new_m', 'new_m_conv_w_proj': 'new_m', 'new_m_attn_sinks': 'new_m', 'new_m_attn_w_o': 'new_m', 'new_m_gate_b': 'new_m', 'new_m_w_out': 'new_m', 'new_m_ffn2_norm': 'new_m', 'new_m_ffn2_w_gate': 'new_m', 'new_m_ffn2_w_up': 'new_m', 'new_m_ffn2_w_down': 'new_m', 'new_m_final_norm': 'new_m', 'new_v_ffn1_norm': 'new_v', 'new_v_ffn1_w_gate': 'new_v', 'new_v_ffn1_w_up': 'new_v', 'new_v_ffn1_w_down': 'new_v', 'new_v_mix_norm': 'new_v', 'new_v_w_in': 'new_v', 'new_v_conv_dw_w': 'new_v', 'new_v_conv_dw_b': 'new_v', 'new_v_conv_ln_g': 'new_v', 'new_v_conv_ln_b': 'new_v', 'new_v_conv_w_proj': 'new_v', 'new_v_attn_sinks': 'new_v', 'new_v_attn_w_o': 'new_v', 'new_v_gate_b': 'new_v', 'new_v_w_out': 'new_v', 'new_v_ffn2_norm': 'new_v', 'new_v_ffn2_w_gate': 'new_v', 'new_v_ffn2_w_up': 'new_v', 'new_v_ffn2_w_down': 'new_v', 'new_v_final_norm': 'new_v'}


def _forward(args):
    return _fwd_reference(*[args[k] for k in FWD_PARAMS])


def _output_shape():
    def fwd():
        inp = _fwd_setup_inputs(0)
        return _fwd_reference(*[inp[k] for k in FWD_PARAMS])
    out = _jax.eval_shape(fwd)
    return out.shape, out.dtype

N_MICROBATCH = 1
ADAM_LR = 0.001
ADAM_B1 = 0.9
ADAM_B2 = 0.999
ADAM_EPS = 1e-08
ADAM_WD = 0.01
ADAM_STEP = 10
PER_EXAMPLE_BATCH_AXIS = {'x': 0, 'positions': 0, 'loss_target': 0}
SHARED_INPUTS = []
_WEIGHT_DTYPES = {'ffn1_norm': _jnp.float32, 'ffn1_w_gate': _jnp.float32, 'ffn1_w_up': _jnp.float32, 'ffn1_w_down': _jnp.float32, 'mix_norm': _jnp.float32, 'w_in': _jnp.float32, 'conv_dw_w': _jnp.float32, 'conv_dw_b': _jnp.float32, 'conv_ln_g': _jnp.float32, 'conv_ln_b': _jnp.float32, 'conv_w_proj': _jnp.float32, 'attn_sinks': _jnp.float32, 'attn_w_o': _jnp.float32, 'gate_b': _jnp.float32, 'w_out': _jnp.float32, 'ffn2_norm': _jnp.float32, 'ffn2_w_gate': _jnp.float32, 'ffn2_w_up': _jnp.float32, 'ffn2_w_down': _jnp.float32, 'final_norm': _jnp.float32}
MOMENT_SCALE = {'ffn1_norm': 1.145225e-01, 'ffn1_w_gate': 4.736373e-02, 'ffn1_w_up': 4.586480e-02, 'ffn1_w_down': 7.598116e-02, 'mix_norm': 1.018180e-01, 'w_in': 4.268219e-02, 'conv_dw_w': 8.010716e-02, 'conv_dw_b': 1.630255e-01, 'conv_ln_g': 9.441915e-02, 'conv_ln_b': 8.360345e-02, 'conv_w_proj': 7.855339e-02, 'attn_sinks': 2.049088e-02, 'attn_w_o': 2.631029e-02, 'gate_b': 2.400084e-02, 'w_out': 8.129508e-02, 'ffn2_norm': 9.341081e-02, 'ffn2_w_gate': 4.123535e-02, 'ffn2_w_up': 3.995016e-02, 'ffn2_w_down': 6.628317e-02, 'final_norm': 6.405446e+01}


def _to_microbatches(a, axis):
    t = _jnp.moveaxis(a, axis, 0)
    t = t.reshape((N_MICROBATCH, t.shape[0] // N_MICROBATCH) + t.shape[1:])
    return _jnp.moveaxis(t, 1, axis + 1)


def setup_inputs(seed: int = 0) -> dict:
    inp = _fwd_setup_inputs(seed)
    key = _jax.random.fold_in(_jax.random.key(seed), 7919)
    shape, _ = _output_shape()
    out = dict(inp)
    out["loss_target"] = _jax.random.normal(_jax.random.fold_in(key, 0), shape, _jnp.float32)
    for i, name in enumerate(TWIN_WEIGHTS):
        w = inp[name].astype(_jnp.float32)
        if MOMENT_SCALE is None:
            s = _jnp.sqrt(_jnp.mean(_jnp.square(w)) + 1e-30)
        else:
            s = MOMENT_SCALE[name]
        km, kv = _jax.random.split(_jax.random.fold_in(key, i + 1))
        out[name] = w
        out["m_" + name] = s * _jax.random.normal(km, w.shape, _jnp.float32)
        out["v_" + name] = (s * s) * _jax.random.uniform(kv, w.shape, _jnp.float32, 0.5, 1.5)
    if N_MICROBATCH > 1:
        for name, axis in PER_EXAMPLE_BATCH_AXIS.items():
            out[name] = _to_microbatches(out[name], axis)
    return {'x': out['x'], 'positions': out['positions'], 'ffn1_norm': out['ffn1_norm'], 'ffn1_w_gate': out['ffn1_w_gate'], 'ffn1_w_up': out['ffn1_w_up'], 'ffn1_w_down': out['ffn1_w_down'], 'mix_norm': out['mix_norm'], 'w_in': out['w_in'], 'conv_dw_w': out['conv_dw_w'], 'conv_dw_b': out['conv_dw_b'], 'conv_ln_g': out['conv_ln_g'], 'conv_ln_b': out['conv_ln_b'], 'conv_w_proj': out['conv_w_proj'], 'attn_sinks': out['attn_sinks'], 'attn_w_o': out['attn_w_o'], 'gate_b': out['gate_b'], 'w_out': out['w_out'], 'ffn2_norm': out['ffn2_norm'], 'ffn2_w_gate': out['ffn2_w_gate'], 'ffn2_w_up': out['ffn2_w_up'], 'ffn2_w_down': out['ffn2_w_down'], 'final_norm': out['final_norm'], 'loss_target': out['loss_target'], 'm_ffn1_norm': out['m_ffn1_norm'], 'm_ffn1_w_gate': out['m_ffn1_w_gate'], 'm_ffn1_w_up': out['m_ffn1_w_up'], 'm_ffn1_w_down': out['m_ffn1_w_down'], 'm_mix_norm': out['m_mix_norm'], 'm_w_in': out['m_w_in'], 'm_conv_dw_w': out['m_conv_dw_w'], 'm_conv_dw_b': out['m_conv_dw_b'], 'm_conv_ln_g': out['m_conv_ln_g'], 'm_conv_ln_b': out['m_conv_ln_b'], 'm_conv_w_proj': out['m_conv_w_proj'], 'm_attn_sinks': out['m_attn_sinks'], 'm_attn_w_o': out['m_attn_w_o'], 'm_gate_b': out['m_gate_b'], 'm_w_out': out['m_w_out'], 'm_ffn2_norm': out['m_ffn2_norm'], 'm_ffn2_w_gate': out['m_ffn2_w_gate'], 'm_ffn2_w_up': out['m_ffn2_w_up'], 'm_ffn2_w_down': out['m_ffn2_w_down'], 'm_final_norm': out['m_final_norm'], 'v_ffn1_norm': out['v_ffn1_norm'], 'v_ffn1_w_gate': out['v_ffn1_w_gate'], 'v_ffn1_w_up': out['v_ffn1_w_up'], 'v_ffn1_w_down': out['v_ffn1_w_down'], 'v_mix_norm': out['v_mix_norm'], 'v_w_in': out['v_w_in'], 'v_conv_dw_w': out['v_conv_dw_w'], 'v_conv_dw_b': out['v_conv_dw_b'], 'v_conv_ln_g': out['v_conv_ln_g'], 'v_conv_ln_b': out['v_conv_ln_b'], 'v_conv_w_proj': out['v_conv_w_proj'], 'v_attn_sinks': out['v_attn_sinks'], 'v_attn_w_o': out['v_attn_w_o'], 'v_gate_b': out['v_gate_b'], 'v_w_out': out['v_w_out'], 'v_ffn2_norm': out['v_ffn2_norm'], 'v_ffn2_w_gate': out['v_ffn2_w_gate'], 'v_ffn2_w_up': out['v_ffn2_w_up'], 'v_ffn2_w_down': out['v_ffn2_w_down'], 'v_final_norm': out['v_final_norm']}


def _loss(weights, diff, rest, loss_target):
    with _jax.named_scope("forward"):
        args = {**rest, TWIN_DIFF_INPUT: diff, **{k: w.astype(_WEIGHT_DTYPES[k]) for k, w in weights.items()}}
        y = _forward(args)
    with _jax.named_scope("loss_head"):
        err = _jnp.square(y.astype(_jnp.float32) - loss_target)
        return 0.5 * _jnp.sum(_jnp.mean(err, axis=-1)) if err.ndim else 0.5 * err


def _adamw(w, g, m, v):
    m = ADAM_B1 * m + (1.0 - ADAM_B1) * g
    v = ADAM_B2 * v + (1.0 - ADAM_B2) * _jnp.square(g)
    m_hat = m / (1.0 - ADAM_B1 ** ADAM_STEP)
    v_hat = v / (1.0 - ADAM_B2 ** ADAM_STEP)
    delta = -ADAM_LR * (m_hat / (_jnp.sqrt(v_hat) + ADAM_EPS) + ADAM_WD * w)
    return delta, m, v


def reference(x, positions, ffn1_norm, ffn1_w_gate, ffn1_w_up, ffn1_w_down, mix_norm, w_in, conv_dw_w, conv_dw_b, conv_ln_g, conv_ln_b, conv_w_proj, attn_sinks, attn_w_o, gate_b, w_out, ffn2_norm, ffn2_w_gate, ffn2_w_up, ffn2_w_down, final_norm, loss_target, m_ffn1_norm, m_ffn1_w_gate, m_ffn1_w_up, m_ffn1_w_down, m_mix_norm, m_w_in, m_conv_dw_w, m_conv_dw_b, m_conv_ln_g, m_conv_ln_b, m_conv_w_proj, m_attn_sinks, m_attn_w_o, m_gate_b, m_w_out, m_ffn2_norm, m_ffn2_w_gate, m_ffn2_w_up, m_ffn2_w_down, m_final_norm, v_ffn1_norm, v_ffn1_w_gate, v_ffn1_w_up, v_ffn1_w_down, v_mix_norm, v_w_in, v_conv_dw_w, v_conv_dw_b, v_conv_ln_g, v_conv_ln_b, v_conv_w_proj, v_attn_sinks, v_attn_w_o, v_gate_b, v_w_out, v_ffn2_norm, v_ffn2_w_gate, v_ffn2_w_up, v_ffn2_w_down, v_final_norm):
    given = dict(x=x, positions=positions, ffn1_norm=ffn1_norm, ffn1_w_gate=ffn1_w_gate, ffn1_w_up=ffn1_w_up, ffn1_w_down=ffn1_w_down, mix_norm=mix_norm, w_in=w_in, conv_dw_w=conv_dw_w, conv_dw_b=conv_dw_b, conv_ln_g=conv_ln_g, conv_ln_b=conv_ln_b, conv_w_proj=conv_w_proj, attn_sinks=attn_sinks, attn_w_o=attn_w_o, gate_b=gate_b, w_out=w_out, ffn2_norm=ffn2_norm, ffn2_w_gate=ffn2_w_gate, ffn2_w_up=ffn2_w_up, ffn2_w_down=ffn2_w_down, final_norm=final_norm, loss_target=loss_target, m_ffn1_norm=m_ffn1_norm, m_ffn1_w_gate=m_ffn1_w_gate, m_ffn1_w_up=m_ffn1_w_up, m_ffn1_w_down=m_ffn1_w_down, m_mix_norm=m_mix_norm, m_w_in=m_w_in, m_conv_dw_w=m_conv_dw_w, m_conv_dw_b=m_conv_dw_b, m_conv_ln_g=m_conv_ln_g, m_conv_ln_b=m_conv_ln_b, m_conv_w_proj=m_conv_w_proj, m_attn_sinks=m_attn_sinks, m_attn_w_o=m_attn_w_o, m_gate_b=m_gate_b, m_w_out=m_w_out, m_ffn2_norm=m_ffn2_norm, m_ffn2_w_gate=m_ffn2_w_gate, m_ffn2_w_up=m_ffn2_w_up, m_ffn2_w_down=m_ffn2_w_down, m_final_norm=m_final_norm, v_ffn1_norm=v_ffn1_norm, v_ffn1_w_gate=v_ffn1_w_gate, v_ffn1_w_up=v_ffn1_w_up, v_ffn1_w_down=v_ffn1_w_down, v_mix_norm=v_mix_norm, v_w_in=v_w_in, v_conv_dw_w=v_conv_dw_w, v_conv_dw_b=v_conv_dw_b, v_conv_ln_g=v_conv_ln_g, v_conv_ln_b=v_conv_ln_b, v_conv_w_proj=v_conv_w_proj, v_attn_sinks=v_attn_sinks, v_attn_w_o=v_attn_w_o, v_gate_b=v_gate_b, v_w_out=v_w_out, v_ffn2_norm=v_ffn2_norm, v_ffn2_w_gate=v_ffn2_w_gate, v_ffn2_w_up=v_ffn2_w_up, v_ffn2_w_down=v_ffn2_w_down, v_final_norm=v_final_norm)
    weights = {n: given[n] for n in TWIN_WEIGHTS}
    shared = {n: given[n] for n in SHARED_INPUTS}
    per_example = {n: given[n] for n in ['x', 'positions']}
    grad_fn = _jax.value_and_grad(_loss, argnums=(0, 1))

    def one_microbatch(ex, loss_target):
        ex = dict(ex)
        diff = ex.pop(TWIN_DIFF_INPUT)
        return grad_fn(weights, diff, {**shared, **ex}, loss_target)

    if N_MICROBATCH == 1:
        loss, (grad_w, grad_x) = one_microbatch(per_example, given["loss_target"])
    else:
        def body(carry, xs):
            loss_sum, grad_sum = carry
            l_k, (gw_k, gx_k) = one_microbatch(xs[0], xs[1])
            with _jax.named_scope("update"):
                return (loss_sum + l_k, _jax.tree.map(_jnp.add, grad_sum, gw_k)), gx_k

        init = (_jnp.zeros((), _jnp.float32), _jax.tree.map(_jnp.zeros_like, weights))
        (loss, grad_w), grad_x = _jax.lax.scan(body, init, (per_example, given["loss_target"]))
    with _jax.named_scope("update"):
        delta_w, new_m, new_v = {}, {}, {}
        for n in TWIN_WEIGHTS:
            delta_w[n], new_m[n], new_v[n] = _adamw(weights[n], grad_w[n], given["m_" + n], given["v_" + n])
    return (loss, grad_x, *[grad_w[n] for n in TWIN_WEIGHTS], *[delta_w[n] for n in TWIN_WEIGHTS],
            *[new_m[n] for n in TWIN_WEIGHTS], *[new_v[n] for n in TWIN_WEIGHTS])
```

```python
import functools

import jax
import jax.numpy as jnp
from jax import lax
from jax.experimental import pallas as pl
from jax.experimental.pallas import tpu as pltpu

F32, BF16 = jnp.float32, jnp.bfloat16

HEAD_DIM = 64
N_KV_HEADS = 4
WINDOW = 128
CONV_WIDTH = 31
ROPE_THETA = 10000.0
EPS = 1e-6
LN_EPS = 1e-5
NEG_INF = -1e30
ADAM_LR, ADAM_B1, ADAM_B2, ADAM_EPS, ADAM_WD, ADAM_STEP = 0.001, 0.9, 0.999, 1e-08, 0.01, 10

N_DEV = 8
LANES = 128
CONV_HALO = 32
ROW_TILE = 512
FFN_CHUNK = 256
VMEM_LIMIT = 56 * 2 ** 20
MESH = pl.DeviceIdType.MESH


def _params(*sem):
    return pltpu.CompilerParams(dimension_semantics=sem or None, vmem_limit_bytes=VMEM_LIMIT)


def _resident(shape):
    zeros = (0,) * len(shape)
    return pl.BlockSpec(shape, lambda *_: zeros, pipeline_mode=pl.Buffered(1))


def _rows(tm, n):
    return pl.BlockSpec((tm, n), lambda i: (i, 0))


def _acc_spec(shape):
    zeros = (0,) * len(shape)
    return pl.BlockSpec(shape, lambda *_: zeros)


def _nt(a, b):
    return lax.dot_general(a, b, (((1,), (1,)), ((), ())), preferred_element_type=F32)


def _tn(a, b):
    return lax.dot_general(a, b, (((0,), (0,)), ((), ())), preferred_element_type=F32)


def _dot(a, b):
    return jnp.dot(a, b, preferred_element_type=F32)


def _sigmoid(x):
    return 1.0 / (1.0 + jnp.exp(-x))


def _rms_fwd(x, g):
    r = lax.rsqrt(jnp.mean(x * x, axis=-1, keepdims=True) + EPS)
    return (x * r) * g


def _rms_bwd(x, g, dy):
    r = lax.rsqrt(jnp.mean(x * x, axis=-1, keepdims=True) + EPS)
    xhat = x * r
    dyg = dy * g
    dx = r * (dyg - xhat * jnp.mean(dyg * xhat, axis=-1, keepdims=True))
    return dx, dy * xhat


def _rot_half(x):
    lane = lax.broadcasted_iota(jnp.int32, (x.shape[0], LANES), 1)
    first = (lane % HEAD_DIM) < (HEAD_DIM // 2)
    out = []
    for s in range(x.shape[1] // LANES):
        xs = x[:, LANES * s:LANES * (s + 1)]
        out.append(jnp.where(first, pltpu.roll(xs, LANES - HEAD_DIM // 2, 1), pltpu.roll(xs, HEAD_DIM // 2, 1)))
    return out[0] if len(out) == 1 else jnp.concatenate(out, axis=1)


def _tile_lanes(t, width):
    return t if width == LANES else jnp.concatenate([t] * (width // LANES), axis=1)


def _rope_fwd(x, cos, sin_signed):
    w = x.shape[1]
    return x * _tile_lanes(cos, w) + _rot_half(x) * _tile_lanes(sin_signed, w)


def _rope_bwd(dy, cos, sin_signed):
    w = dy.shape[1]
    return dy * _tile_lanes(cos, w) + _rot_half(dy * _tile_lanes(sin_signed, w))


def ffn_fwd(x, gain, wg, wu, wd, name):
    T, D = x.shape
    F = wg.shape[1]
    tm = min(ROW_TILE, T)
    fc = FFN_CHUNK

    def body(x_ref, g_ref, wg_ref, wu_ref, wd_ref, xo_ref, h_ref, a_ref, b_ref, acc_ref):
        x = x_ref[...]
        h = _rms_fwd(x, g_ref[...]).astype(BF16)
        h_ref[...] = h
        for c in range(F // fc):
            cs = pl.ds(c * fc, fc)
            a = _dot(h, wg_ref[:, cs])
            b = _dot(h, wu_ref[:, cs])
            a_ref[:, cs] = a.astype(BF16)
            b_ref[:, cs] = b.astype(BF16)
            s = (a * _sigmoid(a) * b).astype(BF16)
            y = _dot(s, wd_ref[cs, :])
            if c == 0:
                acc_ref[...] = y
            else:
                acc_ref[...] += y
        xo_ref[...] = x + 0.5 * acc_ref[...]

    return pl.pallas_call(
        body, name=name, grid=(T // tm,),
        out_shape=(jax.ShapeDtypeStruct((T, D), F32), jax.ShapeDtypeStruct((T, D), BF16),
                   jax.ShapeDtypeStruct((T, F), BF16), jax.ShapeDtypeStruct((T, F), BF16)),
        in_specs=[_rows(tm, D), _resident((1, D)), _resident((D, F)), _resident((D, F)), _resident((F, D))],
        out_specs=(_rows(tm, D), _rows(tm, D), _rows(tm, F), _rows(tm, F)),
        scratch_shapes=[pltpu.VMEM((tm, D), F32)],
        compiler_params=_params("arbitrary"),
    )(x, gain, wg, wu, wd)


def ffn_bwd(dxo, x, gain, a, b, wg, wu, wd, name):
    T, D = x.shape
    F = wg.shape[1]
    tm = min(ROW_TILE // 2, T)
    fc = FFN_CHUNK

    def body(dxo_ref, x_ref, g_ref, a_ref, b_ref, wg_ref, wu_ref, wd_ref,
             dx_ref, da_ref, db_ref, s_ref, g0_ref, dg_ref, acc_ref):
        dxo = dxo_ref[...]
        g0 = (0.5 * dxo).astype(BF16)
        g0_ref[...] = g0
        for c in range(F // fc):
            cs = pl.ds(c * fc, fc)
            ds = _nt(g0, wd_ref[cs, :])
            a = a_ref[:, cs].astype(F32)
            bb = b_ref[:, cs].astype(F32)
            sa = _sigmoid(a)
            silu = a * sa
            da = (ds * bb * (sa * (1.0 + a * (1.0 - sa)))).astype(BF16)
            db = (ds * silu).astype(BF16)
            da_ref[:, cs] = da
            db_ref[:, cs] = db
            s_ref[:, cs] = (silu * bb).astype(BF16)
            dh = _nt(da, wg_ref[:, cs]) + _nt(db, wu_ref[:, cs])
            if c == 0:
                acc_ref[...] = dh
            else:
                acc_ref[...] += dh
        dx, dgt = _rms_bwd(x_ref[...], g_ref[...], acc_ref[...])
        dx_ref[...] = dxo + dx

        @pl.when(pl.program_id(0) == 0)
        def _():
            dg_ref[...] = jnp.zeros_like(dg_ref)
        dg_ref[...] += jnp.sum(dgt, axis=0, keepdims=True)

    return pl.pallas_call(
        body, name=name, grid=(T // tm,),
        out_shape=(jax.ShapeDtypeStruct((T, D), F32), jax.ShapeDtypeStruct((T, F), BF16),
                   jax.ShapeDtypeStruct((T, F), BF16), jax.ShapeDtypeStruct((T, F), BF16),
                   jax.ShapeDtypeStruct((T, D), BF16), jax.ShapeDtypeStruct((1, D), F32)),
        in_specs=[_rows(tm, D), _rows(tm, D), _resident((1, D)), _rows(tm, F), _rows(tm, F),
                  _resident((D, F)), _resident((D, F)), _resident((F, D))],
        out_specs=(_rows(tm, D), _rows(tm, F), _rows(tm, F), _rows(tm, F), _rows(tm, D), _acc_spec((1, D))),
        scratch_shapes=[pltpu.VMEM((tm, D), F32)],
        compiler_params=_params("arbitrary"),
    )(dxo, x, gain, a, b, wg, wu, wd)


def wgrad(a, b, name, tn=None, tk=None):
    T, M = a.shape
    N = b.shape[1]
    tn = tn or N
    tk = tk or min(ROW_TILE, T)
    nk = T // tk

    def body(a_ref, b_ref, o_ref):
        @pl.when(pl.program_id(1) == 0)
        def _():
            o_ref[...] = jnp.zeros_like(o_ref)
        o_ref[...] += _tn(a_ref[...].astype(BF16), b_ref[...].astype(BF16))

    return pl.pallas_call(
        body, name=name, grid=(N // tn, nk),
        out_shape=jax.ShapeDtypeStruct((M, N), F32),
        in_specs=[pl.BlockSpec((tk, M), lambda j, k: (k, 0)), pl.BlockSpec((tk, tn), lambda j, k: (k, j))],
        out_specs=pl.BlockSpec((M, tn), lambda j, k: (0, j)),
        compiler_params=_params("parallel", "arbitrary"),
    )(a, b)


def mix_in_fwd(x, gain, win, gate_b, cos, sin_signed, name):
    T, D = x.shape
    KV = N_KV_HEADS * HEAD_DIM
    tm = min(ROW_TILE, T)
    o_ga, o_gb, o_q, o_gc, o_gt, o_k, o_v = 0, D, 2 * D, 3 * D, 4 * D, 5 * D, 5 * D + KV

    def body(x_ref, g_ref, w_ref, gb_ref, cos_ref, sin_ref,
             h_ref, ga_ref, gb_out_ref, u0_ref, q_ref, sgc_ref, sgt_ref, k_ref, v_ref):
        h = _rms_fwd(x_ref[...], g_ref[...]).astype(BF16)
        h_ref[...] = h
        cos, sin = cos_ref[...], sin_ref[...]
        ga = _dot(h, w_ref[:, pl.ds(o_ga, D)])
        gb = _dot(h, w_ref[:, pl.ds(o_gb, D)])
        ga_ref[...] = ga.astype(BF16)
        gb_out_ref[...] = gb.astype(BF16)
        u0_ref[...] = (ga * _sigmoid(gb)).astype(BF16)
        q = _dot(h, w_ref[:, pl.ds(o_q, D)])
        q_ref[...] = _rope_fwd(q, cos, sin).astype(BF16)
        gc = _dot(h, w_ref[:, pl.ds(o_gc, D)]) + gb_ref[:, pl.ds(0, D)]
        sgc_ref[...] = _sigmoid(gc).astype(BF16)
        gt = _dot(h, w_ref[:, pl.ds(o_gt, D)]) + gb_ref[:, pl.ds(D, D)]
        sgt_ref[...] = _sigmoid(gt).astype(BF16)
        k = _dot(h, w_ref[:, pl.ds(o_k, KV)])
        k_ref[...] = _rope_fwd(k, cos, sin).astype(BF16)
        v_ref[...] = _dot(h, w_ref[:, pl.ds(o_v, KV)]).astype(BF16)

    big = jax.ShapeDtypeStruct((T, D), BF16)
    small = jax.ShapeDtypeStruct((T, KV), BF16)
    return pl.pallas_call(
        body, name=name, grid=(T // tm,),
        out_shape=(big, big, big, big, big, big, big, small, small),
        in_specs=[_rows(tm, D), _resident((1, D)), _resident(win.shape), _resident((1, 2 * D)),
                  _rows(tm, LANES), _rows(tm, LANES)],
        out_specs=(_rows(tm, D),) * 7 + (_rows(tm, KV),) * 2,
        compiler_params=_params("arbitrary"),
    )(x, gain, win, gate_b, cos, sin_signed)


def _attn_masks(tile_is_first, rows):
    qi = lax.broadcasted_iota(jnp.int32, (rows, 2 * WINDOW), 0) % WINDOW
    c = lax.broadcasted_iota(jnp.int32, (rows, 2 * WINDOW), 1)
    base = (c > qi) & (c <= qi + WINDOW)
    first_key = jnp.where(tile_is_first, WINDOW, 0)
    return base, base & (c >= first_key)


def _attn_block(q_ref, k_ref, kh_ref, v_ref, vh_ref, j, g, group):
    rows = pl.ds(j * WINDOW, WINDOW)
    qg = jnp.concatenate([q_ref[rows, pl.ds(HEAD_DIM * (group * g + hh), HEAD_DIM)] for hh in range(group)], axis=0)
    kv_lanes = pl.ds(HEAD_DIM * g, HEAD_DIM)
    if j == 0:
        kprev, vprev = kh_ref[:, kv_lanes], vh_ref[:, kv_lanes]
    else:
        prev = pl.ds((j - 1) * WINDOW, WINDOW)
        kprev, vprev = k_ref[prev, kv_lanes], v_ref[prev, kv_lanes]
    kg = jnp.concatenate([kprev, k_ref[rows, kv_lanes]], axis=0)
    vg = jnp.concatenate([vprev, v_ref[rows, kv_lanes]], axis=0)
    return qg, kg, vg


def _attn_probs(qg, kg, sink, mask):
    s = _nt(qg, kg) * (HEAD_DIM ** -0.5)
    s = jnp.where(mask, s, NEG_INF)
    m = jnp.maximum(jnp.max(s, axis=-1, keepdims=True), sink)
    p = jnp.exp(s - m)
    e_sink = jnp.exp(sink - m)
    inv = 1.0 / (jnp.sum(p, axis=-1, keepdims=True) + e_sink)
    return p * inv, e_sink * inv


def _halo_rows_spec(tq, width, sub):
    return pl.BlockSpec((sub, width), lambda i: (jnp.maximum(i * (tq // sub) - 1, 0), 0))


def attn_fwd(q, k, v, sink_col, name):
    T, D = q.shape
    KV = k.shape[1]
    group = D // KV
    tq = min(ROW_TILE, T)
    nsub = tq // WINDOW

    def body(q_ref, k_ref, kh_ref, v_ref, vh_ref, sink_ref, o_ref):
        base, first = _attn_masks(pl.program_id(0) == 0, group * WINDOW)
        for j in range(nsub):
            mask = first if j == 0 else base
            for g in range(N_KV_HEADS):
                qg, kg, vg = _attn_block(q_ref, k_ref, kh_ref, v_ref, vh_ref, j, g, group)
                probs, _ = _attn_probs(qg, kg, sink_ref[g], mask)
                og = _dot(probs.astype(BF16), vg)
                for hh in range(group):
                    o_ref[pl.ds(j * WINDOW, WINDOW), pl.ds(HEAD_DIM * (group * g + hh), HEAD_DIM)] = (
                        og[hh * WINDOW:(hh + 1) * WINDOW].astype(BF16))

    return pl.pallas_call(
        body, name=name, grid=(T // tq,),
        out_shape=jax.ShapeDtypeStruct((T, D), BF16),
        in_specs=[_rows(tq, D), _rows(tq, KV), _halo_rows_spec(tq, KV, WINDOW), _rows(tq, KV),
                  _halo_rows_spec(tq, KV, WINDOW), _resident(sink_col.shape)],
        out_specs=_rows(tq, D),
        compiler_params=_params("arbitrary"),
    )(q, k, k, v, v, sink_col)


def attn_bwd(q, k, v, do, sink_col, cos, sin_signed, name):
    T, D = q.shape
    KV = k.shape[1]
    group = D // KV
    tq = min(ROW_TILE, T)
    nsub = tq // WINDOW
    nt = T // tq
    scale = HEAD_DIM ** -0.5

    def rev(i):
        return nt - 1 - i

    def body(q_ref, k_ref, kh_ref, v_ref, vh_ref, do_ref, sink_ref, cos_ref, sin_ref,
             dq_ref, dk_ref, dv_ref, dsink_ref, dq_acc, dk_acc, dv_acc, carry_k, carry_v):
        i = pl.program_id(0)

        @pl.when(i == 0)
        def _():
            carry_k[...] = jnp.zeros_like(carry_k)
            carry_v[...] = jnp.zeros_like(carry_v)
            dsink_ref[...] = jnp.zeros_like(dsink_ref)

        dk_acc[...] = jnp.zeros_like(dk_acc)
        dv_acc[...] = jnp.zeros_like(dv_acc)
        base, first = _attn_masks(rev(i) == 0, group * WINDOW)
        for j in range(nsub):
            mask = first if j == 0 else base
            rows = pl.ds(j * WINDOW, WINDOW)
            for g in range(N_KV_HEADS):
                qg, kg, vg = _attn_block(q_ref, k_ref, kh_ref, v_ref, vh_ref, j, g, group)
                dog = jnp.concatenate(
                    [do_ref[rows, pl.ds(HEAD_DIM * (group * g + hh), HEAD_DIM)] for hh in range(group)], axis=0)
                probs, p_sink = _attn_probs(qg, kg, sink_ref[g], mask)
                dp = _nt(dog, vg)
                delta = jnp.sum(probs * dp, axis=-1, keepdims=True)
                ds = (probs * (dp - delta) * scale).astype(BF16)
                dqg = _dot(ds, kg)
                for hh in range(group):
                    dq_acc[rows, pl.ds(HEAD_DIM * (group * g + hh), HEAD_DIM)] = dqg[hh * WINDOW:(hh + 1) * WINDOW]
                keys = pl.ds(j * WINDOW, 2 * WINDOW)
                kv_lanes = pl.ds(HEAD_DIM * g, HEAD_DIM)
                dk_acc[keys, kv_lanes] += _tn(ds, qg)
                dv_acc[keys, kv_lanes] += _tn(probs.astype(BF16), dog)
                dsk = -(p_sink * delta)
                for hh in range(group):
                    head = group * g + hh
                    tot = jnp.sum(dsk[hh * WINDOW:(hh + 1) * WINDOW], axis=0, keepdims=True)
                    dsink_ref[pl.ds(head, 1), :] += jnp.broadcast_to(tot, (1, LANES))

        tail = pl.ds(tq, WINDOW)
        dk_acc[tail, :] += carry_k[...]
        dv_acc[tail, :] += carry_v[...]
        carry_k[...] = dk_acc[pl.ds(0, WINDOW), :]
        carry_v[...] = dv_acc[pl.ds(0, WINDOW), :]
        cos, sin = cos_ref[...], sin_ref[...]
        dq_ref[...] = _rope_bwd(dq_acc[...], cos, sin).astype(BF16)
        dk_ref[...] = _rope_bwd(dk_acc[pl.ds(WINDOW, tq), :], cos, sin).astype(BF16)
        dv_ref[...] = dv_acc[pl.ds(WINDOW, tq), :].astype(BF16)

    def rrows(n):
        return pl.BlockSpec((tq, n), lambda i: (rev(i), 0))

    def rhalo(n):
        return pl.BlockSpec((WINDOW, n), lambda i: (jnp.maximum(rev(i) * nsub - 1, 0), 0))

    return pl.pallas_call(
        body, name=name, grid=(nt,),
        out_shape=(jax.ShapeDtypeStruct((T, D), BF16), jax.ShapeDtypeStruct((T, KV), BF16),
                   jax.ShapeDtypeStruct((T, KV), BF16), jax.ShapeDtypeStruct((D // HEAD_DIM, LANES), F32)),
        in_specs=[rrows(D), rrows(KV), rhalo(KV), rrows(KV), rhalo(KV), rrows(D), _resident(sink_col.shape),
                  rrows(LANES), rrows(LANES)],
        out_specs=(rrows(D), rrows(KV), rrows(KV), _acc_spec((D // HEAD_DIM, LANES))),
        scratch_shapes=[pltpu.VMEM((tq, D), F32), pltpu.VMEM((WINDOW + tq, KV), F32), pltpu.VMEM((WINDOW + tq, KV), F32),
                        pltpu.VMEM((WINDOW, KV), F32), pltpu.VMEM((WINDOW, KV), F32)],
        compiler_params=_params("arbitrary"),
    )(q, k, k, v, v, do, sink_col, cos, sin_signed)


def _ln_stats(u):
    mu = jnp.mean(u, axis=-1, keepdims=True)
    d = u - mu
    rstd = lax.rsqrt(jnp.mean(d * d, axis=-1, keepdims=True) + LN_EPS)
    return d * rstd, rstd


def mix_out_fwd(x, u0, o, sgc, sgt, dw_w, dw_b, ln_g, ln_b, wcp, wo, wout, name):
    T, D = x.shape
    tm = min(ROW_TILE, T)
    K = dw_w.shape[0]

    def body(x_ref, u_ref, uh_ref, o_ref, sgc_ref, sgt_ref, w_ref, b_ref, lg_ref, lb_ref, wcp_ref, wo_ref, wout_ref,
             x2_ref, u1_ref, co_ref, ao_ref, mg_ref, buf):
        keep = (pl.program_id(0) > 0).astype(F32)
        buf[pl.ds(0, CONV_HALO), :] = uh_ref[...].astype(F32) * keep
        buf[pl.ds(CONV_HALO, tm), :] = u_ref[...].astype(F32)
        acc = jnp.broadcast_to(b_ref[...], (tm, D))
        for k in range(K):
            acc = acc + buf[pl.ds(CONV_HALO - (K - 1) + k, tm), :] * w_ref[pl.ds(k, 1), :]
        u1_ref[...] = acc.astype(BF16)
        xhat, _ = _ln_stats(acc)
        u2 = xhat * lg_ref[...] + lb_ref[...]
        u3 = (u2 * _sigmoid(u2)).astype(BF16)
        co = _dot(u3, wcp_ref[...])
        ao = _dot(o_ref[...], wo_ref[...])
        co_ref[...] = co.astype(BF16)
        ao_ref[...] = ao.astype(BF16)
        merged = (sgc_ref[...].astype(F32) * co + sgt_ref[...].astype(F32) * ao).astype(BF16)
        mg_ref[...] = merged
        x2_ref[...] = x_ref[...] + _dot(merged, wout_ref[...])

    big = jax.ShapeDtypeStruct((T, D), BF16)
    vec = _resident((1, D))
    return pl.pallas_call(
        body, name=name, grid=(T // tm,),
        out_shape=(jax.ShapeDtypeStruct((T, D), F32), big, big, big, big),
        in_specs=[_rows(tm, D), _rows(tm, D), _halo_rows_spec(tm, D, CONV_HALO), _rows(tm, D), _rows(tm, D), _rows(tm, D),
                  _resident((K, D)), vec, vec, vec, _resident((D, D)), _resident((D, D)), _resident((D, D))],
        out_specs=(_rows(tm, D),) * 5,
        scratch_shapes=[pltpu.VMEM((CONV_HALO + tm, D), F32)],
        compiler_params=_params("arbitrary"),
    )(x, u0, u0, o, sgc, sgt, dw_w, dw_b, ln_g, ln_b, wcp, wo, wout)


def mix_out_bwd(dx2, u1, co, ao, sgc, sgt, ln_g, ln_b, wcp, wo, wout, name):
    T, D = dx2.shape
    tm = min(ROW_TILE, T)

    def body(dx_ref, u1_ref, co_ref, ao_ref, sgc_ref, sgt_ref, lg_ref, lb_ref, wcp_ref, wo_ref, wout_ref,
             dgc_ref, dgt_ref, do_ref, du1_ref, dco_ref, dao_ref, u3_ref, sums_ref):
        dm = _nt(dx_ref[...].astype(BF16), wout_ref[...])
        sgc, sgt = sgc_ref[...].astype(F32), sgt_ref[...].astype(F32)
        dco = (dm * sgc).astype(BF16)
        dao = (dm * sgt).astype(BF16)
        dgc = dm * co_ref[...].astype(F32) * sgc * (1.0 - sgc)
        dgt = dm * ao_ref[...].astype(F32) * sgt * (1.0 - sgt)
        dco_ref[...] = dco
        dao_ref[...] = dao
        dgc_ref[...] = dgc.astype(BF16)
        dgt_ref[...] = dgt.astype(BF16)
        do_ref[...] = _nt(dao, wo_ref[...]).astype(BF16)
        du3 = _nt(dco, wcp_ref[...])
        xhat, rstd = _ln_stats(u1_ref[...].astype(F32))
        g = lg_ref[...]
        u2 = xhat * g + lb_ref[...]
        su = _sigmoid(u2)
        u3_ref[...] = (u2 * su).astype(BF16)
        du2 = du3 * (su * (1.0 + u2 * (1.0 - su)))
        dxh = du2 * g
        du1 = rstd * (dxh - jnp.mean(dxh, axis=-1, keepdims=True) - xhat * jnp.mean(dxh * xhat, axis=-1, keepdims=True))
        du1_ref[...] = du1.astype(BF16)

        @pl.when(pl.program_id(0) == 0)
        def _():
            sums_ref[...] = jnp.zeros_like(sums_ref)
        for r, val in enumerate((dgc, dgt, du2 * xhat, du2, du1)):
            sums_ref[pl.ds(r, 1), :] += jnp.sum(val, axis=0, keepdims=True)

    big = jax.ShapeDtypeStruct((T, D), BF16)
    vec = _resident((1, D))
    return pl.pallas_call(
        body, name=name, grid=(T // tm,),
        out_shape=(big,) * 7 + (jax.ShapeDtypeStruct((8, D), F32),),
        in_specs=[_rows(tm, D)] * 6 + [vec, vec, _resident((D, D)), _resident((D, D)), _resident((D, D))],
        out_specs=(_rows(tm, D),) * 7 + (_acc_spec((8, D)),),
        compiler_params=_params("arbitrary"),
    )(dx2, u1, co, ao, sgc, sgt, ln_g, ln_b, wcp, wo, wout)


def conv_bwd(du1, u0, ga, gb, dw_w, name):
    T, D = du1.shape
    tm = min(ROW_TILE, T)
    nt = T // tm
    K = dw_w.shape[0]
    per = tm // CONV_HALO

    def body(d_ref, dn_ref, u_ref, uh_ref, ga_ref, gb_ref, w_ref, dga_ref, dgb_ref, dw_ref, dbuf, ubuf):
        i = pl.program_id(0)
        dbuf[pl.ds(0, tm), :] = d_ref[...].astype(F32)
        dbuf[pl.ds(tm, CONV_HALO), :] = dn_ref[...].astype(F32) * (i < nt - 1).astype(F32)
        ubuf[pl.ds(0, CONV_HALO), :] = uh_ref[...].astype(F32) * (i > 0).astype(F32)
        ubuf[pl.ds(CONV_HALO, tm), :] = u_ref[...].astype(F32)

        @pl.when(i == 0)
        def _():
            dw_ref[...] = jnp.zeros_like(dw_ref)

        d = dbuf[pl.ds(0, tm), :]
        du0 = jnp.zeros((tm, D), F32)
        for k in range(K):
            du0 = du0 + dbuf[pl.ds(K - 1 - k, tm), :] * w_ref[pl.ds(k, 1), :]
            dw_ref[pl.ds(k, 1), :] += jnp.sum(d * ubuf[pl.ds(CONV_HALO - (K - 1) + k, tm), :], axis=0, keepdims=True)
        ga, gb = ga_ref[...].astype(F32), gb_ref[...].astype(F32)
        sg = _sigmoid(gb)
        dga_ref[...] = (du0 * sg).astype(BF16)
        dgb_ref[...] = (du0 * ga * sg * (1.0 - sg)).astype(BF16)

    nxt = pl.BlockSpec((CONV_HALO, D), lambda i: (jnp.minimum((i + 1) * per, nt * per - 1), 0))
    big = jax.ShapeDtypeStruct((T, D), BF16)
    return pl.pallas_call(
        body, name=name, grid=(nt,),
        out_shape=(big, big, jax.ShapeDtypeStruct((K, D), F32)),
        in_specs=[_rows(tm, D), nxt, _rows(tm, D), _halo_rows_spec(tm, D, CONV_HALO), _rows(tm, D), _rows(tm, D),
                  _resident((K, D))],
        out_specs=(_rows(tm, D), _rows(tm, D), _acc_spec((K, D))),
        scratch_shapes=[pltpu.VMEM((tm + CONV_HALO, D), F32), pltpu.VMEM((CONV_HALO + tm, D), F32)],
        compiler_params=_params("arbitrary"),
    )(du1, du1, u0, u0, ga, gb, dw_w)


def mix_in_bwd(dx2, x, gain, win, pieces, name):
    T, D = x.shape
    tm = min(ROW_TILE, T)
    widths = [p.shape[1] for p in pieces]
    offs = [sum(widths[:n]) for n in range(len(widths))]

    def body(dx2_ref, x_ref, g_ref, w_ref, *rest):
        piece_refs, (dx_ref, dg_ref) = rest[:len(pieces)], rest[len(pieces):]
        dh = None
        for p_ref, off, w in zip(piece_refs, offs, widths):
            t = _nt(p_ref[...], w_ref[:, pl.ds(off, w)])
            dh = t if dh is None else dh + t
        dx, dgt = _rms_bwd(x_ref[...], g_ref[...], dh)
        dx_ref[...] = dx2_ref[...] + dx

        @pl.when(pl.program_id(0) == 0)
        def _():
            dg_ref[...] = jnp.zeros_like(dg_ref)
        dg_ref[...] += jnp.sum(dgt, axis=0, keepdims=True)

    return pl.pallas_call(
        body, name=name, grid=(T // tm,),
        out_shape=(jax.ShapeDtypeStruct((T, D), F32), jax.ShapeDtypeStruct((1, D), F32)),
        in_specs=[_rows(tm, D), _rows(tm, D), _resident((1, D)), _resident(win.shape)] + [_rows(tm, w) for w in widths],
        out_specs=(_rows(tm, D), _acc_spec((1, D))),
        compiler_params=_params("arbitrary"),
    )(dx2, x, gain, win, *pieces)


def final_loss(x, gain, target, name):
    T, D = x.shape
    tm = min(ROW_TILE, T)

    def body(x_ref, g_ref, t_ref, loss_ref, dx_ref, dg_ref):
        x, g = x_ref[...], g_ref[...]
        err = _rms_fwd(x, g) - t_ref[...]
        dx, dgt = _rms_bwd(x, g, err * (1.0 / D))
        dx_ref[...] = dx

        @pl.when(pl.program_id(0) == 0)
        def _():
            dg_ref[...] = jnp.zeros_like(dg_ref)
            loss_ref[...] = jnp.zeros_like(loss_ref)
        dg_ref[...] += jnp.sum(dgt, axis=0, keepdims=True)
        per_token = jnp.sum(err * err, axis=-1, keepdims=True) * (0.5 / D)
        loss_ref[...] += jnp.broadcast_to(jnp.sum(per_token, axis=0, keepdims=True), (1, LANES))

    return pl.pallas_call(
        body, name=name, grid=(T // tm,),
        out_shape=(jax.ShapeDtypeStruct((1, LANES), F32), jax.ShapeDtypeStruct((T, D), F32),
                   jax.ShapeDtypeStruct((1, D), F32)),
        in_specs=[_rows(tm, D), _resident((1, D)), _rows(tm, D)],
        out_specs=(_acc_spec((1, LANES)), _rows(tm, D), _acc_spec((1, D))),
        compiler_params=_params("arbitrary"),
    )(x, gain, target)


def rope_tables(positions):
    half = HEAD_DIM // 2
    inv_freq = ROPE_THETA ** (-jnp.arange(half, dtype=F32) / half)
    ang = positions.astype(F32)[:, None] * inv_freq
    cos, sin = jnp.cos(ang), jnp.sin(ang)
    reps = LANES // HEAD_DIM
    return jnp.tile(jnp.concatenate([cos, cos], axis=-1), (1, reps)), jnp.tile(jnp.concatenate([-sin, sin], axis=-1), (1, reps))


def permute_w_in(w, D):
    kv2 = 2 * N_KV_HEADS * HEAD_DIM
    return jnp.concatenate([w[:, :3 * D], w[:, 3 * D + kv2:], w[:, 3 * D:3 * D + kv2]], axis=1)


def local_step(x, positions, target, w):
    T, D = x.shape
    KV = N_KV_HEADS * HEAD_DIM
    group = D // KV
    cos, sin = rope_tables(positions)
    win = permute_w_in(w["w_in"], D)
    sink_col = jnp.repeat(w["attn_sinks"].reshape(-1), WINDOW).reshape(N_KV_HEADS, group * WINDOW, 1)

    x1, h1, a1, b1 = ffn_fwd(x, w["ffn1_norm"], w["ffn1_w_gate"], w["ffn1_w_up"], w["ffn1_w_down"], "ffn1_fwd")
    h2, ga, gb, u0, q, sgc, sgt, k, v = mix_in_fwd(x1, w["mix_norm"], win, w["gate_b"], cos, sin, "mix_in_fwd")
    o = attn_fwd(q, k, v, sink_col, "attn_fwd")
    x2, u1, co, ao, merged = mix_out_fwd(x1, u0, o, sgc, sgt, w["conv_dw_w"], w["conv_dw_b"], w["conv_ln_g"],
                                         w["conv_ln_b"], w["conv_w_proj"], w["attn_w_o"], w["w_out"], "mix_out_fwd")
    x3, h3, a2, b2 = ffn_fwd(x2, w["ffn2_norm"], w["ffn2_w_gate"], w["ffn2_w_up"], w["ffn2_w_down"], "ffn2_fwd")
    loss, dx3, d_final = final_loss(x3, w["final_norm"], target, "final_loss")

    g = {"final_norm": d_final}
    dx2, da2, db2, s2, g02, g["ffn2_norm"] = ffn_bwd(dx3, x2, w["ffn2_norm"], a2, b2, w["ffn2_w_gate"], w["ffn2_w_up"],
                                                     w["ffn2_w_down"], "ffn2_bwd")
    F = a2.shape[1]
    g["ffn2_w_gate"] = wgrad(h3, da2, "ffn2_dwg", tn=F // 2)
    g["ffn2_w_up"] = wgrad(h3, db2, "ffn2_dwu", tn=F // 2)
    g["ffn2_w_down"] = wgrad(s2, g02, "ffn2_dwd")

    dgc, dgt, do, du1, dco, dao, u3, sums = mix_out_bwd(dx2, u1, co, ao, sgc, sgt, w["conv_ln_g"], w["conv_ln_b"],
                                                        w["conv_w_proj"], w["attn_w_o"], w["w_out"], "mix_out_bwd")
    g["gate_b"] = jnp.concatenate([sums[0:1], sums[1:2]], axis=1)
    g["conv_ln_g"], g["conv_ln_b"], g["conv_dw_b"] = sums[2:3], sums[3:4], sums[4:5]
    g["w_out"] = wgrad(merged, dx2, "dw_out")
    g["conv_w_proj"] = wgrad(u3, dco, "dw_conv_proj")
    g["attn_w_o"] = wgrad(o, dao, "dw_attn_o")
    dga, dgb, g["conv_dw_w"] = conv_bwd(du1, u0, ga, gb, w["conv_dw_w"], "conv_bwd")
    dq, dk, dv, dsink = attn_bwd(q, k, v, do, sink_col, cos, sin, "attn_bwd")
    g["attn_sinks"] = dsink[:, 0].reshape(1, -1)
    pieces = [dga, dgb, dq, dgc, dgt, dk, dv]
    dx1, g["mix_norm"] = mix_in_bwd(dx2, x1, w["mix_norm"], win, pieces, "mix_in_bwd")
    dwin = [wgrad(h2, p, "dw_in_%d" % n) for n, p in enumerate(pieces)]
    g["w_in"] = jnp.concatenate([dwin[0], dwin[1], dwin[2], dwin[5], dwin[6], dwin[3], dwin[4]], axis=1)

    dx0, da1, db1, s1, g01, g["ffn1_norm"] = ffn_bwd(dx1, x, w["ffn1_norm"], a1, b1, w["ffn1_w_gate"], w["ffn1_w_up"],
                                                     w["ffn1_w_down"], "ffn1_bwd")
    g["ffn1_w_gate"] = wgrad(h1, da1, "ffn1_dwg", tn=F // 2)
    g["ffn1_w_up"] = wgrad(h1, db1, "ffn1_dwu", tn=F // 2)
    g["ffn1_w_down"] = wgrad(s1, g01, "ffn1_dwd")
    return loss, dx0, g


_ANY = pl.BlockSpec(memory_space=pl.ANY)


def _place():
    return lax.axis_index("x"), lax.axis_index("y"), lax.axis_index("c")


def all_gather(block, name):
    def body(x_ref, out_ref, send_sems, recv_sems, local_sem):
        x, y, c = _place()
        me, sibling = (x, y, c), (x, y, 1 - c)
        chips = [(1 - x, y), (x, 1 - y), (1 - x, 1 - y)]

        def rows(px, py, pc):
            return out_ref.at[4 * px + 2 * py + pc]

        def copy(k, block_of, to, src=None):
            return pltpu.make_async_remote_copy(
                src_ref=rows(*block_of) if src is None else src, dst_ref=rows(*block_of),
                send_sem=send_sems.at[k], recv_sem=recv_sems.at[k], device_id=to, device_id_type=MESH)

        mine = pltpu.make_async_copy(x_ref, rows(*me), local_sem)
        mine.start()
        first = [copy(0, me, sibling, src=x_ref)]
        first += [copy(1 + j, me, (*chip, c), src=x_ref) for j, chip in enumerate(chips)]
        for cp in first:
            cp.start()
        passed = [copy(4 + j, (*chip, c), sibling) for j, chip in enumerate(chips)]
        for j, chip in enumerate(chips):
            copy(1 + j, (*chip, c), me).wait_recv()
            passed[j].start()
        copy(0, sibling, me).wait_recv()
        for j, chip in enumerate(chips):
            copy(4 + j, (*chip, 1 - c), me).wait_recv()
        for cp in first + passed:
            cp.wait_send()
        mine.wait()

    return pl.pallas_call(
        body, name=name, out_shape=jax.ShapeDtypeStruct((N_DEV,) + block.shape, block.dtype),
        in_specs=[_ANY], out_specs=_ANY,
        scratch_shapes=[pltpu.SemaphoreType.DMA((7,)), pltpu.SemaphoreType.DMA((7,)), pltpu.SemaphoreType.DMA],
    )(block)


def swap_with_sibling(block, name):
    def body(s_ref, r_ref, send_sem, recv_sem):
        x, y, c = _place()
        cp = pltpu.make_async_remote_copy(src_ref=s_ref, dst_ref=r_ref, send_sem=send_sem, recv_sem=recv_sem,
                                          device_id=(x, y, 1 - c), device_id_type=MESH)
        cp.start()
        cp.wait()

    return pl.pallas_call(
        body, name=name, out_shape=jax.ShapeDtypeStruct(block.shape, block.dtype), in_specs=[_ANY], out_specs=_ANY,
        scratch_shapes=[pltpu.SemaphoreType.DMA, pltpu.SemaphoreType.DMA],
    )(block)


_CHIP_FLIPS = ((1, 0), (0, 1), (1, 1))


def exchange_between_chips(by_chip, name):
    def body(s_ref, r_ref, send_sems, recv_sems):
        x, y, c = _place()
        copies = []
        for k, (fx, fy) in enumerate(_CHIP_FLIPS):
            tx = 1 - x if fx else x
            ty = 1 - y if fy else y
            cp = pltpu.make_async_remote_copy(src_ref=s_ref.at[2 * tx + ty], dst_ref=r_ref.at[k], send_sem=send_sems.at[k],
                                              recv_sem=recv_sems.at[k], device_id=(tx, ty, c), device_id_type=MESH)
            cp.start()
            copies.append(cp)
        for cp in copies:
            cp.wait()

    return pl.pallas_call(
        body, name=name, out_shape=jax.ShapeDtypeStruct((3,) + by_chip.shape[1:], by_chip.dtype),
        in_specs=[_ANY], out_specs=_ANY,
        scratch_shapes=[pltpu.SemaphoreType.DMA((3,)), pltpu.SemaphoreType.DMA((3,))],
    )(by_chip)


def pair_sum(keep, received, name):
    n, R, C = keep.shape
    tr = R // 2 if R % 16 == 0 and R > 1024 else R
    spec = pl.BlockSpec((1, tr, C), lambda j, r: (j, r, 0))

    def body(a_ref, b_ref, o_ref):
        o_ref[...] = (a_ref[...] + b_ref[...].astype(F32)).astype(BF16)

    return pl.pallas_call(
        body, name=name, grid=(n, R // tr), out_shape=jax.ShapeDtypeStruct(keep.shape, BF16),
        in_specs=[spec, spec], out_specs=spec, compiler_params=_params("parallel", "parallel"),
    )(keep, received)


def chip_sum(own, received, name):
    R, C = own.shape
    tr = R // 2 if R % 16 == 0 and R > 1024 else R

    def body(a_ref, b_ref, o_ref):
        acc = a_ref[...].astype(F32)
        for k in range(3):
            acc = acc + b_ref[k].astype(F32)
        o_ref[...] = acc

    return pl.pallas_call(
        body, name=name, grid=(R // tr,), out_shape=jax.ShapeDtypeStruct((R, C), F32),
        in_specs=[pl.BlockSpec((tr, C), lambda r: (r, 0)), pl.BlockSpec((3, tr, C), lambda r: (0, r, 0))],
        out_specs=pl.BlockSpec((tr, C), lambda r: (r, 0)), compiler_params=_params("parallel"),
    )(own, received)


def _adamw_math(w, g, m, v):
    m = ADAM_B1 * m + (1.0 - ADAM_B1) * g
    v = ADAM_B2 * v + (1.0 - ADAM_B2) * (g * g)
    m_hat = m / (1.0 - ADAM_B1 ** ADAM_STEP)
    v_hat = v / (1.0 - ADAM_B2 ** ADAM_STEP)
    delta = -ADAM_LR * (m_hat / (jnp.sqrt(v_hat) + ADAM_EPS) + ADAM_WD * w)
    return delta, m, v


def adamw(ws, gs, ms, vs, name):
    n = len(ws)

    def body(*refs):
        ins, outs = refs[:4 * n], refs[4 * n:]
        for t in range(n):
            delta, m, v = _adamw_math(ins[t][...], ins[n + t][...], ins[2 * n + t][...], ins[3 * n + t][...])
            outs[3 * t][...] = delta
            outs[3 * t + 1][...] = m
            outs[3 * t + 2][...] = v

    shapes = []
    for w in ws:
        shapes += [jax.ShapeDtypeStruct(w.shape, F32)] * 3
    res = pl.pallas_call(body, name=name, out_shape=tuple(shapes), compiler_params=_params())(*ws, *gs, *ms, *vs)
    return [tuple(res[3 * t:3 * t + 3]) for t in range(n)]


def adamw_replicated(w, partials, m, v, name):
    def body(w_ref, p_ref, m_ref, v_ref, g_ref, d_ref, mo_ref, vo_ref):
        g = p_ref[0]
        for k in range(1, N_DEV):
            g = g + p_ref[k]
        g_ref[...] = g
        d_ref[...], mo_ref[...], vo_ref[...] = _adamw_math(w_ref[...], g, m_ref[...], v_ref[...])

    shape = jax.ShapeDtypeStruct(w.shape, F32)
    return pl.pallas_call(body, name=name, out_shape=(shape,) * 4, compiler_params=_params())(w, partials, m, v)


PACK_COLS = 1024
PACK_ROW_ALIGN = 16

SHARDED = {"ffn1_w_gate": 1, "ffn1_w_up": 1, "ffn1_w_down": 0, "w_in": 1, "conv_dw_w": 1, "conv_w_proj": 0, "attn_w_o": 0,
           "w_out": 0, "ffn2_w_gate": 1, "ffn2_w_up": 1, "ffn2_w_down": 0}
REPLICATED = ("ffn1_norm", "mix_norm", "conv_dw_b", "conv_ln_g", "conv_ln_b", "ffn2_norm", "final_norm", "gate_b", "attn_sinks")
WEIGHT_ORDER = ("ffn1_norm", "ffn1_w_gate", "ffn1_w_up", "ffn1_w_down", "mix_norm", "w_in", "conv_dw_w", "conv_dw_b", "conv_ln_g",
                "conv_ln_b", "conv_w_proj", "attn_sinks", "attn_w_o", "gate_b", "w_out", "ffn2_norm", "ffn2_w_gate", "ffn2_w_up",
                "ffn2_w_down", "final_norm")


def _to_rows(flat, lead):
    n = flat.shape[-1]
    rows = -(-n // PACK_COLS)
    flat = jnp.pad(flat, [(0, 0)] * lead + [(0, rows * PACK_COLS - n)])
    return flat.reshape(flat.shape[:lead] + (rows, PACK_COLS))


def _pad_rows(a, axis):
    rows = a.shape[axis]
    pad = -rows % PACK_ROW_ALIGN
    widths = [(0, 0)] * a.ndim
    widths[axis] = (0, pad)
    return jnp.pad(a, widths)


def _from_rows(rows, shape):
    n = 1
    for s in shape:
        n *= s
    return rows.reshape(rows.shape[:-2] + (-1,))[..., :n].reshape(rows.shape[:-2] + tuple(shape))


def _split_shards(full, axis):
    if axis == 0:
        return full.reshape((N_DEV, full.shape[0] // N_DEV, full.shape[1]))
    return full.reshape((full.shape[0], N_DEV, full.shape[1] // N_DEV)).transpose(1, 0, 2)


def _join_shards(shards, axis):
    if axis == 0:
        return shards.reshape((-1, shards.shape[2]))
    return shards.transpose(1, 0, 2).reshape((shards.shape[1], -1))


def kernel(x, positions, ffn1_norm, ffn1_w_gate, ffn1_w_up, ffn1_w_down, mix_norm, w_in, conv_dw_w, conv_dw_b, conv_ln_g, conv_ln_b, conv_w_proj, attn_sinks, attn_w_o, gate_b, w_out, ffn2_norm, ffn2_w_gate, ffn2_w_up, ffn2_w_down, final_norm, loss_target, m_ffn1_norm, m_ffn1_w_gate, m_ffn1_w_up, m_ffn1_w_down, m_mix_norm, m_w_in, m_conv_dw_w, m_conv_dw_b, m_conv_ln_g, m_conv_ln_b, m_conv_w_proj, m_attn_sinks, m_attn_w_o, m_gate_b, m_w_out, m_ffn2_norm, m_ffn2_w_gate, m_ffn2_w_up, m_ffn2_w_down, m_final_norm, v_ffn1_norm, v_ffn1_w_gate, v_ffn1_w_up, v_ffn1_w_down, v_mix_norm, v_w_in, v_conv_dw_w, v_conv_dw_b, v_conv_ln_g, v_conv_ln_b, v_conv_w_proj, v_attn_sinks, v_attn_w_o, v_gate_b, v_w_out, v_ffn2_norm, v_ffn2_w_gate, v_ffn2_w_up, v_ffn2_w_down, v_final_norm):
    given = dict(locals())
    shapes = {n: given[n].shape for n in WEIGHT_ORDER}
    w = {n: given[n].reshape(given[n].shape[-2:]) if given[n].ndim == 3 else given[n].reshape(1, -1) for n in WEIGHT_ORDER}
    m = {n: given["m_" + n].reshape(w[n].shape) for n in WEIGHT_ORDER}
    v = {n: given["v_" + n].reshape(w[n].shape) for n in WEIGHT_ORDER}
    my_c = lax.axis_index("c")
    my_chip = 2 * lax.axis_index("x") + lax.axis_index("y")

    parts, layout, off = [], {}, 0
    for n in SHARDED:
        if n == "conv_dw_w":
            flat = lax.bitcast_convert_type(w[n], BF16).reshape(-1)
        else:
            flat = w[n].astype(BF16).reshape(-1)
        rows = _to_rows(flat, 0)
        layout[n] = (off, rows.shape[0])
        off += rows.shape[0]
        parts.append(rows)
    gathered = all_gather(_pad_rows(jnp.concatenate(parts, axis=0), 0), "gather_weights")
    full = {}
    for n, axis in SHARDED.items():
        o, r = layout[n]
        if n == "conv_dw_w":
            shards = lax.bitcast_convert_type(_from_rows(gathered[:, o:o + r], w[n].shape + (2,)), F32)
        else:
            shards = _from_rows(gathered[:, o:o + r], w[n].shape)
        full[n] = _join_shards(shards, axis)
    for n in REPLICATED:
        full[n] = w[n]

    loss, grad_x, g_part = local_step(x[0], positions[0], loss_target[0], full)

    parts, glayout, off = [], {}, 0
    for n, axis in SHARDED.items():
        rows = _to_rows(_split_shards(g_part[n], axis).reshape(N_DEV, -1), 1)
        glayout[n] = (off, rows.shape[1])
        off += rows.shape[1]
        parts.append(rows)
    packed = _pad_rows(jnp.concatenate(parts, axis=1), 1)
    by_core = packed.reshape((N_DEV // 2, 2) + packed.shape[1:])
    keep = lax.dynamic_index_in_dim(by_core, my_c, axis=1, keepdims=False)
    send = lax.dynamic_index_in_dim(by_core, 1 - my_c, axis=1, keepdims=False).astype(BF16)
    from_sibling = swap_with_sibling(send, "grads_to_sibling")
    by_chip = pair_sum(keep, from_sibling, "grads_pair_sum")
    from_chips = exchange_between_chips(by_chip, "grads_between_chips")
    own = lax.dynamic_index_in_dim(by_chip, my_chip, axis=0, keepdims=False)
    g_rows = chip_sum(own, from_chips, "grads_chip_sum")
    grads = {}
    for n in SHARDED:
        o, r = glayout[n]
        grads[n] = _from_rows(g_rows[o:o + r], w[n].shape)

    def pack_small(d, extra):
        rows = [_to_rows(d[n].reshape(-1), 0) for n in REPLICATED] + [_to_rows(extra.reshape(-1), 0)]
        return _pad_rows(jnp.concatenate(rows, axis=0), 0)

    zero = jnp.zeros((1, LANES), F32)
    shares = all_gather(pack_small(g_part, loss), "gather_small_grads")
    g_s, d_s, m_s, v_s = adamw_replicated(pack_small(w, zero), shares, pack_small(m, zero), pack_small(v, zero), "adamw_replicated")
    delta, new_m, new_v = {}, {}, {}
    off = 0
    for n in REPLICATED:
        r = -(-w[n].shape[1] // PACK_COLS)
        grads[n], delta[n], new_m[n], new_v[n] = (_from_rows(a[off:off + r], w[n].shape) for a in (g_s, d_s, m_s, v_s))
        off += r
    total_loss = g_s[off, 0]

    groups = (("ffn1_w_gate", "ffn1_w_up", "ffn1_w_down"), ("ffn2_w_gate", "ffn2_w_up", "ffn2_w_down"),
              ("w_in", "conv_dw_w", "conv_w_proj", "attn_w_o", "w_out"))
    for k, names in enumerate(groups):
        res = adamw([w[n] for n in names], [grads[n] for n in names], [m[n] for n in names], [v[n] for n in names],
                    "adamw_%d" % k)
        for n, (d, mm, vv) in zip(names, res):
            delta[n], new_m[n], new_v[n] = d, mm, vv

    out = [total_loss, grad_x[None]]
    for d in (grads, delta, new_m, new_v):
        out += [d[n].reshape(shapes[n]) for n in WEIGHT_ORDER]
    return tuple(out)
```

```python
import functools
from typing import Callable, NamedTuple

import jax
import jax.numpy as jnp
from jax import lax
from jax.experimental import pallas as pl
from jax.experimental.pallas import tpu as pltpu

F32, BF16 = jnp.float32, jnp.bfloat16

HEAD_DIM = 64
N_KV_HEADS = 4
WINDOW = 128
CONV_WIDTH = 31
ROPE_THETA = 10000.0
EPS = 1e-6
LN_EPS = 1e-5
NEG_INF = -1e30
ADAM_LR, ADAM_B1, ADAM_B2, ADAM_EPS, ADAM_WD, ADAM_STEP = 0.001, 0.9, 0.999, 1e-08, 0.01, 10

N_DEV = 8
LANES = 128
CONV_HALO = 32
ROW_TILE = 512
FFN_CHUNK = 256
VMEM_LIMIT = 56 * 2 ** 20
MESH = pl.DeviceIdType.MESH


def _params(*sem):
    return pltpu.CompilerParams(dimension_semantics=sem or None, vmem_limit_bytes=VMEM_LIMIT)


def _resident(shape):
    zeros = (0,) * len(shape)
    return pl.BlockSpec(shape, lambda *_: zeros, pipeline_mode=pl.Buffered(1))


def _rows(tm, n):
    return pl.BlockSpec((tm, n), lambda i: (i, 0))


def _acc_spec(shape):
    zeros = (0,) * len(shape)
    return pl.BlockSpec(shape, lambda *_: zeros)


_ANY = pl.BlockSpec(memory_space=pl.ANY)


class Carry(NamedTuple):
    ins: tuple
    out_shapes: tuple
    aliases: dict
    sems: tuple
    start: Callable
    finish: Callable


def _call(body, *, name, grid, in_specs, out_specs, out_shape, args, scratch_shapes=(), carry=None):
    n_in, n_out, n_scr = len(in_specs), len(out_specs), len(scratch_shapes)
    params = _params(*(("arbitrary",) * len(grid)))
    if carry is None:
        res = pl.pallas_call(body, name=name, grid=grid, in_specs=list(in_specs), out_specs=tuple(out_specs),
                             out_shape=tuple(out_shape), scratch_shapes=list(scratch_shapes), compiler_params=params)(*args)
        return tuple(res), ()
    c_in, c_out = len(carry.ins), len(carry.out_shapes)

    def wrapped(*refs):
        ins, c_ins = refs[:n_in], refs[n_in:n_in + c_in]
        p = n_in + c_in
        outs, c_outs = refs[p:p + n_out], refs[p + n_out:p + n_out + c_out]
        p += n_out + c_out
        scr, c_sems = refs[p:p + n_scr], refs[p + n_scr:]
        ids = [pl.program_id(d) for d in range(len(grid))]
        first = functools.reduce(jnp.logical_and, [i == 0 for i in ids])
        last = functools.reduce(jnp.logical_and, [i == n - 1 for i, n in zip(ids, grid)])

        @pl.when(first)
        def _():
            carry.start(c_ins, c_outs, c_sems)

        body(*ins, *outs, *scr)

        @pl.when(last)
        def _():
            carry.finish(c_ins, c_outs, c_sems)

    res = pl.pallas_call(
        wrapped, name=name, grid=grid, in_specs=list(in_specs) + [_ANY] * c_in, out_specs=tuple(out_specs) + (_ANY,) * c_out,
        out_shape=tuple(out_shape) + tuple(carry.out_shapes), scratch_shapes=list(scratch_shapes) + list(carry.sems),
        input_output_aliases={n_in + i: n_out + o for i, o in carry.aliases.items()}, compiler_params=params,
    )(*args, *carry.ins)
    return tuple(res[:n_out]), tuple(res[n_out:])


def _nt(a, b):
    return lax.dot_general(a, b, (((1,), (1,)), ((), ())), preferred_element_type=F32)


def _tn(a, b):
    return lax.dot_general(a, b, (((0,), (0,)), ((), ())), preferred_element_type=F32)


def _dot(a, b):
    return jnp.dot(a, b, preferred_element_type=F32)


def _sigmoid(x):
    return 1.0 / (1.0 + jnp.exp(-x))


def _rms_fwd(x, g):
    r = lax.rsqrt(jnp.mean(x * x, axis=-1, keepdims=True) + EPS)
    return (x * r) * g


def _rms_bwd(x, g, dy):
    r = lax.rsqrt(jnp.mean(x * x, axis=-1, keepdims=True) + EPS)
    xhat = x * r
    dyg = dy * g
    dx = r * (dyg - xhat * jnp.mean(dyg * xhat, axis=-1, keepdims=True))
    return dx, dy * xhat


def _rot_half(x):
    lane = lax.broadcasted_iota(jnp.int32, (x.shape[0], LANES), 1)
    first = (lane % HEAD_DIM) < (HEAD_DIM // 2)
    out = []
    for s in range(x.shape[1] // LANES):
        xs = x[:, LANES * s:LANES * (s + 1)]
        out.append(jnp.where(first, pltpu.roll(xs, LANES - HEAD_DIM // 2, 1), pltpu.roll(xs, HEAD_DIM // 2, 1)))
    return out[0] if len(out) == 1 else jnp.concatenate(out, axis=1)


def _tile_lanes(t, width):
    return t if width == LANES else jnp.concatenate([t] * (width // LANES), axis=1)


def _rope_fwd(x, cos, sin_signed):
    w = x.shape[1]
    return x * _tile_lanes(cos, w) + _rot_half(x) * _tile_lanes(sin_signed, w)


def _rope_bwd(dy, cos, sin_signed):
    w = dy.shape[1]
    return dy * _tile_lanes(cos, w) + _rot_half(dy * _tile_lanes(sin_signed, w))


def ffn_fwd(x, gain, wgt, wut, wd, name, carry=None):
    T, D = x.shape
    F = wgt.shape[0]
    tm = min(ROW_TILE, T)
    fc = FFN_CHUNK

    def body(x_ref, g_ref, wg_ref, wu_ref, wd_ref, xo_ref, h_ref, a_ref, b_ref, acc_ref):
        x = x_ref[...]
        h = _rms_fwd(x, g_ref[...]).astype(BF16)
        h_ref[...] = h
        for c in range(F // fc):
            cs = pl.ds(c * fc, fc)
            a = _nt(h, wg_ref[cs, :])
            b = _nt(h, wu_ref[cs, :])
            a_ref[:, cs] = a.astype(BF16)
            b_ref[:, cs] = b.astype(BF16)
            s = (a * _sigmoid(a) * b).astype(BF16)
            y = _dot(s, wd_ref[cs, :])
            if c == 0:
                acc_ref[...] = y
            else:
                acc_ref[...] += y
        xo_ref[...] = x + 0.5 * acc_ref[...]

    return _call(
        body, name=name, grid=(T // tm,),
        out_shape=(jax.ShapeDtypeStruct((T, D), F32), jax.ShapeDtypeStruct((T, D), BF16),
                   jax.ShapeDtypeStruct((T, F), BF16), jax.ShapeDtypeStruct((T, F), BF16)),
        in_specs=[_rows(tm, D), _resident((1, D)), _resident((F, D)), _resident((F, D)), _resident((F, D))],
        out_specs=(_rows(tm, D), _rows(tm, D), _rows(tm, F), _rows(tm, F)),
        scratch_shapes=[pltpu.VMEM((tm, D), F32)], args=(x, gain, wgt, wut, wd), carry=carry)


def ffn_bwd(dxo, x, gain, a, b, wg, wu, wd, name):
    T, D = x.shape
    F = wg.shape[0]
    tm = min(ROW_TILE // 2, T)
    fc = FFN_CHUNK

    def body(dxo_ref, x_ref, g_ref, a_ref, b_ref, wg_ref, wu_ref, wd_ref,
             dx_ref, da_ref, db_ref, s_ref, g0_ref, dg_ref, acc_ref):
        dxo = dxo_ref[...]
        g0 = (0.5 * dxo).astype(BF16)
        g0_ref[...] = g0
        for c in range(F // fc):
            cs = pl.ds(c * fc, fc)
            ds = _nt(g0, wd_ref[cs, :])
            a = a_ref[:, cs].astype(F32)
            bb = b_ref[:, cs].astype(F32)
            sa = _sigmoid(a)
            silu = a * sa
            da = (ds * bb * (sa * (1.0 + a * (1.0 - sa)))).astype(BF16)
            db = (ds * silu).astype(BF16)
            da_ref[:, cs] = da
            db_ref[:, cs] = db
            s_ref[:, cs] = (silu * bb).astype(BF16)
            dh = _dot(da, wg_ref[cs, :]) + _dot(db, wu_ref[cs, :])
            if c == 0:
                acc_ref[...] = dh
            else:
                acc_ref[...] += dh
        dx, dgt = _rms_bwd(x_ref[...], g_ref[...], acc_ref[...])
        dx_ref[...] = dxo + dx

        @pl.when(pl.program_id(0) == 0)
        def _():
            dg_ref[...] = jnp.zeros_like(dg_ref)
        dg_ref[...] += jnp.sum(dgt, axis=0, keepdims=True)

    return pl.pallas_call(
        body, name=name, grid=(T // tm,),
        out_shape=(jax.ShapeDtypeStruct((T, D), F32), jax.ShapeDtypeStruct((T, F), BF16),
                   jax.ShapeDtypeStruct((T, F), BF16), jax.ShapeDtypeStruct((T, F), BF16),
                   jax.ShapeDtypeStruct((T, D), BF16), jax.ShapeDtypeStruct((1, D), F32)),
        in_specs=[_rows(tm, D), _rows(tm, D), _resident((1, D)), _rows(tm, F), _rows(tm, F),
                  _resident((F, D)), _resident((F, D)), _resident((F, D))],
        out_specs=(_rows(tm, D), _rows(tm, F), _rows(tm, F), _rows(tm, F), _rows(tm, D), _acc_spec((1, D))),
        scratch_shapes=[pltpu.VMEM((tm, D), F32)],
        compiler_params=_params("arbitrary"),
    )(dxo, x, gain, a, b, wg, wu, wd)


def wgrad(a, b, name, carry=None):
    T, M = a.shape
    N = b.shape[1]
    tk = min(ROW_TILE, T)
    nk = T // tk

    def body(a_ref, b_ref, o_ref, acc_ref):
        k = pl.program_id(0)
        part = _tn(a_ref[...].astype(BF16), b_ref[...].astype(BF16))

        @pl.when(k == 0)
        def _():
            acc_ref[...] = part

        @pl.when(k > 0)
        def _():
            acc_ref[...] += part

        @pl.when(k == nk - 1)
        def _():
            o_ref[...] = acc_ref[...].astype(BF16)

    (out,), carried = _call(
        body, name=name, grid=(nk,), out_shape=(jax.ShapeDtypeStruct((M, N), BF16),),
        in_specs=[_rows(tk, M), _rows(tk, N)], out_specs=(_acc_spec((M, N)),),
        scratch_shapes=[pltpu.VMEM((M, N), F32)], args=(a, b), carry=carry)
    return out, carried


def mix_in_fwd(x, gain, wint, gate_b, cos, sin_signed, name, carry=None):
    T, D = x.shape
    KV = N_KV_HEADS * HEAD_DIM
    tm = min(ROW_TILE, T)
    o_ga, o_gb, o_q, o_gc, o_gt, o_k, o_v = 0, D, 2 * D, 3 * D, 4 * D, 5 * D, 5 * D + KV

    def body(x_ref, g_ref, w_ref, gb_ref, cos_ref, sin_ref,
             h_ref, ga_ref, gb_out_ref, u0_ref, q_ref, sgc_ref, sgt_ref, k_ref, v_ref):
        h = _rms_fwd(x_ref[...], g_ref[...]).astype(BF16)
        h_ref[...] = h
        cos, sin = cos_ref[...], sin_ref[...]
        ga = _nt(h, w_ref[pl.ds(o_ga, D), :])
        gb = _nt(h, w_ref[pl.ds(o_gb, D), :])
        ga_ref[...] = ga.astype(BF16)
        gb_out_ref[...] = gb.astype(BF16)
        u0_ref[...] = (ga * _sigmoid(gb)).astype(BF16)
        q = _nt(h, w_ref[pl.ds(o_q, D), :])
        q_ref[...] = _rope_fwd(q, cos, sin).astype(BF16)
        gc = _nt(h, w_ref[pl.ds(o_gc, D), :]) + gb_ref[:, pl.ds(0, D)]
        sgc_ref[...] = _sigmoid(gc).astype(BF16)
        gt = _nt(h, w_ref[pl.ds(o_gt, D), :]) + gb_ref[:, pl.ds(D, D)]
        sgt_ref[...] = _sigmoid(gt).astype(BF16)
        k = _nt(h, w_ref[pl.ds(o_k, KV), :])
        k_ref[...] = _rope_fwd(k, cos, sin).astype(BF16)
        v_ref[...] = _nt(h, w_ref[pl.ds(o_v, KV), :]).astype(BF16)

    big = jax.ShapeDtypeStruct((T, D), BF16)
    small = jax.ShapeDtypeStruct((T, KV), BF16)
    return _call(
        body, name=name, grid=(T // tm,),
        out_shape=(big, big, big, big, big, big, big, small, small),
        in_specs=[_rows(tm, D), _resident((1, D)), _resident(wint.shape), _resident((1, 2 * D)),
                  _rows(tm, LANES), _rows(tm, LANES)],
        out_specs=(_rows(tm, D),) * 7 + (_rows(tm, KV),) * 2,
        args=(x, gain, wint, gate_b, cos, sin_signed), carry=carry)


def _attn_masks(tile_is_first, rows):
    qi = lax.broadcasted_iota(jnp.int32, (rows, 2 * WINDOW), 0) % WINDOW
    c = lax.broadcasted_iota(jnp.int32, (rows, 2 * WINDOW), 1)
    base = (c > qi) & (c <= qi + WINDOW)
    first_key = jnp.where(tile_is_first, WINDOW, 0)
    return base, base & (c >= first_key)


def _attn_block(q_ref, k_ref, kh_ref, v_ref, vh_ref, j, g, group):
    rows = pl.ds(j * WINDOW, WINDOW)
    qg = jnp.concatenate([q_ref[rows, pl.ds(HEAD_DIM * (group * g + hh), HEAD_DIM)] for hh in range(group)], axis=0)
    kv_lanes = pl.ds(HEAD_DIM * g, HEAD_DIM)
    if j == 0:
        kprev, vprev = kh_ref[:, kv_lanes], vh_ref[:, kv_lanes]
    else:
        prev = pl.ds((j - 1) * WINDOW, WINDOW)
        kprev, vprev = k_ref[prev, kv_lanes], v_ref[prev, kv_lanes]
    kg = jnp.concatenate([kprev, k_ref[rows, kv_lanes]], axis=0)
    vg = jnp.concatenate([vprev, v_ref[rows, kv_lanes]], axis=0)
    return qg, kg, vg


def _attn_probs(qg, kg, sink, mask):
    s = _nt(qg, kg) * (HEAD_DIM ** -0.5)
    s = jnp.where(mask, s, NEG_INF)
    m = jnp.maximum(jnp.max(s, axis=-1, keepdims=True), sink)
    p = jnp.exp(s - m)
    e_sink = jnp.exp(sink - m)
    inv = 1.0 / (jnp.sum(p, axis=-1, keepdims=True) + e_sink)
    return p * inv, e_sink * inv


def _halo_rows_spec(tq, width, sub):
    return pl.BlockSpec((sub, width), lambda i: (jnp.maximum(i * (tq // sub) - 1, 0), 0))


def attn_fwd(q, k, v, sink_col, name, carry=None):
    T, D = q.shape
    KV = k.shape[1]
    group = D // KV
    tq = min(ROW_TILE, T)
    nsub = tq // WINDOW

    def body(q_ref, k_ref, kh_ref, v_ref, vh_ref, sink_ref, o_ref):
        base, first = _attn_masks(pl.program_id(0) == 0, group * WINDOW)
        for j in range(nsub):
            mask = first if j == 0 else base
            for g in range(N_KV_HEADS):
                qg, kg, vg = _attn_block(q_ref, k_ref, kh_ref, v_ref, vh_ref, j, g, group)
                probs, _ = _attn_probs(qg, kg, sink_ref[g], mask)
                og = _dot(probs.astype(BF16), vg)
                for hh in range(group):
                    o_ref[pl.ds(j * WINDOW, WINDOW), pl.ds(HEAD_DIM * (group * g + hh), HEAD_DIM)] = (
                        og[hh * WINDOW:(hh + 1) * WINDOW].astype(BF16))

    (o,), carried = _call(
        body, name=name, grid=(T // tq,),
        out_shape=(jax.ShapeDtypeStruct((T, D), BF16),),
        in_specs=[_rows(tq, D), _rows(tq, KV), _halo_rows_spec(tq, KV, WINDOW), _rows(tq, KV),
                  _halo_rows_spec(tq, KV, WINDOW), _resident(sink_col.shape)],
        out_specs=(_rows(tq, D),), args=(q, k, k, v, v, sink_col), carry=carry)
    return o, carried


def attn_bwd(q, k, v, do, sink_col, cos, sin_signed, name):
    T, D = q.shape
    KV = k.shape[1]
    group = D // KV
    tq = min(ROW_TILE, T)
    nsub = tq // WINDOW
    nt = T // tq
    scale = HEAD_DIM ** -0.5

    def rev(i):
        return nt - 1 - i

    def body(q_ref, k_ref, kh_ref, v_ref, vh_ref, do_ref, sink_ref, cos_ref, sin_ref,
             dq_ref, dk_ref, dv_ref, dsink_ref, dq_acc, dk_acc, dv_acc, carry_k, carry_v):
        i = pl.program_id(0)

        @pl.when(i == 0)
        def _():
            carry_k[...] = jnp.zeros_like(carry_k)
            carry_v[...] = jnp.zeros_like(carry_v)
            dsink_ref[...] = jnp.zeros_like(dsink_ref)

        dk_acc[...] = jnp.zeros_like(dk_acc)
        dv_acc[...] = jnp.zeros_like(dv_acc)
        base, first = _attn_masks(rev(i) == 0, group * WINDOW)
        for j in range(nsub):
            mask = first if j == 0 else base
            rows = pl.ds(j * WINDOW, WINDOW)
            for g in range(N_KV_HEADS):
                qg, kg, vg = _attn_block(q_ref, k_ref, kh_ref, v_ref, vh_ref, j, g, group)
                dog = jnp.concatenate(
                    [do_ref[rows, pl.ds(HEAD_DIM * (group * g + hh), HEAD_DIM)] for hh in range(group)], axis=0)
                probs, p_sink = _attn_probs(qg, kg, sink_ref[g], mask)
                dp = _nt(dog, vg)
                delta = jnp.sum(probs * dp, axis=-1, keepdims=True)
                ds = (probs * (dp - delta) * scale).astype(BF16)
                dqg = _dot(ds, kg)
                for hh in range(group):
                    dq_acc[rows, pl.ds(HEAD_DIM * (group * g + hh), HEAD_DIM)] = dqg[hh * WINDOW:(hh + 1) * WINDOW]
                keys = pl.ds(j * WINDOW, 2 * WINDOW)
                kv_lanes = pl.ds(HEAD_DIM * g, HEAD_DIM)
                dk_acc[keys, kv_lanes] += _tn(ds, qg)
                dv_acc[keys, kv_lanes] += _tn(probs.astype(BF16), dog)
                dsk = -(p_sink * delta)
                for hh in range(group):
                    head = group * g + hh
                    tot = jnp.sum(dsk[hh * WINDOW:(hh + 1) * WINDOW], axis=0, keepdims=True)
                    dsink_ref[pl.ds(head, 1), :] += jnp.broadcast_to(tot, (1, LANES))

        tail = pl.ds(tq, WINDOW)
        dk_acc[tail, :] += carry_k[...]
        dv_acc[tail, :] += carry_v[...]
        carry_k[...] = dk_acc[pl.ds(0, WINDOW), :]
        carry_v[...] = dv_acc[pl.ds(0, WINDOW), :]
        cos, sin = cos_ref[...], sin_ref[...]
        dq_ref[...] = _rope_bwd(dq_acc[...], cos, sin).astype(BF16)
        dk_ref[...] = _rope_bwd(dk_acc[pl.ds(WINDOW, tq), :], cos, sin).astype(BF16)
        dv_ref[...] = dv_acc[pl.ds(WINDOW, tq), :].astype(BF16)

    def rrows(n):
        return pl.BlockSpec((tq, n), lambda i: (rev(i), 0))

    def rhalo(n):
        return pl.BlockSpec((WINDOW, n), lambda i: (jnp.maximum(rev(i) * nsub - 1, 0), 0))

    return pl.pallas_call(
        body, name=name, grid=(nt,),
        out_shape=(jax.ShapeDtypeStruct((T, D), BF16), jax.ShapeDtypeStruct((T, KV), BF16),
                   jax.ShapeDtypeStruct((T, KV), BF16), jax.ShapeDtypeStruct((D // HEAD_DIM, LANES), F32)),
        in_specs=[rrows(D), rrows(KV), rhalo(KV), rrows(KV), rhalo(KV), rrows(D), _resident(sink_col.shape),
                  rrows(LANES), rrows(LANES)],
        out_specs=(rrows(D), rrows(KV), rrows(KV), _acc_spec((D // HEAD_DIM, LANES))),
        scratch_shapes=[pltpu.VMEM((tq, D), F32), pltpu.VMEM((WINDOW + tq, KV), F32), pltpu.VMEM((WINDOW + tq, KV), F32),
                        pltpu.VMEM((WINDOW, KV), F32), pltpu.VMEM((WINDOW, KV), F32)],
        compiler_params=_params("arbitrary"),
    )(q, k, k, v, v, do, sink_col, cos, sin_signed)


def _ln_stats(u):
    mu = jnp.mean(u, axis=-1, keepdims=True)
    d = u - mu
    rstd = lax.rsqrt(jnp.mean(d * d, axis=-1, keepdims=True) + LN_EPS)
    return d * rstd, rstd


def mix_out_fwd(x, u0, o, sgc, sgt, dw_w, dw_b, ln_g, ln_b, wcp, wo, wout, name):
    T, D = x.shape
    tm = min(ROW_TILE, T)
    K = dw_w.shape[0]

    def body(x_ref, u_ref, uh_ref, o_ref, sgc_ref, sgt_ref, w_ref, b_ref, lg_ref, lb_ref, wcp_ref, wo_ref, wout_ref,
             x2_ref, u1_ref, co_ref, ao_ref, mg_ref, buf):
        keep = (pl.program_id(0) > 0).astype(F32)
        buf[pl.ds(0, CONV_HALO), :] = uh_ref[...].astype(F32) * keep
        buf[pl.ds(CONV_HALO, tm), :] = u_ref[...].astype(F32)
        acc = jnp.broadcast_to(b_ref[...], (tm, D))
        for k in range(K):
            acc = acc + buf[pl.ds(CONV_HALO - (K - 1) + k, tm), :] * w_ref[pl.ds(k, 1), :]
        u1_ref[...] = acc.astype(BF16)
        xhat, _ = _ln_stats(acc)
        u2 = xhat * lg_ref[...] + lb_ref[...]
        u3 = (u2 * _sigmoid(u2)).astype(BF16)
        co = _dot(u3, wcp_ref[...])
        ao = _dot(o_ref[...], wo_ref[...])
        co_ref[...] = co.astype(BF16)
        ao_ref[...] = ao.astype(BF16)
        merged = (sgc_ref[...].astype(F32) * co + sgt_ref[...].astype(F32) * ao).astype(BF16)
        mg_ref[...] = merged
        x2_ref[...] = x_ref[...] + _dot(merged, wout_ref[...])

    big = jax.ShapeDtypeStruct((T, D), BF16)
    vec = _resident((1, D))
    return pl.pallas_call(
        body, name=name, grid=(T // tm,),
        out_shape=(jax.ShapeDtypeStruct((T, D), F32), big, big, big, big),
        in_specs=[_rows(tm, D), _rows(tm, D), _halo_rows_spec(tm, D, CONV_HALO), _rows(tm, D), _rows(tm, D), _rows(tm, D),
                  _resident((K, D)), vec, vec, vec, _resident((D, D)), _resident((D, D)), _resident((D, D))],
        out_specs=(_rows(tm, D),) * 5,
        scratch_shapes=[pltpu.VMEM((CONV_HALO + tm, D), F32)],
        compiler_params=_params("arbitrary"),
    )(x, u0, u0, o, sgc, sgt, dw_w, dw_b, ln_g, ln_b, wcp, wo, wout)


def mix_out_bwd(dx2, u1, co, ao, sgc, sgt, ln_g, ln_b, wcp, wo, wout, name):
    T, D = dx2.shape
    tm = min(ROW_TILE, T)

    def body(dx_ref, u1_ref, co_ref, ao_ref, sgc_ref, sgt_ref, lg_ref, lb_ref, wcp_ref, wo_ref, wout_ref,
             dgc_ref, dgt_ref, do_ref, du1_ref, dco_ref, dao_ref, u3_ref, sums_ref):
        dm = _nt(dx_ref[...].astype(BF16), wout_ref[...])
        sgc, sgt = sgc_ref[...].astype(F32), sgt_ref[...].astype(F32)
        dco = (dm * sgc).astype(BF16)
        dao = (dm * sgt).astype(BF16)
        dgc = dm * co_ref[...].astype(F32) * sgc * (1.0 - sgc)
        dgt = dm * ao_ref[...].astype(F32) * sgt * (1.0 - sgt)
        dco_ref[...] = dco
        dao_ref[...] = dao
        dgc_ref[...] = dgc.astype(BF16)
        dgt_ref[...] = dgt.astype(BF16)
        do_ref[...] = _nt(dao, wo_ref[...]).astype(BF16)
        du3 = _nt(dco, wcp_ref[...])
        xhat, rstd = _ln_stats(u1_ref[...].astype(F32))
        g = lg_ref[...]
        u2 = xhat * g + lb_ref[...]
        su = _sigmoid(u2)
        u3_ref[...] = (u2 * su).astype(BF16)
        du2 = du3 * (su * (1.0 + u2 * (1.0 - su)))
        dxh = du2 * g
        du1 = rstd * (dxh - jnp.mean(dxh, axis=-1, keepdims=True) - xhat * jnp.mean(dxh * xhat, axis=-1, keepdims=True))
        du1_ref[...] = du1.astype(BF16)

        @pl.when(pl.program_id(0) == 0)
        def _():
            sums_ref[...] = jnp.zeros_like(sums_ref)
        for r, val in enumerate((dgc, dgt, du2 * xhat, du2, du1)):
            sums_ref[pl.ds(r, 1), :] += jnp.sum(val, axis=0, keepdims=True)

    big = jax.ShapeDtypeStruct((T, D), BF16)
    vec = _resident((1, D))
    return pl.pallas_call(
        body, name=name, grid=(T // tm,),
        out_shape=(big,) * 7 + (jax.ShapeDtypeStruct((8, D), F32),),
        in_specs=[_rows(tm, D)] * 6 + [vec, vec, _resident((D, D)), _resident((D, D)), _resident((D, D))],
        out_specs=(_rows(tm, D),) * 7 + (_acc_spec((8, D)),),
        compiler_params=_params("arbitrary"),
    )(dx2, u1, co, ao, sgc, sgt, ln_g, ln_b, wcp, wo, wout)


def conv_bwd(du1, u0, ga, gb, dw_w, name, carry=None):
    T, D = du1.shape
    tm = min(ROW_TILE, T)
    nt = T // tm
    K = dw_w.shape[0]
    per = tm // CONV_HALO

    def body(d_ref, dn_ref, u_ref, uh_ref, ga_ref, gb_ref, w_ref, dga_ref, dgb_ref, dw_ref, dbuf, ubuf):
        i = pl.program_id(0)
        dbuf[pl.ds(0, tm), :] = d_ref[...].astype(F32)
        dbuf[pl.ds(tm, CONV_HALO), :] = dn_ref[...].astype(F32) * (i < nt - 1).astype(F32)
        ubuf[pl.ds(0, CONV_HALO), :] = uh_ref[...].astype(F32) * (i > 0).astype(F32)
        ubuf[pl.ds(CONV_HALO, tm), :] = u_ref[...].astype(F32)

        @pl.when(i == 0)
        def _():
            dw_ref[...] = jnp.zeros_like(dw_ref)

        d = dbuf[pl.ds(0, tm), :]
        du0 = jnp.zeros((tm, D), F32)
        for k in range(K):
            du0 = du0 + dbuf[pl.ds(K - 1 - k, tm), :] * w_ref[pl.ds(k, 1), :]
            dw_ref[pl.ds(k, 1), :] += jnp.sum(d * ubuf[pl.ds(CONV_HALO - (K - 1) + k, tm), :], axis=0, keepdims=True)
        ga, gb = ga_ref[...].astype(F32), gb_ref[...].astype(F32)
        sg = _sigmoid(gb)
        dga_ref[...] = (du0 * sg).astype(BF16)
        dgb_ref[...] = (du0 * ga * sg * (1.0 - sg)).astype(BF16)

    nxt = pl.BlockSpec((CONV_HALO, D), lambda i: (jnp.minimum((i + 1) * per, nt * per - 1), 0))
    big = jax.ShapeDtypeStruct((T, D), BF16)
    return _call(
        body, name=name, grid=(nt,),
        out_shape=(big, big, jax.ShapeDtypeStruct((K, D), F32)),
        in_specs=[_rows(tm, D), nxt, _rows(tm, D), _halo_rows_spec(tm, D, CONV_HALO), _rows(tm, D), _rows(tm, D),
                  _resident((K, D))],
        out_specs=(_rows(tm, D), _rows(tm, D), _acc_spec((K, D))),
        scratch_shapes=[pltpu.VMEM((tm + CONV_HALO, D), F32), pltpu.VMEM((CONV_HALO + tm, D), F32)],
        args=(du1, du1, u0, u0, ga, gb, dw_w), carry=carry)


def mix_in_bwd(dx2, x, gain, win, pieces, name):
    T, D = x.shape
    tm = min(ROW_TILE, T)
    widths = [p.shape[1] for p in pieces]
    offs = [sum(widths[:n]) for n in range(len(widths))]

    def body(dx2_ref, x_ref, g_ref, w_ref, *rest):
        piece_refs, (dx_ref, dg_ref) = rest[:len(pieces)], rest[len(pieces):]
        dh = None
        for p_ref, off, w in zip(piece_refs, offs, widths):
            t = _dot(p_ref[...], w_ref[pl.ds(off, w), :])
            dh = t if dh is None else dh + t
        dx, dgt = _rms_bwd(x_ref[...], g_ref[...], dh)
        dx_ref[...] = dx2_ref[...] + dx

        @pl.when(pl.program_id(0) == 0)
        def _():
            dg_ref[...] = jnp.zeros_like(dg_ref)
        dg_ref[...] += jnp.sum(dgt, axis=0, keepdims=True)

    return pl.pallas_call(
        body, name=name, grid=(T // tm,),
        out_shape=(jax.ShapeDtypeStruct((T, D), F32), jax.ShapeDtypeStruct((1, D), F32)),
        in_specs=[_rows(tm, D), _rows(tm, D), _resident((1, D)), _resident(win.shape)] + [_rows(tm, w) for w in widths],
        out_specs=(_rows(tm, D), _acc_spec((1, D))),
        compiler_params=_params("arbitrary"),
    )(dx2, x, gain, win, *pieces)


def final_loss(x, gain, target, name):
    T, D = x.shape
    tm = min(ROW_TILE, T)

    def body(x_ref, g_ref, t_ref, loss_ref, dx_ref, dg_ref):
        x, g = x_ref[...], g_ref[...]
        err = _rms_fwd(x, g) - t_ref[...]
        dx, dgt = _rms_bwd(x, g, err * (1.0 / D))
        dx_ref[...] = dx

        @pl.when(pl.program_id(0) == 0)
        def _():
            dg_ref[...] = jnp.zeros_like(dg_ref)
            loss_ref[...] = jnp.zeros_like(loss_ref)
        dg_ref[...] += jnp.sum(dgt, axis=0, keepdims=True)
        per_token = jnp.sum(err * err, axis=-1, keepdims=True) * (0.5 / D)
        loss_ref[...] += jnp.broadcast_to(jnp.sum(per_token, axis=0, keepdims=True), (1, LANES))

    return pl.pallas_call(
        body, name=name, grid=(T // tm,),
        out_shape=(jax.ShapeDtypeStruct((1, LANES), F32), jax.ShapeDtypeStruct((T, D), F32),
                   jax.ShapeDtypeStruct((1, D), F32)),
        in_specs=[_rows(tm, D), _resident((1, D)), _rows(tm, D)],
        out_specs=(_acc_spec((1, LANES)), _rows(tm, D), _acc_spec((1, D))),
        compiler_params=_params("arbitrary"),
    )(x, gain, target)


def rope_tables(positions):
    half = HEAD_DIM // 2
    inv_freq = ROPE_THETA ** (-jnp.arange(half, dtype=F32) / half)
    ang = positions.astype(F32)[:, None] * inv_freq
    cos, sin = jnp.cos(ang), jnp.sin(ang)
    reps = LANES // HEAD_DIM
    return jnp.tile(jnp.concatenate([cos, cos], axis=-1), (1, reps)), jnp.tile(jnp.concatenate([-sin, sin], axis=-1), (1, reps))


def _place():
    return lax.axis_index("x"), lax.axis_index("y"), lax.axis_index("c")


def all_gather(block, name):
    def body(x_ref, out_ref, send_sems, recv_sems, local_sem):
        x, y, c = _place()
        me, sibling = (x, y, c), (x, y, 1 - c)
        chips = [(1 - x, y), (x, 1 - y), (1 - x, 1 - y)]

        def rows(px, py, pc):
            return out_ref.at[4 * px + 2 * py + pc]

        def copy(k, block_of, to, src=None):
            return pltpu.make_async_remote_copy(
                src_ref=rows(*block_of) if src is None else src, dst_ref=rows(*block_of),
                send_sem=send_sems.at[k], recv_sem=recv_sems.at[k], device_id=to, device_id_type=MESH)

        mine = pltpu.make_async_copy(x_ref, rows(*me), local_sem)
        mine.start()
        first = [copy(0, me, sibling, src=x_ref)]
        first += [copy(1 + j, me, (*chip, c), src=x_ref) for j, chip in enumerate(chips)]
        for cp in first:
            cp.start()
        passed = [copy(4 + j, (*chip, c), sibling) for j, chip in enumerate(chips)]
        for j, chip in enumerate(chips):
            copy(1 + j, (*chip, c), me).wait_recv()
            passed[j].start()
        copy(0, sibling, me).wait_recv()
        for j, chip in enumerate(chips):
            copy(4 + j, (*chip, 1 - c), me).wait_recv()
        for cp in first + passed:
            cp.wait_send()
        mine.wait()

    return pl.pallas_call(
        body, name=name, out_shape=jax.ShapeDtypeStruct((N_DEV,) + block.shape, block.dtype),
        in_specs=[_ANY], out_specs=_ANY,
        scratch_shapes=[pltpu.SemaphoreType.DMA((7,)), pltpu.SemaphoreType.DMA((7,)), pltpu.SemaphoreType.DMA],
    )(block)


def _chips_across(x, y):
    return [(1 - x, y), (x, 1 - y), (1 - x, 1 - y)]


def _dev_index(x, y, c):
    return 4 * x + 2 * y + c


def gather_send(block):
    def copies(in_refs, out_refs, sems):
        (x_ref,), (out_ref,), (send, recv, local) = in_refs, out_refs, sems
        x, y, c = _place()
        targets = [(x, y, 1 - c)] + [(*chip, c) for chip in _chips_across(x, y)]
        outgoing = [pltpu.make_async_remote_copy(src_ref=x_ref, dst_ref=out_ref.at[_dev_index(x, y, c)], send_sem=send.at[k],
                                                 recv_sem=recv.at[k], device_id=t, device_id_type=MESH)
                    for k, t in enumerate(targets)]
        incoming = [pltpu.make_async_remote_copy(src_ref=x_ref, dst_ref=out_ref.at[_dev_index(*t)], send_sem=send.at[k],
                                                 recv_sem=recv.at[k], device_id=t, device_id_type=MESH)
                    for k, t in enumerate(targets)]
        return outgoing, incoming, pltpu.make_async_copy(x_ref, out_ref.at[_dev_index(x, y, c)], local)

    def start(*refs):
        outgoing, _, mine = copies(*refs)
        mine.start()
        for cp in outgoing:
            cp.start()

    def finish(*refs):
        outgoing, incoming, mine = copies(*refs)
        for cp in incoming:
            cp.wait_recv()
        for cp in outgoing:
            cp.wait_send()
        mine.wait()

    return Carry(ins=(block,), out_shapes=(jax.ShapeDtypeStruct((N_DEV,) + block.shape, block.dtype),), aliases={},
                 sems=(pltpu.SemaphoreType.DMA((4,)), pltpu.SemaphoreType.DMA((4,)), pltpu.SemaphoreType.DMA),
                 start=start, finish=finish)


def gather_forward(gathered):
    def copies(in_refs, out_refs, sems):
        (buf,), (send, recv) = out_refs, sems
        x, y, c = _place()
        outgoing, incoming = [], []
        for k, chip in enumerate(_chips_across(x, y)):
            rows = buf.at[_dev_index(*chip, c)]
            outgoing.append(pltpu.make_async_remote_copy(src_ref=rows, dst_ref=rows, send_sem=send.at[k], recv_sem=recv.at[k],
                                                         device_id=(x, y, 1 - c), device_id_type=MESH))
            theirs = buf.at[_dev_index(*chip, 1 - c)]
            incoming.append(pltpu.make_async_remote_copy(src_ref=theirs, dst_ref=theirs, send_sem=send.at[k],
                                                         recv_sem=recv.at[k], device_id=(x, y, 1 - c), device_id_type=MESH))
        return outgoing, incoming

    def start(*refs):
        for cp in copies(*refs)[0]:
            cp.start()

    def finish(*refs):
        outgoing, incoming = copies(*refs)
        for cp in incoming:
            cp.wait_recv()
        for cp in outgoing:
            cp.wait_send()

    return Carry(ins=(gathered,), out_shapes=(jax.ShapeDtypeStruct(gathered.shape, gathered.dtype),), aliases={0: 0},
                 sems=(pltpu.SemaphoreType.DMA((3,)), pltpu.SemaphoreType.DMA((3,))), start=start, finish=finish)


def swap_halves(by_core):
    n = len(by_core)

    def copies(in_refs, out_refs, sems):
        send, recv = sems
        x, y, c = _place()
        return [pltpu.make_async_remote_copy(src_ref=a.at[:, 1 - c], dst_ref=r, send_sem=send.at[i], recv_sem=recv.at[i],
                                             device_id=(x, y, 1 - c), device_id_type=MESH)
                for i, (a, r) in enumerate(zip(in_refs, out_refs))]

    def start(*refs):
        for cp in copies(*refs):
            cp.start()

    def finish(*refs):
        for cp in copies(*refs):
            cp.wait()

    shapes = tuple(jax.ShapeDtypeStruct((a.shape[0],) + a.shape[2:], a.dtype) for a in by_core)
    return Carry(ins=tuple(by_core), out_shapes=shapes, aliases={},
                 sems=(pltpu.SemaphoreType.DMA((n,)), pltpu.SemaphoreType.DMA((n,))), start=start, finish=finish)


def exchange_between_chips(by_chip):
    n = len(by_chip)

    def copies(in_refs, out_refs, sems):
        send, recv = sems
        x, y, c = _place()
        out = []
        for i, (s, r) in enumerate(zip(in_refs, out_refs)):
            for k, (tx, ty) in enumerate(_chips_across(x, y)):
                out.append(pltpu.make_async_remote_copy(
                    src_ref=s.at[2 * tx + ty], dst_ref=r.at[k], send_sem=send.at[3 * i + k], recv_sem=recv.at[3 * i + k],
                    device_id=(tx, ty, c), device_id_type=MESH))
        return out

    def start(*refs):
        for cp in copies(*refs):
            cp.start()

    def finish(*refs):
        for cp in copies(*refs):
            cp.wait()

    shapes = tuple(jax.ShapeDtypeStruct((3,) + a.shape[1:], a.dtype) for a in by_chip)
    return Carry(ins=tuple(by_chip), out_shapes=shapes, aliases={},
                 sems=(pltpu.SemaphoreType.DMA((3 * n,)), pltpu.SemaphoreType.DMA((3 * n,))), start=start, finish=finish)


def run_exchange(carry, name):
    n_in = len(carry.ins)
    n_out = len(carry.out_shapes)

    def body(*refs):
        parts = refs[:n_in], refs[n_in:n_in + n_out], refs[n_in + n_out:]
        carry.start(*parts)
        carry.finish(*parts)

    return pl.pallas_call(
        body, name=name, out_shape=tuple(carry.out_shapes), in_specs=[_ANY] * n_in, out_specs=(_ANY,) * n_out,
        scratch_shapes=list(carry.sems), input_output_aliases=dict(carry.aliases),
    )(*carry.ins)


def pair_sum(my_core, by_core, received, name):
    n = len(by_core)

    def body(core_ref, *refs):
        for a_ref, b_ref, o_ref in zip(refs[:n], refs[n:2 * n], refs[2 * n:]):
            o_ref[0] = (a_ref[0, 0].astype(F32) + b_ref[0].astype(F32)).astype(BF16)

    mine = [pl.BlockSpec((1, 1) + a.shape[2:], lambda j, core: (j, core[0], 0, 0)) for a in by_core]
    theirs = [pl.BlockSpec((1,) + r.shape[1:], lambda j, core: (j, 0, 0)) for r in received]
    return pl.pallas_call(
        body, name=name, out_shape=tuple(jax.ShapeDtypeStruct(r.shape, BF16) for r in received),
        grid_spec=pltpu.PrefetchScalarGridSpec(num_scalar_prefetch=1, grid=(by_core[0].shape[0],), in_specs=mine + theirs,
                                               out_specs=tuple(theirs)),
        compiler_params=_params("arbitrary"),
    )(my_core, *by_core, *received)


def chip_sum(my_chip, by_chip, received, name):
    n = len(by_chip)

    def body(chip_ref, *refs):
        for a_ref, b_ref, o_ref in zip(refs[:n], refs[n:2 * n], refs[2 * n:]):
            acc = a_ref[0].astype(F32)
            for k in range(3):
                acc = acc + b_ref[k].astype(F32)
            o_ref[...] = acc

    own = [pl.BlockSpec((1,) + a.shape[1:], lambda i, chip: (chip[0], 0, 0)) for a in by_chip]
    got = [pl.BlockSpec(r.shape, lambda i, chip: (0, 0, 0)) for r in received]
    outs = tuple(pl.BlockSpec(a.shape[1:], lambda i, chip: (0, 0)) for a in by_chip)
    return pl.pallas_call(
        body, name=name, out_shape=tuple(jax.ShapeDtypeStruct(a.shape[1:], F32) for a in by_chip),
        grid_spec=pltpu.PrefetchScalarGridSpec(num_scalar_prefetch=1, grid=(1,), in_specs=own + got, out_specs=outs),
        compiler_params=_params("arbitrary"),
    )(my_chip, *by_chip, *received)


def _adamw_math(w, g, m, v):
    m = ADAM_B1 * m + (1.0 - ADAM_B1) * g
    v = ADAM_B2 * v + (1.0 - ADAM_B2) * (g * g)
    m_hat = m / (1.0 - ADAM_B1 ** ADAM_STEP)
    v_hat = v / (1.0 - ADAM_B2 ** ADAM_STEP)
    delta = -ADAM_LR * (m_hat / (jnp.sqrt(v_hat) + ADAM_EPS) + ADAM_WD * w)
    return delta, m, v


def adamw(ws, gs, ms, vs, name):
    n = len(ws)

    def body(*refs):
        ins, outs = refs[:4 * n], refs[4 * n:]
        for t in range(n):
            delta, m, v = _adamw_math(ins[t][...], ins[n + t][...], ins[2 * n + t][...], ins[3 * n + t][...])
            outs[3 * t][...] = delta
            outs[3 * t + 1][...] = m
            outs[3 * t + 2][...] = v

    shapes = []
    for w in ws:
        shapes += [jax.ShapeDtypeStruct(w.shape, F32)] * 3
    res = pl.pallas_call(body, name=name, out_shape=tuple(shapes), compiler_params=_params())(*ws, *gs, *ms, *vs)
    return [tuple(res[3 * t:3 * t + 3]) for t in range(n)]


def adamw_replicated(w, partials, m, v, name):
    def body(w_ref, p_ref, m_ref, v_ref, g_ref, d_ref, mo_ref, vo_ref):
        g = p_ref[0]
        for k in range(1, N_DEV):
            g = g + p_ref[k]
        g_ref[...] = g
        d_ref[...], mo_ref[...], vo_ref[...] = _adamw_math(w_ref[...], g, m_ref[...], v_ref[...])

    shape = jax.ShapeDtypeStruct(w.shape, F32)
    return pl.pallas_call(body, name=name, out_shape=(shape,) * 4, compiler_params=_params())(w, partials, m, v)


PACK_COLS = 1024
PACK_ROW_ALIGN = 16

SHARDED = {"ffn1_w_gate": 1, "ffn1_w_up": 1, "ffn1_w_down": 0, "w_in": 1, "conv_dw_w": 1, "conv_w_proj": 0, "attn_w_o": 0,
           "w_out": 0, "ffn2_w_gate": 1, "ffn2_w_up": 1, "ffn2_w_down": 0}
REPLICATED = ("ffn1_norm", "mix_norm", "conv_dw_b", "conv_ln_g", "conv_ln_b", "ffn2_norm", "final_norm", "gate_b", "attn_sinks")
WEIGHT_ORDER = ("ffn1_norm", "ffn1_w_gate", "ffn1_w_up", "ffn1_w_down", "mix_norm", "w_in", "conv_dw_w", "conv_dw_b", "conv_ln_g",
                "conv_ln_b", "conv_w_proj", "attn_sinks", "attn_w_o", "gate_b", "w_out", "ffn2_norm", "ffn2_w_gate", "ffn2_w_up",
                "ffn2_w_down", "final_norm")


def _to_rows(flat, lead):
    n = flat.shape[-1]
    rows = -(-n // PACK_COLS)
    flat = jnp.pad(flat, [(0, 0)] * lead + [(0, rows * PACK_COLS - n)])
    return flat.reshape(flat.shape[:lead] + (rows, PACK_COLS))


def _pad_rows(a, axis):
    rows = a.shape[axis]
    pad = -rows % PACK_ROW_ALIGN
    widths = [(0, 0)] * a.ndim
    widths[axis] = (0, pad)
    return jnp.pad(a, widths)


def _from_rows(rows, shape):
    n = 1
    for s in shape:
        n *= s
    return rows.reshape(rows.shape[:-2] + (-1,))[..., :n].reshape(rows.shape[:-2] + tuple(shape))


def _pack_weights(parts):
    layout, off = [], 0
    for p in parts:
        layout.append((off, p.shape[0]))
        off += p.shape[0]
    return _pad_rows(jnp.concatenate(parts, axis=0), 0), layout


def _gathered_rows(gathered, off, rows):
    return gathered[:, off:off + rows].reshape(N_DEV * rows, gathered.shape[2])


def _by_core(full_rows):
    return full_rows.reshape((N_DEV // 2, 2, full_rows.shape[0] // N_DEV, full_rows.shape[1]))


def kernel(x, positions, ffn1_norm, ffn1_w_gate, ffn1_w_up, ffn1_w_down, mix_norm, w_in, conv_dw_w, conv_dw_b, conv_ln_g, conv_ln_b, conv_w_proj, attn_sinks, attn_w_o, gate_b, w_out, ffn2_norm, ffn2_w_gate, ffn2_w_up, ffn2_w_down, final_norm, loss_target, m_ffn1_norm, m_ffn1_w_gate, m_ffn1_w_up, m_ffn1_w_down, m_mix_norm, m_w_in, m_conv_dw_w, m_conv_dw_b, m_conv_ln_g, m_conv_ln_b, m_conv_w_proj, m_attn_sinks, m_attn_w_o, m_gate_b, m_w_out, m_ffn2_norm, m_ffn2_w_gate, m_ffn2_w_up, m_ffn2_w_down, m_final_norm, v_ffn1_norm, v_ffn1_w_gate, v_ffn1_w_up, v_ffn1_w_down, v_mix_norm, v_w_in, v_conv_dw_w, v_conv_dw_b, v_conv_ln_g, v_conv_ln_b, v_conv_w_proj, v_attn_sinks, v_attn_w_o, v_gate_b, v_w_out, v_ffn2_norm, v_ffn2_w_gate, v_ffn2_w_up, v_ffn2_w_down, v_final_norm):
    given = dict(locals())
    shapes = {n: given[n].shape for n in WEIGHT_ORDER}
    w = {n: given[n].reshape(given[n].shape[-2:]) if given[n].ndim == 3 else given[n].reshape(1, -1) for n in WEIGHT_ORDER}
    m = {n: given["m_" + n].reshape(w[n].shape) for n in WEIGHT_ORDER}
    v = {n: given["v_" + n].reshape(w[n].shape) for n in WEIGHT_ORDER}
    my_x, my_y, my_c = _place()
    my_core = my_c.astype(jnp.int32).reshape(1)
    my_chip = (2 * my_x + my_y).astype(jnp.int32).reshape(1)
    xs, target = x[0], loss_target[0]
    T, D = xs.shape
    KV = N_KV_HEADS * HEAD_DIM
    K = w["conv_dw_w"].shape[0]

    def t16(n):
        return w[n].T.astype(BF16)

    def r16(n):
        return w[n].astype(BF16)

    pack1, lay1 = _pack_weights([t16("ffn1_w_gate"), t16("ffn1_w_up"), r16("ffn1_w_down")])
    dw_bits = _to_rows(lax.bitcast_convert_type(w["conv_dw_w"], BF16).reshape(-1), 0)
    pack2, lay2 = _pack_weights([t16("w_in"), r16("conv_w_proj"), r16("attn_w_o"), r16("w_out"), dw_bits])
    pack3, lay3 = _pack_weights([t16("ffn2_w_gate"), t16("ffn2_w_up"), r16("ffn2_w_down")])
    cos, sin = rope_tables(positions[0])
    sink_col = jnp.repeat(w["attn_sinks"].reshape(-1), WINDOW).reshape(N_KV_HEADS, (D // KV) * WINDOW, 1)

    gath1 = all_gather(pack1, "gather_ffn1")
    wgt1, wut1, wd1 = (_gathered_rows(gath1, *l) for l in lay1)
    (x1, h1, a1, b1), (gath2,) = ffn_fwd(xs, w["ffn1_norm"], wgt1, wut1, wd1, "ffn1_fwd", carry=gather_send(pack2))
    (gath2,) = run_exchange(gather_forward(gath2), "gather_mix_forward")
    wint = _gathered_rows(gath2, *lay2[0])
    wint = jnp.concatenate([wint[:3 * D], wint[3 * D + 2 * KV:], wint[3 * D:3 * D + 2 * KV]], axis=0)
    wcp, wo, wout = (_gathered_rows(gath2, *l) for l in lay2[1:4])
    o_dw, r_dw = lay2[4]
    dw_full = lax.bitcast_convert_type(_from_rows(gath2[:, o_dw:o_dw + r_dw], w["conv_dw_w"].shape + (2,)), F32)
    dw_full = dw_full.transpose(1, 0, 2).reshape(K, D)
    (h2, ga, gb, u0, q, sgc, sgt, kk, vv), (gath3,) = mix_in_fwd(x1, w["mix_norm"], wint, w["gate_b"], cos, sin, "mix_in_fwd",
                                                                 carry=gather_send(pack3))
    o, (gath3,) = attn_fwd(q, kk, vv, sink_col, "attn_fwd", carry=gather_forward(gath3))
    x2, u1, co, ao, merged = mix_out_fwd(x1, u0, o, sgc, sgt, dw_full, w["conv_dw_b"], w["conv_ln_g"], w["conv_ln_b"],
                                         wcp, wo, wout, "mix_out_fwd")
    wgt2, wut2, wd2 = (_gathered_rows(gath3, *l) for l in lay3)
    (x3, h3, a2, b2), _ = ffn_fwd(x2, w["ffn2_norm"], wgt2, wut2, wd2, "ffn2_fwd")
    loss, dx3, d_final = final_loss(x3, w["final_norm"], target, "final_loss")

    small = {"final_norm": d_final}
    dx2, da2, db2, s2, g02, small["ffn2_norm"] = ffn_bwd(dx3, x2, w["ffn2_norm"], a2, b2, wgt2, wut2, wd2, "ffn2_bwd")
    core2 = [_by_core(wgrad(a, b, nm)[0]) for a, b, nm in ((da2, h3, "ffn2_dwg"), (db2, h3, "ffn2_dwu"), (s2, g02, "ffn2_dwd"))]
    chip2 = pair_sum(my_core, core2, run_exchange(swap_halves(core2), "ffn2_grads_swap"), "ffn2_grads_pair_sum")

    dgc, dgt, do, du1, dco, dao, u3, sums = mix_out_bwd(dx2, u1, co, ao, sgc, sgt, w["conv_ln_g"], w["conv_ln_b"],
                                                        wcp, wo, wout, "mix_out_bwd")
    small["gate_b"] = jnp.concatenate([sums[0:1], sums[1:2]], axis=1)
    small["conv_ln_g"], small["conv_ln_b"], small["conv_dw_b"] = sums[2:3], sums[3:4], sums[4:5]
    g_wout = wgrad(merged, dx2, "dw_out")[0]
    g_wcp = wgrad(u3, dco, "dw_conv_proj")[0]
    g_wo = wgrad(o, dao, "dw_attn_o")[0]
    (dga, dgb, g_dw), got2 = conv_bwd(du1, u0, ga, gb, dw_full, "conv_bwd", carry=exchange_between_chips(chip2))
    dq, dk, dv, dsink = attn_bwd(q, kk, vv, do, sink_col, cos, sin, "attn_bwd")
    small["attn_sinks"] = dsink[:, 0].reshape(1, -1)
    pieces = [dga, dgb, dq, dgc, dgt, dk, dv]
    dx1, small["mix_norm"] = mix_in_bwd(dx2, x1, w["mix_norm"], wint, pieces, "mix_in_bwd")
    dwin = [wgrad(p, h2, "dw_in_%d" % n)[0] for n, p in enumerate(pieces)]
    g_wint = jnp.concatenate([dwin[0], dwin[1], dwin[2], dwin[5], dwin[6], dwin[3], dwin[4]], axis=0)
    corem = [_by_core(a) for a in (g_wint, g_wcp, g_wo, g_wout)]
    chipm = pair_sum(my_core, corem, run_exchange(swap_halves(corem), "mix_grads_swap"), "mix_grads_pair_sum")

    grad_x, da1, db1, s1, g01, small["ffn1_norm"] = ffn_bwd(dx1, xs, w["ffn1_norm"], a1, b1, wgt1, wut1, wd1, "ffn1_bwd")
    g1a, gotm_a = wgrad(da1, h1, "ffn1_dwg", carry=exchange_between_chips(chipm[:1]))
    g1b, gotm_b = wgrad(db1, h1, "ffn1_dwu", carry=exchange_between_chips(chipm[1:]))
    g1c = wgrad(s1, g01, "ffn1_dwd")[0]
    core1 = [_by_core(a) for a in (g1a, g1b, g1c)]
    chip1 = pair_sum(my_core, core1, run_exchange(swap_halves(core1), "ffn1_grads_swap"), "ffn1_grads_pair_sum")
    got1 = run_exchange(exchange_between_chips(chip1), "ffn1_grads_exchange")

    gs2 = chip_sum(my_chip, chip2, got2, "ffn2_grads_sum")
    gsm = chip_sum(my_chip, chipm, gotm_a + gotm_b, "mix_grads_sum")
    gs1 = chip_sum(my_chip, chip1, got1, "ffn1_grads_sum")
    grads = {"ffn1_w_gate": gs1[0].T, "ffn1_w_up": gs1[1].T, "ffn1_w_down": gs1[2],
             "ffn2_w_gate": gs2[0].T, "ffn2_w_up": gs2[1].T, "ffn2_w_down": gs2[2],
             "w_in": gsm[0].T, "conv_w_proj": gsm[1], "attn_w_o": gsm[2], "w_out": gsm[3]}

    def pack_small(d, taps, extra):
        rows = [_to_rows(d[n].reshape(-1), 0) for n in REPLICATED] + [taps, _to_rows(extra.reshape(-1), 0)]
        return _pad_rows(jnp.concatenate(rows, axis=0), 0)

    zero, no_taps = jnp.zeros((1, LANES), F32), jnp.zeros((K, D), F32)
    shares = all_gather(pack_small(small, g_dw, loss), "gather_small_grads")
    g_s, d_s, m_s, v_s = adamw_replicated(pack_small(w, no_taps, zero), shares, pack_small(m, no_taps, zero),
                                          pack_small(v, no_taps, zero), "adamw_replicated")
    delta, new_m, new_v = {}, {}, {}
    off = 0
    for n in REPLICATED:
        r = -(-w[n].shape[1] // PACK_COLS)
        grads[n], delta[n], new_m[n], new_v[n] = (_from_rows(a[off:off + r], w[n].shape) for a in (g_s, d_s, m_s, v_s))
        off += r
    shard_cols = w["conv_dw_w"].shape[1]
    grads["conv_dw_w"] = lax.dynamic_slice_in_dim(g_s[off:off + K], _dev_index(my_x, my_y, my_c) * shard_cols, shard_cols, axis=1)
    total_loss = g_s[off + K, 0]

    groups = (("ffn1_w_gate", "ffn1_w_up", "ffn1_w_down"), ("ffn2_w_gate", "ffn2_w_up", "ffn2_w_down"),
              ("w_in", "conv_dw_w", "conv_w_proj", "attn_w_o", "w_out"))
    for k, names in enumerate(groups):
        res = adamw([w[n] for n in names], [grads[n] for n in names], [m[n] for n in names], [v[n] for n in names],
                    "adamw_%d" % k)
        for n, (d, mm, vv) in zip(names, res):
            delta[n], new_m[n], new_v[n] = d, mm, vv

    out = [total_loss, grad_x[None]]
    for d in (grads, delta, new_m, new_v):
        out += [d[n].reshape(shapes[n]) for n in WEIGHT_ORDER]
    return tuple(out)
```

```python
import functools
from typing import Callable, NamedTuple

import jax
import jax.numpy as jnp
from jax import lax
from jax.experimental import pallas as pl
from jax.experimental.pallas import tpu as pltpu

F32, BF16 = jnp.float32, jnp.bfloat16

HEAD_DIM = 64
N_KV_HEADS = 4
WINDOW = 128
CONV_WIDTH = 31
ROPE_THETA = 10000.0
EPS = 1e-6
LN_EPS = 1e-5
NEG_INF = -1e30
ADAM_LR, ADAM_B1, ADAM_B2, ADAM_EPS, ADAM_WD, ADAM_STEP = 0.001, 0.9, 0.999, 1e-08, 0.01, 10

N_DEV = 8
LANES = 128
CONV_HALO = 32
ROW_TILE = 512
FFN_CHUNK = 256
VMEM_LIMIT = 56 * 2 ** 20
MESH = pl.DeviceIdType.MESH


def _params(*sem):
    return pltpu.CompilerParams(dimension_semantics=sem or None, vmem_limit_bytes=VMEM_LIMIT)


def _resident(shape):
    zeros = (0,) * len(shape)
    return pl.BlockSpec(shape, lambda *_: zeros, pipeline_mode=pl.Buffered(1))


def _rows(tm, n):
    return pl.BlockSpec((tm, n), lambda i: (i, 0))


def _acc_spec(shape):
    zeros = (0,) * len(shape)
    return pl.BlockSpec(shape, lambda *_: zeros)


_ANY = pl.BlockSpec(memory_space=pl.ANY)


class Carry(NamedTuple):
    ins: tuple
    out_shapes: tuple
    aliases: dict
    sems: tuple
    start: Callable
    finish: Callable


def _call(body, *, name, grid, in_specs, out_specs, out_shape, args, scratch_shapes=(), carry=None):
    n_in, n_out, n_scr = len(in_specs), len(out_specs), len(scratch_shapes)
    params = _params(*(("arbitrary",) * len(grid)))
    if carry is None:
        res = pl.pallas_call(body, name=name, grid=grid, in_specs=list(in_specs), out_specs=tuple(out_specs),
                             out_shape=tuple(out_shape), scratch_shapes=list(scratch_shapes), compiler_params=params)(*args)
        return tuple(res), ()
    c_in, c_out = len(carry.ins), len(carry.out_shapes)

    def wrapped(*refs):
        ins, c_ins = refs[:n_in], refs[n_in:n_in + c_in]
        p = n_in + c_in
        outs, c_outs = refs[p:p + n_out], refs[p + n_out:p + n_out + c_out]
        p += n_out + c_out
        scr, c_sems = refs[p:p + n_scr], refs[p + n_scr:]
        ids = [pl.program_id(d) for d in range(len(grid))]
        first = functools.reduce(jnp.logical_and, [i == 0 for i in ids])
        last = functools.reduce(jnp.logical_and, [i == n - 1 for i, n in zip(ids, grid)])

        @pl.when(first)
        def _():
            carry.start(c_ins, c_outs, c_sems)

        body(*ins, *outs, *scr)

        @pl.when(last)
        def _():
            carry.finish(c_ins, c_outs, c_sems)

    res = pl.pallas_call(
        wrapped, name=name, grid=grid, in_specs=list(in_specs) + [_ANY] * c_in, out_specs=tuple(out_specs) + (_ANY,) * c_out,
        out_shape=tuple(out_shape) + tuple(carry.out_shapes), scratch_shapes=list(scratch_shapes) + list(carry.sems),
        input_output_aliases={n_in + i: n_out + o for i, o in carry.aliases.items()}, compiler_params=params,
    )(*args, *carry.ins)
    return tuple(res[:n_out]), tuple(res[n_out:])


def _nt(a, b):
    return lax.dot_general(a, b, (((1,), (1,)), ((), ())), preferred_element_type=F32)


def _tn(a, b):
    return lax.dot_general(a, b, (((0,), (0,)), ((), ())), preferred_element_type=F32)


def _dot(a, b):
    return jnp.dot(a, b, preferred_element_type=F32)


def _sigmoid(x):
    return 1.0 / (1.0 + jnp.exp(-x))


def _rms_fwd(x, g):
    r = lax.rsqrt(jnp.mean(x * x, axis=-1, keepdims=True) + EPS)
    return (x * r) * g


def _rms_bwd(x, g, dy):
    r = lax.rsqrt(jnp.mean(x * x, axis=-1, keepdims=True) + EPS)
    xhat = x * r
    dyg = dy * g
    dx = r * (dyg - xhat * jnp.mean(dyg * xhat, axis=-1, keepdims=True))
    return dx, dy * xhat


def _rot_half(x):
    lane = lax.broadcasted_iota(jnp.int32, (x.shape[0], LANES), 1)
    first = (lane % HEAD_DIM) < (HEAD_DIM // 2)
    out = []
    for s in range(x.shape[1] // LANES):
        xs = x[:, LANES * s:LANES * (s + 1)]
        out.append(jnp.where(first, pltpu.roll(xs, LANES - HEAD_DIM // 2, 1), pltpu.roll(xs, HEAD_DIM // 2, 1)))
    return out[0] if len(out) == 1 else jnp.concatenate(out, axis=1)


def _tile_lanes(t, width):
    return t if width == LANES else jnp.concatenate([t] * (width // LANES), axis=1)


def _rope_fwd(x, cos, sin_signed):
    w = x.shape[1]
    return x * _tile_lanes(cos, w) + _rot_half(x) * _tile_lanes(sin_signed, w)


def _rope_bwd(dy, cos, sin_signed):
    w = dy.shape[1]
    return dy * _tile_lanes(cos, w) + _rot_half(dy * _tile_lanes(sin_signed, w))


def ffn_fwd(x, gain, wgt, wut, wd, name, carry=None):
    T, D = x.shape
    F = wgt.shape[0]
    tm = min(ROW_TILE, T)
    fc = FFN_CHUNK

    def body(x_ref, g_ref, wg_ref, wu_ref, wd_ref, xo_ref, h_ref, a_ref, b_ref, acc_ref):
        x = x_ref[...]
        h = _rms_fwd(x, g_ref[...]).astype(BF16)
        h_ref[...] = h
        for c in range(F // fc):
            cs = pl.ds(c * fc, fc)
            a = _nt(h, wg_ref[cs, :])
            b = _nt(h, wu_ref[cs, :])
            a_ref[:, cs] = a.astype(BF16)
            b_ref[:, cs] = b.astype(BF16)
            s = (a * _sigmoid(a) * b).astype(BF16)
            y = _dot(s, wd_ref[cs, :])
            if c == 0:
                acc_ref[...] = y
            else:
                acc_ref[...] += y
        xo_ref[...] = x + 0.5 * acc_ref[...]

    return _call(
        body, name=name, grid=(T // tm,),
        out_shape=(jax.ShapeDtypeStruct((T, D), F32), jax.ShapeDtypeStruct((T, D), BF16),
                   jax.ShapeDtypeStruct((T, F), BF16), jax.ShapeDtypeStruct((T, F), BF16)),
        in_specs=[_rows(tm, D), _resident((1, D)), _resident((F, D)), _resident((F, D)), _resident((F, D))],
        out_specs=(_rows(tm, D), _rows(tm, D), _rows(tm, F), _rows(tm, F)),
        scratch_shapes=[pltpu.VMEM((tm, D), F32)], args=(x, gain, wgt, wut, wd), carry=carry)


def ffn_bwd(dxo, x, gain, a, b, wg, wu, wd, name):
    T, D = x.shape
    F = wg.shape[0]
    tm = min(ROW_TILE // 2, T)
    fc = FFN_CHUNK

    def body(dxo_ref, x_ref, g_ref, a_ref, b_ref, wg_ref, wu_ref, wd_ref,
             dx_ref, da_ref, db_ref, s_ref, g0_ref, dg_ref, acc_ref):
        dxo = dxo_ref[...]
        g0 = (0.5 * dxo).astype(BF16)
        g0_ref[...] = g0
        for c in range(F // fc):
            cs = pl.ds(c * fc, fc)
            ds = _nt(g0, wd_ref[cs, :])
            a = a_ref[:, cs].astype(F32)
            bb = b_ref[:, cs].astype(F32)
            sa = _sigmoid(a)
            silu = a * sa
            da = (ds * bb * (sa * (1.0 + a * (1.0 - sa)))).astype(BF16)
            db = (ds * silu).astype(BF16)
            da_ref[:, cs] = da
            db_ref[:, cs] = db
            s_ref[:, cs] = (silu * bb).astype(BF16)
            dh = _dot(da, wg_ref[cs, :]) + _dot(db, wu_ref[cs, :])
            if c == 0:
                acc_ref[...] = dh
            else:
                acc_ref[...] += dh
        dx, dgt = _rms_bwd(x_ref[...], g_ref[...], acc_ref[...])
        dx_ref[...] = dxo + dx

        @pl.when(pl.program_id(0) == 0)
        def _():
            dg_ref[...] = jnp.zeros_like(dg_ref)
        dg_ref[...] += jnp.sum(dgt, axis=0, keepdims=True)

    return pl.pallas_call(
        body, name=name, grid=(T // tm,),
        out_shape=(jax.ShapeDtypeStruct((T, D), F32), jax.ShapeDtypeStruct((T, F), BF16),
                   jax.ShapeDtypeStruct((T, F), BF16), jax.ShapeDtypeStruct((T, F), BF16),
                   jax.ShapeDtypeStruct((T, D), BF16), jax.ShapeDtypeStruct((1, D), F32)),
        in_specs=[_rows(tm, D), _rows(tm, D), _resident((1, D)), _rows(tm, F), _rows(tm, F),
                  _resident((F, D)), _resident((F, D)), _resident((F, D))],
        out_specs=(_rows(tm, D), _rows(tm, F), _rows(tm, F), _rows(tm, F), _rows(tm, D), _acc_spec((1, D))),
        scratch_shapes=[pltpu.VMEM((tm, D), F32)],
        compiler_params=_params("arbitrary"),
    )(dxo, x, gain, a, b, wg, wu, wd)


def wgrad(a, b, name, carry=None):
    T, M = a.shape
    N = b.shape[1]
    tk = min(ROW_TILE, T)
    nk = T // tk

    def body(a_ref, b_ref, o_ref, acc_ref):
        k = pl.program_id(0)
        part = _tn(a_ref[...].astype(BF16), b_ref[...].astype(BF16))

        @pl.when(k == 0)
        def _():
            acc_ref[...] = part

        @pl.when(k > 0)
        def _():
            acc_ref[...] += part

        @pl.when(k == nk - 1)
        def _():
            o_ref[...] = acc_ref[...].astype(BF16)

    (out,), carried = _call(
        body, name=name, grid=(nk,), out_shape=(jax.ShapeDtypeStruct((M, N), BF16),),
        in_specs=[_rows(tk, M), _rows(tk, N)], out_specs=(_acc_spec((M, N)),),
        scratch_shapes=[pltpu.VMEM((M, N), F32)], args=(a, b), carry=carry)
    return out, carried


def mix_in_fwd(x, gain, wint, gate_b, cos, sin_signed, name, carry=None):
    T, D = x.shape
    KV = N_KV_HEADS * HEAD_DIM
    tm = min(ROW_TILE, T)
    o_ga, o_gb, o_q, o_gc, o_gt, o_k, o_v = 0, D, 2 * D, 3 * D, 4 * D, 5 * D, 5 * D + KV

    def body(x_ref, g_ref, w_ref, gb_ref, cos_ref, sin_ref,
             h_ref, ga_ref, gb_out_ref, u0_ref, q_ref, sgc_ref, sgt_ref, k_ref, v_ref):
        h = _rms_fwd(x_ref[...], g_ref[...]).astype(BF16)
        h_ref[...] = h
        cos, sin = cos_ref[...], sin_ref[...]
        ga = _nt(h, w_ref[pl.ds(o_ga, D), :])
        gb = _nt(h, w_ref[pl.ds(o_gb, D), :])
        ga_ref[...] = ga.astype(BF16)
        gb_out_ref[...] = gb.astype(BF16)
        u0_ref[...] = (ga * _sigmoid(gb)).astype(BF16)
        q = _nt(h, w_ref[pl.ds(o_q, D), :])
        q_ref[...] = _rope_fwd(q, cos, sin).astype(BF16)
        gc = _nt(h, w_ref[pl.ds(o_gc, D), :]) + gb_ref[:, pl.ds(0, D)]
        sgc_ref[...] = _sigmoid(gc).astype(BF16)
        gt = _nt(h, w_ref[pl.ds(o_gt, D), :]) + gb_ref[:, pl.ds(D, D)]
        sgt_ref[...] = _sigmoid(gt).astype(BF16)
        k = _nt(h, w_ref[pl.ds(o_k, KV), :])
        k_ref[...] = _rope_fwd(k, cos, sin).astype(BF16)
        v_ref[...] = _nt(h, w_ref[pl.ds(o_v, KV), :]).astype(BF16)

    big = jax.ShapeDtypeStruct((T, D), BF16)
    small = jax.ShapeDtypeStruct((T, KV), BF16)
    return _call(
        body, name=name, grid=(T // tm,),
        out_shape=(big, big, big, big, big, big, big, small, small),
        in_specs=[_rows(tm, D), _resident((1, D)), _resident(wint.shape), _resident((1, 2 * D)),
                  _rows(tm, LANES), _rows(tm, LANES)],
        out_specs=(_rows(tm, D),) * 7 + (_rows(tm, KV),) * 2,
        args=(x, gain, wint, gate_b, cos, sin_signed), carry=carry)


def _attn_masks(tile_is_first, rows):
    qi = lax.broadcasted_iota(jnp.int32, (rows, 2 * WINDOW), 0) % WINDOW
    c = lax.broadcasted_iota(jnp.int32, (rows, 2 * WINDOW), 1)
    base = (c > qi) & (c <= qi + WINDOW)
    first_key = jnp.where(tile_is_first, WINDOW, 0)
    return base, base & (c >= first_key)


def _kv_lane_head(rows, width):
    return lax.broadcasted_iota(jnp.int32, (rows, width), 1) // HEAD_DIM


def _block_diag(win):
    head = _kv_lane_head(*win.shape)
    zero = jnp.zeros_like(win)
    return jnp.concatenate([jnp.where(head == g, win, zero) for g in range(N_KV_HEADS)], axis=0)


def _diag_blocks_sum(bd, keys):
    head = _kv_lane_head(keys, bd.shape[1])
    out = jnp.zeros((keys, bd.shape[1]), F32)
    for g in range(N_KV_HEADS):
        out = jnp.where(head == g, bd[g * keys:(g + 1) * keys], out)
    return out


def _kv_windows(k_ref, kh_ref, v_ref, vh_ref, j):
    rows = pl.ds(j * WINDOW, WINDOW)
    if j == 0:
        kprev, vprev = kh_ref[...], vh_ref[...]
    else:
        prev = pl.ds((j - 1) * WINDOW, WINDOW)
        kprev, vprev = k_ref[prev, :], v_ref[prev, :]
    return jnp.concatenate([kprev, k_ref[rows, :]], axis=0), jnp.concatenate([vprev, v_ref[rows, :]], axis=0)


def _stack_slots(ref, j, group, KV):
    rows = pl.ds(j * WINDOW, WINDOW)
    return jnp.concatenate([ref[rows, pl.ds(KV * hh, KV)] for hh in range(group)], axis=0)


def _attn_exp(qs, kbd, sink_ref, mask):
    s = _nt(qs, kbd) * (HEAD_DIM ** -0.5)
    out = []
    for g in range(N_KV_HEADS):
        sg = jnp.where(mask, s[:, 2 * WINDOW * g:2 * WINDOW * (g + 1)], NEG_INF)
        sink = sink_ref[g]
        m = jnp.maximum(jnp.max(sg, axis=-1, keepdims=True), sink)
        out.append((jnp.exp(sg - m), jnp.exp(sink - m)))
    return out


def _spread_over_heads(cols, rows, KV):
    head = _kv_lane_head(rows, KV)
    out = jnp.zeros((rows, KV), F32)
    for g, col in enumerate(cols):
        out = jnp.where(head == g, col, out)
    return out


def _halo_rows_spec(tq, width, sub):
    return pl.BlockSpec((sub, width), lambda i: (jnp.maximum(i * (tq // sub) - 1, 0), 0))


def attn_fwd(q, k, v, sink_col, name, carry=None):
    T, D = q.shape
    KV = k.shape[1]
    group = D // KV
    tq = min(ROW_TILE, T)
    nsub = tq // WINDOW
    rows = group * WINDOW

    def body(q_ref, k_ref, kh_ref, v_ref, vh_ref, sink_ref, o_ref):
        base, first = _attn_masks(pl.program_id(0) == 0, rows)
        head = _kv_lane_head(N_KV_HEADS * 2 * WINDOW, KV)
        block = lax.broadcasted_iota(jnp.int32, head.shape, 0) // (2 * WINDOW)
        ones_bd = jnp.where(head == block, 1.0, 0.0).astype(BF16)
        for j in range(nsub):
            k_win, v_win = _kv_windows(k_ref, kh_ref, v_ref, vh_ref, j)
            parts = _attn_exp(_stack_slots(q_ref, j, group, KV), _block_diag(k_win), sink_ref, first if j == 0 else base)
            p = jnp.concatenate([pg.astype(BF16) for pg, _ in parts], axis=1)
            both = _dot(p, jnp.concatenate([_block_diag(v_win), ones_bd], axis=1))
            denom = both[:, KV:] + _spread_over_heads([es for _, es in parts], rows, KV)
            out = (both[:, :KV] / denom).astype(BF16)
            for hh in range(group):
                o_ref[pl.ds(j * WINDOW, WINDOW), pl.ds(KV * hh, KV)] = out[hh * WINDOW:(hh + 1) * WINDOW]

    (o,), carried = _call(
        body, name=name, grid=(T // tq,),
        out_shape=(jax.ShapeDtypeStruct((T, D), BF16),),
        in_specs=[_rows(tq, D), _rows(tq, KV), _halo_rows_spec(tq, KV, WINDOW), _rows(tq, KV),
                  _halo_rows_spec(tq, KV, WINDOW), _resident(sink_col.shape)],
        out_specs=(_rows(tq, D),), args=(q, k, k, v, v, sink_col), carry=carry)
    return o, carried


def attn_bwd(q, k, v, do, sink_col, cos, sin_signed, name):
    T, D = q.shape
    KV = k.shape[1]
    group = D // KV
    tq = min(ROW_TILE, T)
    nsub = tq // WINDOW
    nt = T // tq
    scale = HEAD_DIM ** -0.5

    def rev(i):
        return nt - 1 - i

    def body(q_ref, k_ref, kh_ref, v_ref, vh_ref, do_ref, sink_ref, cos_ref, sin_ref,
             dq_ref, dk_ref, dv_ref, dsink_ref, dq_acc, dk_acc, dv_acc, carry_k, carry_v):
        i = pl.program_id(0)

        @pl.when(i == 0)
        def _():
            carry_k[...] = jnp.zeros_like(carry_k)
            carry_v[...] = jnp.zeros_like(carry_v)
            dsink_ref[...] = jnp.zeros_like(dsink_ref)

        dk_acc[...] = jnp.zeros_like(dk_acc)
        dv_acc[...] = jnp.zeros_like(dv_acc)
        base, first = _attn_masks(rev(i) == 0, group * WINDOW)
        for j in range(nsub):
            k_win, v_win = _kv_windows(k_ref, kh_ref, v_ref, vh_ref, j)
            kbd, vbd = _block_diag(k_win), _block_diag(v_win)
            qs, dos = _stack_slots(q_ref, j, group, KV), _stack_slots(do_ref, j, group, KV)
            dp = _nt(dos, vbd)
            probs16, ds16 = [], []
            for g, (pg, es) in enumerate(_attn_exp(qs, kbd, sink_ref, first if j == 0 else base)):
                inv = 1.0 / (jnp.sum(pg, axis=-1, keepdims=True) + es)
                probs = pg * inv
                dpg = dp[:, 2 * WINDOW * g:2 * WINDOW * (g + 1)]
                delta = jnp.sum(probs * dpg, axis=-1, keepdims=True)
                probs16.append(probs.astype(BF16))
                ds16.append((probs * (dpg - delta) * scale).astype(BF16))
                dsk = -(es * inv * delta)
                for hh in range(group):
                    tot = jnp.sum(dsk[hh * WINDOW:(hh + 1) * WINDOW], axis=0, keepdims=True)
                    dsink_ref[pl.ds(group * g + hh, 1), :] += jnp.broadcast_to(tot, (1, LANES))
            ds = jnp.concatenate(ds16, axis=1)
            dqs = _dot(ds, kbd)
            for hh in range(group):
                dq_acc[pl.ds(j * WINDOW, WINDOW), pl.ds(KV * hh, KV)] = dqs[hh * WINDOW:(hh + 1) * WINDOW]
            keys = pl.ds(j * WINDOW, 2 * WINDOW)
            dk_acc[keys, :] += _diag_blocks_sum(_tn(ds, qs), 2 * WINDOW)
            dv_acc[keys, :] += _diag_blocks_sum(_tn(jnp.concatenate(probs16, axis=1), dos), 2 * WINDOW)

        tail = pl.ds(tq, WINDOW)
        dk_acc[tail, :] += carry_k[...]
        dv_acc[tail, :] += carry_v[...]
        carry_k[...] = dk_acc[pl.ds(0, WINDOW), :]
        carry_v[...] = dv_acc[pl.ds(0, WINDOW), :]
        cos, sin = cos_ref[...], sin_ref[...]
        dq_ref[...] = _rope_bwd(dq_acc[...], cos, sin).astype(BF16)
        dk_ref[...] = _rope_bwd(dk_acc[pl.ds(WINDOW, tq), :], cos, sin).astype(BF16)
        dv_ref[...] = dv_acc[pl.ds(WINDOW, tq), :].astype(BF16)

    def rrows(n):
        return pl.BlockSpec((tq, n), lambda i: (rev(i), 0))

    def rhalo(n):
        return pl.BlockSpec((WINDOW, n), lambda i: (jnp.maximum(rev(i) * nsub - 1, 0), 0))

    return pl.pallas_call(
        body, name=name, grid=(nt,),
        out_shape=(jax.ShapeDtypeStruct((T, D), BF16), jax.ShapeDtypeStruct((T, KV), BF16),
                   jax.ShapeDtypeStruct((T, KV), BF16), jax.ShapeDtypeStruct((D // HEAD_DIM, LANES), F32)),
        in_specs=[rrows(D), rrows(KV), rhalo(KV), rrows(KV), rhalo(KV), rrows(D), _resident(sink_col.shape),
                  rrows(LANES), rrows(LANES)],
        out_specs=(rrows(D), rrows(KV), rrows(KV), _acc_spec((D // HEAD_DIM, LANES))),
        scratch_shapes=[pltpu.VMEM((tq, D), F32), pltpu.VMEM((WINDOW + tq, KV), F32), pltpu.VMEM((WINDOW + tq, KV), F32),
                        pltpu.VMEM((WINDOW, KV), F32), pltpu.VMEM((WINDOW, KV), F32)],
        compiler_params=_params("arbitrary"),
    )(q, k, k, v, v, do, sink_col, cos, sin_signed)


def _ln_stats(u):
    mu = jnp.mean(u, axis=-1, keepdims=True)
    d = u - mu
    rstd = lax.rsqrt(jnp.mean(d * d, axis=-1, keepdims=True) + LN_EPS)
    return d * rstd, rstd


def mix_out_fwd(x, u0, o, sgc, sgt, dw_w, dw_b, ln_g, ln_b, wcp, wo, wout, name):
    T, D = x.shape
    tm = min(ROW_TILE, T)
    K = dw_w.shape[0]

    def body(x_ref, u_ref, uh_ref, o_ref, sgc_ref, sgt_ref, w_ref, b_ref, lg_ref, lb_ref, wcp_ref, wo_ref, wout_ref,
             x2_ref, u1_ref, co_ref, ao_ref, mg_ref, buf):
        keep = (pl.program_id(0) > 0).astype(F32)
        buf[pl.ds(0, CONV_HALO), :] = uh_ref[...].astype(F32) * keep
        buf[pl.ds(CONV_HALO, tm), :] = u_ref[...].astype(F32)
        acc = jnp.broadcast_to(b_ref[...], (tm, D))
        for k in range(K):
            acc = acc + buf[pl.ds(CONV_HALO - (K - 1) + k, tm), :] * w_ref[pl.ds(k, 1), :]
        u1_ref[...] = acc.astype(BF16)
        xhat, _ = _ln_stats(acc)
        u2 = xhat * lg_ref[...] + lb_ref[...]
        u3 = (u2 * _sigmoid(u2)).astype(BF16)
        co = _dot(u3, wcp_ref[...])
        ao = _dot(o_ref[...], wo_ref[...])
        co_ref[...] = co.astype(BF16)
        ao_ref[...] = ao.astype(BF16)
        merged = (sgc_ref[...].astype(F32) * co + sgt_ref[...].astype(F32) * ao).astype(BF16)
        mg_ref[...] = merged
        x2_ref[...] = x_ref[...] + _dot(merged, wout_ref[...])

    big = jax.ShapeDtypeStruct((T, D), BF16)
    vec = _resident((1, D))
    return pl.pallas_call(
        body, name=name, grid=(T // tm,),
        out_shape=(jax.ShapeDtypeStruct((T, D), F32), big, big, big, big),
        in_specs=[_rows(tm, D), _rows(tm, D), _halo_rows_spec(tm, D, CONV_HALO), _rows(tm, D), _rows(tm, D), _rows(tm, D),
                  _resident((K, D)), vec, vec, vec, _resident((D, D)), _resident((D, D)), _resident((D, D))],
        out_specs=(_rows(tm, D),) * 5,
        scratch_shapes=[pltpu.VMEM((CONV_HALO + tm, D), F32)],
        compiler_params=_params("arbitrary"),
    )(x, u0, u0, o, sgc, sgt, dw_w, dw_b, ln_g, ln_b, wcp, wo, wout)


def mix_out_bwd(dx2, u1, co, ao, sgc, sgt, ln_g, ln_b, wcp, wo, wout, name):
    T, D = dx2.shape
    tm = min(ROW_TILE, T)

    def body(dx_ref, u1_ref, co_ref, ao_ref, sgc_ref, sgt_ref, lg_ref, lb_ref, wcp_ref, wo_ref, wout_ref,
             dgc_ref, dgt_ref, do_ref, du1_ref, dco_ref, dao_ref, u3_ref, sums_ref):
        dm = _nt(dx_ref[...].astype(BF16), wout_ref[...])
        sgc, sgt = sgc_ref[...].astype(F32), sgt_ref[...].astype(F32)
        dco = (dm * sgc).astype(BF16)
        dao = (dm * sgt).astype(BF16)
        dgc = dm * co_ref[...].astype(F32) * sgc * (1.0 - sgc)
        dgt = dm * ao_ref[...].astype(F32) * sgt * (1.0 - sgt)
        dco_ref[...] = dco
        dao_ref[...] = dao
        dgc_ref[...] = dgc.astype(BF16)
        dgt_ref[...] = dgt.astype(BF16)
        do_ref[...] = _nt(dao, wo_ref[...]).astype(BF16)
        du3 = _nt(dco, wcp_ref[...])
        xhat, rstd = _ln_stats(u1_ref[...].astype(F32))
        g = lg_ref[...]
        u2 = xhat * g + lb_ref[...]
        su = _sigmoid(u2)
        u3_ref[...] = (u2 * su).astype(BF16)
        du2 = du3 * (su * (1.0 + u2 * (1.0 - su)))
        dxh = du2 * g
        du1 = rstd * (dxh - jnp.mean(dxh, axis=-1, keepdims=True) - xhat * jnp.mean(dxh * xhat, axis=-1, keepdims=True))
        du1_ref[...] = du1.astype(BF16)

        @pl.when(pl.program_id(0) == 0)
        def _():
            sums_ref[...] = jnp.zeros_like(sums_ref)
        for r, val in enumerate((dgc, dgt, du2 * xhat, du2, du1)):
            sums_ref[pl.ds(r, 1), :] += jnp.sum(val, axis=0, keepdims=True)

    big = jax.ShapeDtypeStruct((T, D), BF16)
    vec = _resident((1, D))
    return pl.pallas_call(
        body, name=name, grid=(T // tm,),
        out_shape=(big,) * 7 + (jax.ShapeDtypeStruct((8, D), F32),),
        in_specs=[_rows(tm, D)] * 6 + [vec, vec, _resident((D, D)), _resident((D, D)), _resident((D, D))],
        out_specs=(_rows(tm, D),) * 7 + (_acc_spec((8, D)),),
        compiler_params=_params("arbitrary"),
    )(dx2, u1, co, ao, sgc, sgt, ln_g, ln_b, wcp, wo, wout)


def conv_bwd(du1, u0, ga, gb, dw_w, name, carry=None):
    T, D = du1.shape
    tm = min(ROW_TILE, T)
    nt = T // tm
    K = dw_w.shape[0]
    per = tm // CONV_HALO

    def body(d_ref, dn_ref, u_ref, uh_ref, ga_ref, gb_ref, w_ref, dga_ref, dgb_ref, dw_ref, dbuf, ubuf):
        i = pl.program_id(0)
        dbuf[pl.ds(0, tm), :] = d_ref[...].astype(F32)
        dbuf[pl.ds(tm, CONV_HALO), :] = dn_ref[...].astype(F32) * (i < nt - 1).astype(F32)
        ubuf[pl.ds(0, CONV_HALO), :] = uh_ref[...].astype(F32) * (i > 0).astype(F32)
        ubuf[pl.ds(CONV_HALO, tm), :] = u_ref[...].astype(F32)

        @pl.when(i == 0)
        def _():
            dw_ref[...] = jnp.zeros_like(dw_ref)

        d = dbuf[pl.ds(0, tm), :]
        du0 = jnp.zeros((tm, D), F32)
        for k in range(K):
            du0 = du0 + dbuf[pl.ds(K - 1 - k, tm), :] * w_ref[pl.ds(k, 1), :]
            dw_ref[pl.ds(k, 1), :] += jnp.sum(d * ubuf[pl.ds(CONV_HALO - (K - 1) + k, tm), :], axis=0, keepdims=True)
        ga, gb = ga_ref[...].astype(F32), gb_ref[...].astype(F32)
        sg = _sigmoid(gb)
        dga_ref[...] = (du0 * sg).astype(BF16)
        dgb_ref[...] = (du0 * ga * sg * (1.0 - sg)).astype(BF16)

    nxt = pl.BlockSpec((CONV_HALO, D), lambda i: (jnp.minimum((i + 1) * per, nt * per - 1), 0))
    big = jax.ShapeDtypeStruct((T, D), BF16)
    return _call(
        body, name=name, grid=(nt,),
        out_shape=(big, big, jax.ShapeDtypeStruct((K, D), F32)),
        in_specs=[_rows(tm, D), nxt, _rows(tm, D), _halo_rows_spec(tm, D, CONV_HALO), _rows(tm, D), _rows(tm, D),
                  _resident((K, D))],
        out_specs=(_rows(tm, D), _rows(tm, D), _acc_spec((K, D))),
        scratch_shapes=[pltpu.VMEM((tm + CONV_HALO, D), F32), pltpu.VMEM((CONV_HALO + tm, D), F32)],
        args=(du1, du1, u0, u0, ga, gb, dw_w), carry=carry)


def mix_in_bwd(dx2, x, gain, win, pieces, name):
    T, D = x.shape
    tm = min(ROW_TILE, T)
    widths = [p.shape[1] for p in pieces]
    offs = [sum(widths[:n]) for n in range(len(widths))]

    def body(dx2_ref, x_ref, g_ref, w_ref, *rest):
        piece_refs, (dx_ref, dg_ref) = rest[:len(pieces)], rest[len(pieces):]
        dh = None
        for p_ref, off, w in zip(piece_refs, offs, widths):
            t = _dot(p_ref[...], w_ref[pl.ds(off, w), :])
            dh = t if dh is None else dh + t
        dx, dgt = _rms_bwd(x_ref[...], g_ref[...], dh)
        dx_ref[...] = dx2_ref[...] + dx

        @pl.when(pl.program_id(0) == 0)
        def _():
            dg_ref[...] = jnp.zeros_like(dg_ref)
        dg_ref[...] += jnp.sum(dgt, axis=0, keepdims=True)

    return pl.pallas_call(
        body, name=name, grid=(T // tm,),
        out_shape=(jax.ShapeDtypeStruct((T, D), F32), jax.ShapeDtypeStruct((1, D), F32)),
        in_specs=[_rows(tm, D), _rows(tm, D), _resident((1, D)), _resident(win.shape)] + [_rows(tm, w) for w in widths],
        out_specs=(_rows(tm, D), _acc_spec((1, D))),
        compiler_params=_params("arbitrary"),
    )(dx2, x, gain, win, *pieces)


def final_loss(x, gain, target, name):
    T, D = x.shape
    tm = min(ROW_TILE, T)

    def body(x_ref, g_ref, t_ref, loss_ref, dx_ref, dg_ref):
        x, g = x_ref[...], g_ref[...]
        err = _rms_fwd(x, g) - t_ref[...]
        dx, dgt = _rms_bwd(x, g, err * (1.0 / D))
        dx_ref[...] = dx

        @pl.when(pl.program_id(0) == 0)
        def _():
            dg_ref[...] = jnp.zeros_like(dg_ref)
            loss_ref[...] = jnp.zeros_like(loss_ref)
        dg_ref[...] += jnp.sum(dgt, axis=0, keepdims=True)
        per_token = jnp.sum(err * err, axis=-1, keepdims=True) * (0.5 / D)
        loss_ref[...] += jnp.broadcast_to(jnp.sum(per_token, axis=0, keepdims=True), (1, LANES))

    return pl.pallas_call(
        body, name=name, grid=(T // tm,),
        out_shape=(jax.ShapeDtypeStruct((1, LANES), F32), jax.ShapeDtypeStruct((T, D), F32),
                   jax.ShapeDtypeStruct((1, D), F32)),
        in_specs=[_rows(tm, D), _resident((1, D)), _rows(tm, D)],
        out_specs=(_acc_spec((1, LANES)), _rows(tm, D), _acc_spec((1, D))),
        compiler_params=_params("arbitrary"),
    )(x, gain, target)


def rope_tables(positions):
    half = HEAD_DIM // 2
    inv_freq = ROPE_THETA ** (-jnp.arange(half, dtype=F32) / half)
    ang = positions.astype(F32)[:, None] * inv_freq
    cos, sin = jnp.cos(ang), jnp.sin(ang)
    reps = LANES // HEAD_DIM
    return jnp.tile(jnp.concatenate([cos, cos], axis=-1), (1, reps)), jnp.tile(jnp.concatenate([-sin, sin], axis=-1), (1, reps))


def _place():
    return lax.axis_index("x"), lax.axis_index("y"), lax.axis_index("c")


def all_gather(block, name):
    def body(x_ref, out_ref, send_sems, recv_sems, local_sem):
        x, y, c = _place()
        me, sibling = (x, y, c), (x, y, 1 - c)
        chips = [(1 - x, y), (x, 1 - y), (1 - x, 1 - y)]

        def rows(px, py, pc):
            return out_ref.at[4 * px + 2 * py + pc]

        def copy(k, block_of, to, src=None):
            return pltpu.make_async_remote_copy(
                src_ref=rows(*block_of) if src is None else src, dst_ref=rows(*block_of),
                send_sem=send_sems.at[k], recv_sem=recv_sems.at[k], device_id=to, device_id_type=MESH)

        mine = pltpu.make_async_copy(x_ref, rows(*me), local_sem)
        mine.start()
        first = [copy(0, me, sibling, src=x_ref)]
        first += [copy(1 + j, me, (*chip, c), src=x_ref) for j, chip in enumerate(chips)]
        for cp in first:
            cp.start()
        passed = [copy(4 + j, (*chip, c), sibling) for j, chip in enumerate(chips)]
        for j, chip in enumerate(chips):
            copy(1 + j, (*chip, c), me).wait_recv()
            passed[j].start()
        copy(0, sibling, me).wait_recv()
        for j, chip in enumerate(chips):
            copy(4 + j, (*chip, 1 - c), me).wait_recv()
        for cp in first + passed:
            cp.wait_send()
        mine.wait()

    return pl.pallas_call(
        body, name=name, out_shape=jax.ShapeDtypeStruct((N_DEV,) + block.shape, block.dtype),
        in_specs=[_ANY], out_specs=_ANY,
        scratch_shapes=[pltpu.SemaphoreType.DMA((7,)), pltpu.SemaphoreType.DMA((7,)), pltpu.SemaphoreType.DMA],
    )(block)


def _chips_across(x, y):
    return [(1 - x, y), (x, 1 - y), (1 - x, 1 - y)]


def _dev_index(x, y, c):
    return 4 * x + 2 * y + c


def gather_send(block):
    def copies(in_refs, out_refs, sems):
        (x_ref,), (out_ref,), (send, recv, local) = in_refs, out_refs, sems
        x, y, c = _place()
        targets = [(x, y, 1 - c)] + [(*chip, c) for chip in _chips_across(x, y)]
        outgoing = [pltpu.make_async_remote_copy(src_ref=x_ref, dst_ref=out_ref.at[_dev_index(x, y, c)], send_sem=send.at[k],
                                                 recv_sem=recv.at[k], device_id=t, device_id_type=MESH)
                    for k, t in enumerate(targets)]
        incoming = [pltpu.make_async_remote_copy(src_ref=x_ref, dst_ref=out_ref.at[_dev_index(*t)], send_sem=send.at[k],
                                                 recv_sem=recv.at[k], device_id=t, device_id_type=MESH)
                    for k, t in enumerate(targets)]
        return outgoing, incoming, pltpu.make_async_copy(x_ref, out_ref.at[_dev_index(x, y, c)], local)

    def start(*refs):
        outgoing, _, mine = copies(*refs)
        mine.start()
        for cp in outgoing:
            cp.start()

    def finish(*refs):
        outgoing, incoming, mine = copies(*refs)
        for cp in incoming:
            cp.wait_recv()
        for cp in outgoing:
            cp.wait_send()
        mine.wait()

    return Carry(ins=(block,), out_shapes=(jax.ShapeDtypeStruct((N_DEV,) + block.shape, block.dtype),), aliases={},
                 sems=(pltpu.SemaphoreType.DMA((4,)), pltpu.SemaphoreType.DMA((4,)), pltpu.SemaphoreType.DMA),
                 start=start, finish=finish)


def gather_forward(gathered):
    def copies(in_refs, out_refs, sems):
        (buf,), (send, recv) = out_refs, sems
        x, y, c = _place()
        outgoing, incoming = [], []
        for k, chip in enumerate(_chips_across(x, y)):
            rows = buf.at[_dev_index(*chip, c)]
            outgoing.append(pltpu.make_async_remote_copy(src_ref=rows, dst_ref=rows, send_sem=send.at[k], recv_sem=recv.at[k],
                                                         device_id=(x, y, 1 - c), device_id_type=MESH))
            theirs = buf.at[_dev_index(*chip, 1 - c)]
            incoming.append(pltpu.make_async_remote_copy(src_ref=theirs, dst_ref=theirs, send_sem=send.at[k],
                                                         recv_sem=recv.at[k], device_id=(x, y, 1 - c), device_id_type=MESH))
        return outgoing, incoming

    def start(*refs):
        for cp in copies(*refs)[0]:
            cp.start()

    def finish(*refs):
        outgoing, incoming = copies(*refs)
        for cp in incoming:
            cp.wait_recv()
        for cp in outgoing:
            cp.wait_send()

    return Carry(ins=(gathered,), out_shapes=(jax.ShapeDtypeStruct(gathered.shape, gathered.dtype),), aliases={0: 0},
                 sems=(pltpu.SemaphoreType.DMA((3,)), pltpu.SemaphoreType.DMA((3,))), start=start, finish=finish)


def swap_halves(by_core):
    n = len(by_core)

    def copies(in_refs, out_refs, sems):
        send, recv = sems
        x, y, c = _place()
        return [pltpu.make_async_remote_copy(src_ref=a.at[:, 1 - c], dst_ref=r, send_sem=send.at[i], recv_sem=recv.at[i],
                                             device_id=(x, y, 1 - c), device_id_type=MESH)
                for i, (a, r) in enumerate(zip(in_refs, out_refs))]

    def start(*refs):
        for cp in copies(*refs):
            cp.start()

    def finish(*refs):
        for cp in copies(*refs):
            cp.wait()

    shapes = tuple(jax.ShapeDtypeStruct((a.shape[0],) + a.shape[2:], a.dtype) for a in by_core)
    return Carry(ins=tuple(by_core), out_shapes=shapes, aliases={},
                 sems=(pltpu.SemaphoreType.DMA((n,)), pltpu.SemaphoreType.DMA((n,))), start=start, finish=finish)


def exchange_between_chips(by_chip):
    n = len(by_chip)

    def copies(in_refs, out_refs, sems):
        send, recv = sems
        x, y, c = _place()
        out = []
        for i, (s, r) in enumerate(zip(in_refs, out_refs)):
            for k, (tx, ty) in enumerate(_chips_across(x, y)):
                out.append(pltpu.make_async_remote_copy(
                    src_ref=s.at[2 * tx + ty], dst_ref=r.at[k], send_sem=send.at[3 * i + k], recv_sem=recv.at[3 * i + k],
                    device_id=(tx, ty, c), device_id_type=MESH))
        return out

    def start(*refs):
        for cp in copies(*refs):
            cp.start()

    def finish(*refs):
        for cp in copies(*refs):
            cp.wait()

    shapes = tuple(jax.ShapeDtypeStruct((3,) + a.shape[1:], a.dtype) for a in by_chip)
    return Carry(ins=tuple(by_chip), out_shapes=shapes, aliases={},
                 sems=(pltpu.SemaphoreType.DMA((3 * n,)), pltpu.SemaphoreType.DMA((3 * n,))), start=start, finish=finish)


def run_exchange(carry, name):
    n_in = len(carry.ins)
    n_out = len(carry.out_shapes)

    def body(*refs):
        parts = refs[:n_in], refs[n_in:n_in + n_out], refs[n_in + n_out:]
        carry.start(*parts)
        carry.finish(*parts)

    return pl.pallas_call(
        body, name=name, out_shape=tuple(carry.out_shapes), in_specs=[_ANY] * n_in, out_specs=(_ANY,) * n_out,
        scratch_shapes=list(carry.sems), input_output_aliases=dict(carry.aliases),
    )(*carry.ins)


def pair_sum(my_core, by_core, received, name):
    n = len(by_core)

    def body(core_ref, *refs):
        for a_ref, b_ref, o_ref in zip(refs[:n], refs[n:2 * n], refs[2 * n:]):
            o_ref[0] = (a_ref[0, 0].astype(F32) + b_ref[0].astype(F32)).astype(BF16)

    mine = [pl.BlockSpec((1, 1) + a.shape[2:], lambda j, core: (j, core[0], 0, 0)) for a in by_core]
    theirs = [pl.BlockSpec((1,) + r.shape[1:], lambda j, core: (j, 0, 0)) for r in received]
    return pl.pallas_call(
        body, name=name, out_shape=tuple(jax.ShapeDtypeStruct(r.shape, BF16) for r in received),
        grid_spec=pltpu.PrefetchScalarGridSpec(num_scalar_prefetch=1, grid=(by_core[0].shape[0],), in_specs=mine + theirs,
                                               out_specs=tuple(theirs)),
        compiler_params=_params("arbitrary"),
    )(my_core, *by_core, *received)


def chip_sum(my_chip, by_chip, received, name):
    n = len(by_chip)

    def body(chip_ref, *refs):
        for a_ref, b_ref, o_ref in zip(refs[:n], refs[n:2 * n], refs[2 * n:]):
            acc = a_ref[0].astype(F32)
            for k in range(3):
                acc = acc + b_ref[k].astype(F32)
            o_ref[...] = acc

    own = [pl.BlockSpec((1,) + a.shape[1:], lambda i, chip: (chip[0], 0, 0)) for a in by_chip]
    got = [pl.BlockSpec(r.shape, lambda i, chip: (0, 0, 0)) for r in received]
    outs = tuple(pl.BlockSpec(a.shape[1:], lambda i, chip: (0, 0)) for a in by_chip)
    return pl.pallas_call(
        body, name=name, out_shape=tuple(jax.ShapeDtypeStruct(a.shape[1:], F32) for a in by_chip),
        grid_spec=pltpu.PrefetchScalarGridSpec(num_scalar_prefetch=1, grid=(1,), in_specs=own + got, out_specs=outs),
        compiler_params=_params("arbitrary"),
    )(my_chip, *by_chip, *received)


def _adamw_math(w, g, m, v):
    m = ADAM_B1 * m + (1.0 - ADAM_B1) * g
    v = ADAM_B2 * v + (1.0 - ADAM_B2) * (g * g)
    m_hat = m / (1.0 - ADAM_B1 ** ADAM_STEP)
    v_hat = v / (1.0 - ADAM_B2 ** ADAM_STEP)
    delta = -ADAM_LR * (m_hat / (jnp.sqrt(v_hat) + ADAM_EPS) + ADAM_WD * w)
    return delta, m, v


def adamw(ws, gs, ms, vs, name):
    n = len(ws)

    def body(*refs):
        ins, outs = refs[:4 * n], refs[4 * n:]
        for t in range(n):
            delta, m, v = _adamw_math(ins[t][...], ins[n + t][...], ins[2 * n + t][...], ins[3 * n + t][...])
            outs[3 * t][...] = delta
            outs[3 * t + 1][...] = m
            outs[3 * t + 2][...] = v

    shapes = []
    for w in ws:
        shapes += [jax.ShapeDtypeStruct(w.shape, F32)] * 3
    res = pl.pallas_call(body, name=name, out_shape=tuple(shapes), compiler_params=_params())(*ws, *gs, *ms, *vs)
    return [tuple(res[3 * t:3 * t + 3]) for t in range(n)]


def adamw_replicated(w, partials, m, v, name):
    def body(w_ref, p_ref, m_ref, v_ref, g_ref, d_ref, mo_ref, vo_ref):
        g = p_ref[0]
        for k in range(1, N_DEV):
            g = g + p_ref[k]
        g_ref[...] = g
        d_ref[...], mo_ref[...], vo_ref[...] = _adamw_math(w_ref[...], g, m_ref[...], v_ref[...])

    shape = jax.ShapeDtypeStruct(w.shape, F32)
    return pl.pallas_call(body, name=name, out_shape=(shape,) * 4, compiler_params=_params())(w, partials, m, v)


PACK_COLS = 1024
PACK_ROW_ALIGN = 16

SHARDED = {"ffn1_w_gate": 1, "ffn1_w_up": 1, "ffn1_w_down": 0, "w_in": 1, "conv_dw_w": 1, "conv_w_proj": 0, "attn_w_o": 0,
           "w_out": 0, "ffn2_w_gate": 1, "ffn2_w_up": 1, "ffn2_w_down": 0}
REPLICATED = ("ffn1_norm", "mix_norm", "conv_dw_b", "conv_ln_g", "conv_ln_b", "ffn2_norm", "final_norm", "gate_b", "attn_sinks")
WEIGHT_ORDER = ("ffn1_norm", "ffn1_w_gate", "ffn1_w_up", "ffn1_w_down", "mix_norm", "w_in", "conv_dw_w", "conv_dw_b", "conv_ln_g",
                "conv_ln_b", "conv_w_proj", "attn_sinks", "attn_w_o", "gate_b", "w_out", "ffn2_norm", "ffn2_w_gate", "ffn2_w_up",
                "ffn2_w_down", "final_norm")


def _to_rows(flat, lead):
    n = flat.shape[-1]
    rows = -(-n // PACK_COLS)
    flat = jnp.pad(flat, [(0, 0)] * lead + [(0, rows * PACK_COLS - n)])
    return flat.reshape(flat.shape[:lead] + (rows, PACK_COLS))


def _pad_rows(a, axis):
    rows = a.shape[axis]
    pad = -rows % PACK_ROW_ALIGN
    widths = [(0, 0)] * a.ndim
    widths[axis] = (0, pad)
    return jnp.pad(a, widths)


def _from_rows(rows, shape):
    n = 1
    for s in shape:
        n *= s
    return rows.reshape(rows.shape[:-2] + (-1,))[..., :n].reshape(rows.shape[:-2] + tuple(shape))


def _pack_weights(parts):
    layout, off = [], 0
    for p in parts:
        layout.append((off, p.shape[0]))
        off += p.shape[0]
    return _pad_rows(jnp.concatenate(parts, axis=0), 0), layout


def _gathered_rows(gathered, off, rows):
    return gathered[:, off:off + rows].reshape(N_DEV * rows, gathered.shape[2])


def _heads_slot_major(rows):
    group = rows.shape[0] // (N_KV_HEADS * HEAD_DIM)
    return rows.reshape(N_KV_HEADS, group, HEAD_DIM, rows.shape[1]).transpose(1, 0, 2, 3).reshape(rows.shape)


def _heads_kv_major(rows):
    group = rows.shape[0] // (N_KV_HEADS * HEAD_DIM)
    return rows.reshape(group, N_KV_HEADS, HEAD_DIM, rows.shape[1]).transpose(1, 0, 2, 3).reshape(rows.shape)


def _by_core(full_rows):
    return full_rows.reshape((N_DEV // 2, 2, full_rows.shape[0] // N_DEV, full_rows.shape[1]))


def kernel(x, positions, ffn1_norm, ffn1_w_gate, ffn1_w_up, ffn1_w_down, mix_norm, w_in, conv_dw_w, conv_dw_b, conv_ln_g, conv_ln_b, conv_w_proj, attn_sinks, attn_w_o, gate_b, w_out, ffn2_norm, ffn2_w_gate, ffn2_w_up, ffn2_w_down, final_norm, loss_target, m_ffn1_norm, m_ffn1_w_gate, m_ffn1_w_up, m_ffn1_w_down, m_mix_norm, m_w_in, m_conv_dw_w, m_conv_dw_b, m_conv_ln_g, m_conv_ln_b, m_conv_w_proj, m_attn_sinks, m_attn_w_o, m_gate_b, m_w_out, m_ffn2_norm, m_ffn2_w_gate, m_ffn2_w_up, m_ffn2_w_down, m_final_norm, v_ffn1_norm, v_ffn1_w_gate, v_ffn1_w_up, v_ffn1_w_down, v_mix_norm, v_w_in, v_conv_dw_w, v_conv_dw_b, v_conv_ln_g, v_conv_ln_b, v_conv_w_proj, v_attn_sinks, v_attn_w_o, v_gate_b, v_w_out, v_ffn2_norm, v_ffn2_w_gate, v_ffn2_w_up, v_ffn2_w_down, v_final_norm):
    given = dict(locals())
    shapes = {n: given[n].shape for n in WEIGHT_ORDER}
    w = {n: given[n].reshape(given[n].shape[-2:]) if given[n].ndim == 3 else given[n].reshape(1, -1) for n in WEIGHT_ORDER}
    m = {n: given["m_" + n].reshape(w[n].shape) for n in WEIGHT_ORDER}
    v = {n: given["v_" + n].reshape(w[n].shape) for n in WEIGHT_ORDER}
    my_x, my_y, my_c = _place()
    my_core = my_c.astype(jnp.int32).reshape(1)
    my_chip = (2 * my_x + my_y).astype(jnp.int32).reshape(1)
    xs, target = x[0], loss_target[0]
    T, D = xs.shape
    KV = N_KV_HEADS * HEAD_DIM
    K = w["conv_dw_w"].shape[0]

    def t16(n):
        return w[n].T.astype(BF16)

    def r16(n):
        return w[n].astype(BF16)

    pack1, lay1 = _pack_weights([t16("ffn1_w_gate"), t16("ffn1_w_up"), r16("ffn1_w_down")])
    dw_bits = _to_rows(lax.bitcast_convert_type(w["conv_dw_w"], BF16).reshape(-1), 0)
    pack2, lay2 = _pack_weights([t16("w_in"), r16("conv_w_proj"), r16("attn_w_o"), r16("w_out"), dw_bits])
    pack3, lay3 = _pack_weights([t16("ffn2_w_gate"), t16("ffn2_w_up"), r16("ffn2_w_down")])
    cos, sin = rope_tables(positions[0])
    sink_col = jnp.repeat(w["attn_sinks"].reshape(-1), WINDOW).reshape(N_KV_HEADS, (D // KV) * WINDOW, 1)

    gath1 = all_gather(pack1, "gather_ffn1")
    wgt1, wut1, wd1 = (_gathered_rows(gath1, *l) for l in lay1)
    (x1, h1, a1, b1), (gath2,) = ffn_fwd(xs, w["ffn1_norm"], wgt1, wut1, wd1, "ffn1_fwd", carry=gather_send(pack2))
    (gath2,) = run_exchange(gather_forward(gath2), "gather_mix_forward")
    wint = _gathered_rows(gath2, *lay2[0])
    wint = jnp.concatenate([wint[:2 * D], _heads_slot_major(wint[2 * D:3 * D]), wint[3 * D + 2 * KV:],
                            wint[3 * D:3 * D + 2 * KV]], axis=0)
    wcp, wo, wout = (_gathered_rows(gath2, *l) for l in lay2[1:4])
    wo = _heads_slot_major(wo)
    o_dw, r_dw = lay2[4]
    dw_full = lax.bitcast_convert_type(_from_rows(gath2[:, o_dw:o_dw + r_dw], w["conv_dw_w"].shape + (2,)), F32)
    dw_full = dw_full.transpose(1, 0, 2).reshape(K, D)
    (h2, ga, gb, u0, q, sgc, sgt, kk, vv), (gath3,) = mix_in_fwd(x1, w["mix_norm"], wint, w["gate_b"], cos, sin, "mix_in_fwd",
                                                                 carry=gather_send(pack3))
    o, (gath3,) = attn_fwd(q, kk, vv, sink_col, "attn_fwd", carry=gather_forward(gath3))
    x2, u1, co, ao, merged = mix_out_fwd(x1, u0, o, sgc, sgt, dw_full, w["conv_dw_b"], w["conv_ln_g"], w["conv_ln_b"],
                                         wcp, wo, wout, "mix_out_fwd")
    wgt2, wut2, wd2 = (_gathered_rows(gath3, *l) for l in lay3)
    (x3, h3, a2, b2), _ = ffn_fwd(x2, w["ffn2_norm"], wgt2, wut2, wd2, "ffn2_fwd")
    loss, dx3, d_final = final_loss(x3, w["final_norm"], target, "final_loss")

    small = {"final_norm": d_final}
    dx2, da2, db2, s2, g02, small["ffn2_norm"] = ffn_bwd(dx3, x2, w["ffn2_norm"], a2, b2, wgt2, wut2, wd2, "ffn2_bwd")
    core2 = [_by_core(wgrad(a, b, nm)[0]) for a, b, nm in ((da2, h3, "ffn2_dwg"), (db2, h3, "ffn2_dwu"), (s2, g02, "ffn2_dwd"))]
    chip2 = pair_sum(my_core, core2, run_exchange(swap_halves(core2), "ffn2_grads_swap"), "ffn2_grads_pair_sum")

    dgc, dgt, do, du1, dco, dao, u3, sums = mix_out_bwd(dx2, u1, co, ao, sgc, sgt, w["conv_ln_g"], w["conv_ln_b"],
                                                        wcp, wo, wout, "mix_out_bwd")
    small["gate_b"] = jnp.concatenate([sums[0:1], sums[1:2]], axis=1)
    small["conv_ln_g"], small["conv_ln_b"], small["conv_dw_b"] = sums[2:3], sums[3:4], sums[4:5]
    g_wout = wgrad(merged, dx2, "dw_out")[0]
    g_wcp = wgrad(u3, dco, "dw_conv_proj")[0]
    g_wo = _heads_kv_major(wgrad(o, dao, "dw_attn_o")[0])
    (dga, dgb, g_dw), got2 = conv_bwd(du1, u0, ga, gb, dw_full, "conv_bwd", carry=exchange_between_chips(chip2))
    dq, dk, dv, dsink = attn_bwd(q, kk, vv, do, sink_col, cos, sin, "attn_bwd")
    small["attn_sinks"] = dsink[:, 0].reshape(1, -1)
    pieces = [dga, dgb, dq, dgc, dgt, dk, dv]
    dx1, small["mix_norm"] = mix_in_bwd(dx2, x1, w["mix_norm"], wint, pieces, "mix_in_bwd")
    dwin = [wgrad(p, h2, "dw_in_%d" % n)[0] for n, p in enumerate(pieces)]
    g_wint = jnp.concatenate([dwin[0], dwin[1], _heads_kv_major(dwin[2]), dwin[5], dwin[6], dwin[3], dwin[4]],
                             axis=0)
    corem = [_by_core(a) for a in (g_wint, g_wcp, g_wo, g_wout)]
    chipm = pair_sum(my_core, corem, run_exchange(swap_halves(corem), "mix_grads_swap"), "mix_grads_pair_sum")

    grad_x, da1, db1, s1, g01, small["ffn1_norm"] = ffn_bwd(dx1, xs, w["ffn1_norm"], a1, b1, wgt1, wut1, wd1, "ffn1_bwd")
    g1a, gotm_a = wgrad(da1, h1, "ffn1_dwg", carry=exchange_between_chips(chipm[:1]))
    g1b, gotm_b = wgrad(db1, h1, "ffn1_dwu", carry=exchange_between_chips(chipm[1:]))
    g1c = wgrad(s1, g01, "ffn1_dwd")[0]
    core1 = [_by_core(a) for a in (g1a, g1b, g1c)]
    chip1 = pair_sum(my_core, core1, run_exchange(swap_halves(core1), "ffn1_grads_swap"), "ffn1_grads_pair_sum")
    got1 = run_exchange(exchange_between_chips(chip1), "ffn1_grads_exchange")

    gs2 = chip_sum(my_chip, chip2, got2, "ffn2_grads_sum")
    gsm = chip_sum(my_chip, chipm, gotm_a + gotm_b, "mix_grads_sum")
    gs1 = chip_sum(my_chip, chip1, got1, "ffn1_grads_sum")
    grads = {"ffn1_w_gate": gs1[0].T, "ffn1_w_up": gs1[1].T, "ffn1_w_down": gs1[2],
             "ffn2_w_gate": gs2[0].T, "ffn2_w_up": gs2[1].T, "ffn2_w_down": gs2[2],
             "w_in": gsm[0].T, "conv_w_proj": gsm[1], "attn_w_o": gsm[2], "w_out": gsm[3]}

    def pack_small(d, taps, extra):
        rows = [_to_rows(d[n].reshape(-1), 0) for n in REPLICATED] + [taps, _to_rows(extra.reshape(-1), 0)]
        return _pad_rows(jnp.concatenate(rows, axis=0), 0)

    zero, no_taps = jnp.zeros((1, LANES), F32), jnp.zeros((K, D), F32)
    shares = all_gather(pack_small(small, g_dw, loss), "gather_small_grads")
    g_s, d_s, m_s, v_s = adamw_replicated(pack_small(w, no_taps, zero), shares, pack_small(m, no_taps, zero),
                                          pack_small(v, no_taps, zero), "adamw_replicated")
    delta, new_m, new_v = {}, {}, {}
    off = 0
    for n in REPLICATED:
        r = -(-w[n].shape[1] // PACK_COLS)
        grads[n], delta[n], new_m[n], new_v[n] = (_from_rows(a[off:off + r], w[n].shape) for a in (g_s, d_s, m_s, v_s))
        off += r
    shard_cols = w["conv_dw_w"].shape[1]
    grads["conv_dw_w"] = lax.dynamic_slice_in_dim(g_s[off:off + K], _dev_index(my_x, my_y, my_c) * shard_cols, shard_cols, axis=1)
    total_loss = g_s[off + K, 0]

    groups = (("ffn1_w_gate", "ffn1_w_up", "ffn1_w_down"), ("ffn2_w_gate", "ffn2_w_up", "ffn2_w_down"),
              ("w_in", "conv_dw_w", "conv_w_proj", "attn_w_o", "w_out"))
    for k, names in enumerate(groups):
        res = adamw([w[n] for n in names], [grads[n] for n in names], [m[n] for n in names], [v[n] for n in names],
                    "adamw_%d" % k)
        for n, (d, mm, vv) in zip(names, res):
            delta[n], new_m[n], new_v[n] = d, mm, vv

    out = [total_loss, grad_x[None]]
    for d in (grads, delta, new_m, new_v):
        out += [d[n].reshape(shapes[n]) for n in WEIGHT_ORDER]
    return tuple(out)
```

```python
import functools
from typing import Callable, NamedTuple

import jax
import jax.numpy as jnp
from jax import lax
from jax.experimental import pallas as pl
from jax.experimental.pallas import tpu as pltpu

F32, BF16 = jnp.float32, jnp.bfloat16

HEAD_DIM = 64
N_KV_HEADS = 4
WINDOW = 128
CONV_WIDTH = 31
ROPE_THETA = 10000.0
EPS = 1e-6
LN_EPS = 1e-5
NEG_INF = -1e30
ADAM_LR, ADAM_B1, ADAM_B2, ADAM_EPS, ADAM_WD, ADAM_STEP = 0.001, 0.9, 0.999, 1e-08, 0.01, 10

N_DEV = 8
LANES = 128
SUBLANES = 8
CONV_HALO = 32
CONV_ROWS, CONV_LANES = 64, 256
ROW_TILE = 512
FFN_CHUNK = 256
WGRAD_TILE_ELEMS = 2 ** 22
WGRAD_TILE_ROWS = 2048
VMEM_LIMIT = 56 * 2 ** 20
MESH = pl.DeviceIdType.MESH


def _params(*sem):
    return pltpu.CompilerParams(dimension_semantics=sem or None, vmem_limit_bytes=VMEM_LIMIT)


def _resident(shape):
    zeros = (0,) * len(shape)
    return pl.BlockSpec(shape, lambda *_: zeros, pipeline_mode=pl.Buffered(1))


def _rows(tm, n):
    return pl.BlockSpec((tm, n), lambda i: (i, 0))


def _acc_spec(shape):
    zeros = (0,) * len(shape)
    return pl.BlockSpec(shape, lambda *_: zeros)


_ANY = pl.BlockSpec(memory_space=pl.ANY)


class Carry(NamedTuple):
    ins: tuple
    out_shapes: tuple
    aliases: dict
    sems: tuple
    start: Callable
    finish: Callable


def _call(body, *, name, grid, in_specs, out_specs, out_shape, args, scratch_shapes=(), carry=None):
    n_in, n_out, n_scr = len(in_specs), len(out_specs), len(scratch_shapes)
    params = _params(*(("arbitrary",) * len(grid)))
    if carry is None:
        res = pl.pallas_call(body, name=name, grid=grid, in_specs=list(in_specs), out_specs=tuple(out_specs),
                             out_shape=tuple(out_shape), scratch_shapes=list(scratch_shapes), compiler_params=params)(*args)
        return tuple(res), ()
    c_in, c_out = len(carry.ins), len(carry.out_shapes)

    def wrapped(*refs):
        ins, c_ins = refs[:n_in], refs[n_in:n_in + c_in]
        p = n_in + c_in
        outs, c_outs = refs[p:p + n_out], refs[p + n_out:p + n_out + c_out]
        p += n_out + c_out
        scr, c_sems = refs[p:p + n_scr], refs[p + n_scr:]
        ids = [pl.program_id(d) for d in range(len(grid))]
        first = functools.reduce(jnp.logical_and, [i == 0 for i in ids])
        last = functools.reduce(jnp.logical_and, [i == n - 1 for i, n in zip(ids, grid)])

        @pl.when(first)
        def _():
            carry.start(c_ins, c_outs, c_sems)

        body(*ins, *outs, *scr)

        @pl.when(last)
        def _():
            carry.finish(c_ins, c_outs, c_sems)

    res = pl.pallas_call(
        wrapped, name=name, grid=grid, in_specs=list(in_specs) + [_ANY] * c_in, out_specs=tuple(out_specs) + (_ANY,) * c_out,
        out_shape=tuple(out_shape) + tuple(carry.out_shapes), scratch_shapes=list(scratch_shapes) + list(carry.sems),
        input_output_aliases={n_in + i: n_out + o for i, o in carry.aliases.items()}, compiler_params=params,
    )(*args, *carry.ins)
    return tuple(res[:n_out]), tuple(res[n_out:])


def _nt(a, b):
    return lax.dot_general(a, b, (((1,), (1,)), ((), ())), preferred_element_type=F32)


def _tn(a, b):
    return lax.dot_general(a, b, (((0,), (0,)), ((), ())), preferred_element_type=F32)


def _dot(a, b):
    return jnp.dot(a, b, preferred_element_type=F32)


def _sigmoid(x):
    return 1.0 / (1.0 + jnp.exp(-x))


def _rms_fwd(x, g):
    r = lax.rsqrt(jnp.mean(x * x, axis=-1, keepdims=True) + EPS)
    return (x * r) * g


def _rms_bwd(x, g, dy):
    r = lax.rsqrt(jnp.mean(x * x, axis=-1, keepdims=True) + EPS)
    xhat = x * r
    dyg = dy * g
    dx = r * (dyg - xhat * jnp.mean(dyg * xhat, axis=-1, keepdims=True))
    return dx, dy * xhat


def _rot_half(x):
    lane = lax.broadcasted_iota(jnp.int32, (x.shape[0], LANES), 1)
    first = (lane % HEAD_DIM) < (HEAD_DIM // 2)
    out = []
    for s in range(x.shape[1] // LANES):
        xs = x[:, LANES * s:LANES * (s + 1)]
        out.append(jnp.where(first, pltpu.roll(xs, LANES - HEAD_DIM // 2, 1), pltpu.roll(xs, HEAD_DIM // 2, 1)))
    return out[0] if len(out) == 1 else jnp.concatenate(out, axis=1)


def _tile_lanes(t, width):
    return t if width == LANES else jnp.concatenate([t] * (width // LANES), axis=1)


def _rope_fwd(x, cos, sin_signed):
    w = x.shape[1]
    return x * _tile_lanes(cos, w) + _rot_half(x) * _tile_lanes(sin_signed, w)


def _rope_bwd(dy, cos, sin_signed):
    w = dy.shape[1]
    return dy * _tile_lanes(cos, w) + _rot_half(dy * _tile_lanes(sin_signed, w))


def ffn_fwd(x, gain, wgt, wut, wd, name, carry=None):
    T, D = x.shape
    F = wgt.shape[0]
    tm = min(ROW_TILE, T)
    fc = FFN_CHUNK

    def body(x_ref, g_ref, wg_ref, wu_ref, wd_ref, xo_ref, h_ref, a_ref, b_ref, acc_ref):
        x = x_ref[...]
        h = _rms_fwd(x, g_ref[...]).astype(BF16)
        h_ref[...] = h
        for c in range(F // fc):
            cs = pl.ds(c * fc, fc)
            a = _nt(h, wg_ref[cs, :])
            b = _nt(h, wu_ref[cs, :])
            a_ref[:, cs] = a.astype(BF16)
            b_ref[:, cs] = b.astype(BF16)
            s = (a * _sigmoid(a) * b).astype(BF16)
            y = _dot(s, wd_ref[cs, :])
            if c == 0:
                acc_ref[...] = y
            else:
                acc_ref[...] += y
        xo_ref[...] = x + 0.5 * acc_ref[...]

    return _call(
        body, name=name, grid=(T // tm,),
        out_shape=(jax.ShapeDtypeStruct((T, D), F32), jax.ShapeDtypeStruct((T, D), BF16),
                   jax.ShapeDtypeStruct((T, F), BF16), jax.ShapeDtypeStruct((T, F), BF16)),
        in_specs=[_rows(tm, D), _resident((1, D)), _resident((F, D)), _resident((F, D)), _resident((F, D))],
        out_specs=(_rows(tm, D), _rows(tm, D), _rows(tm, F), _rows(tm, F)),
        scratch_shapes=[pltpu.VMEM((tm, D), F32)], args=(x, gain, wgt, wut, wd), carry=carry)


def ffn_bwd(dxo, x, gain, a, b, wg, wu, wd, name):
    T, D = x.shape
    F = wg.shape[0]
    tm = min(ROW_TILE, T)
    fc = FFN_CHUNK

    def hidden_body(dxo_ref, a_ref, b_ref, wd_ref, da_ref, db_ref, s_ref, g0_ref):
        g0 = (0.5 * dxo_ref[...]).astype(BF16)
        g0_ref[...] = g0
        for c in range(F // fc):
            cs = pl.ds(c * fc, fc)
            ds = _nt(g0, wd_ref[cs, :])
            a = a_ref[:, cs].astype(F32)
            bb = b_ref[:, cs].astype(F32)
            sa = _sigmoid(a)
            silu = a * sa
            da_ref[:, cs] = (ds * bb * (sa * (1.0 + a * (1.0 - sa)))).astype(BF16)
            db_ref[:, cs] = (ds * silu).astype(BF16)
            s_ref[:, cs] = (silu * bb).astype(BF16)

    wide = jax.ShapeDtypeStruct((T, F), BF16)
    da, db, s, g0 = pl.pallas_call(
        hidden_body, name=name + "_hidden", grid=(T // tm,),
        out_shape=(wide, wide, wide, jax.ShapeDtypeStruct((T, D), BF16)),
        in_specs=[_rows(tm, D), _rows(tm, F), _rows(tm, F), _resident((F, D))],
        out_specs=(_rows(tm, F), _rows(tm, F), _rows(tm, F), _rows(tm, D)),
        compiler_params=_params("arbitrary"),
    )(dxo, a, b, wd)

    def input_body(dxo_ref, x_ref, g_ref, da_ref, db_ref, wg_ref, wu_ref, dx_ref, dg_ref):
        dh = _dot(da_ref[...], wg_ref[...]) + _dot(db_ref[...], wu_ref[...])
        dx, dgt = _rms_bwd(x_ref[...], g_ref[...], dh)
        dx_ref[...] = dxo_ref[...] + dx

        @pl.when(pl.program_id(0) == 0)
        def _():
            dg_ref[...] = jnp.zeros_like(dg_ref)
        dg_ref[...] += jnp.sum(dgt, axis=0, keepdims=True)

    dx, dg = pl.pallas_call(
        input_body, name=name + "_input", grid=(T // tm,),
        out_shape=(jax.ShapeDtypeStruct((T, D), F32), jax.ShapeDtypeStruct((1, D), F32)),
        in_specs=[_rows(tm, D), _rows(tm, D), _resident((1, D)), _rows(tm, F), _rows(tm, F), _resident((F, D)),
                  _resident((F, D))],
        out_specs=(_rows(tm, D), _acc_spec((1, D))),
        compiler_params=_params("arbitrary"),
    )(dxo, x, gain, da, db, wg, wu)
    return dx, da, db, s, g0, dg


def wgrad(a, b, name, carry=None):
    T, M = a.shape
    N = b.shape[1]
    tk = ROW_TILE
    while 2 * tk * M <= WGRAD_TILE_ELEMS and 2 * tk <= WGRAD_TILE_ROWS:
        tk *= 2
    tk = min(tk, T)
    nk = T // tk

    def body(a_ref, b_ref, o_ref, acc_ref):
        k = pl.program_id(0)
        part = _tn(a_ref[...].astype(BF16), b_ref[...].astype(BF16))

        @pl.when(k == 0)
        def _():
            acc_ref[...] = part

        @pl.when(k > 0)
        def _():
            acc_ref[...] += part

        @pl.when(k == nk - 1)
        def _():
            o_ref[...] = acc_ref[...].astype(BF16)

    (out,), carried = _call(
        body, name=name, grid=(nk,), out_shape=(jax.ShapeDtypeStruct((M, N), BF16),),
        in_specs=[_rows(tk, M), _rows(tk, N)], out_specs=(_acc_spec((M, N)),),
        scratch_shapes=[pltpu.VMEM((M, N), F32)], args=(a, b), carry=carry)
    return out, carried


def mix_in_fwd(x, gain, wint, gate_b, cos, sin_signed, name, carry=None):
    T, D = x.shape
    KV = N_KV_HEADS * HEAD_DIM
    tm = min(ROW_TILE, T)
    o_ga, o_gb, o_q, o_gc, o_gt, o_k, o_v = 0, D, 2 * D, 3 * D, 4 * D, 5 * D, 5 * D + KV

    def body(x_ref, g_ref, w_ref, gb_ref, cos_ref, sin_ref,
             h_ref, ga_ref, gb_out_ref, u0_ref, q_ref, sgc_ref, sgt_ref, k_ref, v_ref):
        h = _rms_fwd(x_ref[...], g_ref[...]).astype(BF16)
        h_ref[...] = h
        cos, sin = cos_ref[...], sin_ref[...]
        ga = _nt(h, w_ref[pl.ds(o_ga, D), :])
        gb = _nt(h, w_ref[pl.ds(o_gb, D), :])
        ga_ref[...] = ga.astype(BF16)
        gb_out_ref[...] = gb.astype(BF16)
        u0_ref[...] = (ga * _sigmoid(gb)).astype(BF16)
        q = _nt(h, w_ref[pl.ds(o_q, D), :])
        q_ref[...] = _rope_fwd(q, cos, sin).astype(BF16)
        gc = _nt(h, w_ref[pl.ds(o_gc, D), :]) + gb_ref[:, pl.ds(0, D)]
        sgc_ref[...] = _sigmoid(gc).astype(BF16)
        gt = _nt(h, w_ref[pl.ds(o_gt, D), :]) + gb_ref[:, pl.ds(D, D)]
        sgt_ref[...] = _sigmoid(gt).astype(BF16)
        k = _nt(h, w_ref[pl.ds(o_k, KV), :])
        k_ref[...] = _rope_fwd(k, cos, sin).astype(BF16)
        v_ref[...] = _nt(h, w_ref[pl.ds(o_v, KV), :]).astype(BF16)

    big = jax.ShapeDtypeStruct((T, D), BF16)
    small = jax.ShapeDtypeStruct((T, KV), BF16)
    return _call(
        body, name=name, grid=(T // tm,),
        out_shape=(big, big, big, big, big, big, big, small, small),
        in_specs=[_rows(tm, D), _resident((1, D)), _resident(wint.shape), _resident((1, 2 * D)),
                  _rows(tm, LANES), _rows(tm, LANES)],
        out_specs=(_rows(tm, D),) * 7 + (_rows(tm, KV),) * 2,
        args=(x, gain, wint, gate_b, cos, sin_signed), carry=carry)


def _attn_masks(tile_is_first, rows):
    qi = lax.broadcasted_iota(jnp.int32, (rows, 2 * WINDOW), 0) % WINDOW
    c = lax.broadcasted_iota(jnp.int32, (rows, 2 * WINDOW), 1)
    base = (c > qi) & (c <= qi + WINDOW)
    first_key = jnp.where(tile_is_first, WINDOW, 0)
    return base, base & (c >= first_key)


def _kv_lane_head(rows, width):
    return lax.broadcasted_iota(jnp.int32, (rows, width), 1) // HEAD_DIM


def _block_diag(win):
    head = _kv_lane_head(*win.shape)
    zero = jnp.zeros_like(win)
    return jnp.concatenate([jnp.where(head == g, win, zero) for g in range(N_KV_HEADS)], axis=0)


def _diag_blocks_sum(bd, keys):
    head = _kv_lane_head(keys, bd.shape[1])
    out = jnp.zeros((keys, bd.shape[1]), F32)
    for g in range(N_KV_HEADS):
        out = jnp.where(head == g, bd[g * keys:(g + 1) * keys], out)
    return out


def _kv_windows(k_ref, kh_ref, v_ref, vh_ref, j):
    rows = pl.ds(j * WINDOW, WINDOW)
    if j == 0:
        kprev, vprev = kh_ref[...], vh_ref[...]
    else:
        prev = pl.ds((j - 1) * WINDOW, WINDOW)
        kprev, vprev = k_ref[prev, :], v_ref[prev, :]
    return jnp.concatenate([kprev, k_ref[rows, :]], axis=0), jnp.concatenate([vprev, v_ref[rows, :]], axis=0)


def _stack_slots(ref, j, group, KV):
    rows = pl.ds(j * WINDOW, WINDOW)
    return jnp.concatenate([ref[rows, pl.ds(KV * hh, KV)] for hh in range(group)], axis=0)


def _attn_exp(qs, kbd, sink_ref, mask):
    s = _nt(qs, kbd) * (HEAD_DIM ** -0.5)
    out = []
    for g in range(N_KV_HEADS):
        sg = jnp.where(mask, s[:, 2 * WINDOW * g:2 * WINDOW * (g + 1)], NEG_INF)
        sink = sink_ref[g]
        m = jnp.maximum(jnp.max(sg, axis=-1, keepdims=True), sink)
        out.append((jnp.exp(sg - m), jnp.exp(sink - m)))
    return out


def _spread_over_heads(cols, rows, KV):
    head = _kv_lane_head(rows, KV)
    out = jnp.zeros((rows, KV), F32)
    for g, col in enumerate(cols):
        out = jnp.where(head == g, col, out)
    return out


def _halo_rows_spec(tq, width, sub):
    return pl.BlockSpec((sub, width), lambda i: (jnp.maximum(i * (tq // sub) - 1, 0), 0))


def attn_fwd(q, k, v, sink_col, name, carry=None):
    T, D = q.shape
    KV = k.shape[1]
    group = D // KV
    tq = min(ROW_TILE, T)
    nsub = tq // WINDOW
    rows = group * WINDOW

    def body(q_ref, k_ref, kh_ref, v_ref, vh_ref, sink_ref, o_ref):
        base, first = _attn_masks(pl.program_id(0) == 0, rows)
        head = _kv_lane_head(N_KV_HEADS * 2 * WINDOW, KV)
        block = lax.broadcasted_iota(jnp.int32, head.shape, 0) // (2 * WINDOW)
        ones_bd = jnp.where(head == block, 1.0, 0.0).astype(BF16)
        for j in range(nsub):
            k_win, v_win = _kv_windows(k_ref, kh_ref, v_ref, vh_ref, j)
            parts = _attn_exp(_stack_slots(q_ref, j, group, KV), _block_diag(k_win), sink_ref, first if j == 0 else base)
            p = jnp.concatenate([pg.astype(BF16) for pg, _ in parts], axis=1)
            both = _dot(p, jnp.concatenate([_block_diag(v_win), ones_bd], axis=1))
            denom = both[:, KV:] + _spread_over_heads([es for _, es in parts], rows, KV)
            out = (both[:, :KV] / denom).astype(BF16)
            for hh in range(group):
                o_ref[pl.ds(j * WINDOW, WINDOW), pl.ds(KV * hh, KV)] = out[hh * WINDOW:(hh + 1) * WINDOW]

    (o,), carried = _call(
        body, name=name, grid=(T // tq,),
        out_shape=(jax.ShapeDtypeStruct((T, D), BF16),),
        in_specs=[_rows(tq, D), _rows(tq, KV), _halo_rows_spec(tq, KV, WINDOW), _rows(tq, KV),
                  _halo_rows_spec(tq, KV, WINDOW), _resident(sink_col.shape)],
        out_specs=(_rows(tq, D),), args=(q, k, k, v, v, sink_col), carry=carry)
    return o, carried


def attn_bwd(q, k, v, do, sink_col, cos, sin_signed, name):
    T, D = q.shape
    KV = k.shape[1]
    group = D // KV
    tq = min(ROW_TILE, T)
    nsub = tq // WINDOW
    nt = T // tq
    scale = HEAD_DIM ** -0.5

    def rev(i):
        return nt - 1 - i

    def body(q_ref, k_ref, kh_ref, v_ref, vh_ref, do_ref, sink_ref, cos_ref, sin_ref,
             dq_ref, dk_ref, dv_ref, dsink_ref, dq_acc, dk_acc, dv_acc, carry_k, carry_v):
        i = pl.program_id(0)

        @pl.when(i == 0)
        def _():
            carry_k[...] = jnp.zeros_like(carry_k)
            carry_v[...] = jnp.zeros_like(carry_v)
            dsink_ref[...] = jnp.zeros_like(dsink_ref)

        dk_acc[...] = jnp.zeros_like(dk_acc)
        dv_acc[...] = jnp.zeros_like(dv_acc)
        base, first = _attn_masks(rev(i) == 0, group * WINDOW)
        for j in range(nsub):
            k_win, v_win = _kv_windows(k_ref, kh_ref, v_ref, vh_ref, j)
            kbd, vbd = _block_diag(k_win), _block_diag(v_win)
            qs, dos = _stack_slots(q_ref, j, group, KV), _stack_slots(do_ref, j, group, KV)
            dp = _nt(dos, vbd)
            probs16, ds16 = [], []
            for g, (pg, es) in enumerate(_attn_exp(qs, kbd, sink_ref, first if j == 0 else base)):
                inv = 1.0 / (jnp.sum(pg, axis=-1, keepdims=True) + es)
                probs = pg * inv
                dpg = dp[:, 2 * WINDOW * g:2 * WINDOW * (g + 1)]
                delta = jnp.sum(probs * dpg, axis=-1, keepdims=True)
                probs16.append(probs.astype(BF16))
                ds16.append((probs * (dpg - delta) * scale).astype(BF16))
                dsk = -(es * inv * delta)
                for hh in range(group):
                    tot = jnp.sum(dsk[hh * WINDOW:(hh + 1) * WINDOW], axis=0, keepdims=True)
                    dsink_ref[pl.ds(group * g + hh, 1), :] += jnp.broadcast_to(tot, (1, LANES))
            ds = jnp.concatenate(ds16, axis=1)
            dqs = _dot(ds, kbd)
            for hh in range(group):
                dq_acc[pl.ds(j * WINDOW, WINDOW), pl.ds(KV * hh, KV)] = dqs[hh * WINDOW:(hh + 1) * WINDOW]
            keys = pl.ds(j * WINDOW, 2 * WINDOW)
            dk_acc[keys, :] += _diag_blocks_sum(_tn(ds, qs), 2 * WINDOW)
            dv_acc[keys, :] += _diag_blocks_sum(_tn(jnp.concatenate(probs16, axis=1), dos), 2 * WINDOW)

        tail = pl.ds(tq, WINDOW)
        dk_acc[tail, :] += carry_k[...]
        dv_acc[tail, :] += carry_v[...]
        carry_k[...] = dk_acc[pl.ds(0, WINDOW), :]
        carry_v[...] = dv_acc[pl.ds(0, WINDOW), :]
        cos, sin = cos_ref[...], sin_ref[...]
        dq_ref[...] = _rope_bwd(dq_acc[...], cos, sin).astype(BF16)
        dk_ref[...] = _rope_bwd(dk_acc[pl.ds(WINDOW, tq), :], cos, sin).astype(BF16)
        dv_ref[...] = dv_acc[pl.ds(WINDOW, tq), :].astype(BF16)

    def rrows(n):
        return pl.BlockSpec((tq, n), lambda i: (rev(i), 0))

    def rhalo(n):
        return pl.BlockSpec((WINDOW, n), lambda i: (jnp.maximum(rev(i) * nsub - 1, 0), 0))

    return pl.pallas_call(
        body, name=name, grid=(nt,),
        out_shape=(jax.ShapeDtypeStruct((T, D), BF16), jax.ShapeDtypeStruct((T, KV), BF16),
                   jax.ShapeDtypeStruct((T, KV), BF16), jax.ShapeDtypeStruct((D // HEAD_DIM, LANES), F32)),
        in_specs=[rrows(D), rrows(KV), rhalo(KV), rrows(KV), rhalo(KV), rrows(D), _resident(sink_col.shape),
                  rrows(LANES), rrows(LANES)],
        out_specs=(rrows(D), rrows(KV), rrows(KV), _acc_spec((D // HEAD_DIM, LANES))),
        scratch_shapes=[pltpu.VMEM((tq, D), F32), pltpu.VMEM((WINDOW + tq, KV), F32), pltpu.VMEM((WINDOW + tq, KV), F32),
                        pltpu.VMEM((WINDOW, KV), F32), pltpu.VMEM((WINDOW, KV), F32)],
        compiler_params=_params("arbitrary"),
    )(q, k, k, v, v, do, sink_col, cos, sin_signed)


def _ln_stats(u):
    mu = jnp.mean(u, axis=-1, keepdims=True)
    d = u - mu
    rstd = lax.rsqrt(jnp.mean(d * d, axis=-1, keepdims=True) + LN_EPS)
    return d * rstd, rstd


def _lag_taps(b, K):
    return [(a, K - 1 - (SUBLANES * a + b)) for a in range(-(-K // SUBLANES)) if SUBLANES * a + b <= K - 1]


def _conv_chunks(tm, D, chunk):
    def rows(c, carry):
        r0 = pl.multiple_of(c * CONV_ROWS, CONV_ROWS)
        for l0 in range(0, D, CONV_LANES):
            chunk(r0, pl.ds(l0, CONV_LANES))
        return carry
    lax.fori_loop(0, tm // CONV_ROWS, rows, 0)


def _conv_causal(buf, w_ref, bias_ref, out_ref, tm, D, K):
    def chunk(r0, lanes):
        acc = jnp.broadcast_to(bias_ref[:, lanes], (CONV_ROWS, CONV_LANES))
        for b in range(SUBLANES):
            y = None
            for a, k in _lag_taps(b, K):
                start = pl.multiple_of(r0 + CONV_HALO - SUBLANES * (a + 1), SUBLANES)
                t = buf[pl.ds(start, CONV_ROWS + SUBLANES), lanes] * w_ref[pl.ds(k, 1), lanes]
                y = t if y is None else y + t
            acc = acc + y[SUBLANES - b:SUBLANES - b + CONV_ROWS]
        out_ref[pl.ds(r0, CONV_ROWS), lanes] = acc
    _conv_chunks(tm, D, chunk)


def _conv_anticausal(dbuf, w_ref, out_ref, tm, D, K):
    def chunk(r0, lanes):
        acc = jnp.zeros((CONV_ROWS, CONV_LANES), F32)
        for b in range(SUBLANES):
            y = None
            for a, k in _lag_taps(b, K):
                start = pl.multiple_of(r0 + SUBLANES * a, SUBLANES)
                t = dbuf[pl.ds(start, CONV_ROWS + SUBLANES), lanes] * w_ref[pl.ds(k, 1), lanes]
                y = t if y is None else y + t
            acc = acc + y[b:b + CONV_ROWS]
        out_ref[pl.ds(r0, CONV_ROWS), lanes] = acc
    _conv_chunks(tm, D, chunk)


def _conv_tap_grads(dbuf, ubuf, acc_ref, tm, D, K):
    reach = SUBLANES * (-(-K // SUBLANES) - 1)

    def chunk(r0, lanes):
        d = dbuf[pl.ds(r0, CONV_ROWS), lanes]
        around = ubuf[pl.ds(pl.multiple_of(r0 + CONV_HALO - reach - SUBLANES, SUBLANES), CONV_ROWS + reach + SUBLANES), lanes]
        for b in range(SUBLANES):
            shifted = around[SUBLANES - b:SUBLANES - b + CONV_ROWS + reach]
            for a, k in _lag_taps(b, K):
                prod = d * shifted[reach - SUBLANES * a:reach - SUBLANES * a + CONV_ROWS]
                part = prod[0:SUBLANES]
                for i in range(1, CONV_ROWS // SUBLANES):
                    part = part + prod[SUBLANES * i:SUBLANES * (i + 1)]
                acc_ref[k, :, lanes] += part
    _conv_chunks(tm, D, chunk)


def mix_out_fwd(x, u0, o, sgc, sgt, dw_w, dw_b, ln_g, ln_b, wcp, wo, wout, name):
    T, D = x.shape
    tm = min(ROW_TILE, T)
    K = dw_w.shape[0]

    def body(x_ref, u_ref, uh_ref, o_ref, sgc_ref, sgt_ref, w_ref, b_ref, lg_ref, lb_ref, wcp_ref, wo_ref, wout_ref,
             x2_ref, u1_ref, co_ref, ao_ref, mg_ref, buf, conv):
        keep = (pl.program_id(0) > 0).astype(F32)
        buf[pl.ds(0, CONV_HALO), :] = uh_ref[...].astype(F32) * keep
        buf[pl.ds(CONV_HALO, tm), :] = u_ref[...].astype(F32)
        _conv_causal(buf, w_ref, b_ref, conv, tm, D, K)
        acc = conv[...]
        u1_ref[...] = acc.astype(BF16)
        xhat, _ = _ln_stats(acc)
        u2 = xhat * lg_ref[...] + lb_ref[...]
        u3 = (u2 * _sigmoid(u2)).astype(BF16)
        co = _dot(u3, wcp_ref[...])
        ao = _dot(o_ref[...], wo_ref[...])
        co_ref[...] = co.astype(BF16)
        ao_ref[...] = ao.astype(BF16)
        merged = (sgc_ref[...].astype(F32) * co + sgt_ref[...].astype(F32) * ao).astype(BF16)
        mg_ref[...] = merged
        x2_ref[...] = x_ref[...] + _dot(merged, wout_ref[...])

    big = jax.ShapeDtypeStruct((T, D), BF16)
    vec = _resident((1, D))
    return pl.pallas_call(
        body, name=name, grid=(T // tm,),
        out_shape=(jax.ShapeDtypeStruct((T, D), F32), big, big, big, big),
        in_specs=[_rows(tm, D), _rows(tm, D), _halo_rows_spec(tm, D, CONV_HALO), _rows(tm, D), _rows(tm, D), _rows(tm, D),
                  _resident((K, D)), vec, vec, vec, _resident((D, D)), _resident((D, D)), _resident((D, D))],
        out_specs=(_rows(tm, D),) * 5,
        scratch_shapes=[pltpu.VMEM((CONV_HALO + tm, D), F32), pltpu.VMEM((tm, D), F32)],
        compiler_params=_params("arbitrary"),
    )(x, u0, u0, o, sgc, sgt, dw_w, dw_b, ln_g, ln_b, wcp, wo, wout)


def mix_out_bwd(dx2, u1, co, ao, sgc, sgt, ln_g, ln_b, wcp, wo, wout, name):
    T, D = dx2.shape
    tm = min(ROW_TILE, T)

    def body(dx_ref, u1_ref, co_ref, ao_ref, sgc_ref, sgt_ref, lg_ref, lb_ref, wcp_ref, wo_ref, wout_ref,
             dgc_ref, dgt_ref, do_ref, du1_ref, dco_ref, dao_ref, u3_ref, sums_ref):
        dm = _nt(dx_ref[...].astype(BF16), wout_ref[...])
        sgc, sgt = sgc_ref[...].astype(F32), sgt_ref[...].astype(F32)
        dco = (dm * sgc).astype(BF16)
        dao = (dm * sgt).astype(BF16)
        dgc = dm * co_ref[...].astype(F32) * sgc * (1.0 - sgc)
        dgt = dm * ao_ref[...].astype(F32) * sgt * (1.0 - sgt)
        dco_ref[...] = dco
        dao_ref[...] = dao
        dgc_ref[...] = dgc.astype(BF16)
        dgt_ref[...] = dgt.astype(BF16)
        do_ref[...] = _nt(dao, wo_ref[...]).astype(BF16)
        du3 = _nt(dco, wcp_ref[...])
        xhat, rstd = _ln_stats(u1_ref[...].astype(F32))
        g = lg_ref[...]
        u2 = xhat * g + lb_ref[...]
        su = _sigmoid(u2)
        u3_ref[...] = (u2 * su).astype(BF16)
        du2 = du3 * (su * (1.0 + u2 * (1.0 - su)))
        dxh = du2 * g
        du1 = rstd * (dxh - jnp.mean(dxh, axis=-1, keepdims=True) - xhat * jnp.mean(dxh * xhat, axis=-1, keepdims=True))
        du1_ref[...] = du1.astype(BF16)

        @pl.when(pl.program_id(0) == 0)
        def _():
            sums_ref[...] = jnp.zeros_like(sums_ref)
        for r, val in enumerate((dgc, dgt, du2 * xhat, du2, du1)):
            sums_ref[pl.ds(r, 1), :] += jnp.sum(val, axis=0, keepdims=True)

    big = jax.ShapeDtypeStruct((T, D), BF16)
    vec = _resident((1, D))
    return pl.pallas_call(
        body, name=name, grid=(T // tm,),
        out_shape=(big,) * 7 + (jax.ShapeDtypeStruct((8, D), F32),),
        in_specs=[_rows(tm, D)] * 6 + [vec, vec, _resident((D, D)), _resident((D, D)), _resident((D, D))],
        out_specs=(_rows(tm, D),) * 7 + (_acc_spec((8, D)),),
        compiler_params=_params("arbitrary"),
    )(dx2, u1, co, ao, sgc, sgt, ln_g, ln_b, wcp, wo, wout)


def conv_bwd(du1, u0, ga, gb, dw_w, name, carry=None):
    T, D = du1.shape
    tm = min(ROW_TILE, T)
    nt = T // tm
    K = dw_w.shape[0]
    per = tm // CONV_HALO

    def body(d_ref, dn_ref, u_ref, uh_ref, ga_ref, gb_ref, w_ref, dga_ref, dgb_ref, dw_ref, dbuf, ubuf, du0_buf, taps):
        i = pl.program_id(0)
        dbuf[pl.ds(0, tm), :] = d_ref[...].astype(F32)
        dbuf[pl.ds(tm, CONV_HALO), :] = dn_ref[...].astype(F32) * (i < nt - 1).astype(F32)
        ubuf[pl.ds(0, CONV_HALO), :] = uh_ref[...].astype(F32) * (i > 0).astype(F32)
        ubuf[pl.ds(CONV_HALO, tm), :] = u_ref[...].astype(F32)

        @pl.when(i == 0)
        def _():
            taps[...] = jnp.zeros_like(taps)

        _conv_anticausal(dbuf, w_ref, du0_buf, tm, D, K)
        _conv_tap_grads(dbuf, ubuf, taps, tm, D, K)
        du0 = du0_buf[...]
        ga, gb = ga_ref[...].astype(F32), gb_ref[...].astype(F32)
        sg = _sigmoid(gb)
        dga_ref[...] = (du0 * sg).astype(BF16)
        dgb_ref[...] = (du0 * ga * sg * (1.0 - sg)).astype(BF16)

        @pl.when(i == nt - 1)
        def _():
            for k in range(K):
                dw_ref[pl.ds(k, 1), :] = jnp.sum(taps[k], axis=0, keepdims=True)

    nxt = pl.BlockSpec((CONV_HALO, D), lambda i: (jnp.minimum((i + 1) * per, nt * per - 1), 0))
    big = jax.ShapeDtypeStruct((T, D), BF16)
    return _call(
        body, name=name, grid=(nt,),
        out_shape=(big, big, jax.ShapeDtypeStruct((K, D), F32)),
        in_specs=[_rows(tm, D), nxt, _rows(tm, D), _halo_rows_spec(tm, D, CONV_HALO), _rows(tm, D), _rows(tm, D),
                  _resident((K, D))],
        out_specs=(_rows(tm, D), _rows(tm, D), _acc_spec((K, D))),
        scratch_shapes=[pltpu.VMEM((tm + CONV_HALO, D), F32), pltpu.VMEM((CONV_HALO + tm, D), F32), pltpu.VMEM((tm, D), F32),
                        pltpu.VMEM((K, SUBLANES, D), F32)],
        args=(du1, du1, u0, u0, ga, gb, dw_w), carry=carry)


def mix_in_bwd(dx2, x, gain, win, pieces, name):
    T, D = x.shape
    tm = min(ROW_TILE, T)
    widths = [p.shape[1] for p in pieces]
    offs = [sum(widths[:n]) for n in range(len(widths))]

    def body(dx2_ref, x_ref, g_ref, w_ref, *rest):
        piece_refs, (dx_ref, dg_ref) = rest[:len(pieces)], rest[len(pieces):]
        dh = None
        for p_ref, off, w in zip(piece_refs, offs, widths):
            t = _dot(p_ref[...], w_ref[pl.ds(off, w), :])
            dh = t if dh is None else dh + t
        dx, dgt = _rms_bwd(x_ref[...], g_ref[...], dh)
        dx_ref[...] = dx2_ref[...] + dx

        @pl.when(pl.program_id(0) == 0)
        def _():
            dg_ref[...] = jnp.zeros_like(dg_ref)
        dg_ref[...] += jnp.sum(dgt, axis=0, keepdims=True)

    return pl.pallas_call(
        body, name=name, grid=(T // tm,),
        out_shape=(jax.ShapeDtypeStruct((T, D), F32), jax.ShapeDtypeStruct((1, D), F32)),
        in_specs=[_rows(tm, D), _rows(tm, D), _resident((1, D)), _resident(win.shape)] + [_rows(tm, w) for w in widths],
        out_specs=(_rows(tm, D), _acc_spec((1, D))),
        compiler_params=_params("arbitrary"),
    )(dx2, x, gain, win, *pieces)


def final_loss(x, gain, target, name):
    T, D = x.shape
    tm = min(ROW_TILE, T)

    def body(x_ref, g_ref, t_ref, loss_ref, dx_ref, dg_ref):
        x, g = x_ref[...], g_ref[...]
        err = _rms_fwd(x, g) - t_ref[...]
        dx, dgt = _rms_bwd(x, g, err * (1.0 / D))
        dx_ref[...] = dx

        @pl.when(pl.program_id(0) == 0)
        def _():
            dg_ref[...] = jnp.zeros_like(dg_ref)
            loss_ref[...] = jnp.zeros_like(loss_ref)
        dg_ref[...] += jnp.sum(dgt, axis=0, keepdims=True)
        per_token = jnp.sum(err * err, axis=-1, keepdims=True) * (0.5 / D)
        loss_ref[...] += jnp.broadcast_to(jnp.sum(per_token, axis=0, keepdims=True), (1, LANES))

    return pl.pallas_call(
        body, name=name, grid=(T // tm,),
        out_shape=(jax.ShapeDtypeStruct((1, LANES), F32), jax.ShapeDtypeStruct((T, D), F32),
                   jax.ShapeDtypeStruct((1, D), F32)),
        in_specs=[_rows(tm, D), _resident((1, D)), _rows(tm, D)],
        out_specs=(_acc_spec((1, LANES)), _rows(tm, D), _acc_spec((1, D))),
        compiler_params=_params("arbitrary"),
    )(x, gain, target)


def rope_tables(positions):
    half = HEAD_DIM // 2
    inv_freq = ROPE_THETA ** (-jnp.arange(half, dtype=F32) / half)
    ang = positions.astype(F32)[:, None] * inv_freq
    cos, sin = jnp.cos(ang), jnp.sin(ang)
    reps = LANES // HEAD_DIM
    return jnp.tile(jnp.concatenate([cos, cos], axis=-1), (1, reps)), jnp.tile(jnp.concatenate([-sin, sin], axis=-1), (1, reps))


def _place():
    return lax.axis_index("x"), lax.axis_index("y"), lax.axis_index("c")


def all_gather(block, name):
    def body(x_ref, out_ref, send_sems, recv_sems, local_sem):
        x, y, c = _place()
        me, sibling = (x, y, c), (x, y, 1 - c)
        chips = [(1 - x, y), (x, 1 - y), (1 - x, 1 - y)]

        def rows(px, py, pc):
            return out_ref.at[4 * px + 2 * py + pc]

        def copy(k, block_of, to, src=None):
            return pltpu.make_async_remote_copy(
                src_ref=rows(*block_of) if src is None else src, dst_ref=rows(*block_of),
                send_sem=send_sems.at[k], recv_sem=recv_sems.at[k], device_id=to, device_id_type=MESH)

        mine = pltpu.make_async_copy(x_ref, rows(*me), local_sem)
        mine.start()
        first = [copy(0, me, sibling, src=x_ref)]
        first += [copy(1 + j, me, (*chip, c), src=x_ref) for j, chip in enumerate(chips)]
        for cp in first:
            cp.start()
        passed = [copy(4 + j, (*chip, c), sibling) for j, chip in enumerate(chips)]
        for j, chip in enumerate(chips):
            copy(1 + j, (*chip, c), me).wait_recv()
            passed[j].start()
        copy(0, sibling, me).wait_recv()
        for j, chip in enumerate(chips):
            copy(4 + j, (*chip, 1 - c), me).wait_recv()
        for cp in first + passed:
            cp.wait_send()
        mine.wait()

    return pl.pallas_call(
        body, name=name, out_shape=jax.ShapeDtypeStruct((N_DEV,) + block.shape, block.dtype),
        in_specs=[_ANY], out_specs=_ANY,
        scratch_shapes=[pltpu.SemaphoreType.DMA((7,)), pltpu.SemaphoreType.DMA((7,)), pltpu.SemaphoreType.DMA],
    )(block)


def _chips_across(x, y):
    return [(1 - x, y), (x, 1 - y), (1 - x, 1 - y)]


def _dev_index(x, y, c):
    return 4 * x + 2 * y + c


def gather_send(block):
    def copies(in_refs, out_refs, sems):
        (x_ref,), (out_ref,), (send, recv, local) = in_refs, out_refs, sems
        x, y, c = _place()
        targets = [(x, y, 1 - c)] + [(*chip, c) for chip in _chips_across(x, y)]
        outgoing = [pltpu.make_async_remote_copy(src_ref=x_ref, dst_ref=out_ref.at[_dev_index(x, y, c)], send_sem=send.at[k],
                                                 recv_sem=recv.at[k], device_id=t, device_id_type=MESH)
                    for k, t in enumerate(targets)]
        incoming = [pltpu.make_async_remote_copy(src_ref=x_ref, dst_ref=out_ref.at[_dev_index(*t)], send_sem=send.at[k],
                                                 recv_sem=recv.at[k], device_id=t, device_id_type=MESH)
                    for k, t in enumerate(targets)]
        return outgoing, incoming, pltpu.make_async_copy(x_ref, out_ref.at[_dev_index(x, y, c)], local)

    def start(*refs):
        outgoing, _, mine = copies(*refs)
        mine.start()
        for cp in outgoing:
            cp.start()

    def finish(*refs):
        outgoing, incoming, mine = copies(*refs)
        for cp in incoming:
            cp.wait_recv()
        for cp in outgoing:
            cp.wait_send()
        mine.wait()

    return Carry(ins=(block,), out_shapes=(jax.ShapeDtypeStruct((N_DEV,) + block.shape, block.dtype),), aliases={},
                 sems=(pltpu.SemaphoreType.DMA((4,)), pltpu.SemaphoreType.DMA((4,)), pltpu.SemaphoreType.DMA),
                 start=start, finish=finish)


def gather_forward(gathered):
    def copies(in_refs, out_refs, sems):
        (buf,), (send, recv) = out_refs, sems
        x, y, c = _place()
        outgoing, incoming = [], []
        for k, chip in enumerate(_chips_across(x, y)):
            rows = buf.at[_dev_index(*chip, c)]
            outgoing.append(pltpu.make_async_remote_copy(src_ref=rows, dst_ref=rows, send_sem=send.at[k], recv_sem=recv.at[k],
                                                         device_id=(x, y, 1 - c), device_id_type=MESH))
            theirs = buf.at[_dev_index(*chip, 1 - c)]
            incoming.append(pltpu.make_async_remote_copy(src_ref=theirs, dst_ref=theirs, send_sem=send.at[k],
                                                         recv_sem=recv.at[k], device_id=(x, y, 1 - c), device_id_type=MESH))
        return outgoing, incoming

    def start(*refs):
        for cp in copies(*refs)[0]:
            cp.start()

    def finish(*refs):
        outgoing, incoming = copies(*refs)
        for cp in incoming:
            cp.wait_recv()
        for cp in outgoing:
            cp.wait_send()

    return Carry(ins=(gathered,), out_shapes=(jax.ShapeDtypeStruct(gathered.shape, gathered.dtype),), aliases={0: 0},
                 sems=(pltpu.SemaphoreType.DMA((3,)), pltpu.SemaphoreType.DMA((3,))), start=start, finish=finish)


def swap_halves(by_core):
    n = len(by_core)

    def copies(in_refs, out_refs, sems):
        send, recv = sems
        x, y, c = _place()
        return [pltpu.make_async_remote_copy(src_ref=a.at[:, 1 - c], dst_ref=r, send_sem=send.at[i], recv_sem=recv.at[i],
                                             device_id=(x, y, 1 - c), device_id_type=MESH)
                for i, (a, r) in enumerate(zip(in_refs, out_refs))]

    def start(*refs):
        for cp in copies(*refs):
            cp.start()

    def finish(*refs):
        for cp in copies(*refs):
            cp.wait()

    shapes = tuple(jax.ShapeDtypeStruct((a.shape[0],) + a.shape[2:], a.dtype) for a in by_core)
    return Carry(ins=tuple(by_core), out_shapes=shapes, aliases={},
                 sems=(pltpu.SemaphoreType.DMA((n,)), pltpu.SemaphoreType.DMA((n,))), start=start, finish=finish)


def exchange_between_chips(by_chip):
    n = len(by_chip)

    def copies(in_refs, out_refs, sems):
        send, recv = sems
        x, y, c = _place()
        out = []
        for i, (s, r) in enumerate(zip(in_refs, out_refs)):
            for k, (tx, ty) in enumerate(_chips_across(x, y)):
                out.append(pltpu.make_async_remote_copy(
                    src_ref=s.at[2 * tx + ty], dst_ref=r.at[k], send_sem=send.at[3 * i + k], recv_sem=recv.at[3 * i + k],
                    device_id=(tx, ty, c), device_id_type=MESH))
        return out

    def start(*refs):
        for cp in copies(*refs):
            cp.start()

    def finish(*refs):
        for cp in copies(*refs):
            cp.wait()

    shapes = tuple(jax.ShapeDtypeStruct((3,) + a.shape[1:], a.dtype) for a in by_chip)
    return Carry(ins=tuple(by_chip), out_shapes=shapes, aliases={},
                 sems=(pltpu.SemaphoreType.DMA((3 * n,)), pltpu.SemaphoreType.DMA((3 * n,))), start=start, finish=finish)


def run_exchange(carry, name):
    n_in = len(carry.ins)
    n_out = len(carry.out_shapes)

    def body(*refs):
        parts = refs[:n_in], refs[n_in:n_in + n_out], refs[n_in + n_out:]
        carry.start(*parts)
        carry.finish(*parts)

    return pl.pallas_call(
        body, name=name, out_shape=tuple(carry.out_shapes), in_specs=[_ANY] * n_in, out_specs=(_ANY,) * n_out,
        scratch_shapes=list(carry.sems), input_output_aliases=dict(carry.aliases),
    )(*carry.ins)


def pair_sum(my_core, by_core, received, name):
    n = len(by_core)

    def body(core_ref, *refs):
        for a_ref, b_ref, o_ref in zip(refs[:n], refs[n:2 * n], refs[2 * n:]):
            o_ref[0] = (a_ref[0, 0].astype(F32) + b_ref[0].astype(F32)).astype(BF16)

    mine = [pl.BlockSpec((1, 1) + a.shape[2:], lambda j, core: (j, core[0], 0, 0)) for a in by_core]
    theirs = [pl.BlockSpec((1,) + r.shape[1:], lambda j, core: (j, 0, 0)) for r in received]
    return pl.pallas_call(
        body, name=name, out_shape=tuple(jax.ShapeDtypeStruct(r.shape, BF16) for r in received),
        grid_spec=pltpu.PrefetchScalarGridSpec(num_scalar_prefetch=1, grid=(by_core[0].shape[0],), in_specs=mine + theirs,
                                               out_specs=tuple(theirs)),
        compiler_params=_params("arbitrary"),
    )(my_core, *by_core, *received)


def chip_sum(my_chip, by_chip, received, name):
    n = len(by_chip)

    def body(chip_ref, *refs):
        for a_ref, b_ref, o_ref in zip(refs[:n], refs[n:2 * n], refs[2 * n:]):
            acc = a_ref[0].astype(F32)
            for k in range(3):
                acc = acc + b_ref[k].astype(F32)
            o_ref[...] = acc

    own = [pl.BlockSpec((1,) + a.shape[1:], lambda i, chip: (chip[0], 0, 0)) for a in by_chip]
    got = [pl.BlockSpec(r.shape, lambda i, chip: (0, 0, 0)) for r in received]
    outs = tuple(pl.BlockSpec(a.shape[1:], lambda i, chip: (0, 0)) for a in by_chip)
    return pl.pallas_call(
        body, name=name, out_shape=tuple(jax.ShapeDtypeStruct(a.shape[1:], F32) for a in by_chip),
        grid_spec=pltpu.PrefetchScalarGridSpec(num_scalar_prefetch=1, grid=(1,), in_specs=own + got, out_specs=outs),
        compiler_params=_params("arbitrary"),
    )(my_chip, *by_chip, *received)


def _adamw_math(w, g, m, v):
    m = ADAM_B1 * m + (1.0 - ADAM_B1) * g
    v = ADAM_B2 * v + (1.0 - ADAM_B2) * (g * g)
    m_hat = m / (1.0 - ADAM_B1 ** ADAM_STEP)
    v_hat = v / (1.0 - ADAM_B2 ** ADAM_STEP)
    delta = -ADAM_LR * (m_hat / (jnp.sqrt(v_hat) + ADAM_EPS) + ADAM_WD * w)
    return delta, m, v


def adamw(ws, gs, ms, vs, name):
    n = len(ws)

    def body(*refs):
        ins, outs = refs[:4 * n], refs[4 * n:]
        for t in range(n):
            delta, m, v = _adamw_math(ins[t][...], ins[n + t][...], ins[2 * n + t][...], ins[3 * n + t][...])
            outs[3 * t][...] = delta
            outs[3 * t + 1][...] = m
            outs[3 * t + 2][...] = v

    shapes = []
    for w in ws:
        shapes += [jax.ShapeDtypeStruct(w.shape, F32)] * 3
    res = pl.pallas_call(body, name=name, out_shape=tuple(shapes), compiler_params=_params())(*ws, *gs, *ms, *vs)
    return [tuple(res[3 * t:3 * t + 3]) for t in range(n)]


def adamw_replicated(w, partials, m, v, name):
    def body(w_ref, p_ref, m_ref, v_ref, g_ref, d_ref, mo_ref, vo_ref):
        g = p_ref[0]
        for k in range(1, N_DEV):
            g = g + p_ref[k]
        g_ref[...] = g
        d_ref[...], mo_ref[...], vo_ref[...] = _adamw_math(w_ref[...], g, m_ref[...], v_ref[...])

    shape = jax.ShapeDtypeStruct(w.shape, F32)
    return pl.pallas_call(body, name=name, out_shape=(shape,) * 4, compiler_params=_params())(w, partials, m, v)


PACK_COLS = 1024
PACK_ROW_ALIGN = 16

SHARDED = {"ffn1_w_gate": 1, "ffn1_w_up": 1, "ffn1_w_down": 0, "w_in": 1, "conv_dw_w": 1, "conv_w_proj": 0, "attn_w_o": 0,
           "w_out": 0, "ffn2_w_gate": 1, "ffn2_w_up": 1, "ffn2_w_down": 0}
REPLICATED = ("ffn1_norm", "mix_norm", "conv_dw_b", "conv_ln_g", "conv_ln_b", "ffn2_norm", "final_norm", "gate_b", "attn_sinks")
WEIGHT_ORDER = ("ffn1_norm", "ffn1_w_gate", "ffn1_w_up", "ffn1_w_down", "mix_norm", "w_in", "conv_dw_w", "conv_dw_b", "conv_ln_g",
                "conv_ln_b", "conv_w_proj", "attn_sinks", "attn_w_o", "gate_b", "w_out", "ffn2_norm", "ffn2_w_gate", "ffn2_w_up",
                "ffn2_w_down", "final_norm")


def _to_rows(flat, lead):
    n = flat.shape[-1]
    rows = -(-n // PACK_COLS)
    flat = jnp.pad(flat, [(0, 0)] * lead + [(0, rows * PACK_COLS - n)])
    return flat.reshape(flat.shape[:lead] + (rows, PACK_COLS))


def _pad_rows(a, axis):
    rows = a.shape[axis]
    pad = -rows % PACK_ROW_ALIGN
    widths = [(0, 0)] * a.ndim
    widths[axis] = (0, pad)
    return jnp.pad(a, widths)


def _from_rows(rows, shape):
    n = 1
    for s in shape:
        n *= s
    return rows.reshape(rows.shape[:-2] + (-1,))[..., :n].reshape(rows.shape[:-2] + tuple(shape))


def _pack_weights(parts):
    layout, off = [], 0
    for p in parts:
        layout.append((off, p.shape[0]))
        off += p.shape[0]
    return _pad_rows(jnp.concatenate(parts, axis=0), 0), layout


def _gathered_rows(gathered, off, rows):
    return gathered[:, off:off + rows].reshape(N_DEV * rows, gathered.shape[2])


def _heads_slot_major(rows):
    group = rows.shape[0] // (N_KV_HEADS * HEAD_DIM)
    return rows.reshape(N_KV_HEADS, group, HEAD_DIM, rows.shape[1]).transpose(1, 0, 2, 3).reshape(rows.shape)


def _heads_kv_major(rows):
    group = rows.shape[0] // (N_KV_HEADS * HEAD_DIM)
    return rows.reshape(group, N_KV_HEADS, HEAD_DIM, rows.shape[1]).transpose(1, 0, 2, 3).reshape(rows.shape)


def _by_core(full_rows):
    return full_rows.reshape((N_DEV // 2, 2, full_rows.shape[0] // N_DEV, full_rows.shape[1]))


def kernel(x, positions, ffn1_norm, ffn1_w_gate, ffn1_w_up, ffn1_w_down, mix_norm, w_in, conv_dw_w, conv_dw_b, conv_ln_g, conv_ln_b, conv_w_proj, attn_sinks, attn_w_o, gate_b, w_out, ffn2_norm, ffn2_w_gate, ffn2_w_up, ffn2_w_down, final_norm, loss_target, m_ffn1_norm, m_ffn1_w_gate, m_ffn1_w_up, m_ffn1_w_down, m_mix_norm, m_w_in, m_conv_dw_w, m_conv_dw_b, m_conv_ln_g, m_conv_ln_b, m_conv_w_proj, m_attn_sinks, m_attn_w_o, m_gate_b, m_w_out, m_ffn2_norm, m_ffn2_w_gate, m_ffn2_w_up, m_ffn2_w_down, m_final_norm, v_ffn1_norm, v_ffn1_w_gate, v_ffn1_w_up, v_ffn1_w_down, v_mix_norm, v_w_in, v_conv_dw_w, v_conv_dw_b, v_conv_ln_g, v_conv_ln_b, v_conv_w_proj, v_attn_sinks, v_attn_w_o, v_gate_b, v_w_out, v_ffn2_norm, v_ffn2_w_gate, v_ffn2_w_up, v_ffn2_w_down, v_final_norm):
    given = dict(locals())
    shapes = {n: given[n].shape for n in WEIGHT_ORDER}
    w = {n: given[n].reshape(given[n].shape[-2:]) if given[n].ndim == 3 else given[n].reshape(1, -1) for n in WEIGHT_ORDER}
    m = {n: given["m_" + n].reshape(w[n].shape) for n in WEIGHT_ORDER}
    v = {n: given["v_" + n].reshape(w[n].shape) for n in WEIGHT_ORDER}
    my_x, my_y, my_c = _place()
    my_core = my_c.astype(jnp.int32).reshape(1)
    my_chip = (2 * my_x + my_y).astype(jnp.int32).reshape(1)
    xs, target = x[0], loss_target[0]
    T, D = xs.shape
    KV = N_KV_HEADS * HEAD_DIM
    K = w["conv_dw_w"].shape[0]

    def t16(n):
        return w[n].T.astype(BF16)

    def r16(n):
        return w[n].astype(BF16)

    pack1, lay1 = _pack_weights([t16("ffn1_w_gate"), t16("ffn1_w_up"), r16("ffn1_w_down")])
    dw_bits = _to_rows(lax.bitcast_convert_type(w["conv_dw_w"], BF16).reshape(-1), 0)
    pack2, lay2 = _pack_weights([t16("w_in"), r16("conv_w_proj"), r16("attn_w_o"), r16("w_out"), dw_bits])
    pack3, lay3 = _pack_weights([t16("ffn2_w_gate"), t16("ffn2_w_up"), r16("ffn2_w_down")])
    cos, sin = rope_tables(positions[0])
    sink_col = jnp.repeat(w["attn_sinks"].reshape(-1), WINDOW).reshape(N_KV_HEADS, (D // KV) * WINDOW, 1)

    gath1 = all_gather(pack1, "gather_ffn1")
    wgt1, wut1, wd1 = (_gathered_rows(gath1, *l) for l in lay1)
    (x1, h1, a1, b1), (gath2,) = ffn_fwd(xs, w["ffn1_norm"], wgt1, wut1, wd1, "ffn1_fwd", carry=gather_send(pack2))
    (gath2,) = run_exchange(gather_forward(gath2), "gather_mix_forward")
    wint = _gathered_rows(gath2, *lay2[0])
    wint = jnp.concatenate([wint[:2 * D], _heads_slot_major(wint[2 * D:3 * D]), wint[3 * D + 2 * KV:],
                            wint[3 * D:3 * D + 2 * KV]], axis=0)
    wcp, wo, wout = (_gathered_rows(gath2, *l) for l in lay2[1:4])
    wo = _heads_slot_major(wo)
    o_dw, r_dw = lay2[4]
    dw_full = lax.bitcast_convert_type(_from_rows(gath2[:, o_dw:o_dw + r_dw], w["conv_dw_w"].shape + (2,)), F32)
    dw_full = dw_full.transpose(1, 0, 2).reshape(K, D)
    (h2, ga, gb, u0, q, sgc, sgt, kk, vv), (gath3,) = mix_in_fwd(x1, w["mix_norm"], wint, w["gate_b"], cos, sin, "mix_in_fwd",
                                                                 carry=gather_send(pack3))
    o, (gath3,) = attn_fwd(q, kk, vv, sink_col, "attn_fwd", carry=gather_forward(gath3))
    x2, u1, co, ao, merged = mix_out_fwd(x1, u0, o, sgc, sgt, dw_full, w["conv_dw_b"], w["conv_ln_g"], w["conv_ln_b"],
                                         wcp, wo, wout, "mix_out_fwd")
    wgt2, wut2, wd2 = (_gathered_rows(gath3, *l) for l in lay3)
    (x3, h3, a2, b2), _ = ffn_fwd(x2, w["ffn2_norm"], wgt2, wut2, wd2, "ffn2_fwd")
    loss, dx3, d_final = final_loss(x3, w["final_norm"], target, "final_loss")

    small = {"final_norm": d_final}
    dx2, da2, db2, s2, g02, small["ffn2_norm"] = ffn_bwd(dx3, x2, w["ffn2_norm"], a2, b2, wgt2, wut2, wd2, "ffn2_bwd")
    core2 = [_by_core(wgrad(a, b, nm)[0]) for a, b, nm in ((da2, h3, "ffn2_dwg"), (db2, h3, "ffn2_dwu"), (s2, g02, "ffn2_dwd"))]
    chip2 = pair_sum(my_core, core2, run_exchange(swap_halves(core2), "ffn2_grads_swap"), "ffn2_grads_pair_sum")

    dgc, dgt, do, du1, dco, dao, u3, sums = mix_out_bwd(dx2, u1, co, ao, sgc, sgt, w["conv_ln_g"], w["conv_ln_b"],
                                                        wcp, wo, wout, "mix_out_bwd")
    small["gate_b"] = jnp.concatenate([sums[0:1], sums[1:2]], axis=1)
    small["conv_ln_g"], small["conv_ln_b"], small["conv_dw_b"] = sums[2:3], sums[3:4], sums[4:5]
    g_wout = wgrad(merged, dx2, "dw_out")[0]
    g_wcp = wgrad(u3, dco, "dw_conv_proj")[0]
    g_wo = _heads_kv_major(wgrad(o, dao, "dw_attn_o")[0])
    (dga, dgb, g_dw), got2 = conv_bwd(du1, u0, ga, gb, dw_full, "conv_bwd", carry=exchange_between_chips(chip2))
    dq, dk, dv, dsink = attn_bwd(q, kk, vv, do, sink_col, cos, sin, "attn_bwd")
    small["attn_sinks"] = dsink[:, 0].reshape(1, -1)
    pieces = [dga, dgb, dq, dgc, dgt, dk, dv]
    dx1, small["mix_norm"] = mix_in_bwd(dx2, x1, w["mix_norm"], wint, pieces, "mix_in_bwd")
    dwin = [wgrad(p, h2, "dw_in_%d" % n)[0] for n, p in enumerate(pieces)]
    g_wint = jnp.concatenate([dwin[0], dwin[1], _heads_kv_major(dwin[2]), dwin[5], dwin[6], dwin[3], dwin[4]],
                             axis=0)
    corem = [_by_core(a) for a in (g_wint, g_wcp, g_wo, g_wout)]
    chipm = pair_sum(my_core, corem, run_exchange(swap_halves(corem), "mix_grads_swap"), "mix_grads_pair_sum")

    grad_x, da1, db1, s1, g01, small["ffn1_norm"] = ffn_bwd(dx1, xs, w["ffn1_norm"], a1, b1, wgt1, wut1, wd1, "ffn1_bwd")
    g1a, gotm_a = wgrad(da1, h1, "ffn1_dwg", carry=exchange_between_chips(chipm[:1]))
    g1b, gotm_b = wgrad(db1, h1, "ffn1_dwu", carry=exchange_between_chips(chipm[1:]))
    g1c = wgrad(s1, g01, "ffn1_dwd")[0]
    core1 = [_by_core(a) for a in (g1a, g1b, g1c)]
    chip1 = pair_sum(my_core, core1, run_exchange(swap_halves(core1), "ffn1_grads_swap"), "ffn1_grads_pair_sum")
    got1 = run_exchange(exchange_between_chips(chip1), "ffn1_grads_exchange")

    gs2 = chip_sum(my_chip, chip2, got2, "ffn2_grads_sum")
    gsm = chip_sum(my_chip, chipm, gotm_a + gotm_b, "mix_grads_sum")
    gs1 = chip_sum(my_chip, chip1, got1, "ffn1_grads_sum")
    grads = {"ffn1_w_gate": gs1[0].T, "ffn1_w_up": gs1[1].T, "ffn1_w_down": gs1[2],
             "ffn2_w_gate": gs2[0].T, "ffn2_w_up": gs2[1].T, "ffn2_w_down": gs2[2],
             "w_in": gsm[0].T, "conv_w_proj": gsm[1], "attn_w_o": gsm[2], "w_out": gsm[3]}

    def pack_small(d, taps, extra):
        rows = [_to_rows(d[n].reshape(-1), 0) for n in REPLICATED] + [taps, _to_rows(extra.reshape(-1), 0)]
        return _pad_rows(jnp.concatenate(rows, axis=0), 0)

    zero, no_taps = jnp.zeros((1, LANES), F32), jnp.zeros((K, D), F32)
    shares = all_gather(pack_small(small, g_dw, loss), "gather_small_grads")
    g_s, d_s, m_s, v_s = adamw_replicated(pack_small(w, no_taps, zero), shares, pack_small(m, no_taps, zero),
                                          pack_small(v, no_taps, zero), "adamw_replicated")
    delta, new_m, new_v = {}, {}, {}
    off = 0
    for n in REPLICATED:
        r = -(-w[n].shape[1] // PACK_COLS)
        grads[n], delta[n], new_m[n], new_v[n] = (_from_rows(a[off:off + r], w[n].shape) for a in (g_s, d_s, m_s, v_s))
        off += r
    shard_cols = w["conv_dw_w"].shape[1]
    grads["conv_dw_w"] = lax.dynamic_slice_in_dim(g_s[off:off + K], _dev_index(my_x, my_y, my_c) * shard_cols, shard_cols, axis=1)
    total_loss = g_s[off + K, 0]

    groups = (("ffn1_w_gate", "ffn1_w_up", "ffn1_w_down"), ("ffn2_w_gate", "ffn2_w_up", "ffn2_w_down"),
              ("w_in", "conv_dw_w", "conv_w_proj", "attn_w_o", "w_out"))
    for k, names in enumerate(groups):
        res = adamw([w[n] for n in names], [grads[n] for n in names], [m[n] for n in names], [v[n] for n in names],
                    "adamw_%d" % k)
        for n, (d, mm, vv) in zip(names, res):
            delta[n], new_m[n], new_v[n] = d, mm, vv

    out = [total_loss, grad_x[None]]
    for d in (grads, delta, new_m, new_v):
        out += [d[n].reshape(shapes[n]) for n in WEIGHT_ORDER]
    return tuple(out)
```

```python
import functools
from typing import Callable, NamedTuple

import jax
import jax.numpy as jnp
from jax import lax
from jax.experimental import pallas as pl
from jax.experimental.pallas import tpu as pltpu

F32, BF16 = jnp.float32, jnp.bfloat16

HEAD_DIM = 64
N_KV_HEADS = 4
WINDOW = 128
CONV_WIDTH = 31
ROPE_THETA = 10000.0
EPS = 1e-6
LN_EPS = 1e-5
NEG_INF = -1e30
ADAM_LR, ADAM_B1, ADAM_B2, ADAM_EPS, ADAM_WD, ADAM_STEP = 0.001, 0.9, 0.999, 1e-08, 0.01, 10

N_DEV = 8
LANES = 128
SUBLANES = 8
CONV_HALO = 32
CONV_ROWS, CONV_LANES = 64, 256
ROW_TILE = 512
FFN_CHUNK = 256
WGRAD_TILE_ELEMS = 2 ** 22
WGRAD_TILE_ROWS = 2048
VMEM_LIMIT = 56 * 2 ** 20
MESH = pl.DeviceIdType.MESH


def _params(*sem):
    return pltpu.CompilerParams(dimension_semantics=sem or None, vmem_limit_bytes=VMEM_LIMIT)


def _resident(shape):
    zeros = (0,) * len(shape)
    return pl.BlockSpec(shape, lambda *_: zeros, pipeline_mode=pl.Buffered(1))


def _rows(tm, n):
    return pl.BlockSpec((tm, n), lambda i: (i, 0))


def _acc_spec(shape):
    zeros = (0,) * len(shape)
    return pl.BlockSpec(shape, lambda *_: zeros)


_ANY = pl.BlockSpec(memory_space=pl.ANY)


class Carry(NamedTuple):
    ins: tuple
    out_shapes: tuple
    aliases: dict
    sems: tuple
    start: Callable
    finish: Callable


def _call(body, *, name, grid, in_specs, out_specs, out_shape, args, scratch_shapes=(), carry=None):
    n_in, n_out, n_scr = len(in_specs), len(out_specs), len(scratch_shapes)
    params = _params(*(("arbitrary",) * len(grid)))
    if carry is None:
        res = pl.pallas_call(body, name=name, grid=grid, in_specs=list(in_specs), out_specs=tuple(out_specs),
                             out_shape=tuple(out_shape), scratch_shapes=list(scratch_shapes), compiler_params=params)(*args)
        return tuple(res), ()
    c_in, c_out = len(carry.ins), len(carry.out_shapes)

    def wrapped(*refs):
        ins, c_ins = refs[:n_in], refs[n_in:n_in + c_in]
        p = n_in + c_in
        outs, c_outs = refs[p:p + n_out], refs[p + n_out:p + n_out + c_out]
        p += n_out + c_out
        scr, c_sems = refs[p:p + n_scr], refs[p + n_scr:]
        ids = [pl.program_id(d) for d in range(len(grid))]
        first = functools.reduce(jnp.logical_and, [i == 0 for i in ids])
        last = functools.reduce(jnp.logical_and, [i == n - 1 for i, n in zip(ids, grid)])

        @pl.when(first)
        def _():
            carry.start(c_ins, c_outs, c_sems)

        body(*ins, *outs, *scr)

        @pl.when(last)
        def _():
            carry.finish(c_ins, c_outs, c_sems)

    res = pl.pallas_call(
        wrapped, name=name, grid=grid, in_specs=list(in_specs) + [_ANY] * c_in, out_specs=tuple(out_specs) + (_ANY,) * c_out,
        out_shape=tuple(out_shape) + tuple(carry.out_shapes), scratch_shapes=list(scratch_shapes) + list(carry.sems),
        input_output_aliases={n_in + i: n_out + o for i, o in carry.aliases.items()}, compiler_params=params,
    )(*args, *carry.ins)
    return tuple(res[:n_out]), tuple(res[n_out:])


def _nt(a, b):
    return lax.dot_general(a, b, (((1,), (1,)), ((), ())), preferred_element_type=F32)


def _tn(a, b):
    return lax.dot_general(a, b, (((0,), (0,)), ((), ())), preferred_element_type=F32)


def _dot(a, b):
    return jnp.dot(a, b, preferred_element_type=F32)


def _sigmoid(x):
    return 1.0 / (1.0 + jnp.exp(-x))


def _rms_fwd(x, g):
    r = lax.rsqrt(jnp.mean(x * x, axis=-1, keepdims=True) + EPS)
    return (x * r) * g


def _rms_bwd(x, g, dy):
    r = lax.rsqrt(jnp.mean(x * x, axis=-1, keepdims=True) + EPS)
    xhat = x * r
    dyg = dy * g
    dx = r * (dyg - xhat * jnp.mean(dyg * xhat, axis=-1, keepdims=True))
    return dx, dy * xhat


def _rot_half(x):
    lane = lax.broadcasted_iota(jnp.int32, (x.shape[0], LANES), 1)
    first = (lane % HEAD_DIM) < (HEAD_DIM // 2)
    out = []
    for s in range(x.shape[1] // LANES):
        xs = x[:, LANES * s:LANES * (s + 1)]
        out.append(jnp.where(first, pltpu.roll(xs, LANES - HEAD_DIM // 2, 1), pltpu.roll(xs, HEAD_DIM // 2, 1)))
    return out[0] if len(out) == 1 else jnp.concatenate(out, axis=1)


def _tile_lanes(t, width):
    return t if width == LANES else jnp.concatenate([t] * (width // LANES), axis=1)


def _rope_fwd(x, cos, sin_signed):
    w = x.shape[1]
    return x * _tile_lanes(cos, w) + _rot_half(x) * _tile_lanes(sin_signed, w)


def _rope_bwd(dy, cos, sin_signed):
    w = dy.shape[1]
    return dy * _tile_lanes(cos, w) + _rot_half(dy * _tile_lanes(sin_signed, w))


def ffn_fwd(x, gain, wgt, wut, wd, name, carry=None):
    T, D = x.shape
    F = wgt.shape[0]
    tm = min(ROW_TILE, T)
    fc = FFN_CHUNK

    def body(x_ref, g_ref, wg_ref, wu_ref, wd_ref, xo_ref, h_ref, a_ref, b_ref, acc_ref):
        x = x_ref[...]
        h = _rms_fwd(x, g_ref[...]).astype(BF16)
        h_ref[...] = h
        for c in range(F // fc):
            cs = pl.ds(c * fc, fc)
            a = _nt(h, wg_ref[cs, :])
            b = _nt(h, wu_ref[cs, :])
            a_ref[:, cs] = a.astype(BF16)
            b_ref[:, cs] = b.astype(BF16)
            s = (a * _sigmoid(a) * b).astype(BF16)
            y = _dot(s, wd_ref[cs, :])
            if c == 0:
                acc_ref[...] = y
            else:
                acc_ref[...] += y
        xo_ref[...] = x + 0.5 * acc_ref[...]

    return _call(
        body, name=name, grid=(T // tm,),
        out_shape=(jax.ShapeDtypeStruct((T, D), F32), jax.ShapeDtypeStruct((T, D), BF16),
                   jax.ShapeDtypeStruct((T, F), BF16), jax.ShapeDtypeStruct((T, F), BF16)),
        in_specs=[_rows(tm, D), _resident((1, D)), _resident((F, D)), _resident((F, D)), _resident((F, D))],
        out_specs=(_rows(tm, D), _rows(tm, D), _rows(tm, F), _rows(tm, F)),
        scratch_shapes=[pltpu.VMEM((tm, D), F32)], args=(x, gain, wgt, wut, wd), carry=carry)


def ffn_bwd_hidden(dxo, a, b, wd, name, carry=None):
    T, D = dxo.shape
    F = wd.shape[0]
    tm = min(ROW_TILE, T)
    fc = FFN_CHUNK

    def hidden_body(dxo_ref, a_ref, b_ref, wd_ref, da_ref, db_ref, s_ref, g0_ref):
        g0 = (0.5 * dxo_ref[...]).astype(BF16)
        g0_ref[...] = g0
        for c in range(F // fc):
            cs = pl.ds(c * fc, fc)
            ds = _nt(g0, wd_ref[cs, :])
            a = a_ref[:, cs].astype(F32)
            bb = b_ref[:, cs].astype(F32)
            sa = _sigmoid(a)
            silu = a * sa
            da_ref[:, cs] = (ds * bb * (sa * (1.0 + a * (1.0 - sa)))).astype(BF16)
            db_ref[:, cs] = (ds * silu).astype(BF16)
            s_ref[:, cs] = (silu * bb).astype(BF16)

    wide = jax.ShapeDtypeStruct((T, F), BF16)
    return _call(
        hidden_body, name=name, grid=(T // tm,),
        out_shape=(wide, wide, wide, jax.ShapeDtypeStruct((T, D), BF16)),
        in_specs=[_rows(tm, D), _rows(tm, F), _rows(tm, F), _resident((F, D))],
        out_specs=(_rows(tm, F), _rows(tm, F), _rows(tm, F), _rows(tm, D)), args=(dxo, a, b, wd), carry=carry)


def ffn_bwd_input(dxo, x, gain, da, db, wg, wu, name, carry=None):
    T, D = x.shape
    F = wg.shape[0]
    tm = min(ROW_TILE, T)

    def input_body(dxo_ref, x_ref, g_ref, da_ref, db_ref, wg_ref, wu_ref, dx_ref, dg_ref):
        dh = _dot(da_ref[...], wg_ref[...]) + _dot(db_ref[...], wu_ref[...])
        dx, dgt = _rms_bwd(x_ref[...], g_ref[...], dh)
        dx_ref[...] = dxo_ref[...] + dx

        @pl.when(pl.program_id(0) == 0)
        def _():
            dg_ref[...] = jnp.zeros_like(dg_ref)
        dg_ref[...] += jnp.sum(dgt, axis=0, keepdims=True)

    return _call(
        input_body, name=name, grid=(T // tm,),
        out_shape=(jax.ShapeDtypeStruct((T, D), F32), jax.ShapeDtypeStruct((1, D), F32)),
        in_specs=[_rows(tm, D), _rows(tm, D), _resident((1, D)), _rows(tm, F), _rows(tm, F), _resident((F, D)),
                  _resident((F, D))],
        out_specs=(_rows(tm, D), _acc_spec((1, D))), args=(dxo, x, gain, da, db, wg, wu), carry=carry)


def wgrad(a, b, name, carry=None):
    T, M = a.shape
    N = b.shape[1]
    tk = ROW_TILE
    while 2 * tk * M <= WGRAD_TILE_ELEMS and 2 * tk <= WGRAD_TILE_ROWS:
        tk *= 2
    tk = min(tk, T)
    nk = T // tk

    def body(a_ref, b_ref, o_ref, acc_ref):
        k = pl.program_id(0)
        part = _tn(a_ref[...].astype(BF16), b_ref[...].astype(BF16))

        @pl.when(k == 0)
        def _():
            acc_ref[...] = part

        @pl.when(k > 0)
        def _():
            acc_ref[...] += part

        @pl.when(k == nk - 1)
        def _():
            o_ref[...] = acc_ref[...].astype(BF16)

    (out,), carried = _call(
        body, name=name, grid=(nk,), out_shape=(jax.ShapeDtypeStruct((M, N), BF16),),
        in_specs=[_rows(tk, M), _rows(tk, N)], out_specs=(_acc_spec((M, N)),),
        scratch_shapes=[pltpu.VMEM((M, N), F32)], args=(a, b), carry=carry)
    return out, carried


def _w_in_rows(D):
    KV = N_KV_HEADS * HEAD_DIM
    return 0, D, 2 * D, 3 * D, 3 * D + KV, 3 * D + 2 * KV, 4 * D + 2 * KV


def mix_in_fwd(x, gain, wint, wq, gate_b, cos, sin_signed, name, carry=None):
    T, D = x.shape
    KV = N_KV_HEADS * HEAD_DIM
    tm = min(ROW_TILE, T)
    o_ga, o_gb, _, o_k, o_v, o_gc, o_gt = _w_in_rows(D)

    def body(x_ref, g_ref, w_ref, wq_ref, gb_ref, cos_ref, sin_ref,
             h_ref, ga_ref, gb_out_ref, u0_ref, q_ref, sgc_ref, sgt_ref, k_ref, v_ref):
        h = _rms_fwd(x_ref[...], g_ref[...]).astype(BF16)
        h_ref[...] = h
        cos, sin = cos_ref[...], sin_ref[...]
        ga = _nt(h, w_ref[pl.ds(o_ga, D), :])
        gb = _nt(h, w_ref[pl.ds(o_gb, D), :])
        ga_ref[...] = ga.astype(BF16)
        gb_out_ref[...] = gb.astype(BF16)
        u0_ref[...] = (ga * _sigmoid(gb)).astype(BF16)
        q = _nt(h, wq_ref[...])
        q_ref[...] = _rope_fwd(q, cos, sin).astype(BF16)
        gc = _nt(h, w_ref[pl.ds(o_gc, D), :]) + gb_ref[:, pl.ds(0, D)]
        sgc_ref[...] = _sigmoid(gc).astype(BF16)
        gt = _nt(h, w_ref[pl.ds(o_gt, D), :]) + gb_ref[:, pl.ds(D, D)]
        sgt_ref[...] = _sigmoid(gt).astype(BF16)
        k = _nt(h, w_ref[pl.ds(o_k, KV), :])
        k_ref[...] = _rope_fwd(k, cos, sin).astype(BF16)
        v_ref[...] = _nt(h, w_ref[pl.ds(o_v, KV), :]).astype(BF16)

    big = jax.ShapeDtypeStruct((T, D), BF16)
    small = jax.ShapeDtypeStruct((T, KV), BF16)
    return _call(
        body, name=name, grid=(T // tm,),
        out_shape=(big, big, big, big, big, big, big, small, small),
        in_specs=[_rows(tm, D), _resident((1, D)), _resident(wint.shape), _resident(wq.shape), _resident((1, 2 * D)),
                  _rows(tm, LANES), _rows(tm, LANES)],
        out_specs=(_rows(tm, D),) * 7 + (_rows(tm, KV),) * 2,
        args=(x, gain, wint, wq, gate_b, cos, sin_signed), carry=carry)


def _attn_masks(tile_is_first, rows):
    qi = lax.broadcasted_iota(jnp.int32, (rows, 2 * WINDOW), 0) % WINDOW
    c = lax.broadcasted_iota(jnp.int32, (rows, 2 * WINDOW), 1)
    base = (c > qi) & (c <= qi + WINDOW)
    first_key = jnp.where(tile_is_first, WINDOW, 0)
    return base, base & (c >= first_key)


def _kv_lane_head(rows, width):
    return lax.broadcasted_iota(jnp.int32, (rows, width), 1) // HEAD_DIM


def _block_diag(win):
    head = _kv_lane_head(*win.shape)
    zero = jnp.zeros_like(win)
    return jnp.concatenate([jnp.where(head == g, win, zero) for g in range(N_KV_HEADS)], axis=0)


def _diag_blocks_sum(bd, keys):
    head = _kv_lane_head(keys, bd.shape[1])
    out = jnp.zeros((keys, bd.shape[1]), F32)
    for g in range(N_KV_HEADS):
        out = jnp.where(head == g, bd[g * keys:(g + 1) * keys], out)
    return out


def _kv_windows(k_ref, kh_ref, v_ref, vh_ref, j):
    rows = pl.ds(j * WINDOW, WINDOW)
    if j == 0:
        kprev, vprev = kh_ref[...], vh_ref[...]
    else:
        prev = pl.ds((j - 1) * WINDOW, WINDOW)
        kprev, vprev = k_ref[prev, :], v_ref[prev, :]
    return jnp.concatenate([kprev, k_ref[rows, :]], axis=0), jnp.concatenate([vprev, v_ref[rows, :]], axis=0)


def _stack_slots(ref, j, group, KV):
    rows = pl.ds(j * WINDOW, WINDOW)
    return jnp.concatenate([ref[rows, pl.ds(KV * hh, KV)] for hh in range(group)], axis=0)


def _attn_exp(qs, kbd, sink_ref, mask):
    s = _nt(qs, kbd) * (HEAD_DIM ** -0.5)
    out = []
    for g in range(N_KV_HEADS):
        sg = jnp.where(mask, s[:, 2 * WINDOW * g:2 * WINDOW * (g + 1)], NEG_INF)
        sink = sink_ref[g]
        m = jnp.maximum(jnp.max(sg, axis=-1, keepdims=True), sink)
        out.append((jnp.exp(sg - m), jnp.exp(sink - m)))
    return out


def _spread_over_heads(cols, rows, KV):
    head = _kv_lane_head(rows, KV)
    out = jnp.zeros((rows, KV), F32)
    for g, col in enumerate(cols):
        out = jnp.where(head == g, col, out)
    return out


def _halo_rows_spec(tq, width, sub):
    return pl.BlockSpec((sub, width), lambda i: (jnp.maximum(i * (tq // sub) - 1, 0), 0))


def attn_fwd(q, k, v, sink_col, name, carry=None):
    T, D = q.shape
    KV = k.shape[1]
    group = D // KV
    tq = min(ROW_TILE, T)
    nsub = tq // WINDOW
    rows = group * WINDOW

    def body(q_ref, k_ref, kh_ref, v_ref, vh_ref, sink_ref, o_ref):
        base, first = _attn_masks(pl.program_id(0) == 0, rows)
        head = _kv_lane_head(N_KV_HEADS * 2 * WINDOW, KV)
        block = lax.broadcasted_iota(jnp.int32, head.shape, 0) // (2 * WINDOW)
        ones_bd = jnp.where(head == block, 1.0, 0.0).astype(BF16)
        for j in range(nsub):
            k_win, v_win = _kv_windows(k_ref, kh_ref, v_ref, vh_ref, j)
            parts = _attn_exp(_stack_slots(q_ref, j, group, KV), _block_diag(k_win), sink_ref, first if j == 0 else base)
            p = jnp.concatenate([pg.astype(BF16) for pg, _ in parts], axis=1)
            both = _dot(p, jnp.concatenate([_block_diag(v_win), ones_bd], axis=1))
            denom = both[:, KV:] + _spread_over_heads([es for _, es in parts], rows, KV)
            out = (both[:, :KV] / denom).astype(BF16)
            for hh in range(group):
                o_ref[pl.ds(j * WINDOW, WINDOW), pl.ds(KV * hh, KV)] = out[hh * WINDOW:(hh + 1) * WINDOW]

    (o,), carried = _call(
        body, name=name, grid=(T // tq,),
        out_shape=(jax.ShapeDtypeStruct((T, D), BF16),),
        in_specs=[_rows(tq, D), _rows(tq, KV), _halo_rows_spec(tq, KV, WINDOW), _rows(tq, KV),
                  _halo_rows_spec(tq, KV, WINDOW), _resident(sink_col.shape)],
        out_specs=(_rows(tq, D),), args=(q, k, k, v, v, sink_col), carry=carry)
    return o, carried


def attn_bwd(q, k, v, do, sink_col, cos, sin_signed, name):
    T, D = q.shape
    KV = k.shape[1]
    group = D // KV
    tq = min(ROW_TILE, T)
    nsub = tq // WINDOW
    nt = T // tq
    scale = HEAD_DIM ** -0.5

    def rev(i):
        return nt - 1 - i

    def body(q_ref, k_ref, kh_ref, v_ref, vh_ref, do_ref, sink_ref, cos_ref, sin_ref,
             dq_ref, dk_ref, dv_ref, dsink_ref, dq_acc, dk_acc, dv_acc, carry_k, carry_v):
        i = pl.program_id(0)

        @pl.when(i == 0)
        def _():
            carry_k[...] = jnp.zeros_like(carry_k)
            carry_v[...] = jnp.zeros_like(carry_v)
            dsink_ref[...] = jnp.zeros_like(dsink_ref)

        dk_acc[...] = jnp.zeros_like(dk_acc)
        dv_acc[...] = jnp.zeros_like(dv_acc)
        base, first = _attn_masks(rev(i) == 0, group * WINDOW)
        for j in range(nsub):
            k_win, v_win = _kv_windows(k_ref, kh_ref, v_ref, vh_ref, j)
            kbd, vbd = _block_diag(k_win), _block_diag(v_win)
            qs, dos = _stack_slots(q_ref, j, group, KV), _stack_slots(do_ref, j, group, KV)
            dp = _nt(dos, vbd)
            probs16, ds16 = [], []
            for g, (pg, es) in enumerate(_attn_exp(qs, kbd, sink_ref, first if j == 0 else base)):
                inv = 1.0 / (jnp.sum(pg, axis=-1, keepdims=True) + es)
                probs = pg * inv
                dpg = dp[:, 2 * WINDOW * g:2 * WINDOW * (g + 1)]
                delta = jnp.sum(probs * dpg, axis=-1, keepdims=True)
                probs16.append(probs.astype(BF16))
                ds16.append((probs * (dpg - delta) * scale).astype(BF16))
                dsk = -(es * inv * delta)
                for hh in range(group):
                    tot = jnp.sum(dsk[hh * WINDOW:(hh + 1) * WINDOW], axis=0, keepdims=True)
                    dsink_ref[pl.ds(group * g + hh, 1), :] += jnp.broadcast_to(tot, (1, LANES))
            ds = jnp.concatenate(ds16, axis=1)
            dqs = _dot(ds, kbd)
            for hh in range(group):
                dq_acc[pl.ds(j * WINDOW, WINDOW), pl.ds(KV * hh, KV)] = dqs[hh * WINDOW:(hh + 1) * WINDOW]
            keys = pl.ds(j * WINDOW, 2 * WINDOW)
            dk_acc[keys, :] += _diag_blocks_sum(_tn(ds, qs), 2 * WINDOW)
            dv_acc[keys, :] += _diag_blocks_sum(_tn(jnp.concatenate(probs16, axis=1), dos), 2 * WINDOW)

        tail = pl.ds(tq, WINDOW)
        dk_acc[tail, :] += carry_k[...]
        dv_acc[tail, :] += carry_v[...]
        carry_k[...] = dk_acc[pl.ds(0, WINDOW), :]
        carry_v[...] = dv_acc[pl.ds(0, WINDOW), :]
        cos, sin = cos_ref[...], sin_ref[...]
        dq_ref[...] = _rope_bwd(dq_acc[...], cos, sin).astype(BF16)
        dk_ref[...] = _rope_bwd(dk_acc[pl.ds(WINDOW, tq), :], cos, sin).astype(BF16)
        dv_ref[...] = dv_acc[pl.ds(WINDOW, tq), :].astype(BF16)

    def rrows(n):
        return pl.BlockSpec((tq, n), lambda i: (rev(i), 0))

    def rhalo(n):
        return pl.BlockSpec((WINDOW, n), lambda i: (jnp.maximum(rev(i) * nsub - 1, 0), 0))

    return pl.pallas_call(
        body, name=name, grid=(nt,),
        out_shape=(jax.ShapeDtypeStruct((T, D), BF16), jax.ShapeDtypeStruct((T, KV), BF16),
                   jax.ShapeDtypeStruct((T, KV), BF16), jax.ShapeDtypeStruct((D // HEAD_DIM, LANES), F32)),
        in_specs=[rrows(D), rrows(KV), rhalo(KV), rrows(KV), rhalo(KV), rrows(D), _resident(sink_col.shape),
                  rrows(LANES), rrows(LANES)],
        out_specs=(rrows(D), rrows(KV), rrows(KV), _acc_spec((D // HEAD_DIM, LANES))),
        scratch_shapes=[pltpu.VMEM((tq, D), F32), pltpu.VMEM((WINDOW + tq, KV), F32), pltpu.VMEM((WINDOW + tq, KV), F32),
                        pltpu.VMEM((WINDOW, KV), F32), pltpu.VMEM((WINDOW, KV), F32)],
        compiler_params=_params("arbitrary"),
    )(q, k, k, v, v, do, sink_col, cos, sin_signed)


def _ln_stats(u):
    mu = jnp.mean(u, axis=-1, keepdims=True)
    d = u - mu
    rstd = lax.rsqrt(jnp.mean(d * d, axis=-1, keepdims=True) + LN_EPS)
    return d * rstd, rstd


def _lag_taps(b, K):
    return [(a, K - 1 - (SUBLANES * a + b)) for a in range(-(-K // SUBLANES)) if SUBLANES * a + b <= K - 1]


def _conv_chunks(tm, D, chunk):
    def rows(c, carry):
        r0 = pl.multiple_of(c * CONV_ROWS, CONV_ROWS)
        for l0 in range(0, D, CONV_LANES):
            chunk(r0, pl.ds(l0, CONV_LANES))
        return carry
    lax.fori_loop(0, tm // CONV_ROWS, rows, 0)


def _conv_causal(buf, w_ref, bias_ref, out_ref, tm, D, K):
    def chunk(r0, lanes):
        acc = jnp.broadcast_to(bias_ref[:, lanes], (CONV_ROWS, CONV_LANES))
        for b in range(SUBLANES):
            y = None
            for a, k in _lag_taps(b, K):
                start = pl.multiple_of(r0 + CONV_HALO - SUBLANES * (a + 1), SUBLANES)
                t = buf[pl.ds(start, CONV_ROWS + SUBLANES), lanes] * w_ref[pl.ds(k, 1), lanes]
                y = t if y is None else y + t
            acc = acc + y[SUBLANES - b:SUBLANES - b + CONV_ROWS]
        out_ref[pl.ds(r0, CONV_ROWS), lanes] = acc
    _conv_chunks(tm, D, chunk)


def _conv_anticausal(dbuf, w_ref, out_ref, tm, D, K):
    def chunk(r0, lanes):
        acc = jnp.zeros((CONV_ROWS, CONV_LANES), F32)
        for b in range(SUBLANES):
            y = None
            for a, k in _lag_taps(b, K):
                start = pl.multiple_of(r0 + SUBLANES * a, SUBLANES)
                t = dbuf[pl.ds(start, CONV_ROWS + SUBLANES), lanes] * w_ref[pl.ds(k, 1), lanes]
                y = t if y is None else y + t
            acc = acc + y[b:b + CONV_ROWS]
        out_ref[pl.ds(r0, CONV_ROWS), lanes] = acc
    _conv_chunks(tm, D, chunk)


def _conv_tap_grads(dbuf, ubuf, acc_ref, tm, D, K):
    reach = SUBLANES * (-(-K // SUBLANES) - 1)

    def chunk(r0, lanes):
        d = dbuf[pl.ds(r0, CONV_ROWS), lanes]
        around = ubuf[pl.ds(pl.multiple_of(r0 + CONV_HALO - reach - SUBLANES, SUBLANES), CONV_ROWS + reach + SUBLANES), lanes]
        for b in range(SUBLANES):
            shifted = around[SUBLANES - b:SUBLANES - b + CONV_ROWS + reach]
            for a, k in _lag_taps(b, K):
                prod = d * shifted[reach - SUBLANES * a:reach - SUBLANES * a + CONV_ROWS]
                part = prod[0:SUBLANES]
                for i in range(1, CONV_ROWS // SUBLANES):
                    part = part + prod[SUBLANES * i:SUBLANES * (i + 1)]
                acc_ref[k, :, lanes] += part
    _conv_chunks(tm, D, chunk)


def mix_out_fwd(x, u0, o, sgc, sgt, dw_w, dw_b, ln_g, ln_b, wcp, wo, wout, name):
    T, D = x.shape
    tm = min(ROW_TILE, T)
    K = dw_w.shape[0]

    def body(x_ref, u_ref, uh_ref, o_ref, sgc_ref, sgt_ref, w_ref, b_ref, lg_ref, lb_ref, wcp_ref, wo_ref, wout_ref,
             x2_ref, u1_ref, co_ref, ao_ref, mg_ref, buf, conv):
        keep = (pl.program_id(0) > 0).astype(F32)
        buf[pl.ds(0, CONV_HALO), :] = uh_ref[...].astype(F32) * keep
        buf[pl.ds(CONV_HALO, tm), :] = u_ref[...].astype(F32)
        _conv_causal(buf, w_ref, b_ref, conv, tm, D, K)
        acc = conv[...]
        u1_ref[...] = acc.astype(BF16)
        xhat, _ = _ln_stats(acc)
        u2 = xhat * lg_ref[...] + lb_ref[...]
        u3 = (u2 * _sigmoid(u2)).astype(BF16)
        co = _dot(u3, wcp_ref[...])
        ao = _dot(o_ref[...], wo_ref[...])
        co_ref[...] = co.astype(BF16)
        ao_ref[...] = ao.astype(BF16)
        merged = (sgc_ref[...].astype(F32) * co + sgt_ref[...].astype(F32) * ao).astype(BF16)
        mg_ref[...] = merged
        x2_ref[...] = x_ref[...] + _dot(merged, wout_ref[...])

    big = jax.ShapeDtypeStruct((T, D), BF16)
    vec = _resident((1, D))
    return pl.pallas_call(
        body, name=name, grid=(T // tm,),
        out_shape=(jax.ShapeDtypeStruct((T, D), F32), big, big, big, big),
        in_specs=[_rows(tm, D), _rows(tm, D), _halo_rows_spec(tm, D, CONV_HALO), _rows(tm, D), _rows(tm, D), _rows(tm, D),
                  _resident((K, D)), vec, vec, vec, _resident((D, D)), _resident((D, D)), _resident((D, D))],
        out_specs=(_rows(tm, D),) * 5,
        scratch_shapes=[pltpu.VMEM((CONV_HALO + tm, D), F32), pltpu.VMEM((tm, D), F32)],
        compiler_params=_params("arbitrary"),
    )(x, u0, u0, o, sgc, sgt, dw_w, dw_b, ln_g, ln_b, wcp, wo, wout)


def mix_out_bwd(dx2, u1, co, ao, sgc, sgt, ln_g, ln_b, wcp, wo, wout, name, carry=None):
    T, D = dx2.shape
    tm = min(ROW_TILE, T)

    def body(dx_ref, u1_ref, co_ref, ao_ref, sgc_ref, sgt_ref, lg_ref, lb_ref, wcp_ref, wo_ref, wout_ref,
             dgc_ref, dgt_ref, do_ref, du1_ref, dco_ref, dao_ref, u3_ref, sums_ref):
        dm = _nt(dx_ref[...].astype(BF16), wout_ref[...])
        sgc, sgt = sgc_ref[...].astype(F32), sgt_ref[...].astype(F32)
        dco = (dm * sgc).astype(BF16)
        dao = (dm * sgt).astype(BF16)
        dgc = dm * co_ref[...].astype(F32) * sgc * (1.0 - sgc)
        dgt = dm * ao_ref[...].astype(F32) * sgt * (1.0 - sgt)
        dco_ref[...] = dco
        dao_ref[...] = dao
        dgc_ref[...] = dgc.astype(BF16)
        dgt_ref[...] = dgt.astype(BF16)
        do_ref[...] = _nt(dao, wo_ref[...]).astype(BF16)
        du3 = _nt(dco, wcp_ref[...])
        xhat, rstd = _ln_stats(u1_ref[...].astype(F32))
        g = lg_ref[...]
        u2 = xhat * g + lb_ref[...]
        su = _sigmoid(u2)
        u3_ref[...] = (u2 * su).astype(BF16)
        du2 = du3 * (su * (1.0 + u2 * (1.0 - su)))
        dxh = du2 * g
        du1 = rstd * (dxh - jnp.mean(dxh, axis=-1, keepdims=True) - xhat * jnp.mean(dxh * xhat, axis=-1, keepdims=True))
        du1_ref[...] = du1.astype(BF16)

        @pl.when(pl.program_id(0) == 0)
        def _():
            sums_ref[...] = jnp.zeros_like(sums_ref)
        for r, val in enumerate((dgc, dgt, du2 * xhat, du2, du1)):
            sums_ref[pl.ds(r, 1), :] += jnp.sum(val, axis=0, keepdims=True)

    big = jax.ShapeDtypeStruct((T, D), BF16)
    vec = _resident((1, D))
    return _call(
        body, name=name, grid=(T // tm,),
        out_shape=(big,) * 7 + (jax.ShapeDtypeStruct((8, D), F32),),
        in_specs=[_rows(tm, D)] * 6 + [vec, vec, _resident((D, D)), _resident((D, D)), _resident((D, D))],
        out_specs=(_rows(tm, D),) * 7 + (_acc_spec((8, D)),),
        args=(dx2, u1, co, ao, sgc, sgt, ln_g, ln_b, wcp, wo, wout), carry=carry)


def conv_bwd(du1, u0, ga, gb, dw_w, name, carry=None):
    T, D = du1.shape
    tm = min(ROW_TILE, T)
    nt = T // tm
    K = dw_w.shape[0]
    per = tm // CONV_HALO

    def body(d_ref, dn_ref, u_ref, uh_ref, ga_ref, gb_ref, w_ref, dga_ref, dgb_ref, dw_ref, dbuf, ubuf, du0_buf, taps):
        i = pl.program_id(0)
        dbuf[pl.ds(0, tm), :] = d_ref[...].astype(F32)
        dbuf[pl.ds(tm, CONV_HALO), :] = dn_ref[...].astype(F32) * (i < nt - 1).astype(F32)
        ubuf[pl.ds(0, CONV_HALO), :] = uh_ref[...].astype(F32) * (i > 0).astype(F32)
        ubuf[pl.ds(CONV_HALO, tm), :] = u_ref[...].astype(F32)

        @pl.when(i == 0)
        def _():
            taps[...] = jnp.zeros_like(taps)

        _conv_anticausal(dbuf, w_ref, du0_buf, tm, D, K)
        _conv_tap_grads(dbuf, ubuf, taps, tm, D, K)
        du0 = du0_buf[...]
        ga, gb = ga_ref[...].astype(F32), gb_ref[...].astype(F32)
        sg = _sigmoid(gb)
        dga_ref[...] = (du0 * sg).astype(BF16)
        dgb_ref[...] = (du0 * ga * sg * (1.0 - sg)).astype(BF16)

        @pl.when(i == nt - 1)
        def _():
            for k in range(K):
                dw_ref[pl.ds(k, 1), :] = jnp.sum(taps[k], axis=0, keepdims=True)

    nxt = pl.BlockSpec((CONV_HALO, D), lambda i: (jnp.minimum((i + 1) * per, nt * per - 1), 0))
    big = jax.ShapeDtypeStruct((T, D), BF16)
    return _call(
        body, name=name, grid=(nt,),
        out_shape=(big, big, jax.ShapeDtypeStruct((K, D), F32)),
        in_specs=[_rows(tm, D), nxt, _rows(tm, D), _halo_rows_spec(tm, D, CONV_HALO), _rows(tm, D), _rows(tm, D),
                  _resident((K, D))],
        out_specs=(_rows(tm, D), _rows(tm, D), _acc_spec((K, D))),
        scratch_shapes=[pltpu.VMEM((tm + CONV_HALO, D), F32), pltpu.VMEM((CONV_HALO + tm, D), F32), pltpu.VMEM((tm, D), F32),
                        pltpu.VMEM((K, SUBLANES, D), F32)],
        args=(du1, du1, u0, u0, ga, gb, dw_w), carry=carry)


def mix_in_bwd(dx2, x, gain, wint, wq, pieces, name):
    T, D = x.shape
    tm = min(ROW_TILE, T)
    widths = [p.shape[1] for p in pieces]
    offs = _w_in_rows(D)

    def body(dx2_ref, x_ref, g_ref, w_ref, wq_ref, *rest):
        piece_refs, (dx_ref, dg_ref) = rest[:len(pieces)], rest[len(pieces):]
        dh = None
        for n, (p_ref, off, w) in enumerate(zip(piece_refs, offs, widths)):
            t = _dot(p_ref[...], wq_ref[...] if n == 2 else w_ref[pl.ds(off, w), :])
            dh = t if dh is None else dh + t
        dx, dgt = _rms_bwd(x_ref[...], g_ref[...], dh)
        dx_ref[...] = dx2_ref[...] + dx

        @pl.when(pl.program_id(0) == 0)
        def _():
            dg_ref[...] = jnp.zeros_like(dg_ref)
        dg_ref[...] += jnp.sum(dgt, axis=0, keepdims=True)

    return pl.pallas_call(
        body, name=name, grid=(T // tm,),
        out_shape=(jax.ShapeDtypeStruct((T, D), F32), jax.ShapeDtypeStruct((1, D), F32)),
        in_specs=[_rows(tm, D), _rows(tm, D), _resident((1, D)), _resident(wint.shape), _resident(wq.shape)]
        + [_rows(tm, w) for w in widths],
        out_specs=(_rows(tm, D), _acc_spec((1, D))),
        compiler_params=_params("arbitrary"),
    )(dx2, x, gain, wint, wq, *pieces)


def final_loss(x, gain, target, name):
    T, D = x.shape
    tm = min(ROW_TILE, T)

    def body(x_ref, g_ref, t_ref, loss_ref, dx_ref, dg_ref):
        x, g = x_ref[...], g_ref[...]
        err = _rms_fwd(x, g) - t_ref[...]
        dx, dgt = _rms_bwd(x, g, err * (1.0 / D))
        dx_ref[...] = dx

        @pl.when(pl.program_id(0) == 0)
        def _():
            dg_ref[...] = jnp.zeros_like(dg_ref)
            loss_ref[...] = jnp.zeros_like(loss_ref)
        dg_ref[...] += jnp.sum(dgt, axis=0, keepdims=True)
        per_token = jnp.sum(err * err, axis=-1, keepdims=True) * (0.5 / D)
        loss_ref[...] += jnp.broadcast_to(jnp.sum(per_token, axis=0, keepdims=True), (1, LANES))

    return pl.pallas_call(
        body, name=name, grid=(T // tm,),
        out_shape=(jax.ShapeDtypeStruct((1, LANES), F32), jax.ShapeDtypeStruct((T, D), F32),
                   jax.ShapeDtypeStruct((1, D), F32)),
        in_specs=[_rows(tm, D), _resident((1, D)), _rows(tm, D)],
        out_specs=(_acc_spec((1, LANES)), _rows(tm, D), _acc_spec((1, D))),
        compiler_params=_params("arbitrary"),
    )(x, gain, target)


def rope_tables(positions):
    half = HEAD_DIM // 2
    inv_freq = ROPE_THETA ** (-jnp.arange(half, dtype=F32) / half)
    ang = positions.astype(F32)[:, None] * inv_freq
    cos, sin = jnp.cos(ang), jnp.sin(ang)
    reps = LANES // HEAD_DIM
    return jnp.tile(jnp.concatenate([cos, cos], axis=-1), (1, reps)), jnp.tile(jnp.concatenate([-sin, sin], axis=-1), (1, reps))


def _place():
    return lax.axis_index("x"), lax.axis_index("y"), lax.axis_index("c")


def all_gather(block, name):
    def body(x_ref, out_ref, send_sems, recv_sems, local_sem):
        x, y, c = _place()
        me, sibling = (x, y, c), (x, y, 1 - c)
        chips = [(1 - x, y), (x, 1 - y), (1 - x, 1 - y)]

        def rows(px, py, pc):
            return out_ref.at[4 * px + 2 * py + pc]

        def copy(k, block_of, to, src=None):
            return pltpu.make_async_remote_copy(
                src_ref=rows(*block_of) if src is None else src, dst_ref=rows(*block_of),
                send_sem=send_sems.at[k], recv_sem=recv_sems.at[k], device_id=to, device_id_type=MESH)

        mine = pltpu.make_async_copy(x_ref, rows(*me), local_sem)
        mine.start()
        first = [copy(0, me, sibling, src=x_ref)]
        first += [copy(1 + j, me, (*chip, c), src=x_ref) for j, chip in enumerate(chips)]
        for cp in first:
            cp.start()
        passed = [copy(4 + j, (*chip, c), sibling) for j, chip in enumerate(chips)]
        for j, chip in enumerate(chips):
            copy(1 + j, (*chip, c), me).wait_recv()
            passed[j].start()
        copy(0, sibling, me).wait_recv()
        for j, chip in enumerate(chips):
            copy(4 + j, (*chip, 1 - c), me).wait_recv()
        for cp in first + passed:
            cp.wait_send()
        mine.wait()

    return pl.pallas_call(
        body, name=name, out_shape=jax.ShapeDtypeStruct((N_DEV,) + block.shape, block.dtype),
        in_specs=[_ANY], out_specs=_ANY,
        scratch_shapes=[pltpu.SemaphoreType.DMA((7,)), pltpu.SemaphoreType.DMA((7,)), pltpu.SemaphoreType.DMA],
    )(block)


def _chips_across(x, y):
    return [(1 - x, y), (x, 1 - y), (1 - x, 1 - y)]


def _dev_index(x, y, c):
    return 4 * x + 2 * y + c


def gather_send(block):
    def copies(in_refs, out_refs, sems):
        (x_ref,), (out_ref,), (send, recv, local) = in_refs, out_refs, sems
        x, y, c = _place()
        targets = [(x, y, 1 - c)] + [(*chip, c) for chip in _chips_across(x, y)]
        outgoing = [pltpu.make_async_remote_copy(src_ref=x_ref, dst_ref=out_ref.at[_dev_index(x, y, c)], send_sem=send.at[k],
                                                 recv_sem=recv.at[k], device_id=t, device_id_type=MESH)
                    for k, t in enumerate(targets)]
        incoming = [pltpu.make_async_remote_copy(src_ref=x_ref, dst_ref=out_ref.at[_dev_index(*t)], send_sem=send.at[k],
                                                 recv_sem=recv.at[k], device_id=t, device_id_type=MESH)
                    for k, t in enumerate(targets)]
        return outgoing, incoming, pltpu.make_async_copy(x_ref, out_ref.at[_dev_index(x, y, c)], local)

    def start(*refs):
        outgoing, _, mine = copies(*refs)
        mine.start()
        for cp in outgoing:
            cp.start()

    def finish(*refs):
        outgoing, incoming, mine = copies(*refs)
        for cp in incoming:
            cp.wait_recv()
        for cp in outgoing:
            cp.wait_send()
        mine.wait()

    return Carry(ins=(block,), out_shapes=(jax.ShapeDtypeStruct((N_DEV,) + block.shape, block.dtype),), aliases={},
                 sems=(pltpu.SemaphoreType.DMA((4,)), pltpu.SemaphoreType.DMA((4,)), pltpu.SemaphoreType.DMA),
                 start=start, finish=finish)


def gather_forward(gathered):
    def copies(in_refs, out_refs, sems):
        (buf,), (send, recv) = out_refs, sems
        x, y, c = _place()
        outgoing, incoming = [], []
        for k, chip in enumerate(_chips_across(x, y)):
            rows = buf.at[_dev_index(*chip, c)]
            outgoing.append(pltpu.make_async_remote_copy(src_ref=rows, dst_ref=rows, send_sem=send.at[k], recv_sem=recv.at[k],
                                                         device_id=(x, y, 1 - c), device_id_type=MESH))
            theirs = buf.at[_dev_index(*chip, 1 - c)]
            incoming.append(pltpu.make_async_remote_copy(src_ref=theirs, dst_ref=theirs, send_sem=send.at[k],
                                                         recv_sem=recv.at[k], device_id=(x, y, 1 - c), device_id_type=MESH))
        return outgoing, incoming

    def start(*refs):
        for cp in copies(*refs)[0]:
            cp.start()

    def finish(*refs):
        outgoing, incoming = copies(*refs)
        for cp in incoming:
            cp.wait_recv()
        for cp in outgoing:
            cp.wait_send()

    return Carry(ins=(gathered,), out_shapes=(jax.ShapeDtypeStruct(gathered.shape, gathered.dtype),), aliases={0: 0},
                 sems=(pltpu.SemaphoreType.DMA((3,)), pltpu.SemaphoreType.DMA((3,))), start=start, finish=finish)


def swap_halves(by_core):
    n = len(by_core)

    def copies(in_refs, out_refs, sems):
        send, recv = sems
        x, y, c = _place()
        return [pltpu.make_async_remote_copy(src_ref=a.at[:, 1 - c], dst_ref=r, send_sem=send.at[i], recv_sem=recv.at[i],
                                             device_id=(x, y, 1 - c), device_id_type=MESH)
                for i, (a, r) in enumerate(zip(in_refs, out_refs))]

    def start(*refs):
        for cp in copies(*refs):
            cp.start()

    def finish(*refs):
        for cp in copies(*refs):
            cp.wait()

    shapes = tuple(jax.ShapeDtypeStruct((a.shape[0],) + a.shape[2:], a.dtype) for a in by_core)
    return Carry(ins=tuple(by_core), out_shapes=shapes, aliases={},
                 sems=(pltpu.SemaphoreType.DMA((n,)), pltpu.SemaphoreType.DMA((n,))), start=start, finish=finish)


def exchange_between_chips(by_chip):
    n = len(by_chip)

    def copies(in_refs, out_refs, sems):
        send, recv = sems
        x, y, c = _place()
        out = []
        for i, (s, r) in enumerate(zip(in_refs, out_refs)):
            for k, (tx, ty) in enumerate(_chips_across(x, y)):
                out.append(pltpu.make_async_remote_copy(
                    src_ref=s.at[2 * tx + ty], dst_ref=r.at[k], send_sem=send.at[3 * i + k], recv_sem=recv.at[3 * i + k],
                    device_id=(tx, ty, c), device_id_type=MESH))
        return out

    def start(*refs):
        for cp in copies(*refs):
            cp.start()

    def finish(*refs):
        for cp in copies(*refs):
            cp.wait()

    shapes = tuple(jax.ShapeDtypeStruct((3,) + a.shape[1:], a.dtype) for a in by_chip)
    return Carry(ins=tuple(by_chip), out_shapes=shapes, aliases={},
                 sems=(pltpu.SemaphoreType.DMA((3 * n,)), pltpu.SemaphoreType.DMA((3 * n,))), start=start, finish=finish)


def run_exchange(carry, name):
    n_in = len(carry.ins)
    n_out = len(carry.out_shapes)

    def body(*refs):
        parts = refs[:n_in], refs[n_in:n_in + n_out], refs[n_in + n_out:]
        carry.start(*parts)
        carry.finish(*parts)

    return pl.pallas_call(
        body, name=name, out_shape=tuple(carry.out_shapes), in_specs=[_ANY] * n_in, out_specs=(_ANY,) * n_out,
        scratch_shapes=list(carry.sems), input_output_aliases=dict(carry.aliases),
    )(*carry.ins)


def pair_sum(my_core, by_core, received, name):
    n = len(by_core)

    def body(core_ref, *refs):
        for a_ref, b_ref, o_ref in zip(refs[:n], refs[n:2 * n], refs[2 * n:]):
            o_ref[0] = (a_ref[0, 0].astype(F32) + b_ref[0].astype(F32)).astype(BF16)

    mine = [pl.BlockSpec((1, 1) + a.shape[2:], lambda j, core: (j, core[0], 0, 0)) for a in by_core]
    theirs = [pl.BlockSpec((1,) + r.shape[1:], lambda j, core: (j, 0, 0)) for r in received]
    return pl.pallas_call(
        body, name=name, out_shape=tuple(jax.ShapeDtypeStruct(r.shape, BF16) for r in received),
        grid_spec=pltpu.PrefetchScalarGridSpec(num_scalar_prefetch=1, grid=(by_core[0].shape[0],), in_specs=mine + theirs,
                                               out_specs=tuple(theirs)),
        compiler_params=_params("arbitrary"),
    )(my_core, *by_core, *received)


def chip_sum(my_chip, by_chip, received, name):
    n = len(by_chip)

    def body(chip_ref, *refs):
        for a_ref, b_ref, o_ref in zip(refs[:n], refs[n:2 * n], refs[2 * n:]):
            acc = a_ref[0].astype(F32)
            for k in range(3):
                acc = acc + b_ref[k].astype(F32)
            o_ref[...] = acc

    own = [pl.BlockSpec((1,) + a.shape[1:], lambda i, chip: (chip[0], 0, 0)) for a in by_chip]
    got = [pl.BlockSpec(r.shape, lambda i, chip: (0, 0, 0)) for r in received]
    outs = tuple(pl.BlockSpec(a.shape[1:], lambda i, chip: (0, 0)) for a in by_chip)
    return pl.pallas_call(
        body, name=name, out_shape=tuple(jax.ShapeDtypeStruct(a.shape[1:], F32) for a in by_chip),
        grid_spec=pltpu.PrefetchScalarGridSpec(num_scalar_prefetch=1, grid=(1,), in_specs=own + got, out_specs=outs),
        compiler_params=_params("arbitrary"),
    )(my_chip, *by_chip, *received)


def _adamw_math(w, g, m, v):
    m = ADAM_B1 * m + (1.0 - ADAM_B1) * g
    v = ADAM_B2 * v + (1.0 - ADAM_B2) * (g * g)
    m_hat = m / (1.0 - ADAM_B1 ** ADAM_STEP)
    v_hat = v / (1.0 - ADAM_B2 ** ADAM_STEP)
    delta = -ADAM_LR * (m_hat / (jnp.sqrt(v_hat) + ADAM_EPS) + ADAM_WD * w)
    return delta, m, v


def adamw(ws, gs, ms, vs, name):
    n = len(ws)

    def body(*refs):
        ins, outs = refs[:4 * n], refs[4 * n:]
        for t in range(n):
            delta, m, v = _adamw_math(ins[t][...], ins[n + t][...], ins[2 * n + t][...], ins[3 * n + t][...])
            outs[3 * t][...] = delta
            outs[3 * t + 1][...] = m
            outs[3 * t + 2][...] = v

    shapes = []
    for w in ws:
        shapes += [jax.ShapeDtypeStruct(w.shape, F32)] * 3
    res = pl.pallas_call(body, name=name, out_shape=tuple(shapes), compiler_params=_params())(*ws, *gs, *ms, *vs)
    return [tuple(res[3 * t:3 * t + 3]) for t in range(n)]


def adamw_replicated(w, partials, m, v, name):
    def body(w_ref, p_ref, m_ref, v_ref, g_ref, d_ref, mo_ref, vo_ref):
        g = p_ref[0]
        for k in range(1, N_DEV):
            g = g + p_ref[k]
        g_ref[...] = g
        d_ref[...], mo_ref[...], vo_ref[...] = _adamw_math(w_ref[...], g, m_ref[...], v_ref[...])

    shape = jax.ShapeDtypeStruct(w.shape, F32)
    return pl.pallas_call(body, name=name, out_shape=(shape,) * 4, compiler_params=_params())(w, partials, m, v)


PACK_COLS = 1024
PACK_ROW_ALIGN = 16

SHARDED = {"ffn1_w_gate": 1, "ffn1_w_up": 1, "ffn1_w_down": 0, "w_in": 1, "conv_dw_w": 1, "conv_w_proj": 0, "attn_w_o": 0,
           "w_out": 0, "ffn2_w_gate": 1, "ffn2_w_up": 1, "ffn2_w_down": 0}
REPLICATED = ("ffn1_norm", "mix_norm", "conv_dw_b", "conv_ln_g", "conv_ln_b", "ffn2_norm", "final_norm", "gate_b", "attn_sinks")
WEIGHT_ORDER = ("ffn1_norm", "ffn1_w_gate", "ffn1_w_up", "ffn1_w_down", "mix_norm", "w_in", "conv_dw_w", "conv_dw_b", "conv_ln_g",
                "conv_ln_b", "conv_w_proj", "attn_sinks", "attn_w_o", "gate_b", "w_out", "ffn2_norm", "ffn2_w_gate", "ffn2_w_up",
                "ffn2_w_down", "final_norm")


def _to_rows(flat, lead):
    n = flat.shape[-1]
    rows = -(-n // PACK_COLS)
    flat = jnp.pad(flat, [(0, 0)] * lead + [(0, rows * PACK_COLS - n)])
    return flat.reshape(flat.shape[:lead] + (rows, PACK_COLS))


def _pad_rows(a, axis):
    rows = a.shape[axis]
    pad = -rows % PACK_ROW_ALIGN
    widths = [(0, 0)] * a.ndim
    widths[axis] = (0, pad)
    return jnp.pad(a, widths)


def _from_rows(rows, shape):
    n = 1
    for s in shape:
        n *= s
    return rows.reshape(rows.shape[:-2] + (-1,))[..., :n].reshape(rows.shape[:-2] + tuple(shape))


def _pack_weights(parts):
    layout, off = [], 0
    for p in parts:
        layout.append((off, p.shape[0]))
        off += p.shape[0]
    return _pad_rows(jnp.concatenate(parts, axis=0), 0), layout


def _gathered_rows(gathered, off, rows):
    return gathered[:, off:off + rows].reshape(N_DEV * rows, gathered.shape[2])


def _heads_slot_major(rows):
    group = rows.shape[0] // (N_KV_HEADS * HEAD_DIM)
    return rows.reshape(N_KV_HEADS, group, HEAD_DIM, rows.shape[1]).transpose(1, 0, 2, 3).reshape(rows.shape)


def _heads_kv_major(rows):
    group = rows.shape[0] // (N_KV_HEADS * HEAD_DIM)
    return rows.reshape(group, N_KV_HEADS, HEAD_DIM, rows.shape[1]).transpose(1, 0, 2, 3).reshape(rows.shape)


def _by_core(full_rows):
    return full_rows.reshape((N_DEV // 2, 2, full_rows.shape[0] // N_DEV, full_rows.shape[1]))


def kernel(x, positions, ffn1_norm, ffn1_w_gate, ffn1_w_up, ffn1_w_down, mix_norm, w_in, conv_dw_w, conv_dw_b, conv_ln_g, conv_ln_b, conv_w_proj, attn_sinks, attn_w_o, gate_b, w_out, ffn2_norm, ffn2_w_gate, ffn2_w_up, ffn2_w_down, final_norm, loss_target, m_ffn1_norm, m_ffn1_w_gate, m_ffn1_w_up, m_ffn1_w_down, m_mix_norm, m_w_in, m_conv_dw_w, m_conv_dw_b, m_conv_ln_g, m_conv_ln_b, m_conv_w_proj, m_attn_sinks, m_attn_w_o, m_gate_b, m_w_out, m_ffn2_norm, m_ffn2_w_gate, m_ffn2_w_up, m_ffn2_w_down, m_final_norm, v_ffn1_norm, v_ffn1_w_gate, v_ffn1_w_up, v_ffn1_w_down, v_mix_norm, v_w_in, v_conv_dw_w, v_conv_dw_b, v_conv_ln_g, v_conv_ln_b, v_conv_w_proj, v_attn_sinks, v_attn_w_o, v_gate_b, v_w_out, v_ffn2_norm, v_ffn2_w_gate, v_ffn2_w_up, v_ffn2_w_down, v_final_norm):
    given = dict(locals())
    shapes = {n: given[n].shape for n in WEIGHT_ORDER}
    w = {n: given[n].reshape(given[n].shape[-2:]) if given[n].ndim == 3 else given[n].reshape(1, -1) for n in WEIGHT_ORDER}
    m = {n: given["m_" + n].reshape(w[n].shape) for n in WEIGHT_ORDER}
    v = {n: given["v_" + n].reshape(w[n].shape) for n in WEIGHT_ORDER}
    my_x, my_y, my_c = _place()
    my_core = my_c.astype(jnp.int32).reshape(1)
    my_chip = (2 * my_x + my_y).astype(jnp.int32).reshape(1)
    xs, target = x[0], loss_target[0]
    T, D = xs.shape
    KV = N_KV_HEADS * HEAD_DIM
    K = w["conv_dw_w"].shape[0]

    def t16(n):
        return w[n].T.astype(BF16)

    def r16(n):
        return w[n].astype(BF16)

    pack1, lay1 = _pack_weights([t16("ffn1_w_gate"), t16("ffn1_w_up"), r16("ffn1_w_down")])
    dw_bits = _to_rows(lax.bitcast_convert_type(w["conv_dw_w"], BF16).reshape(-1), 0)
    pack2, lay2 = _pack_weights([t16("w_in"), r16("conv_w_proj"), r16("attn_w_o"), r16("w_out"), dw_bits])
    pack3, lay3 = _pack_weights([t16("ffn2_w_gate"), t16("ffn2_w_up"), r16("ffn2_w_down")])
    cos, sin = rope_tables(positions[0])
    sink_col = jnp.repeat(w["attn_sinks"].reshape(-1), WINDOW).reshape(N_KV_HEADS, (D // KV) * WINDOW, 1)

    gath1 = all_gather(pack1, "gather_ffn1")
    wgt1, wut1, wd1 = (_gathered_rows(gath1, *l) for l in lay1)
    (x1, h1, a1, b1), (gath2,) = ffn_fwd(xs, w["ffn1_norm"], wgt1, wut1, wd1, "ffn1_fwd", carry=gather_send(pack2))
    (gath2,) = run_exchange(gather_forward(gath2), "gather_mix_forward")
    wint = _gathered_rows(gath2, *lay2[0])
    wq = _heads_slot_major(wint[2 * D:3 * D])
    wcp, wo, wout = (_gathered_rows(gath2, *l) for l in lay2[1:4])
    wo = _heads_slot_major(wo)
    o_dw, r_dw = lay2[4]
    dw_full = lax.bitcast_convert_type(_from_rows(gath2[:, o_dw:o_dw + r_dw], w["conv_dw_w"].shape + (2,)), F32)
    dw_full = dw_full.transpose(1, 0, 2).reshape(K, D)
    (h2, ga, gb, u0, q, sgc, sgt, kk, vv), (gath3,) = mix_in_fwd(x1, w["mix_norm"], wint, wq, w["gate_b"], cos, sin,
                                                                 "mix_in_fwd", carry=gather_send(pack3))
    o, (gath3,) = attn_fwd(q, kk, vv, sink_col, "attn_fwd", carry=gather_forward(gath3))
    x2, u1, co, ao, merged = mix_out_fwd(x1, u0, o, sgc, sgt, dw_full, w["conv_dw_b"], w["conv_ln_g"], w["conv_ln_b"],
                                         wcp, wo, wout, "mix_out_fwd")
    wgt2, wut2, wd2 = (_gathered_rows(gath3, *l) for l in lay3)
    (x3, h3, a2, b2), _ = ffn_fwd(x2, w["ffn2_norm"], wgt2, wut2, wd2, "ffn2_fwd")
    loss, dx3, d_final = final_loss(x3, w["final_norm"], target, "final_loss")

    small = {"final_norm": d_final}
    (da2, db2, s2, g02), _ = ffn_bwd_hidden(dx3, a2, b2, wd2, "ffn2_bwd_hidden")
    (dx2, small["ffn2_norm"]), _ = ffn_bwd_input(dx3, x2, w["ffn2_norm"], da2, db2, wgt2, wut2, "ffn2_bwd_input")
    core2 = [_by_core(wgrad(a, b, nm)[0]) for a, b, nm in ((da2, h3, "ffn2_dwg"), (db2, h3, "ffn2_dwu"), (s2, g02, "ffn2_dwd"))]
    (dgc, dgt, do, du1, dco, dao, u3, sums), recv2 = mix_out_bwd(dx2, u1, co, ao, sgc, sgt, w["conv_ln_g"], w["conv_ln_b"],
                                                                 wcp, wo, wout, "mix_out_bwd", carry=swap_halves(core2))
    chip2 = pair_sum(my_core, core2, recv2, "ffn2_grads_pair_sum")
    small["gate_b"] = jnp.concatenate([sums[0:1], sums[1:2]], axis=1)
    small["conv_ln_g"], small["conv_ln_b"], small["conv_dw_b"] = sums[2:3], sums[3:4], sums[4:5]
    g_wout = wgrad(merged, dx2, "dw_out")[0]
    g_wcp = wgrad(u3, dco, "dw_conv_proj")[0]
    g_wo = _heads_kv_major(wgrad(o, dao, "dw_attn_o")[0])
    (dga, dgb, g_dw), got2 = conv_bwd(du1, u0, ga, gb, dw_full, "conv_bwd", carry=exchange_between_chips(chip2))
    dq, dk, dv, dsink = attn_bwd(q, kk, vv, do, sink_col, cos, sin, "attn_bwd")
    small["attn_sinks"] = dsink[:, 0].reshape(1, -1)
    pieces = [dga, dgb, dq, dk, dv, dgc, dgt]
    dx1, small["mix_norm"] = mix_in_bwd(dx2, x1, w["mix_norm"], wint, wq, pieces, "mix_in_bwd")
    dwin = [wgrad(p, h2, "dw_in_%d" % n)[0] for n, p in enumerate(pieces)]
    dwin[2] = _heads_kv_major(dwin[2])
    corem = [_by_core(a) for a in (jnp.concatenate(dwin, axis=0), g_wcp, g_wo, g_wout)]

    (da1, db1, s1, g01), recvm = ffn_bwd_hidden(dx1, a1, b1, wd1, "ffn1_bwd_hidden", carry=swap_halves(corem))
    chipm = pair_sum(my_core, corem, recvm, "mix_grads_pair_sum")
    g1c, gotm_a = wgrad(s1, g01, "ffn1_dwd", carry=exchange_between_chips(chipm[:1]))
    g1a, gotm_b = wgrad(da1, h1, "ffn1_dwg", carry=exchange_between_chips(chipm[1:]))
    g1b = wgrad(db1, h1, "ffn1_dwu")[0]
    core1 = [_by_core(a) for a in (g1a, g1b, g1c)]
    chip1 = pair_sum(my_core, core1, run_exchange(swap_halves(core1), "ffn1_grads_swap"), "ffn1_grads_pair_sum")
    (grad_x, small["ffn1_norm"]), got1 = ffn_bwd_input(dx1, xs, w["ffn1_norm"], da1, db1, wgt1, wut1, "ffn1_bwd_input",
                                                       carry=exchange_between_chips(chip1))

    gs2 = chip_sum(my_chip, chip2, got2, "ffn2_grads_sum")
    gsm = chip_sum(my_chip, chipm, gotm_a + gotm_b, "mix_grads_sum")
    gs1 = chip_sum(my_chip, chip1, got1, "ffn1_grads_sum")
    grads_t = {"ffn1_w_gate": gs1[0], "ffn1_w_up": gs1[1], "ffn1_w_down": gs1[2],
               "ffn2_w_gate": gs2[0], "ffn2_w_up": gs2[1], "ffn2_w_down": gs2[2],
               "w_in": gsm[0], "conv_w_proj": gsm[1], "attn_w_o": gsm[2], "w_out": gsm[3]}
    grads = {}

    def pack_small(d, taps, extra):
        rows = [_to_rows(d[n].reshape(-1), 0) for n in REPLICATED] + [taps, _to_rows(extra.reshape(-1), 0)]
        return _pad_rows(jnp.concatenate(rows, axis=0), 0)

    zero, no_taps = jnp.zeros((1, LANES), F32), jnp.zeros((K, D), F32)
    shares = all_gather(pack_small(small, g_dw, loss), "gather_small_grads")
    g_s, d_s, m_s, v_s = adamw_replicated(pack_small(w, no_taps, zero), shares, pack_small(m, no_taps, zero),
                                          pack_small(v, no_taps, zero), "adamw_replicated")
    delta, new_m, new_v = {}, {}, {}
    off = 0
    for n in REPLICATED:
        r = -(-w[n].shape[1] // PACK_COLS)
        grads[n], delta[n], new_m[n], new_v[n] = (_from_rows(a[off:off + r], w[n].shape) for a in (g_s, d_s, m_s, v_s))
        off += r
    shard_cols = w["conv_dw_w"].shape[1]
    grads_t["conv_dw_w"] = lax.dynamic_slice_in_dim(g_s[off:off + K], _dev_index(my_x, my_y, my_c) * shard_cols, shard_cols,
                                                    axis=1)
    total_loss = g_s[off + K, 0]

    def like(a, ref):
        return a if a.shape == ref.shape else a.T

    groups = (("ffn1_w_gate", "ffn1_w_up", "ffn1_w_down"), ("ffn2_w_gate", "ffn2_w_up", "ffn2_w_down"),
              ("w_in", "conv_dw_w", "conv_w_proj", "attn_w_o", "w_out"))
    for k, names in enumerate(groups):
        res = adamw([like(w[n], grads_t[n]) for n in names], [grads_t[n] for n in names],
                    [like(m[n], grads_t[n]) for n in names], [like(v[n], grads_t[n]) for n in names], "adamw_%d" % k)
        for n, (d, mm, vv) in zip(names, res):
            grads[n], delta[n], new_m[n], new_v[n] = (like(a, w[n]) for a in (grads_t[n], d, mm, vv))

    out = [total_loss, grad_x[None]]
    for d in (grads, delta, new_m, new_v):
        out += [d[n].reshape(shapes[n]) for n in WEIGHT_ORDER]
    return tuple(out)
```

```python
import functools
from typing import Callable, NamedTuple

import jax
import jax.numpy as jnp
from jax import lax
from jax.experimental import pallas as pl
from jax.experimental.pallas import tpu as pltpu

F32, BF16 = jnp.float32, jnp.bfloat16

HEAD_DIM = 64
N_KV_HEADS = 4
WINDOW = 128
CONV_WIDTH = 31
ROPE_THETA = 10000.0
EPS = 1e-6
LN_EPS = 1e-5
NEG_INF = -1e30
ADAM_LR, ADAM_B1, ADAM_B2, ADAM_EPS, ADAM_WD, ADAM_STEP = 0.001, 0.9, 0.999, 1e-08, 0.01, 10

N_DEV = 8
LANES = 128
SUBLANES = 8
CONV_HALO = 32
CONV_ROWS, CONV_LANES = 64, 256
ROW_TILE = 512
FFN_CHUNK = 256
WGRAD_TILE_ELEMS = 2 ** 22
WGRAD_TILE_ROWS = 2048
VMEM_LIMIT = 56 * 2 ** 20
MESH = pl.DeviceIdType.MESH


def _params(*sem):
    return pltpu.CompilerParams(dimension_semantics=sem or None, vmem_limit_bytes=VMEM_LIMIT)


def _resident(shape):
    zeros = (0,) * len(shape)
    return pl.BlockSpec(shape, lambda *_: zeros, pipeline_mode=pl.Buffered(1))


def _rows(tm, n):
    return pl.BlockSpec((tm, n), lambda i: (i, 0))


def _acc_spec(shape):
    zeros = (0,) * len(shape)
    return pl.BlockSpec(shape, lambda *_: zeros)


_ANY = pl.BlockSpec(memory_space=pl.ANY)


class Carry(NamedTuple):
    ins: tuple
    out_shapes: tuple
    aliases: dict
    sems: tuple
    start: Callable
    finish: Callable


def _call(body, *, name, grid, in_specs, out_specs, out_shape, args, scratch_shapes=(), carry=None):
    n_in, n_out, n_scr = len(in_specs), len(out_specs), len(scratch_shapes)
    params = _params(*(("arbitrary",) * len(grid)))
    if carry is None:
        res = pl.pallas_call(body, name=name, grid=grid, in_specs=list(in_specs), out_specs=tuple(out_specs),
                             out_shape=tuple(out_shape), scratch_shapes=list(scratch_shapes), compiler_params=params)(*args)
        return tuple(res), ()
    c_in, c_out = len(carry.ins), len(carry.out_shapes)

    def wrapped(*refs):
        ins, c_ins = refs[:n_in], refs[n_in:n_in + c_in]
        p = n_in + c_in
        outs, c_outs = refs[p:p + n_out], refs[p + n_out:p + n_out + c_out]
        p += n_out + c_out
        scr, c_sems = refs[p:p + n_scr], refs[p + n_scr:]
        ids = [pl.program_id(d) for d in range(len(grid))]
        first = functools.reduce(jnp.logical_and, [i == 0 for i in ids])
        last = functools.reduce(jnp.logical_and, [i == n - 1 for i, n in zip(ids, grid)])

        @pl.when(first)
        def _():
            carry.start(c_ins, c_outs, c_sems)

        body(*ins, *outs, *scr)

        @pl.when(last)
        def _():
            carry.finish(c_ins, c_outs, c_sems)

    res = pl.pallas_call(
        wrapped, name=name, grid=grid, in_specs=list(in_specs) + [_ANY] * c_in, out_specs=tuple(out_specs) + (_ANY,) * c_out,
        out_shape=tuple(out_shape) + tuple(carry.out_shapes), scratch_shapes=list(scratch_shapes) + list(carry.sems),
        input_output_aliases={n_in + i: n_out + o for i, o in carry.aliases.items()}, compiler_params=params,
    )(*args, *carry.ins)
    return tuple(res[:n_out]), tuple(res[n_out:])


def _nt(a, b):
    return lax.dot_general(a, b, (((1,), (1,)), ((), ())), preferred_element_type=F32)


def _tn(a, b):
    return lax.dot_general(a, b, (((0,), (0,)), ((), ())), preferred_element_type=F32)


def _dot(a, b):
    return jnp.dot(a, b, preferred_element_type=F32)


def _sigmoid(x):
    return 1.0 / (1.0 + jnp.exp(-x))


def _rms_fwd(x, g):
    r = lax.rsqrt(jnp.mean(x * x, axis=-1, keepdims=True) + EPS)
    return (x * r) * g


def _rms_bwd(x, g, dy):
    r = lax.rsqrt(jnp.mean(x * x, axis=-1, keepdims=True) + EPS)
    xhat = x * r
    dyg = dy * g
    dx = r * (dyg - xhat * jnp.mean(dyg * xhat, axis=-1, keepdims=True))
    return dx, dy * xhat


def _rot_half(x):
    lane = lax.broadcasted_iota(jnp.int32, (x.shape[0], LANES), 1)
    first = (lane % HEAD_DIM) < (HEAD_DIM // 2)
    out = []
    for s in range(x.shape[1] // LANES):
        xs = x[:, LANES * s:LANES * (s + 1)]
        out.append(jnp.where(first, pltpu.roll(xs, LANES - HEAD_DIM // 2, 1), pltpu.roll(xs, HEAD_DIM // 2, 1)))
    return out[0] if len(out) == 1 else jnp.concatenate(out, axis=1)


def _tile_lanes(t, width):
    return t if width == LANES else jnp.concatenate([t] * (width // LANES), axis=1)


def _rope_fwd(x, cos, sin_signed):
    w = x.shape[1]
    return x * _tile_lanes(cos, w) + _rot_half(x) * _tile_lanes(sin_signed, w)


def _rope_bwd(dy, cos, sin_signed):
    w = dy.shape[1]
    return dy * _tile_lanes(cos, w) + _rot_half(dy * _tile_lanes(sin_signed, w))


def _ffn_rows(x, g_ref, wg_ref, wu_ref, wd_ref, h_ref, a_ref, b_ref, acc_ref):
    F = wg_ref.shape[0]
    h = _rms_fwd(x, g_ref[...]).astype(BF16)
    h_ref[...] = h
    for c in range(F // FFN_CHUNK):
        cs = pl.ds(c * FFN_CHUNK, FFN_CHUNK)
        a = _nt(h, wg_ref[cs, :])
        b = _nt(h, wu_ref[cs, :])
        a_ref[:, cs] = a.astype(BF16)
        b_ref[:, cs] = b.astype(BF16)
        s = (a * _sigmoid(a) * b).astype(BF16)
        y = _dot(s, wd_ref[cs, :])
        if c == 0:
            acc_ref[...] = y
        else:
            acc_ref[...] += y
    return x + 0.5 * acc_ref[...]


def ffn_fwd(x, gain, wgt, wut, wd, name, carry=None):
    T, D = x.shape
    F = wgt.shape[0]
    tm = min(ROW_TILE, T)

    def body(x_ref, g_ref, wg_ref, wu_ref, wd_ref, xo_ref, h_ref, a_ref, b_ref, acc_ref):
        xo_ref[...] = _ffn_rows(x_ref[...], g_ref, wg_ref, wu_ref, wd_ref, h_ref, a_ref, b_ref, acc_ref)

    return _call(
        body, name=name, grid=(T // tm,),
        out_shape=(jax.ShapeDtypeStruct((T, D), F32), jax.ShapeDtypeStruct((T, D), BF16),
                   jax.ShapeDtypeStruct((T, F), BF16), jax.ShapeDtypeStruct((T, F), BF16)),
        in_specs=[_rows(tm, D), _resident((1, D)), _resident((F, D)), _resident((F, D)), _resident((F, D))],
        out_specs=(_rows(tm, D), _rows(tm, D), _rows(tm, F), _rows(tm, F)),
        scratch_shapes=[pltpu.VMEM((tm, D), F32)], args=(x, gain, wgt, wut, wd), carry=carry)


def ffn_fwd_loss(x, gain, wgt, wut, wd, final_gain, target, name):
    T, D = x.shape
    F = wgt.shape[0]
    tm = min(ROW_TILE, T)

    def body(x_ref, g_ref, wg_ref, wu_ref, wd_ref, gf_ref, t_ref, loss_ref, dx_ref, dg_ref, h_ref, a_ref, b_ref, acc_ref):
        xo = _ffn_rows(x_ref[...], g_ref, wg_ref, wu_ref, wd_ref, h_ref, a_ref, b_ref, acc_ref)
        gf = gf_ref[...]
        err = _rms_fwd(xo, gf) - t_ref[...]
        dx, dgt = _rms_bwd(xo, gf, err * (1.0 / D))
        dx_ref[...] = dx

        @pl.when(pl.program_id(0) == 0)
        def _():
            dg_ref[...] = jnp.zeros_like(dg_ref)
            loss_ref[...] = jnp.zeros_like(loss_ref)
        dg_ref[...] += jnp.sum(dgt, axis=0, keepdims=True)
        per_token = jnp.sum(err * err, axis=-1, keepdims=True) * (0.5 / D)
        loss_ref[...] += jnp.broadcast_to(jnp.sum(per_token, axis=0, keepdims=True), (1, LANES))

    return pl.pallas_call(
        body, name=name, grid=(T // tm,),
        out_shape=(jax.ShapeDtypeStruct((1, LANES), F32), jax.ShapeDtypeStruct((T, D), F32), jax.ShapeDtypeStruct((1, D), F32),
                   jax.ShapeDtypeStruct((T, D), BF16), jax.ShapeDtypeStruct((T, F), BF16), jax.ShapeDtypeStruct((T, F), BF16)),
        in_specs=[_rows(tm, D), _resident((1, D)), _resident((F, D)), _resident((F, D)), _resident((F, D)), _resident((1, D)),
                  _rows(tm, D)],
        out_specs=(_acc_spec((1, LANES)), _rows(tm, D), _acc_spec((1, D)), _rows(tm, D), _rows(tm, F), _rows(tm, F)),
        scratch_shapes=[pltpu.VMEM((tm, D), F32)], compiler_params=_params("arbitrary"),
    )(x, gain, wgt, wut, wd, final_gain, target)


def ffn_bwd_hidden(dxo, a, b, wd, name, carry=None):
    T, D = dxo.shape
    F = wd.shape[0]
    tm = min(ROW_TILE, T)
    fc = FFN_CHUNK

    def hidden_body(dxo_ref, a_ref, b_ref, wd_ref, da_ref, db_ref, s_ref, g0_ref):
        g0 = (0.5 * dxo_ref[...]).astype(BF16)
        g0_ref[...] = g0
        for c in range(F // fc):
            cs = pl.ds(c * fc, fc)
            ds = _nt(g0, wd_ref[cs, :])
            a = a_ref[:, cs].astype(F32)
            bb = b_ref[:, cs].astype(F32)
            sa = _sigmoid(a)
            silu = a * sa
            da_ref[:, cs] = (ds * bb * (sa * (1.0 + a * (1.0 - sa)))).astype(BF16)
            db_ref[:, cs] = (ds * silu).astype(BF16)
            s_ref[:, cs] = (silu * bb).astype(BF16)

    wide = jax.ShapeDtypeStruct((T, F), BF16)
    return _call(
        hidden_body, name=name, grid=(T // tm,),
        out_shape=(wide, wide, wide, jax.ShapeDtypeStruct((T, D), BF16)),
        in_specs=[_rows(tm, D), _rows(tm, F), _rows(tm, F), _resident((F, D))],
        out_specs=(_rows(tm, F), _rows(tm, F), _rows(tm, F), _rows(tm, D)), args=(dxo, a, b, wd), carry=carry)


def ffn_bwd_input(dxo, x, gain, da, db, wg, wu, name, carry=None):
    T, D = x.shape
    F = wg.shape[0]
    tm = min(ROW_TILE, T)

    def input_body(dxo_ref, x_ref, g_ref, da_ref, db_ref, wg_ref, wu_ref, dx_ref, dg_ref):
        dh = _dot(da_ref[...], wg_ref[...]) + _dot(db_ref[...], wu_ref[...])
        dx, dgt = _rms_bwd(x_ref[...], g_ref[...], dh)
        dx_ref[...] = dxo_ref[...] + dx

        @pl.when(pl.program_id(0) == 0)
        def _():
            dg_ref[...] = jnp.zeros_like(dg_ref)
        dg_ref[...] += jnp.sum(dgt, axis=0, keepdims=True)

    return _call(
        input_body, name=name, grid=(T // tm,),
        out_shape=(jax.ShapeDtypeStruct((T, D), F32), jax.ShapeDtypeStruct((1, D), F32)),
        in_specs=[_rows(tm, D), _rows(tm, D), _resident((1, D)), _rows(tm, F), _rows(tm, F), _resident((F, D)),
                  _resident((F, D))],
        out_specs=(_rows(tm, D), _acc_spec((1, D))), args=(dxo, x, gain, da, db, wg, wu), carry=carry)


def wgrad(a, b, name, carry=None):
    T, M = a.shape
    N = b.shape[1]
    tk = ROW_TILE
    while 2 * tk * M <= WGRAD_TILE_ELEMS and 2 * tk <= WGRAD_TILE_ROWS:
        tk *= 2
    tk = min(tk, T)
    nk = T // tk

    def body(a_ref, b_ref, o_ref, acc_ref):
        k = pl.program_id(0)
        part = _tn(a_ref[...].astype(BF16), b_ref[...].astype(BF16))

        @pl.when(k == 0)
        def _():
            acc_ref[...] = part

        @pl.when(k > 0)
        def _():
            acc_ref[...] += part

        @pl.when(k == nk - 1)
        def _():
            o_ref[...] = acc_ref[...].astype(BF16)

    (out,), carried = _call(
        body, name=name, grid=(nk,), out_shape=(jax.ShapeDtypeStruct((M, N), BF16),),
        in_specs=[_rows(tk, M), _rows(tk, N)], out_specs=(_acc_spec((M, N)),),
        scratch_shapes=[pltpu.VMEM((M, N), F32)], args=(a, b), carry=carry)
    return out, carried


def wgrad_stacked(pieces, b, name, moves=None):
    T, N = b.shape
    n = len(pieces)
    widths = [p.shape[1] for p in pieces]
    offs = [sum(widths[:i]) for i in range(n)]
    M = sum(widths)
    tk = ROW_TILE
    while 2 * tk * M <= WGRAD_TILE_ELEMS and 2 * tk <= WGRAD_TILE_ROWS:
        tk *= 2
    tk = min(tk, T)
    nk = T // tk
    moves = moves or [(0, 0, M)]

    def body(*refs):
        a_refs, b_ref, o_ref, acc_ref = refs[:n], refs[n], refs[n + 1], refs[n + 2]
        k = pl.program_id(0)

        @pl.when(k == 0)
        def _():
            acc_ref[...] = jnp.zeros_like(acc_ref)
        bt = b_ref[...].astype(BF16)
        for a_ref, off, width in zip(a_refs, offs, widths):
            acc_ref[pl.ds(off, width), :] += _tn(a_ref[...].astype(BF16), bt)

        @pl.when(k == nk - 1)
        def _():
            for to, start, rows in moves:
                o_ref[pl.ds(to, rows), :] = acc_ref[pl.ds(start, rows), :].astype(BF16)

    return pl.pallas_call(
        body, name=name, grid=(nk,), out_shape=jax.ShapeDtypeStruct((M, N), BF16),
        in_specs=[_rows(tk, width) for width in widths] + [_rows(tk, N)], out_specs=_acc_spec((M, N)),
        scratch_shapes=[pltpu.VMEM((M, N), F32)], compiler_params=_params("arbitrary"),
    )(*pieces, b)


def _w_in_rows(D):
    KV = N_KV_HEADS * HEAD_DIM
    return 0, D, 2 * D, 3 * D, 3 * D + KV, 3 * D + 2 * KV, 4 * D + 2 * KV


def mix_in_fwd(x, gain, wint, wq, gate_b, cos, sin_signed, name, carry=None):
    T, D = x.shape
    KV = N_KV_HEADS * HEAD_DIM
    tm = min(ROW_TILE, T)
    o_ga, o_gb, _, o_k, o_v, o_gc, o_gt = _w_in_rows(D)

    def body(x_ref, g_ref, w_ref, wq_ref, gb_ref, cos_ref, sin_ref,
             h_ref, ga_ref, gb_out_ref, u0_ref, q_ref, sgc_ref, sgt_ref, k_ref, v_ref):
        h = _rms_fwd(x_ref[...], g_ref[...]).astype(BF16)
        h_ref[...] = h
        cos, sin = cos_ref[...], sin_ref[...]
        ga = _nt(h, w_ref[pl.ds(o_ga, D), :])
        gb = _nt(h, w_ref[pl.ds(o_gb, D), :])
        ga_ref[...] = ga.astype(BF16)
        gb_out_ref[...] = gb.astype(BF16)
        u0_ref[...] = (ga * _sigmoid(gb)).astype(BF16)
        q = _nt(h, wq_ref[...])
        q_ref[...] = _rope_fwd(q, cos, sin).astype(BF16)
        gc = _nt(h, w_ref[pl.ds(o_gc, D), :]) + gb_ref[:, pl.ds(0, D)]
        sgc_ref[...] = _sigmoid(gc).astype(BF16)
        gt = _nt(h, w_ref[pl.ds(o_gt, D), :]) + gb_ref[:, pl.ds(D, D)]
        sgt_ref[...] = _sigmoid(gt).astype(BF16)
        k = _nt(h, w_ref[pl.ds(o_k, KV), :])
        k_ref[...] = _rope_fwd(k, cos, sin).astype(BF16)
        v_ref[...] = _nt(h, w_ref[pl.ds(o_v, KV), :]).astype(BF16)

    big = jax.ShapeDtypeStruct((T, D), BF16)
    small = jax.ShapeDtypeStruct((T, KV), BF16)
    return _call(
        body, name=name, grid=(T // tm,),
        out_shape=(big, big, big, big, big, big, big, small, small),
        in_specs=[_rows(tm, D), _resident((1, D)), _resident(wint.shape), _resident(wq.shape), _resident((1, 2 * D)),
                  _rows(tm, LANES), _rows(tm, LANES)],
        out_specs=(_rows(tm, D),) * 7 + (_rows(tm, KV),) * 2,
        args=(x, gain, wint, wq, gate_b, cos, sin_signed), carry=carry)


def _attn_masks(tile_is_first, rows):
    qi = lax.broadcasted_iota(jnp.int32, (rows, 2 * WINDOW), 0) % WINDOW
    c = lax.broadcasted_iota(jnp.int32, (rows, 2 * WINDOW), 1)
    base = (c > qi) & (c <= qi + WINDOW)
    first_key = jnp.where(tile_is_first, WINDOW, 0)
    return base, base & (c >= first_key)


def _kv_lane_head(rows, width):
    return lax.broadcasted_iota(jnp.int32, (rows, width), 1) // HEAD_DIM


def _block_diag(win):
    head = _kv_lane_head(*win.shape)
    zero = jnp.zeros_like(win)
    return jnp.concatenate([jnp.where(head == g, win, zero) for g in range(N_KV_HEADS)], axis=0)


def _diag_blocks_sum(bd, keys):
    head = _kv_lane_head(keys, bd.shape[1])
    out = jnp.zeros((keys, bd.shape[1]), F32)
    for g in range(N_KV_HEADS):
        out = jnp.where(head == g, bd[g * keys:(g + 1) * keys], out)
    return out


def _kv_windows(k_ref, kh_ref, v_ref, vh_ref, j):
    rows = pl.ds(j * WINDOW, WINDOW)
    if j == 0:
        kprev, vprev = kh_ref[...], vh_ref[...]
    else:
        prev = pl.ds((j - 1) * WINDOW, WINDOW)
        kprev, vprev = k_ref[prev, :], v_ref[prev, :]
    return jnp.concatenate([kprev, k_ref[rows, :]], axis=0), jnp.concatenate([vprev, v_ref[rows, :]], axis=0)


def _stack_slots(ref, j, group, KV):
    rows = pl.ds(j * WINDOW, WINDOW)
    return jnp.concatenate([ref[rows, pl.ds(KV * hh, KV)] for hh in range(group)], axis=0)


def _attn_exp(qs, kbd, sink_ref, mask):
    s = _nt(qs, kbd) * (HEAD_DIM ** -0.5)
    out = []
    for g in range(N_KV_HEADS):
        sg = jnp.where(mask, s[:, 2 * WINDOW * g:2 * WINDOW * (g + 1)], NEG_INF)
        sink = sink_ref[g]
        m = jnp.maximum(jnp.max(sg, axis=-1, keepdims=True), sink)
        out.append((jnp.exp(sg - m), jnp.exp(sink - m)))
    return out


def _spread_over_heads(cols, rows, KV):
    head = _kv_lane_head(rows, KV)
    out = jnp.zeros((rows, KV), F32)
    for g, col in enumerate(cols):
        out = jnp.where(head == g, col, out)
    return out


def _halo_rows_spec(tq, width, sub):
    return pl.BlockSpec((sub, width), lambda i: (jnp.maximum(i * (tq // sub) - 1, 0), 0))


def attn_fwd(q, k, v, sink_col, name, carry=None):
    T, D = q.shape
    KV = k.shape[1]
    group = D // KV
    tq = min(ROW_TILE, T)
    nsub = tq // WINDOW
    rows = group * WINDOW

    def body(q_ref, k_ref, kh_ref, v_ref, vh_ref, sink_ref, o_ref):
        base, first = _attn_masks(pl.program_id(0) == 0, rows)
        head = _kv_lane_head(N_KV_HEADS * 2 * WINDOW, KV)
        block = lax.broadcasted_iota(jnp.int32, head.shape, 0) // (2 * WINDOW)
        ones_bd = jnp.where(head == block, 1.0, 0.0).astype(BF16)
        for j in range(nsub):
            k_win, v_win = _kv_windows(k_ref, kh_ref, v_ref, vh_ref, j)
            parts = _attn_exp(_stack_slots(q_ref, j, group, KV), _block_diag(k_win), sink_ref, first if j == 0 else base)
            p = jnp.concatenate([pg.astype(BF16) for pg, _ in parts], axis=1)
            both = _dot(p, jnp.concatenate([_block_diag(v_win), ones_bd], axis=1))
            denom = both[:, KV:] + _spread_over_heads([es for _, es in parts], rows, KV)
            out = (both[:, :KV] / denom).astype(BF16)
            for hh in range(group):
                o_ref[pl.ds(j * WINDOW, WINDOW), pl.ds(KV * hh, KV)] = out[hh * WINDOW:(hh + 1) * WINDOW]

    (o,), carried = _call(
        body, name=name, grid=(T // tq,),
        out_shape=(jax.ShapeDtypeStruct((T, D), BF16),),
        in_specs=[_rows(tq, D), _rows(tq, KV), _halo_rows_spec(tq, KV, WINDOW), _rows(tq, KV),
                  _halo_rows_spec(tq, KV, WINDOW), _resident(sink_col.shape)],
        out_specs=(_rows(tq, D),), args=(q, k, k, v, v, sink_col), carry=carry)
    return o, carried


def attn_bwd(q, k, v, do, sink_col, cos, sin_signed, name):
    T, D = q.shape
    KV = k.shape[1]
    group = D // KV
    tq = min(ROW_TILE, T)
    nsub = tq // WINDOW
    nt = T // tq
    scale = HEAD_DIM ** -0.5

    def rev(i):
        return nt - 1 - i

    def body(q_ref, k_ref, kh_ref, v_ref, vh_ref, do_ref, sink_ref, cos_ref, sin_ref,
             dq_ref, dk_ref, dv_ref, dsink_ref, dq_acc, dk_acc, dv_acc, carry_k, carry_v):
        i = pl.program_id(0)

        @pl.when(i == 0)
        def _():
            carry_k[...] = jnp.zeros_like(carry_k)
            carry_v[...] = jnp.zeros_like(carry_v)
            dsink_ref[...] = jnp.zeros_like(dsink_ref)

        dk_acc[...] = jnp.zeros_like(dk_acc)
        dv_acc[...] = jnp.zeros_like(dv_acc)
        base, first = _attn_masks(rev(i) == 0, group * WINDOW)
        for j in range(nsub):
            k_win, v_win = _kv_windows(k_ref, kh_ref, v_ref, vh_ref, j)
            kbd, vbd = _block_diag(k_win), _block_diag(v_win)
            qs, dos = _stack_slots(q_ref, j, group, KV), _stack_slots(do_ref, j, group, KV)
            dp = _nt(dos, vbd)
            probs16, ds16 = [], []
            for g, (pg, es) in enumerate(_attn_exp(qs, kbd, sink_ref, first if j == 0 else base)):
                inv = 1.0 / (jnp.sum(pg, axis=-1, keepdims=True) + es)
                probs = pg * inv
                dpg = dp[:, 2 * WINDOW * g:2 * WINDOW * (g + 1)]
                delta = jnp.sum(probs * dpg, axis=-1, keepdims=True)
                probs16.append(probs.astype(BF16))
                ds16.append((probs * (dpg - delta) * scale).astype(BF16))
                dsk = -(es * inv * delta)
                for hh in range(group):
                    tot = jnp.sum(dsk[hh * WINDOW:(hh + 1) * WINDOW], axis=0, keepdims=True)
                    dsink_ref[pl.ds(group * g + hh, 1), :] += jnp.broadcast_to(tot, (1, LANES))
            ds = jnp.concatenate(ds16, axis=1)
            dqs = _dot(ds, kbd)
            for hh in range(group):
                dq_acc[pl.ds(j * WINDOW, WINDOW), pl.ds(KV * hh, KV)] = dqs[hh * WINDOW:(hh + 1) * WINDOW]
            keys = pl.ds(j * WINDOW, 2 * WINDOW)
            dk_acc[keys, :] += _diag_blocks_sum(_tn(ds, qs), 2 * WINDOW)
            dv_acc[keys, :] += _diag_blocks_sum(_tn(jnp.concatenate(probs16, axis=1), dos), 2 * WINDOW)

        tail = pl.ds(tq, WINDOW)
        dk_acc[tail, :] += carry_k[...]
        dv_acc[tail, :] += carry_v[...]
        carry_k[...] = dk_acc[pl.ds(0, WINDOW), :]
        carry_v[...] = dv_acc[pl.ds(0, WINDOW), :]
        cos, sin = cos_ref[...], sin_ref[...]
        dq_ref[...] = _rope_bwd(dq_acc[...], cos, sin).astype(BF16)
        dk_ref[...] = _rope_bwd(dk_acc[pl.ds(WINDOW, tq), :], cos, sin).astype(BF16)
        dv_ref[...] = dv_acc[pl.ds(WINDOW, tq), :].astype(BF16)

    def rrows(n):
        return pl.BlockSpec((tq, n), lambda i: (rev(i), 0))

    def rhalo(n):
        return pl.BlockSpec((WINDOW, n), lambda i: (jnp.maximum(rev(i) * nsub - 1, 0), 0))

    return pl.pallas_call(
        body, name=name, grid=(nt,),
        out_shape=(jax.ShapeDtypeStruct((T, D), BF16), jax.ShapeDtypeStruct((T, KV), BF16),
                   jax.ShapeDtypeStruct((T, KV), BF16), jax.ShapeDtypeStruct((D // HEAD_DIM, LANES), F32)),
        in_specs=[rrows(D), rrows(KV), rhalo(KV), rrows(KV), rhalo(KV), rrows(D), _resident(sink_col.shape),
                  rrows(LANES), rrows(LANES)],
        out_specs=(rrows(D), rrows(KV), rrows(KV), _acc_spec((D // HEAD_DIM, LANES))),
        scratch_shapes=[pltpu.VMEM((tq, D), F32), pltpu.VMEM((WINDOW + tq, KV), F32), pltpu.VMEM((WINDOW + tq, KV), F32),
                        pltpu.VMEM((WINDOW, KV), F32), pltpu.VMEM((WINDOW, KV), F32)],
        compiler_params=_params("arbitrary"),
    )(q, k, k, v, v, do, sink_col, cos, sin_signed)


def _ln_stats(u):
    mu = jnp.mean(u, axis=-1, keepdims=True)
    d = u - mu
    rstd = lax.rsqrt(jnp.mean(d * d, axis=-1, keepdims=True) + LN_EPS)
    return d * rstd, rstd


def _lag_taps(b, K):
    return [(a, K - 1 - (SUBLANES * a + b)) for a in range(-(-K // SUBLANES)) if SUBLANES * a + b <= K - 1]


def _conv_chunks(tm, D, chunk):
    def rows(c, carry):
        r0 = pl.multiple_of(c * CONV_ROWS, CONV_ROWS)
        for l0 in range(0, D, CONV_LANES):
            chunk(r0, pl.ds(l0, CONV_LANES))
        return carry
    lax.fori_loop(0, tm // CONV_ROWS, rows, 0)


def _conv_causal(buf, w_ref, bias_ref, out_ref, tm, D, K):
    def chunk(r0, lanes):
        acc = jnp.broadcast_to(bias_ref[:, lanes], (CONV_ROWS, CONV_LANES))
        for b in range(SUBLANES):
            y = None
            for a, k in _lag_taps(b, K):
                start = pl.multiple_of(r0 + CONV_HALO - SUBLANES * (a + 1), SUBLANES)
                t = buf[pl.ds(start, CONV_ROWS + SUBLANES), lanes] * w_ref[pl.ds(k, 1), lanes]
                y = t if y is None else y + t
            acc = acc + y[SUBLANES - b:SUBLANES - b + CONV_ROWS]
        out_ref[pl.ds(r0, CONV_ROWS), lanes] = acc
    _conv_chunks(tm, D, chunk)


def _conv_anticausal(dbuf, w_ref, out_ref, tm, D, K):
    def chunk(r0, lanes):
        acc = jnp.zeros((CONV_ROWS, CONV_LANES), F32)
        for b in range(SUBLANES):
            y = None
            for a, k in _lag_taps(b, K):
                start = pl.multiple_of(r0 + SUBLANES * a, SUBLANES)
                t = dbuf[pl.ds(start, CONV_ROWS + SUBLANES), lanes] * w_ref[pl.ds(k, 1), lanes]
                y = t if y is None else y + t
            acc = acc + y[b:b + CONV_ROWS]
        out_ref[pl.ds(r0, CONV_ROWS), lanes] = acc
    _conv_chunks(tm, D, chunk)


def _conv_tap_grads(dbuf, ubuf, acc_ref, tm, D, K):
    reach = SUBLANES * (-(-K // SUBLANES) - 1)

    def chunk(r0, lanes):
        d = dbuf[pl.ds(r0, CONV_ROWS), lanes]
        around = ubuf[pl.ds(pl.multiple_of(r0 + CONV_HALO - reach - SUBLANES, SUBLANES), CONV_ROWS + reach + SUBLANES), lanes]
        for b in range(SUBLANES):
            shifted = around[SUBLANES - b:SUBLANES - b + CONV_ROWS + reach]
            for a, k in _lag_taps(b, K):
                prod = d * shifted[reach - SUBLANES * a:reach - SUBLANES * a + CONV_ROWS]
                part = prod[0:SUBLANES]
                for i in range(1, CONV_ROWS // SUBLANES):
                    part = part + prod[SUBLANES * i:SUBLANES * (i + 1)]
                acc_ref[k, :, lanes] += part
    _conv_chunks(tm, D, chunk)


def mix_out_fwd(x, u0, o, sgc, sgt, dw_w, dw_b, ln_g, ln_b, wcp, wo, wout, name):
    T, D = x.shape
    tm = min(ROW_TILE, T)
    K = dw_w.shape[0]

    def body(x_ref, u_ref, uh_ref, o_ref, sgc_ref, sgt_ref, w_ref, b_ref, lg_ref, lb_ref, wcp_ref, wo_ref, wout_ref,
             x2_ref, u1_ref, co_ref, ao_ref, mg_ref, buf, conv):
        keep = (pl.program_id(0) > 0).astype(F32)
        buf[pl.ds(0, CONV_HALO), :] = uh_ref[...].astype(F32) * keep
        buf[pl.ds(CONV_HALO, tm), :] = u_ref[...].astype(F32)
        _conv_causal(buf, w_ref, b_ref, conv, tm, D, K)
        acc = conv[...]
        u1_ref[...] = acc.astype(BF16)
        xhat, _ = _ln_stats(acc)
        u2 = xhat * lg_ref[...] + lb_ref[...]
        u3 = (u2 * _sigmoid(u2)).astype(BF16)
        co = _dot(u3, wcp_ref[...])
        ao = _dot(o_ref[...], wo_ref[...])
        co_ref[...] = co.astype(BF16)
        ao_ref[...] = ao.astype(BF16)
        merged = (sgc_ref[...].astype(F32) * co + sgt_ref[...].astype(F32) * ao).astype(BF16)
        mg_ref[...] = merged
        x2_ref[...] = x_ref[...] + _dot(merged, wout_ref[...])

    big = jax.ShapeDtypeStruct((T, D), BF16)
    vec = _resident((1, D))
    return pl.pallas_call(
        body, name=name, grid=(T // tm,),
        out_shape=(jax.ShapeDtypeStruct((T, D), F32), big, big, big, big),
        in_specs=[_rows(tm, D), _rows(tm, D), _halo_rows_spec(tm, D, CONV_HALO), _rows(tm, D), _rows(tm, D), _rows(tm, D),
                  _resident((K, D)), vec, vec, vec, _resident((D, D)), _resident((D, D)), _resident((D, D))],
        out_specs=(_rows(tm, D),) * 5,
        scratch_shapes=[pltpu.VMEM((CONV_HALO + tm, D), F32), pltpu.VMEM((tm, D), F32)],
        compiler_params=_params("arbitrary"),
    )(x, u0, u0, o, sgc, sgt, dw_w, dw_b, ln_g, ln_b, wcp, wo, wout)


def mix_out_bwd(dx2, u1, co, ao, sgc, sgt, ln_g, ln_b, wcp, wo, wout, name, carry=None):
    T, D = dx2.shape
    tm = min(ROW_TILE, T)

    def body(dx_ref, u1_ref, co_ref, ao_ref, sgc_ref, sgt_ref, lg_ref, lb_ref, wcp_ref, wo_ref, wout_ref,
             dgc_ref, dgt_ref, do_ref, du1_ref, dco_ref, dao_ref, u3_ref, sums_ref):
        dm = _nt(dx_ref[...].astype(BF16), wout_ref[...])
        sgc, sgt = sgc_ref[...].astype(F32), sgt_ref[...].astype(F32)
        dco = (dm * sgc).astype(BF16)
        dao = (dm * sgt).astype(BF16)
        dgc = dm * co_ref[...].astype(F32) * sgc * (1.0 - sgc)
        dgt = dm * ao_ref[...].astype(F32) * sgt * (1.0 - sgt)
        dco_ref[...] = dco
        dao_ref[...] = dao
        dgc_ref[...] = dgc.astype(BF16)
        dgt_ref[...] = dgt.astype(BF16)
        do_ref[...] = _nt(dao, wo_ref[...]).astype(BF16)
        du3 = _nt(dco, wcp_ref[...])
        xhat, rstd = _ln_stats(u1_ref[...].astype(F32))
        g = lg_ref[...]
        u2 = xhat * g + lb_ref[...]
        su = _sigmoid(u2)
        u3_ref[...] = (u2 * su).astype(BF16)
        du2 = du3 * (su * (1.0 + u2 * (1.0 - su)))
        dxh = du2 * g
        du1 = rstd * (dxh - jnp.mean(dxh, axis=-1, keepdims=True) - xhat * jnp.mean(dxh * xhat, axis=-1, keepdims=True))
        du1_ref[...] = du1.astype(BF16)

        @pl.when(pl.program_id(0) == 0)
        def _():
            sums_ref[...] = jnp.zeros_like(sums_ref)
        for r, val in enumerate((dgc, dgt, du2 * xhat, du2, du1)):
            sums_ref[pl.ds(r, 1), :] += jnp.sum(val, axis=0, keepdims=True)

    big = jax.ShapeDtypeStruct((T, D), BF16)
    vec = _resident((1, D))
    return _call(
        body, name=name, grid=(T // tm,),
        out_shape=(big,) * 7 + (jax.ShapeDtypeStruct((8, D), F32),),
        in_specs=[_rows(tm, D)] * 6 + [vec, vec, _resident((D, D)), _resident((D, D)), _resident((D, D))],
        out_specs=(_rows(tm, D),) * 7 + (_acc_spec((8, D)),),
        args=(dx2, u1, co, ao, sgc, sgt, ln_g, ln_b, wcp, wo, wout), carry=carry)


def conv_bwd(du1, u0, ga, gb, dw_w, name, carry=None):
    T, D = du1.shape
    tm = min(ROW_TILE, T)
    nt = T // tm
    K = dw_w.shape[0]
    per = tm // CONV_HALO

    def body(d_ref, dn_ref, u_ref, uh_ref, ga_ref, gb_ref, w_ref, dga_ref, dgb_ref, dw_ref, dbuf, ubuf, du0_buf, taps):
        i = pl.program_id(0)
        dbuf[pl.ds(0, tm), :] = d_ref[...].astype(F32)
        dbuf[pl.ds(tm, CONV_HALO), :] = dn_ref[...].astype(F32) * (i < nt - 1).astype(F32)
        ubuf[pl.ds(0, CONV_HALO), :] = uh_ref[...].astype(F32) * (i > 0).astype(F32)
        ubuf[pl.ds(CONV_HALO, tm), :] = u_ref[...].astype(F32)

        @pl.when(i == 0)
        def _():
            taps[...] = jnp.zeros_like(taps)

        _conv_anticausal(dbuf, w_ref, du0_buf, tm, D, K)
        _conv_tap_grads(dbuf, ubuf, taps, tm, D, K)
        du0 = du0_buf[...]
        ga, gb = ga_ref[...].astype(F32), gb_ref[...].astype(F32)
        sg = _sigmoid(gb)
        dga_ref[...] = (du0 * sg).astype(BF16)
        dgb_ref[...] = (du0 * ga * sg * (1.0 - sg)).astype(BF16)

        @pl.when(i == nt - 1)
        def _():
            for k in range(K):
                dw_ref[pl.ds(k, 1), :] = jnp.sum(taps[k], axis=0, keepdims=True)

    nxt = pl.BlockSpec((CONV_HALO, D), lambda i: (jnp.minimum((i + 1) * per, nt * per - 1), 0))
    big = jax.ShapeDtypeStruct((T, D), BF16)
    return _call(
        body, name=name, grid=(nt,),
        out_shape=(big, big, jax.ShapeDtypeStruct((K, D), F32)),
        in_specs=[_rows(tm, D), nxt, _rows(tm, D), _halo_rows_spec(tm, D, CONV_HALO), _rows(tm, D), _rows(tm, D),
                  _resident((K, D))],
        out_specs=(_rows(tm, D), _rows(tm, D), _acc_spec((K, D))),
        scratch_shapes=[pltpu.VMEM((tm + CONV_HALO, D), F32), pltpu.VMEM((CONV_HALO + tm, D), F32), pltpu.VMEM((tm, D), F32),
                        pltpu.VMEM((K, SUBLANES, D), F32)],
        args=(du1, du1, u0, u0, ga, gb, dw_w), carry=carry)


def mix_in_bwd(dx2, x, gain, wint, wq, pieces, name):
    T, D = x.shape
    tm = min(ROW_TILE, T)
    widths = [p.shape[1] for p in pieces]
    offs = _w_in_rows(D)

    def body(dx2_ref, x_ref, g_ref, w_ref, wq_ref, *rest):
        piece_refs, (dx_ref, dg_ref) = rest[:len(pieces)], rest[len(pieces):]
        dh = None
        for n, (p_ref, off, w) in enumerate(zip(piece_refs, offs, widths)):
            t = _dot(p_ref[...], wq_ref[...] if n == 2 else w_ref[pl.ds(off, w), :])
            dh = t if dh is None else dh + t
        dx, dgt = _rms_bwd(x_ref[...], g_ref[...], dh)
        dx_ref[...] = dx2_ref[...] + dx

        @pl.when(pl.program_id(0) == 0)
        def _():
            dg_ref[...] = jnp.zeros_like(dg_ref)
        dg_ref[...] += jnp.sum(dgt, axis=0, keepdims=True)

    return pl.pallas_call(
        body, name=name, grid=(T // tm,),
        out_shape=(jax.ShapeDtypeStruct((T, D), F32), jax.ShapeDtypeStruct((1, D), F32)),
        in_specs=[_rows(tm, D), _rows(tm, D), _resident((1, D)), _resident(wint.shape), _resident(wq.shape)]
        + [_rows(tm, w) for w in widths],
        out_specs=(_rows(tm, D), _acc_spec((1, D))),
        compiler_params=_params("arbitrary"),
    )(dx2, x, gain, wint, wq, *pieces)


def final_loss(x, gain, target, name):
    T, D = x.shape
    tm = min(ROW_TILE, T)

    def body(x_ref, g_ref, t_ref, loss_ref, dx_ref, dg_ref):
        x, g = x_ref[...], g_ref[...]
        err = _rms_fwd(x, g) - t_ref[...]
        dx, dgt = _rms_bwd(x, g, err * (1.0 / D))
        dx_ref[...] = dx

        @pl.when(pl.program_id(0) == 0)
        def _():
            dg_ref[...] = jnp.zeros_like(dg_ref)
            loss_ref[...] = jnp.zeros_like(loss_ref)
        dg_ref[...] += jnp.sum(dgt, axis=0, keepdims=True)
        per_token = jnp.sum(err * err, axis=-1, keepdims=True) * (0.5 / D)
        loss_ref[...] += jnp.broadcast_to(jnp.sum(per_token, axis=0, keepdims=True), (1, LANES))

    return pl.pallas_call(
        body, name=name, grid=(T // tm,),
        out_shape=(jax.ShapeDtypeStruct((1, LANES), F32), jax.ShapeDtypeStruct((T, D), F32),
                   jax.ShapeDtypeStruct((1, D), F32)),
        in_specs=[_rows(tm, D), _resident((1, D)), _rows(tm, D)],
        out_specs=(_acc_spec((1, LANES)), _rows(tm, D), _acc_spec((1, D))),
        compiler_params=_params("arbitrary"),
    )(x, gain, target)


def rope_tables(positions):
    half = HEAD_DIM // 2
    inv_freq = ROPE_THETA ** (-jnp.arange(half, dtype=F32) / half)
    ang = positions.astype(F32)[:, None] * inv_freq
    cos, sin = jnp.cos(ang), jnp.sin(ang)
    reps = LANES // HEAD_DIM
    return jnp.tile(jnp.concatenate([cos, cos], axis=-1), (1, reps)), jnp.tile(jnp.concatenate([-sin, sin], axis=-1), (1, reps))


def _place():
    return lax.axis_index("x"), lax.axis_index("y"), lax.axis_index("c")


def all_gather(blocks, name):
    n = len(blocks)
    send = gather_send(blocks)
    forward = gather_forward(send.out_shapes)
    n_sems = len(send.sems)

    def body(*refs):
        ins, outs, sems = refs[:n], refs[n:2 * n], refs[2 * n:]
        send.start(ins, outs, sems[:n_sems])
        send.finish(ins, outs, sems[:n_sems])
        forward.start((), outs, sems[n_sems:])
        forward.finish((), outs, sems[n_sems:])

    return pl.pallas_call(
        body, name=name, out_shape=tuple(send.out_shapes), in_specs=[_ANY] * n, out_specs=(_ANY,) * n,
        scratch_shapes=list(send.sems) + list(forward.sems),
    )(*blocks)


def _chips_across(x, y):
    return [(1 - x, y), (x, 1 - y), (1 - x, 1 - y)]


def _dev_index(x, y, c):
    return 4 * x + 2 * y + c


def gather_send(blocks):
    n = len(blocks)

    def copies(in_refs, out_refs, sems):
        send, recv, local = sems
        x, y, c = _place()
        targets = [(x, y, 1 - c)] + [(*chip, c) for chip in _chips_across(x, y)]
        outgoing, incoming, mine = [], [], []
        for i, (x_ref, out_ref) in enumerate(zip(in_refs, out_refs)):
            for k, t in enumerate(targets):
                pair = dict(send_sem=send.at[4 * i + k], recv_sem=recv.at[4 * i + k], device_id=t, device_id_type=MESH)
                outgoing.append(pltpu.make_async_remote_copy(src_ref=x_ref, dst_ref=out_ref.at[_dev_index(x, y, c)], **pair))
                incoming.append(pltpu.make_async_remote_copy(src_ref=x_ref, dst_ref=out_ref.at[_dev_index(*t)], **pair))
            mine.append(pltpu.make_async_copy(x_ref, out_ref.at[_dev_index(x, y, c)], local.at[i]))
        return outgoing, incoming, mine

    def start(*refs):
        outgoing, _, mine = copies(*refs)
        for cp in mine + outgoing:
            cp.start()

    def finish(*refs):
        outgoing, incoming, mine = copies(*refs)
        for cp in incoming:
            cp.wait_recv()
        for cp in outgoing:
            cp.wait_send()
        for cp in mine:
            cp.wait()

    return Carry(ins=tuple(blocks), out_shapes=tuple(jax.ShapeDtypeStruct((N_DEV,) + b.shape, b.dtype) for b in blocks),
                 aliases={}, sems=(pltpu.SemaphoreType.DMA((4 * n,)), pltpu.SemaphoreType.DMA((4 * n,)),
                                   pltpu.SemaphoreType.DMA((n,))), start=start, finish=finish)


def gather_forward(gathered):
    n = len(gathered)

    def copies(in_refs, out_refs, sems):
        send, recv = sems
        x, y, c = _place()
        outgoing, incoming = [], []
        for i, buf in enumerate(out_refs):
            for k, chip in enumerate(_chips_across(x, y)):
                pair = dict(send_sem=send.at[3 * i + k], recv_sem=recv.at[3 * i + k], device_id=(x, y, 1 - c),
                            device_id_type=MESH)
                rows = buf.at[_dev_index(*chip, c)]
                outgoing.append(pltpu.make_async_remote_copy(src_ref=rows, dst_ref=rows, **pair))
                theirs = buf.at[_dev_index(*chip, 1 - c)]
                incoming.append(pltpu.make_async_remote_copy(src_ref=theirs, dst_ref=theirs, **pair))
        return outgoing, incoming

    def start(*refs):
        for cp in copies(*refs)[0]:
            cp.start()

    def finish(*refs):
        outgoing, incoming = copies(*refs)
        for cp in incoming:
            cp.wait_recv()
        for cp in outgoing:
            cp.wait_send()

    return Carry(ins=tuple(gathered), out_shapes=tuple(jax.ShapeDtypeStruct(g.shape, g.dtype) for g in gathered),
                 aliases={i: i for i in range(n)},
                 sems=(pltpu.SemaphoreType.DMA((3 * n,)), pltpu.SemaphoreType.DMA((3 * n,))), start=start, finish=finish)


def swap_halves(by_core):
    n = len(by_core)

    def copies(in_refs, out_refs, sems):
        send, recv = sems
        x, y, c = _place()
        return [pltpu.make_async_remote_copy(src_ref=a.at[:, 1 - c], dst_ref=r, send_sem=send.at[i], recv_sem=recv.at[i],
                                             device_id=(x, y, 1 - c), device_id_type=MESH)
                for i, (a, r) in enumerate(zip(in_refs, out_refs))]

    def start(*refs):
        for cp in copies(*refs):
            cp.start()

    def finish(*refs):
        for cp in copies(*refs):
            cp.wait()

    shapes = tuple(jax.ShapeDtypeStruct((a.shape[0],) + a.shape[2:], a.dtype) for a in by_core)
    return Carry(ins=tuple(by_core), out_shapes=shapes, aliases={},
                 sems=(pltpu.SemaphoreType.DMA((n,)), pltpu.SemaphoreType.DMA((n,))), start=start, finish=finish)


def exchange_between_chips(by_chip):
    n = len(by_chip)

    def copies(in_refs, out_refs, sems):
        send, recv = sems
        x, y, c = _place()
        out = []
        for i, (s, r) in enumerate(zip(in_refs, out_refs)):
            for k, (tx, ty) in enumerate(_chips_across(x, y)):
                out.append(pltpu.make_async_remote_copy(
                    src_ref=s.at[2 * tx + ty], dst_ref=r.at[k], send_sem=send.at[3 * i + k], recv_sem=recv.at[3 * i + k],
                    device_id=(tx, ty, c), device_id_type=MESH))
        return out

    def start(*refs):
        for cp in copies(*refs):
            cp.start()

    def finish(*refs):
        for cp in copies(*refs):
            cp.wait()

    shapes = tuple(jax.ShapeDtypeStruct((3,) + a.shape[1:], a.dtype) for a in by_chip)
    return Carry(ins=tuple(by_chip), out_shapes=shapes, aliases={},
                 sems=(pltpu.SemaphoreType.DMA((3 * n,)), pltpu.SemaphoreType.DMA((3 * n,))), start=start, finish=finish)


def run_exchange(carry, name):
    n_in = len(carry.ins)
    n_out = len(carry.out_shapes)

    def body(*refs):
        parts = refs[:n_in], refs[n_in:n_in + n_out], refs[n_in + n_out:]
        carry.start(*parts)
        carry.finish(*parts)

    return pl.pallas_call(
        body, name=name, out_shape=tuple(carry.out_shapes), in_specs=[_ANY] * n_in, out_specs=(_ANY,) * n_out,
        scratch_shapes=list(carry.sems), input_output_aliases=dict(carry.aliases),
    )(*carry.ins)


def pair_sum(my_core, by_core, received, name):
    n = len(by_core)

    def body(core_ref, *refs):
        for a_ref, b_ref, o_ref in zip(refs[:n], refs[n:2 * n], refs[2 * n:]):
            o_ref[0] = (a_ref[0, 0].astype(F32) + b_ref[0].astype(F32)).astype(BF16)

    mine = [pl.BlockSpec((1, 1) + a.shape[2:], lambda j, core: (j, core[0], 0, 0)) for a in by_core]
    theirs = [pl.BlockSpec((1,) + r.shape[1:], lambda j, core: (j, 0, 0)) for r in received]
    return pl.pallas_call(
        body, name=name, out_shape=tuple(jax.ShapeDtypeStruct(r.shape, BF16) for r in received),
        grid_spec=pltpu.PrefetchScalarGridSpec(num_scalar_prefetch=1, grid=(by_core[0].shape[0],), in_specs=mine + theirs,
                                               out_specs=tuple(theirs)),
        compiler_params=_params("arbitrary"),
    )(my_core, *by_core, *received)


def chip_sum(my_chip, by_chip, received, name):
    n = len(by_chip)

    def body(chip_ref, *refs):
        for a_ref, b_ref, o_ref in zip(refs[:n], refs[n:2 * n], refs[2 * n:]):
            acc = a_ref[0].astype(F32)
            for k in range(3):
                acc = acc + b_ref[k].astype(F32)
            o_ref[...] = acc

    own = [pl.BlockSpec((1,) + a.shape[1:], lambda i, chip: (chip[0], 0, 0)) for a in by_chip]
    got = [pl.BlockSpec(r.shape, lambda i, chip: (0, 0, 0)) for r in received]
    outs = tuple(pl.BlockSpec(a.shape[1:], lambda i, chip: (0, 0)) for a in by_chip)
    return pl.pallas_call(
        body, name=name, out_shape=tuple(jax.ShapeDtypeStruct(a.shape[1:], F32) for a in by_chip),
        grid_spec=pltpu.PrefetchScalarGridSpec(num_scalar_prefetch=1, grid=(1,), in_specs=own + got, out_specs=outs),
        compiler_params=_params("arbitrary"),
    )(my_chip, *by_chip, *received)


def _adamw_math(w, g, m, v):
    m = ADAM_B1 * m + (1.0 - ADAM_B1) * g
    v = ADAM_B2 * v + (1.0 - ADAM_B2) * (g * g)
    m_hat = m / (1.0 - ADAM_B1 ** ADAM_STEP)
    v_hat = v / (1.0 - ADAM_B2 ** ADAM_STEP)
    delta = -ADAM_LR * (m_hat / (jnp.sqrt(v_hat) + ADAM_EPS) + ADAM_WD * w)
    return delta, m, v


def adamw(ws, gs, ms, vs, name):
    n = len(ws)

    def body(*refs):
        ins, outs = refs[:4 * n], refs[4 * n:]
        for t in range(n):
            delta, m, v = _adamw_math(ins[t][...], ins[n + t][...], ins[2 * n + t][...], ins[3 * n + t][...])
            outs[3 * t][...] = delta
            outs[3 * t + 1][...] = m
            outs[3 * t + 2][...] = v

    shapes = []
    for w in ws:
        shapes += [jax.ShapeDtypeStruct(w.shape, F32)] * 3
    res = pl.pallas_call(body, name=name, out_shape=tuple(shapes), compiler_params=_params())(*ws, *gs, *ms, *vs)
    return [tuple(res[3 * t:3 * t + 3]) for t in range(n)]


def adamw_replicated(w, partials, m, v, name):
    def body(w_ref, p_ref, m_ref, v_ref, g_ref, d_ref, mo_ref, vo_ref):
        g = p_ref[0]
        for k in range(1, N_DEV):
            g = g + p_ref[k]
        g_ref[...] = g
        d_ref[...], mo_ref[...], vo_ref[...] = _adamw_math(w_ref[...], g, m_ref[...], v_ref[...])

    shape = jax.ShapeDtypeStruct(w.shape, F32)
    return pl.pallas_call(body, name=name, out_shape=(shape,) * 4, compiler_params=_params())(w, partials, m, v)


PACK_COLS = 1024
PACK_ROW_ALIGN = 16

SHARDED = {"ffn1_w_gate": 1, "ffn1_w_up": 1, "ffn1_w_down": 0, "w_in": 1, "conv_dw_w": 1, "conv_w_proj": 0, "attn_w_o": 0,
           "w_out": 0, "ffn2_w_gate": 1, "ffn2_w_up": 1, "ffn2_w_down": 0}
REPLICATED = ("ffn1_norm", "mix_norm", "conv_dw_b", "conv_ln_g", "conv_ln_b", "ffn2_norm", "final_norm", "gate_b", "attn_sinks")
WEIGHT_ORDER = ("ffn1_norm", "ffn1_w_gate", "ffn1_w_up", "ffn1_w_down", "mix_norm", "w_in", "conv_dw_w", "conv_dw_b", "conv_ln_g",
                "conv_ln_b", "conv_w_proj", "attn_sinks", "attn_w_o", "gate_b", "w_out", "ffn2_norm", "ffn2_w_gate", "ffn2_w_up",
                "ffn2_w_down", "final_norm")


def _to_rows(flat, lead):
    n = flat.shape[-1]
    rows = -(-n // PACK_COLS)
    flat = jnp.pad(flat, [(0, 0)] * lead + [(0, rows * PACK_COLS - n)])
    return flat.reshape(flat.shape[:lead] + (rows, PACK_COLS))


def _pad_rows(a, axis):
    rows = a.shape[axis]
    pad = -rows % PACK_ROW_ALIGN
    widths = [(0, 0)] * a.ndim
    widths[axis] = (0, pad)
    return jnp.pad(a, widths)


def _from_rows(rows, shape):
    n = 1
    for s in shape:
        n *= s
    return rows.reshape(rows.shape[:-2] + (-1,))[..., :n].reshape(rows.shape[:-2] + tuple(shape))


def _pack_weights(parts):
    layout, off = [], 0
    for p in parts:
        layout.append((off, p.shape[0]))
        off += p.shape[0]
    return _pad_rows(jnp.concatenate(parts, axis=0), 0), layout


def _gathered_rows(gathered, off, rows):
    return gathered[:, off:off + rows].reshape(N_DEV * rows, gathered.shape[2])


def _heads_slot_major(rows):
    group = rows.shape[0] // (N_KV_HEADS * HEAD_DIM)
    return rows.reshape(N_KV_HEADS, group, HEAD_DIM, rows.shape[1]).transpose(1, 0, 2, 3).reshape(rows.shape)


def _heads_kv_major(rows):
    group = rows.shape[0] // (N_KV_HEADS * HEAD_DIM)
    return rows.reshape(group, N_KV_HEADS, HEAD_DIM, rows.shape[1]).transpose(1, 0, 2, 3).reshape(rows.shape)


def _by_core(full_rows):
    return full_rows.reshape((N_DEV // 2, 2, full_rows.shape[0] // N_DEV, full_rows.shape[1]))


def kernel(x, positions, ffn1_norm, ffn1_w_gate, ffn1_w_up, ffn1_w_down, mix_norm, w_in, conv_dw_w, conv_dw_b, conv_ln_g, conv_ln_b, conv_w_proj, attn_sinks, attn_w_o, gate_b, w_out, ffn2_norm, ffn2_w_gate, ffn2_w_up, ffn2_w_down, final_norm, loss_target, m_ffn1_norm, m_ffn1_w_gate, m_ffn1_w_up, m_ffn1_w_down, m_mix_norm, m_w_in, m_conv_dw_w, m_conv_dw_b, m_conv_ln_g, m_conv_ln_b, m_conv_w_proj, m_attn_sinks, m_attn_w_o, m_gate_b, m_w_out, m_ffn2_norm, m_ffn2_w_gate, m_ffn2_w_up, m_ffn2_w_down, m_final_norm, v_ffn1_norm, v_ffn1_w_gate, v_ffn1_w_up, v_ffn1_w_down, v_mix_norm, v_w_in, v_conv_dw_w, v_conv_dw_b, v_conv_ln_g, v_conv_ln_b, v_conv_w_proj, v_attn_sinks, v_attn_w_o, v_gate_b, v_w_out, v_ffn2_norm, v_ffn2_w_gate, v_ffn2_w_up, v_ffn2_w_down, v_final_norm):
    given = dict(locals())
    shapes = {n: given[n].shape for n in WEIGHT_ORDER}
    w = {n: given[n].reshape(given[n].shape[-2:]) if given[n].ndim == 3 else given[n].reshape(1, -1) for n in WEIGHT_ORDER}
    m = {n: given["m_" + n].reshape(w[n].shape) for n in WEIGHT_ORDER}
    v = {n: given["v_" + n].reshape(w[n].shape) for n in WEIGHT_ORDER}
    my_x, my_y, my_c = _place()
    my_core = my_c.astype(jnp.int32).reshape(1)
    my_chip = (2 * my_x + my_y).astype(jnp.int32).reshape(1)
    xs, target = x[0], loss_target[0]
    T, D = xs.shape
    KV = N_KV_HEADS * HEAD_DIM
    K = w["conv_dw_w"].shape[0]

    def t16(n):
        return w[n].T.astype(BF16)

    def r16(n):
        return w[n].astype(BF16)

    blocks1 = [t16("ffn1_w_gate"), t16("ffn1_w_up"), r16("ffn1_w_down")]
    dw_bits = _pad_rows(_to_rows(lax.bitcast_convert_type(w["conv_dw_w"], BF16).reshape(-1), 0), 0)
    blocks2 = [t16("w_in"), r16("conv_w_proj"), r16("attn_w_o"), r16("w_out"), dw_bits]
    blocks3 = [t16("ffn2_w_gate"), t16("ffn2_w_up"), r16("ffn2_w_down")]
    cos, sin = rope_tables(positions[0])
    sink_col = jnp.repeat(w["attn_sinks"].reshape(-1), WINDOW).reshape(N_KV_HEADS, (D // KV) * WINDOW, 1)

    def full(gathered):
        return gathered.reshape(-1, gathered.shape[2])

    wgt1, wut1, wd1 = (full(g) for g in all_gather(blocks1, "gather_ffn1"))
    (x1, h1, a1, b1), gath2 = ffn_fwd(xs, w["ffn1_norm"], wgt1, wut1, wd1, "ffn1_fwd", carry=gather_send(blocks2))
    gath2 = run_exchange(gather_forward(gath2), "gather_mix_forward")
    wint, wcp, wo, wout = (full(g) for g in gath2[:4])
    wq = _heads_slot_major(wint[2 * D:3 * D])
    wo = _heads_slot_major(wo)
    dw_full = lax.bitcast_convert_type(_from_rows(gath2[4], w["conv_dw_w"].shape + (2,)), F32)
    dw_full = dw_full.transpose(1, 0, 2).reshape(K, D)
    (h2, ga, gb, u0, q, sgc, sgt, kk, vv), gath3 = mix_in_fwd(x1, w["mix_norm"], wint, wq, w["gate_b"], cos, sin,
                                                              "mix_in_fwd", carry=gather_send(blocks3))
    o, gath3 = attn_fwd(q, kk, vv, sink_col, "attn_fwd", carry=gather_forward(gath3))
    x2, u1, co, ao, merged = mix_out_fwd(x1, u0, o, sgc, sgt, dw_full, w["conv_dw_b"], w["conv_ln_g"], w["conv_ln_b"],
                                         wcp, wo, wout, "mix_out_fwd")
    wgt2, wut2, wd2 = (full(g) for g in gath3)
    loss, dx3, d_final, h3, a2, b2 = ffn_fwd_loss(x2, w["ffn2_norm"], wgt2, wut2, wd2, w["final_norm"], target, "ffn2_fwd_loss")

    small = {"final_norm": d_final}
    (da2, db2, s2, g02), _ = ffn_bwd_hidden(dx3, a2, b2, wd2, "ffn2_bwd_hidden")
    (dx2, small["ffn2_norm"]), _ = ffn_bwd_input(dx3, x2, w["ffn2_norm"], da2, db2, wgt2, wut2, "ffn2_bwd_input")
    core2 = [_by_core(wgrad(a, b, nm)[0]) for a, b, nm in ((da2, h3, "ffn2_dwg"), (db2, h3, "ffn2_dwu"), (s2, g02, "ffn2_dwd"))]
    (dgc, dgt, do, du1, dco, dao, u3, sums), recv2 = mix_out_bwd(dx2, u1, co, ao, sgc, sgt, w["conv_ln_g"], w["conv_ln_b"],
                                                                 wcp, wo, wout, "mix_out_bwd", carry=swap_halves(core2))
    chip2 = pair_sum(my_core, core2, recv2, "ffn2_grads_pair_sum")
    small["gate_b"] = jnp.concatenate([sums[0:1], sums[1:2]], axis=1)
    small["conv_ln_g"], small["conv_ln_b"], small["conv_dw_b"] = sums[2:3], sums[3:4], sums[4:5]
    g_wout = wgrad(merged, dx2, "dw_out")[0]
    g_wcp = wgrad(u3, dco, "dw_conv_proj")[0]
    g_wo = _heads_kv_major(wgrad(o, dao, "dw_attn_o")[0])
    (dga, dgb, g_dw), got2 = conv_bwd(du1, u0, ga, gb, dw_full, "conv_bwd", carry=exchange_between_chips(chip2))
    dq, dk, dv, dsink = attn_bwd(q, kk, vv, do, sink_col, cos, sin, "attn_bwd")
    small["attn_sinks"] = dsink[:, 0].reshape(1, -1)
    pieces = [dga, dgb, dq, dk, dv, dgc, dgt]
    dx1, small["mix_norm"] = mix_in_bwd(dx2, x1, w["mix_norm"], wint, wq, pieces, "mix_in_bwd")
    group = D // KV
    q_moves = [(2 * D + HEAD_DIM * (group * g + hh), 2 * D + HEAD_DIM * (N_KV_HEADS * hh + g), HEAD_DIM)
               for g in range(N_KV_HEADS) for hh in range(group)]
    g_wint = jnp.concatenate([wgrad_stacked(pieces[:3], h2, "dw_in_a", moves=[(0, 0, 2 * D)] + q_moves),
                              wgrad_stacked(pieces[3:], h2, "dw_in_b")], axis=0)
    corem = [_by_core(a) for a in (g_wint, g_wcp, g_wo, g_wout)]

    (da1, db1, s1, g01), recvm = ffn_bwd_hidden(dx1, a1, b1, wd1, "ffn1_bwd_hidden", carry=swap_halves(corem))
    chipm = pair_sum(my_core, corem, recvm, "mix_grads_pair_sum")
    g1c, gotm_a = wgrad(s1, g01, "ffn1_dwd", carry=exchange_between_chips(chipm[:1]))
    g1a, gotm_b = wgrad(da1, h1, "ffn1_dwg", carry=exchange_between_chips(chipm[1:]))
    g1b = wgrad(db1, h1, "ffn1_dwu")[0]
    core1 = [_by_core(a) for a in (g1a, g1b, g1c)]
    chip1 = pair_sum(my_core, core1, run_exchange(swap_halves(core1), "ffn1_grads_swap"), "ffn1_grads_pair_sum")
    (grad_x, small["ffn1_norm"]), got1 = ffn_bwd_input(dx1, xs, w["ffn1_norm"], da1, db1, wgt1, wut1, "ffn1_bwd_input",
                                                       carry=exchange_between_chips(chip1))

    gs2 = chip_sum(my_chip, chip2, got2, "ffn2_grads_sum")
    gsm = chip_sum(my_chip, chipm, gotm_a + gotm_b, "mix_grads_sum")
    gs1 = chip_sum(my_chip, chip1, got1, "ffn1_grads_sum")
    grads_t = {"ffn1_w_gate": gs1[0], "ffn1_w_up": gs1[1], "ffn1_w_down": gs1[2],
               "ffn2_w_gate": gs2[0], "ffn2_w_up": gs2[1], "ffn2_w_down": gs2[2],
               "w_in": gsm[0], "conv_w_proj": gsm[1], "attn_w_o": gsm[2], "w_out": gsm[3]}
    grads = {}

    def pack_small(d, taps, extra):
        rows = [_to_rows(d[n].reshape(-1), 0) for n in REPLICATED] + [taps, _to_rows(extra.reshape(-1), 0)]
        return _pad_rows(jnp.concatenate(rows, axis=0), 0)

    zero, no_taps = jnp.zeros((1, LANES), F32), jnp.zeros((K, D), F32)
    (shares,) = all_gather([pack_small(small, g_dw, loss)], "gather_small_grads")
    g_s, d_s, m_s, v_s = adamw_replicated(pack_small(w, no_taps, zero), shares, pack_small(m, no_taps, zero),
                                          pack_small(v, no_taps, zero), "adamw_replicated")
    delta, new_m, new_v = {}, {}, {}
    off = 0
    for n in REPLICATED:
        r = -(-w[n].shape[1] // PACK_COLS)
        grads[n], delta[n], new_m[n], new_v[n] = (_from_rows(a[off:off + r], w[n].shape) for a in (g_s, d_s, m_s, v_s))
        off += r
    shard_cols = w["conv_dw_w"].shape[1]
    grads_t["conv_dw_w"] = lax.dynamic_slice_in_dim(g_s[off:off + K], _dev_index(my_x, my_y, my_c) * shard_cols, shard_cols,
                                                    axis=1)
    total_loss = g_s[off + K, 0]

    def like(a, ref):
        return a if a.shape == ref.shape else a.T

    groups = (("ffn1_w_gate", "ffn1_w_up", "ffn1_w_down"), ("ffn2_w_gate", "ffn2_w_up", "ffn2_w_down"),
              ("w_in", "conv_dw_w", "conv_w_proj", "attn_w_o", "w_out"))
    for k, names in enumerate(groups):
        res = adamw([like(w[n], grads_t[n]) for n in names], [grads_t[n] for n in names],
                    [like(m[n], grads_t[n]) for n in names], [like(v[n], grads_t[n]) for n in names], "adamw_%d" % k)
        for n, (d, mm, vv) in zip(names, res):
            grads[n], delta[n], new_m[n], new_v[n] = (like(a, w[n]) for a in (grads_t[n], d, mm, vv))

    out = [total_loss, grad_x[None]]
    for d in (grads, delta, new_m, new_v):
        out += [d[n].reshape(shapes[n]) for n in WEIGHT_ORDER]
    return tuple(out)
```

```python
import functools
from typing import Callable, NamedTuple

import jax
import jax.numpy as jnp
from jax import lax
from jax.experimental import pallas as pl
from jax.experimental.pallas import tpu as pltpu

F32, BF16 = jnp.float32, jnp.bfloat16

HEAD_DIM = 64
N_KV_HEADS = 4
WINDOW = 128
CONV_WIDTH = 31
ROPE_THETA = 10000.0
EPS = 1e-6
LN_EPS = 1e-5
NEG_INF = -1e30
ADAM_LR, ADAM_B1, ADAM_B2, ADAM_EPS, ADAM_WD, ADAM_STEP = 0.001, 0.9, 0.999, 1e-08, 0.01, 10

N_DEV = 8
LANES = 128
SUBLANES = 8
CONV_HALO = 32
CONV_ROWS, CONV_LANES = 64, 256
ROW_TILE = 512
FFN_CHUNK = 256
FFN_SCALE = 0.5
WGRAD_TILE_ELEMS = 2 ** 22
WGRAD_TILE_ROWS = 2048
VMEM_LIMIT = 56 * 2 ** 20
MESH = pl.DeviceIdType.MESH


def _params(*sem):
    return pltpu.CompilerParams(dimension_semantics=sem or None, vmem_limit_bytes=VMEM_LIMIT)


def _resident(shape):
    zeros = (0,) * len(shape)
    return pl.BlockSpec(shape, lambda *_: zeros, pipeline_mode=pl.Buffered(1))


def _rows(tm, n):
    return pl.BlockSpec((tm, n), lambda i: (i, 0))


def _acc_spec(shape):
    zeros = (0,) * len(shape)
    return pl.BlockSpec(shape, lambda *_: zeros)


_ANY = pl.BlockSpec(memory_space=pl.ANY)


class Carry(NamedTuple):
    ins: tuple
    out_shapes: tuple
    aliases: dict
    sems: tuple
    start: Callable
    finish: Callable


def _call(body, *, name, grid, in_specs, out_specs, out_shape, args, scratch_shapes=(), carry=None):
    n_in, n_out, n_scr = len(in_specs), len(out_specs), len(scratch_shapes)
    params = _params(*(("arbitrary",) * len(grid)))
    if carry is None:
        res = pl.pallas_call(body, name=name, grid=grid, in_specs=list(in_specs), out_specs=tuple(out_specs),
                             out_shape=tuple(out_shape), scratch_shapes=list(scratch_shapes), compiler_params=params)(*args)
        return tuple(res), ()
    c_in, c_out = len(carry.ins), len(carry.out_shapes)

    def wrapped(*refs):
        ins, c_ins = refs[:n_in], refs[n_in:n_in + c_in]
        p = n_in + c_in
        outs, c_outs = refs[p:p + n_out], refs[p + n_out:p + n_out + c_out]
        p += n_out + c_out
        scr, c_sems = refs[p:p + n_scr], refs[p + n_scr:]
        ids = [pl.program_id(d) for d in range(len(grid))]
        first = functools.reduce(jnp.logical_and, [i == 0 for i in ids])
        last = functools.reduce(jnp.logical_and, [i == n - 1 for i, n in zip(ids, grid)])

        @pl.when(first)
        def _():
            carry.start(c_ins, c_outs, c_sems)

        body(*ins, *outs, *scr)

        @pl.when(last)
        def _():
            carry.finish(c_ins, c_outs, c_sems)

    res = pl.pallas_call(
        wrapped, name=name, grid=grid, in_specs=list(in_specs) + [_ANY] * c_in, out_specs=tuple(out_specs) + (_ANY,) * c_out,
        out_shape=tuple(out_shape) + tuple(carry.out_shapes), scratch_shapes=list(scratch_shapes) + list(carry.sems),
        input_output_aliases={n_in + i: n_out + o for i, o in carry.aliases.items()}, compiler_params=params,
    )(*args, *carry.ins)
    return tuple(res[:n_out]), tuple(res[n_out:])


def _nt(a, b):
    return lax.dot_general(a, b, (((1,), (1,)), ((), ())), preferred_element_type=F32)


def _tn(a, b):
    return lax.dot_general(a, b, (((0,), (0,)), ((), ())), preferred_element_type=F32)


def _dot(a, b):
    return jnp.dot(a, b, preferred_element_type=F32)


def _sigmoid(x):
    return 1.0 / (1.0 + jnp.exp(-x))


def _rms_fwd(x, g):
    r = lax.rsqrt(jnp.mean(x * x, axis=-1, keepdims=True) + EPS)
    return (x * r) * g


def _rms_bwd(x, g, dy):
    r = lax.rsqrt(jnp.mean(x * x, axis=-1, keepdims=True) + EPS)
    xhat = x * r
    dyg = dy * g
    dx = r * (dyg - xhat * jnp.mean(dyg * xhat, axis=-1, keepdims=True))
    return dx, dy * xhat


def _rot_half(x):
    lane = lax.broadcasted_iota(jnp.int32, (x.shape[0], LANES), 1)
    first = (lane % HEAD_DIM) < (HEAD_DIM // 2)
    out = []
    for s in range(x.shape[1] // LANES):
        xs = x[:, LANES * s:LANES * (s + 1)]
        out.append(jnp.where(first, pltpu.roll(xs, LANES - HEAD_DIM // 2, 1), pltpu.roll(xs, HEAD_DIM // 2, 1)))
    return out[0] if len(out) == 1 else jnp.concatenate(out, axis=1)


def _tile_lanes(t, width):
    return t if width == LANES else jnp.concatenate([t] * (width // LANES), axis=1)


def _rope_fwd(x, cos, sin_signed):
    w = x.shape[1]
    return x * _tile_lanes(cos, w) + _rot_half(x) * _tile_lanes(sin_signed, w)


def _rope_bwd(dy, cos, sin_signed):
    w = dy.shape[1]
    return dy * _tile_lanes(cos, w) + _rot_half(dy * _tile_lanes(sin_signed, w))


def _ffn_rows(x, g_ref, wg_ref, wu_ref, wd_ref, h_ref, a_ref, b_ref, s_ref, acc_ref):
    F = wg_ref.shape[0]
    h = _rms_fwd(x, g_ref[...]).astype(BF16)
    h_ref[...] = h
    for c in range(F // FFN_CHUNK):
        cs = pl.ds(c * FFN_CHUNK, FFN_CHUNK)
        a = _nt(h, wg_ref[cs, :])
        b = _nt(h, wu_ref[cs, :])
        a_ref[:, cs] = a.astype(BF16)
        b_ref[:, cs] = b.astype(BF16)
        s = (a * _sigmoid(a) * b).astype(BF16)
        s_ref[:, cs] = s
        y = _dot(s, wd_ref[cs, :])
        if c == 0:
            acc_ref[...] = y
        else:
            acc_ref[...] += y
    return x + FFN_SCALE * acc_ref[...]


def ffn_fwd(x, gain, wgt, wut, wd, name, carry=None):
    T, D = x.shape
    F = wgt.shape[0]
    tm = min(ROW_TILE, T)

    def body(x_ref, g_ref, wg_ref, wu_ref, wd_ref, xo_ref, h_ref, a_ref, b_ref, s_ref, acc_ref):
        xo_ref[...] = _ffn_rows(x_ref[...], g_ref, wg_ref, wu_ref, wd_ref, h_ref, a_ref, b_ref, s_ref, acc_ref)

    wide = jax.ShapeDtypeStruct((T, F), BF16)
    return _call(
        body, name=name, grid=(T // tm,),
        out_shape=(jax.ShapeDtypeStruct((T, D), F32), jax.ShapeDtypeStruct((T, D), BF16), wide, wide, wide),
        in_specs=[_rows(tm, D), _resident((1, D)), _resident((F, D)), _resident((F, D)), _resident((F, D))],
        out_specs=(_rows(tm, D), _rows(tm, D), _rows(tm, F), _rows(tm, F), _rows(tm, F)),
        scratch_shapes=[pltpu.VMEM((tm, D), F32)], args=(x, gain, wgt, wut, wd), carry=carry)


def ffn_fwd_loss(x, gain, wgt, wut, wd, final_gain, target, name):
    T, D = x.shape
    F = wgt.shape[0]
    tm = min(ROW_TILE, T)

    def body(x_ref, g_ref, wg_ref, wu_ref, wd_ref, gf_ref, t_ref,
             loss_ref, dx_ref, dg_ref, h_ref, a_ref, b_ref, s_ref, acc_ref):
        xo = _ffn_rows(x_ref[...], g_ref, wg_ref, wu_ref, wd_ref, h_ref, a_ref, b_ref, s_ref, acc_ref)
        gf = gf_ref[...]
        err = _rms_fwd(xo, gf) - t_ref[...]
        dx, dgt = _rms_bwd(xo, gf, err * (1.0 / D))
        dx_ref[...] = dx

        @pl.when(pl.program_id(0) == 0)
        def _():
            dg_ref[...] = jnp.zeros_like(dg_ref)
            loss_ref[...] = jnp.zeros_like(loss_ref)
        dg_ref[...] += jnp.sum(dgt, axis=0, keepdims=True)
        per_token = jnp.sum(err * err, axis=-1, keepdims=True) * (0.5 / D)
        loss_ref[...] += jnp.broadcast_to(jnp.sum(per_token, axis=0, keepdims=True), (1, LANES))

    return pl.pallas_call(
        body, name=name, grid=(T // tm,),
        out_shape=(jax.ShapeDtypeStruct((1, LANES), F32), jax.ShapeDtypeStruct((T, D), F32), jax.ShapeDtypeStruct((1, D), F32),
                   jax.ShapeDtypeStruct((T, D), BF16), jax.ShapeDtypeStruct((T, F), BF16), jax.ShapeDtypeStruct((T, F), BF16),
                   jax.ShapeDtypeStruct((T, F), BF16)),
        in_specs=[_rows(tm, D), _resident((1, D)), _resident((F, D)), _resident((F, D)), _resident((F, D)), _resident((1, D)),
                  _rows(tm, D)],
        out_specs=(_acc_spec((1, LANES)), _rows(tm, D), _acc_spec((1, D)), _rows(tm, D), _rows(tm, F), _rows(tm, F),
                   _rows(tm, F)),
        scratch_shapes=[pltpu.VMEM((tm, D), F32)], compiler_params=_params("arbitrary"),
    )(x, gain, wgt, wut, wd, final_gain, target)


def ffn_bwd_hidden(dxo, a, b, wd, name, carry=None):
    T, D = dxo.shape
    F = wd.shape[0]
    tm = min(ROW_TILE, T)
    fc = FFN_CHUNK

    def hidden_body(dxo_ref, a_ref, b_ref, wd_ref, da_ref, db_ref):
        g0 = (FFN_SCALE * dxo_ref[...]).astype(BF16)
        for c in range(F // fc):
            cs = pl.ds(c * fc, fc)
            ds = _nt(g0, wd_ref[cs, :])
            a = a_ref[:, cs].astype(F32)
            bb = b_ref[:, cs].astype(F32)
            sa = _sigmoid(a)
            da_ref[:, cs] = (ds * bb * (sa * (1.0 + a * (1.0 - sa)))).astype(BF16)
            db_ref[:, cs] = (ds * (a * sa)).astype(BF16)

    wide = jax.ShapeDtypeStruct((T, F), BF16)
    return _call(
        hidden_body, name=name, grid=(T // tm,), out_shape=(wide, wide),
        in_specs=[_rows(tm, D), _rows(tm, F), _rows(tm, F), _resident((F, D))],
        out_specs=(_rows(tm, F), _rows(tm, F)), args=(dxo, a, b, wd), carry=carry)


def ffn_bwd_input(dxo, x, gain, da, db, wg, wu, name, carry=None):
    T, D = x.shape
    F = wg.shape[0]
    tm = min(ROW_TILE, T)

    def input_body(dxo_ref, x_ref, g_ref, da_ref, db_ref, wg_ref, wu_ref, dx_ref, dg_ref):
        dh = _dot(da_ref[...], wg_ref[...]) + _dot(db_ref[...], wu_ref[...])
        dx, dgt = _rms_bwd(x_ref[...], g_ref[...], dh)
        dx_ref[...] = dxo_ref[...] + dx

        @pl.when(pl.program_id(0) == 0)
        def _():
            dg_ref[...] = jnp.zeros_like(dg_ref)
        dg_ref[...] += jnp.sum(dgt, axis=0, keepdims=True)

    return _call(
        input_body, name=name, grid=(T // tm,),
        out_shape=(jax.ShapeDtypeStruct((T, D), F32), jax.ShapeDtypeStruct((1, D), F32)),
        in_specs=[_rows(tm, D), _rows(tm, D), _resident((1, D)), _rows(tm, F), _rows(tm, F), _resident((F, D)),
                  _resident((F, D))],
        out_specs=(_rows(tm, D), _acc_spec((1, D))), args=(dxo, x, gain, da, db, wg, wu), carry=carry)


def wgrad(a, b, name, carry=None, b_scale=None):
    T, M = a.shape
    N = b.shape[1]
    tk = ROW_TILE
    while 2 * tk * M <= WGRAD_TILE_ELEMS and 2 * tk <= WGRAD_TILE_ROWS:
        tk *= 2
    tk = min(tk, T)
    nk = T // tk

    def body(a_ref, b_ref, o_ref, acc_ref):
        k = pl.program_id(0)
        bt = b_ref[...] if b_scale is None else b_scale * b_ref[...]
        part = _tn(a_ref[...].astype(BF16), bt.astype(BF16))

        @pl.when(k == 0)
        def _():
            acc_ref[...] = part

        @pl.when(k > 0)
        def _():
            acc_ref[...] += part

        @pl.when(k == nk - 1)
        def _():
            o_ref[...] = acc_ref[...].astype(BF16)

    (out,), carried = _call(
        body, name=name, grid=(nk,), out_shape=(jax.ShapeDtypeStruct((M, N), BF16),),
        in_specs=[_rows(tk, M), _rows(tk, N)], out_specs=(_acc_spec((M, N)),),
        scratch_shapes=[pltpu.VMEM((M, N), F32)], args=(a, b), carry=carry)
    return out, carried


def wgrad_stacked(pieces, b, name, moves=None):
    T, N = b.shape
    n = len(pieces)
    widths = [p.shape[1] for p in pieces]
    offs = [sum(widths[:i]) for i in range(n)]
    M = sum(widths)
    tk = ROW_TILE
    while 2 * tk * M <= WGRAD_TILE_ELEMS and 2 * tk <= WGRAD_TILE_ROWS:
        tk *= 2
    tk = min(tk, T)
    nk = T // tk
    moves = moves or [(0, 0, M)]

    def body(*refs):
        a_refs, b_ref, o_ref, acc_ref = refs[:n], refs[n], refs[n + 1], refs[n + 2]
        k = pl.program_id(0)

        @pl.when(k == 0)
        def _():
            acc_ref[...] = jnp.zeros_like(acc_ref)
        bt = b_ref[...].astype(BF16)
        for a_ref, off, width in zip(a_refs, offs, widths):
            acc_ref[pl.ds(off, width), :] += _tn(a_ref[...].astype(BF16), bt)

        @pl.when(k == nk - 1)
        def _():
            for to, start, rows in moves:
                o_ref[pl.ds(to, rows), :] = acc_ref[pl.ds(start, rows), :].astype(BF16)

    return pl.pallas_call(
        body, name=name, grid=(nk,), out_shape=jax.ShapeDtypeStruct((M, N), BF16),
        in_specs=[_rows(tk, width) for width in widths] + [_rows(tk, N)], out_specs=_acc_spec((M, N)),
        scratch_shapes=[pltpu.VMEM((M, N), F32)], compiler_params=_params("arbitrary"),
    )(*pieces, b)


def _w_in_rows(D):
    KV = N_KV_HEADS * HEAD_DIM
    return 0, D, 2 * D, 3 * D, 3 * D + KV, 3 * D + 2 * KV, 4 * D + 2 * KV


def mix_in_fwd(x, gain, wint, wq, gate_b, cos, sin_signed, name, carry=None):
    T, D = x.shape
    KV = N_KV_HEADS * HEAD_DIM
    tm = min(ROW_TILE, T)
    o_ga, o_gb, _, o_k, o_v, o_gc, o_gt = _w_in_rows(D)

    def body(x_ref, g_ref, w_ref, wq_ref, gb_ref, cos_ref, sin_ref,
             h_ref, ga_ref, gb_out_ref, u0_ref, q_ref, sgc_ref, sgt_ref, k_ref, v_ref):
        h = _rms_fwd(x_ref[...], g_ref[...]).astype(BF16)
        h_ref[...] = h
        cos, sin = cos_ref[...], sin_ref[...]
        ga = _nt(h, w_ref[pl.ds(o_ga, D), :])
        gb = _nt(h, w_ref[pl.ds(o_gb, D), :])
        ga_ref[...] = ga.astype(BF16)
        gb_out_ref[...] = gb.astype(BF16)
        u0_ref[...] = (ga * _sigmoid(gb)).astype(BF16)
        q = _nt(h, wq_ref[...])
        q_ref[...] = _rope_fwd(q, cos, sin).astype(BF16)
        gc = _nt(h, w_ref[pl.ds(o_gc, D), :]) + gb_ref[:, pl.ds(0, D)]
        sgc_ref[...] = _sigmoid(gc).astype(BF16)
        gt = _nt(h, w_ref[pl.ds(o_gt, D), :]) + gb_ref[:, pl.ds(D, D)]
        sgt_ref[...] = _sigmoid(gt).astype(BF16)
        k = _nt(h, w_ref[pl.ds(o_k, KV), :])
        k_ref[...] = _rope_fwd(k, cos, sin).astype(BF16)
        v_ref[...] = _nt(h, w_ref[pl.ds(o_v, KV), :]).astype(BF16)

    big = jax.ShapeDtypeStruct((T, D), BF16)
    small = jax.ShapeDtypeStruct((T, KV), BF16)
    return _call(
        body, name=name, grid=(T // tm,),
        out_shape=(big, big, big, big, big, big, big, small, small),
        in_specs=[_rows(tm, D), _resident((1, D)), _resident(wint.shape), _resident(wq.shape), _resident((1, 2 * D)),
                  _rows(tm, LANES), _rows(tm, LANES)],
        out_specs=(_rows(tm, D),) * 7 + (_rows(tm, KV),) * 2,
        args=(x, gain, wint, wq, gate_b, cos, sin_signed), carry=carry)


def _attn_masks(tile_is_first, rows):
    qi = lax.broadcasted_iota(jnp.int32, (rows, 2 * WINDOW), 0) % WINDOW
    c = lax.broadcasted_iota(jnp.int32, (rows, 2 * WINDOW), 1)
    base = (c > qi) & (c <= qi + WINDOW)
    first_key = jnp.where(tile_is_first, WINDOW, 0)
    return base, base & (c >= first_key)


def _kv_lane_head(rows, width):
    return lax.broadcasted_iota(jnp.int32, (rows, width), 1) // HEAD_DIM


def _block_diag(win):
    head = _kv_lane_head(*win.shape)
    zero = jnp.zeros_like(win)
    return jnp.concatenate([jnp.where(head == g, win, zero) for g in range(N_KV_HEADS)], axis=0)


def _diag_blocks_sum(bd, keys):
    head = _kv_lane_head(keys, bd.shape[1])
    out = jnp.zeros((keys, bd.shape[1]), F32)
    for g in range(N_KV_HEADS):
        out = jnp.where(head == g, bd[g * keys:(g + 1) * keys], out)
    return out


def _kv_windows(k_ref, kh_ref, v_ref, vh_ref, j):
    rows = pl.ds(j * WINDOW, WINDOW)
    if j == 0:
        kprev, vprev = kh_ref[...], vh_ref[...]
    else:
        prev = pl.ds((j - 1) * WINDOW, WINDOW)
        kprev, vprev = k_ref[prev, :], v_ref[prev, :]
    return jnp.concatenate([kprev, k_ref[rows, :]], axis=0), jnp.concatenate([vprev, v_ref[rows, :]], axis=0)


def _stack_slots(ref, j, group, KV):
    rows = pl.ds(j * WINDOW, WINDOW)
    return jnp.concatenate([ref[rows, pl.ds(KV * hh, KV)] for hh in range(group)], axis=0)


def _attn_exp(qs, kbd, sink_ref, mask):
    s = _nt(qs, kbd) * (HEAD_DIM ** -0.5)
    out = []
    for g in range(N_KV_HEADS):
        sg = jnp.where(mask, s[:, 2 * WINDOW * g:2 * WINDOW * (g + 1)], NEG_INF)
        sink = sink_ref[g]
        m = jnp.maximum(jnp.max(sg, axis=-1, keepdims=True), sink)
        out.append((jnp.exp(sg - m), jnp.exp(sink - m)))
    return out


def _spread_over_heads(cols, rows, KV):
    head = _kv_lane_head(rows, KV)
    out = jnp.zeros((rows, KV), F32)
    for g, col in enumerate(cols):
        out = jnp.where(head == g, col, out)
    return out


def _halo_rows_spec(tq, width, sub):
    return pl.BlockSpec((sub, width), lambda i: (jnp.maximum(i * (tq // sub) - 1, 0), 0))


def attn_fwd(q, k, v, sink_col, name, carry=None):
    T, D = q.shape
    KV = k.shape[1]
    group = D // KV
    tq = min(ROW_TILE, T)
    nsub = tq // WINDOW
    rows, wide = group * WINDOW, N_KV_HEADS * 2 * WINDOW

    def body(q_ref, k_ref, kh_ref, v_ref, vh_ref, sink_ref, o_ref):
        base, first = _attn_masks(pl.program_id(0) == 0, rows)
        head = _kv_lane_head(wide, KV)
        block = lax.broadcasted_iota(jnp.int32, head.shape, 0) // (2 * WINDOW)
        ones_bd = jnp.where(head == block, 1.0, 0.0).astype(BF16)
        for j in range(nsub):
            k_win, v_win = _kv_windows(k_ref, kh_ref, v_ref, vh_ref, j)
            parts = _attn_exp(_stack_slots(q_ref, j, group, KV), _block_diag(k_win), sink_ref, first if j == 0 else base)
            p = jnp.concatenate([pg.astype(BF16) for pg, _ in parts], axis=1)
            both = _dot(p, jnp.concatenate([_block_diag(v_win), ones_bd], axis=1))
            denom = both[:, KV:] + _spread_over_heads([es for _, es in parts], rows, KV)
            out = (both[:, :KV] / denom).astype(BF16)
            for hh in range(group):
                o_ref[pl.ds(j * WINDOW, WINDOW), pl.ds(KV * hh, KV)] = out[hh * WINDOW:(hh + 1) * WINDOW]

    (o,), carried = _call(
        body, name=name, grid=(T // tq,),
        out_shape=(jax.ShapeDtypeStruct((T, D), BF16),),
        in_specs=[_rows(tq, D), _rows(tq, KV), _halo_rows_spec(tq, KV, WINDOW), _rows(tq, KV),
                  _halo_rows_spec(tq, KV, WINDOW), _resident(sink_col.shape)],
        out_specs=(_rows(tq, D),), args=(q, k, k, v, v, sink_col), carry=carry)
    return o, carried


def attn_bwd(q, k, v, do, sink_col, cos, sin_signed, name):
    T, D = q.shape
    KV = k.shape[1]
    group = D // KV
    tq = min(ROW_TILE, T)
    nsub = tq // WINDOW
    nt = T // tq
    scale = HEAD_DIM ** -0.5
    rows, wide = group * WINDOW, N_KV_HEADS * 2 * WINDOW

    def rev(i):
        return nt - 1 - i

    def body(q_ref, k_ref, kh_ref, v_ref, vh_ref, do_ref, sink_ref, cos_ref, sin_ref,
             dq_ref, dk_ref, dv_ref, dsink_ref, dq_acc, dk_acc, dv_acc, carry_k, carry_v):
        i = pl.program_id(0)

        @pl.when(i == 0)
        def _():
            carry_k[...] = jnp.zeros_like(carry_k)
            carry_v[...] = jnp.zeros_like(carry_v)
            dsink_ref[...] = jnp.zeros_like(dsink_ref)

        dk_acc[...] = jnp.zeros_like(dk_acc)
        dv_acc[...] = jnp.zeros_like(dv_acc)
        base, first = _attn_masks(rev(i) == 0, rows)
        lane = lax.broadcasted_iota(jnp.int32, (1, LANES), 1)
        for j in range(nsub):
            k_win, v_win = _kv_windows(k_ref, kh_ref, v_ref, vh_ref, j)
            kbd, vbd = _block_diag(k_win), _block_diag(v_win)
            qs, dos = _stack_slots(q_ref, j, group, KV), _stack_slots(do_ref, j, group, KV)
            dp = _nt(dos, vbd)
            probs16, ds16 = [], []
            for g, (pg, es) in enumerate(_attn_exp(qs, kbd, sink_ref, first if j == 0 else base)):
                inv = 1.0 / (jnp.sum(pg, axis=-1, keepdims=True) + es)
                probs = pg * inv
                dpg = dp[:, 2 * WINDOW * g:2 * WINDOW * (g + 1)]
                delta = jnp.sum(probs * dpg, axis=-1, keepdims=True)
                probs16.append(probs.astype(BF16))
                ds16.append((probs * (dpg - delta) * scale).astype(BF16))
                dsk = -(es * inv * delta)
                for hh in range(group):
                    tot = jnp.sum(dsk[hh * WINDOW:(hh + 1) * WINDOW], axis=0, keepdims=True)
                    dsink_ref[pl.ds(hh, 1), :] += jnp.where(lane == g, tot, 0.0)
            ds = jnp.concatenate(ds16, axis=1)
            dqs = _dot(ds, kbd)
            for hh in range(group):
                dq_acc[pl.ds(j * WINDOW, WINDOW), pl.ds(KV * hh, KV)] = dqs[hh * WINDOW:(hh + 1) * WINDOW]
            keys = pl.ds(j * WINDOW, 2 * WINDOW)
            dk_acc[keys, :] += _diag_blocks_sum(_tn(ds, qs), 2 * WINDOW)
            dv_acc[keys, :] += _diag_blocks_sum(_tn(jnp.concatenate(probs16, axis=1), dos), 2 * WINDOW)

        tail = pl.ds(tq, WINDOW)
        dk_acc[tail, :] += carry_k[...]
        dv_acc[tail, :] += carry_v[...]
        carry_k[...] = dk_acc[pl.ds(0, WINDOW), :]
        carry_v[...] = dv_acc[pl.ds(0, WINDOW), :]
        cos, sin = cos_ref[...], sin_ref[...]
        dq_ref[...] = _rope_bwd(dq_acc[...], cos, sin).astype(BF16)
        dk_ref[...] = _rope_bwd(dk_acc[pl.ds(WINDOW, tq), :], cos, sin).astype(BF16)
        dv_ref[...] = dv_acc[pl.ds(WINDOW, tq), :].astype(BF16)

    def rrows(n):
        return pl.BlockSpec((tq, n), lambda i: (rev(i), 0))

    def rhalo(n):
        return pl.BlockSpec((WINDOW, n), lambda i: (jnp.maximum(rev(i) * nsub - 1, 0), 0))

    return pl.pallas_call(
        body, name=name, grid=(nt,),
        out_shape=(jax.ShapeDtypeStruct((T, D), BF16), jax.ShapeDtypeStruct((T, KV), BF16),
                   jax.ShapeDtypeStruct((T, KV), BF16), jax.ShapeDtypeStruct((SUBLANES, LANES), F32)),
        in_specs=[rrows(D), rrows(KV), rhalo(KV), rrows(KV), rhalo(KV), rrows(D), _resident(sink_col.shape),
                  rrows(LANES), rrows(LANES)],
        out_specs=(rrows(D), rrows(KV), rrows(KV), _acc_spec((SUBLANES, LANES))),
        scratch_shapes=[pltpu.VMEM((tq, D), F32), pltpu.VMEM((WINDOW + tq, KV), F32), pltpu.VMEM((WINDOW + tq, KV), F32),
                        pltpu.VMEM((WINDOW, KV), F32), pltpu.VMEM((WINDOW, KV), F32)],
        compiler_params=_params("arbitrary"),
    )(q, k, k, v, v, do, sink_col, cos, sin_signed)


def _ln_stats(u):
    mu = jnp.mean(u, axis=-1, keepdims=True)
    d = u - mu
    rstd = lax.rsqrt(jnp.mean(d * d, axis=-1, keepdims=True) + LN_EPS)
    return d * rstd, rstd


def _lag_taps(b, K):
    return [(a, K - 1 - (SUBLANES * a + b)) for a in range(-(-K // SUBLANES)) if SUBLANES * a + b <= K - 1]


def _conv_chunks(tm, D, chunk):
    def rows(c, carry):
        r0 = pl.multiple_of(c * CONV_ROWS, CONV_ROWS)
        for l0 in range(0, D, CONV_LANES):
            chunk(r0, pl.ds(l0, CONV_LANES))
        return carry
    lax.fori_loop(0, tm // CONV_ROWS, rows, 0)


def _conv_causal(buf, w_ref, bias_ref, out_ref, tm, D, K):
    def chunk(r0, lanes):
        acc = jnp.broadcast_to(bias_ref[:, lanes], (CONV_ROWS, CONV_LANES))
        for b in range(SUBLANES):
            y = None
            for a, k in _lag_taps(b, K):
                start = pl.multiple_of(r0 + CONV_HALO - SUBLANES * (a + 1), SUBLANES)
                t = buf[pl.ds(start, CONV_ROWS + SUBLANES), lanes] * w_ref[pl.ds(k, 1), lanes]
                y = t if y is None else y + t
            acc = acc + y[SUBLANES - b:SUBLANES - b + CONV_ROWS]
        out_ref[pl.ds(r0, CONV_ROWS), lanes] = acc
    _conv_chunks(tm, D, chunk)


def _conv_anticausal(dbuf, w_ref, out_ref, tm, D, K):
    def chunk(r0, lanes):
        acc = jnp.zeros((CONV_ROWS, CONV_LANES), F32)
        for b in range(SUBLANES):
            y = None
            for a, k in _lag_taps(b, K):
                start = pl.multiple_of(r0 + SUBLANES * a, SUBLANES)
                t = dbuf[pl.ds(start, CONV_ROWS + SUBLANES), lanes] * w_ref[pl.ds(k, 1), lanes]
                y = t if y is None else y + t
            acc = acc + y[b:b + CONV_ROWS]
        out_ref[pl.ds(r0, CONV_ROWS), lanes] = acc
    _conv_chunks(tm, D, chunk)


def _conv_tap_grads(dbuf, ubuf, acc_ref, tm, D, K):
    reach = SUBLANES * (-(-K // SUBLANES) - 1)

    def chunk(r0, lanes):
        d = dbuf[pl.ds(r0, CONV_ROWS), lanes]
        around = ubuf[pl.ds(pl.multiple_of(r0 + CONV_HALO - reach - SUBLANES, SUBLANES), CONV_ROWS + reach + SUBLANES), lanes]
        for b in range(SUBLANES):
            shifted = around[SUBLANES - b:SUBLANES - b + CONV_ROWS + reach]
            for a, k in _lag_taps(b, K):
                prod = d * shifted[reach - SUBLANES * a:reach - SUBLANES * a + CONV_ROWS]
                part = prod[0:SUBLANES]
                for i in range(1, CONV_ROWS // SUBLANES):
                    part = part + prod[SUBLANES * i:SUBLANES * (i + 1)]
                acc_ref[k, :, lanes] += part
    _conv_chunks(tm, D, chunk)


def mix_out_fwd(x, u0, o, sgc, sgt, dw_w, dw_b, ln_g, ln_b, wcp, wo, wout, name):
    T, D = x.shape
    tm = min(ROW_TILE, T)
    K = dw_w.shape[0]

    def body(x_ref, u_ref, uh_ref, o_ref, sgc_ref, sgt_ref, w_ref, b_ref, lg_ref, lb_ref, wcp_ref, wo_ref, wout_ref,
             x2_ref, u1_ref, co_ref, ao_ref, mg_ref, buf, conv):
        keep = (pl.program_id(0) > 0).astype(F32)
        buf[pl.ds(0, CONV_HALO), :] = uh_ref[...].astype(F32) * keep
        buf[pl.ds(CONV_HALO, tm), :] = u_ref[...].astype(F32)
        _conv_causal(buf, w_ref, b_ref, conv, tm, D, K)
        acc = conv[...]
        u1_ref[...] = acc.astype(BF16)
        xhat, _ = _ln_stats(acc)
        u2 = xhat * lg_ref[...] + lb_ref[...]
        u3 = (u2 * _sigmoid(u2)).astype(BF16)
        co = _dot(u3, wcp_ref[...])
        ao = _dot(o_ref[...], wo_ref[...])
        co_ref[...] = co.astype(BF16)
        ao_ref[...] = ao.astype(BF16)
        merged = (sgc_ref[...].astype(F32) * co + sgt_ref[...].astype(F32) * ao).astype(BF16)
        mg_ref[...] = merged
        x2_ref[...] = x_ref[...] + _dot(merged, wout_ref[...])

    big = jax.ShapeDtypeStruct((T, D), BF16)
    vec = _resident((1, D))
    return pl.pallas_call(
        body, name=name, grid=(T // tm,),
        out_shape=(jax.ShapeDtypeStruct((T, D), F32), big, big, big, big),
        in_specs=[_rows(tm, D), _rows(tm, D), _halo_rows_spec(tm, D, CONV_HALO), _rows(tm, D), _rows(tm, D), _rows(tm, D),
                  _resident((K, D)), vec, vec, vec, _resident((D, D)), _resident((D, D)), _resident((D, D))],
        out_specs=(_rows(tm, D),) * 5,
        scratch_shapes=[pltpu.VMEM((CONV_HALO + tm, D), F32), pltpu.VMEM((tm, D), F32)],
        compiler_params=_params("arbitrary"),
    )(x, u0, u0, o, sgc, sgt, dw_w, dw_b, ln_g, ln_b, wcp, wo, wout)


def mix_out_bwd(dx2, u1, co, ao, sgc, sgt, ln_g, ln_b, wcp, wo, wout, name, carry=None):
    T, D = dx2.shape
    tm = min(ROW_TILE, T)

    def body(dx_ref, u1_ref, co_ref, ao_ref, sgc_ref, sgt_ref, lg_ref, lb_ref, wcp_ref, wo_ref, wout_ref,
             dgc_ref, dgt_ref, do_ref, du1_ref, dco_ref, dao_ref, u3_ref, sums_ref):
        dm = _nt(dx_ref[...].astype(BF16), wout_ref[...])
        sgc, sgt = sgc_ref[...].astype(F32), sgt_ref[...].astype(F32)
        dco = (dm * sgc).astype(BF16)
        dao = (dm * sgt).astype(BF16)
        dgc = dm * co_ref[...].astype(F32) * sgc * (1.0 - sgc)
        dgt = dm * ao_ref[...].astype(F32) * sgt * (1.0 - sgt)
        dco_ref[...] = dco
        dao_ref[...] = dao
        dgc_ref[...] = dgc.astype(BF16)
        dgt_ref[...] = dgt.astype(BF16)
        do_ref[...] = _nt(dao, wo_ref[...]).astype(BF16)
        du3 = _nt(dco, wcp_ref[...])
        xhat, rstd = _ln_stats(u1_ref[...].astype(F32))
        g = lg_ref[...]
        u2 = xhat * g + lb_ref[...]
        su = _sigmoid(u2)
        u3_ref[...] = (u2 * su).astype(BF16)
        du2 = du3 * (su * (1.0 + u2 * (1.0 - su)))
        dxh = du2 * g
        du1 = rstd * (dxh - jnp.mean(dxh, axis=-1, keepdims=True) - xhat * jnp.mean(dxh * xhat, axis=-1, keepdims=True))
        du1_ref[...] = du1.astype(BF16)

        @pl.when(pl.program_id(0) == 0)
        def _():
            sums_ref[...] = jnp.zeros_like(sums_ref)
        for r, val in enumerate((dgc, dgt, du2 * xhat, du2, du1)):
            sums_ref[pl.ds(r, 1), :] += jnp.sum(val, axis=0, keepdims=True)

    big = jax.ShapeDtypeStruct((T, D), BF16)
    vec = _resident((1, D))
    return _call(
        body, name=name, grid=(T // tm,),
        out_shape=(big,) * 7 + (jax.ShapeDtypeStruct((8, D), F32),),
        in_specs=[_rows(tm, D)] * 6 + [vec, vec, _resident((D, D)), _resident((D, D)), _resident((D, D))],
        out_specs=(_rows(tm, D),) * 7 + (_acc_spec((8, D)),),
        args=(dx2, u1, co, ao, sgc, sgt, ln_g, ln_b, wcp, wo, wout), carry=carry)


def conv_bwd(du1, u0, ga, gb, dw_w, name, carry=None):
    T, D = du1.shape
    tm = min(ROW_TILE, T)
    nt = T // tm
    K = dw_w.shape[0]
    per = tm // CONV_HALO

    def body(d_ref, dn_ref, u_ref, uh_ref, ga_ref, gb_ref, w_ref, dga_ref, dgb_ref, dw_ref, dbuf, ubuf, du0_buf, taps):
        i = pl.program_id(0)
        dbuf[pl.ds(0, tm), :] = d_ref[...].astype(F32)
        dbuf[pl.ds(tm, CONV_HALO), :] = dn_ref[...].astype(F32) * (i < nt - 1).astype(F32)
        ubuf[pl.ds(0, CONV_HALO), :] = uh_ref[...].astype(F32) * (i > 0).astype(F32)
        ubuf[pl.ds(CONV_HALO, tm), :] = u_ref[...].astype(F32)

        @pl.when(i == 0)
        def _():
            taps[...] = jnp.zeros_like(taps)

        _conv_anticausal(dbuf, w_ref, du0_buf, tm, D, K)
        _conv_tap_grads(dbuf, ubuf, taps, tm, D, K)
        du0 = du0_buf[...]
        ga, gb = ga_ref[...].astype(F32), gb_ref[...].astype(F32)
        sg = _sigmoid(gb)
        dga_ref[...] = (du0 * sg).astype(BF16)
        dgb_ref[...] = (du0 * ga * sg * (1.0 - sg)).astype(BF16)

        @pl.when(i == nt - 1)
        def _():
            for k in range(K):
                dw_ref[pl.ds(k, 1), :] = jnp.sum(taps[k], axis=0, keepdims=True)

    nxt = pl.BlockSpec((CONV_HALO, D), lambda i: (jnp.minimum((i + 1) * per, nt * per - 1), 0))
    big = jax.ShapeDtypeStruct((T, D), BF16)
    return _call(
        body, name=name, grid=(nt,),
        out_shape=(big, big, jax.ShapeDtypeStruct((K, D), F32)),
        in_specs=[_rows(tm, D), nxt, _rows(tm, D), _halo_rows_spec(tm, D, CONV_HALO), _rows(tm, D), _rows(tm, D),
                  _resident((K, D))],
        out_specs=(_rows(tm, D), _rows(tm, D), _acc_spec((K, D))),
        scratch_shapes=[pltpu.VMEM((tm + CONV_HALO, D), F32), pltpu.VMEM((CONV_HALO + tm, D), F32), pltpu.VMEM((tm, D), F32),
                        pltpu.VMEM((K, SUBLANES, D), F32)],
        args=(du1, du1, u0, u0, ga, gb, dw_w), carry=carry)


def mix_in_bwd(dx2, x, gain, wint, wq, pieces, name):
    T, D = x.shape
    tm = min(ROW_TILE, T)
    widths = [p.shape[1] for p in pieces]
    offs = _w_in_rows(D)

    def body(dx2_ref, x_ref, g_ref, w_ref, wq_ref, *rest):
        piece_refs, (dx_ref, dg_ref) = rest[:len(pieces)], rest[len(pieces):]
        dh = None
        for n, (p_ref, off, w) in enumerate(zip(piece_refs, offs, widths)):
            t = _dot(p_ref[...], wq_ref[...] if n == 2 else w_ref[pl.ds(off, w), :])
            dh = t if dh is None else dh + t
        dx, dgt = _rms_bwd(x_ref[...], g_ref[...], dh)
        dx_ref[...] = dx2_ref[...] + dx

        @pl.when(pl.program_id(0) == 0)
        def _():
            dg_ref[...] = jnp.zeros_like(dg_ref)
        dg_ref[...] += jnp.sum(dgt, axis=0, keepdims=True)

    return pl.pallas_call(
        body, name=name, grid=(T // tm,),
        out_shape=(jax.ShapeDtypeStruct((T, D), F32), jax.ShapeDtypeStruct((1, D), F32)),
        in_specs=[_rows(tm, D), _rows(tm, D), _resident((1, D)), _resident(wint.shape), _resident(wq.shape)]
        + [_rows(tm, w) for w in widths],
        out_specs=(_rows(tm, D), _acc_spec((1, D))),
        compiler_params=_params("arbitrary"),
    )(dx2, x, gain, wint, wq, *pieces)


def final_loss(x, gain, target, name):
    T, D = x.shape
    tm = min(ROW_TILE, T)

    def body(x_ref, g_ref, t_ref, loss_ref, dx_ref, dg_ref):
        x, g = x_ref[...], g_ref[...]
        err = _rms_fwd(x, g) - t_ref[...]
        dx, dgt = _rms_bwd(x, g, err * (1.0 / D))
        dx_ref[...] = dx

        @pl.when(pl.program_id(0) == 0)
        def _():
            dg_ref[...] = jnp.zeros_like(dg_ref)
            loss_ref[...] = jnp.zeros_like(loss_ref)
        dg_ref[...] += jnp.sum(dgt, axis=0, keepdims=True)
        per_token = jnp.sum(err * err, axis=-1, keepdims=True) * (0.5 / D)
        loss_ref[...] += jnp.broadcast_to(jnp.sum(per_token, axis=0, keepdims=True), (1, LANES))

    return pl.pallas_call(
        body, name=name, grid=(T // tm,),
        out_shape=(jax.ShapeDtypeStruct((1, LANES), F32), jax.ShapeDtypeStruct((T, D), F32),
                   jax.ShapeDtypeStruct((1, D), F32)),
        in_specs=[_rows(tm, D), _resident((1, D)), _rows(tm, D)],
        out_specs=(_acc_spec((1, LANES)), _rows(tm, D), _acc_spec((1, D))),
        compiler_params=_params("arbitrary"),
    )(x, gain, target)


def rope_tables(positions):
    half = HEAD_DIM // 2
    inv_freq = ROPE_THETA ** (-jnp.arange(half, dtype=F32) / half)
    ang = positions.astype(F32)[:, None] * inv_freq
    cos, sin = jnp.cos(ang), jnp.sin(ang)
    reps = LANES // HEAD_DIM
    return jnp.tile(jnp.concatenate([cos, cos], axis=-1), (1, reps)), jnp.tile(jnp.concatenate([-sin, sin], axis=-1), (1, reps))


def _place():
    return lax.axis_index("x"), lax.axis_index("y"), lax.axis_index("c")


def all_gather(blocks, name):
    n = len(blocks)
    send = gather_send(blocks)
    forward = gather_forward(send.out_shapes)
    n_sems = len(send.sems)

    def body(*refs):
        ins, outs, sems = refs[:n], refs[n:2 * n], refs[2 * n:]
        send.start(ins, outs, sems[:n_sems])
        send.finish(ins, outs, sems[:n_sems])
        forward.start((), outs, sems[n_sems:])
        forward.finish((), outs, sems[n_sems:])

    return pl.pallas_call(
        body, name=name, out_shape=tuple(send.out_shapes), in_specs=[_ANY] * n, out_specs=(_ANY,) * n,
        scratch_shapes=list(send.sems) + list(forward.sems),
    )(*blocks)


def _chips_across(x, y):
    return [(1 - x, y), (x, 1 - y), (1 - x, 1 - y)]


def _dev_index(x, y, c):
    return 4 * x + 2 * y + c


def gather_send(blocks):
    n = len(blocks)

    def copies(in_refs, out_refs, sems):
        send, recv, local = sems
        x, y, c = _place()
        targets = [(x, y, 1 - c)] + [(*chip, c) for chip in _chips_across(x, y)]
        outgoing, incoming, mine = [], [], []
        for i, (x_ref, out_ref) in enumerate(zip(in_refs, out_refs)):
            for k, t in enumerate(targets):
                pair = dict(send_sem=send.at[4 * i + k], recv_sem=recv.at[4 * i + k], device_id=t, device_id_type=MESH)
                outgoing.append(pltpu.make_async_remote_copy(src_ref=x_ref, dst_ref=out_ref.at[_dev_index(x, y, c)], **pair))
                incoming.append(pltpu.make_async_remote_copy(src_ref=x_ref, dst_ref=out_ref.at[_dev_index(*t)], **pair))
            mine.append(pltpu.make_async_copy(x_ref, out_ref.at[_dev_index(x, y, c)], local.at[i]))
        return outgoing, incoming, mine

    def start(*refs):
        outgoing, _, mine = copies(*refs)
        for cp in mine + outgoing:
            cp.start()

    def finish(*refs):
        outgoing, incoming, mine = copies(*refs)
        for cp in incoming:
            cp.wait_recv()
        for cp in outgoing:
            cp.wait_send()
        for cp in mine:
            cp.wait()

    return Carry(ins=tuple(blocks), out_shapes=tuple(jax.ShapeDtypeStruct((N_DEV,) + b.shape, b.dtype) for b in blocks),
                 aliases={}, sems=(pltpu.SemaphoreType.DMA((4 * n,)), pltpu.SemaphoreType.DMA((4 * n,)),
                                   pltpu.SemaphoreType.DMA((n,))), start=start, finish=finish)


def gather_forward(gathered):
    n = len(gathered)

    def copies(in_refs, out_refs, sems):
        send, recv = sems
        x, y, c = _place()
        outgoing, incoming = [], []
        for i, buf in enumerate(out_refs):
            for k, chip in enumerate(_chips_across(x, y)):
                pair = dict(send_sem=send.at[3 * i + k], recv_sem=recv.at[3 * i + k], device_id=(x, y, 1 - c),
                            device_id_type=MESH)
                rows = buf.at[_dev_index(*chip, c)]
                outgoing.append(pltpu.make_async_remote_copy(src_ref=rows, dst_ref=rows, **pair))
                theirs = buf.at[_dev_index(*chip, 1 - c)]
                incoming.append(pltpu.make_async_remote_copy(src_ref=theirs, dst_ref=theirs, **pair))
        return outgoing, incoming

    def start(*refs):
        for cp in copies(*refs)[0]:
            cp.start()

    def finish(*refs):
        outgoing, incoming = copies(*refs)
        for cp in incoming:
            cp.wait_recv()
        for cp in outgoing:
            cp.wait_send()

    return Carry(ins=tuple(gathered), out_shapes=tuple(jax.ShapeDtypeStruct(g.shape, g.dtype) for g in gathered),
                 aliases={i: i for i in range(n)},
                 sems=(pltpu.SemaphoreType.DMA((3 * n,)), pltpu.SemaphoreType.DMA((3 * n,))), start=start, finish=finish)


def swap_halves(by_core):
    n = len(by_core)

    def copies(in_refs, out_refs, sems):
        send, recv = sems
        x, y, c = _place()
        return [pltpu.make_async_remote_copy(src_ref=a.at[:, 1 - c], dst_ref=r, send_sem=send.at[i], recv_sem=recv.at[i],
                                             device_id=(x, y, 1 - c), device_id_type=MESH)
                for i, (a, r) in enumerate(zip(in_refs, out_refs))]

    def start(*refs):
        for cp in copies(*refs):
            cp.start()

    def finish(*refs):
        for cp in copies(*refs):
            cp.wait()

    shapes = tuple(jax.ShapeDtypeStruct((a.shape[0],) + a.shape[2:], a.dtype) for a in by_core)
    return Carry(ins=tuple(by_core), out_shapes=shapes, aliases={},
                 sems=(pltpu.SemaphoreType.DMA((n,)), pltpu.SemaphoreType.DMA((n,))), start=start, finish=finish)


def exchange_between_chips(by_chip):
    n = len(by_chip)

    def copies(in_refs, out_refs, sems):
        send, recv = sems
        x, y, c = _place()
        out = []
        for i, (s, r) in enumerate(zip(in_refs, out_refs)):
            for k, (tx, ty) in enumerate(_chips_across(x, y)):
                out.append(pltpu.make_async_remote_copy(
                    src_ref=s.at[2 * tx + ty], dst_ref=r.at[k], send_sem=send.at[3 * i + k], recv_sem=recv.at[3 * i + k],
                    device_id=(tx, ty, c), device_id_type=MESH))
        return out

    def start(*refs):
        for cp in copies(*refs):
            cp.start()

    def finish(*refs):
        for cp in copies(*refs):
            cp.wait()

    shapes = tuple(jax.ShapeDtypeStruct((3,) + a.shape[1:], a.dtype) for a in by_chip)
    return Carry(ins=tuple(by_chip), out_shapes=shapes, aliases={},
                 sems=(pltpu.SemaphoreType.DMA((3 * n,)), pltpu.SemaphoreType.DMA((3 * n,))), start=start, finish=finish)


def run_exchange(carry, name):
    n_in = len(carry.ins)
    n_out = len(carry.out_shapes)

    def body(*refs):
        parts = refs[:n_in], refs[n_in:n_in + n_out], refs[n_in + n_out:]
        carry.start(*parts)
        carry.finish(*parts)

    return pl.pallas_call(
        body, name=name, out_shape=tuple(carry.out_shapes), in_specs=[_ANY] * n_in, out_specs=(_ANY,) * n_out,
        scratch_shapes=list(carry.sems), input_output_aliases=dict(carry.aliases),
    )(*carry.ins)


def pair_sum(my_core, by_core, received, name):
    n = len(by_core)

    def body(core_ref, *refs):
        for a_ref, b_ref, o_ref in zip(refs[:n], refs[n:2 * n], refs[2 * n:]):
            o_ref[0] = (a_ref[0, 0].astype(F32) + b_ref[0].astype(F32)).astype(BF16)

    mine = [pl.BlockSpec((1, 1) + a.shape[2:], lambda j, core: (j, core[0], 0, 0)) for a in by_core]
    theirs = [pl.BlockSpec((1,) + r.shape[1:], lambda j, core: (j, 0, 0)) for r in received]
    return pl.pallas_call(
        body, name=name, out_shape=tuple(jax.ShapeDtypeStruct(r.shape, BF16) for r in received),
        grid_spec=pltpu.PrefetchScalarGridSpec(num_scalar_prefetch=1, grid=(by_core[0].shape[0],), in_specs=mine + theirs,
                                               out_specs=tuple(theirs)),
        compiler_params=_params("arbitrary"),
    )(my_core, *by_core, *received)


def chip_sum(my_chip, by_chip, received, name):
    n = len(by_chip)

    def body(chip_ref, *refs):
        for a_ref, b_ref, o_ref in zip(refs[:n], refs[n:2 * n], refs[2 * n:]):
            acc = a_ref[0].astype(F32)
            for k in range(3):
                acc = acc + b_ref[k].astype(F32)
            o_ref[...] = acc

    own = [pl.BlockSpec((1,) + a.shape[1:], lambda i, chip: (chip[0], 0, 0)) for a in by_chip]
    got = [pl.BlockSpec(r.shape, lambda i, chip: (0, 0, 0)) for r in received]
    outs = tuple(pl.BlockSpec(a.shape[1:], lambda i, chip: (0, 0)) for a in by_chip)
    return pl.pallas_call(
        body, name=name, out_shape=tuple(jax.ShapeDtypeStruct(a.shape[1:], F32) for a in by_chip),
        grid_spec=pltpu.PrefetchScalarGridSpec(num_scalar_prefetch=1, grid=(1,), in_specs=own + got, out_specs=outs),
        compiler_params=_params("arbitrary"),
    )(my_chip, *by_chip, *received)


def _adamw_math(w, g, m, v):
    m = ADAM_B1 * m + (1.0 - ADAM_B1) * g
    v = ADAM_B2 * v + (1.0 - ADAM_B2) * (g * g)
    m_hat = m / (1.0 - ADAM_B1 ** ADAM_STEP)
    v_hat = v / (1.0 - ADAM_B2 ** ADAM_STEP)
    delta = -ADAM_LR * (m_hat / (jnp.sqrt(v_hat) + ADAM_EPS) + ADAM_WD * w)
    return delta, m, v


def adamw(ws, gs, ms, vs, name):
    n = len(ws)

    def body(*refs):
        ins, outs = refs[:4 * n], refs[4 * n:]
        for t in range(n):
            delta, m, v = _adamw_math(ins[t][...], ins[n + t][...], ins[2 * n + t][...], ins[3 * n + t][...])
            outs[3 * t][...] = delta
            outs[3 * t + 1][...] = m
            outs[3 * t + 2][...] = v

    shapes = []
    for w in ws:
        shapes += [jax.ShapeDtypeStruct(w.shape, F32)] * 3
    res = pl.pallas_call(body, name=name, out_shape=tuple(shapes), compiler_params=_params())(*ws, *gs, *ms, *vs)
    return [tuple(res[3 * t:3 * t + 3]) for t in range(n)]


def adamw_replicated(w, partials, m, v, name):
    def body(w_ref, p_ref, m_ref, v_ref, g_ref, d_ref, mo_ref, vo_ref):
        g = p_ref[0]
        for k in range(1, N_DEV):
            g = g + p_ref[k]
        g_ref[...] = g
        d_ref[...], mo_ref[...], vo_ref[...] = _adamw_math(w_ref[...], g, m_ref[...], v_ref[...])

    shape = jax.ShapeDtypeStruct(w.shape, F32)
    return pl.pallas_call(body, name=name, out_shape=(shape,) * 4, compiler_params=_params())(w, partials, m, v)


PACK_COLS = 1024
PACK_ROW_ALIGN = 16

SHARDED = {"ffn1_w_gate": 1, "ffn1_w_up": 1, "ffn1_w_down": 0, "w_in": 1, "conv_dw_w": 1, "conv_w_proj": 0, "attn_w_o": 0,
           "w_out": 0, "ffn2_w_gate": 1, "ffn2_w_up": 1, "ffn2_w_down": 0}
REPLICATED = ("ffn1_norm", "mix_norm", "conv_dw_b", "conv_ln_g", "conv_ln_b", "ffn2_norm", "final_norm", "gate_b", "attn_sinks")
WEIGHT_ORDER = ("ffn1_norm", "ffn1_w_gate", "ffn1_w_up", "ffn1_w_down", "mix_norm", "w_in", "conv_dw_w", "conv_dw_b", "conv_ln_g",
                "conv_ln_b", "conv_w_proj", "attn_sinks", "attn_w_o", "gate_b", "w_out", "ffn2_norm", "ffn2_w_gate", "ffn2_w_up",
                "ffn2_w_down", "final_norm")


def _to_rows(flat, lead):
    n = flat.shape[-1]
    rows = -(-n // PACK_COLS)
    flat = jnp.pad(flat, [(0, 0)] * lead + [(0, rows * PACK_COLS - n)])
    return flat.reshape(flat.shape[:lead] + (rows, PACK_COLS))


def _pad_rows(a, axis):
    rows = a.shape[axis]
    pad = -rows % PACK_ROW_ALIGN
    widths = [(0, 0)] * a.ndim
    widths[axis] = (0, pad)
    return jnp.pad(a, widths)


def _from_rows(rows, shape):
    n = 1
    for s in shape:
        n *= s
    return rows.reshape(rows.shape[:-2] + (-1,))[..., :n].reshape(rows.shape[:-2] + tuple(shape))


def _pack_weights(parts):
    layout, off = [], 0
    for p in parts:
        layout.append((off, p.shape[0]))
        off += p.shape[0]
    return _pad_rows(jnp.concatenate(parts, axis=0), 0), layout


def _gathered_rows(gathered, off, rows):
    return gathered[:, off:off + rows].reshape(N_DEV * rows, gathered.shape[2])


def _heads_slot_major(rows):
    group = rows.shape[0] // (N_KV_HEADS * HEAD_DIM)
    return rows.reshape(N_KV_HEADS, group, HEAD_DIM, rows.shape[1]).transpose(1, 0, 2, 3).reshape(rows.shape)


def _heads_kv_major(rows):
    group = rows.shape[0] // (N_KV_HEADS * HEAD_DIM)
    return rows.reshape(group, N_KV_HEADS, HEAD_DIM, rows.shape[1]).transpose(1, 0, 2, 3).reshape(rows.shape)


def _by_core(full_rows):
    return full_rows.reshape((N_DEV // 2, 2, full_rows.shape[0] // N_DEV, full_rows.shape[1]))


def kernel(x, positions, ffn1_norm, ffn1_w_gate, ffn1_w_up, ffn1_w_down, mix_norm, w_in, conv_dw_w, conv_dw_b, conv_ln_g, conv_ln_b, conv_w_proj, attn_sinks, attn_w_o, gate_b, w_out, ffn2_norm, ffn2_w_gate, ffn2_w_up, ffn2_w_down, final_norm, loss_target, m_ffn1_norm, m_ffn1_w_gate, m_ffn1_w_up, m_ffn1_w_down, m_mix_norm, m_w_in, m_conv_dw_w, m_conv_dw_b, m_conv_ln_g, m_conv_ln_b, m_conv_w_proj, m_attn_sinks, m_attn_w_o, m_gate_b, m_w_out, m_ffn2_norm, m_ffn2_w_gate, m_ffn2_w_up, m_ffn2_w_down, m_final_norm, v_ffn1_norm, v_ffn1_w_gate, v_ffn1_w_up, v_ffn1_w_down, v_mix_norm, v_w_in, v_conv_dw_w, v_conv_dw_b, v_conv_ln_g, v_conv_ln_b, v_conv_w_proj, v_attn_sinks, v_attn_w_o, v_gate_b, v_w_out, v_ffn2_norm, v_ffn2_w_gate, v_ffn2_w_up, v_ffn2_w_down, v_final_norm):
    given = dict(locals())
    shapes = {n: given[n].shape for n in WEIGHT_ORDER}
    w = {n: given[n].reshape(given[n].shape[-2:]) if given[n].ndim == 3 else given[n].reshape(1, -1) for n in WEIGHT_ORDER}
    m = {n: given["m_" + n].reshape(w[n].shape) for n in WEIGHT_ORDER}
    v = {n: given["v_" + n].reshape(w[n].shape) for n in WEIGHT_ORDER}
    my_x, my_y, my_c = _place()
    my_core = my_c.astype(jnp.int32).reshape(1)
    my_chip = (2 * my_x + my_y).astype(jnp.int32).reshape(1)
    xs, target = x[0], loss_target[0]
    T, D = xs.shape
    KV = N_KV_HEADS * HEAD_DIM
    K = w["conv_dw_w"].shape[0]

    def t16(n):
        return w[n].T.astype(BF16)

    def r16(n):
        return w[n].astype(BF16)

    blocks1 = [t16("ffn1_w_gate"), t16("ffn1_w_up"), r16("ffn1_w_down")]
    dw_bits = _pad_rows(_to_rows(lax.bitcast_convert_type(w["conv_dw_w"], BF16).reshape(-1), 0), 0)
    blocks2 = [t16("w_in"), r16("conv_w_proj"), r16("attn_w_o"), r16("w_out"), dw_bits]
    blocks3 = [t16("ffn2_w_gate"), t16("ffn2_w_up"), r16("ffn2_w_down")]
    cos, sin = rope_tables(positions[0])
    sink_col = jnp.repeat(w["attn_sinks"].reshape(-1), WINDOW).reshape(N_KV_HEADS, (D // KV) * WINDOW, 1)

    def full(gathered):
        return gathered.reshape(-1, gathered.shape[2])

    wgt1, wut1, wd1 = (full(g) for g in all_gather(blocks1, "gather_ffn1"))
    (x1, h1, a1, b1, s1), gath2 = ffn_fwd(xs, w["ffn1_norm"], wgt1, wut1, wd1, "ffn1_fwd", carry=gather_send(blocks2))
    gath2 = run_exchange(gather_forward(gath2), "gather_mix_forward")
    wint, wcp, wo, wout = (full(g) for g in gath2[:4])
    wq = _heads_slot_major(wint[2 * D:3 * D])
    wo = _heads_slot_major(wo)
    dw_full = lax.bitcast_convert_type(_from_rows(gath2[4], w["conv_dw_w"].shape + (2,)), F32)
    dw_full = dw_full.transpose(1, 0, 2).reshape(K, D)
    (h2, ga, gb, u0, q, sgc, sgt, kk, vv), gath3 = mix_in_fwd(x1, w["mix_norm"], wint, wq, w["gate_b"], cos, sin,
                                                              "mix_in_fwd", carry=gather_send(blocks3))
    o, gath3 = attn_fwd(q, kk, vv, sink_col, "attn_fwd", carry=gather_forward(gath3))
    x2, u1, co, ao, merged = mix_out_fwd(x1, u0, o, sgc, sgt, dw_full, w["conv_dw_b"], w["conv_ln_g"], w["conv_ln_b"],
                                         wcp, wo, wout, "mix_out_fwd")
    wgt2, wut2, wd2 = (full(g) for g in gath3)
    loss, dx3, d_final, h3, a2, b2, s2 = ffn_fwd_loss(x2, w["ffn2_norm"], wgt2, wut2, wd2, w["final_norm"], target,
                                                      "ffn2_fwd_loss")

    small = {"final_norm": d_final}
    (da2, db2), _ = ffn_bwd_hidden(dx3, a2, b2, wd2, "ffn2_bwd_hidden")
    (dx2, small["ffn2_norm"]), _ = ffn_bwd_input(dx3, x2, w["ffn2_norm"], da2, db2, wgt2, wut2, "ffn2_bwd_input")
    core2 = [_by_core(g) for g in (wgrad(da2, h3, "ffn2_dwg")[0], wgrad(db2, h3, "ffn2_dwu")[0],
                                   wgrad(s2, dx3, "ffn2_dwd", b_scale=FFN_SCALE)[0])]
    (dgc, dgt, do, du1, dco, dao, u3, sums), recv2 = mix_out_bwd(dx2, u1, co, ao, sgc, sgt, w["conv_ln_g"], w["conv_ln_b"],
                                                                 wcp, wo, wout, "mix_out_bwd", carry=swap_halves(core2))
    chip2 = pair_sum(my_core, core2, recv2, "ffn2_grads_pair_sum")
    small["gate_b"] = jnp.concatenate([sums[0:1], sums[1:2]], axis=1)
    small["conv_ln_g"], small["conv_ln_b"], small["conv_dw_b"] = sums[2:3], sums[3:4], sums[4:5]
    g_wout = wgrad(merged, dx2, "dw_out")[0]
    g_wcp = wgrad(u3, dco, "dw_conv_proj")[0]
    g_wo = _heads_kv_major(wgrad(o, dao, "dw_attn_o")[0])
    (dga, dgb, g_dw), got2 = conv_bwd(du1, u0, ga, gb, dw_full, "conv_bwd", carry=exchange_between_chips(chip2))
    dq, dk, dv, dsink = attn_bwd(q, kk, vv, do, sink_col, cos, sin, "attn_bwd")
    small["attn_sinks"] = dsink[:D // KV, :N_KV_HEADS].T.reshape(1, -1)
    pieces = [dga, dgb, dq, dk, dv, dgc, dgt]
    dx1, small["mix_norm"] = mix_in_bwd(dx2, x1, w["mix_norm"], wint, wq, pieces, "mix_in_bwd")
    group = D // KV
    q_moves = [(2 * D + HEAD_DIM * (group * g + hh), 2 * D + HEAD_DIM * (N_KV_HEADS * hh + g), HEAD_DIM)
               for g in range(N_KV_HEADS) for hh in range(group)]
    g_wint = jnp.concatenate([wgrad_stacked(pieces[:3], h2, "dw_in_a", moves=[(0, 0, 2 * D)] + q_moves),
                              wgrad_stacked(pieces[3:], h2, "dw_in_b")], axis=0)
    corem = [_by_core(a) for a in (g_wint, g_wcp, g_wo, g_wout)]

    (da1, db1), recvm = ffn_bwd_hidden(dx1, a1, b1, wd1, "ffn1_bwd_hidden", carry=swap_halves(corem))
    chipm = pair_sum(my_core, corem, recvm, "mix_grads_pair_sum")
    g1c, gotm_a = wgrad(s1, dx1, "ffn1_dwd", carry=exchange_between_chips(chipm[:1]), b_scale=FFN_SCALE)
    g1a, gotm_b = wgrad(da1, h1, "ffn1_dwg", carry=exchange_between_chips(chipm[1:]))
    g1b = wgrad(db1, h1, "ffn1_dwu")[0]
    core1 = [_by_core(a) for a in (g1a, g1b, g1c)]
    chip1 = pair_sum(my_core, core1, run_exchange(swap_halves(core1), "ffn1_grads_swap"), "ffn1_grads_pair_sum")
    (grad_x, small["ffn1_norm"]), got1 = ffn_bwd_input(dx1, xs, w["ffn1_norm"], da1, db1, wgt1, wut1, "ffn1_bwd_input",
                                                       carry=exchange_between_chips(chip1))

    gs2 = chip_sum(my_chip, chip2, got2, "ffn2_grads_sum")
    gsm = chip_sum(my_chip, chipm, gotm_a + gotm_b, "mix_grads_sum")
    gs1 = chip_sum(my_chip, chip1, got1, "ffn1_grads_sum")
    grads_t = {"ffn1_w_gate": gs1[0], "ffn1_w_up": gs1[1], "ffn1_w_down": gs1[2],
               "ffn2_w_gate": gs2[0], "ffn2_w_up": gs2[1], "ffn2_w_down": gs2[2],
               "w_in": gsm[0], "conv_w_proj": gsm[1], "attn_w_o": gsm[2], "w_out": gsm[3]}
    grads = {}

    def pack_small(d, taps, extra):
        rows = [_to_rows(d[n].reshape(-1), 0) for n in REPLICATED] + [taps, _to_rows(extra.reshape(-1), 0)]
        return _pad_rows(jnp.concatenate(rows, axis=0), 0)

    zero, no_taps = jnp.zeros((1, LANES), F32), jnp.zeros((K, D), F32)
    (shares,) = all_gather([pack_small(small, g_dw, loss)], "gather_small_grads")
    g_s, d_s, m_s, v_s = adamw_replicated(pack_small(w, no_taps, zero), shares, pack_small(m, no_taps, zero),
                                          pack_small(v, no_taps, zero), "adamw_replicated")
    delta, new_m, new_v = {}, {}, {}
    off = 0
    for n in REPLICATED:
        r = -(-w[n].shape[1] // PACK_COLS)
        grads[n], delta[n], new_m[n], new_v[n] = (_from_rows(a[off:off + r], w[n].shape) for a in (g_s, d_s, m_s, v_s))
        off += r
    shard_cols = w["conv_dw_w"].shape[1]
    grads_t["conv_dw_w"] = lax.dynamic_slice_in_dim(g_s[off:off + K], _dev_index(my_x, my_y, my_c) * shard_cols, shard_cols,
                                                    axis=1)
    total_loss = g_s[off + K, 0]

    def like(a, ref):
        return a if a.shape == ref.shape else a.T

    groups = (("ffn1_w_gate", "ffn1_w_up", "ffn1_w_down"), ("ffn2_w_gate", "ffn2_w_up", "ffn2_w_down"),
              ("w_in", "conv_dw_w", "conv_w_proj", "attn_w_o", "w_out"))
    for k, names in enumerate(groups):
        res = adamw([like(w[n], grads_t[n]) for n in names], [grads_t[n] for n in names],
                    [like(m[n], grads_t[n]) for n in names], [like(v[n], grads_t[n]) for n in names], "adamw_%d" % k)
        for n, (d, mm, vv) in zip(names, res):
            grads[n], delta[n], new_m[n], new_v[n] = (like(a, w[n]) for a in (grads_t[n], d, mm, vv))

    out = [total_loss, grad_x[None]]
    for d in (grads, delta, new_m, new_v):
        out += [d[n].reshape(shapes[n]) for n in WEIGHT_ORDER]
    return tuple(out)
```

```python
import functools
from typing import Callable, NamedTuple

import jax
import jax.numpy as jnp
from jax import lax
from jax.experimental import pallas as pl
from jax.experimental.pallas import tpu as pltpu

F32, BF16 = jnp.float32, jnp.bfloat16

HEAD_DIM = 64
N_KV_HEADS = 4
WINDOW = 128
CONV_WIDTH = 31
ROPE_THETA = 10000.0
EPS = 1e-6
LN_EPS = 1e-5
NEG_INF = -1e30
ADAM_LR, ADAM_B1, ADAM_B2, ADAM_EPS, ADAM_WD, ADAM_STEP = 0.001, 0.9, 0.999, 1e-08, 0.01, 10

N_DEV = 8
LANES = 128
SUBLANES = 8
CONV_HALO = 32
CONV_ROWS, CONV_LANES = 64, 256
ROW_TILE = 512
FFN_CHUNK = 256
FFN_SCALE = 0.5
WGRAD_TILE_ELEMS = 2 ** 22
WGRAD_TILE_ROWS = 2048
VMEM_LIMIT = 56 * 2 ** 20
MESH = pl.DeviceIdType.MESH


def _params(*sem):
    return pltpu.CompilerParams(dimension_semantics=sem or None, vmem_limit_bytes=VMEM_LIMIT)


def _resident(shape):
    zeros = (0,) * len(shape)
    return pl.BlockSpec(shape, lambda *_: zeros, pipeline_mode=pl.Buffered(1))


def _rows(tm, n):
    return pl.BlockSpec((tm, n), lambda i: (i, 0))


def _acc_spec(shape):
    zeros = (0,) * len(shape)
    return pl.BlockSpec(shape, lambda *_: zeros)


_ANY = pl.BlockSpec(memory_space=pl.ANY)


class Carry(NamedTuple):
    ins: tuple
    out_shapes: tuple
    aliases: dict
    sems: tuple
    start: Callable
    finish: Callable
    relay: Callable = None


def _call(body, *, name, grid, in_specs, out_specs, out_shape, args, scratch_shapes=(), carry=None):
    n_in, n_out, n_scr = len(in_specs), len(out_specs), len(scratch_shapes)
    params = _params(*(("arbitrary",) * len(grid)))
    if carry is None:
        res = pl.pallas_call(body, name=name, grid=grid, in_specs=list(in_specs), out_specs=tuple(out_specs),
                             out_shape=tuple(out_shape), scratch_shapes=list(scratch_shapes), compiler_params=params)(*args)
        return tuple(res), ()
    c_in, c_out = len(carry.ins), len(carry.out_shapes)

    def wrapped(*refs):
        ins, c_ins = refs[:n_in], refs[n_in:n_in + c_in]
        p = n_in + c_in
        outs, c_outs = refs[p:p + n_out], refs[p + n_out:p + n_out + c_out]
        p += n_out + c_out
        scr, c_sems = refs[p:p + n_scr], refs[p + n_scr:]
        ids = [pl.program_id(d) for d in range(len(grid))]
        first = functools.reduce(jnp.logical_and, [i == 0 for i in ids])
        last = functools.reduce(jnp.logical_and, [i == n - 1 for i, n in zip(ids, grid)])

        @pl.when(first)
        def _():
            carry.start(c_ins, c_outs, c_sems)

        body(*ins, *outs, *scr)

        if carry.relay is not None:
            @pl.when(ids[0] == (3 * grid[0]) // 4)
            def _():
                carry.relay(c_ins, c_outs, c_sems)

        @pl.when(last)
        def _():
            carry.finish(c_ins, c_outs, c_sems)

    res = pl.pallas_call(
        wrapped, name=name, grid=grid, in_specs=list(in_specs) + [_ANY] * c_in, out_specs=tuple(out_specs) + (_ANY,) * c_out,
        out_shape=tuple(out_shape) + tuple(carry.out_shapes), scratch_shapes=list(scratch_shapes) + list(carry.sems),
        input_output_aliases={n_in + i: n_out + o for i, o in carry.aliases.items()}, compiler_params=params,
    )(*args, *carry.ins)
    return tuple(res[:n_out]), tuple(res[n_out:])


def _nt(a, b):
    return lax.dot_general(a, b, (((1,), (1,)), ((), ())), preferred_element_type=F32)


def _tn(a, b):
    return lax.dot_general(a, b, (((0,), (0,)), ((), ())), preferred_element_type=F32)


def _dot(a, b):
    return jnp.dot(a, b, preferred_element_type=F32)


def _sigmoid(x):
    return 1.0 / (1.0 + jnp.exp(-x))


def _rms_fwd(x, g):
    r = lax.rsqrt(jnp.mean(x * x, axis=-1, keepdims=True) + EPS)
    return (x * r) * g


def _rms_bwd(x, g, dy):
    r = lax.rsqrt(jnp.mean(x * x, axis=-1, keepdims=True) + EPS)
    xhat = x * r
    dyg = dy * g
    dx = r * (dyg - xhat * jnp.mean(dyg * xhat, axis=-1, keepdims=True))
    return dx, dy * xhat


def _rot_half(x):
    lane = lax.broadcasted_iota(jnp.int32, (x.shape[0], LANES), 1)
    first = (lane % HEAD_DIM) < (HEAD_DIM // 2)
    out = []
    for s in range(x.shape[1] // LANES):
        xs = x[:, LANES * s:LANES * (s + 1)]
        out.append(jnp.where(first, pltpu.roll(xs, LANES - HEAD_DIM // 2, 1), pltpu.roll(xs, HEAD_DIM // 2, 1)))
    return out[0] if len(out) == 1 else jnp.concatenate(out, axis=1)


def _tile_lanes(t, width):
    return t if width == LANES else jnp.concatenate([t] * (width // LANES), axis=1)


def _rope_fwd(x, cos, sin_signed):
    w = x.shape[1]
    return x * _tile_lanes(cos, w) + _rot_half(x) * _tile_lanes(sin_signed, w)


def _rope_bwd(dy, cos, sin_signed):
    w = dy.shape[1]
    return dy * _tile_lanes(cos, w) + _rot_half(dy * _tile_lanes(sin_signed, w))


def _ffn_rows(x, g_ref, wg_ref, wu_ref, wd_ref, h_ref, a_ref, b_ref, s_ref, acc_ref):
    F = wg_ref.shape[0]
    h = _rms_fwd(x, g_ref[...]).astype(BF16)
    h_ref[...] = h
    for c in range(F // FFN_CHUNK):
        cs = pl.ds(c * FFN_CHUNK, FFN_CHUNK)
        a = _nt(h, wg_ref[cs, :])
        b = _nt(h, wu_ref[cs, :])
        a_ref[:, cs] = a.astype(BF16)
        b_ref[:, cs] = b.astype(BF16)
        s = (a * _sigmoid(a) * b).astype(BF16)
        s_ref[:, cs] = s
        y = _dot(s, wd_ref[cs, :])
        if c == 0:
            acc_ref[...] = y
        else:
            acc_ref[...] += y
    return x + FFN_SCALE * acc_ref[...]


def ffn_fwd(x, gain, wgt, wut, wd, name, carry=None):
    T, D = x.shape
    F = wgt.shape[0]
    tm = min(ROW_TILE, T)

    def body(x_ref, g_ref, wg_ref, wu_ref, wd_ref, xo_ref, h_ref, a_ref, b_ref, s_ref, acc_ref):
        xo_ref[...] = _ffn_rows(x_ref[...], g_ref, wg_ref, wu_ref, wd_ref, h_ref, a_ref, b_ref, s_ref, acc_ref)

    wide = jax.ShapeDtypeStruct((T, F), BF16)
    return _call(
        body, name=name, grid=(T // tm,),
        out_shape=(jax.ShapeDtypeStruct((T, D), F32), jax.ShapeDtypeStruct((T, D), BF16), wide, wide, wide),
        in_specs=[_rows(tm, D), _resident((1, D)), _resident((F, D)), _resident((F, D)), _resident((F, D))],
        out_specs=(_rows(tm, D), _rows(tm, D), _rows(tm, F), _rows(tm, F), _rows(tm, F)),
        scratch_shapes=[pltpu.VMEM((tm, D), F32)], args=(x, gain, wgt, wut, wd), carry=carry)


def ffn_fwd_loss(x, gain, wgt, wut, wd, final_gain, target, name):
    T, D = x.shape
    F = wgt.shape[0]
    tm = min(ROW_TILE, T)

    def body(x_ref, g_ref, wg_ref, wu_ref, wd_ref, gf_ref, t_ref,
             loss_ref, dx_ref, dg_ref, h_ref, a_ref, b_ref, s_ref, acc_ref):
        xo = _ffn_rows(x_ref[...], g_ref, wg_ref, wu_ref, wd_ref, h_ref, a_ref, b_ref, s_ref, acc_ref)
        gf = gf_ref[...]
        err = _rms_fwd(xo, gf) - t_ref[...]
        dx, dgt = _rms_bwd(xo, gf, err * (1.0 / D))
        dx_ref[...] = dx

        @pl.when(pl.program_id(0) == 0)
        def _():
            dg_ref[...] = jnp.zeros_like(dg_ref)
            loss_ref[...] = jnp.zeros_like(loss_ref)
        dg_ref[...] += jnp.sum(dgt, axis=0, keepdims=True)
        per_token = jnp.sum(err * err, axis=-1, keepdims=True) * (0.5 / D)
        loss_ref[...] += jnp.broadcast_to(jnp.sum(per_token, axis=0, keepdims=True), (1, LANES))

    return pl.pallas_call(
        body, name=name, grid=(T // tm,),
        out_shape=(jax.ShapeDtypeStruct((1, LANES), F32), jax.ShapeDtypeStruct((T, D), F32), jax.ShapeDtypeStruct((1, D), F32),
                   jax.ShapeDtypeStruct((T, D), BF16), jax.ShapeDtypeStruct((T, F), BF16), jax.ShapeDtypeStruct((T, F), BF16),
                   jax.ShapeDtypeStruct((T, F), BF16)),
        in_specs=[_rows(tm, D), _resident((1, D)), _resident((F, D)), _resident((F, D)), _resident((F, D)), _resident((1, D)),
                  _rows(tm, D)],
        out_specs=(_acc_spec((1, LANES)), _rows(tm, D), _acc_spec((1, D)), _rows(tm, D), _rows(tm, F), _rows(tm, F),
                   _rows(tm, F)),
        scratch_shapes=[pltpu.VMEM((tm, D), F32)], compiler_params=_params("arbitrary"),
    )(x, gain, wgt, wut, wd, final_gain, target)


def ffn_bwd_hidden(dxo, a, b, wd, name, carry=None):
    T, D = dxo.shape
    F = wd.shape[0]
    tm = min(ROW_TILE, T)
    fc = FFN_CHUNK

    def hidden_body(dxo_ref, a_ref, b_ref, wd_ref, da_ref, db_ref):
        g0 = (FFN_SCALE * dxo_ref[...]).astype(BF16)
        for c in range(F // fc):
            cs = pl.ds(c * fc, fc)
            ds = _nt(g0, wd_ref[cs, :])
            a = a_ref[:, cs].astype(F32)
            bb = b_ref[:, cs].astype(F32)
            sa = _sigmoid(a)
            da_ref[:, cs] = (ds * bb * (sa * (1.0 + a * (1.0 - sa)))).astype(BF16)
            db_ref[:, cs] = (ds * (a * sa)).astype(BF16)

    wide = jax.ShapeDtypeStruct((T, F), BF16)
    return _call(
        hidden_body, name=name, grid=(T // tm,), out_shape=(wide, wide),
        in_specs=[_rows(tm, D), _rows(tm, F), _rows(tm, F), _resident((F, D))],
        out_specs=(_rows(tm, F), _rows(tm, F)), args=(dxo, a, b, wd), carry=carry)


def ffn_bwd_input(dxo, x, gain, da, db, wg, wu, name, carry=None):
    T, D = x.shape
    F = wg.shape[0]
    tm = min(ROW_TILE, T)

    def input_body(dxo_ref, x_ref, g_ref, da_ref, db_ref, wg_ref, wu_ref, dx_ref, dg_ref):
        dh = _dot(da_ref[...], wg_ref[...]) + _dot(db_ref[...], wu_ref[...])
        dx, dgt = _rms_bwd(x_ref[...], g_ref[...], dh)
        dx_ref[...] = dxo_ref[...] + dx

        @pl.when(pl.program_id(0) == 0)
        def _():
            dg_ref[...] = jnp.zeros_like(dg_ref)
        dg_ref[...] += jnp.sum(dgt, axis=0, keepdims=True)

    return _call(
        input_body, name=name, grid=(T // tm,),
        out_shape=(jax.ShapeDtypeStruct((T, D), F32), jax.ShapeDtypeStruct((1, D), F32)),
        in_specs=[_rows(tm, D), _rows(tm, D), _resident((1, D)), _rows(tm, F), _rows(tm, F), _resident((F, D)),
                  _resident((F, D))],
        out_specs=(_rows(tm, D), _acc_spec((1, D))), args=(dxo, x, gain, da, db, wg, wu), carry=carry)


def wgrad(a, b, name, carry=None, b_scale=None):
    T, M = a.shape
    N = b.shape[1]
    tk = ROW_TILE
    while 2 * tk * M <= WGRAD_TILE_ELEMS and 2 * tk <= WGRAD_TILE_ROWS:
        tk *= 2
    tk = min(tk, T)
    nk = T // tk

    def body(a_ref, b_ref, o_ref, acc_ref):
        k = pl.program_id(0)
        bt = b_ref[...] if b_scale is None else b_scale * b_ref[...]
        part = _tn(a_ref[...].astype(BF16), bt.astype(BF16))

        @pl.when(k == 0)
        def _():
            acc_ref[...] = part

        @pl.when(k > 0)
        def _():
            acc_ref[...] += part

        @pl.when(k == nk - 1)
        def _():
            o_ref[...] = acc_ref[...].astype(BF16)

    (out,), carried = _call(
        body, name=name, grid=(nk,), out_shape=(jax.ShapeDtypeStruct((M, N), BF16),),
        in_specs=[_rows(tk, M), _rows(tk, N)], out_specs=(_acc_spec((M, N)),),
        scratch_shapes=[pltpu.VMEM((M, N), F32)], args=(a, b), carry=carry)
    return out, carried


def wgrad_stacked(pieces, b, name, moves=None):
    T, N = b.shape
    n = len(pieces)
    widths = [p.shape[1] for p in pieces]
    offs = [sum(widths[:i]) for i in range(n)]
    M = sum(widths)
    tk = ROW_TILE
    while 2 * tk * M <= WGRAD_TILE_ELEMS and 2 * tk <= WGRAD_TILE_ROWS:
        tk *= 2
    tk = min(tk, T)
    nk = T // tk
    moves = moves or [(0, 0, M)]

    def body(*refs):
        a_refs, b_ref, o_ref, acc_ref = refs[:n], refs[n], refs[n + 1], refs[n + 2]
        k = pl.program_id(0)

        @pl.when(k == 0)
        def _():
            acc_ref[...] = jnp.zeros_like(acc_ref)
        bt = b_ref[...].astype(BF16)
        for a_ref, off, width in zip(a_refs, offs, widths):
            acc_ref[pl.ds(off, width), :] += _tn(a_ref[...].astype(BF16), bt)

        @pl.when(k == nk - 1)
        def _():
            for to, start, rows in moves:
                o_ref[pl.ds(to, rows), :] = acc_ref[pl.ds(start, rows), :].astype(BF16)

    return pl.pallas_call(
        body, name=name, grid=(nk,), out_shape=jax.ShapeDtypeStruct((M, N), BF16),
        in_specs=[_rows(tk, width) for width in widths] + [_rows(tk, N)], out_specs=_acc_spec((M, N)),
        scratch_shapes=[pltpu.VMEM((M, N), F32)], compiler_params=_params("arbitrary"),
    )(*pieces, b)


def _w_in_rows(D):
    KV = N_KV_HEADS * HEAD_DIM
    return 0, D, 2 * D, 3 * D, 3 * D + KV, 3 * D + 2 * KV, 4 * D + 2 * KV


def mix_in_fwd(x, gain, wint, wq, gate_b, cos, sin_signed, name, carry=None):
    T, D = x.shape
    KV = N_KV_HEADS * HEAD_DIM
    tm = min(ROW_TILE, T)
    o_ga, o_gb, _, o_k, o_v, o_gc, o_gt = _w_in_rows(D)

    def body(x_ref, g_ref, w_ref, wq_ref, gb_ref, cos_ref, sin_ref,
             h_ref, ga_ref, gb_out_ref, u0_ref, q_ref, sgc_ref, sgt_ref, k_ref, v_ref):
        h = _rms_fwd(x_ref[...], g_ref[...]).astype(BF16)
        h_ref[...] = h
        cos, sin = cos_ref[...], sin_ref[...]
        ga = _nt(h, w_ref[pl.ds(o_ga, D), :])
        gb = _nt(h, w_ref[pl.ds(o_gb, D), :])
        ga_ref[...] = ga.astype(BF16)
        gb_out_ref[...] = gb.astype(BF16)
        u0_ref[...] = (ga * _sigmoid(gb)).astype(BF16)
        q = _nt(h, wq_ref[...])
        q_ref[...] = _rope_fwd(q, cos, sin).astype(BF16)
        gc = _nt(h, w_ref[pl.ds(o_gc, D), :]) + gb_ref[:, pl.ds(0, D)]
        sgc_ref[...] = _sigmoid(gc).astype(BF16)
        gt = _nt(h, w_ref[pl.ds(o_gt, D), :]) + gb_ref[:, pl.ds(D, D)]
        sgt_ref[...] = _sigmoid(gt).astype(BF16)
        k = _nt(h, w_ref[pl.ds(o_k, KV), :])
        k_ref[...] = _rope_fwd(k, cos, sin).astype(BF16)
        v_ref[...] = _nt(h, w_ref[pl.ds(o_v, KV), :]).astype(BF16)

    big = jax.ShapeDtypeStruct((T, D), BF16)
    small = jax.ShapeDtypeStruct((T, KV), BF16)
    return _call(
        body, name=name, grid=(T // tm,),
        out_shape=(big, big, big, big, big, big, big, small, small),
        in_specs=[_rows(tm, D), _resident((1, D)), _resident(wint.shape), _resident(wq.shape), _resident((1, 2 * D)),
                  _rows(tm, LANES), _rows(tm, LANES)],
        out_specs=(_rows(tm, D),) * 7 + (_rows(tm, KV),) * 2,
        args=(x, gain, wint, wq, gate_b, cos, sin_signed), carry=carry)


def _from_prev(rows):
    qi = lax.broadcasted_iota(jnp.int32, (rows, WINDOW), 0) % WINDOW
    return lax.broadcasted_iota(jnp.int32, (rows, WINDOW), 1) > qi


def _fold(x, g, from_prev):
    lo = 2 * WINDOW * g
    return jnp.where(from_prev, x[:, lo:lo + WINDOW], x[:, lo + WINDOW:lo + 2 * WINDOW])


def _unfold(folded, from_prev):
    zero = jnp.zeros_like(folded[0])
    parts = []
    for x in folded:
        parts += [jnp.where(from_prev, x, zero), jnp.where(from_prev, zero, x)]
    return jnp.concatenate(parts, axis=1)


def _kv_lane_head(rows, width):
    return lax.broadcasted_iota(jnp.int32, (rows, width), 1) // HEAD_DIM


def _block_diag(win):
    head = _kv_lane_head(*win.shape)
    zero = jnp.zeros_like(win)
    return jnp.concatenate([jnp.where(head == g, win, zero) for g in range(N_KV_HEADS)], axis=0)


def _diag_blocks_sum(bd, keys):
    head = _kv_lane_head(keys, bd.shape[1])
    out = jnp.zeros((keys, bd.shape[1]), F32)
    for g in range(N_KV_HEADS):
        out = jnp.where(head == g, bd[g * keys:(g + 1) * keys], out)
    return out


def _kv_windows(k_ref, kh_ref, v_ref, vh_ref, j):
    rows = pl.ds(j * WINDOW, WINDOW)
    if j == 0:
        kprev, vprev = kh_ref[...], vh_ref[...]
    else:
        prev = pl.ds((j - 1) * WINDOW, WINDOW)
        kprev, vprev = k_ref[prev, :], v_ref[prev, :]
    return jnp.concatenate([kprev, k_ref[rows, :]], axis=0), jnp.concatenate([vprev, v_ref[rows, :]], axis=0)


def _stack_slots(ref, j, group, KV):
    rows = pl.ds(j * WINDOW, WINDOW)
    return jnp.concatenate([ref[rows, pl.ds(KV * hh, KV)] for hh in range(group)], axis=0)


def _attn_exp(qs, kbd, sink_ref, from_prev, no_prev):
    s = _nt(qs, kbd)
    if no_prev is not None:
        qi = lax.broadcasted_iota(jnp.int32, from_prev.shape, 0) % WINDOW
        absent = lax.broadcasted_iota(jnp.int32, from_prev.shape, 1) > jnp.where(no_prev, qi, WINDOW)
    out = []
    for g in range(N_KV_HEADS):
        sg = _fold(s, g, from_prev) * (HEAD_DIM ** -0.5)
        if no_prev is not None:
            sg = jnp.where(absent, NEG_INF, sg)
        sink = sink_ref[g]
        m = jnp.maximum(jnp.max(sg, axis=-1, keepdims=True), sink)
        out.append((jnp.exp(sg - m), jnp.exp(sink - m)))
    return out


def _spread_over_heads(cols, rows, KV):
    head = _kv_lane_head(rows, KV)
    out = jnp.zeros((rows, KV), F32)
    for g, col in enumerate(cols):
        out = jnp.where(head == g, col, out)
    return out


def _halo_rows_spec(tq, width, sub):
    return pl.BlockSpec((sub, width), lambda i: (jnp.maximum(i * (tq // sub) - 1, 0), 0))


def attn_fwd(q, k, v, sink_col, name, carry=None):
    T, D = q.shape
    KV = k.shape[1]
    group = D // KV
    tq = min(ROW_TILE, T)
    nsub = tq // WINDOW
    rows, wide = group * WINDOW, N_KV_HEADS * 2 * WINDOW

    def body(q_ref, k_ref, kh_ref, v_ref, vh_ref, sink_ref, o_ref):
        from_prev = _from_prev(rows)
        head = _kv_lane_head(wide, KV)
        block = lax.broadcasted_iota(jnp.int32, head.shape, 0) // (2 * WINDOW)
        ones_bd = jnp.where(head == block, 1.0, 0.0).astype(BF16)
        for j in range(nsub):
            k_win, v_win = _kv_windows(k_ref, kh_ref, v_ref, vh_ref, j)
            parts = _attn_exp(_stack_slots(q_ref, j, group, KV), _block_diag(k_win), sink_ref, from_prev,
                              pl.program_id(0) == 0 if j == 0 else None)
            p = _unfold([pg.astype(BF16) for pg, _ in parts], from_prev)
            both = _dot(p, jnp.concatenate([_block_diag(v_win), ones_bd], axis=1))
            denom = both[:, KV:] + _spread_over_heads([es for _, es in parts], rows, KV)
            out = (both[:, :KV] / denom).astype(BF16)
            for hh in range(group):
                o_ref[pl.ds(j * WINDOW, WINDOW), pl.ds(KV * hh, KV)] = out[hh * WINDOW:(hh + 1) * WINDOW]

    (o,), carried = _call(
        body, name=name, grid=(T // tq,),
        out_shape=(jax.ShapeDtypeStruct((T, D), BF16),),
        in_specs=[_rows(tq, D), _rows(tq, KV), _halo_rows_spec(tq, KV, WINDOW), _rows(tq, KV),
                  _halo_rows_spec(tq, KV, WINDOW), _resident(sink_col.shape)],
        out_specs=(_rows(tq, D),), args=(q, k, k, v, v, sink_col), carry=carry)
    return o, carried


def attn_bwd(q, k, v, do, sink_col, cos, sin_signed, name):
    T, D = q.shape
    KV = k.shape[1]
    group = D // KV
    tq = min(ROW_TILE, T)
    nsub = tq // WINDOW
    nt = T // tq
    scale = HEAD_DIM ** -0.5
    rows, wide = group * WINDOW, N_KV_HEADS * 2 * WINDOW

    def rev(i):
        return nt - 1 - i

    def body(q_ref, k_ref, kh_ref, v_ref, vh_ref, do_ref, sink_ref, cos_ref, sin_ref,
             dq_ref, dk_ref, dv_ref, dsink_ref, dq_acc, dk_acc, dv_acc, carry_k, carry_v):
        i = pl.program_id(0)

        @pl.when(i == 0)
        def _():
            carry_k[...] = jnp.zeros_like(carry_k)
            carry_v[...] = jnp.zeros_like(carry_v)
            dsink_ref[...] = jnp.zeros_like(dsink_ref)

        dk_acc[...] = jnp.zeros_like(dk_acc)
        dv_acc[...] = jnp.zeros_like(dv_acc)
        from_prev = _from_prev(rows)
        lane = lax.broadcasted_iota(jnp.int32, (1, LANES), 1)
        for j in range(nsub):
            k_win, v_win = _kv_windows(k_ref, kh_ref, v_ref, vh_ref, j)
            kbd, vbd = _block_diag(k_win), _block_diag(v_win)
            qs, dos = _stack_slots(q_ref, j, group, KV), _stack_slots(do_ref, j, group, KV)
            dp = _nt(dos, vbd)
            probs16, ds16 = [], []
            for g, (pg, es) in enumerate(_attn_exp(qs, kbd, sink_ref, from_prev, rev(i) == 0 if j == 0 else None)):
                inv = 1.0 / (jnp.sum(pg, axis=-1, keepdims=True) + es)
                probs = pg * inv
                dpg = _fold(dp, g, from_prev)
                delta = jnp.sum(probs * dpg, axis=-1, keepdims=True)
                probs16.append(probs.astype(BF16))
                ds16.append((probs * (dpg - delta) * scale).astype(BF16))
                dsk = -(es * inv * delta)
                for hh in range(group):
                    tot = jnp.sum(dsk[hh * WINDOW:(hh + 1) * WINDOW], axis=0, keepdims=True)
                    dsink_ref[pl.ds(hh, 1), :] += jnp.where(lane == g, tot, 0.0)
            ds = _unfold(ds16, from_prev)
            dqs = _dot(ds, kbd)
            for hh in range(group):
                dq_acc[pl.ds(j * WINDOW, WINDOW), pl.ds(KV * hh, KV)] = dqs[hh * WINDOW:(hh + 1) * WINDOW]
            keys = pl.ds(j * WINDOW, 2 * WINDOW)
            dk_acc[keys, :] += _diag_blocks_sum(_tn(ds, qs), 2 * WINDOW)
            dv_acc[keys, :] += _diag_blocks_sum(_tn(_unfold(probs16, from_prev), dos), 2 * WINDOW)

        tail = pl.ds(tq, WINDOW)
        dk_acc[tail, :] += carry_k[...]
        dv_acc[tail, :] += carry_v[...]
        carry_k[...] = dk_acc[pl.ds(0, WINDOW), :]
        carry_v[...] = dv_acc[pl.ds(0, WINDOW), :]
        cos, sin = cos_ref[...], sin_ref[...]
        dq_ref[...] = _rope_bwd(dq_acc[...], cos, sin).astype(BF16)
        dk_ref[...] = _rope_bwd(dk_acc[pl.ds(WINDOW, tq), :], cos, sin).astype(BF16)
        dv_ref[...] = dv_acc[pl.ds(WINDOW, tq), :].astype(BF16)

    def rrows(n):
        return pl.BlockSpec((tq, n), lambda i: (rev(i), 0))

    def rhalo(n):
        return pl.BlockSpec((WINDOW, n), lambda i: (jnp.maximum(rev(i) * nsub - 1, 0), 0))

    return pl.pallas_call(
        body, name=name, grid=(nt,),
        out_shape=(jax.ShapeDtypeStruct((T, D), BF16), jax.ShapeDtypeStruct((T, KV), BF16),
                   jax.ShapeDtypeStruct((T, KV), BF16), jax.ShapeDtypeStruct((SUBLANES, LANES), F32)),
        in_specs=[rrows(D), rrows(KV), rhalo(KV), rrows(KV), rhalo(KV), rrows(D), _resident(sink_col.shape),
                  rrows(LANES), rrows(LANES)],
        out_specs=(rrows(D), rrows(KV), rrows(KV), _acc_spec((SUBLANES, LANES))),
        scratch_shapes=[pltpu.VMEM((tq, D), F32), pltpu.VMEM((WINDOW + tq, KV), F32), pltpu.VMEM((WINDOW + tq, KV), F32),
                        pltpu.VMEM((WINDOW, KV), F32), pltpu.VMEM((WINDOW, KV), F32)],
        compiler_params=_params("arbitrary"),
    )(q, k, k, v, v, do, sink_col, cos, sin_signed)


def _ln_stats(u):
    mu = jnp.mean(u, axis=-1, keepdims=True)
    d = u - mu
    rstd = lax.rsqrt(jnp.mean(d * d, axis=-1, keepdims=True) + LN_EPS)
    return d * rstd, rstd


def _lag_taps(b, K):
    return [(a, K - 1 - (SUBLANES * a + b)) for a in range(-(-K // SUBLANES)) if SUBLANES * a + b <= K - 1]


def _conv_chunks(tm, D, chunk):
    def rows(c, carry):
        r0 = pl.multiple_of(c * CONV_ROWS, CONV_ROWS)
        for l0 in range(0, D, CONV_LANES):
            chunk(r0, pl.ds(l0, CONV_LANES))
        return carry
    lax.fori_loop(0, tm // CONV_ROWS, rows, 0)


def _conv_causal(buf, w_ref, bias_ref, out_ref, tm, D, K):
    def chunk(r0, lanes):
        acc = jnp.broadcast_to(bias_ref[:, lanes], (CONV_ROWS, CONV_LANES))
        for b in range(SUBLANES):
            y = None
            for a, k in _lag_taps(b, K):
                start = pl.multiple_of(r0 + CONV_HALO - SUBLANES * (a + 1), SUBLANES)
                t = buf[pl.ds(start, CONV_ROWS + SUBLANES), lanes] * w_ref[pl.ds(k, 1), lanes]
                y = t if y is None else y + t
            acc = acc + y[SUBLANES - b:SUBLANES - b + CONV_ROWS]
        out_ref[pl.ds(r0, CONV_ROWS), lanes] = acc
    _conv_chunks(tm, D, chunk)


def _conv_anticausal(dbuf, w_ref, out_ref, tm, D, K):
    def chunk(r0, lanes):
        acc = jnp.zeros((CONV_ROWS, CONV_LANES), F32)
        for b in range(SUBLANES):
            y = None
            for a, k in _lag_taps(b, K):
                start = pl.multiple_of(r0 + SUBLANES * a, SUBLANES)
                t = dbuf[pl.ds(start, CONV_ROWS + SUBLANES), lanes] * w_ref[pl.ds(k, 1), lanes]
                y = t if y is None else y + t
            acc = acc + y[b:b + CONV_ROWS]
        out_ref[pl.ds(r0, CONV_ROWS), lanes] = acc
    _conv_chunks(tm, D, chunk)


def _conv_tap_grads(dbuf, ubuf, acc_ref, tm, D, K):
    reach = SUBLANES * (-(-K // SUBLANES) - 1)

    def chunk(r0, lanes):
        d = dbuf[pl.ds(r0, CONV_ROWS), lanes]
        around = ubuf[pl.ds(pl.multiple_of(r0 + CONV_HALO - reach - SUBLANES, SUBLANES), CONV_ROWS + reach + SUBLANES), lanes]
        for b in range(SUBLANES):
            shifted = around[SUBLANES - b:SUBLANES - b + CONV_ROWS + reach]
            for a, k in _lag_taps(b, K):
                prod = d * shifted[reach - SUBLANES * a:reach - SUBLANES * a + CONV_ROWS]
                part = prod[0:SUBLANES]
                for i in range(1, CONV_ROWS // SUBLANES):
                    part = part + prod[SUBLANES * i:SUBLANES * (i + 1)]
                acc_ref[k, :, lanes] += part
    _conv_chunks(tm, D, chunk)


def mix_out_fwd(x, u0, o, sgc, sgt, dw_w, dw_b, ln_g, ln_b, wcp, wo, wout, name):
    T, D = x.shape
    tm = min(ROW_TILE, T)
    K = dw_w.shape[0]

    def body(x_ref, u_ref, uh_ref, o_ref, sgc_ref, sgt_ref, w_ref, b_ref, lg_ref, lb_ref, wcp_ref, wo_ref, wout_ref,
             x2_ref, u1_ref, co_ref, ao_ref, mg_ref, buf, conv):
        keep = (pl.program_id(0) > 0).astype(F32)
        buf[pl.ds(0, CONV_HALO), :] = uh_ref[...].astype(F32) * keep
        buf[pl.ds(CONV_HALO, tm), :] = u_ref[...].astype(F32)
        _conv_causal(buf, w_ref, b_ref, conv, tm, D, K)
        acc = conv[...]
        u1_ref[...] = acc.astype(BF16)
        xhat, _ = _ln_stats(acc)
        u2 = xhat * lg_ref[...] + lb_ref[...]
        u3 = (u2 * _sigmoid(u2)).astype(BF16)
        co = _dot(u3, wcp_ref[...])
        ao = _dot(o_ref[...], wo_ref[...])
        co_ref[...] = co.astype(BF16)
        ao_ref[...] = ao.astype(BF16)
        merged = (sgc_ref[...].astype(F32) * co + sgt_ref[...].astype(F32) * ao).astype(BF16)
        mg_ref[...] = merged
        x2_ref[...] = x_ref[...] + _dot(merged, wout_ref[...])

    big = jax.ShapeDtypeStruct((T, D), BF16)
    vec = _resident((1, D))
    return pl.pallas_call(
        body, name=name, grid=(T // tm,),
        out_shape=(jax.ShapeDtypeStruct((T, D), F32), big, big, big, big),
        in_specs=[_rows(tm, D), _rows(tm, D), _halo_rows_spec(tm, D, CONV_HALO), _rows(tm, D), _rows(tm, D), _rows(tm, D),
                  _resident((K, D)), vec, vec, vec, _resident((D, D)), _resident((D, D)), _resident((D, D))],
        out_specs=(_rows(tm, D),) * 5,
        scratch_shapes=[pltpu.VMEM((CONV_HALO + tm, D), F32), pltpu.VMEM((tm, D), F32)],
        compiler_params=_params("arbitrary"),
    )(x, u0, u0, o, sgc, sgt, dw_w, dw_b, ln_g, ln_b, wcp, wo, wout)


def mix_out_bwd(dx2, u1, co, ao, sgc, sgt, ln_g, ln_b, wcp, wo, wout, name, carry=None):
    T, D = dx2.shape
    tm = min(ROW_TILE, T)

    def body(dx_ref, u1_ref, co_ref, ao_ref, sgc_ref, sgt_ref, lg_ref, lb_ref, wcp_ref, wo_ref, wout_ref,
             dgc_ref, dgt_ref, do_ref, du1_ref, dco_ref, dao_ref, u3_ref, sums_ref):
        dm = _nt(dx_ref[...].astype(BF16), wout_ref[...])
        sgc, sgt = sgc_ref[...].astype(F32), sgt_ref[...].astype(F32)
        dco = (dm * sgc).astype(BF16)
        dao = (dm * sgt).astype(BF16)
        dgc = dm * co_ref[...].astype(F32) * sgc * (1.0 - sgc)
        dgt = dm * ao_ref[...].astype(F32) * sgt * (1.0 - sgt)
        dco_ref[...] = dco
        dao_ref[...] = dao
        dgc_ref[...] = dgc.astype(BF16)
        dgt_ref[...] = dgt.astype(BF16)
        do_ref[...] = _nt(dao, wo_ref[...]).astype(BF16)
        du3 = _nt(dco, wcp_ref[...])
        xhat, rstd = _ln_stats(u1_ref[...].astype(F32))
        g = lg_ref[...]
        u2 = xhat * g + lb_ref[...]
        su = _sigmoid(u2)
        u3_ref[...] = (u2 * su).astype(BF16)
        du2 = du3 * (su * (1.0 + u2 * (1.0 - su)))
        dxh = du2 * g
        du1 = rstd * (dxh - jnp.mean(dxh, axis=-1, keepdims=True) - xhat * jnp.mean(dxh * xhat, axis=-1, keepdims=True))
        du1_ref[...] = du1.astype(BF16)

        @pl.when(pl.program_id(0) == 0)
        def _():
            sums_ref[...] = jnp.zeros_like(sums_ref)
        for r, val in enumerate((dgc, dgt, du2 * xhat, du2, du1)):
            sums_ref[pl.ds(r, 1), :] += jnp.sum(val, axis=0, keepdims=True)

    big = jax.ShapeDtypeStruct((T, D), BF16)
    vec = _resident((1, D))
    return _call(
        body, name=name, grid=(T // tm,),
        out_shape=(big,) * 7 + (jax.ShapeDtypeStruct((8, D), F32),),
        in_specs=[_rows(tm, D)] * 6 + [vec, vec, _resident((D, D)), _resident((D, D)), _resident((D, D))],
        out_specs=(_rows(tm, D),) * 7 + (_acc_spec((8, D)),),
        args=(dx2, u1, co, ao, sgc, sgt, ln_g, ln_b, wcp, wo, wout), carry=carry)


def conv_bwd(du1, u0, ga, gb, dw_w, name, carry=None):
    T, D = du1.shape
    tm = min(ROW_TILE, T)
    nt = T // tm
    K = dw_w.shape[0]
    per = tm // CONV_HALO

    def body(d_ref, dn_ref, u_ref, uh_ref, ga_ref, gb_ref, w_ref, dga_ref, dgb_ref, dw_ref, dbuf, ubuf, du0_buf, taps):
        i = pl.program_id(0)
        dbuf[pl.ds(0, tm), :] = d_ref[...].astype(F32)
        dbuf[pl.ds(tm, CONV_HALO), :] = dn_ref[...].astype(F32) * (i < nt - 1).astype(F32)
        ubuf[pl.ds(0, CONV_HALO), :] = uh_ref[...].astype(F32) * (i > 0).astype(F32)
        ubuf[pl.ds(CONV_HALO, tm), :] = u_ref[...].astype(F32)

        @pl.when(i == 0)
        def _():
            taps[...] = jnp.zeros_like(taps)

        _conv_anticausal(dbuf, w_ref, du0_buf, tm, D, K)
        _conv_tap_grads(dbuf, ubuf, taps, tm, D, K)
        du0 = du0_buf[...]
        ga, gb = ga_ref[...].astype(F32), gb_ref[...].astype(F32)
        sg = _sigmoid(gb)
        dga_ref[...] = (du0 * sg).astype(BF16)
        dgb_ref[...] = (du0 * ga * sg * (1.0 - sg)).astype(BF16)

        @pl.when(i == nt - 1)
        def _():
            for k in range(K):
                dw_ref[pl.ds(k, 1), :] = jnp.sum(taps[k], axis=0, keepdims=True)

    nxt = pl.BlockSpec((CONV_HALO, D), lambda i: (jnp.minimum((i + 1) * per, nt * per - 1), 0))
    big = jax.ShapeDtypeStruct((T, D), BF16)
    return _call(
        body, name=name, grid=(nt,),
        out_shape=(big, big, jax.ShapeDtypeStruct((K, D), F32)),
        in_specs=[_rows(tm, D), nxt, _rows(tm, D), _halo_rows_spec(tm, D, CONV_HALO), _rows(tm, D), _rows(tm, D),
                  _resident((K, D))],
        out_specs=(_rows(tm, D), _rows(tm, D), _acc_spec((K, D))),
        scratch_shapes=[pltpu.VMEM((tm + CONV_HALO, D), F32), pltpu.VMEM((CONV_HALO + tm, D), F32), pltpu.VMEM((tm, D), F32),
                        pltpu.VMEM((K, SUBLANES, D), F32)],
        args=(du1, du1, u0, u0, ga, gb, dw_w), carry=carry)


def mix_in_bwd(dx2, x, gain, wint, wq, pieces, name):
    T, D = x.shape
    tm = min(ROW_TILE, T)
    widths = [p.shape[1] for p in pieces]
    offs = _w_in_rows(D)

    def body(dx2_ref, x_ref, g_ref, w_ref, wq_ref, *rest):
        piece_refs, (dx_ref, dg_ref) = rest[:len(pieces)], rest[len(pieces):]
        dh = None
        for n, (p_ref, off, w) in enumerate(zip(piece_refs, offs, widths)):
            t = _dot(p_ref[...], wq_ref[...] if n == 2 else w_ref[pl.ds(off, w), :])
            dh = t if dh is None else dh + t
        dx, dgt = _rms_bwd(x_ref[...], g_ref[...], dh)
        dx_ref[...] = dx2_ref[...] + dx

        @pl.when(pl.program_id(0) == 0)
        def _():
            dg_ref[...] = jnp.zeros_like(dg_ref)
        dg_ref[...] += jnp.sum(dgt, axis=0, keepdims=True)

    return pl.pallas_call(
        body, name=name, grid=(T // tm,),
        out_shape=(jax.ShapeDtypeStruct((T, D), F32), jax.ShapeDtypeStruct((1, D), F32)),
        in_specs=[_rows(tm, D), _rows(tm, D), _resident((1, D)), _resident(wint.shape), _resident(wq.shape)]
        + [_rows(tm, w) for w in widths],
        out_specs=(_rows(tm, D), _acc_spec((1, D))),
        compiler_params=_params("arbitrary"),
    )(dx2, x, gain, wint, wq, *pieces)


def rope_tables(positions):
    half = HEAD_DIM // 2
    inv_freq = ROPE_THETA ** (-jnp.arange(half, dtype=F32) / half)
    ang = positions.astype(F32)[:, None] * inv_freq
    cos, sin = jnp.cos(ang), jnp.sin(ang)
    reps = LANES // HEAD_DIM
    return jnp.tile(jnp.concatenate([cos, cos], axis=-1), (1, reps)), jnp.tile(jnp.concatenate([-sin, sin], axis=-1), (1, reps))


def _place():
    return lax.axis_index("x"), lax.axis_index("y"), lax.axis_index("c")


def all_gather(blocks, name):
    n = len(blocks)
    send = gather_send(blocks)
    forward = gather_forward(send.out_shapes)
    n_sems = len(send.sems)

    def body(*refs):
        ins, outs, sems = refs[:n], refs[n:2 * n], refs[2 * n:]
        send.start(ins, outs, sems[:n_sems])
        send.finish(ins, outs, sems[:n_sems])
        forward.start((), outs, sems[n_sems:])
        forward.finish((), outs, sems[n_sems:])

    return pl.pallas_call(
        body, name=name, out_shape=tuple(send.out_shapes), in_specs=[_ANY] * n, out_specs=(_ANY,) * n,
        scratch_shapes=list(send.sems) + list(forward.sems),
    )(*blocks)


def _chips_across(x, y):
    return [(1 - x, y), (x, 1 - y), (1 - x, 1 - y)]


def _dev_index(x, y, c):
    return 4 * x + 2 * y + c


def gather_send(blocks):
    n = len(blocks)

    def copies(in_refs, out_refs, sems):
        send, recv, local = sems
        x, y, c = _place()
        targets = [(x, y, 1 - c)] + [(*chip, c) for chip in _chips_across(x, y)]
        outgoing, incoming, mine = [], [], []
        for i, (x_ref, out_ref) in enumerate(zip(in_refs, out_refs)):
            for k, t in enumerate(targets):
                pair = dict(send_sem=send.at[4 * i + k], recv_sem=recv.at[4 * i + k], device_id=t, device_id_type=MESH)
                outgoing.append(pltpu.make_async_remote_copy(src_ref=x_ref, dst_ref=out_ref.at[_dev_index(x, y, c)], **pair))
                incoming.append(pltpu.make_async_remote_copy(src_ref=x_ref, dst_ref=out_ref.at[_dev_index(*t)], **pair))
            mine.append(pltpu.make_async_copy(x_ref, out_ref.at[_dev_index(x, y, c)], local.at[i]))
        return outgoing, incoming, mine

    def start(*refs):
        outgoing, _, mine = copies(*refs)
        for cp in mine + outgoing:
            cp.start()

    def finish(*refs):
        outgoing, incoming, mine = copies(*refs)
        for cp in incoming:
            cp.wait_recv()
        for cp in outgoing:
            cp.wait_send()
        for cp in mine:
            cp.wait()

    return Carry(ins=tuple(blocks), out_shapes=tuple(jax.ShapeDtypeStruct((N_DEV,) + b.shape, b.dtype) for b in blocks),
                 aliases={}, sems=(pltpu.SemaphoreType.DMA((4 * n,)), pltpu.SemaphoreType.DMA((4 * n,)),
                                   pltpu.SemaphoreType.DMA((n,))), start=start, finish=finish)


def gather_forward(gathered):
    n = len(gathered)

    def copies(in_refs, out_refs, sems):
        send, recv = sems
        x, y, c = _place()
        outgoing, incoming = [], []
        for i, buf in enumerate(out_refs):
            for k, chip in enumerate(_chips_across(x, y)):
                pair = dict(send_sem=send.at[3 * i + k], recv_sem=recv.at[3 * i + k], device_id=(x, y, 1 - c),
                            device_id_type=MESH)
                rows = buf.at[_dev_index(*chip, c)]
                outgoing.append(pltpu.make_async_remote_copy(src_ref=rows, dst_ref=rows, **pair))
                theirs = buf.at[_dev_index(*chip, 1 - c)]
                incoming.append(pltpu.make_async_remote_copy(src_ref=theirs, dst_ref=theirs, **pair))
        return outgoing, incoming

    def start(*refs):
        for cp in copies(*refs)[0]:
            cp.start()

    def finish(*refs):
        outgoing, incoming = copies(*refs)
        for cp in incoming:
            cp.wait_recv()
        for cp in outgoing:
            cp.wait_send()

    return Carry(ins=tuple(gathered), out_shapes=tuple(jax.ShapeDtypeStruct(g.shape, g.dtype) for g in gathered),
                 aliases={i: i for i in range(n)},
                 sems=(pltpu.SemaphoreType.DMA((3 * n,)), pltpu.SemaphoreType.DMA((3 * n,))), start=start, finish=finish)


def gather_whole(blocks):
    send = gather_send(blocks)
    forward = gather_forward(send.out_shapes)
    n = len(send.sems)

    def relay(ins, outs, sems):
        send.finish(ins, outs, sems[:n])
        forward.start((), outs, sems[n:])

    return Carry(ins=send.ins, out_shapes=send.out_shapes, aliases={}, sems=send.sems + forward.sems,
                 start=lambda ins, outs, sems: send.start(ins, outs, sems[:n]), relay=relay,
                 finish=lambda ins, outs, sems: forward.finish((), outs, sems[n:]))


def swap_halves(by_core):
    n = len(by_core)

    def copies(in_refs, out_refs, sems):
        send, recv = sems
        x, y, c = _place()
        return [pltpu.make_async_remote_copy(src_ref=a.at[:, 1 - c], dst_ref=r, send_sem=send.at[i], recv_sem=recv.at[i],
                                             device_id=(x, y, 1 - c), device_id_type=MESH)
                for i, (a, r) in enumerate(zip(in_refs, out_refs))]

    def start(*refs):
        for cp in copies(*refs):
            cp.start()

    def finish(*refs):
        for cp in copies(*refs):
            cp.wait()

    shapes = tuple(jax.ShapeDtypeStruct((a.shape[0],) + a.shape[2:], a.dtype) for a in by_core)
    return Carry(ins=tuple(by_core), out_shapes=shapes, aliases={},
                 sems=(pltpu.SemaphoreType.DMA((n,)), pltpu.SemaphoreType.DMA((n,))), start=start, finish=finish)


def exchange_between_chips(by_chip):
    n = len(by_chip)

    def copies(in_refs, out_refs, sems):
        send, recv = sems
        x, y, c = _place()
        out = []
        for i, (s, r) in enumerate(zip(in_refs, out_refs)):
            for k, (tx, ty) in enumerate(_chips_across(x, y)):
                out.append(pltpu.make_async_remote_copy(
                    src_ref=s.at[2 * tx + ty], dst_ref=r.at[k], send_sem=send.at[3 * i + k], recv_sem=recv.at[3 * i + k],
                    device_id=(tx, ty, c), device_id_type=MESH))
        return out

    def start(*refs):
        for cp in copies(*refs):
            cp.start()

    def finish(*refs):
        for cp in copies(*refs):
            cp.wait()

    shapes = tuple(jax.ShapeDtypeStruct((3,) + a.shape[1:], a.dtype) for a in by_chip)
    return Carry(ins=tuple(by_chip), out_shapes=shapes, aliases={},
                 sems=(pltpu.SemaphoreType.DMA((3 * n,)), pltpu.SemaphoreType.DMA((3 * n,))), start=start, finish=finish)


def run_exchange(carry, name):
    n_in = len(carry.ins)
    n_out = len(carry.out_shapes)

    def body(*refs):
        parts = refs[:n_in], refs[n_in:n_in + n_out], refs[n_in + n_out:]
        carry.start(*parts)
        carry.finish(*parts)

    return pl.pallas_call(
        body, name=name, out_shape=tuple(carry.out_shapes), in_specs=[_ANY] * n_in, out_specs=(_ANY,) * n_out,
        scratch_shapes=list(carry.sems), input_output_aliases=dict(carry.aliases),
    )(*carry.ins)


def pair_sum(my_core, by_core, received, name):
    n = len(by_core)

    def body(core_ref, *refs):
        for a_ref, b_ref, o_ref in zip(refs[:n], refs[n:2 * n], refs[2 * n:]):
            o_ref[0] = (a_ref[0, 0].astype(F32) + b_ref[0].astype(F32)).astype(BF16)

    mine = [pl.BlockSpec((1, 1) + a.shape[2:], lambda j, core: (j, core[0], 0, 0)) for a in by_core]
    theirs = [pl.BlockSpec((1,) + r.shape[1:], lambda j, core: (j, 0, 0)) for r in received]
    return pl.pallas_call(
        body, name=name, out_shape=tuple(jax.ShapeDtypeStruct(r.shape, BF16) for r in received),
        grid_spec=pltpu.PrefetchScalarGridSpec(num_scalar_prefetch=1, grid=(by_core[0].shape[0],), in_specs=mine + theirs,
                                               out_specs=tuple(theirs)),
        compiler_params=_params("arbitrary"),
    )(my_core, *by_core, *received)


def _adamw_math(w, g, m, v):
    m = ADAM_B1 * m + (1.0 - ADAM_B1) * g
    v = ADAM_B2 * v + (1.0 - ADAM_B2) * (g * g)
    m_hat = m / (1.0 - ADAM_B1 ** ADAM_STEP)
    v_hat = v / (1.0 - ADAM_B2 ** ADAM_STEP)
    delta = -ADAM_LR * (m_hat / (jnp.sqrt(v_hat) + ADAM_EPS) + ADAM_WD * w)
    return delta, m, v


def adamw(my_chip, ws, gs, ms, vs, name):
    n = len(ws)
    flat, widths = [], []
    for g in gs:
        parts = list(g) if isinstance(g, (tuple, list)) else [g]
        flat += parts
        widths.append(len(parts))

    def body(chip_ref, *refs):
        w_refs, refs = refs[:n], refs[n:]
        g_refs, refs = refs[:len(flat)], refs[len(flat):]
        m_refs, v_refs, outs = refs[:n], refs[n:2 * n], refs[2 * n:]
        at = 0
        for t in range(n):
            if widths[t] == 1:
                g = g_refs[at][...]
            else:
                g = g_refs[at][0].astype(F32)
                for k in range(3):
                    g = g + g_refs[at + 1][k].astype(F32)
            at += widths[t]
            outs[4 * t][...] = g
            outs[4 * t + 1][...], outs[4 * t + 2][...], outs[4 * t + 3][...] = _adamw_math(
                w_refs[t][...], g, m_refs[t][...], v_refs[t][...])

    def whole(a):
        zeros = (0,) * a.ndim
        return pl.BlockSpec(a.shape, lambda i, chip: zeros, pipeline_mode=pl.Buffered(1))

    g_specs = []
    for g in gs:
        if isinstance(g, (tuple, list)):
            g_specs += [pl.BlockSpec((1,) + g[0].shape[1:], lambda i, chip: (chip[0], 0, 0), pipeline_mode=pl.Buffered(1)),
                        whole(g[1])]
        else:
            g_specs.append(whole(g))
    shapes, out_specs = [], []
    for w in ws:
        shapes += [jax.ShapeDtypeStruct(w.shape, F32)] * 4
        out_specs += [whole(w)] * 4
    res = pl.pallas_call(
        body, name=name, out_shape=tuple(shapes),
        grid_spec=pltpu.PrefetchScalarGridSpec(
            num_scalar_prefetch=1, grid=(1,), in_specs=[whole(w) for w in ws] + g_specs + [whole(a) for a in ms + vs],
            out_specs=tuple(out_specs)),
        compiler_params=_params("arbitrary"),
    )(my_chip, *ws, *flat, *ms, *vs)
    return [tuple(res[4 * t:4 * t + 4]) for t in range(n)]


def adamw_replicated(w, partials, m, v, name):
    def body(w_ref, p_ref, m_ref, v_ref, g_ref, d_ref, mo_ref, vo_ref):
        g = p_ref[0]
        for k in range(1, N_DEV):
            g = g + p_ref[k]
        g_ref[...] = g
        d_ref[...], mo_ref[...], vo_ref[...] = _adamw_math(w_ref[...], g, m_ref[...], v_ref[...])

    shape = jax.ShapeDtypeStruct(w.shape, F32)
    return pl.pallas_call(body, name=name, out_shape=(shape,) * 4, compiler_params=_params())(w, partials, m, v)


PACK_COLS = 1024
PACK_ROW_ALIGN = 16

REPLICATED = ("ffn1_norm", "mix_norm", "conv_dw_b", "conv_ln_g", "conv_ln_b", "ffn2_norm", "final_norm", "gate_b", "attn_sinks")
WEIGHT_ORDER = ("ffn1_norm", "ffn1_w_gate", "ffn1_w_up", "ffn1_w_down", "mix_norm", "w_in", "conv_dw_w", "conv_dw_b", "conv_ln_g",
                "conv_ln_b", "conv_w_proj", "attn_sinks", "attn_w_o", "gate_b", "w_out", "ffn2_norm", "ffn2_w_gate", "ffn2_w_up",
                "ffn2_w_down", "final_norm")


def _to_rows(flat, lead):
    n = flat.shape[-1]
    rows = -(-n // PACK_COLS)
    flat = jnp.pad(flat, [(0, 0)] * lead + [(0, rows * PACK_COLS - n)])
    return flat.reshape(flat.shape[:lead] + (rows, PACK_COLS))


def _pad_rows(a, axis):
    rows = a.shape[axis]
    pad = -rows % PACK_ROW_ALIGN
    widths = [(0, 0)] * a.ndim
    widths[axis] = (0, pad)
    return jnp.pad(a, widths)


def _from_rows(rows, shape):
    n = 1
    for s in shape:
        n *= s
    return rows.reshape(rows.shape[:-2] + (-1,))[..., :n].reshape(rows.shape[:-2] + tuple(shape))


def _heads_slot_major(rows):
    group = rows.shape[0] // (N_KV_HEADS * HEAD_DIM)
    return rows.reshape(N_KV_HEADS, group, HEAD_DIM, rows.shape[1]).transpose(1, 0, 2, 3).reshape(rows.shape)


def _heads_kv_major(rows):
    group = rows.shape[0] // (N_KV_HEADS * HEAD_DIM)
    return rows.reshape(group, N_KV_HEADS, HEAD_DIM, rows.shape[1]).transpose(1, 0, 2, 3).reshape(rows.shape)


def _by_core(full_rows):
    return full_rows.reshape((N_DEV // 2, 2, full_rows.shape[0] // N_DEV, full_rows.shape[1]))


def kernel(x, positions, ffn1_norm, ffn1_w_gate, ffn1_w_up, ffn1_w_down, mix_norm, w_in, conv_dw_w, conv_dw_b, conv_ln_g, conv_ln_b, conv_w_proj, attn_sinks, attn_w_o, gate_b, w_out, ffn2_norm, ffn2_w_gate, ffn2_w_up, ffn2_w_down, final_norm, loss_target, m_ffn1_norm, m_ffn1_w_gate, m_ffn1_w_up, m_ffn1_w_down, m_mix_norm, m_w_in, m_conv_dw_w, m_conv_dw_b, m_conv_ln_g, m_conv_ln_b, m_conv_w_proj, m_attn_sinks, m_attn_w_o, m_gate_b, m_w_out, m_ffn2_norm, m_ffn2_w_gate, m_ffn2_w_up, m_ffn2_w_down, m_final_norm, v_ffn1_norm, v_ffn1_w_gate, v_ffn1_w_up, v_ffn1_w_down, v_mix_norm, v_w_in, v_conv_dw_w, v_conv_dw_b, v_conv_ln_g, v_conv_ln_b, v_conv_w_proj, v_attn_sinks, v_attn_w_o, v_gate_b, v_w_out, v_ffn2_norm, v_ffn2_w_gate, v_ffn2_w_up, v_ffn2_w_down, v_final_norm):
    given = dict(locals())
    shapes = {n: given[n].shape for n in WEIGHT_ORDER}
    w = {n: given[n].reshape(given[n].shape[-2:]) if given[n].ndim == 3 else given[n].reshape(1, -1) for n in WEIGHT_ORDER}
    m = {n: given["m_" + n].reshape(w[n].shape) for n in WEIGHT_ORDER}
    v = {n: given["v_" + n].reshape(w[n].shape) for n in WEIGHT_ORDER}
    my_x, my_y, my_c = _place()
    my_core = my_c.astype(jnp.int32).reshape(1)
    my_chip = (2 * my_x + my_y).astype(jnp.int32).reshape(1)
    xs, target = x[0], loss_target[0]
    T, D = xs.shape
    KV = N_KV_HEADS * HEAD_DIM
    K = w["conv_dw_w"].shape[0]

    def t16(n):
        return w[n].T.astype(BF16)

    def r16(n):
        return w[n].astype(BF16)

    blocks1 = [t16("ffn1_w_gate"), t16("ffn1_w_up"), r16("ffn1_w_down")]
    dw_bits = _pad_rows(_to_rows(lax.bitcast_convert_type(w["conv_dw_w"], BF16).reshape(-1), 0), 0)
    blocks2 = [t16("w_in"), r16("conv_w_proj"), r16("attn_w_o"), r16("w_out"), dw_bits]
    blocks3 = [t16("ffn2_w_gate"), t16("ffn2_w_up"), r16("ffn2_w_down")]
    cos, sin = rope_tables(positions[0])
    sink_col = jnp.repeat(w["attn_sinks"].reshape(-1), WINDOW).reshape(N_KV_HEADS, (D // KV) * WINDOW, 1)

    def full(gathered):
        return gathered.reshape(-1, gathered.shape[2])

    wgt1, wut1, wd1 = (full(g) for g in all_gather(blocks1, "gather_ffn1"))
    (x1, h1, a1, b1, s1), gath2 = ffn_fwd(xs, w["ffn1_norm"], wgt1, wut1, wd1, "ffn1_fwd", carry=gather_whole(blocks2))
    wint, wcp, wo, wout = (full(g) for g in gath2[:4])
    wq = _heads_slot_major(wint[2 * D:3 * D])
    wo = _heads_slot_major(wo)
    dw_full = lax.bitcast_convert_type(_from_rows(gath2[4], w["conv_dw_w"].shape + (2,)), F32)
    dw_full = dw_full.transpose(1, 0, 2).reshape(K, D)
    (h2, ga, gb, u0, q, sgc, sgt, kk, vv), gath3 = mix_in_fwd(x1, w["mix_norm"], wint, wq, w["gate_b"], cos, sin,
                                                              "mix_in_fwd", carry=gather_send(blocks3))
    o, gath3 = attn_fwd(q, kk, vv, sink_col, "attn_fwd", carry=gather_forward(gath3))
    x2, u1, co, ao, merged = mix_out_fwd(x1, u0, o, sgc, sgt, dw_full, w["conv_dw_b"], w["conv_ln_g"], w["conv_ln_b"],
                                         wcp, wo, wout, "mix_out_fwd")
    wgt2, wut2, wd2 = (full(g) for g in gath3)
    loss, dx3, d_final, h3, a2, b2, s2 = ffn_fwd_loss(x2, w["ffn2_norm"], wgt2, wut2, wd2, w["final_norm"], target,
                                                      "ffn2_fwd_loss")

    small = {"final_norm": d_final}
    (da2, db2), _ = ffn_bwd_hidden(dx3, a2, b2, wd2, "ffn2_bwd_hidden")
    (dx2, small["ffn2_norm"]), _ = ffn_bwd_input(dx3, x2, w["ffn2_norm"], da2, db2, wgt2, wut2, "ffn2_bwd_input")
    core2 = [_by_core(g) for g in (wgrad(da2, h3, "ffn2_dwg")[0], wgrad(db2, h3, "ffn2_dwu")[0],
                                   wgrad(s2, dx3, "ffn2_dwd", b_scale=FFN_SCALE)[0])]
    (dgc, dgt, do, du1, dco, dao, u3, sums), recv2 = mix_out_bwd(dx2, u1, co, ao, sgc, sgt, w["conv_ln_g"], w["conv_ln_b"],
                                                                 wcp, wo, wout, "mix_out_bwd", carry=swap_halves(core2))
    chip2 = pair_sum(my_core, core2, recv2, "ffn2_grads_pair_sum")
    small["gate_b"] = jnp.concatenate([sums[0:1], sums[1:2]], axis=1)
    small["conv_ln_g"], small["conv_ln_b"], small["conv_dw_b"] = sums[2:3], sums[3:4], sums[4:5]
    g_wout = wgrad(merged, dx2, "dw_out")[0]
    g_wcp = wgrad(u3, dco, "dw_conv_proj")[0]
    g_wo = _heads_kv_major(wgrad(o, dao, "dw_attn_o")[0])
    (dga, dgb, g_dw), got2 = conv_bwd(du1, u0, ga, gb, dw_full, "conv_bwd", carry=exchange_between_chips(chip2))
    dq, dk, dv, dsink = attn_bwd(q, kk, vv, do, sink_col, cos, sin, "attn_bwd")
    small["attn_sinks"] = dsink[:D // KV, :N_KV_HEADS].T.reshape(1, -1)
    pieces = [dga, dgb, dq, dk, dv, dgc, dgt]
    dx1, small["mix_norm"] = mix_in_bwd(dx2, x1, w["mix_norm"], wint, wq, pieces, "mix_in_bwd")
    group = D // KV
    q_moves = [(2 * D + HEAD_DIM * (group * g + hh), 2 * D + HEAD_DIM * (N_KV_HEADS * hh + g), HEAD_DIM)
               for g in range(N_KV_HEADS) for hh in range(group)]
    g_wint = jnp.concatenate([wgrad_stacked(pieces[:3], h2, "dw_in_a", moves=[(0, 0, 2 * D)] + q_moves),
                              wgrad_stacked(pieces[3:], h2, "dw_in_b")], axis=0)
    corem = [_by_core(a) for a in (g_wint, g_wcp, g_wo, g_wout)]

    (da1, db1), recvm = ffn_bwd_hidden(dx1, a1, b1, wd1, "ffn1_bwd_hidden", carry=swap_halves(corem))
    chipm = pair_sum(my_core, corem, recvm, "mix_grads_pair_sum")
    g1c, gotm_a = wgrad(s1, dx1, "ffn1_dwd", carry=exchange_between_chips(chipm[:1]), b_scale=FFN_SCALE)
    g1a, gotm_b = wgrad(da1, h1, "ffn1_dwg", carry=exchange_between_chips(chipm[1:]))
    g1b = wgrad(db1, h1, "ffn1_dwu")[0]
    core1 = [_by_core(a) for a in (g1a, g1b, g1c)]
    chip1 = pair_sum(my_core, core1, run_exchange(swap_halves(core1), "ffn1_grads_swap"), "ffn1_grads_pair_sum")
    (grad_x, small["ffn1_norm"]), got1 = ffn_bwd_input(dx1, xs, w["ffn1_norm"], da1, db1, wgt1, wut1, "ffn1_bwd_input",
                                                       carry=exchange_between_chips(chip1))

    gotm = gotm_a + gotm_b
    grad_src = {"ffn1_w_gate": (chip1[0], got1[0]), "ffn1_w_up": (chip1[1], got1[1]), "ffn1_w_down": (chip1[2], got1[2]),
                "ffn2_w_gate": (chip2[0], got2[0]), "ffn2_w_up": (chip2[1], got2[1]), "ffn2_w_down": (chip2[2], got2[2]),
                "w_in": (chipm[0], gotm[0]), "conv_w_proj": (chipm[1], gotm[1]), "attn_w_o": (chipm[2], gotm[2]),
                "w_out": (chipm[3], gotm[3])}
    grads = {}

    def pack_small(d, taps, extra):
        rows = [_to_rows(d[n].reshape(-1), 0) for n in REPLICATED] + [taps, _to_rows(extra.reshape(-1), 0)]
        return _pad_rows(jnp.concatenate(rows, axis=0), 0)

    zero, no_taps = jnp.zeros((1, LANES), F32), jnp.zeros((K, D), F32)
    (shares,) = all_gather([pack_small(small, g_dw, loss)], "gather_small_grads")
    g_s, d_s, m_s, v_s = adamw_replicated(pack_small(w, no_taps, zero), shares, pack_small(m, no_taps, zero),
                                          pack_small(v, no_taps, zero), "adamw_replicated")
    delta, new_m, new_v = {}, {}, {}
    off = 0
    for n in REPLICATED:
        r = -(-w[n].shape[1] // PACK_COLS)
        grads[n], delta[n], new_m[n], new_v[n] = (_from_rows(a[off:off + r], w[n].shape) for a in (g_s, d_s, m_s, v_s))
        off += r
    shard_cols = w["conv_dw_w"].shape[1]
    grad_src["conv_dw_w"] = lax.dynamic_slice_in_dim(g_s[off:off + K], _dev_index(my_x, my_y, my_c) * shard_cols, shard_cols,
                                                    axis=1)
    total_loss = g_s[off + K, 0]

    def like(a, ref):
        return a if a.shape == ref.shape else a.T

    groups = (("ffn1_w_gate", "ffn1_w_up", "ffn1_w_down"), ("ffn2_w_gate", "ffn2_w_up", "ffn2_w_down"),
              ("w_in", "conv_dw_w", "conv_w_proj", "attn_w_o", "w_out"))
    for k, names in enumerate(groups):
        refs = [grad_src[n][0][0] if isinstance(grad_src[n], tuple) else grad_src[n] for n in names]
        res = adamw(my_chip, [like(w[n], r) for n, r in zip(names, refs)], [grad_src[n] for n in names],
                    [like(m[n], r) for n, r in zip(names, refs)], [like(v[n], r) for n, r in zip(names, refs)], "adamw_%d" % k)
        for n, outs in zip(names, res):
            grads[n], delta[n], new_m[n], new_v[n] = (like(a, w[n]) for a in outs)

    out = [total_loss, grad_x[None]]
    for d in (grads, delta, new_m, new_v):
        out += [d[n].reshape(shapes[n]) for n in WEIGHT_ORDER]
    return tuple(out)
```

```python
import functools
from typing import Callable, NamedTuple

import jax
import jax.numpy as jnp
from jax import lax
from jax.experimental import pallas as pl
from jax.experimental.pallas import tpu as pltpu

F32, BF16 = jnp.float32, jnp.bfloat16

HEAD_DIM = 64
N_KV_HEADS = 4
WINDOW = 128
CONV_WIDTH = 31
ROPE_THETA = 10000.0
EPS = 1e-6
LN_EPS = 1e-5
NEG_INF = -1e30
ADAM_LR, ADAM_B1, ADAM_B2, ADAM_EPS, ADAM_WD, ADAM_STEP = 0.001, 0.9, 0.999, 1e-08, 0.01, 10

N_DEV = 8
LANES = 128
SUBLANES = 8
CONV_HALO = 32
CONV_ROWS, CONV_LANES = 64, 256
ROW_TILE = 512
FFN_CHUNK = 256
FFN_SCALE = 0.5
WGRAD_TILE_ELEMS = 2 ** 22
WGRAD_TILE_ROWS = 2048
WGRAD_VMEM = 40 * 2 ** 20
VMEM_LIMIT = 56 * 2 ** 20
MESH = pl.DeviceIdType.MESH


def _params(*sem):
    return pltpu.CompilerParams(dimension_semantics=sem or None, vmem_limit_bytes=VMEM_LIMIT)


def _resident(shape):
    zeros = (0,) * len(shape)
    return pl.BlockSpec(shape, lambda *_: zeros, pipeline_mode=pl.Buffered(1))


def _rows(tm, n):
    return pl.BlockSpec((tm, n), lambda i: (i, 0))


def _acc_spec(shape):
    zeros = (0,) * len(shape)
    return pl.BlockSpec(shape, lambda *_: zeros)


_ANY = pl.BlockSpec(memory_space=pl.ANY)


class Carry(NamedTuple):
    ins: tuple
    out_shapes: tuple
    aliases: dict
    sems: tuple
    start: Callable
    finish: Callable
    relay: Callable = None


def _call(body, *, name, grid, in_specs, out_specs, out_shape, args, scratch_shapes=(), carry=None):
    n_in, n_out, n_scr = len(in_specs), len(out_specs), len(scratch_shapes)
    params = _params(*(("arbitrary",) * len(grid)))
    if carry is None:
        res = pl.pallas_call(body, name=name, grid=grid, in_specs=list(in_specs), out_specs=tuple(out_specs),
                             out_shape=tuple(out_shape), scratch_shapes=list(scratch_shapes), compiler_params=params)(*args)
        return tuple(res), ()
    c_in, c_out = len(carry.ins), len(carry.out_shapes)

    def wrapped(*refs):
        ins, c_ins = refs[:n_in], refs[n_in:n_in + c_in]
        p = n_in + c_in
        outs, c_outs = refs[p:p + n_out], refs[p + n_out:p + n_out + c_out]
        p += n_out + c_out
        scr, c_sems = refs[p:p + n_scr], refs[p + n_scr:]
        ids = [pl.program_id(d) for d in range(len(grid))]
        first = functools.reduce(jnp.logical_and, [i == 0 for i in ids])
        last = functools.reduce(jnp.logical_and, [i == n - 1 for i, n in zip(ids, grid)])

        @pl.when(first)
        def _():
            carry.start(c_ins, c_outs, c_sems)

        body(*ins, *outs, *scr)

        if carry.relay is not None:
            @pl.when(ids[0] == (3 * grid[0]) // 4)
            def _():
                carry.relay(c_ins, c_outs, c_sems)

        @pl.when(last)
        def _():
            carry.finish(c_ins, c_outs, c_sems)

    res = pl.pallas_call(
        wrapped, name=name, grid=grid, in_specs=list(in_specs) + [_ANY] * c_in, out_specs=tuple(out_specs) + (_ANY,) * c_out,
        out_shape=tuple(out_shape) + tuple(carry.out_shapes), scratch_shapes=list(scratch_shapes) + list(carry.sems),
        input_output_aliases={n_in + i: n_out + o for i, o in carry.aliases.items()}, compiler_params=params,
    )(*args, *carry.ins)
    return tuple(res[:n_out]), tuple(res[n_out:])


def _nt(a, b):
    return lax.dot_general(a, b, (((1,), (1,)), ((), ())), preferred_element_type=F32)


def _tn(a, b):
    return lax.dot_general(a, b, (((0,), (0,)), ((), ())), preferred_element_type=F32)


def _dot(a, b):
    return jnp.dot(a, b, preferred_element_type=F32)


def _sigmoid(x):
    return 1.0 / (1.0 + jnp.exp(-x))


def _rms_fwd(x, g):
    r = lax.rsqrt(jnp.mean(x * x, axis=-1, keepdims=True) + EPS)
    return (x * r) * g


def _rms_bwd(x, g, dy):
    r = lax.rsqrt(jnp.mean(x * x, axis=-1, keepdims=True) + EPS)
    xhat = x * r
    dyg = dy * g
    dx = r * (dyg - xhat * jnp.mean(dyg * xhat, axis=-1, keepdims=True))
    return dx, dy * xhat


def _rot_half(x):
    lane = lax.broadcasted_iota(jnp.int32, (x.shape[0], LANES), 1)
    first = (lane % HEAD_DIM) < (HEAD_DIM // 2)
    out = []
    for s in range(x.shape[1] // LANES):
        xs = x[:, LANES * s:LANES * (s + 1)]
        out.append(jnp.where(first, pltpu.roll(xs, LANES - HEAD_DIM // 2, 1), pltpu.roll(xs, HEAD_DIM // 2, 1)))
    return out[0] if len(out) == 1 else jnp.concatenate(out, axis=1)


def _tile_lanes(t, width):
    return t if width == LANES else jnp.concatenate([t] * (width // LANES), axis=1)


def _rope_fwd(x, cos, sin_signed):
    w = x.shape[1]
    return x * _tile_lanes(cos, w) + _rot_half(x) * _tile_lanes(sin_signed, w)


def _rope_bwd(dy, cos, sin_signed):
    w = dy.shape[1]
    return dy * _tile_lanes(cos, w) + _rot_half(dy * _tile_lanes(sin_signed, w))


def _ffn_rows(x, g_ref, wg_ref, wu_ref, wd_ref, h_ref, a_ref, b_ref, s_ref, acc_ref):
    F = wg_ref.shape[0]
    h = _rms_fwd(x, g_ref[...]).astype(BF16)
    h_ref[...] = h
    for c in range(F // FFN_CHUNK):
        cs = pl.ds(c * FFN_CHUNK, FFN_CHUNK)
        a = _nt(h, wg_ref[cs, :])
        b = _nt(h, wu_ref[cs, :])
        a_ref[:, cs] = a.astype(BF16)
        b_ref[:, cs] = b.astype(BF16)
        s = (a * _sigmoid(a) * b).astype(BF16)
        s_ref[:, cs] = s
        y = _dot(s, wd_ref[cs, :])
        if c == 0:
            acc_ref[...] = y
        else:
            acc_ref[...] += y
    return x + FFN_SCALE * acc_ref[...]


def ffn_fwd(x, gain, wgt, wut, wd, name, carry=None):
    T, D = x.shape
    F = wgt.shape[0]
    tm = min(ROW_TILE, T)

    def body(x_ref, g_ref, wg_ref, wu_ref, wd_ref, xo_ref, h_ref, a_ref, b_ref, s_ref, acc_ref):
        xo_ref[...] = _ffn_rows(x_ref[...], g_ref, wg_ref, wu_ref, wd_ref, h_ref, a_ref, b_ref, s_ref, acc_ref)

    wide = jax.ShapeDtypeStruct((T, F), BF16)
    return _call(
        body, name=name, grid=(T // tm,),
        out_shape=(jax.ShapeDtypeStruct((T, D), F32), jax.ShapeDtypeStruct((T, D), BF16), wide, wide, wide),
        in_specs=[_rows(tm, D), _resident((1, D)), _resident((F, D)), _resident((F, D)), _resident((F, D))],
        out_specs=(_rows(tm, D), _rows(tm, D), _rows(tm, F), _rows(tm, F), _rows(tm, F)),
        scratch_shapes=[pltpu.VMEM((tm, D), F32)], args=(x, gain, wgt, wut, wd), carry=carry)


def ffn_fwd_loss(x, gain, wgt, wut, wd, final_gain, target, name):
    T, D = x.shape
    F = wgt.shape[0]
    tm = min(ROW_TILE, T)

    def body(x_ref, g_ref, wg_ref, wu_ref, wd_ref, gf_ref, t_ref,
             loss_ref, dx_ref, dg_ref, h_ref, a_ref, b_ref, s_ref, acc_ref):
        xo = _ffn_rows(x_ref[...], g_ref, wg_ref, wu_ref, wd_ref, h_ref, a_ref, b_ref, s_ref, acc_ref)
        gf = gf_ref[...]
        err = _rms_fwd(xo, gf) - t_ref[...]
        dx, dgt = _rms_bwd(xo, gf, err * (1.0 / D))
        dx_ref[...] = dx

        @pl.when(pl.program_id(0) == 0)
        def _():
            dg_ref[...] = jnp.zeros_like(dg_ref)
            loss_ref[...] = jnp.zeros_like(loss_ref)
        dg_ref[...] += jnp.sum(dgt, axis=0, keepdims=True)
        per_token = jnp.sum(err * err, axis=-1, keepdims=True) * (0.5 / D)
        loss_ref[...] += jnp.broadcast_to(jnp.sum(per_token, axis=0, keepdims=True), (1, LANES))

    return pl.pallas_call(
        body, name=name, grid=(T // tm,),
        out_shape=(jax.ShapeDtypeStruct((1, LANES), F32), jax.ShapeDtypeStruct((T, D), F32), jax.ShapeDtypeStruct((1, D), F32),
                   jax.ShapeDtypeStruct((T, D), BF16), jax.ShapeDtypeStruct((T, F), BF16), jax.ShapeDtypeStruct((T, F), BF16),
                   jax.ShapeDtypeStruct((T, F), BF16)),
        in_specs=[_rows(tm, D), _resident((1, D)), _resident((F, D)), _resident((F, D)), _resident((F, D)), _resident((1, D)),
                  _rows(tm, D)],
        out_specs=(_acc_spec((1, LANES)), _rows(tm, D), _acc_spec((1, D)), _rows(tm, D), _rows(tm, F), _rows(tm, F),
                   _rows(tm, F)),
        scratch_shapes=[pltpu.VMEM((tm, D), F32)], compiler_params=_params("arbitrary"),
    )(x, gain, wgt, wut, wd, final_gain, target)


def ffn_bwd_hidden(dxo, a, b, wd, name, carry=None):
    T, D = dxo.shape
    F = wd.shape[0]
    tm = min(ROW_TILE, T)
    fc = FFN_CHUNK

    def hidden_body(dxo_ref, a_ref, b_ref, wd_ref, da_ref, db_ref):
        g0 = (FFN_SCALE * dxo_ref[...]).astype(BF16)
        for c in range(F // fc):
            cs = pl.ds(c * fc, fc)
            ds = _nt(g0, wd_ref[cs, :])
            a = a_ref[:, cs].astype(F32)
            bb = b_ref[:, cs].astype(F32)
            sa = _sigmoid(a)
            da_ref[:, cs] = (ds * bb * (sa * (1.0 + a * (1.0 - sa)))).astype(BF16)
            db_ref[:, cs] = (ds * (a * sa)).astype(BF16)

    wide = jax.ShapeDtypeStruct((T, F), BF16)
    return _call(
        hidden_body, name=name, grid=(T // tm,), out_shape=(wide, wide),
        in_specs=[_rows(tm, D), _rows(tm, F), _rows(tm, F), _resident((F, D))],
        out_specs=(_rows(tm, F), _rows(tm, F)), args=(dxo, a, b, wd), carry=carry)


def ffn_bwd_input(dxo, x, gain, da, db, wg, wu, name, carry=None):
    T, D = x.shape
    F = wg.shape[0]
    tm = min(ROW_TILE, T)

    def input_body(dxo_ref, x_ref, g_ref, da_ref, db_ref, wg_ref, wu_ref, dx_ref, dg_ref):
        dh = _dot(da_ref[...], wg_ref[...]) + _dot(db_ref[...], wu_ref[...])
        dx, dgt = _rms_bwd(x_ref[...], g_ref[...], dh)
        dx_ref[...] = dxo_ref[...] + dx

        @pl.when(pl.program_id(0) == 0)
        def _():
            dg_ref[...] = jnp.zeros_like(dg_ref)
        dg_ref[...] += jnp.sum(dgt, axis=0, keepdims=True)

    return _call(
        input_body, name=name, grid=(T // tm,),
        out_shape=(jax.ShapeDtypeStruct((T, D), F32), jax.ShapeDtypeStruct((1, D), F32)),
        in_specs=[_rows(tm, D), _rows(tm, D), _resident((1, D)), _rows(tm, F), _rows(tm, F), _resident((F, D)),
                  _resident((F, D))],
        out_specs=(_rows(tm, D), _acc_spec((1, D))), args=(dxo, x, gain, da, db, wg, wu), carry=carry)


def wgrad(a, b, name, carry=None, b_scale=None):
    T, M = a.shape
    N = b.shape[1]
    fixed = M * N * (4 + 2)
    per_row = 2 * (M * a.dtype.itemsize + N * b.dtype.itemsize)
    tk = ROW_TILE
    while fixed + 2 * tk * per_row <= WGRAD_VMEM and 2 * tk <= WGRAD_TILE_ROWS:
        tk *= 2
    tk = min(tk, T)
    nk = T // tk

    def body(a_ref, b_ref, o_ref, acc_ref):
        k = pl.program_id(0)
        bt = b_ref[...] if b_scale is None else b_scale * b_ref[...]
        part = _tn(a_ref[...].astype(BF16), bt.astype(BF16))

        @pl.when(k == 0)
        def _():
            acc_ref[...] = part

        @pl.when(k > 0)
        def _():
            acc_ref[...] += part

        @pl.when(k == nk - 1)
        def _():
            o_ref[...] = acc_ref[...].astype(BF16)

    (out,), carried = _call(
        body, name=name, grid=(nk,), out_shape=(jax.ShapeDtypeStruct((M, N), BF16),),
        in_specs=[_rows(tk, M), _rows(tk, N)], out_specs=(_resident((M, N)),),
        scratch_shapes=[pltpu.VMEM((M, N), F32)], args=(a, b), carry=carry)
    return out, carried


def wgrad_stacked(pieces, b, name, total_rows, at, into=None, moves=None):
    T, N = b.shape
    n = len(pieces)
    widths = [p.shape[1] for p in pieces]
    offs = [sum(widths[:i]) for i in range(n)]
    M = sum(widths)
    tk = ROW_TILE
    while 2 * tk * M <= WGRAD_TILE_ELEMS and 2 * tk <= WGRAD_TILE_ROWS:
        tk *= 2
    tk = min(tk, T)
    nk = T // tk
    moves = moves or [(0, 0, M)]
    n_in = n + 1 + (into is not None)

    def body(*refs):
        a_refs, b_ref = refs[:n], refs[n]
        o_ref, acc_ref, stage, sem = refs[n_in:]
        k = pl.program_id(0)

        @pl.when(k == 0)
        def _():
            acc_ref[...] = jnp.zeros_like(acc_ref)
        bt = b_ref[...].astype(BF16)
        for a_ref, off, width in zip(a_refs, offs, widths):
            acc_ref[pl.ds(off, width), :] += _tn(a_ref[...].astype(BF16), bt)

        @pl.when(k == nk - 1)
        def _():
            for to, start, rows in moves:
                stage[pl.ds(to, rows), :] = acc_ref[pl.ds(start, rows), :].astype(BF16)
            cp = pltpu.make_async_copy(stage, o_ref.at[pl.ds(at, M)], sem)
            cp.start()
            cp.wait()

    return pl.pallas_call(
        body, name=name, grid=(nk,), out_shape=jax.ShapeDtypeStruct((total_rows, N), BF16),
        in_specs=[_rows(tk, width) for width in widths] + [_rows(tk, N)] + [_ANY] * (into is not None), out_specs=_ANY,
        scratch_shapes=[pltpu.VMEM((M, N), F32), pltpu.VMEM((M, N), BF16), pltpu.SemaphoreType.DMA],
        input_output_aliases={n + 1: 0} if into is not None else {}, compiler_params=_params("arbitrary"),
    )(*pieces, b, *([into] if into is not None else []))


def _w_in_rows(D):
    KV = N_KV_HEADS * HEAD_DIM
    return 0, D, 2 * D, 3 * D, 3 * D + KV, 3 * D + 2 * KV, 4 * D + 2 * KV


def mix_in_fwd(x, gain, wint, wq, gate_b, cos, sin_signed, name, carry=None):
    T, D = x.shape
    KV = N_KV_HEADS * HEAD_DIM
    tm = min(ROW_TILE, T)
    o_ga, o_gb, _, o_k, o_v, o_gc, o_gt = _w_in_rows(D)

    def body(x_ref, g_ref, w_ref, wq_ref, gb_ref, cos_ref, sin_ref,
             h_ref, ga_ref, gb_out_ref, u0_ref, q_ref, sgc_ref, sgt_ref, k_ref, v_ref):
        h = _rms_fwd(x_ref[...], g_ref[...]).astype(BF16)
        h_ref[...] = h
        cos, sin = cos_ref[...], sin_ref[...]
        ga = _nt(h, w_ref[pl.ds(o_ga, D), :])
        gb = _nt(h, w_ref[pl.ds(o_gb, D), :])
        ga_ref[...] = ga.astype(BF16)
        gb_out_ref[...] = gb.astype(BF16)
        u0_ref[...] = (ga * _sigmoid(gb)).astype(BF16)
        q = _nt(h, wq_ref[...])
        q_ref[...] = _rope_fwd(q, cos, sin).astype(BF16)
        gc = _nt(h, w_ref[pl.ds(o_gc, D), :]) + gb_ref[:, pl.ds(0, D)]
        sgc_ref[...] = _sigmoid(gc).astype(BF16)
        gt = _nt(h, w_ref[pl.ds(o_gt, D), :]) + gb_ref[:, pl.ds(D, D)]
        sgt_ref[...] = _sigmoid(gt).astype(BF16)
        k = _nt(h, w_ref[pl.ds(o_k, KV), :])
        k_ref[...] = _rope_fwd(k, cos, sin).astype(BF16)
        v_ref[...] = _nt(h, w_ref[pl.ds(o_v, KV), :]).astype(BF16)

    big = jax.ShapeDtypeStruct((T, D), BF16)
    small = jax.ShapeDtypeStruct((T, KV), BF16)
    return _call(
        body, name=name, grid=(T // tm,),
        out_shape=(big, big, big, big, big, big, big, small, small),
        in_specs=[_rows(tm, D), _resident((1, D)), _resident(wint.shape), _resident(wq.shape), _resident((1, 2 * D)),
                  _rows(tm, LANES), _rows(tm, LANES)],
        out_specs=(_rows(tm, D),) * 7 + (_rows(tm, KV),) * 2,
        args=(x, gain, wint, wq, gate_b, cos, sin_signed), carry=carry)


def _from_prev(rows):
    qi = lax.broadcasted_iota(jnp.int32, (rows, WINDOW), 0) % WINDOW
    return lax.broadcasted_iota(jnp.int32, (rows, WINDOW), 1) > qi


def _fold(x, g, from_prev):
    lo = 2 * WINDOW * g
    return jnp.where(from_prev, x[:, lo:lo + WINDOW], x[:, lo + WINDOW:lo + 2 * WINDOW])


def _unfold(folded, from_prev):
    zero = jnp.zeros_like(folded[0])
    parts = []
    for x in folded:
        parts += [jnp.where(from_prev, x, zero), jnp.where(from_prev, zero, x)]
    return jnp.concatenate(parts, axis=1)


def _kv_lane_head(rows, width):
    return lax.broadcasted_iota(jnp.int32, (rows, width), 1) // HEAD_DIM


def _block_diag(win):
    head = _kv_lane_head(*win.shape)
    zero = jnp.zeros_like(win)
    return jnp.concatenate([jnp.where(head == g, win, zero) for g in range(N_KV_HEADS)], axis=0)


def _diag_blocks_sum(bd, keys):
    head = _kv_lane_head(keys, bd.shape[1])
    out = jnp.zeros((keys, bd.shape[1]), F32)
    for g in range(N_KV_HEADS):
        out = jnp.where(head == g, bd[g * keys:(g + 1) * keys], out)
    return out


def _kv_windows(k_ref, kh_ref, v_ref, vh_ref, j):
    rows = pl.ds(j * WINDOW, WINDOW)
    if j == 0:
        kprev, vprev = kh_ref[...], vh_ref[...]
    else:
        prev = pl.ds((j - 1) * WINDOW, WINDOW)
        kprev, vprev = k_ref[prev, :], v_ref[prev, :]
    return jnp.concatenate([kprev, k_ref[rows, :]], axis=0), jnp.concatenate([vprev, v_ref[rows, :]], axis=0)


def _stack_slots(ref, j, group, KV):
    rows = pl.ds(j * WINDOW, WINDOW)
    return jnp.concatenate([ref[rows, pl.ds(KV * hh, KV)] for hh in range(group)], axis=0)


def _attn_exp(qs, kbd, sink_ref, from_prev, no_prev):
    s = _nt(qs, kbd)
    if no_prev is not None:
        qi = lax.broadcasted_iota(jnp.int32, from_prev.shape, 0) % WINDOW
        absent = lax.broadcasted_iota(jnp.int32, from_prev.shape, 1) > jnp.where(no_prev, qi, WINDOW)
    out = []
    for g in range(N_KV_HEADS):
        sg = _fold(s, g, from_prev) * (HEAD_DIM ** -0.5)
        if no_prev is not None:
            sg = jnp.where(absent, NEG_INF, sg)
        sink = sink_ref[g]
        m = jnp.maximum(jnp.max(sg, axis=-1, keepdims=True), sink)
        out.append((jnp.exp(sg - m), jnp.exp(sink - m)))
    return out


def _spread_over_heads(cols, rows, KV):
    head = _kv_lane_head(rows, KV)
    out = jnp.zeros((rows, KV), F32)
    for g, col in enumerate(cols):
        out = jnp.where(head == g, col, out)
    return out


def _halo_rows_spec(tq, width, sub):
    return pl.BlockSpec((sub, width), lambda i: (jnp.maximum(i * (tq // sub) - 1, 0), 0))


def attn_fwd(q, k, v, sink_col, name, carry=None):
    T, D = q.shape
    KV = k.shape[1]
    group = D // KV
    tq = min(ROW_TILE, T)
    nsub = tq // WINDOW
    rows, wide = group * WINDOW, N_KV_HEADS * 2 * WINDOW

    def body(q_ref, k_ref, kh_ref, v_ref, vh_ref, sink_ref, o_ref):
        from_prev = _from_prev(rows)
        head = _kv_lane_head(wide, KV)
        block = lax.broadcasted_iota(jnp.int32, head.shape, 0) // (2 * WINDOW)
        ones_bd = jnp.where(head == block, 1.0, 0.0).astype(BF16)
        for j in range(nsub):
            k_win, v_win = _kv_windows(k_ref, kh_ref, v_ref, vh_ref, j)
            parts = _attn_exp(_stack_slots(q_ref, j, group, KV), _block_diag(k_win), sink_ref, from_prev,
                              pl.program_id(0) == 0 if j == 0 else None)
            p = _unfold([pg.astype(BF16) for pg, _ in parts], from_prev)
            both = _dot(p, jnp.concatenate([_block_diag(v_win), ones_bd], axis=1))
            denom = both[:, KV:] + _spread_over_heads([es for _, es in parts], rows, KV)
            out = (both[:, :KV] / denom).astype(BF16)
            for hh in range(group):
                o_ref[pl.ds(j * WINDOW, WINDOW), pl.ds(KV * hh, KV)] = out[hh * WINDOW:(hh + 1) * WINDOW]

    (o,), carried = _call(
        body, name=name, grid=(T // tq,),
        out_shape=(jax.ShapeDtypeStruct((T, D), BF16),),
        in_specs=[_rows(tq, D), _rows(tq, KV), _halo_rows_spec(tq, KV, WINDOW), _rows(tq, KV),
                  _halo_rows_spec(tq, KV, WINDOW), _resident(sink_col.shape)],
        out_specs=(_rows(tq, D),), args=(q, k, k, v, v, sink_col), carry=carry)
    return o, carried


def attn_bwd(q, k, v, do, sink_col, cos, sin_signed, name):
    T, D = q.shape
    KV = k.shape[1]
    group = D // KV
    tq = min(ROW_TILE, T)
    nsub = tq // WINDOW
    nt = T // tq
    scale = HEAD_DIM ** -0.5
    rows, wide = group * WINDOW, N_KV_HEADS * 2 * WINDOW

    def rev(i):
        return nt - 1 - i

    def body(q_ref, k_ref, kh_ref, v_ref, vh_ref, do_ref, sink_ref, cos_ref, sin_ref,
             dq_ref, dk_ref, dv_ref, dsink_ref, dq_acc, dk_acc, dv_acc, carry_k, carry_v):
        i = pl.program_id(0)

        @pl.when(i == 0)
        def _():
            carry_k[...] = jnp.zeros_like(carry_k)
            carry_v[...] = jnp.zeros_like(carry_v)
            dsink_ref[...] = jnp.zeros_like(dsink_ref)

        dk_acc[...] = jnp.zeros_like(dk_acc)
        dv_acc[...] = jnp.zeros_like(dv_acc)
        from_prev = _from_prev(rows)
        lane = lax.broadcasted_iota(jnp.int32, (1, LANES), 1)
        for j in range(nsub):
            k_win, v_win = _kv_windows(k_ref, kh_ref, v_ref, vh_ref, j)
            kbd, vbd = _block_diag(k_win), _block_diag(v_win)
            qs, dos = _stack_slots(q_ref, j, group, KV), _stack_slots(do_ref, j, group, KV)
            dp = _nt(dos, vbd)
            probs16, ds16 = [], []
            for g, (pg, es) in enumerate(_attn_exp(qs, kbd, sink_ref, from_prev, rev(i) == 0 if j == 0 else None)):
                inv = 1.0 / (jnp.sum(pg, axis=-1, keepdims=True) + es)
                probs = pg * inv
                dpg = _fold(dp, g, from_prev)
                delta = jnp.sum(probs * dpg, axis=-1, keepdims=True)
                probs16.append(probs.astype(BF16))
                ds16.append((probs * (dpg - delta) * scale).astype(BF16))
                dsk = -(es * inv * delta)
                for hh in range(group):
                    tot = jnp.sum(dsk[hh * WINDOW:(hh + 1) * WINDOW], axis=0, keepdims=True)
                    dsink_ref[pl.ds(hh, 1), :] += jnp.where(lane == g, tot, 0.0)
            ds = _unfold(ds16, from_prev)
            dqs = _dot(ds, kbd)
            for hh in range(group):
                dq_acc[pl.ds(j * WINDOW, WINDOW), pl.ds(KV * hh, KV)] = dqs[hh * WINDOW:(hh + 1) * WINDOW]
            keys = pl.ds(j * WINDOW, 2 * WINDOW)
            dk_acc[keys, :] += _diag_blocks_sum(_tn(ds, qs), 2 * WINDOW)
            dv_acc[keys, :] += _diag_blocks_sum(_tn(_unfold(probs16, from_prev), dos), 2 * WINDOW)

        tail = pl.ds(tq, WINDOW)
        dk_acc[tail, :] += carry_k[...]
        dv_acc[tail, :] += carry_v[...]
        carry_k[...] = dk_acc[pl.ds(0, WINDOW), :]
        carry_v[...] = dv_acc[pl.ds(0, WINDOW), :]
        cos, sin = cos_ref[...], sin_ref[...]
        dq_ref[...] = _rope_bwd(dq_acc[...], cos, sin).astype(BF16)
        dk_ref[...] = _rope_bwd(dk_acc[pl.ds(WINDOW, tq), :], cos, sin).astype(BF16)
        dv_ref[...] = dv_acc[pl.ds(WINDOW, tq), :].astype(BF16)

    def rrows(n):
        return pl.BlockSpec((tq, n), lambda i: (rev(i), 0))

    def rhalo(n):
        return pl.BlockSpec((WINDOW, n), lambda i: (jnp.maximum(rev(i) * nsub - 1, 0), 0))

    return pl.pallas_call(
        body, name=name, grid=(nt,),
        out_shape=(jax.ShapeDtypeStruct((T, D), BF16), jax.ShapeDtypeStruct((T, KV), BF16),
                   jax.ShapeDtypeStruct((T, KV), BF16), jax.ShapeDtypeStruct((SUBLANES, LANES), F32)),
        in_specs=[rrows(D), rrows(KV), rhalo(KV), rrows(KV), rhalo(KV), rrows(D), _resident(sink_col.shape),
                  rrows(LANES), rrows(LANES)],
        out_specs=(rrows(D), rrows(KV), rrows(KV), _acc_spec((SUBLANES, LANES))),
        scratch_shapes=[pltpu.VMEM((tq, D), F32), pltpu.VMEM((WINDOW + tq, KV), F32), pltpu.VMEM((WINDOW + tq, KV), F32),
                        pltpu.VMEM((WINDOW, KV), F32), pltpu.VMEM((WINDOW, KV), F32)],
        compiler_params=_params("arbitrary"),
    )(q, k, k, v, v, do, sink_col, cos, sin_signed)


def _ln_stats(u):
    mu = jnp.mean(u, axis=-1, keepdims=True)
    d = u - mu
    rstd = lax.rsqrt(jnp.mean(d * d, axis=-1, keepdims=True) + LN_EPS)
    return d * rstd, rstd


def _lag_taps(b, K):
    return [(a, K - 1 - (SUBLANES * a + b)) for a in range(-(-K // SUBLANES)) if SUBLANES * a + b <= K - 1]


def _conv_chunks(tm, D, chunk):
    def rows(c, carry):
        r0 = pl.multiple_of(c * CONV_ROWS, CONV_ROWS)
        for l0 in range(0, D, CONV_LANES):
            chunk(r0, pl.ds(l0, CONV_LANES))
        return carry
    lax.fori_loop(0, tm // CONV_ROWS, rows, 0)


def _conv_causal(buf, w_ref, bias_ref, out_ref, tm, D, K):
    def chunk(r0, lanes):
        acc = jnp.broadcast_to(bias_ref[:, lanes], (CONV_ROWS, CONV_LANES))
        for b in range(SUBLANES):
            y = None
            for a, k in _lag_taps(b, K):
                start = pl.multiple_of(r0 + CONV_HALO - SUBLANES * (a + 1), SUBLANES)
                t = buf[pl.ds(start, CONV_ROWS + SUBLANES), lanes] * w_ref[pl.ds(k, 1), lanes]
                y = t if y is None else y + t
            acc = acc + y[SUBLANES - b:SUBLANES - b + CONV_ROWS]
        out_ref[pl.ds(r0, CONV_ROWS), lanes] = acc
    _conv_chunks(tm, D, chunk)


def _conv_anticausal(dbuf, w_ref, out_ref, tm, D, K):
    def chunk(r0, lanes):
        acc = jnp.zeros((CONV_ROWS, CONV_LANES), F32)
        for b in range(SUBLANES):
            y = None
            for a, k in _lag_taps(b, K):
                start = pl.multiple_of(r0 + SUBLANES * a, SUBLANES)
                t = dbuf[pl.ds(start, CONV_ROWS + SUBLANES), lanes] * w_ref[pl.ds(k, 1), lanes]
                y = t if y is None else y + t
            acc = acc + y[b:b + CONV_ROWS]
        out_ref[pl.ds(r0, CONV_ROWS), lanes] = acc
    _conv_chunks(tm, D, chunk)


def _conv_tap_grads(dbuf, ubuf, acc_ref, tm, D, K):
    reach = SUBLANES * (-(-K // SUBLANES) - 1)

    def chunk(r0, lanes):
        d = dbuf[pl.ds(r0, CONV_ROWS), lanes]
        around = ubuf[pl.ds(pl.multiple_of(r0 + CONV_HALO - reach - SUBLANES, SUBLANES), CONV_ROWS + reach + SUBLANES), lanes]
        for b in range(SUBLANES):
            shifted = around[SUBLANES - b:SUBLANES - b + CONV_ROWS + reach]
            for a, k in _lag_taps(b, K):
                prod = d * shifted[reach - SUBLANES * a:reach - SUBLANES * a + CONV_ROWS]
                part = prod[0:SUBLANES]
                for i in range(1, CONV_ROWS // SUBLANES):
                    part = part + prod[SUBLANES * i:SUBLANES * (i + 1)]
                acc_ref[k, :, lanes] += part
    _conv_chunks(tm, D, chunk)


def mix_out_fwd(x, u0, o, sgc, sgt, dw_w, dw_b, ln_g, ln_b, wcp, wo, wout, name):
    T, D = x.shape
    tm = min(ROW_TILE, T)
    K = dw_w.shape[0]

    def body(x_ref, u_ref, uh_ref, o_ref, sgc_ref, sgt_ref, w_ref, b_ref, lg_ref, lb_ref, wcp_ref, wo_ref, wout_ref,
             x2_ref, u1_ref, co_ref, ao_ref, mg_ref, buf, conv):
        keep = (pl.program_id(0) > 0).astype(F32)
        buf[pl.ds(0, CONV_HALO), :] = uh_ref[...].astype(F32) * keep
        buf[pl.ds(CONV_HALO, tm), :] = u_ref[...].astype(F32)
        _conv_causal(buf, w_ref, b_ref, conv, tm, D, K)
        acc = conv[...]
        u1_ref[...] = acc.astype(BF16)
        xhat, _ = _ln_stats(acc)
        u2 = xhat * lg_ref[...] + lb_ref[...]
        u3 = (u2 * _sigmoid(u2)).astype(BF16)
        co = _dot(u3, wcp_ref[...])
        ao = _dot(o_ref[...], wo_ref[...])
        co_ref[...] = co.astype(BF16)
        ao_ref[...] = ao.astype(BF16)
        merged = (sgc_ref[...].astype(F32) * co + sgt_ref[...].astype(F32) * ao).astype(BF16)
        mg_ref[...] = merged
        x2_ref[...] = x_ref[...] + _dot(merged, wout_ref[...])

    big = jax.ShapeDtypeStruct((T, D), BF16)
    vec = _resident((1, D))
    return pl.pallas_call(
        body, name=name, grid=(T // tm,),
        out_shape=(jax.ShapeDtypeStruct((T, D), F32), big, big, big, big),
        in_specs=[_rows(tm, D), _rows(tm, D), _halo_rows_spec(tm, D, CONV_HALO), _rows(tm, D), _rows(tm, D), _rows(tm, D),
                  _resident((K, D)), vec, vec, vec, _resident((D, D)), _resident((D, D)), _resident((D, D))],
        out_specs=(_rows(tm, D),) * 5,
        scratch_shapes=[pltpu.VMEM((CONV_HALO + tm, D), F32), pltpu.VMEM((tm, D), F32)],
        compiler_params=_params("arbitrary"),
    )(x, u0, u0, o, sgc, sgt, dw_w, dw_b, ln_g, ln_b, wcp, wo, wout)


def mix_out_bwd(dx2, u1, co, ao, sgc, sgt, ln_g, ln_b, wcp, wo, wout, name, carry=None):
    T, D = dx2.shape
    tm = min(ROW_TILE, T)

    def body(dx_ref, u1_ref, co_ref, ao_ref, sgc_ref, sgt_ref, lg_ref, lb_ref, wcp_ref, wo_ref, wout_ref,
             dgc_ref, dgt_ref, do_ref, du1_ref, dco_ref, dao_ref, u3_ref, sums_ref):
        dm = _nt(dx_ref[...].astype(BF16), wout_ref[...])
        sgc, sgt = sgc_ref[...].astype(F32), sgt_ref[...].astype(F32)
        dco = (dm * sgc).astype(BF16)
        dao = (dm * sgt).astype(BF16)
        dgc = dm * co_ref[...].astype(F32) * sgc * (1.0 - sgc)
        dgt = dm * ao_ref[...].astype(F32) * sgt * (1.0 - sgt)
        dco_ref[...] = dco
        dao_ref[...] = dao
        dgc_ref[...] = dgc.astype(BF16)
        dgt_ref[...] = dgt.astype(BF16)
        do_ref[...] = _nt(dao, wo_ref[...]).astype(BF16)
        du3 = _nt(dco, wcp_ref[...])
        xhat, rstd = _ln_stats(u1_ref[...].astype(F32))
        g = lg_ref[...]
        u2 = xhat * g + lb_ref[...]
        su = _sigmoid(u2)
        u3_ref[...] = (u2 * su).astype(BF16)
        du2 = du3 * (su * (1.0 + u2 * (1.0 - su)))
        dxh = du2 * g
        du1 = rstd * (dxh - jnp.mean(dxh, axis=-1, keepdims=True) - xhat * jnp.mean(dxh * xhat, axis=-1, keepdims=True))
        du1_ref[...] = du1.astype(BF16)

        @pl.when(pl.program_id(0) == 0)
        def _():
            sums_ref[...] = jnp.zeros_like(sums_ref)
        for r, val in enumerate((dgc, dgt, du2 * xhat, du2, du1)):
            sums_ref[pl.ds(r, 1), :] += jnp.sum(val, axis=0, keepdims=True)

    big = jax.ShapeDtypeStruct((T, D), BF16)
    vec = _resident((1, D))
    return _call(
        body, name=name, grid=(T // tm,),
        out_shape=(big,) * 7 + (jax.ShapeDtypeStruct((8, D), F32),),
        in_specs=[_rows(tm, D)] * 6 + [vec, vec, _resident((D, D)), _resident((D, D)), _resident((D, D))],
        out_specs=(_rows(tm, D),) * 7 + (_acc_spec((8, D)),),
        args=(dx2, u1, co, ao, sgc, sgt, ln_g, ln_b, wcp, wo, wout), carry=carry)


def conv_bwd(du1, u0, ga, gb, dw_w, name, carry=None):
    T, D = du1.shape
    tm = min(ROW_TILE, T)
    nt = T // tm
    K = dw_w.shape[0]
    per = tm // CONV_HALO

    def body(d_ref, dn_ref, u_ref, uh_ref, ga_ref, gb_ref, w_ref, dga_ref, dgb_ref, dw_ref, dbuf, ubuf, du0_buf, taps):
        i = pl.program_id(0)
        dbuf[pl.ds(0, tm), :] = d_ref[...].astype(F32)
        dbuf[pl.ds(tm, CONV_HALO), :] = dn_ref[...].astype(F32) * (i < nt - 1).astype(F32)
        ubuf[pl.ds(0, CONV_HALO), :] = uh_ref[...].astype(F32) * (i > 0).astype(F32)
        ubuf[pl.ds(CONV_HALO, tm), :] = u_ref[...].astype(F32)

        @pl.when(i == 0)
        def _():
            taps[...] = jnp.zeros_like(taps)

        _conv_anticausal(dbuf, w_ref, du0_buf, tm, D, K)
        _conv_tap_grads(dbuf, ubuf, taps, tm, D, K)
        du0 = du0_buf[...]
        ga, gb = ga_ref[...].astype(F32), gb_ref[...].astype(F32)
        sg = _sigmoid(gb)
        dga_ref[...] = (du0 * sg).astype(BF16)
        dgb_ref[...] = (du0 * ga * sg * (1.0 - sg)).astype(BF16)

        @pl.when(i == nt - 1)
        def _():
            for k in range(K):
                dw_ref[pl.ds(k, 1), :] = jnp.sum(taps[k], axis=0, keepdims=True)

    nxt = pl.BlockSpec((CONV_HALO, D), lambda i: (jnp.minimum((i + 1) * per, nt * per - 1), 0))
    big = jax.ShapeDtypeStruct((T, D), BF16)
    return _call(
        body, name=name, grid=(nt,),
        out_shape=(big, big, jax.ShapeDtypeStruct((K, D), F32)),
        in_specs=[_rows(tm, D), nxt, _rows(tm, D), _halo_rows_spec(tm, D, CONV_HALO), _rows(tm, D), _rows(tm, D),
                  _resident((K, D))],
        out_specs=(_rows(tm, D), _rows(tm, D), _acc_spec((K, D))),
        scratch_shapes=[pltpu.VMEM((tm + CONV_HALO, D), F32), pltpu.VMEM((CONV_HALO + tm, D), F32), pltpu.VMEM((tm, D), F32),
                        pltpu.VMEM((K, SUBLANES, D), F32)],
        args=(du1, du1, u0, u0, ga, gb, dw_w), carry=carry)


def mix_in_bwd(dx2, x, gain, wint, wq, pieces, name):
    T, D = x.shape
    tm = min(ROW_TILE, T)
    widths = [p.shape[1] for p in pieces]
    offs = _w_in_rows(D)

    def body(dx2_ref, x_ref, g_ref, w_ref, wq_ref, *rest):
        piece_refs, (dx_ref, dg_ref) = rest[:len(pieces)], rest[len(pieces):]
        dh = None
        for n, (p_ref, off, w) in enumerate(zip(piece_refs, offs, widths)):
            t = _dot(p_ref[...], wq_ref[...] if n == 2 else w_ref[pl.ds(off, w), :])
            dh = t if dh is None else dh + t
        dx, dgt = _rms_bwd(x_ref[...], g_ref[...], dh)
        dx_ref[...] = dx2_ref[...] + dx

        @pl.when(pl.program_id(0) == 0)
        def _():
            dg_ref[...] = jnp.zeros_like(dg_ref)
        dg_ref[...] += jnp.sum(dgt, axis=0, keepdims=True)

    return pl.pallas_call(
        body, name=name, grid=(T // tm,),
        out_shape=(jax.ShapeDtypeStruct((T, D), F32), jax.ShapeDtypeStruct((1, D), F32)),
        in_specs=[_rows(tm, D), _rows(tm, D), _resident((1, D)), _resident(wint.shape), _resident(wq.shape)]
        + [_rows(tm, w) for w in widths],
        out_specs=(_rows(tm, D), _acc_spec((1, D))),
        compiler_params=_params("arbitrary"),
    )(dx2, x, gain, wint, wq, *pieces)


def rope_tables(positions):
    half = HEAD_DIM // 2
    inv_freq = ROPE_THETA ** (-jnp.arange(half, dtype=F32) / half)
    ang = positions.astype(F32)[:, None] * inv_freq
    cos, sin = jnp.cos(ang), jnp.sin(ang)
    reps = LANES // HEAD_DIM
    return jnp.tile(jnp.concatenate([cos, cos], axis=-1), (1, reps)), jnp.tile(jnp.concatenate([-sin, sin], axis=-1), (1, reps))


def _place():
    return lax.axis_index("x"), lax.axis_index("y"), lax.axis_index("c")


def all_gather(blocks, name):
    n = len(blocks)
    send = gather_send(blocks)
    forward = gather_forward(send.out_shapes)
    n_sems = len(send.sems)

    def body(*refs):
        ins, outs, sems = refs[:n], refs[n:2 * n], refs[2 * n:]
        send.start(ins, outs, sems[:n_sems])
        send.finish(ins, outs, sems[:n_sems])
        forward.start((), outs, sems[n_sems:])
        forward.finish((), outs, sems[n_sems:])

    return pl.pallas_call(
        body, name=name, out_shape=tuple(send.out_shapes), in_specs=[_ANY] * n, out_specs=(_ANY,) * n,
        scratch_shapes=list(send.sems) + list(forward.sems),
    )(*blocks)


def _chips_across(x, y):
    return [(1 - x, y), (x, 1 - y), (1 - x, 1 - y)]


def _dev_index(x, y, c):
    return 4 * x + 2 * y + c


def gather_send(blocks):
    n = len(blocks)

    def copies(in_refs, out_refs, sems):
        send, recv, local = sems
        x, y, c = _place()
        targets = [(x, y, 1 - c)] + [(*chip, c) for chip in _chips_across(x, y)]
        outgoing, incoming, mine = [], [], []
        for i, (x_ref, out_ref) in enumerate(zip(in_refs, out_refs)):
            for k, t in enumerate(targets):
                pair = dict(send_sem=send.at[4 * i + k], recv_sem=recv.at[4 * i + k], device_id=t, device_id_type=MESH)
                outgoing.append(pltpu.make_async_remote_copy(src_ref=x_ref, dst_ref=out_ref.at[_dev_index(x, y, c)], **pair))
                incoming.append(pltpu.make_async_remote_copy(src_ref=x_ref, dst_ref=out_ref.at[_dev_index(*t)], **pair))
            mine.append(pltpu.make_async_copy(x_ref, out_ref.at[_dev_index(x, y, c)], local.at[i]))
        return outgoing, incoming, mine

    def start(*refs):
        outgoing, _, mine = copies(*refs)
        for cp in mine + outgoing:
            cp.start()

    def finish(*refs):
        outgoing, incoming, mine = copies(*refs)
        for cp in incoming:
            cp.wait_recv()
        for cp in outgoing:
            cp.wait_send()
        for cp in mine:
            cp.wait()

    return Carry(ins=tuple(blocks), out_shapes=tuple(jax.ShapeDtypeStruct((N_DEV,) + b.shape, b.dtype) for b in blocks),
                 aliases={}, sems=(pltpu.SemaphoreType.DMA((4 * n,)), pltpu.SemaphoreType.DMA((4 * n,)),
                                   pltpu.SemaphoreType.DMA((n,))), start=start, finish=finish)


def gather_forward(gathered):
    n = len(gathered)

    def copies(in_refs, out_refs, sems):
        send, recv = sems
        x, y, c = _place()
        outgoing, incoming = [], []
        for i, buf in enumerate(out_refs):
            for k, chip in enumerate(_chips_across(x, y)):
                pair = dict(send_sem=send.at[3 * i + k], recv_sem=recv.at[3 * i + k], device_id=(x, y, 1 - c),
                            device_id_type=MESH)
                rows = buf.at[_dev_index(*chip, c)]
                outgoing.append(pltpu.make_async_remote_copy(src_ref=rows, dst_ref=rows, **pair))
                theirs = buf.at[_dev_index(*chip, 1 - c)]
                incoming.append(pltpu.make_async_remote_copy(src_ref=theirs, dst_ref=theirs, **pair))
        return outgoing, incoming

    def start(*refs):
        for cp in copies(*refs)[0]:
            cp.start()

    def finish(*refs):
        outgoing, incoming = copies(*refs)
        for cp in incoming:
            cp.wait_recv()
        for cp in outgoing:
            cp.wait_send()

    return Carry(ins=tuple(gathered), out_shapes=tuple(jax.ShapeDtypeStruct(g.shape, g.dtype) for g in gathered),
                 aliases={i: i for i in range(n)},
                 sems=(pltpu.SemaphoreType.DMA((3 * n,)), pltpu.SemaphoreType.DMA((3 * n,))), start=start, finish=finish)


def gather_whole(blocks):
    send = gather_send(blocks)
    forward = gather_forward(send.out_shapes)
    n = len(send.sems)

    def relay(ins, outs, sems):
        send.finish(ins, outs, sems[:n])
        forward.start((), outs, sems[n:])

    return Carry(ins=send.ins, out_shapes=send.out_shapes, aliases={}, sems=send.sems + forward.sems,
                 start=lambda ins, outs, sems: send.start(ins, outs, sems[:n]), relay=relay,
                 finish=lambda ins, outs, sems: forward.finish((), outs, sems[n:]))


def swap_halves(by_core):
    n = len(by_core)

    def copies(in_refs, out_refs, sems):
        send, recv = sems
        x, y, c = _place()
        return [pltpu.make_async_remote_copy(src_ref=a.at[:, 1 - c], dst_ref=r, send_sem=send.at[i], recv_sem=recv.at[i],
                                             device_id=(x, y, 1 - c), device_id_type=MESH)
                for i, (a, r) in enumerate(zip(in_refs, out_refs))]

    def start(*refs):
        for cp in copies(*refs):
            cp.start()

    def finish(*refs):
        for cp in copies(*refs):
            cp.wait()

    shapes = tuple(jax.ShapeDtypeStruct((a.shape[0],) + a.shape[2:], a.dtype) for a in by_core)
    return Carry(ins=tuple(by_core), out_shapes=shapes, aliases={},
                 sems=(pltpu.SemaphoreType.DMA((n,)), pltpu.SemaphoreType.DMA((n,))), start=start, finish=finish)


def exchange_between_chips(by_chip):
    n = len(by_chip)

    def copies(in_refs, out_refs, sems):
        send, recv = sems
        x, y, c = _place()
        out = []
        for i, (s, r) in enumerate(zip(in_refs, out_refs)):
            for k, (tx, ty) in enumerate(_chips_across(x, y)):
                out.append(pltpu.make_async_remote_copy(
                    src_ref=s.at[2 * tx + ty], dst_ref=r.at[k], send_sem=send.at[3 * i + k], recv_sem=recv.at[3 * i + k],
                    device_id=(tx, ty, c), device_id_type=MESH))
        return out

    def start(*refs):
        for cp in copies(*refs):
            cp.start()

    def finish(*refs):
        for cp in copies(*refs):
            cp.wait()

    shapes = tuple(jax.ShapeDtypeStruct((3,) + a.shape[1:], a.dtype) for a in by_chip)
    return Carry(ins=tuple(by_chip), out_shapes=shapes, aliases={},
                 sems=(pltpu.SemaphoreType.DMA((3 * n,)), pltpu.SemaphoreType.DMA((3 * n,))), start=start, finish=finish)


def run_exchange(carry, name):
    n_in = len(carry.ins)
    n_out = len(carry.out_shapes)

    def body(*refs):
        parts = refs[:n_in], refs[n_in:n_in + n_out], refs[n_in + n_out:]
        carry.start(*parts)
        carry.finish(*parts)

    return pl.pallas_call(
        body, name=name, out_shape=tuple(carry.out_shapes), in_specs=[_ANY] * n_in, out_specs=(_ANY,) * n_out,
        scratch_shapes=list(carry.sems), input_output_aliases=dict(carry.aliases),
    )(*carry.ins)


def pair_sum(my_core, by_core, received, name):
    n = len(by_core)

    def body(core_ref, *refs):
        for a_ref, b_ref, o_ref in zip(refs[:n], refs[n:2 * n], refs[2 * n:]):
            o_ref[0] = (a_ref[0, 0].astype(F32) + b_ref[0].astype(F32)).astype(BF16)

    mine = [pl.BlockSpec((1, 1) + a.shape[2:], lambda j, core: (j, core[0], 0, 0)) for a in by_core]
    theirs = [pl.BlockSpec((1,) + r.shape[1:], lambda j, core: (j, 0, 0)) for r in received]
    return pl.pallas_call(
        body, name=name, out_shape=tuple(jax.ShapeDtypeStruct(r.shape, BF16) for r in received),
        grid_spec=pltpu.PrefetchScalarGridSpec(num_scalar_prefetch=1, grid=(by_core[0].shape[0],), in_specs=mine + theirs,
                                               out_specs=tuple(theirs)),
        compiler_params=_params("arbitrary"),
    )(my_core, *by_core, *received)


def _adamw_math(w, g, m, v):
    m = ADAM_B1 * m + (1.0 - ADAM_B1) * g
    v = ADAM_B2 * v + (1.0 - ADAM_B2) * (g * g)
    m_hat = m / (1.0 - ADAM_B1 ** ADAM_STEP)
    v_hat = v / (1.0 - ADAM_B2 ** ADAM_STEP)
    delta = -ADAM_LR * (m_hat / (jnp.sqrt(v_hat) + ADAM_EPS) + ADAM_WD * w)
    return delta, m, v


def adamw(my_chip, ws, gs, ms, vs, name):
    n = len(ws)
    flat, widths = [], []
    for g in gs:
        parts = list(g) if isinstance(g, (tuple, list)) else [g]
        flat += parts
        widths.append(len(parts))

    def body(chip_ref, *refs):
        w_refs, refs = refs[:n], refs[n:]
        g_refs, refs = refs[:len(flat)], refs[len(flat):]
        m_refs, v_refs, outs = refs[:n], refs[n:2 * n], refs[2 * n:]
        at = 0
        for t in range(n):
            if widths[t] == 1:
                g = g_refs[at][...]
            else:
                g = g_refs[at][0].astype(F32)
                for k in range(3):
                    g = g + g_refs[at + 1][k].astype(F32)
            at += widths[t]
            outs[4 * t][...] = g
            outs[4 * t + 1][...], outs[4 * t + 2][...], outs[4 * t + 3][...] = _adamw_math(
                w_refs[t][...], g, m_refs[t][...], v_refs[t][...])

    def whole(a):
        zeros = (0,) * a.ndim
        return pl.BlockSpec(a.shape, lambda i, chip: zeros, pipeline_mode=pl.Buffered(1))

    g_specs = []
    for g in gs:
        if isinstance(g, (tuple, list)):
            g_specs += [pl.BlockSpec((1,) + g[0].shape[1:], lambda i, chip: (chip[0], 0, 0), pipeline_mode=pl.Buffered(1)),
                        whole(g[1])]
        else:
            g_specs.append(whole(g))
    shapes, out_specs = [], []
    for w in ws:
        shapes += [jax.ShapeDtypeStruct(w.shape, F32)] * 4
        out_specs += [whole(w)] * 4
    res = pl.pallas_call(
        body, name=name, out_shape=tuple(shapes),
        grid_spec=pltpu.PrefetchScalarGridSpec(
            num_scalar_prefetch=1, grid=(1,), in_specs=[whole(w) for w in ws] + g_specs + [whole(a) for a in ms + vs],
            out_specs=tuple(out_specs)),
        compiler_params=_params("arbitrary"),
    )(my_chip, *ws, *flat, *ms, *vs)
    return [tuple(res[4 * t:4 * t + 4]) for t in range(n)]


def adamw_replicated(w, partials, m, v, name):
    def body(w_ref, p_ref, m_ref, v_ref, g_ref, d_ref, mo_ref, vo_ref):
        g = p_ref[0]
        for k in range(1, N_DEV):
            g = g + p_ref[k]
        g_ref[...] = g
        d_ref[...], mo_ref[...], vo_ref[...] = _adamw_math(w_ref[...], g, m_ref[...], v_ref[...])

    shape = jax.ShapeDtypeStruct(w.shape, F32)
    return pl.pallas_call(body, name=name, out_shape=(shape,) * 4, compiler_params=_params())(w, partials, m, v)


PACK_COLS = 1024
PACK_ROW_ALIGN = 16

REPLICATED = ("ffn1_norm", "mix_norm", "conv_dw_b", "conv_ln_g", "conv_ln_b", "ffn2_norm", "final_norm", "gate_b", "attn_sinks")
WEIGHT_ORDER = ("ffn1_norm", "ffn1_w_gate", "ffn1_w_up", "ffn1_w_down", "mix_norm", "w_in", "conv_dw_w", "conv_dw_b", "conv_ln_g",
                "conv_ln_b", "conv_w_proj", "attn_sinks", "attn_w_o", "gate_b", "w_out", "ffn2_norm", "ffn2_w_gate", "ffn2_w_up",
                "ffn2_w_down", "final_norm")


def _to_rows(flat, lead):
    n = flat.shape[-1]
    rows = -(-n // PACK_COLS)
    flat = jnp.pad(flat, [(0, 0)] * lead + [(0, rows * PACK_COLS - n)])
    return flat.reshape(flat.shape[:lead] + (rows, PACK_COLS))


def _pad_rows(a, axis):
    rows = a.shape[axis]
    pad = -rows % PACK_ROW_ALIGN
    widths = [(0, 0)] * a.ndim
    widths[axis] = (0, pad)
    return jnp.pad(a, widths)


def _from_rows(rows, shape):
    n = 1
    for s in shape:
        n *= s
    return rows.reshape(rows.shape[:-2] + (-1,))[..., :n].reshape(rows.shape[:-2] + tuple(shape))


def _heads_slot_major(rows):
    group = rows.shape[0] // (N_KV_HEADS * HEAD_DIM)
    return rows.reshape(N_KV_HEADS, group, HEAD_DIM, rows.shape[1]).transpose(1, 0, 2, 3).reshape(rows.shape)


def _heads_kv_major(rows):
    group = rows.shape[0] // (N_KV_HEADS * HEAD_DIM)
    return rows.reshape(group, N_KV_HEADS, HEAD_DIM, rows.shape[1]).transpose(1, 0, 2, 3).reshape(rows.shape)


def _by_core(full_rows):
    return full_rows.reshape((N_DEV // 2, 2, full_rows.shape[0] // N_DEV, full_rows.shape[1]))


def kernel(x, positions, ffn1_norm, ffn1_w_gate, ffn1_w_up, ffn1_w_down, mix_norm, w_in, conv_dw_w, conv_dw_b, conv_ln_g, conv_ln_b, conv_w_proj, attn_sinks, attn_w_o, gate_b, w_out, ffn2_norm, ffn2_w_gate, ffn2_w_up, ffn2_w_down, final_norm, loss_target, m_ffn1_norm, m_ffn1_w_gate, m_ffn1_w_up, m_ffn1_w_down, m_mix_norm, m_w_in, m_conv_dw_w, m_conv_dw_b, m_conv_ln_g, m_conv_ln_b, m_conv_w_proj, m_attn_sinks, m_attn_w_o, m_gate_b, m_w_out, m_ffn2_norm, m_ffn2_w_gate, m_ffn2_w_up, m_ffn2_w_down, m_final_norm, v_ffn1_norm, v_ffn1_w_gate, v_ffn1_w_up, v_ffn1_w_down, v_mix_norm, v_w_in, v_conv_dw_w, v_conv_dw_b, v_conv_ln_g, v_conv_ln_b, v_conv_w_proj, v_attn_sinks, v_attn_w_o, v_gate_b, v_w_out, v_ffn2_norm, v_ffn2_w_gate, v_ffn2_w_up, v_ffn2_w_down, v_final_norm):
    given = dict(locals())
    shapes = {n: given[n].shape for n in WEIGHT_ORDER}
    w = {n: given[n].reshape(given[n].shape[-2:]) if given[n].ndim == 3 else given[n].reshape(1, -1) for n in WEIGHT_ORDER}
    m = {n: given["m_" + n].reshape(w[n].shape) for n in WEIGHT_ORDER}
    v = {n: given["v_" + n].reshape(w[n].shape) for n in WEIGHT_ORDER}
    my_x, my_y, my_c = _place()
    my_core = my_c.astype(jnp.int32).reshape(1)
    my_chip = (2 * my_x + my_y).astype(jnp.int32).reshape(1)
    xs, target = x[0], loss_target[0]
    T, D = xs.shape
    KV = N_KV_HEADS * HEAD_DIM
    K = w["conv_dw_w"].shape[0]

    def t16(n):
        return w[n].T.astype(BF16)

    def r16(n):
        return w[n].astype(BF16)

    blocks1 = [t16("ffn1_w_gate"), t16("ffn1_w_up"), r16("ffn1_w_down")]
    dw_bits = _pad_rows(_to_rows(lax.bitcast_convert_type(w["conv_dw_w"], BF16).reshape(-1), 0), 0)
    blocks2 = [t16("w_in"), r16("conv_w_proj"), r16("attn_w_o"), r16("w_out"), dw_bits]
    blocks3 = [t16("ffn2_w_gate"), t16("ffn2_w_up"), r16("ffn2_w_down")]
    cos, sin = rope_tables(positions[0])
    sink_col = jnp.repeat(w["attn_sinks"].reshape(-1), WINDOW).reshape(N_KV_HEADS, (D // KV) * WINDOW, 1)

    def full(gathered):
        return gathered.reshape(-1, gathered.shape[2])

    wgt1, wut1, wd1 = (full(g) for g in all_gather(blocks1, "gather_ffn1"))
    (x1, h1, a1, b1, s1), gath2 = ffn_fwd(xs, w["ffn1_norm"], wgt1, wut1, wd1, "ffn1_fwd", carry=gather_whole(blocks2))
    wint, wcp, wo, wout = (full(g) for g in gath2[:4])
    wq = _heads_slot_major(wint[2 * D:3 * D])
    wo = _heads_slot_major(wo)
    dw_full = lax.bitcast_convert_type(_from_rows(gath2[4], w["conv_dw_w"].shape + (2,)), F32)
    dw_full = dw_full.transpose(1, 0, 2).reshape(K, D)
    (h2, ga, gb, u0, q, sgc, sgt, kk, vv), gath3 = mix_in_fwd(x1, w["mix_norm"], wint, wq, w["gate_b"], cos, sin,
                                                              "mix_in_fwd", carry=gather_send(blocks3))
    o, gath3 = attn_fwd(q, kk, vv, sink_col, "attn_fwd", carry=gather_forward(gath3))
    x2, u1, co, ao, merged = mix_out_fwd(x1, u0, o, sgc, sgt, dw_full, w["conv_dw_b"], w["conv_ln_g"], w["conv_ln_b"],
                                         wcp, wo, wout, "mix_out_fwd")
    wgt2, wut2, wd2 = (full(g) for g in gath3)
    loss, dx3, d_final, h3, a2, b2, s2 = ffn_fwd_loss(x2, w["ffn2_norm"], wgt2, wut2, wd2, w["final_norm"], target,
                                                      "ffn2_fwd_loss")

    small = {"final_norm": d_final}
    (da2, db2), _ = ffn_bwd_hidden(dx3, a2, b2, wd2, "ffn2_bwd_hidden")
    (dx2, small["ffn2_norm"]), _ = ffn_bwd_input(dx3, x2, w["ffn2_norm"], da2, db2, wgt2, wut2, "ffn2_bwd_input")
    core2 = [_by_core(g) for g in (wgrad(da2, h3, "ffn2_dwg")[0], wgrad(db2, h3, "ffn2_dwu")[0],
                                   wgrad(s2, dx3, "ffn2_dwd", b_scale=FFN_SCALE)[0])]
    (dgc, dgt, do, du1, dco, dao, u3, sums), recv2 = mix_out_bwd(dx2, u1, co, ao, sgc, sgt, w["conv_ln_g"], w["conv_ln_b"],
                                                                 wcp, wo, wout, "mix_out_bwd", carry=swap_halves(core2))
    chip2 = pair_sum(my_core, core2, recv2, "ffn2_grads_pair_sum")
    small["gate_b"] = jnp.concatenate([sums[0:1], sums[1:2]], axis=1)
    small["conv_ln_g"], small["conv_ln_b"], small["conv_dw_b"] = sums[2:3], sums[3:4], sums[4:5]
    g_wout = wgrad(merged, dx2, "dw_out")[0]
    g_wcp = wgrad(u3, dco, "dw_conv_proj")[0]
    g_wo = _heads_kv_major(wgrad(o, dao, "dw_attn_o")[0])
    (dga, dgb, g_dw), got2 = conv_bwd(du1, u0, ga, gb, dw_full, "conv_bwd", carry=exchange_between_chips(chip2))
    dq, dk, dv, dsink = attn_bwd(q, kk, vv, do, sink_col, cos, sin, "attn_bwd")
    small["attn_sinks"] = dsink[:D // KV, :N_KV_HEADS].T.reshape(1, -1)
    pieces = [dga, dgb, dq, dk, dv, dgc, dgt]
    dx1, small["mix_norm"] = mix_in_bwd(dx2, x1, w["mix_norm"], wint, wq, pieces, "mix_in_bwd")
    group = D // KV
    q_moves = [(2 * D + HEAD_DIM * (group * g + hh), 2 * D + HEAD_DIM * (N_KV_HEADS * hh + g), HEAD_DIM)
               for g in range(N_KV_HEADS) for hh in range(group)]
    g_wint = wgrad_stacked(pieces[:3], h2, "dw_in_a", wint.shape[0], 0, moves=[(0, 0, 2 * D)] + q_moves)
    g_wint = wgrad_stacked(pieces[3:], h2, "dw_in_b", wint.shape[0], 3 * D, into=g_wint)
    corem = [_by_core(a) for a in (g_wint, g_wcp, g_wo, g_wout)]

    (da1, db1), recvm = ffn_bwd_hidden(dx1, a1, b1, wd1, "ffn1_bwd_hidden", carry=swap_halves(corem))
    chipm = pair_sum(my_core, corem, recvm, "mix_grads_pair_sum")
    g1c, gotm_a = wgrad(s1, dx1, "ffn1_dwd", carry=exchange_between_chips(chipm[:1]), b_scale=FFN_SCALE)
    g1a, gotm_b = wgrad(da1, h1, "ffn1_dwg", carry=exchange_between_chips(chipm[1:]))
    g1b = wgrad(db1, h1, "ffn1_dwu")[0]
    core1 = [_by_core(a) for a in (g1a, g1b, g1c)]
    chip1 = pair_sum(my_core, core1, run_exchange(swap_halves(core1), "ffn1_grads_swap"), "ffn1_grads_pair_sum")
    (grad_x, small["ffn1_norm"]), got1 = ffn_bwd_input(dx1, xs, w["ffn1_norm"], da1, db1, wgt1, wut1, "ffn1_bwd_input",
                                                       carry=exchange_between_chips(chip1))

    gotm = gotm_a + gotm_b
    grad_src = {"ffn1_w_gate": (chip1[0], got1[0]), "ffn1_w_up": (chip1[1], got1[1]), "ffn1_w_down": (chip1[2], got1[2]),
                "ffn2_w_gate": (chip2[0], got2[0]), "ffn2_w_up": (chip2[1], got2[1]), "ffn2_w_down": (chip2[2], got2[2]),
                "w_in": (chipm[0], gotm[0]), "conv_w_proj": (chipm[1], gotm[1]), "attn_w_o": (chipm[2], gotm[2]),
                "w_out": (chipm[3], gotm[3])}
    grads = {}

    def pack_small(d, taps, extra):
        rows = [_to_rows(d[n].reshape(-1), 0) for n in REPLICATED] + [taps, _to_rows(extra.reshape(-1), 0)]
        return _pad_rows(jnp.concatenate(rows, axis=0), 0)

    zero, no_taps = jnp.zeros((1, LANES), F32), jnp.zeros((K, D), F32)
    (shares,) = all_gather([pack_small(small, g_dw, loss)], "gather_small_grads")
    g_s, d_s, m_s, v_s = adamw_replicated(pack_small(w, no_taps, zero), shares, pack_small(m, no_taps, zero),
                                          pack_small(v, no_taps, zero), "adamw_replicated")
    delta, new_m, new_v = {}, {}, {}
    off = 0
    for n in REPLICATED:
        r = -(-w[n].shape[1] // PACK_COLS)
        grads[n], delta[n], new_m[n], new_v[n] = (_from_rows(a[off:off + r], w[n].shape) for a in (g_s, d_s, m_s, v_s))
        off += r
    shard_cols = w["conv_dw_w"].shape[1]
    grad_src["conv_dw_w"] = lax.dynamic_slice_in_dim(g_s[off:off + K], _dev_index(my_x, my_y, my_c) * shard_cols, shard_cols,
                                                    axis=1)
    total_loss = g_s[off + K, 0]

    def like(a, ref):
        return a if a.shape == ref.shape else a.T

    groups = (("ffn1_w_gate", "ffn1_w_up", "ffn1_w_down"), ("ffn2_w_gate", "ffn2_w_up", "ffn2_w_down"),
              ("w_in", "conv_dw_w", "conv_w_proj", "attn_w_o", "w_out"))
    for k, names in enumerate(groups):
        refs = [grad_src[n][0][0] if isinstance(grad_src[n], tuple) else grad_src[n] for n in names]
        res = adamw(my_chip, [like(w[n], r) for n, r in zip(names, refs)], [grad_src[n] for n in names],
                    [like(m[n], r) for n, r in zip(names, refs)], [like(v[n], r) for n, r in zip(names, refs)], "adamw_%d" % k)
        for n, outs in zip(names, res):
            grads[n], delta[n], new_m[n], new_v[n] = (like(a, w[n]) for a in outs)

    out = [total_loss, grad_x[None]]
    for d in (grads, delta, new_m, new_v):
        out += [d[n].reshape(shapes[n]) for n in WEIGHT_ORDER]
    return tuple(out)
```

```python
import functools
from typing import Callable, NamedTuple

import jax
import jax.numpy as jnp
from jax import lax
from jax.experimental import pallas as pl
from jax.experimental.pallas import tpu as pltpu

F32, BF16 = jnp.float32, jnp.bfloat16

HEAD_DIM = 64
N_KV_HEADS = 4
WINDOW = 128
CONV_WIDTH = 31
ROPE_THETA = 10000.0
EPS = 1e-6
LN_EPS = 1e-5
NEG_INF = -1e30
ADAM_LR, ADAM_B1, ADAM_B2, ADAM_EPS, ADAM_WD, ADAM_STEP = 0.001, 0.9, 0.999, 1e-08, 0.01, 10

N_DEV = 8
LANES = 128
SUBLANES = 8
CONV_HALO = 32
CONV_ROWS, CONV_LANES = 64, 256
ROW_TILE = 512
FFN_CHUNK = 256
FFN_SCALE = 0.5
WGRAD_TILE_ELEMS = 2 ** 22
WGRAD_TILE_ROWS = 2048
WGRAD_VMEM = 40 * 2 ** 20
VMEM_LIMIT = 56 * 2 ** 20
MESH = pl.DeviceIdType.MESH


def _params(*sem):
    return pltpu.CompilerParams(dimension_semantics=sem or None, vmem_limit_bytes=VMEM_LIMIT)


def _resident(shape):
    zeros = (0,) * len(shape)
    return pl.BlockSpec(shape, lambda *_: zeros, pipeline_mode=pl.Buffered(1))


def _rows(tm, n):
    return pl.BlockSpec((tm, n), lambda i: (i, 0))


def _acc_spec(shape):
    zeros = (0,) * len(shape)
    return pl.BlockSpec(shape, lambda *_: zeros)


_ANY = pl.BlockSpec(memory_space=pl.ANY)


class Carry(NamedTuple):
    ins: tuple
    out_shapes: tuple
    aliases: dict
    sems: tuple
    start: Callable
    finish: Callable
    relay: Callable = None


def _call(body, *, name, grid, in_specs, out_specs, out_shape, args, scratch_shapes=(), carry=None):
    n_in, n_out, n_scr = len(in_specs), len(out_specs), len(scratch_shapes)
    params = _params(*(("arbitrary",) * len(grid)))
    if carry is None:
        res = pl.pallas_call(body, name=name, grid=grid, in_specs=list(in_specs), out_specs=tuple(out_specs),
                             out_shape=tuple(out_shape), scratch_shapes=list(scratch_shapes), compiler_params=params)(*args)
        return tuple(res), ()
    c_in, c_out = len(carry.ins), len(carry.out_shapes)

    def wrapped(*refs):
        ins, c_ins = refs[:n_in], refs[n_in:n_in + c_in]
        p = n_in + c_in
        outs, c_outs = refs[p:p + n_out], refs[p + n_out:p + n_out + c_out]
        p += n_out + c_out
        scr, c_sems = refs[p:p + n_scr], refs[p + n_scr:]
        ids = [pl.program_id(d) for d in range(len(grid))]
        first = functools.reduce(jnp.logical_and, [i == 0 for i in ids])
        last = functools.reduce(jnp.logical_and, [i == n - 1 for i, n in zip(ids, grid)])

        @pl.when(first)
        def _():
            carry.start(c_ins, c_outs, c_sems)

        body(*ins, *outs, *scr)

        if carry.relay is not None:
            @pl.when(ids[0] == (3 * grid[0]) // 4)
            def _():
                carry.relay(c_ins, c_outs, c_sems)

        @pl.when(last)
        def _():
            carry.finish(c_ins, c_outs, c_sems)

    res = pl.pallas_call(
        wrapped, name=name, grid=grid, in_specs=list(in_specs) + [_ANY] * c_in, out_specs=tuple(out_specs) + (_ANY,) * c_out,
        out_shape=tuple(out_shape) + tuple(carry.out_shapes), scratch_shapes=list(scratch_shapes) + list(carry.sems),
        input_output_aliases={n_in + i: n_out + o for i, o in carry.aliases.items()}, compiler_params=params,
    )(*args, *carry.ins)
    return tuple(res[:n_out]), tuple(res[n_out:])


def _nt(a, b):
    return lax.dot_general(a, b, (((1,), (1,)), ((), ())), preferred_element_type=F32)


def _tn(a, b):
    return lax.dot_general(a, b, (((0,), (0,)), ((), ())), preferred_element_type=F32)


def _dot(a, b):
    return jnp.dot(a, b, preferred_element_type=F32)


def _sigmoid(x):
    return 1.0 / (1.0 + jnp.exp(-x))


def _rms_fwd(x, g):
    r = lax.rsqrt(jnp.mean(x * x, axis=-1, keepdims=True) + EPS)
    return (x * r) * g


def _rms_bwd(x, g, dy):
    r = lax.rsqrt(jnp.mean(x * x, axis=-1, keepdims=True) + EPS)
    xhat = x * r
    dyg = dy * g
    dx = r * (dyg - xhat * jnp.mean(dyg * xhat, axis=-1, keepdims=True))
    return dx, dy * xhat


def _rot_half(x):
    lane = lax.broadcasted_iota(jnp.int32, (x.shape[0], LANES), 1)
    first = (lane % HEAD_DIM) < (HEAD_DIM // 2)
    out = []
    for s in range(x.shape[1] // LANES):
        xs = x[:, LANES * s:LANES * (s + 1)]
        out.append(jnp.where(first, pltpu.roll(xs, LANES - HEAD_DIM // 2, 1), pltpu.roll(xs, HEAD_DIM // 2, 1)))
    return out[0] if len(out) == 1 else jnp.concatenate(out, axis=1)


def _tile_lanes(t, width):
    return t if width == LANES else jnp.concatenate([t] * (width // LANES), axis=1)


def _rope_fwd(x, cos, sin_signed):
    w = x.shape[1]
    return x * _tile_lanes(cos, w) + _rot_half(x) * _tile_lanes(sin_signed, w)


def _rope_bwd(dy, cos, sin_signed):
    w = dy.shape[1]
    return dy * _tile_lanes(cos, w) + _rot_half(dy * _tile_lanes(sin_signed, w))


def _ffn_rows(x, g_ref, wg_ref, wu_ref, wd_ref, h_ref, a_ref, b_ref, s_ref, acc_ref):
    F = wg_ref.shape[0]
    h = _rms_fwd(x, g_ref[...]).astype(BF16)
    h_ref[...] = h
    for c in range(F // FFN_CHUNK):
        cs = pl.ds(c * FFN_CHUNK, FFN_CHUNK)
        a = _nt(h, wg_ref[cs, :])
        b = _nt(h, wu_ref[cs, :])
        a_ref[:, cs] = a.astype(BF16)
        b_ref[:, cs] = b.astype(BF16)
        s = (a * _sigmoid(a) * b).astype(BF16)
        s_ref[:, cs] = s
        y = _dot(s, wd_ref[cs, :])
        if c == 0:
            acc_ref[...] = y
        else:
            acc_ref[...] += y
    return x + FFN_SCALE * acc_ref[...]


def ffn_fwd(x, gain, wgt, wut, wd, name, carry=None):
    T, D = x.shape
    F = wgt.shape[0]
    tm = min(ROW_TILE, T)

    def body(x_ref, g_ref, wg_ref, wu_ref, wd_ref, xo_ref, h_ref, a_ref, b_ref, s_ref, acc_ref):
        xo_ref[...] = _ffn_rows(x_ref[...], g_ref, wg_ref, wu_ref, wd_ref, h_ref, a_ref, b_ref, s_ref, acc_ref)

    wide = jax.ShapeDtypeStruct((T, F), BF16)
    return _call(
        body, name=name, grid=(T // tm,),
        out_shape=(jax.ShapeDtypeStruct((T, D), F32), jax.ShapeDtypeStruct((T, D), BF16), wide, wide, wide),
        in_specs=[_rows(tm, D), _resident((1, D)), _resident((F, D)), _resident((F, D)), _resident((F, D))],
        out_specs=(_rows(tm, D), _rows(tm, D), _rows(tm, F), _rows(tm, F), _rows(tm, F)),
        scratch_shapes=[pltpu.VMEM((tm, D), F32)], args=(x, gain, wgt, wut, wd), carry=carry)


def ffn_up(x, gain, wgt, wut, name, carry=None):
    T, D = x.shape
    F = wgt.shape[0]
    tm = min(ROW_TILE, T)

    def body(x_ref, g_ref, wg_ref, wu_ref, h_ref, a_ref, b_ref, s_ref):
        h = _rms_fwd(x_ref[...], g_ref[...]).astype(BF16)
        h_ref[...] = h
        for c in range(F // FFN_CHUNK):
            cs = pl.ds(c * FFN_CHUNK, FFN_CHUNK)
            a = _nt(h, wg_ref[cs, :])
            b = _nt(h, wu_ref[cs, :])
            a_ref[:, cs] = a.astype(BF16)
            b_ref[:, cs] = b.astype(BF16)
            s_ref[:, cs] = (a * _sigmoid(a) * b).astype(BF16)

    wide = jax.ShapeDtypeStruct((T, F), BF16)
    return _call(
        body, name=name, grid=(T // tm,), out_shape=(jax.ShapeDtypeStruct((T, D), BF16), wide, wide, wide),
        in_specs=[_rows(tm, D), _resident((1, D)), _resident((F, D)), _resident((F, D))],
        out_specs=(_rows(tm, D), _rows(tm, F), _rows(tm, F), _rows(tm, F)), args=(x, gain, wgt, wut), carry=carry)


def ffn_down(x, s, wd, name, carry=None):
    T, D = x.shape
    F = wd.shape[0]
    tm = min(ROW_TILE, T)

    def body(x_ref, s_ref, wd_ref, xo_ref):
        xo_ref[...] = x_ref[...] + FFN_SCALE * _dot(s_ref[...], wd_ref[...])

    return _call(
        body, name=name, grid=(T // tm,), out_shape=(jax.ShapeDtypeStruct((T, D), F32),),
        in_specs=[_rows(tm, D), _rows(tm, F), _resident((F, D))], out_specs=(_rows(tm, D),), args=(x, s, wd), carry=carry)


def ffn_fwd_loss(x, gain, wgt, wut, wd, final_gain, target, name):
    T, D = x.shape
    F = wgt.shape[0]
    tm = min(ROW_TILE, T)

    def body(x_ref, g_ref, wg_ref, wu_ref, wd_ref, gf_ref, t_ref,
             loss_ref, dx_ref, dg_ref, h_ref, a_ref, b_ref, s_ref, acc_ref):
        xo = _ffn_rows(x_ref[...], g_ref, wg_ref, wu_ref, wd_ref, h_ref, a_ref, b_ref, s_ref, acc_ref)
        gf = gf_ref[...]
        err = _rms_fwd(xo, gf) - t_ref[...]
        dx, dgt = _rms_bwd(xo, gf, err * (1.0 / D))
        dx_ref[...] = dx

        @pl.when(pl.program_id(0) == 0)
        def _():
            dg_ref[...] = jnp.zeros_like(dg_ref)
            loss_ref[...] = jnp.zeros_like(loss_ref)
        dg_ref[...] += jnp.sum(dgt, axis=0, keepdims=True)
        per_token = jnp.sum(err * err, axis=-1, keepdims=True) * (0.5 / D)
        loss_ref[...] += jnp.broadcast_to(jnp.sum(per_token, axis=0, keepdims=True), (1, LANES))

    return pl.pallas_call(
        body, name=name, grid=(T // tm,),
        out_shape=(jax.ShapeDtypeStruct((1, LANES), F32), jax.ShapeDtypeStruct((T, D), F32), jax.ShapeDtypeStruct((1, D), F32),
                   jax.ShapeDtypeStruct((T, D), BF16), jax.ShapeDtypeStruct((T, F), BF16), jax.ShapeDtypeStruct((T, F), BF16),
                   jax.ShapeDtypeStruct((T, F), BF16)),
        in_specs=[_rows(tm, D), _resident((1, D)), _resident((F, D)), _resident((F, D)), _resident((F, D)), _resident((1, D)),
                  _rows(tm, D)],
        out_specs=(_acc_spec((1, LANES)), _rows(tm, D), _acc_spec((1, D)), _rows(tm, D), _rows(tm, F), _rows(tm, F),
                   _rows(tm, F)),
        scratch_shapes=[pltpu.VMEM((tm, D), F32)], compiler_params=_params("arbitrary"),
    )(x, gain, wgt, wut, wd, final_gain, target)


def ffn_bwd_hidden(dxo, a, b, wd, name, carry=None):
    T, D = dxo.shape
    F = wd.shape[0]
    tm = min(ROW_TILE, T)
    fc = FFN_CHUNK

    def hidden_body(dxo_ref, a_ref, b_ref, wd_ref, da_ref, db_ref):
        g0 = (FFN_SCALE * dxo_ref[...]).astype(BF16)
        for c in range(F // fc):
            cs = pl.ds(c * fc, fc)
            ds = _nt(g0, wd_ref[cs, :])
            a = a_ref[:, cs].astype(F32)
            bb = b_ref[:, cs].astype(F32)
            sa = _sigmoid(a)
            da_ref[:, cs] = (ds * bb * (sa * (1.0 + a * (1.0 - sa)))).astype(BF16)
            db_ref[:, cs] = (ds * (a * sa)).astype(BF16)

    wide = jax.ShapeDtypeStruct((T, F), BF16)
    return _call(
        hidden_body, name=name, grid=(T // tm,), out_shape=(wide, wide),
        in_specs=[_rows(tm, D), _rows(tm, F), _rows(tm, F), _resident((F, D))],
        out_specs=(_rows(tm, F), _rows(tm, F)), args=(dxo, a, b, wd), carry=carry)


def ffn_bwd_input(dxo, x, gain, da, db, wg, wu, name, carry=None):
    T, D = x.shape
    F = wg.shape[0]
    tm = min(ROW_TILE, T)

    def input_body(dxo_ref, x_ref, g_ref, da_ref, db_ref, wg_ref, wu_ref, dx_ref, dg_ref):
        dh = _dot(da_ref[...], wg_ref[...]) + _dot(db_ref[...], wu_ref[...])
        dx, dgt = _rms_bwd(x_ref[...], g_ref[...], dh)
        dx_ref[...] = dxo_ref[...] + dx

        @pl.when(pl.program_id(0) == 0)
        def _():
            dg_ref[...] = jnp.zeros_like(dg_ref)
        dg_ref[...] += jnp.sum(dgt, axis=0, keepdims=True)

    return _call(
        input_body, name=name, grid=(T // tm,),
        out_shape=(jax.ShapeDtypeStruct((T, D), F32), jax.ShapeDtypeStruct((1, D), F32)),
        in_specs=[_rows(tm, D), _rows(tm, D), _resident((1, D)), _rows(tm, F), _rows(tm, F), _resident((F, D)),
                  _resident((F, D))],
        out_specs=(_rows(tm, D), _acc_spec((1, D))), args=(dxo, x, gain, da, db, wg, wu), carry=carry)


def wgrad(a, b, name, carry=None, b_scale=None):
    T, M = a.shape
    N = b.shape[1]
    fixed = M * N * (4 + 2)
    per_row = 2 * (M * a.dtype.itemsize + N * b.dtype.itemsize)
    tk = ROW_TILE
    while fixed + 2 * tk * per_row <= WGRAD_VMEM and 2 * tk <= WGRAD_TILE_ROWS:
        tk *= 2
    tk = min(tk, T)
    nk = T // tk

    def body(a_ref, b_ref, o_ref, acc_ref):
        k = pl.program_id(0)
        bt = b_ref[...] if b_scale is None else b_scale * b_ref[...]
        part = _tn(a_ref[...].astype(BF16), bt.astype(BF16))

        @pl.when(k == 0)
        def _():
            acc_ref[...] = part

        @pl.when(k > 0)
        def _():
            acc_ref[...] += part

        @pl.when(k == nk - 1)
        def _():
            o_ref[...] = acc_ref[...].astype(BF16)

    (out,), carried = _call(
        body, name=name, grid=(nk,), out_shape=(jax.ShapeDtypeStruct((M, N), BF16),),
        in_specs=[_rows(tk, M), _rows(tk, N)], out_specs=(_resident((M, N)),),
        scratch_shapes=[pltpu.VMEM((M, N), F32)], args=(a, b), carry=carry)
    return out, carried


def wgrad_stacked(pieces, b, name, total_rows, at, into=None, moves=None):
    T, N = b.shape
    n = len(pieces)
    widths = [p.shape[1] for p in pieces]
    offs = [sum(widths[:i]) for i in range(n)]
    M = sum(widths)
    tk = ROW_TILE
    while 2 * tk * M <= WGRAD_TILE_ELEMS and 2 * tk <= WGRAD_TILE_ROWS:
        tk *= 2
    tk = min(tk, T)
    nk = T // tk
    moves = moves or [(0, 0, M)]
    n_in = n + 1 + (into is not None)

    def body(*refs):
        a_refs, b_ref = refs[:n], refs[n]
        o_ref, acc_ref, stage, sem = refs[n_in:]
        k = pl.program_id(0)

        @pl.when(k == 0)
        def _():
            acc_ref[...] = jnp.zeros_like(acc_ref)
        bt = b_ref[...].astype(BF16)
        for a_ref, off, width in zip(a_refs, offs, widths):
            acc_ref[pl.ds(off, width), :] += _tn(a_ref[...].astype(BF16), bt)

        @pl.when(k == nk - 1)
        def _():
            for to, start, rows in moves:
                stage[pl.ds(to, rows), :] = acc_ref[pl.ds(start, rows), :].astype(BF16)
            cp = pltpu.make_async_copy(stage, o_ref.at[pl.ds(at, M)], sem)
            cp.start()
            cp.wait()

    return pl.pallas_call(
        body, name=name, grid=(nk,), out_shape=jax.ShapeDtypeStruct((total_rows, N), BF16),
        in_specs=[_rows(tk, width) for width in widths] + [_rows(tk, N)] + [_ANY] * (into is not None), out_specs=_ANY,
        scratch_shapes=[pltpu.VMEM((M, N), F32), pltpu.VMEM((M, N), BF16), pltpu.SemaphoreType.DMA],
        input_output_aliases={n + 1: 0} if into is not None else {}, compiler_params=_params("arbitrary"),
    )(*pieces, b, *([into] if into is not None else []))


def _w_in_rows(D):
    KV = N_KV_HEADS * HEAD_DIM
    return 0, D, 2 * D, 3 * D, 3 * D + KV, 3 * D + 2 * KV, 4 * D + 2 * KV


def mix_in_fwd(x, gain, wint, wq, gate_b, cos, sin_signed, name, carry=None):
    T, D = x.shape
    KV = N_KV_HEADS * HEAD_DIM
    tm = min(ROW_TILE, T)
    o_ga, o_gb, _, o_k, o_v, o_gc, o_gt = _w_in_rows(D)

    def body(x_ref, g_ref, w_ref, wq_ref, gb_ref, cos_ref, sin_ref,
             h_ref, ga_ref, gb_out_ref, u0_ref, q_ref, sgc_ref, sgt_ref, k_ref, v_ref):
        h = _rms_fwd(x_ref[...], g_ref[...]).astype(BF16)
        h_ref[...] = h
        cos, sin = cos_ref[...], sin_ref[...]
        ga = _nt(h, w_ref[pl.ds(o_ga, D), :])
        gb = _nt(h, w_ref[pl.ds(o_gb, D), :])
        ga_ref[...] = ga.astype(BF16)
        gb_out_ref[...] = gb.astype(BF16)
        u0_ref[...] = (ga * _sigmoid(gb)).astype(BF16)
        q = _nt(h, wq_ref[...])
        q_ref[...] = _rope_fwd(q, cos, sin).astype(BF16)
        gc = _nt(h, w_ref[pl.ds(o_gc, D), :]) + gb_ref[:, pl.ds(0, D)]
        sgc_ref[...] = _sigmoid(gc).astype(BF16)
        gt = _nt(h, w_ref[pl.ds(o_gt, D), :]) + gb_ref[:, pl.ds(D, D)]
        sgt_ref[...] = _sigmoid(gt).astype(BF16)
        k = _nt(h, w_ref[pl.ds(o_k, KV), :])
        k_ref[...] = _rope_fwd(k, cos, sin).astype(BF16)
        v_ref[...] = _nt(h, w_ref[pl.ds(o_v, KV), :]).astype(BF16)

    big = jax.ShapeDtypeStruct((T, D), BF16)
    small = jax.ShapeDtypeStruct((T, KV), BF16)
    return _call(
        body, name=name, grid=(T // tm,),
        out_shape=(big, big, big, big, big, big, big, small, small),
        in_specs=[_rows(tm, D), _resident((1, D)), _resident(wint.shape), _resident(wq.shape), _resident((1, 2 * D)),
                  _rows(tm, LANES), _rows(tm, LANES)],
        out_specs=(_rows(tm, D),) * 7 + (_rows(tm, KV),) * 2,
        args=(x, gain, wint, wq, gate_b, cos, sin_signed), carry=carry)


def _from_prev(rows):
    qi = lax.broadcasted_iota(jnp.int32, (rows, WINDOW), 0) % WINDOW
    return lax.broadcasted_iota(jnp.int32, (rows, WINDOW), 1) > qi


def _fold(x, g, from_prev):
    lo = 2 * WINDOW * g
    return jnp.where(from_prev, x[:, lo:lo + WINDOW], x[:, lo + WINDOW:lo + 2 * WINDOW])


def _unfold(folded, from_prev):
    zero = jnp.zeros_like(folded[0])
    parts = []
    for x in folded:
        parts += [jnp.where(from_prev, x, zero), jnp.where(from_prev, zero, x)]
    return jnp.concatenate(parts, axis=1)


def _kv_lane_head(rows, width):
    return lax.broadcasted_iota(jnp.int32, (rows, width), 1) // HEAD_DIM


def _block_diag(win):
    head = _kv_lane_head(*win.shape)
    zero = jnp.zeros_like(win)
    return jnp.concatenate([jnp.where(head == g, win, zero) for g in range(N_KV_HEADS)], axis=0)


def _diag_blocks_sum(bd, keys):
    head = _kv_lane_head(keys, bd.shape[1])
    out = jnp.zeros((keys, bd.shape[1]), F32)
    for g in range(N_KV_HEADS):
        out = jnp.where(head == g, bd[g * keys:(g + 1) * keys], out)
    return out


def _kv_windows(k_ref, kh_ref, v_ref, vh_ref, j):
    rows = pl.ds(j * WINDOW, WINDOW)
    if j == 0:
        kprev, vprev = kh_ref[...], vh_ref[...]
    else:
        prev = pl.ds((j - 1) * WINDOW, WINDOW)
        kprev, vprev = k_ref[prev, :], v_ref[prev, :]
    return jnp.concatenate([kprev, k_ref[rows, :]], axis=0), jnp.concatenate([vprev, v_ref[rows, :]], axis=0)


def _stack_slots(ref, j, group, KV):
    rows = pl.ds(j * WINDOW, WINDOW)
    return jnp.concatenate([ref[rows, pl.ds(KV * hh, KV)] for hh in range(group)], axis=0)


def _attn_exp(qs, kbd, sink_ref, from_prev, no_prev):
    s = _nt(qs, kbd)
    if no_prev is not None:
        qi = lax.broadcasted_iota(jnp.int32, from_prev.shape, 0) % WINDOW
        absent = lax.broadcasted_iota(jnp.int32, from_prev.shape, 1) > jnp.where(no_prev, qi, WINDOW)
    out = []
    for g in range(N_KV_HEADS):
        sg = _fold(s, g, from_prev) * (HEAD_DIM ** -0.5)
        if no_prev is not None:
            sg = jnp.where(absent, NEG_INF, sg)
        sink = sink_ref[g]
        m = jnp.maximum(jnp.max(sg, axis=-1, keepdims=True), sink)
        out.append((jnp.exp(sg - m), jnp.exp(sink - m)))
    return out


def _spread_over_heads(cols, rows, KV):
    head = _kv_lane_head(rows, KV)
    out = jnp.zeros((rows, KV), F32)
    for g, col in enumerate(cols):
        out = jnp.where(head == g, col, out)
    return out


def _halo_rows_spec(tq, width, sub):
    return pl.BlockSpec((sub, width), lambda i: (jnp.maximum(i * (tq // sub) - 1, 0), 0))


def attn_fwd(q, k, v, sink_col, name, carry=None):
    T, D = q.shape
    KV = k.shape[1]
    group = D // KV
    tq = min(ROW_TILE, T)
    nsub = tq // WINDOW
    rows, wide = group * WINDOW, N_KV_HEADS * 2 * WINDOW

    def body(q_ref, k_ref, kh_ref, v_ref, vh_ref, sink_ref, o_ref):
        from_prev = _from_prev(rows)
        head = _kv_lane_head(wide, KV)
        block = lax.broadcasted_iota(jnp.int32, head.shape, 0) // (2 * WINDOW)
        ones_bd = jnp.where(head == block, 1.0, 0.0).astype(BF16)
        for j in range(nsub):
            k_win, v_win = _kv_windows(k_ref, kh_ref, v_ref, vh_ref, j)
            parts = _attn_exp(_stack_slots(q_ref, j, group, KV), _block_diag(k_win), sink_ref, from_prev,
                              pl.program_id(0) == 0 if j == 0 else None)
            p = _unfold([pg.astype(BF16) for pg, _ in parts], from_prev)
            both = _dot(p, jnp.concatenate([_block_diag(v_win), ones_bd], axis=1))
            denom = both[:, KV:] + _spread_over_heads([es for _, es in parts], rows, KV)
            out = (both[:, :KV] / denom).astype(BF16)
            for hh in range(group):
                o_ref[pl.ds(j * WINDOW, WINDOW), pl.ds(KV * hh, KV)] = out[hh * WINDOW:(hh + 1) * WINDOW]

    (o,), carried = _call(
        body, name=name, grid=(T // tq,),
        out_shape=(jax.ShapeDtypeStruct((T, D), BF16),),
        in_specs=[_rows(tq, D), _rows(tq, KV), _halo_rows_spec(tq, KV, WINDOW), _rows(tq, KV),
                  _halo_rows_spec(tq, KV, WINDOW), _resident(sink_col.shape)],
        out_specs=(_rows(tq, D),), args=(q, k, k, v, v, sink_col), carry=carry)
    return o, carried


def attn_bwd(q, k, v, do, sink_col, cos, sin_signed, name):
    T, D = q.shape
    KV = k.shape[1]
    group = D // KV
    tq = min(ROW_TILE, T)
    nsub = tq // WINDOW
    nt = T // tq
    scale = HEAD_DIM ** -0.5
    rows, wide = group * WINDOW, N_KV_HEADS * 2 * WINDOW

    def rev(i):
        return nt - 1 - i

    def body(q_ref, k_ref, kh_ref, v_ref, vh_ref, do_ref, sink_ref, cos_ref, sin_ref,
             dq_ref, dk_ref, dv_ref, dsink_ref, dq_acc, dk_acc, dv_acc, carry_k, carry_v):
        i = pl.program_id(0)

        @pl.when(i == 0)
        def _():
            carry_k[...] = jnp.zeros_like(carry_k)
            carry_v[...] = jnp.zeros_like(carry_v)
            dsink_ref[...] = jnp.zeros_like(dsink_ref)

        dk_acc[...] = jnp.zeros_like(dk_acc)
        dv_acc[...] = jnp.zeros_like(dv_acc)
        from_prev = _from_prev(rows)
        lane = lax.broadcasted_iota(jnp.int32, (1, LANES), 1)
        for j in range(nsub):
            k_win, v_win = _kv_windows(k_ref, kh_ref, v_ref, vh_ref, j)
            kbd, vbd = _block_diag(k_win), _block_diag(v_win)
            qs, dos = _stack_slots(q_ref, j, group, KV), _stack_slots(do_ref, j, group, KV)
            dp = _nt(dos, vbd)
            probs16, ds16 = [], []
            for g, (pg, es) in enumerate(_attn_exp(qs, kbd, sink_ref, from_prev, rev(i) == 0 if j == 0 else None)):
                inv = 1.0 / (jnp.sum(pg, axis=-1, keepdims=True) + es)
                probs = pg * inv
                dpg = _fold(dp, g, from_prev)
                delta = jnp.sum(probs * dpg, axis=-1, keepdims=True)
                probs16.append(probs.astype(BF16))
                ds16.append((probs * (dpg - delta) * scale).astype(BF16))
                dsk = -(es * inv * delta)
                for hh in range(group):
                    tot = jnp.sum(dsk[hh * WINDOW:(hh + 1) * WINDOW], axis=0, keepdims=True)
                    dsink_ref[pl.ds(hh, 1), :] += jnp.where(lane == g, tot, 0.0)
            ds = _unfold(ds16, from_prev)
            dqs = _dot(ds, kbd)
            for hh in range(group):
                dq_acc[pl.ds(j * WINDOW, WINDOW), pl.ds(KV * hh, KV)] = dqs[hh * WINDOW:(hh + 1) * WINDOW]
            keys = pl.ds(j * WINDOW, 2 * WINDOW)
            dk_acc[keys, :] += _diag_blocks_sum(_tn(ds, qs), 2 * WINDOW)
            dv_acc[keys, :] += _diag_blocks_sum(_tn(_unfold(probs16, from_prev), dos), 2 * WINDOW)

        tail = pl.ds(tq, WINDOW)
        dk_acc[tail, :] += carry_k[...]
        dv_acc[tail, :] += carry_v[...]
        carry_k[...] = dk_acc[pl.ds(0, WINDOW), :]
        carry_v[...] = dv_acc[pl.ds(0, WINDOW), :]
        cos, sin = cos_ref[...], sin_ref[...]
        dq_ref[...] = _rope_bwd(dq_acc[...], cos, sin).astype(BF16)
        dk_ref[...] = _rope_bwd(dk_acc[pl.ds(WINDOW, tq), :], cos, sin).astype(BF16)
        dv_ref[...] = dv_acc[pl.ds(WINDOW, tq), :].astype(BF16)

    def rrows(n):
        return pl.BlockSpec((tq, n), lambda i: (rev(i), 0))

    def rhalo(n):
        return pl.BlockSpec((WINDOW, n), lambda i: (jnp.maximum(rev(i) * nsub - 1, 0), 0))

    return pl.pallas_call(
        body, name=name, grid=(nt,),
        out_shape=(jax.ShapeDtypeStruct((T, D), BF16), jax.ShapeDtypeStruct((T, KV), BF16),
                   jax.ShapeDtypeStruct((T, KV), BF16), jax.ShapeDtypeStruct((SUBLANES, LANES), F32)),
        in_specs=[rrows(D), rrows(KV), rhalo(KV), rrows(KV), rhalo(KV), rrows(D), _resident(sink_col.shape),
                  rrows(LANES), rrows(LANES)],
        out_specs=(rrows(D), rrows(KV), rrows(KV), _acc_spec((SUBLANES, LANES))),
        scratch_shapes=[pltpu.VMEM((tq, D), F32), pltpu.VMEM((WINDOW + tq, KV), F32), pltpu.VMEM((WINDOW + tq, KV), F32),
                        pltpu.VMEM((WINDOW, KV), F32), pltpu.VMEM((WINDOW, KV), F32)],
        compiler_params=_params("arbitrary"),
    )(q, k, k, v, v, do, sink_col, cos, sin_signed)


def _ln_stats(u):
    mu = jnp.mean(u, axis=-1, keepdims=True)
    d = u - mu
    rstd = lax.rsqrt(jnp.mean(d * d, axis=-1, keepdims=True) + LN_EPS)
    return d * rstd, rstd


def _lag_taps(b, K):
    return [(a, K - 1 - (SUBLANES * a + b)) for a in range(-(-K // SUBLANES)) if SUBLANES * a + b <= K - 1]


def _conv_chunks(tm, D, chunk):
    def rows(c, carry):
        r0 = pl.multiple_of(c * CONV_ROWS, CONV_ROWS)
        for l0 in range(0, D, CONV_LANES):
            chunk(r0, pl.ds(l0, CONV_LANES))
        return carry
    lax.fori_loop(0, tm // CONV_ROWS, rows, 0)


def _conv_causal(buf, w_ref, bias_ref, out_ref, tm, D, K):
    def chunk(r0, lanes):
        acc = jnp.broadcast_to(bias_ref[:, lanes], (CONV_ROWS, CONV_LANES))
        for b in range(SUBLANES):
            y = None
            for a, k in _lag_taps(b, K):
                start = pl.multiple_of(r0 + CONV_HALO - SUBLANES * (a + 1), SUBLANES)
                t = buf[pl.ds(start, CONV_ROWS + SUBLANES), lanes] * w_ref[pl.ds(k, 1), lanes]
                y = t if y is None else y + t
            acc = acc + y[SUBLANES - b:SUBLANES - b + CONV_ROWS]
        out_ref[pl.ds(r0, CONV_ROWS), lanes] = acc
    _conv_chunks(tm, D, chunk)


def _conv_anticausal(dbuf, w_ref, out_ref, tm, D, K):
    def chunk(r0, lanes):
        acc = jnp.zeros((CONV_ROWS, CONV_LANES), F32)
        for b in range(SUBLANES):
            y = None
            for a, k in _lag_taps(b, K):
                start = pl.multiple_of(r0 + SUBLANES * a, SUBLANES)
                t = dbuf[pl.ds(start, CONV_ROWS + SUBLANES), lanes] * w_ref[pl.ds(k, 1), lanes]
                y = t if y is None else y + t
            acc = acc + y[b:b + CONV_ROWS]
        out_ref[pl.ds(r0, CONV_ROWS), lanes] = acc
    _conv_chunks(tm, D, chunk)


def _conv_tap_grads(dbuf, ubuf, acc_ref, tm, D, K):
    reach = SUBLANES * (-(-K // SUBLANES) - 1)

    def chunk(r0, lanes):
        d = dbuf[pl.ds(r0, CONV_ROWS), lanes]
        around = ubuf[pl.ds(pl.multiple_of(r0 + CONV_HALO - reach - SUBLANES, SUBLANES), CONV_ROWS + reach + SUBLANES), lanes]
        for b in range(SUBLANES):
            shifted = around[SUBLANES - b:SUBLANES - b + CONV_ROWS + reach]
            for a, k in _lag_taps(b, K):
                prod = d * shifted[reach - SUBLANES * a:reach - SUBLANES * a + CONV_ROWS]
                part = prod[0:SUBLANES]
                for i in range(1, CONV_ROWS // SUBLANES):
                    part = part + prod[SUBLANES * i:SUBLANES * (i + 1)]
                acc_ref[k, :, lanes] += part
    _conv_chunks(tm, D, chunk)


def mix_out_fwd(x, u0, o, sgc, sgt, dw_w, dw_b, ln_g, ln_b, wcp, wo, wout, name):
    T, D = x.shape
    tm = min(ROW_TILE, T)
    K = dw_w.shape[0]

    def body(x_ref, u_ref, uh_ref, o_ref, sgc_ref, sgt_ref, w_ref, b_ref, lg_ref, lb_ref, wcp_ref, wo_ref, wout_ref,
             x2_ref, u1_ref, co_ref, ao_ref, mg_ref, buf, conv):
        keep = (pl.program_id(0) > 0).astype(F32)
        buf[pl.ds(0, CONV_HALO), :] = uh_ref[...].astype(F32) * keep
        buf[pl.ds(CONV_HALO, tm), :] = u_ref[...].astype(F32)
        _conv_causal(buf, w_ref, b_ref, conv, tm, D, K)
        acc = conv[...]
        u1_ref[...] = acc.astype(BF16)
        xhat, _ = _ln_stats(acc)
        u2 = xhat * lg_ref[...] + lb_ref[...]
        u3 = (u2 * _sigmoid(u2)).astype(BF16)
        co = _dot(u3, wcp_ref[...])
        ao = _dot(o_ref[...], wo_ref[...])
        co_ref[...] = co.astype(BF16)
        ao_ref[...] = ao.astype(BF16)
        merged = (sgc_ref[...].astype(F32) * co + sgt_ref[...].astype(F32) * ao).astype(BF16)
        mg_ref[...] = merged
        x2_ref[...] = x_ref[...] + _dot(merged, wout_ref[...])

    big = jax.ShapeDtypeStruct((T, D), BF16)
    vec = _resident((1, D))
    return pl.pallas_call(
        body, name=name, grid=(T // tm,),
        out_shape=(jax.ShapeDtypeStruct((T, D), F32), big, big, big, big),
        in_specs=[_rows(tm, D), _rows(tm, D), _halo_rows_spec(tm, D, CONV_HALO), _rows(tm, D), _rows(tm, D), _rows(tm, D),
                  _resident((K, D)), vec, vec, vec, _resident((D, D)), _resident((D, D)), _resident((D, D))],
        out_specs=(_rows(tm, D),) * 5,
        scratch_shapes=[pltpu.VMEM((CONV_HALO + tm, D), F32), pltpu.VMEM((tm, D), F32)],
        compiler_params=_params("arbitrary"),
    )(x, u0, u0, o, sgc, sgt, dw_w, dw_b, ln_g, ln_b, wcp, wo, wout)


def mix_out_bwd(dx2, u1, co, ao, sgc, sgt, ln_g, ln_b, wcp, wo, wout, name, carry=None):
    T, D = dx2.shape
    tm = min(ROW_TILE, T)

    def body(dx_ref, u1_ref, co_ref, ao_ref, sgc_ref, sgt_ref, lg_ref, lb_ref, wcp_ref, wo_ref, wout_ref,
             dgc_ref, dgt_ref, do_ref, du1_ref, dco_ref, dao_ref, u3_ref, sums_ref):
        dm = _nt(dx_ref[...].astype(BF16), wout_ref[...])
        sgc, sgt = sgc_ref[...].astype(F32), sgt_ref[...].astype(F32)
        dco = (dm * sgc).astype(BF16)
        dao = (dm * sgt).astype(BF16)
        dgc = dm * co_ref[...].astype(F32) * sgc * (1.0 - sgc)
        dgt = dm * ao_ref[...].astype(F32) * sgt * (1.0 - sgt)
        dco_ref[...] = dco
        dao_ref[...] = dao
        dgc_ref[...] = dgc.astype(BF16)
        dgt_ref[...] = dgt.astype(BF16)
        do_ref[...] = _nt(dao, wo_ref[...]).astype(BF16)
        du3 = _nt(dco, wcp_ref[...])
        xhat, rstd = _ln_stats(u1_ref[...].astype(F32))
        g = lg_ref[...]
        u2 = xhat * g + lb_ref[...]
        su = _sigmoid(u2)
        u3_ref[...] = (u2 * su).astype(BF16)
        du2 = du3 * (su * (1.0 + u2 * (1.0 - su)))
        dxh = du2 * g
        du1 = rstd * (dxh - jnp.mean(dxh, axis=-1, keepdims=True) - xhat * jnp.mean(dxh * xhat, axis=-1, keepdims=True))
        du1_ref[...] = du1.astype(BF16)

        @pl.when(pl.program_id(0) == 0)
        def _():
            sums_ref[...] = jnp.zeros_like(sums_ref)
        for r, val in enumerate((dgc, dgt, du2 * xhat, du2, du1)):
            sums_ref[pl.ds(r, 1), :] += jnp.sum(val, axis=0, keepdims=True)

    big = jax.ShapeDtypeStruct((T, D), BF16)
    vec = _resident((1, D))
    return _call(
        body, name=name, grid=(T // tm,),
        out_shape=(big,) * 7 + (jax.ShapeDtypeStruct((8, D), F32),),
        in_specs=[_rows(tm, D)] * 6 + [vec, vec, _resident((D, D)), _resident((D, D)), _resident((D, D))],
        out_specs=(_rows(tm, D),) * 7 + (_acc_spec((8, D)),),
        args=(dx2, u1, co, ao, sgc, sgt, ln_g, ln_b, wcp, wo, wout), carry=carry)


def conv_bwd(du1, u0, ga, gb, dw_w, name, carry=None):
    T, D = du1.shape
    tm = min(ROW_TILE, T)
    nt = T // tm
    K = dw_w.shape[0]
    per = tm // CONV_HALO

    def body(d_ref, dn_ref, u_ref, uh_ref, ga_ref, gb_ref, w_ref, dga_ref, dgb_ref, dw_ref, dbuf, ubuf, du0_buf, taps):
        i = pl.program_id(0)
        dbuf[pl.ds(0, tm), :] = d_ref[...].astype(F32)
        dbuf[pl.ds(tm, CONV_HALO), :] = dn_ref[...].astype(F32) * (i < nt - 1).astype(F32)
        ubuf[pl.ds(0, CONV_HALO), :] = uh_ref[...].astype(F32) * (i > 0).astype(F32)
        ubuf[pl.ds(CONV_HALO, tm), :] = u_ref[...].astype(F32)

        @pl.when(i == 0)
        def _():
            taps[...] = jnp.zeros_like(taps)

        _conv_anticausal(dbuf, w_ref, du0_buf, tm, D, K)
        _conv_tap_grads(dbuf, ubuf, taps, tm, D, K)
        du0 = du0_buf[...]
        ga, gb = ga_ref[...].astype(F32), gb_ref[...].astype(F32)
        sg = _sigmoid(gb)
        dga_ref[...] = (du0 * sg).astype(BF16)
        dgb_ref[...] = (du0 * ga * sg * (1.0 - sg)).astype(BF16)

        @pl.when(i == nt - 1)
        def _():
            for k in range(K):
                dw_ref[pl.ds(k, 1), :] = jnp.sum(taps[k], axis=0, keepdims=True)

    nxt = pl.BlockSpec((CONV_HALO, D), lambda i: (jnp.minimum((i + 1) * per, nt * per - 1), 0))
    big = jax.ShapeDtypeStruct((T, D), BF16)
    return _call(
        body, name=name, grid=(nt,),
        out_shape=(big, big, jax.ShapeDtypeStruct((K, D), F32)),
        in_specs=[_rows(tm, D), nxt, _rows(tm, D), _halo_rows_spec(tm, D, CONV_HALO), _rows(tm, D), _rows(tm, D),
                  _resident((K, D))],
        out_specs=(_rows(tm, D), _rows(tm, D), _acc_spec((K, D))),
        scratch_shapes=[pltpu.VMEM((tm + CONV_HALO, D), F32), pltpu.VMEM((CONV_HALO + tm, D), F32), pltpu.VMEM((tm, D), F32),
                        pltpu.VMEM((K, SUBLANES, D), F32)],
        args=(du1, du1, u0, u0, ga, gb, dw_w), carry=carry)


def mix_in_bwd(dx2, x, gain, wint, wq, pieces, name):
    T, D = x.shape
    tm = min(ROW_TILE, T)
    widths = [p.shape[1] for p in pieces]
    offs = _w_in_rows(D)

    def body(dx2_ref, x_ref, g_ref, w_ref, wq_ref, *rest):
        piece_refs, (dx_ref, dg_ref) = rest[:len(pieces)], rest[len(pieces):]
        dh = None
        for n, (p_ref, off, w) in enumerate(zip(piece_refs, offs, widths)):
            t = _dot(p_ref[...], wq_ref[...] if n == 2 else w_ref[pl.ds(off, w), :])
            dh = t if dh is None else dh + t
        dx, dgt = _rms_bwd(x_ref[...], g_ref[...], dh)
        dx_ref[...] = dx2_ref[...] + dx

        @pl.when(pl.program_id(0) == 0)
        def _():
            dg_ref[...] = jnp.zeros_like(dg_ref)
        dg_ref[...] += jnp.sum(dgt, axis=0, keepdims=True)

    return pl.pallas_call(
        body, name=name, grid=(T // tm,),
        out_shape=(jax.ShapeDtypeStruct((T, D), F32), jax.ShapeDtypeStruct((1, D), F32)),
        in_specs=[_rows(tm, D), _rows(tm, D), _resident((1, D)), _resident(wint.shape), _resident(wq.shape)]
        + [_rows(tm, w) for w in widths],
        out_specs=(_rows(tm, D), _acc_spec((1, D))),
        compiler_params=_params("arbitrary"),
    )(dx2, x, gain, wint, wq, *pieces)


def rope_tables(positions):
    half = HEAD_DIM // 2
    inv_freq = ROPE_THETA ** (-jnp.arange(half, dtype=F32) / half)
    ang = positions.astype(F32)[:, None] * inv_freq
    cos, sin = jnp.cos(ang), jnp.sin(ang)
    reps = LANES // HEAD_DIM
    return jnp.tile(jnp.concatenate([cos, cos], axis=-1), (1, reps)), jnp.tile(jnp.concatenate([-sin, sin], axis=-1), (1, reps))


def _place():
    return lax.axis_index("x"), lax.axis_index("y"), lax.axis_index("c")


def all_gather(blocks, name):
    n = len(blocks)
    send = gather_send(blocks)
    forward = gather_forward(send.out_shapes)
    n_sems = len(send.sems)

    def body(*refs):
        ins, outs, sems = refs[:n], refs[n:2 * n], refs[2 * n:]
        send.start(ins, outs, sems[:n_sems])
        send.finish(ins, outs, sems[:n_sems])
        forward.start((), outs, sems[n_sems:])
        forward.finish((), outs, sems[n_sems:])

    return pl.pallas_call(
        body, name=name, out_shape=tuple(send.out_shapes), in_specs=[_ANY] * n, out_specs=(_ANY,) * n,
        scratch_shapes=list(send.sems) + list(forward.sems),
    )(*blocks)


def _chips_across(x, y):
    return [(1 - x, y), (x, 1 - y), (1 - x, 1 - y)]


def _dev_index(x, y, c):
    return 4 * x + 2 * y + c


def gather_send(blocks):
    n = len(blocks)

    def copies(in_refs, out_refs, sems):
        send, recv, local = sems
        x, y, c = _place()
        targets = [(x, y, 1 - c)] + [(*chip, c) for chip in _chips_across(x, y)]
        outgoing, incoming, mine = [], [], []
        for i, (x_ref, out_ref) in enumerate(zip(in_refs, out_refs)):
            for k, t in enumerate(targets):
                pair = dict(send_sem=send.at[4 * i + k], recv_sem=recv.at[4 * i + k], device_id=t, device_id_type=MESH)
                outgoing.append(pltpu.make_async_remote_copy(src_ref=x_ref, dst_ref=out_ref.at[_dev_index(x, y, c)], **pair))
                incoming.append(pltpu.make_async_remote_copy(src_ref=x_ref, dst_ref=out_ref.at[_dev_index(*t)], **pair))
            mine.append(pltpu.make_async_copy(x_ref, out_ref.at[_dev_index(x, y, c)], local.at[i]))
        return outgoing, incoming, mine

    def start(*refs):
        outgoing, _, mine = copies(*refs)
        for cp in mine + outgoing:
            cp.start()

    def finish(*refs):
        outgoing, incoming, mine = copies(*refs)
        for cp in incoming:
            cp.wait_recv()
        for cp in outgoing:
            cp.wait_send()
        for cp in mine:
            cp.wait()

    return Carry(ins=tuple(blocks), out_shapes=tuple(jax.ShapeDtypeStruct((N_DEV,) + b.shape, b.dtype) for b in blocks),
                 aliases={}, sems=(pltpu.SemaphoreType.DMA((4 * n,)), pltpu.SemaphoreType.DMA((4 * n,)),
                                   pltpu.SemaphoreType.DMA((n,))), start=start, finish=finish)


def gather_forward(gathered):
    n = len(gathered)

    def copies(in_refs, out_refs, sems):
        send, recv = sems
        x, y, c = _place()
        outgoing, incoming = [], []
        for i, buf in enumerate(out_refs):
            for k, chip in enumerate(_chips_across(x, y)):
                pair = dict(send_sem=send.at[3 * i + k], recv_sem=recv.at[3 * i + k], device_id=(x, y, 1 - c),
                            device_id_type=MESH)
                rows = buf.at[_dev_index(*chip, c)]
                outgoing.append(pltpu.make_async_remote_copy(src_ref=rows, dst_ref=rows, **pair))
                theirs = buf.at[_dev_index(*chip, 1 - c)]
                incoming.append(pltpu.make_async_remote_copy(src_ref=theirs, dst_ref=theirs, **pair))
        return outgoing, incoming

    def start(*refs):
        for cp in copies(*refs)[0]:
            cp.start()

    def finish(*refs):
        outgoing, incoming = copies(*refs)
        for cp in incoming:
            cp.wait_recv()
        for cp in outgoing:
            cp.wait_send()

    return Carry(ins=tuple(gathered), out_shapes=tuple(jax.ShapeDtypeStruct(g.shape, g.dtype) for g in gathered),
                 aliases={i: i for i in range(n)},
                 sems=(pltpu.SemaphoreType.DMA((3 * n,)), pltpu.SemaphoreType.DMA((3 * n,))), start=start, finish=finish)


def gather_whole(blocks):
    send = gather_send(blocks)
    forward = gather_forward(send.out_shapes)
    n = len(send.sems)

    def relay(ins, outs, sems):
        send.finish(ins, outs, sems[:n])
        forward.start((), outs, sems[n:])

    return Carry(ins=send.ins, out_shapes=send.out_shapes, aliases={}, sems=send.sems + forward.sems,
                 start=lambda ins, outs, sems: send.start(ins, outs, sems[:n]), relay=relay,
                 finish=lambda ins, outs, sems: forward.finish((), outs, sems[n:]))


def compose(*carries):
    def split(refs, count):
        out, at = [], 0
        for c in carries:
            n = count(c)
            out.append(refs[at:at + n])
            at += n
        return out

    def each(stage):
        def run(ins, outs, sems):
            parts = zip(carries, split(ins, lambda c: len(c.ins)), split(outs, lambda c: len(c.out_shapes)),
                        split(sems, lambda c: len(c.sems)))
            for c, i, o, s in parts:
                if getattr(c, stage) is not None:
                    getattr(c, stage)(i, o, s)
        return run

    aliases, n_in, n_out = {}, 0, 0
    for c in carries:
        aliases.update({n_in + i: n_out + o for i, o in c.aliases.items()})
        n_in += len(c.ins)
        n_out += len(c.out_shapes)
    return Carry(ins=sum((tuple(c.ins) for c in carries), ()), out_shapes=sum((tuple(c.out_shapes) for c in carries), ()),
                 aliases=aliases, sems=sum((tuple(c.sems) for c in carries), ()), start=each("start"), finish=each("finish"),
                 relay=each("relay") if any(c.relay is not None for c in carries) else None)


def swap_halves(by_core):
    n = len(by_core)

    def copies(in_refs, out_refs, sems):
        send, recv = sems
        x, y, c = _place()
        return [pltpu.make_async_remote_copy(src_ref=a.at[:, 1 - c], dst_ref=r, send_sem=send.at[i], recv_sem=recv.at[i],
                                             device_id=(x, y, 1 - c), device_id_type=MESH)
                for i, (a, r) in enumerate(zip(in_refs, out_refs))]

    def start(*refs):
        for cp in copies(*refs):
            cp.start()

    def finish(*refs):
        for cp in copies(*refs):
            cp.wait()

    shapes = tuple(jax.ShapeDtypeStruct((a.shape[0],) + a.shape[2:], a.dtype) for a in by_core)
    return Carry(ins=tuple(by_core), out_shapes=shapes, aliases={},
                 sems=(pltpu.SemaphoreType.DMA((n,)), pltpu.SemaphoreType.DMA((n,))), start=start, finish=finish)


def exchange_between_chips(by_chip):
    n = len(by_chip)

    def copies(in_refs, out_refs, sems):
        send, recv = sems
        x, y, c = _place()
        out = []
        for i, (s, r) in enumerate(zip(in_refs, out_refs)):
            for k, (tx, ty) in enumerate(_chips_across(x, y)):
                out.append(pltpu.make_async_remote_copy(
                    src_ref=s.at[2 * tx + ty], dst_ref=r.at[k], send_sem=send.at[3 * i + k], recv_sem=recv.at[3 * i + k],
                    device_id=(tx, ty, c), device_id_type=MESH))
        return out

    def start(*refs):
        for cp in copies(*refs):
            cp.start()

    def finish(*refs):
        for cp in copies(*refs):
            cp.wait()

    shapes = tuple(jax.ShapeDtypeStruct((3,) + a.shape[1:], a.dtype) for a in by_chip)
    return Carry(ins=tuple(by_chip), out_shapes=shapes, aliases={},
                 sems=(pltpu.SemaphoreType.DMA((3 * n,)), pltpu.SemaphoreType.DMA((3 * n,))), start=start, finish=finish)


def run_exchange(carry, name):
    n_in = len(carry.ins)
    n_out = len(carry.out_shapes)

    def body(*refs):
        parts = refs[:n_in], refs[n_in:n_in + n_out], refs[n_in + n_out:]
        carry.start(*parts)
        carry.finish(*parts)

    return pl.pallas_call(
        body, name=name, out_shape=tuple(carry.out_shapes), in_specs=[_ANY] * n_in, out_specs=(_ANY,) * n_out,
        scratch_shapes=list(carry.sems), input_output_aliases=dict(carry.aliases),
    )(*carry.ins)


def pair_sum(my_core, by_core, received, name):
    n = len(by_core)

    def body(core_ref, *refs):
        for a_ref, b_ref, o_ref in zip(refs[:n], refs[n:2 * n], refs[2 * n:]):
            o_ref[0] = (a_ref[0, 0].astype(F32) + b_ref[0].astype(F32)).astype(BF16)

    mine = [pl.BlockSpec((1, 1) + a.shape[2:], lambda j, core: (j, core[0], 0, 0)) for a in by_core]
    theirs = [pl.BlockSpec((1,) + r.shape[1:], lambda j, core: (j, 0, 0)) for r in received]
    return pl.pallas_call(
        body, name=name, out_shape=tuple(jax.ShapeDtypeStruct(r.shape, BF16) for r in received),
        grid_spec=pltpu.PrefetchScalarGridSpec(num_scalar_prefetch=1, grid=(by_core[0].shape[0],), in_specs=mine + theirs,
                                               out_specs=tuple(theirs)),
        compiler_params=_params("arbitrary"),
    )(my_core, *by_core, *received)


def _adamw_math(w, g, m, v):
    m = ADAM_B1 * m + (1.0 - ADAM_B1) * g
    v = ADAM_B2 * v + (1.0 - ADAM_B2) * (g * g)
    m_hat = m / (1.0 - ADAM_B1 ** ADAM_STEP)
    v_hat = v / (1.0 - ADAM_B2 ** ADAM_STEP)
    delta = -ADAM_LR * (m_hat / (jnp.sqrt(v_hat) + ADAM_EPS) + ADAM_WD * w)
    return delta, m, v


def adamw(my_chip, ws, gs, ms, vs, name, carry=None):
    n = len(ws)
    flat, widths = [], []
    for g in gs:
        parts = list(g) if isinstance(g, (tuple, list)) else [g]
        flat += parts
        widths.append(len(parts))
    c_ins = list(carry.ins) if carry else []
    c_outs = list(carry.out_shapes) if carry else []

    def body(chip_ref, *refs):
        w_refs, refs = refs[:n], refs[n:]
        g_refs, refs = refs[:len(flat)], refs[len(flat):]
        m_refs, v_refs, refs = refs[:n], refs[n:2 * n], refs[2 * n:]
        carried = (refs[:len(c_ins)], refs[len(c_ins) + 4 * n:len(c_ins) + 4 * n + len(c_outs)],
                   refs[len(c_ins) + 4 * n + len(c_outs):])
        outs = refs[len(c_ins):]
        if carry:
            carry.start(*carried)
        at = 0
        for t in range(n):
            if widths[t] == 1:
                g = g_refs[at][...]
            else:
                g = g_refs[at][0].astype(F32)
                for k in range(3):
                    g = g + g_refs[at + 1][k].astype(F32)
            at += widths[t]
            outs[4 * t][...] = g
            outs[4 * t + 1][...], outs[4 * t + 2][...], outs[4 * t + 3][...] = _adamw_math(
                w_refs[t][...], g, m_refs[t][...], v_refs[t][...])
        if carry:
            if carry.relay is not None:
                carry.relay(*carried)
            carry.finish(*carried)

    def whole(a):
        zeros = (0,) * a.ndim
        return pl.BlockSpec(a.shape, lambda i, chip: zeros, pipeline_mode=pl.Buffered(1))

    g_specs = []
    for g in gs:
        if isinstance(g, (tuple, list)):
            g_specs += [pl.BlockSpec((1,) + g[0].shape[1:], lambda i, chip: (chip[0], 0, 0), pipeline_mode=pl.Buffered(1)),
                        whole(g[1])]
        else:
            g_specs.append(whole(g))
    shapes, out_specs = [], []
    for w in ws:
        shapes += [jax.ShapeDtypeStruct(w.shape, F32)] * 4
        out_specs += [whole(w)] * 4
    res = pl.pallas_call(
        body, name=name, out_shape=tuple(shapes) + tuple(c_outs),
        grid_spec=pltpu.PrefetchScalarGridSpec(
            num_scalar_prefetch=1, grid=(1,),
            in_specs=[whole(w) for w in ws] + g_specs + [whole(a) for a in ms + vs] + [_ANY] * len(c_ins),
            out_specs=tuple(out_specs) + (_ANY,) * len(c_outs), scratch_shapes=list(carry.sems) if carry else []),
        compiler_params=_params("arbitrary"),
    )(my_chip, *ws, *flat, *ms, *vs, *c_ins)
    return [tuple(res[4 * t:4 * t + 4]) for t in range(n)], tuple(res[4 * n:])


def adamw_replicated(w, partials, m, v, name):
    def body(w_ref, p_ref, m_ref, v_ref, g_ref, d_ref, mo_ref, vo_ref):
        g = p_ref[0]
        for k in range(1, N_DEV):
            g = g + p_ref[k]
        g_ref[...] = g
        d_ref[...], mo_ref[...], vo_ref[...] = _adamw_math(w_ref[...], g, m_ref[...], v_ref[...])

    shape = jax.ShapeDtypeStruct(w.shape, F32)
    return pl.pallas_call(body, name=name, out_shape=(shape,) * 4, compiler_params=_params())(w, partials, m, v)


PACK_COLS = 1024
PACK_ROW_ALIGN = 16

REPLICATED = ("ffn1_norm", "mix_norm", "conv_dw_b", "conv_ln_g", "conv_ln_b", "ffn2_norm", "final_norm", "gate_b", "attn_sinks")
WEIGHT_ORDER = ("ffn1_norm", "ffn1_w_gate", "ffn1_w_up", "ffn1_w_down", "mix_norm", "w_in", "conv_dw_w", "conv_dw_b", "conv_ln_g",
                "conv_ln_b", "conv_w_proj", "attn_sinks", "attn_w_o", "gate_b", "w_out", "ffn2_norm", "ffn2_w_gate", "ffn2_w_up",
                "ffn2_w_down", "final_norm")


def _to_rows(flat, lead):
    n = flat.shape[-1]
    rows = -(-n // PACK_COLS)
    flat = jnp.pad(flat, [(0, 0)] * lead + [(0, rows * PACK_COLS - n)])
    return flat.reshape(flat.shape[:lead] + (rows, PACK_COLS))


def _pad_rows(a, axis):
    rows = a.shape[axis]
    pad = -rows % PACK_ROW_ALIGN
    widths = [(0, 0)] * a.ndim
    widths[axis] = (0, pad)
    return jnp.pad(a, widths)


def _from_rows(rows, shape):
    n = 1
    for s in shape:
        n *= s
    return rows.reshape(rows.shape[:-2] + (-1,))[..., :n].reshape(rows.shape[:-2] + tuple(shape))


def _heads_slot_major(rows):
    group = rows.shape[0] // (N_KV_HEADS * HEAD_DIM)
    return rows.reshape(N_KV_HEADS, group, HEAD_DIM, rows.shape[1]).transpose(1, 0, 2, 3).reshape(rows.shape)


def _heads_kv_major(rows):
    group = rows.shape[0] // (N_KV_HEADS * HEAD_DIM)
    return rows.reshape(group, N_KV_HEADS, HEAD_DIM, rows.shape[1]).transpose(1, 0, 2, 3).reshape(rows.shape)


def _by_core(full_rows):
    return full_rows.reshape((N_DEV // 2, 2, full_rows.shape[0] // N_DEV, full_rows.shape[1]))


def kernel(x, positions, ffn1_norm, ffn1_w_gate, ffn1_w_up, ffn1_w_down, mix_norm, w_in, conv_dw_w, conv_dw_b, conv_ln_g, conv_ln_b, conv_w_proj, attn_sinks, attn_w_o, gate_b, w_out, ffn2_norm, ffn2_w_gate, ffn2_w_up, ffn2_w_down, final_norm, loss_target, m_ffn1_norm, m_ffn1_w_gate, m_ffn1_w_up, m_ffn1_w_down, m_mix_norm, m_w_in, m_conv_dw_w, m_conv_dw_b, m_conv_ln_g, m_conv_ln_b, m_conv_w_proj, m_attn_sinks, m_attn_w_o, m_gate_b, m_w_out, m_ffn2_norm, m_ffn2_w_gate, m_ffn2_w_up, m_ffn2_w_down, m_final_norm, v_ffn1_norm, v_ffn1_w_gate, v_ffn1_w_up, v_ffn1_w_down, v_mix_norm, v_w_in, v_conv_dw_w, v_conv_dw_b, v_conv_ln_g, v_conv_ln_b, v_conv_w_proj, v_attn_sinks, v_attn_w_o, v_gate_b, v_w_out, v_ffn2_norm, v_ffn2_w_gate, v_ffn2_w_up, v_ffn2_w_down, v_final_norm):
    given = dict(locals())
    shapes = {n: given[n].shape for n in WEIGHT_ORDER}
    w = {n: given[n].reshape(given[n].shape[-2:]) if given[n].ndim == 3 else given[n].reshape(1, -1) for n in WEIGHT_ORDER}
    m = {n: given["m_" + n].reshape(w[n].shape) for n in WEIGHT_ORDER}
    v = {n: given["v_" + n].reshape(w[n].shape) for n in WEIGHT_ORDER}
    my_x, my_y, my_c = _place()
    my_core = my_c.astype(jnp.int32).reshape(1)
    my_chip = (2 * my_x + my_y).astype(jnp.int32).reshape(1)
    xs, target = x[0], loss_target[0]
    T, D = xs.shape
    KV = N_KV_HEADS * HEAD_DIM
    K = w["conv_dw_w"].shape[0]

    def t16(n):
        return w[n].T.astype(BF16)

    def r16(n):
        return w[n].astype(BF16)

    blocks1 = [t16("ffn1_w_gate"), t16("ffn1_w_up"), r16("ffn1_w_down")]
    dw_bits = _pad_rows(_to_rows(lax.bitcast_convert_type(w["conv_dw_w"], BF16).reshape(-1), 0), 0)
    blocks2 = [t16("w_in"), r16("conv_w_proj"), r16("attn_w_o"), r16("w_out"), dw_bits]
    blocks3 = [t16("ffn2_w_gate"), t16("ffn2_w_up"), r16("ffn2_w_down")]
    cos, sin = rope_tables(positions[0])
    sink_col = jnp.repeat(w["attn_sinks"].reshape(-1), WINDOW).reshape(N_KV_HEADS, (D // KV) * WINDOW, 1)

    def full(gathered):
        return gathered.reshape(-1, gathered.shape[2])

    wgt1, wut1 = (full(g) for g in all_gather(blocks1[:2], "gather_ffn1_up"))
    (h1, a1, b1, s1), got = ffn_up(xs, w["ffn1_norm"], wgt1, wut1, "ffn1_up",
                                   carry=compose(gather_whole(blocks1[2:]), gather_send(blocks2[:1])))
    wd1 = full(got[0])
    (x1,), got = ffn_down(xs, s1, wd1, "ffn1_down", carry=compose(gather_forward(got[1:]), gather_send(blocks2[1:])))
    wint = full(got[0])
    wq = _heads_slot_major(wint[2 * D:3 * D])
    (h2, ga, gb, u0, q, sgc, sgt, kk, vv), got = mix_in_fwd(x1, w["mix_norm"], wint, wq, w["gate_b"], cos, sin, "mix_in_fwd",
                                                            carry=compose(gather_forward(got[1:]), gather_send(blocks3)))
    wcp, wo, wout = (full(g) for g in got[:3])
    wo = _heads_slot_major(wo)
    dw_full = lax.bitcast_convert_type(_from_rows(got[3], w["conv_dw_w"].shape + (2,)), F32)
    dw_full = dw_full.transpose(1, 0, 2).reshape(K, D)
    o, gath3 = attn_fwd(q, kk, vv, sink_col, "attn_fwd", carry=gather_forward(got[4:]))
    x2, u1, co, ao, merged = mix_out_fwd(x1, u0, o, sgc, sgt, dw_full, w["conv_dw_b"], w["conv_ln_g"], w["conv_ln_b"],
                                         wcp, wo, wout, "mix_out_fwd")
    wgt2, wut2, wd2 = (full(g) for g in gath3)
    loss, dx3, d_final, h3, a2, b2, s2 = ffn_fwd_loss(x2, w["ffn2_norm"], wgt2, wut2, wd2, w["final_norm"], target,
                                                      "ffn2_fwd_loss")

    small = {"final_norm": d_final}
    (da2, db2), _ = ffn_bwd_hidden(dx3, a2, b2, wd2, "ffn2_bwd_hidden")
    (dx2, small["ffn2_norm"]), _ = ffn_bwd_input(dx3, x2, w["ffn2_norm"], da2, db2, wgt2, wut2, "ffn2_bwd_input")
    core2 = [_by_core(g) for g in (wgrad(da2, h3, "ffn2_dwg")[0], wgrad(db2, h3, "ffn2_dwu")[0],
                                   wgrad(s2, dx3, "ffn2_dwd", b_scale=FFN_SCALE)[0])]
    (dgc, dgt, do, du1, dco, dao, u3, sums), recv2 = mix_out_bwd(dx2, u1, co, ao, sgc, sgt, w["conv_ln_g"], w["conv_ln_b"],
                                                                 wcp, wo, wout, "mix_out_bwd", carry=swap_halves(core2))
    chip2 = pair_sum(my_core, core2, recv2, "ffn2_grads_pair_sum")
    small["gate_b"] = jnp.concatenate([sums[0:1], sums[1:2]], axis=1)
    small["conv_ln_g"], small["conv_ln_b"], small["conv_dw_b"] = sums[2:3], sums[3:4], sums[4:5]
    g_wout = wgrad(merged, dx2, "dw_out")[0]
    g_wcp = wgrad(u3, dco, "dw_conv_proj")[0]
    g_wo = _heads_kv_major(wgrad(o, dao, "dw_attn_o")[0])
    (dga, dgb, g_dw), got2 = conv_bwd(du1, u0, ga, gb, dw_full, "conv_bwd", carry=exchange_between_chips(chip2))
    dq, dk, dv, dsink = attn_bwd(q, kk, vv, do, sink_col, cos, sin, "attn_bwd")
    small["attn_sinks"] = dsink[:D // KV, :N_KV_HEADS].T.reshape(1, -1)
    pieces = [dga, dgb, dq, dk, dv, dgc, dgt]
    dx1, small["mix_norm"] = mix_in_bwd(dx2, x1, w["mix_norm"], wint, wq, pieces, "mix_in_bwd")
    group = D // KV
    q_moves = [(2 * D + HEAD_DIM * (group * g + hh), 2 * D + HEAD_DIM * (N_KV_HEADS * hh + g), HEAD_DIM)
               for g in range(N_KV_HEADS) for hh in range(group)]
    g_wint = wgrad_stacked(pieces[:3], h2, "dw_in_a", wint.shape[0], 0, moves=[(0, 0, 2 * D)] + q_moves)
    g_wint = wgrad_stacked(pieces[3:], h2, "dw_in_b", wint.shape[0], 3 * D, into=g_wint)
    corem = [_by_core(a) for a in (g_wint, g_wcp, g_wo, g_wout)]

    (da1, db1), recvm = ffn_bwd_hidden(dx1, a1, b1, wd1, "ffn1_bwd_hidden", carry=swap_halves(corem))
    chipm = pair_sum(my_core, corem, recvm, "mix_grads_pair_sum")
    g1c, gotm_a = wgrad(s1, dx1, "ffn1_dwd", carry=exchange_between_chips(chipm[:1]), b_scale=FFN_SCALE)
    g1a, gotm_b = wgrad(da1, h1, "ffn1_dwg", carry=exchange_between_chips(chipm[1:]))
    g1b = wgrad(db1, h1, "ffn1_dwu")[0]
    core1 = [_by_core(a) for a in (g1a, g1b, g1c)]
    chip1 = pair_sum(my_core, core1, run_exchange(swap_halves(core1), "ffn1_grads_swap"), "ffn1_grads_pair_sum")
    (grad_x, small["ffn1_norm"]), got1 = ffn_bwd_input(dx1, xs, w["ffn1_norm"], da1, db1, wgt1, wut1, "ffn1_bwd_input",
                                                       carry=exchange_between_chips(chip1))

    gotm = gotm_a + gotm_b
    grad_src = {"ffn1_w_gate": (chip1[0], got1[0]), "ffn1_w_up": (chip1[1], got1[1]), "ffn1_w_down": (chip1[2], got1[2]),
                "ffn2_w_gate": (chip2[0], got2[0]), "ffn2_w_up": (chip2[1], got2[1]), "ffn2_w_down": (chip2[2], got2[2]),
                "w_in": (chipm[0], gotm[0]), "conv_w_proj": (chipm[1], gotm[1]), "attn_w_o": (chipm[2], gotm[2]),
                "w_out": (chipm[3], gotm[3])}
    grads = {}

    def pack_small(d, taps, extra):
        rows = [_to_rows(d[n].reshape(-1), 0) for n in REPLICATED] + [taps, _to_rows(extra.reshape(-1), 0)]
        return _pad_rows(jnp.concatenate(rows, axis=0), 0)

    def like(a, ref):
        return a if a.shape == ref.shape else a.T

    def adamw_group(names, name, carry=None):
        refs = [grad_src[n][0][0] if isinstance(grad_src[n], tuple) else grad_src[n] for n in names]
        res, carried = adamw(my_chip, [like(w[n], r) for n, r in zip(names, refs)], [grad_src[n] for n in names],
                             [like(m[n], r) for n, r in zip(names, refs)], [like(v[n], r) for n, r in zip(names, refs)], name,
                             carry=carry)
        for n, outs in zip(names, res):
            grads[n], delta[n], new_m[n], new_v[n] = (like(a, w[n]) for a in outs)
        return carried

    delta, new_m, new_v = {}, {}, {}
    zero, no_taps = jnp.zeros((1, LANES), F32), jnp.zeros((K, D), F32)
    (shares,) = adamw_group(("ffn2_w_gate", "ffn2_w_up", "ffn2_w_down"), "adamw_ffn2",
                            carry=gather_whole([pack_small(small, g_dw, loss)]))
    g_s, d_s, m_s, v_s = adamw_replicated(pack_small(w, no_taps, zero), shares, pack_small(m, no_taps, zero),
                                          pack_small(v, no_taps, zero), "adamw_replicated")
    off = 0
    for n in REPLICATED:
        r = -(-w[n].shape[1] // PACK_COLS)
        grads[n], delta[n], new_m[n], new_v[n] = (_from_rows(a[off:off + r], w[n].shape) for a in (g_s, d_s, m_s, v_s))
        off += r
    shard_cols = w["conv_dw_w"].shape[1]
    grad_src["conv_dw_w"] = lax.dynamic_slice_in_dim(g_s[off:off + K], _dev_index(my_x, my_y, my_c) * shard_cols, shard_cols,
                                                    axis=1)
    total_loss = g_s[off + K, 0]
    adamw_group(("ffn1_w_gate", "ffn1_w_up", "ffn1_w_down"), "adamw_ffn1")
    adamw_group(("w_in", "conv_dw_w", "conv_w_proj", "attn_w_o", "w_out"), "adamw_mix")

    out = [total_loss, grad_x[None]]
    for d in (grads, delta, new_m, new_v):
        out += [d[n].reshape(shapes[n]) for n in WEIGHT_ORDER]
    return tuple(out)
```

```python
import functools
from typing import Callable, NamedTuple

import jax
import jax.numpy as jnp
from jax import lax
from jax.experimental import pallas as pl
from jax.experimental.pallas import tpu as pltpu

F32, BF16 = jnp.float32, jnp.bfloat16

HEAD_DIM = 64
N_KV_HEADS = 4
WINDOW = 128
ROPE_THETA = 10000.0
EPS = 1e-6
LN_EPS = 1e-5
NEG_INF = -1e30
ADAM_LR, ADAM_B1, ADAM_B2, ADAM_EPS, ADAM_WD, ADAM_STEP = 0.001, 0.9, 0.999, 1e-08, 0.01, 10

N_DEV = 8
LANES = 128
SUBLANES = 8
CONV_HALO = 32
CONV_ROWS, CONV_LANES = 64, 256
ROW_TILE = 512
FFN_CHUNK = 256
FFN_SCALE = 0.5
WGRAD_TILE_ELEMS = 2 ** 22
WGRAD_TILE_ROWS = 2048
WGRAD_VMEM = 40 * 2 ** 20
VMEM_LIMIT = 56 * 2 ** 20
MESH = pl.DeviceIdType.MESH


def _params(*sem):
    return pltpu.CompilerParams(dimension_semantics=sem or None, vmem_limit_bytes=VMEM_LIMIT)


def _resident(shape):
    zeros = (0,) * len(shape)
    return pl.BlockSpec(shape, lambda *_: zeros, pipeline_mode=pl.Buffered(1))


def _rows(tm, n):
    return pl.BlockSpec((tm, n), lambda i: (i, 0))


def _acc_spec(shape):
    zeros = (0,) * len(shape)
    return pl.BlockSpec(shape, lambda *_: zeros)


_ANY = pl.BlockSpec(memory_space=pl.ANY)


class Carry(NamedTuple):
    ins: tuple
    out_shapes: tuple
    aliases: dict
    sems: tuple
    start: Callable
    finish: Callable
    relay: Callable = None


def _call(body, *, name, grid, in_specs, out_specs, out_shape, args, scratch_shapes=(), carry=None):
    n_in, n_out, n_scr = len(in_specs), len(out_specs), len(scratch_shapes)
    params = _params(*(("arbitrary",) * len(grid)))
    if carry is None:
        res = pl.pallas_call(body, name=name, grid=grid, in_specs=list(in_specs), out_specs=tuple(out_specs),
                             out_shape=tuple(out_shape), scratch_shapes=list(scratch_shapes), compiler_params=params)(*args)
        return tuple(res), ()
    c_in, c_out = len(carry.ins), len(carry.out_shapes)

    def wrapped(*refs):
        ins, c_ins = refs[:n_in], refs[n_in:n_in + c_in]
        p = n_in + c_in
        outs, c_outs = refs[p:p + n_out], refs[p + n_out:p + n_out + c_out]
        p += n_out + c_out
        scr, c_sems = refs[p:p + n_scr], refs[p + n_scr:]
        ids = [pl.program_id(d) for d in range(len(grid))]
        first = functools.reduce(jnp.logical_and, [i == 0 for i in ids])
        last = functools.reduce(jnp.logical_and, [i == n - 1 for i, n in zip(ids, grid)])

        @pl.when(first)
        def _():
            carry.start(c_ins, c_outs, c_sems)

        body(*ins, *outs, *scr)

        if carry.relay is not None:
            @pl.when(ids[0] == (3 * grid[0]) // 4)
            def _():
                carry.relay(c_ins, c_outs, c_sems)

        @pl.when(last)
        def _():
            carry.finish(c_ins, c_outs, c_sems)

    res = pl.pallas_call(
        wrapped, name=name, grid=grid, in_specs=list(in_specs) + [_ANY] * c_in, out_specs=tuple(out_specs) + (_ANY,) * c_out,
        out_shape=tuple(out_shape) + tuple(carry.out_shapes), scratch_shapes=list(scratch_shapes) + list(carry.sems),
        input_output_aliases={n_in + i: n_out + o for i, o in carry.aliases.items()}, compiler_params=params,
    )(*args, *carry.ins)
    return tuple(res[:n_out]), tuple(res[n_out:])


def _nt(a, b):
    return lax.dot_general(a, b, (((1,), (1,)), ((), ())), preferred_element_type=F32)


def _tn(a, b):
    return lax.dot_general(a, b, (((0,), (0,)), ((), ())), preferred_element_type=F32)


def _dot(a, b):
    return jnp.dot(a, b, preferred_element_type=F32)


def _sigmoid(x):
    return 1.0 / (1.0 + jnp.exp(-x))


def _rms_fwd(x, g):
    r = lax.rsqrt(jnp.mean(x * x, axis=-1, keepdims=True) + EPS)
    return (x * r) * g


def _rms_bwd(x, g, dy):
    r = lax.rsqrt(jnp.mean(x * x, axis=-1, keepdims=True) + EPS)
    xhat = x * r
    dyg = dy * g
    dx = r * (dyg - xhat * jnp.mean(dyg * xhat, axis=-1, keepdims=True))
    return dx, dy * xhat


def _rot_half(x):
    lane = lax.broadcasted_iota(jnp.int32, (x.shape[0], LANES), 1)
    first = (lane % HEAD_DIM) < (HEAD_DIM // 2)
    out = []
    for s in range(x.shape[1] // LANES):
        xs = x[:, LANES * s:LANES * (s + 1)]
        out.append(jnp.where(first, pltpu.roll(xs, LANES - HEAD_DIM // 2, 1), pltpu.roll(xs, HEAD_DIM // 2, 1)))
    return out[0] if len(out) == 1 else jnp.concatenate(out, axis=1)


def _tile_lanes(t, width):
    return t if width == LANES else jnp.concatenate([t] * (width // LANES), axis=1)


def _rope_fwd(x, cos, sin_signed):
    w = x.shape[1]
    return x * _tile_lanes(cos, w) + _rot_half(x) * _tile_lanes(sin_signed, w)


def _rope_bwd(dy, cos, sin_signed):
    w = dy.shape[1]
    return dy * _tile_lanes(cos, w) + _rot_half(dy * _tile_lanes(sin_signed, w))


def _ffn_rows(x, g_ref, wg_ref, wu_ref, wd_ref, h_ref, a_ref, b_ref, s_ref, acc_ref):
    F = wg_ref.shape[0]
    h = _rms_fwd(x, g_ref[...]).astype(BF16)
    h_ref[...] = h
    for c in range(F // FFN_CHUNK):
        cs = pl.ds(c * FFN_CHUNK, FFN_CHUNK)
        a = _nt(h, wg_ref[cs, :])
        b = _nt(h, wu_ref[cs, :])
        a_ref[:, cs] = a.astype(BF16)
        b_ref[:, cs] = b.astype(BF16)
        s = (a * _sigmoid(a) * b).astype(BF16)
        s_ref[:, cs] = s
        y = _dot(s, wd_ref[cs, :])
        if c == 0:
            acc_ref[...] = y
        else:
            acc_ref[...] += y
    return x + FFN_SCALE * acc_ref[...]


def ffn_up(x, gain, wgt, wut, name, carry=None):
    T, D = x.shape
    F = wgt.shape[0]
    tm = min(ROW_TILE, T)

    def body(x_ref, g_ref, wg_ref, wu_ref, h_ref, a_ref, b_ref, s_ref):
        h = _rms_fwd(x_ref[...], g_ref[...]).astype(BF16)
        h_ref[...] = h
        for c in range(F // FFN_CHUNK):
            cs = pl.ds(c * FFN_CHUNK, FFN_CHUNK)
            a = _nt(h, wg_ref[cs, :])
            b = _nt(h, wu_ref[cs, :])
            a_ref[:, cs] = a.astype(BF16)
            b_ref[:, cs] = b.astype(BF16)
            s_ref[:, cs] = (a * _sigmoid(a) * b).astype(BF16)

    wide = jax.ShapeDtypeStruct((T, F), BF16)
    return _call(
        body, name=name, grid=(T // tm,), out_shape=(jax.ShapeDtypeStruct((T, D), BF16), wide, wide, wide),
        in_specs=[_rows(tm, D), _resident((1, D)), _resident((F, D)), _resident((F, D))],
        out_specs=(_rows(tm, D), _rows(tm, F), _rows(tm, F), _rows(tm, F)), args=(x, gain, wgt, wut), carry=carry)


def ffn_down(x, s, wd, name, carry=None):
    T, D = x.shape
    F = wd.shape[0]
    tm = min(2 * ROW_TILE, T)

    def body(x_ref, s_ref, wd_ref, xo_ref):
        xo_ref[...] = x_ref[...] + FFN_SCALE * _dot(s_ref[...], wd_ref[...])

    return _call(
        body, name=name, grid=(T // tm,), out_shape=(jax.ShapeDtypeStruct((T, D), F32),),
        in_specs=[_rows(tm, D), _rows(tm, F), _resident((F, D))], out_specs=(_rows(tm, D),), args=(x, s, wd), carry=carry)


def ffn_fwd_loss(x, gain, wgt, wut, wd, final_gain, target, name):
    T, D = x.shape
    F = wgt.shape[0]
    tm = min(ROW_TILE, T)

    def body(x_ref, g_ref, wg_ref, wu_ref, wd_ref, gf_ref, t_ref,
             loss_ref, dx_ref, dg_ref, h_ref, a_ref, b_ref, s_ref, acc_ref):
        xo = _ffn_rows(x_ref[...], g_ref, wg_ref, wu_ref, wd_ref, h_ref, a_ref, b_ref, s_ref, acc_ref)
        gf = gf_ref[...]
        err = _rms_fwd(xo, gf) - t_ref[...]
        dx, dgt = _rms_bwd(xo, gf, err * (1.0 / D))
        dx_ref[...] = dx

        @pl.when(pl.program_id(0) == 0)
        def _():
            dg_ref[...] = jnp.zeros_like(dg_ref)
            loss_ref[...] = jnp.zeros_like(loss_ref)
        dg_ref[...] += jnp.sum(dgt, axis=0, keepdims=True)
        per_token = jnp.sum(err * err, axis=-1, keepdims=True) * (0.5 / D)
        loss_ref[...] += jnp.broadcast_to(jnp.sum(per_token, axis=0, keepdims=True), (1, LANES))

    return pl.pallas_call(
        body, name=name, grid=(T // tm,),
        out_shape=(jax.ShapeDtypeStruct((1, LANES), F32), jax.ShapeDtypeStruct((T, D), F32), jax.ShapeDtypeStruct((1, D), F32),
                   jax.ShapeDtypeStruct((T, D), BF16), jax.ShapeDtypeStruct((T, F), BF16), jax.ShapeDtypeStruct((T, F), BF16),
                   jax.ShapeDtypeStruct((T, F), BF16)),
        in_specs=[_rows(tm, D), _resident((1, D)), _resident((F, D)), _resident((F, D)), _resident((F, D)), _resident((1, D)),
                  _rows(tm, D)],
        out_specs=(_acc_spec((1, LANES)), _rows(tm, D), _acc_spec((1, D)), _rows(tm, D), _rows(tm, F), _rows(tm, F),
                   _rows(tm, F)),
        scratch_shapes=[pltpu.VMEM((tm, D), F32)], compiler_params=_params("arbitrary"),
    )(x, gain, wgt, wut, wd, final_gain, target)


def ffn_bwd_hidden(dxo, a, b, wd, name, carry=None):
    T, D = dxo.shape
    F = wd.shape[0]
    tm = min(ROW_TILE, T)
    fc = FFN_CHUNK

    def hidden_body(dxo_ref, a_ref, b_ref, wd_ref, da_ref, db_ref):
        g0 = (FFN_SCALE * dxo_ref[...]).astype(BF16)
        for c in range(F // fc):
            cs = pl.ds(c * fc, fc)
            ds = _nt(g0, wd_ref[cs, :])
            a = a_ref[:, cs].astype(F32)
            bb = b_ref[:, cs].astype(F32)
            sa = _sigmoid(a)
            da_ref[:, cs] = (ds * bb * (sa * (1.0 + a * (1.0 - sa)))).astype(BF16)
            db_ref[:, cs] = (ds * (a * sa)).astype(BF16)

    wide = jax.ShapeDtypeStruct((T, F), BF16)
    return _call(
        hidden_body, name=name, grid=(T // tm,), out_shape=(wide, wide),
        in_specs=[_rows(tm, D), _rows(tm, F), _rows(tm, F), _resident((F, D))],
        out_specs=(_rows(tm, F), _rows(tm, F)), args=(dxo, a, b, wd), carry=carry)


def ffn_bwd_input(dxo, x, gain, da, db, wg, wu, name, carry=None):
    T, D = x.shape
    F = wg.shape[0]
    tm = min(ROW_TILE, T)

    def input_body(dxo_ref, x_ref, g_ref, da_ref, db_ref, wg_ref, wu_ref, dx_ref, dg_ref):
        dh = _dot(da_ref[...], wg_ref[...]) + _dot(db_ref[...], wu_ref[...])
        dx, dgt = _rms_bwd(x_ref[...], g_ref[...], dh)
        dx_ref[...] = dxo_ref[...] + dx

        @pl.when(pl.program_id(0) == 0)
        def _():
            dg_ref[...] = jnp.zeros_like(dg_ref)
        dg_ref[...] += jnp.sum(dgt, axis=0, keepdims=True)

    return _call(
        input_body, name=name, grid=(T // tm,),
        out_shape=(jax.ShapeDtypeStruct((T, D), F32), jax.ShapeDtypeStruct((1, D), F32)),
        in_specs=[_rows(tm, D), _rows(tm, D), _resident((1, D)), _rows(tm, F), _rows(tm, F), _resident((F, D)),
                  _resident((F, D))],
        out_specs=(_rows(tm, D), _acc_spec((1, D))), args=(dxo, x, gain, da, db, wg, wu), carry=carry)


def wgrad(a, b, name, carry=None, b_scale=None):
    T, M = a.shape
    N = b.shape[1]
    fixed = M * N * (4 + 2)
    per_row = 2 * (M * a.dtype.itemsize + N * b.dtype.itemsize)
    tk = ROW_TILE
    while fixed + 2 * tk * per_row <= WGRAD_VMEM and 2 * tk <= WGRAD_TILE_ROWS:
        tk *= 2
    tk = min(tk, T)
    nk = T // tk

    def body(a_ref, b_ref, o_ref, acc_ref):
        k = pl.program_id(0)
        bt = b_ref[...] if b_scale is None else b_scale * b_ref[...]
        part = _tn(a_ref[...].astype(BF16), bt.astype(BF16))

        @pl.when(k == 0)
        def _():
            acc_ref[...] = part

        @pl.when(k > 0)
        def _():
            acc_ref[...] += part

        @pl.when(k == nk - 1)
        def _():
            o_ref[...] = acc_ref[...].astype(BF16)

    (out,), carried = _call(
        body, name=name, grid=(nk,), out_shape=(jax.ShapeDtypeStruct((M, N), BF16),),
        in_specs=[_rows(tk, M), _rows(tk, N)], out_specs=(_resident((M, N)),),
        scratch_shapes=[pltpu.VMEM((M, N), F32)], args=(a, b), carry=carry)
    return out, carried


def wgrad_stacked(pieces, b, name, total_rows, at, into=None, moves=None):
    T, N = b.shape
    n = len(pieces)
    widths = [p.shape[1] for p in pieces]
    offs = [sum(widths[:i]) for i in range(n)]
    M = sum(widths)
    tk = ROW_TILE
    while 2 * tk * M <= WGRAD_TILE_ELEMS and 2 * tk <= WGRAD_TILE_ROWS:
        tk *= 2
    tk = min(tk, T)
    nk = T // tk
    moves = moves or [(0, 0, M)]
    n_in = n + 1 + (into is not None)

    def body(*refs):
        a_refs, b_ref = refs[:n], refs[n]
        o_ref, acc_ref, stage, sem = refs[n_in:]
        k = pl.program_id(0)

        @pl.when(k == 0)
        def _():
            acc_ref[...] = jnp.zeros_like(acc_ref)
        bt = b_ref[...].astype(BF16)
        for a_ref, off, width in zip(a_refs, offs, widths):
            acc_ref[pl.ds(off, width), :] += _tn(a_ref[...].astype(BF16), bt)

        @pl.when(k == nk - 1)
        def _():
            for to, start, rows in moves:
                stage[pl.ds(to, rows), :] = acc_ref[pl.ds(start, rows), :].astype(BF16)
            cp = pltpu.make_async_copy(stage, o_ref.at[pl.ds(at, M)], sem)
            cp.start()
            cp.wait()

    return pl.pallas_call(
        body, name=name, grid=(nk,), out_shape=jax.ShapeDtypeStruct((total_rows, N), BF16),
        in_specs=[_rows(tk, width) for width in widths] + [_rows(tk, N)] + [_ANY] * (into is not None), out_specs=_ANY,
        scratch_shapes=[pltpu.VMEM((M, N), F32), pltpu.VMEM((M, N), BF16), pltpu.SemaphoreType.DMA],
        input_output_aliases={n + 1: 0} if into is not None else {}, compiler_params=_params("arbitrary"),
    )(*pieces, b, *([into] if into is not None else []))


def _w_in_rows(D):
    KV = N_KV_HEADS * HEAD_DIM
    return 0, D, 2 * D, 3 * D, 3 * D + KV, 3 * D + 2 * KV, 4 * D + 2 * KV


def mix_in_fwd(x, gain, wint, wq, gate_b, cos, sin_signed, name, carry=None):
    T, D = x.shape
    KV = N_KV_HEADS * HEAD_DIM
    tm = min(ROW_TILE, T)
    o_ga, o_gb, _, o_k, o_v, o_gc, o_gt = _w_in_rows(D)

    def body(x_ref, g_ref, w_ref, wq_ref, gb_ref, cos_ref, sin_ref,
             h_ref, ga_ref, gb_out_ref, u0_ref, q_ref, sgc_ref, sgt_ref, k_ref, v_ref):
        h = _rms_fwd(x_ref[...], g_ref[...]).astype(BF16)
        h_ref[...] = h
        cos, sin = cos_ref[...], sin_ref[...]
        ga = _nt(h, w_ref[pl.ds(o_ga, D), :])
        gb = _nt(h, w_ref[pl.ds(o_gb, D), :])
        ga_ref[...] = ga.astype(BF16)
        gb_out_ref[...] = gb.astype(BF16)
        u0_ref[...] = (ga * _sigmoid(gb)).astype(BF16)
        q = _nt(h, wq_ref[...])
        q_ref[...] = _rope_fwd(q, cos, sin).astype(BF16)
        gc = _nt(h, w_ref[pl.ds(o_gc, D), :]) + gb_ref[:, pl.ds(0, D)]
        sgc_ref[...] = _sigmoid(gc).astype(BF16)
        gt = _nt(h, w_ref[pl.ds(o_gt, D), :]) + gb_ref[:, pl.ds(D, D)]
        sgt_ref[...] = _sigmoid(gt).astype(BF16)
        k = _nt(h, w_ref[pl.ds(o_k, KV), :])
        k_ref[...] = _rope_fwd(k, cos, sin).astype(BF16)
        v_ref[...] = _nt(h, w_ref[pl.ds(o_v, KV), :]).astype(BF16)

    big = jax.ShapeDtypeStruct((T, D), BF16)
    small = jax.ShapeDtypeStruct((T, KV), BF16)
    return _call(
        body, name=name, grid=(T // tm,),
        out_shape=(big, big, big, big, big, big, big, small, small),
        in_specs=[_rows(tm, D), _resident((1, D)), _resident(wint.shape), _resident(wq.shape), _resident((1, 2 * D)),
                  _rows(tm, LANES), _rows(tm, LANES)],
        out_specs=(_rows(tm, D),) * 7 + (_rows(tm, KV),) * 2,
        args=(x, gain, wint, wq, gate_b, cos, sin_signed), carry=carry)


def _from_prev(rows):
    qi = lax.broadcasted_iota(jnp.int32, (rows, WINDOW), 0) % WINDOW
    return lax.broadcasted_iota(jnp.int32, (rows, WINDOW), 1) > qi


def _fold(x, g, from_prev):
    lo = 2 * WINDOW * g
    return jnp.where(from_prev, x[:, lo:lo + WINDOW], x[:, lo + WINDOW:lo + 2 * WINDOW])


def _unfold(folded, from_prev):
    zero = jnp.zeros_like(folded[0])
    parts = []
    for x in folded:
        parts += [jnp.where(from_prev, x, zero), jnp.where(from_prev, zero, x)]
    return jnp.concatenate(parts, axis=1)


def _kv_lane_head(rows, width):
    return lax.broadcasted_iota(jnp.int32, (rows, width), 1) // HEAD_DIM


def _block_diag(win):
    head = _kv_lane_head(*win.shape)
    zero = jnp.zeros_like(win)
    return jnp.concatenate([jnp.where(head == g, win, zero) for g in range(N_KV_HEADS)], axis=0)


def _diag_blocks_sum(bd, keys):
    head = _kv_lane_head(keys, bd.shape[1])
    out = jnp.zeros((keys, bd.shape[1]), F32)
    for g in range(N_KV_HEADS):
        out = jnp.where(head == g, bd[g * keys:(g + 1) * keys], out)
    return out


def _kv_windows(k_ref, kh_ref, v_ref, vh_ref, j):
    rows = pl.ds(j * WINDOW, WINDOW)
    if j == 0:
        kprev, vprev = kh_ref[...], vh_ref[...]
    else:
        prev = pl.ds((j - 1) * WINDOW, WINDOW)
        kprev, vprev = k_ref[prev, :], v_ref[prev, :]
    return jnp.concatenate([kprev, k_ref[rows, :]], axis=0), jnp.concatenate([vprev, v_ref[rows, :]], axis=0)


def _stack_slots(ref, j, group, KV):
    rows = pl.ds(j * WINDOW, WINDOW)
    return jnp.concatenate([ref[rows, pl.ds(KV * hh, KV)] for hh in range(group)], axis=0)


def _attn_exp(qs, kbd, sink_ref, from_prev, no_prev):
    s = _nt(qs, kbd)
    if no_prev is not None:
        qi = lax.broadcasted_iota(jnp.int32, from_prev.shape, 0) % WINDOW
        absent = lax.broadcasted_iota(jnp.int32, from_prev.shape, 1) > jnp.where(no_prev, qi, WINDOW)
    out = []
    for g in range(N_KV_HEADS):
        sg = _fold(s, g, from_prev) * (HEAD_DIM ** -0.5)
        if no_prev is not None:
            sg = jnp.where(absent, NEG_INF, sg)
        sink = sink_ref[g]
        m = jnp.maximum(jnp.max(sg, axis=-1, keepdims=True), sink)
        out.append((jnp.exp(sg - m), jnp.exp(sink - m)))
    return out


def _spread_over_heads(cols, rows, KV):
    head = _kv_lane_head(rows, KV)
    out = jnp.zeros((rows, KV), F32)
    for g, col in enumerate(cols):
        out = jnp.where(head == g, col, out)
    return out


def _halo_rows_spec(tq, width, sub):
    return pl.BlockSpec((sub, width), lambda i: (jnp.maximum(i * (tq // sub) - 1, 0), 0))


def attn_fwd(q, k, v, sink_col, name, carry=None):
    T, D = q.shape
    KV = k.shape[1]
    group = D // KV
    tq = min(ROW_TILE, T)
    nsub = tq // WINDOW
    rows, wide = group * WINDOW, N_KV_HEADS * 2 * WINDOW

    def body(q_ref, k_ref, kh_ref, v_ref, vh_ref, sink_ref, o_ref):
        from_prev = _from_prev(rows)
        head = _kv_lane_head(wide, KV)
        block = lax.broadcasted_iota(jnp.int32, head.shape, 0) // (2 * WINDOW)
        ones_bd = jnp.where(head == block, 1.0, 0.0).astype(BF16)
        for j in range(nsub):
            k_win, v_win = _kv_windows(k_ref, kh_ref, v_ref, vh_ref, j)
            parts = _attn_exp(_stack_slots(q_ref, j, group, KV), _block_diag(k_win), sink_ref, from_prev,
                              pl.program_id(0) == 0 if j == 0 else None)
            p = _unfold([pg.astype(BF16) for pg, _ in parts], from_prev)
            both = _dot(p, jnp.concatenate([_block_diag(v_win), ones_bd], axis=1))
            denom = both[:, KV:] + _spread_over_heads([es for _, es in parts], rows, KV)
            out = (both[:, :KV] / denom).astype(BF16)
            for hh in range(group):
                o_ref[pl.ds(j * WINDOW, WINDOW), pl.ds(KV * hh, KV)] = out[hh * WINDOW:(hh + 1) * WINDOW]

    (o,), carried = _call(
        body, name=name, grid=(T // tq,),
        out_shape=(jax.ShapeDtypeStruct((T, D), BF16),),
        in_specs=[_rows(tq, D), _rows(tq, KV), _halo_rows_spec(tq, KV, WINDOW), _rows(tq, KV),
                  _halo_rows_spec(tq, KV, WINDOW), _resident(sink_col.shape)],
        out_specs=(_rows(tq, D),), args=(q, k, k, v, v, sink_col), carry=carry)
    return o, carried


def attn_bwd(q, k, v, do, sink_col, cos, sin_signed, name):
    T, D = q.shape
    KV = k.shape[1]
    group = D // KV
    tq = min(ROW_TILE, T)
    nsub = tq // WINDOW
    nt = T // tq
    scale = HEAD_DIM ** -0.5
    rows, wide = group * WINDOW, N_KV_HEADS * 2 * WINDOW

    def rev(i):
        return nt - 1 - i

    def body(q_ref, k_ref, kh_ref, v_ref, vh_ref, do_ref, sink_ref, cos_ref, sin_ref,
             dq_ref, dk_ref, dv_ref, dsink_ref, dq_acc, dk_acc, dv_acc, carry_k, carry_v):
        i = pl.program_id(0)

        @pl.when(i == 0)
        def _():
            carry_k[...] = jnp.zeros_like(carry_k)
            carry_v[...] = jnp.zeros_like(carry_v)
            dsink_ref[...] = jnp.zeros_like(dsink_ref)

        dk_acc[...] = jnp.zeros_like(dk_acc)
        dv_acc[...] = jnp.zeros_like(dv_acc)
        from_prev = _from_prev(rows)
        lane = lax.broadcasted_iota(jnp.int32, (1, LANES), 1)
        for j in range(nsub):
            k_win, v_win = _kv_windows(k_ref, kh_ref, v_ref, vh_ref, j)
            kbd, vbd = _block_diag(k_win), _block_diag(v_win)
            qs, dos = _stack_slots(q_ref, j, group, KV), _stack_slots(do_ref, j, group, KV)
            dp = _nt(dos, vbd)
            probs16, ds16 = [], []
            for g, (pg, es) in enumerate(_attn_exp(qs, kbd, sink_ref, from_prev, rev(i) == 0 if j == 0 else None)):
                inv = 1.0 / (jnp.sum(pg, axis=-1, keepdims=True) + es)
                probs = pg * inv
                dpg = _fold(dp, g, from_prev)
                delta = jnp.sum(probs * dpg, axis=-1, keepdims=True)
                probs16.append(probs.astype(BF16))
                ds16.append((probs * (dpg - delta) * scale).astype(BF16))
                dsk = -(es * inv * delta)
                for hh in range(group):
                    tot = jnp.sum(dsk[hh * WINDOW:(hh + 1) * WINDOW], axis=0, keepdims=True)
                    dsink_ref[pl.ds(hh, 1), :] += jnp.where(lane == g, tot, 0.0)
            ds = _unfold(ds16, from_prev)
            dqs = _dot(ds, kbd)
            for hh in range(group):
                dq_acc[pl.ds(j * WINDOW, WINDOW), pl.ds(KV * hh, KV)] = dqs[hh * WINDOW:(hh + 1) * WINDOW]
            keys = pl.ds(j * WINDOW, 2 * WINDOW)
            dk_acc[keys, :] += _diag_blocks_sum(_tn(ds, qs), 2 * WINDOW)
            dv_acc[keys, :] += _diag_blocks_sum(_tn(_unfold(probs16, from_prev), dos), 2 * WINDOW)

        tail = pl.ds(tq, WINDOW)
        dk_acc[tail, :] += carry_k[...]
        dv_acc[tail, :] += carry_v[...]
        carry_k[...] = dk_acc[pl.ds(0, WINDOW), :]
        carry_v[...] = dv_acc[pl.ds(0, WINDOW), :]
        cos, sin = cos_ref[...], sin_ref[...]
        dq_ref[...] = _rope_bwd(dq_acc[...], cos, sin).astype(BF16)
        dk_ref[...] = _rope_bwd(dk_acc[pl.ds(WINDOW, tq), :], cos, sin).astype(BF16)
        dv_ref[...] = dv_acc[pl.ds(WINDOW, tq), :].astype(BF16)

    def rrows(n):
        return pl.BlockSpec((tq, n), lambda i: (rev(i), 0))

    def rhalo(n):
        return pl.BlockSpec((WINDOW, n), lambda i: (jnp.maximum(rev(i) * nsub - 1, 0), 0))

    return pl.pallas_call(
        body, name=name, grid=(nt,),
        out_shape=(jax.ShapeDtypeStruct((T, D), BF16), jax.ShapeDtypeStruct((T, KV), BF16),
                   jax.ShapeDtypeStruct((T, KV), BF16), jax.ShapeDtypeStruct((SUBLANES, LANES), F32)),
        in_specs=[rrows(D), rrows(KV), rhalo(KV), rrows(KV), rhalo(KV), rrows(D), _resident(sink_col.shape),
                  rrows(LANES), rrows(LANES)],
        out_specs=(rrows(D), rrows(KV), rrows(KV), _acc_spec((SUBLANES, LANES))),
        scratch_shapes=[pltpu.VMEM((tq, D), F32), pltpu.VMEM((WINDOW + tq, KV), F32), pltpu.VMEM((WINDOW + tq, KV), F32),
                        pltpu.VMEM((WINDOW, KV), F32), pltpu.VMEM((WINDOW, KV), F32)],
        compiler_params=_params("arbitrary"),
    )(q, k, k, v, v, do, sink_col, cos, sin_signed)


def _ln_stats(u):
    mu = jnp.mean(u, axis=-1, keepdims=True)
    d = u - mu
    rstd = lax.rsqrt(jnp.mean(d * d, axis=-1, keepdims=True) + LN_EPS)
    return d * rstd, rstd


def _lag_taps(b, K):
    return [(a, K - 1 - (SUBLANES * a + b)) for a in range(-(-K // SUBLANES)) if SUBLANES * a + b <= K - 1]


def _conv_chunks(tm, D, chunk):
    def rows(c, carry):
        r0 = pl.multiple_of(c * CONV_ROWS, CONV_ROWS)
        for l0 in range(0, D, CONV_LANES):
            chunk(r0, pl.ds(l0, CONV_LANES))
        return carry
    lax.fori_loop(0, tm // CONV_ROWS, rows, 0)


def _conv_causal(buf, w_ref, bias_ref, out_ref, tm, D, K):
    def chunk(r0, lanes):
        acc = jnp.broadcast_to(bias_ref[:, lanes], (CONV_ROWS, CONV_LANES))
        for b in range(SUBLANES):
            y = None
            for a, k in _lag_taps(b, K):
                start = pl.multiple_of(r0 + CONV_HALO - SUBLANES * (a + 1), SUBLANES)
                t = buf[pl.ds(start, CONV_ROWS + SUBLANES), lanes] * w_ref[pl.ds(k, 1), lanes]
                y = t if y is None else y + t
            acc = acc + y[SUBLANES - b:SUBLANES - b + CONV_ROWS]
        out_ref[pl.ds(r0, CONV_ROWS), lanes] = acc
    _conv_chunks(tm, D, chunk)


def _conv_anticausal(dbuf, w_ref, out_ref, tm, D, K):
    def chunk(r0, lanes):
        acc = jnp.zeros((CONV_ROWS, CONV_LANES), F32)
        for b in range(SUBLANES):
            y = None
            for a, k in _lag_taps(b, K):
                start = pl.multiple_of(r0 + SUBLANES * a, SUBLANES)
                t = dbuf[pl.ds(start, CONV_ROWS + SUBLANES), lanes] * w_ref[pl.ds(k, 1), lanes]
                y = t if y is None else y + t
            acc = acc + y[b:b + CONV_ROWS]
        out_ref[pl.ds(r0, CONV_ROWS), lanes] = acc
    _conv_chunks(tm, D, chunk)


def _conv_tap_grads(dbuf, ubuf, acc_ref, tm, D, K):
    reach = SUBLANES * (-(-K // SUBLANES) - 1)

    def chunk(r0, lanes):
        d = dbuf[pl.ds(r0, CONV_ROWS), lanes]
        around = ubuf[pl.ds(pl.multiple_of(r0 + CONV_HALO - reach - SUBLANES, SUBLANES), CONV_ROWS + reach + SUBLANES), lanes]
        for b in range(SUBLANES):
            shifted = around[SUBLANES - b:SUBLANES - b + CONV_ROWS + reach]
            for a, k in _lag_taps(b, K):
                prod = d * shifted[reach - SUBLANES * a:reach - SUBLANES * a + CONV_ROWS]
                part = prod[0:SUBLANES]
                for i in range(1, CONV_ROWS // SUBLANES):
                    part = part + prod[SUBLANES * i:SUBLANES * (i + 1)]
                acc_ref[k, :, lanes] += part
    _conv_chunks(tm, D, chunk)


def mix_out_fwd(x, u0, o, sgc, sgt, dw_w, dw_b, ln_g, ln_b, wcp, wo, wout, name):
    T, D = x.shape
    tm = min(ROW_TILE, T)
    K = dw_w.shape[0]

    def body(x_ref, u_ref, uh_ref, o_ref, sgc_ref, sgt_ref, w_ref, b_ref, lg_ref, lb_ref, wcp_ref, wo_ref, wout_ref,
             x2_ref, u1_ref, co_ref, ao_ref, mg_ref, buf, conv):
        keep = (pl.program_id(0) > 0).astype(F32)
        buf[pl.ds(0, CONV_HALO), :] = uh_ref[...].astype(F32) * keep
        buf[pl.ds(CONV_HALO, tm), :] = u_ref[...].astype(F32)
        _conv_causal(buf, w_ref, b_ref, conv, tm, D, K)
        acc = conv[...]
        u1_ref[...] = acc.astype(BF16)
        xhat, _ = _ln_stats(acc)
        u2 = xhat * lg_ref[...] + lb_ref[...]
        u3 = (u2 * _sigmoid(u2)).astype(BF16)
        co = _dot(u3, wcp_ref[...])
        ao = _dot(o_ref[...], wo_ref[...])
        co_ref[...] = co.astype(BF16)
        ao_ref[...] = ao.astype(BF16)
        merged = (sgc_ref[...].astype(F32) * co + sgt_ref[...].astype(F32) * ao).astype(BF16)
        mg_ref[...] = merged
        x2_ref[...] = x_ref[...] + _dot(merged, wout_ref[...])

    big = jax.ShapeDtypeStruct((T, D), BF16)
    vec = _resident((1, D))
    return pl.pallas_call(
        body, name=name, grid=(T // tm,),
        out_shape=(jax.ShapeDtypeStruct((T, D), F32), big, big, big, big),
        in_specs=[_rows(tm, D), _rows(tm, D), _halo_rows_spec(tm, D, CONV_HALO), _rows(tm, D), _rows(tm, D), _rows(tm, D),
                  _resident((K, D)), vec, vec, vec, _resident((D, D)), _resident((D, D)), _resident((D, D))],
        out_specs=(_rows(tm, D),) * 5,
        scratch_shapes=[pltpu.VMEM((CONV_HALO + tm, D), F32), pltpu.VMEM((tm, D), F32)],
        compiler_params=_params("arbitrary"),
    )(x, u0, u0, o, sgc, sgt, dw_w, dw_b, ln_g, ln_b, wcp, wo, wout)


def mix_out_bwd(dx2, u1, co, ao, sgc, sgt, ln_g, ln_b, wcp, wo, wout, name, carry=None):
    T, D = dx2.shape
    tm = min(ROW_TILE, T)

    def body(dx_ref, u1_ref, co_ref, ao_ref, sgc_ref, sgt_ref, lg_ref, lb_ref, wcp_ref, wo_ref, wout_ref,
             dgc_ref, dgt_ref, do_ref, du1_ref, dco_ref, dao_ref, u3_ref, sums_ref):
        dm = _nt(dx_ref[...].astype(BF16), wout_ref[...])
        sgc, sgt = sgc_ref[...].astype(F32), sgt_ref[...].astype(F32)
        dco = (dm * sgc).astype(BF16)
        dao = (dm * sgt).astype(BF16)
        dgc = dm * co_ref[...].astype(F32) * sgc * (1.0 - sgc)
        dgt = dm * ao_ref[...].astype(F32) * sgt * (1.0 - sgt)
        dco_ref[...] = dco
        dao_ref[...] = dao
        dgc_ref[...] = dgc.astype(BF16)
        dgt_ref[...] = dgt.astype(BF16)
        do_ref[...] = _nt(dao, wo_ref[...]).astype(BF16)
        du3 = _nt(dco, wcp_ref[...])
        xhat, rstd = _ln_stats(u1_ref[...].astype(F32))
        g = lg_ref[...]
        u2 = xhat * g + lb_ref[...]
        su = _sigmoid(u2)
        u3_ref[...] = (u2 * su).astype(BF16)
        du2 = du3 * (su * (1.0 + u2 * (1.0 - su)))
        dxh = du2 * g
        du1 = rstd * (dxh - jnp.mean(dxh, axis=-1, keepdims=True) - xhat * jnp.mean(dxh * xhat, axis=-1, keepdims=True))
        du1_ref[...] = du1.astype(BF16)

        @pl.when(pl.program_id(0) == 0)
        def _():
            sums_ref[...] = jnp.zeros_like(sums_ref)
        for r, val in enumerate((dgc, dgt, du2 * xhat, du2, du1)):
            sums_ref[pl.ds(r, 1), :] += jnp.sum(val, axis=0, keepdims=True)

    big = jax.ShapeDtypeStruct((T, D), BF16)
    vec = _resident((1, D))
    return _call(
        body, name=name, grid=(T // tm,),
        out_shape=(big,) * 7 + (jax.ShapeDtypeStruct((8, D), F32),),
        in_specs=[_rows(tm, D)] * 6 + [vec, vec, _resident((D, D)), _resident((D, D)), _resident((D, D))],
        out_specs=(_rows(tm, D),) * 7 + (_acc_spec((8, D)),),
        args=(dx2, u1, co, ao, sgc, sgt, ln_g, ln_b, wcp, wo, wout), carry=carry)


def conv_bwd(du1, u0, ga, gb, dw_w, name, carry=None):
    T, D = du1.shape
    tm = min(ROW_TILE, T)
    nt = T // tm
    K = dw_w.shape[0]
    per = tm // CONV_HALO

    def body(d_ref, dn_ref, u_ref, uh_ref, ga_ref, gb_ref, w_ref, dga_ref, dgb_ref, dw_ref, dbuf, ubuf, du0_buf, taps):
        i = pl.program_id(0)
        dbuf[pl.ds(0, tm), :] = d_ref[...].astype(F32)
        dbuf[pl.ds(tm, CONV_HALO), :] = dn_ref[...].astype(F32) * (i < nt - 1).astype(F32)
        ubuf[pl.ds(0, CONV_HALO), :] = uh_ref[...].astype(F32) * (i > 0).astype(F32)
        ubuf[pl.ds(CONV_HALO, tm), :] = u_ref[...].astype(F32)

        @pl.when(i == 0)
        def _():
            taps[...] = jnp.zeros_like(taps)

        _conv_anticausal(dbuf, w_ref, du0_buf, tm, D, K)
        _conv_tap_grads(dbuf, ubuf, taps, tm, D, K)
        du0 = du0_buf[...]
        ga, gb = ga_ref[...].astype(F32), gb_ref[...].astype(F32)
        sg = _sigmoid(gb)
        dga_ref[...] = (du0 * sg).astype(BF16)
        dgb_ref[...] = (du0 * ga * sg * (1.0 - sg)).astype(BF16)

        @pl.when(i == nt - 1)
        def _():
            for k in range(K):
                dw_ref[pl.ds(k, 1), :] = jnp.sum(taps[k], axis=0, keepdims=True)

    nxt = pl.BlockSpec((CONV_HALO, D), lambda i: (jnp.minimum((i + 1) * per, nt * per - 1), 0))
    big = jax.ShapeDtypeStruct((T, D), BF16)
    return _call(
        body, name=name, grid=(nt,),
        out_shape=(big, big, jax.ShapeDtypeStruct((K, D), F32)),
        in_specs=[_rows(tm, D), nxt, _rows(tm, D), _halo_rows_spec(tm, D, CONV_HALO), _rows(tm, D), _rows(tm, D),
                  _resident((K, D))],
        out_specs=(_rows(tm, D), _rows(tm, D), _acc_spec((K, D))),
        scratch_shapes=[pltpu.VMEM((tm + CONV_HALO, D), F32), pltpu.VMEM((CONV_HALO + tm, D), F32), pltpu.VMEM((tm, D), F32),
                        pltpu.VMEM((K, SUBLANES, D), F32)],
        args=(du1, du1, u0, u0, ga, gb, dw_w), carry=carry)


def mix_in_bwd(dx2, x, gain, wint, wq, pieces, name):
    T, D = x.shape
    tm = min(ROW_TILE, T)
    widths = [p.shape[1] for p in pieces]
    offs = _w_in_rows(D)

    def body(dx2_ref, x_ref, g_ref, w_ref, wq_ref, *rest):
        piece_refs, (dx_ref, dg_ref) = rest[:len(pieces)], rest[len(pieces):]
        dh = None
        for n, (p_ref, off, w) in enumerate(zip(piece_refs, offs, widths)):
            t = _dot(p_ref[...], wq_ref[...] if n == 2 else w_ref[pl.ds(off, w), :])
            dh = t if dh is None else dh + t
        dx, dgt = _rms_bwd(x_ref[...], g_ref[...], dh)
        dx_ref[...] = dx2_ref[...] + dx

        @pl.when(pl.program_id(0) == 0)
        def _():
            dg_ref[...] = jnp.zeros_like(dg_ref)
        dg_ref[...] += jnp.sum(dgt, axis=0, keepdims=True)

    return pl.pallas_call(
        body, name=name, grid=(T // tm,),
        out_shape=(jax.ShapeDtypeStruct((T, D), F32), jax.ShapeDtypeStruct((1, D), F32)),
        in_specs=[_rows(tm, D), _rows(tm, D), _resident((1, D)), _resident(wint.shape), _resident(wq.shape)]
        + [_rows(tm, w) for w in widths],
        out_specs=(_rows(tm, D), _acc_spec((1, D))),
        compiler_params=_params("arbitrary"),
    )(dx2, x, gain, wint, wq, *pieces)


def rope_tables(positions):
    half = HEAD_DIM // 2
    inv_freq = ROPE_THETA ** (-jnp.arange(half, dtype=F32) / half)
    ang = positions.astype(F32)[:, None] * inv_freq
    cos, sin = jnp.cos(ang), jnp.sin(ang)
    reps = LANES // HEAD_DIM
    return jnp.tile(jnp.concatenate([cos, cos], axis=-1), (1, reps)), jnp.tile(jnp.concatenate([-sin, sin], axis=-1), (1, reps))


def _place():
    return lax.axis_index("x"), lax.axis_index("y"), lax.axis_index("c")


def all_gather(blocks, name):
    n = len(blocks)
    send = gather_send(blocks)
    forward = gather_forward(send.out_shapes)
    n_sems = len(send.sems)

    def body(*refs):
        ins, outs, sems = refs[:n], refs[n:2 * n], refs[2 * n:]
        send.start(ins, outs, sems[:n_sems])
        send.finish(ins, outs, sems[:n_sems])
        forward.start((), outs, sems[n_sems:])
        forward.finish((), outs, sems[n_sems:])

    return pl.pallas_call(
        body, name=name, out_shape=tuple(send.out_shapes), in_specs=[_ANY] * n, out_specs=(_ANY,) * n,
        scratch_shapes=list(send.sems) + list(forward.sems),
    )(*blocks)


def _chips_across(x, y):
    return [(1 - x, y), (x, 1 - y), (1 - x, 1 - y)]


def _dev_index(x, y, c):
    return 4 * x + 2 * y + c


def gather_send(blocks):
    n = len(blocks)

    def copies(in_refs, out_refs, sems):
        send, recv, local = sems
        x, y, c = _place()
        targets = [(x, y, 1 - c)] + [(*chip, c) for chip in _chips_across(x, y)]
        outgoing, incoming, mine = [], [], []
        for i, (x_ref, out_ref) in enumerate(zip(in_refs, out_refs)):
            for k, t in enumerate(targets):
                pair = dict(send_sem=send.at[4 * i + k], recv_sem=recv.at[4 * i + k], device_id=t, device_id_type=MESH)
                outgoing.append(pltpu.make_async_remote_copy(src_ref=x_ref, dst_ref=out_ref.at[_dev_index(x, y, c)], **pair))
                incoming.append(pltpu.make_async_remote_copy(src_ref=x_ref, dst_ref=out_ref.at[_dev_index(*t)], **pair))
            mine.append(pltpu.make_async_copy(x_ref, out_ref.at[_dev_index(x, y, c)], local.at[i]))
        return outgoing, incoming, mine

    def start(*refs):
        outgoing, _, mine = copies(*refs)
        for cp in mine + outgoing:
            cp.start()

    def finish(*refs):
        outgoing, incoming, mine = copies(*refs)
        for cp in incoming:
            cp.wait_recv()
        for cp in outgoing:
            cp.wait_send()
        for cp in mine:
            cp.wait()

    return Carry(ins=tuple(blocks), out_shapes=tuple(jax.ShapeDtypeStruct((N_DEV,) + b.shape, b.dtype) for b in blocks),
                 aliases={}, sems=(pltpu.SemaphoreType.DMA((4 * n,)), pltpu.SemaphoreType.DMA((4 * n,)),
                                   pltpu.SemaphoreType.DMA((n,))), start=start, finish=finish)


def gather_forward(gathered):
    n = len(gathered)

    def copies(in_refs, out_refs, sems):
        send, recv = sems
        x, y, c = _place()
        outgoing, incoming = [], []
        for i, buf in enumerate(out_refs):
            for k, chip in enumerate(_chips_across(x, y)):
                pair = dict(send_sem=send.at[3 * i + k], recv_sem=recv.at[3 * i + k], device_id=(x, y, 1 - c),
                            device_id_type=MESH)
                rows = buf.at[_dev_index(*chip, c)]
                outgoing.append(pltpu.make_async_remote_copy(src_ref=rows, dst_ref=rows, **pair))
                theirs = buf.at[_dev_index(*chip, 1 - c)]
                incoming.append(pltpu.make_async_remote_copy(src_ref=theirs, dst_ref=theirs, **pair))
        return outgoing, incoming

    def start(*refs):
        for cp in copies(*refs)[0]:
            cp.start()

    def finish(*refs):
        outgoing, incoming = copies(*refs)
        for cp in incoming:
            cp.wait_recv()
        for cp in outgoing:
            cp.wait_send()

    return Carry(ins=tuple(gathered), out_shapes=tuple(jax.ShapeDtypeStruct(g.shape, g.dtype) for g in gathered),
                 aliases={i: i for i in range(n)},
                 sems=(pltpu.SemaphoreType.DMA((3 * n,)), pltpu.SemaphoreType.DMA((3 * n,))), start=start, finish=finish)


def gather_whole(blocks):
    send = gather_send(blocks)
    forward = gather_forward(send.out_shapes)
    n = len(send.sems)

    def relay(ins, outs, sems):
        send.finish(ins, outs, sems[:n])
        forward.start((), outs, sems[n:])

    return Carry(ins=send.ins, out_shapes=send.out_shapes, aliases={}, sems=send.sems + forward.sems,
                 start=lambda ins, outs, sems: send.start(ins, outs, sems[:n]), relay=relay,
                 finish=lambda ins, outs, sems: forward.finish((), outs, sems[n:]))


def compose(*carries):
    def split(refs, count):
        out, at = [], 0
        for c in carries:
            n = count(c)
            out.append(refs[at:at + n])
            at += n
        return out

    def each(stage):
        def run(ins, outs, sems):
            parts = zip(carries, split(ins, lambda c: len(c.ins)), split(outs, lambda c: len(c.out_shapes)),
                        split(sems, lambda c: len(c.sems)))
            for c, i, o, s in parts:
                if getattr(c, stage) is not None:
                    getattr(c, stage)(i, o, s)
        return run

    aliases, n_in, n_out = {}, 0, 0
    for c in carries:
        aliases.update({n_in + i: n_out + o for i, o in c.aliases.items()})
        n_in += len(c.ins)
        n_out += len(c.out_shapes)
    return Carry(ins=sum((tuple(c.ins) for c in carries), ()), out_shapes=sum((tuple(c.out_shapes) for c in carries), ()),
                 aliases=aliases, sems=sum((tuple(c.sems) for c in carries), ()), start=each("start"), finish=each("finish"),
                 relay=each("relay") if any(c.relay is not None for c in carries) else None)


def swap_halves(by_core):
    n = len(by_core)

    def copies(in_refs, out_refs, sems):
        send, recv = sems
        x, y, c = _place()
        return [pltpu.make_async_remote_copy(src_ref=a.at[:, 1 - c], dst_ref=r, send_sem=send.at[i], recv_sem=recv.at[i],
                                             device_id=(x, y, 1 - c), device_id_type=MESH)
                for i, (a, r) in enumerate(zip(in_refs, out_refs))]

    def start(*refs):
        for cp in copies(*refs):
            cp.start()

    def finish(*refs):
        for cp in copies(*refs):
            cp.wait()

    shapes = tuple(jax.ShapeDtypeStruct((a.shape[0],) + a.shape[2:], a.dtype) for a in by_core)
    return Carry(ins=tuple(by_core), out_shapes=shapes, aliases={},
                 sems=(pltpu.SemaphoreType.DMA((n,)), pltpu.SemaphoreType.DMA((n,))), start=start, finish=finish)


def exchange_between_chips(by_chip):
    n = len(by_chip)

    def copies(in_refs, out_refs, sems):
        send, recv = sems
        x, y, c = _place()
        out = []
        for i, (s, r) in enumerate(zip(in_refs, out_refs)):
            for k, (tx, ty) in enumerate(_chips_across(x, y)):
                out.append(pltpu.make_async_remote_copy(
                    src_ref=s.at[2 * tx + ty], dst_ref=r.at[k], send_sem=send.at[3 * i + k], recv_sem=recv.at[3 * i + k],
                    device_id=(tx, ty, c), device_id_type=MESH))
        return out

    def start(*refs):
        for cp in copies(*refs):
            cp.start()

    def finish(*refs):
        for cp in copies(*refs):
            cp.wait()

    shapes = tuple(jax.ShapeDtypeStruct((3,) + a.shape[1:], a.dtype) for a in by_chip)
    return Carry(ins=tuple(by_chip), out_shapes=shapes, aliases={},
                 sems=(pltpu.SemaphoreType.DMA((3 * n,)), pltpu.SemaphoreType.DMA((3 * n,))), start=start, finish=finish)


def run_exchange(carry, name):
    n_in = len(carry.ins)
    n_out = len(carry.out_shapes)

    def body(*refs):
        parts = refs[:n_in], refs[n_in:n_in + n_out], refs[n_in + n_out:]
        carry.start(*parts)
        carry.finish(*parts)

    return pl.pallas_call(
        body, name=name, out_shape=tuple(carry.out_shapes), in_specs=[_ANY] * n_in, out_specs=(_ANY,) * n_out,
        scratch_shapes=list(carry.sems), input_output_aliases=dict(carry.aliases),
    )(*carry.ins)


def pair_sum(my_core, by_core, received, name):
    n = len(by_core)

    def body(core_ref, *refs):
        for a_ref, b_ref, o_ref in zip(refs[:n], refs[n:2 * n], refs[2 * n:]):
            o_ref[0] = (a_ref[0, 0].astype(F32) + b_ref[0].astype(F32)).astype(BF16)

    mine = [pl.BlockSpec((1, 1) + a.shape[2:], lambda j, core: (j, core[0], 0, 0)) for a in by_core]
    theirs = [pl.BlockSpec((1,) + r.shape[1:], lambda j, core: (j, 0, 0)) for r in received]
    return pl.pallas_call(
        body, name=name, out_shape=tuple(jax.ShapeDtypeStruct(r.shape, BF16) for r in received),
        grid_spec=pltpu.PrefetchScalarGridSpec(num_scalar_prefetch=1, grid=(by_core[0].shape[0],), in_specs=mine + theirs,
                                               out_specs=tuple(theirs)),
        compiler_params=_params("arbitrary"),
    )(my_core, *by_core, *received)


def _adamw_math(w, g, m, v):
    m = ADAM_B1 * m + (1.0 - ADAM_B1) * g
    v = ADAM_B2 * v + (1.0 - ADAM_B2) * (g * g)
    m_hat = m / (1.0 - ADAM_B1 ** ADAM_STEP)
    v_hat = v / (1.0 - ADAM_B2 ** ADAM_STEP)
    delta = -ADAM_LR * (m_hat / (jnp.sqrt(v_hat) + ADAM_EPS) + ADAM_WD * w)
    return delta, m, v


def adamw(my_chip, ws, gs, ms, vs, name, carry=None):
    n = len(ws)
    flat, widths = [], []
    for g in gs:
        parts = list(g) if isinstance(g, (tuple, list)) else [g]
        flat += parts
        widths.append(len(parts))
    c_ins = list(carry.ins) if carry else []
    c_outs = list(carry.out_shapes) if carry else []

    def body(chip_ref, *refs):
        w_refs, refs = refs[:n], refs[n:]
        g_refs, refs = refs[:len(flat)], refs[len(flat):]
        m_refs, v_refs, refs = refs[:n], refs[n:2 * n], refs[2 * n:]
        carried = (refs[:len(c_ins)], refs[len(c_ins) + 4 * n:len(c_ins) + 4 * n + len(c_outs)],
                   refs[len(c_ins) + 4 * n + len(c_outs):])
        outs = refs[len(c_ins):]
        if carry:
            carry.start(*carried)
        at = 0
        for t in range(n):
            if widths[t] == 1:
                g = g_refs[at][...]
            else:
                g = g_refs[at][0].astype(F32)
                for k in range(3):
                    g = g + g_refs[at + 1][k].astype(F32)
            at += widths[t]
            outs[4 * t][...] = g
            outs[4 * t + 1][...], outs[4 * t + 2][...], outs[4 * t + 3][...] = _adamw_math(
                w_refs[t][...], g, m_refs[t][...], v_refs[t][...])
        if carry:
            if carry.relay is not None:
                carry.relay(*carried)
            carry.finish(*carried)

    def whole(a):
        zeros = (0,) * a.ndim
        return pl.BlockSpec(a.shape, lambda i, chip: zeros, pipeline_mode=pl.Buffered(1))

    g_specs = []
    for g in gs:
        if isinstance(g, (tuple, list)):
            g_specs += [pl.BlockSpec((1,) + g[0].shape[1:], lambda i, chip: (chip[0], 0, 0), pipeline_mode=pl.Buffered(1)),
                        whole(g[1])]
        else:
            g_specs.append(whole(g))
    shapes, out_specs = [], []
    for w in ws:
        shapes += [jax.ShapeDtypeStruct(w.shape, F32)] * 4
        out_specs += [whole(w)] * 4
    res = pl.pallas_call(
        body, name=name, out_shape=tuple(shapes) + tuple(c_outs),
        grid_spec=pltpu.PrefetchScalarGridSpec(
            num_scalar_prefetch=1, grid=(1,),
            in_specs=[whole(w) for w in ws] + g_specs + [whole(a) for a in ms + vs] + [_ANY] * len(c_ins),
            out_specs=tuple(out_specs) + (_ANY,) * len(c_outs), scratch_shapes=list(carry.sems) if carry else []),
        compiler_params=_params("arbitrary"),
    )(my_chip, *ws, *flat, *ms, *vs, *c_ins)
    return [tuple(res[4 * t:4 * t + 4]) for t in range(n)], tuple(res[4 * n:])


def adamw_replicated(w, partials, m, v, name):
    def body(w_ref, p_ref, m_ref, v_ref, g_ref, d_ref, mo_ref, vo_ref):
        g = p_ref[0]
        for k in range(1, N_DEV):
            g = g + p_ref[k]
        g_ref[...] = g
        d_ref[...], mo_ref[...], vo_ref[...] = _adamw_math(w_ref[...], g, m_ref[...], v_ref[...])

    shape = jax.ShapeDtypeStruct(w.shape, F32)
    return pl.pallas_call(body, name=name, out_shape=(shape,) * 4, compiler_params=_params())(w, partials, m, v)


PACK_COLS = 1024
PACK_ROW_ALIGN = 16

REPLICATED = ("ffn1_norm", "mix_norm", "conv_dw_b", "conv_ln_g", "conv_ln_b", "ffn2_norm", "final_norm", "gate_b", "attn_sinks")
WEIGHT_ORDER = ("ffn1_norm", "ffn1_w_gate", "ffn1_w_up", "ffn1_w_down", "mix_norm", "w_in", "conv_dw_w", "conv_dw_b", "conv_ln_g",
                "conv_ln_b", "conv_w_proj", "attn_sinks", "attn_w_o", "gate_b", "w_out", "ffn2_norm", "ffn2_w_gate", "ffn2_w_up",
                "ffn2_w_down", "final_norm")


def _to_rows(flat, lead):
    n = flat.shape[-1]
    rows = -(-n // PACK_COLS)
    flat = jnp.pad(flat, [(0, 0)] * lead + [(0, rows * PACK_COLS - n)])
    return flat.reshape(flat.shape[:lead] + (rows, PACK_COLS))


def _pad_rows(a, axis):
    rows = a.shape[axis]
    pad = -rows % PACK_ROW_ALIGN
    widths = [(0, 0)] * a.ndim
    widths[axis] = (0, pad)
    return jnp.pad(a, widths)


def _from_rows(rows, shape):
    n = 1
    for s in shape:
        n *= s
    return rows.reshape(rows.shape[:-2] + (-1,))[..., :n].reshape(rows.shape[:-2] + tuple(shape))


def _heads_slot_major(rows):
    group = rows.shape[0] // (N_KV_HEADS * HEAD_DIM)
    return rows.reshape(N_KV_HEADS, group, HEAD_DIM, rows.shape[1]).transpose(1, 0, 2, 3).reshape(rows.shape)


def _heads_kv_major(rows):
    group = rows.shape[0] // (N_KV_HEADS * HEAD_DIM)
    return rows.reshape(group, N_KV_HEADS, HEAD_DIM, rows.shape[1]).transpose(1, 0, 2, 3).reshape(rows.shape)


def _by_core(full_rows):
    return full_rows.reshape((N_DEV // 2, 2, full_rows.shape[0] // N_DEV, full_rows.shape[1]))


def kernel(x, positions, ffn1_norm, ffn1_w_gate, ffn1_w_up, ffn1_w_down, mix_norm, w_in, conv_dw_w, conv_dw_b, conv_ln_g, conv_ln_b, conv_w_proj, attn_sinks, attn_w_o, gate_b, w_out, ffn2_norm, ffn2_w_gate, ffn2_w_up, ffn2_w_down, final_norm, loss_target, m_ffn1_norm, m_ffn1_w_gate, m_ffn1_w_up, m_ffn1_w_down, m_mix_norm, m_w_in, m_conv_dw_w, m_conv_dw_b, m_conv_ln_g, m_conv_ln_b, m_conv_w_proj, m_attn_sinks, m_attn_w_o, m_gate_b, m_w_out, m_ffn2_norm, m_ffn2_w_gate, m_ffn2_w_up, m_ffn2_w_down, m_final_norm, v_ffn1_norm, v_ffn1_w_gate, v_ffn1_w_up, v_ffn1_w_down, v_mix_norm, v_w_in, v_conv_dw_w, v_conv_dw_b, v_conv_ln_g, v_conv_ln_b, v_conv_w_proj, v_attn_sinks, v_attn_w_o, v_gate_b, v_w_out, v_ffn2_norm, v_ffn2_w_gate, v_ffn2_w_up, v_ffn2_w_down, v_final_norm):
    given = dict(locals())
    shapes = {n: given[n].shape for n in WEIGHT_ORDER}
    w = {n: given[n].reshape(given[n].shape[-2:]) if given[n].ndim == 3 else given[n].reshape(1, -1) for n in WEIGHT_ORDER}
    m = {n: given["m_" + n].reshape(w[n].shape) for n in WEIGHT_ORDER}
    v = {n: given["v_" + n].reshape(w[n].shape) for n in WEIGHT_ORDER}
    my_x, my_y, my_c = _place()
    my_core = my_c.astype(jnp.int32).reshape(1)
    my_chip = (2 * my_x + my_y).astype(jnp.int32).reshape(1)
    xs, target = x[0], loss_target[0]
    T, D = xs.shape
    KV = N_KV_HEADS * HEAD_DIM
    K = w["conv_dw_w"].shape[0]

    def t16(n):
        return w[n].T.astype(BF16)

    def r16(n):
        return w[n].astype(BF16)

    blocks1 = [t16("ffn1_w_gate"), t16("ffn1_w_up"), r16("ffn1_w_down")]
    dw_bits = _pad_rows(_to_rows(lax.bitcast_convert_type(w["conv_dw_w"], BF16).reshape(-1), 0), 0)
    blocks2 = [t16("w_in"), r16("conv_w_proj"), r16("attn_w_o"), r16("w_out"), dw_bits]
    blocks3 = [t16("ffn2_w_gate"), t16("ffn2_w_up"), r16("ffn2_w_down")]
    cos, sin = rope_tables(positions[0])
    sink_col = jnp.repeat(w["attn_sinks"].reshape(-1), WINDOW).reshape(N_KV_HEADS, (D // KV) * WINDOW, 1)

    def full(gathered):
        return gathered.reshape(-1, gathered.shape[2])

    wgt1, wut1 = (full(g) for g in all_gather(blocks1[:2], "gather_ffn1_up"))
    (h1, a1, b1, s1), got = ffn_up(xs, w["ffn1_norm"], wgt1, wut1, "ffn1_up",
                                   carry=compose(gather_whole(blocks1[2:]), gather_send(blocks2[:1])))
    wd1 = full(got[0])
    (x1,), got = ffn_down(xs, s1, wd1, "ffn1_down", carry=compose(gather_forward(got[1:]), gather_send(blocks2[1:])))
    wint = full(got[0])
    wq = _heads_slot_major(wint[2 * D:3 * D])
    (h2, ga, gb, u0, q, sgc, sgt, kk, vv), got = mix_in_fwd(x1, w["mix_norm"], wint, wq, w["gate_b"], cos, sin, "mix_in_fwd",
                                                            carry=compose(gather_forward(got[1:]), gather_send(blocks3)))
    wcp, wo, wout = (full(g) for g in got[:3])
    wo = _heads_slot_major(wo)
    dw_full = lax.bitcast_convert_type(_from_rows(got[3], w["conv_dw_w"].shape + (2,)), F32)
    dw_full = dw_full.transpose(1, 0, 2).reshape(K, D)
    o, gath3 = attn_fwd(q, kk, vv, sink_col, "attn_fwd", carry=gather_forward(got[4:]))
    x2, u1, co, ao, merged = mix_out_fwd(x1, u0, o, sgc, sgt, dw_full, w["conv_dw_b"], w["conv_ln_g"], w["conv_ln_b"],
                                         wcp, wo, wout, "mix_out_fwd")
    wgt2, wut2, wd2 = (full(g) for g in gath3)
    loss, dx3, d_final, h3, a2, b2, s2 = ffn_fwd_loss(x2, w["ffn2_norm"], wgt2, wut2, wd2, w["final_norm"], target,
                                                      "ffn2_fwd_loss")

    small = {"final_norm": d_final}
    (da2, db2), _ = ffn_bwd_hidden(dx3, a2, b2, wd2, "ffn2_bwd_hidden")
    (dx2, small["ffn2_norm"]), _ = ffn_bwd_input(dx3, x2, w["ffn2_norm"], da2, db2, wgt2, wut2, "ffn2_bwd_input")
    core2 = [_by_core(g) for g in (wgrad(da2, h3, "ffn2_dwg")[0], wgrad(db2, h3, "ffn2_dwu")[0],
                                   wgrad(s2, dx3, "ffn2_dwd", b_scale=FFN_SCALE)[0])]
    (dgc, dgt, do, du1, dco, dao, u3, sums), recv2 = mix_out_bwd(dx2, u1, co, ao, sgc, sgt, w["conv_ln_g"], w["conv_ln_b"],
                                                                 wcp, wo, wout, "mix_out_bwd", carry=swap_halves(core2))
    chip2 = pair_sum(my_core, core2, recv2, "ffn2_grads_pair_sum")
    small["gate_b"] = jnp.concatenate([sums[0:1], sums[1:2]], axis=1)
    small["conv_ln_g"], small["conv_ln_b"], small["conv_dw_b"] = sums[2:3], sums[3:4], sums[4:5]
    g_wout = wgrad(merged, dx2, "dw_out")[0]
    g_wcp = wgrad(u3, dco, "dw_conv_proj")[0]
    g_wo = _heads_kv_major(wgrad(o, dao, "dw_attn_o")[0])
    (dga, dgb, g_dw), got2 = conv_bwd(du1, u0, ga, gb, dw_full, "conv_bwd", carry=exchange_between_chips(chip2))
    dq, dk, dv, dsink = attn_bwd(q, kk, vv, do, sink_col, cos, sin, "attn_bwd")
    small["attn_sinks"] = dsink[:D // KV, :N_KV_HEADS].T.reshape(1, -1)
    pieces = [dga, dgb, dq, dk, dv, dgc, dgt]
    dx1, small["mix_norm"] = mix_in_bwd(dx2, x1, w["mix_norm"], wint, wq, pieces, "mix_in_bwd")
    group = D // KV
    q_moves = [(2 * D + HEAD_DIM * (group * g + hh), 2 * D + HEAD_DIM * (N_KV_HEADS * hh + g), HEAD_DIM)
               for g in range(N_KV_HEADS) for hh in range(group)]
    g_wint = wgrad_stacked(pieces[:3], h2, "dw_in_a", wint.shape[0], 0, moves=[(0, 0, 2 * D)] + q_moves)
    g_wint = wgrad_stacked(pieces[3:], h2, "dw_in_b", wint.shape[0], 3 * D, into=g_wint)
    corem = [_by_core(a) for a in (g_wint, g_wcp, g_wo, g_wout)]

    (da1, db1), recvm = ffn_bwd_hidden(dx1, a1, b1, wd1, "ffn1_bwd_hidden", carry=swap_halves(corem))
    chipm = pair_sum(my_core, corem, recvm, "mix_grads_pair_sum")
    g1c, gotm_a = wgrad(s1, dx1, "ffn1_dwd", carry=exchange_between_chips(chipm[:1]), b_scale=FFN_SCALE)
    g1a, gotm_b = wgrad(da1, h1, "ffn1_dwg", carry=exchange_between_chips(chipm[1:]))
    g1b = wgrad(db1, h1, "ffn1_dwu")[0]
    core1 = [_by_core(a) for a in (g1a, g1b, g1c)]
    chip1 = pair_sum(my_core, core1, run_exchange(swap_halves(core1), "ffn1_grads_swap"), "ffn1_grads_pair_sum")
    (grad_x, small["ffn1_norm"]), got1 = ffn_bwd_input(dx1, xs, w["ffn1_norm"], da1, db1, wgt1, wut1, "ffn1_bwd_input",
                                                       carry=exchange_between_chips(chip1))

    gotm = gotm_a + gotm_b
    grad_src = {"ffn1_w_gate": (chip1[0], got1[0]), "ffn1_w_up": (chip1[1], got1[1]), "ffn1_w_down": (chip1[2], got1[2]),
                "ffn2_w_gate": (chip2[0], got2[0]), "ffn2_w_up": (chip2[1], got2[1]), "ffn2_w_down": (chip2[2], got2[2]),
                "w_in": (chipm[0], gotm[0]), "conv_w_proj": (chipm[1], gotm[1]), "attn_w_o": (chipm[2], gotm[2]),
                "w_out": (chipm[3], gotm[3])}
    grads = {}

    def pack_small(d, taps, extra):
        rows = [_to_rows(d[n].reshape(-1), 0) for n in REPLICATED] + [taps, _to_rows(extra.reshape(-1), 0)]
        return _pad_rows(jnp.concatenate(rows, axis=0), 0)

    def like(a, ref):
        return a if a.shape == ref.shape else a.T

    def adamw_group(names, name, carry=None):
        refs = [grad_src[n][0][0] if isinstance(grad_src[n], tuple) else grad_src[n] for n in names]
        res, carried = adamw(my_chip, [like(w[n], r) for n, r in zip(names, refs)], [grad_src[n] for n in names],
                             [like(m[n], r) for n, r in zip(names, refs)], [like(v[n], r) for n, r in zip(names, refs)], name,
                             carry=carry)
        for n, outs in zip(names, res):
            grads[n], delta[n], new_m[n], new_v[n] = (like(a, w[n]) for a in outs)
        return carried

    delta, new_m, new_v = {}, {}, {}
    zero, no_taps = jnp.zeros((1, LANES), F32), jnp.zeros((K, D), F32)
    (shares,) = adamw_group(("ffn2_w_gate", "ffn2_w_up", "ffn2_w_down"), "adamw_ffn2",
                            carry=gather_whole([pack_small(small, g_dw, loss)]))
    g_s, d_s, m_s, v_s = adamw_replicated(pack_small(w, no_taps, zero), shares, pack_small(m, no_taps, zero),
                                          pack_small(v, no_taps, zero), "adamw_replicated")
    off = 0
    for n in REPLICATED:
        r = -(-w[n].shape[1] // PACK_COLS)
        grads[n], delta[n], new_m[n], new_v[n] = (_from_rows(a[off:off + r], w[n].shape) for a in (g_s, d_s, m_s, v_s))
        off += r
    shard_cols = w["conv_dw_w"].shape[1]
    grad_src["conv_dw_w"] = lax.dynamic_slice_in_dim(g_s[off:off + K], _dev_index(my_x, my_y, my_c) * shard_cols, shard_cols,
                                                    axis=1)
    total_loss = g_s[off + K, 0]
    adamw_group(("ffn1_w_gate", "ffn1_w_up", "ffn1_w_down"), "adamw_ffn1")
    adamw_group(("w_in", "conv_dw_w", "conv_w_proj", "attn_w_o", "w_out"), "adamw_mix")

    out = [total_loss, grad_x[None]]
    for d in (grads, delta, new_m, new_v):
        out += [d[n].reshape(shapes[n]) for n in WEIGHT_ORDER]
    return tuple(out)
```

```python
import functools
from typing import Callable, NamedTuple

import jax
import jax.numpy as jnp
from jax import lax
from jax.experimental import pallas as pl
from jax.experimental.pallas import tpu as pltpu

F32, BF16 = jnp.float32, jnp.bfloat16

HEAD_DIM = 64
N_KV_HEADS = 4
WINDOW = 128
ROPE_THETA = 10000.0
EPS = 1e-6
LN_EPS = 1e-5
NEG_INF = -1e30
ADAM_LR, ADAM_B1, ADAM_B2, ADAM_EPS, ADAM_WD, ADAM_STEP = 0.001, 0.9, 0.999, 1e-08, 0.01, 10

N_DEV = 8
LANES = 128
SUBLANES = 8
CONV_HALO = 32
CONV_ROWS, CONV_LANES = 64, 256
ROW_TILE = 512
FFN_CHUNK = 256
FFN_SCALE = 0.5
WGRAD_TILE_ELEMS = 2 ** 22
WGRAD_TILE_ROWS = 2048
WGRAD_VMEM = 40 * 2 ** 20
VMEM_LIMIT = 56 * 2 ** 20
MESH = pl.DeviceIdType.MESH


def _params(*sem):
    return pltpu.CompilerParams(dimension_semantics=sem or None, vmem_limit_bytes=VMEM_LIMIT)


def _resident(shape):
    zeros = (0,) * len(shape)
    return pl.BlockSpec(shape, lambda *_: zeros, pipeline_mode=pl.Buffered(1))


def _rows(tm, n):
    return pl.BlockSpec((tm, n), lambda i: (i, 0))


def _acc_spec(shape):
    zeros = (0,) * len(shape)
    return pl.BlockSpec(shape, lambda *_: zeros)


_ANY = pl.BlockSpec(memory_space=pl.ANY)


class Carry(NamedTuple):
    ins: tuple
    out_shapes: tuple
    aliases: dict
    sems: tuple
    start: Callable
    finish: Callable
    relay: Callable = None


def _call(body, *, name, grid, in_specs, out_specs, out_shape, args, scratch_shapes=(), carry=None):
    n_in, n_out, n_scr = len(in_specs), len(out_specs), len(scratch_shapes)
    params = _params(*(("arbitrary",) * len(grid)))
    if carry is None:
        res = pl.pallas_call(body, name=name, grid=grid, in_specs=list(in_specs), out_specs=tuple(out_specs),
                             out_shape=tuple(out_shape), scratch_shapes=list(scratch_shapes), compiler_params=params)(*args)
        return tuple(res), ()
    c_in, c_out = len(carry.ins), len(carry.out_shapes)

    def wrapped(*refs):
        ins, c_ins = refs[:n_in], refs[n_in:n_in + c_in]
        p = n_in + c_in
        outs, c_outs = refs[p:p + n_out], refs[p + n_out:p + n_out + c_out]
        p += n_out + c_out
        scr, c_sems = refs[p:p + n_scr], refs[p + n_scr:]
        ids = [pl.program_id(d) for d in range(len(grid))]
        first = functools.reduce(jnp.logical_and, [i == 0 for i in ids])
        last = functools.reduce(jnp.logical_and, [i == n - 1 for i, n in zip(ids, grid)])

        @pl.when(first)
        def _():
            carry.start(c_ins, c_outs, c_sems)

        body(*ins, *outs, *scr)

        if carry.relay is not None:
            @pl.when(ids[0] == (3 * grid[0]) // 4)
            def _():
                carry.relay(c_ins, c_outs, c_sems)

        @pl.when(last)
        def _():
            carry.finish(c_ins, c_outs, c_sems)

    res = pl.pallas_call(
        wrapped, name=name, grid=grid, in_specs=list(in_specs) + [_ANY] * c_in, out_specs=tuple(out_specs) + (_ANY,) * c_out,
        out_shape=tuple(out_shape) + tuple(carry.out_shapes), scratch_shapes=list(scratch_shapes) + list(carry.sems),
        input_output_aliases={n_in + i: n_out + o for i, o in carry.aliases.items()}, compiler_params=params,
    )(*args, *carry.ins)
    return tuple(res[:n_out]), tuple(res[n_out:])


def _nt(a, b):
    return lax.dot_general(a, b, (((1,), (1,)), ((), ())), preferred_element_type=F32)


def _tn(a, b):
    return lax.dot_general(a, b, (((0,), (0,)), ((), ())), preferred_element_type=F32)


def _dot(a, b):
    return jnp.dot(a, b, preferred_element_type=F32)


def _sigmoid(x):
    return 1.0 / (1.0 + jnp.exp(-x))


def _rms_fwd(x, g):
    r = lax.rsqrt(jnp.mean(x * x, axis=-1, keepdims=True) + EPS)
    return (x * r) * g


def _rms_bwd(x, g, dy):
    r = lax.rsqrt(jnp.mean(x * x, axis=-1, keepdims=True) + EPS)
    xhat = x * r
    dyg = dy * g
    dx = r * (dyg - xhat * jnp.mean(dyg * xhat, axis=-1, keepdims=True))
    return dx, dy * xhat


def _rot_half(x):
    lane = lax.broadcasted_iota(jnp.int32, (x.shape[0], LANES), 1)
    first = (lane % HEAD_DIM) < (HEAD_DIM // 2)
    out = []
    for s in range(x.shape[1] // LANES):
        xs = x[:, LANES * s:LANES * (s + 1)]
        out.append(jnp.where(first, pltpu.roll(xs, LANES - HEAD_DIM // 2, 1), pltpu.roll(xs, HEAD_DIM // 2, 1)))
    return out[0] if len(out) == 1 else jnp.concatenate(out, axis=1)


def _tile_lanes(t, width):
    return t if width == LANES else jnp.concatenate([t] * (width // LANES), axis=1)


def _rope_fwd(x, cos, sin_signed):
    w = x.shape[1]
    return x * _tile_lanes(cos, w) + _rot_half(x) * _tile_lanes(sin_signed, w)


def _rope_bwd(dy, cos, sin_signed):
    w = dy.shape[1]
    return dy * _tile_lanes(cos, w) + _rot_half(dy * _tile_lanes(sin_signed, w))


def _ffn_rows(x, g_ref, wg_ref, wu_ref, wd_ref, h_ref, a_ref, b_ref, s_ref, acc_ref):
    F = wg_ref.shape[0]
    h = _rms_fwd(x, g_ref[...]).astype(BF16)
    h_ref[...] = h
    for c in range(F // FFN_CHUNK):
        cs = pl.ds(c * FFN_CHUNK, FFN_CHUNK)
        a = _nt(h, wg_ref[cs, :])
        b = _nt(h, wu_ref[cs, :])
        a_ref[:, cs] = a.astype(BF16)
        b_ref[:, cs] = b.astype(BF16)
        s = (a * _sigmoid(a) * b).astype(BF16)
        s_ref[:, cs] = s
        y = _dot(s, wd_ref[cs, :])
        if c == 0:
            acc_ref[...] = y
        else:
            acc_ref[...] += y
    return x + FFN_SCALE * acc_ref[...]


def ffn_up(x, gain, wgt, wut, name, carry=None):
    T, D = x.shape
    F = wgt.shape[0]
    tm = min(ROW_TILE, T)

    def body(x_ref, g_ref, wg_ref, wu_ref, h_ref, a_ref, b_ref, s_ref):
        h = _rms_fwd(x_ref[...], g_ref[...]).astype(BF16)
        h_ref[...] = h
        for c in range(F // FFN_CHUNK):
            cs = pl.ds(c * FFN_CHUNK, FFN_CHUNK)
            a = _nt(h, wg_ref[cs, :])
            b = _nt(h, wu_ref[cs, :])
            a_ref[:, cs] = a.astype(BF16)
            b_ref[:, cs] = b.astype(BF16)
            s_ref[:, cs] = (a * _sigmoid(a) * b).astype(BF16)

    wide = jax.ShapeDtypeStruct((T, F), BF16)
    return _call(
        body, name=name, grid=(T // tm,), out_shape=(jax.ShapeDtypeStruct((T, D), BF16), wide, wide, wide),
        in_specs=[_rows(tm, D), _resident((1, D)), _resident((F, D)), _resident((F, D))],
        out_specs=(_rows(tm, D), _rows(tm, F), _rows(tm, F), _rows(tm, F)), args=(x, gain, wgt, wut), carry=carry)


def ffn_down(x, s, wd, name, carry=None):
    T, D = x.shape
    F = wd.shape[0]
    tm = min(2 * ROW_TILE, T)

    def body(x_ref, s_ref, wd_ref, xo_ref):
        xo_ref[...] = x_ref[...] + FFN_SCALE * _dot(s_ref[...], wd_ref[...])

    return _call(
        body, name=name, grid=(T // tm,), out_shape=(jax.ShapeDtypeStruct((T, D), F32),),
        in_specs=[_rows(tm, D), _rows(tm, F), _resident((F, D))], out_specs=(_rows(tm, D),), args=(x, s, wd), carry=carry)


def ffn_fwd_loss(x, gain, wgt, wut, wd, final_gain, target, name):
    T, D = x.shape
    F = wgt.shape[0]
    tm = min(ROW_TILE, T)

    def body(x_ref, g_ref, wg_ref, wu_ref, wd_ref, gf_ref, t_ref,
             loss_ref, dx_ref, dg_ref, h_ref, a_ref, b_ref, s_ref, acc_ref):
        xo = _ffn_rows(x_ref[...], g_ref, wg_ref, wu_ref, wd_ref, h_ref, a_ref, b_ref, s_ref, acc_ref)
        gf = gf_ref[...]
        err = _rms_fwd(xo, gf) - t_ref[...]
        dx, dgt = _rms_bwd(xo, gf, err * (1.0 / D))
        dx_ref[...] = dx

        @pl.when(pl.program_id(0) == 0)
        def _():
            dg_ref[...] = jnp.zeros_like(dg_ref)
            loss_ref[...] = jnp.zeros_like(loss_ref)
        dg_ref[...] += jnp.sum(dgt, axis=0, keepdims=True)
        per_token = jnp.sum(err * err, axis=-1, keepdims=True) * (0.5 / D)
        loss_ref[...] += jnp.broadcast_to(jnp.sum(per_token, axis=0, keepdims=True), (1, LANES))

    return pl.pallas_call(
        body, name=name, grid=(T // tm,),
        out_shape=(jax.ShapeDtypeStruct((1, LANES), F32), jax.ShapeDtypeStruct((T, D), F32), jax.ShapeDtypeStruct((1, D), F32),
                   jax.ShapeDtypeStruct((T, D), BF16), jax.ShapeDtypeStruct((T, F), BF16), jax.ShapeDtypeStruct((T, F), BF16),
                   jax.ShapeDtypeStruct((T, F), BF16)),
        in_specs=[_rows(tm, D), _resident((1, D)), _resident((F, D)), _resident((F, D)), _resident((F, D)), _resident((1, D)),
                  _rows(tm, D)],
        out_specs=(_acc_spec((1, LANES)), _rows(tm, D), _acc_spec((1, D)), _rows(tm, D), _rows(tm, F), _rows(tm, F),
                   _rows(tm, F)),
        scratch_shapes=[pltpu.VMEM((tm, D), F32)], compiler_params=_params("arbitrary"),
    )(x, gain, wgt, wut, wd, final_gain, target)


def ffn_bwd_hidden(dxo, a, b, wd, name, carry=None):
    T, D = dxo.shape
    F = wd.shape[0]
    tm = min(ROW_TILE, T)
    fc = FFN_CHUNK

    def hidden_body(dxo_ref, a_ref, b_ref, wd_ref, da_ref, db_ref):
        g0 = (FFN_SCALE * dxo_ref[...]).astype(BF16)
        for c in range(F // fc):
            cs = pl.ds(c * fc, fc)
            ds = _nt(g0, wd_ref[cs, :])
            a = a_ref[:, cs].astype(F32)
            bb = b_ref[:, cs].astype(F32)
            sa = _sigmoid(a)
            da_ref[:, cs] = (ds * bb * (sa * (1.0 + a * (1.0 - sa)))).astype(BF16)
            db_ref[:, cs] = (ds * (a * sa)).astype(BF16)

    wide = jax.ShapeDtypeStruct((T, F), BF16)
    return _call(
        hidden_body, name=name, grid=(T // tm,), out_shape=(wide, wide),
        in_specs=[_rows(tm, D), _rows(tm, F), _rows(tm, F), _resident((F, D))],
        out_specs=(_rows(tm, F), _rows(tm, F)), args=(dxo, a, b, wd), carry=carry)


def ffn_bwd_input(dxo, x, gain, da, db, wg, wu, name, carry=None):
    T, D = x.shape
    F = wg.shape[0]
    tm = min(ROW_TILE, T)

    def input_body(dxo_ref, x_ref, g_ref, da_ref, db_ref, wg_ref, wu_ref, dx_ref, dg_ref):
        dh = _dot(da_ref[...], wg_ref[...]) + _dot(db_ref[...], wu_ref[...])
        dx, dgt = _rms_bwd(x_ref[...], g_ref[...], dh)
        dx_ref[...] = dxo_ref[...] + dx

        @pl.when(pl.program_id(0) == 0)
        def _():
            dg_ref[...] = jnp.zeros_like(dg_ref)
        dg_ref[...] += jnp.sum(dgt, axis=0, keepdims=True)

    return _call(
        input_body, name=name, grid=(T // tm,),
        out_shape=(jax.ShapeDtypeStruct((T, D), F32), jax.ShapeDtypeStruct((1, D), F32)),
        in_specs=[_rows(tm, D), _rows(tm, D), _resident((1, D)), _rows(tm, F), _rows(tm, F), _resident((F, D)),
                  _resident((F, D))],
        out_specs=(_rows(tm, D), _acc_spec((1, D))), args=(dxo, x, gain, da, db, wg, wu), carry=carry)


def wgrad(a, b, name, carry=None, b_scale=None):
    T, M = a.shape
    N = b.shape[1]
    fixed = M * N * (4 + 2)
    per_row = 2 * (M * a.dtype.itemsize + N * b.dtype.itemsize)
    tk = ROW_TILE
    while fixed + 2 * tk * per_row <= WGRAD_VMEM and 2 * tk <= WGRAD_TILE_ROWS:
        tk *= 2
    tk = min(tk, T)
    nk = T // tk

    def body(a_ref, b_ref, o_ref, acc_ref):
        k = pl.program_id(0)
        bt = b_ref[...] if b_scale is None else b_scale * b_ref[...]
        part = _tn(a_ref[...].astype(BF16), bt.astype(BF16))

        @pl.when(k == 0)
        def _():
            acc_ref[...] = part

        @pl.when(k > 0)
        def _():
            acc_ref[...] += part

        @pl.when(k == nk - 1)
        def _():
            o_ref[...] = acc_ref[...].astype(BF16)

    (out,), carried = _call(
        body, name=name, grid=(nk,), out_shape=(jax.ShapeDtypeStruct((M, N), BF16),),
        in_specs=[_rows(tk, M), _rows(tk, N)], out_specs=(_resident((M, N)),),
        scratch_shapes=[pltpu.VMEM((M, N), F32)], args=(a, b), carry=carry)
    return out, carried


def wgrad_stacked(pieces, b, name, total_rows, at, into=None, moves=None):
    T, N = b.shape
    n = len(pieces)
    widths = [p.shape[1] for p in pieces]
    offs = [sum(widths[:i]) for i in range(n)]
    M = sum(widths)
    tk = ROW_TILE
    while 2 * tk * M <= WGRAD_TILE_ELEMS and 2 * tk <= WGRAD_TILE_ROWS:
        tk *= 2
    tk = min(tk, T)
    nk = T // tk
    moves = moves or [(0, 0, M)]
    n_in = n + 1 + (into is not None)

    def body(*refs):
        a_refs, b_ref = refs[:n], refs[n]
        o_ref, acc_ref, stage, sem = refs[n_in:]
        k = pl.program_id(0)

        @pl.when(k == 0)
        def _():
            acc_ref[...] = jnp.zeros_like(acc_ref)
        bt = b_ref[...].astype(BF16)
        for a_ref, off, width in zip(a_refs, offs, widths):
            acc_ref[pl.ds(off, width), :] += _tn(a_ref[...].astype(BF16), bt)

        @pl.when(k == nk - 1)
        def _():
            for to, start, rows in moves:
                stage[pl.ds(to, rows), :] = acc_ref[pl.ds(start, rows), :].astype(BF16)
            cp = pltpu.make_async_copy(stage, o_ref.at[pl.ds(at, M)], sem)
            cp.start()
            cp.wait()

    return pl.pallas_call(
        body, name=name, grid=(nk,), out_shape=jax.ShapeDtypeStruct((total_rows, N), BF16),
        in_specs=[_rows(tk, width) for width in widths] + [_rows(tk, N)] + [_ANY] * (into is not None), out_specs=_ANY,
        scratch_shapes=[pltpu.VMEM((M, N), F32), pltpu.VMEM((M, N), BF16), pltpu.SemaphoreType.DMA],
        input_output_aliases={n + 1: 0} if into is not None else {}, compiler_params=_params("arbitrary"),
    )(*pieces, b, *([into] if into is not None else []))


def _w_in_rows(D):
    KV = N_KV_HEADS * HEAD_DIM
    return 0, D, 2 * D, 3 * D, 3 * D + KV, 3 * D + 2 * KV, 4 * D + 2 * KV


def mix_in_fwd(x, gain, wint, wq, gate_b, cos, sin_signed, name, carry=None):
    T, D = x.shape
    KV = N_KV_HEADS * HEAD_DIM
    tm = min(ROW_TILE, T)
    o_ga, o_gb, _, o_k, o_v, o_gc, o_gt = _w_in_rows(D)

    def body(x_ref, g_ref, w_ref, wq_ref, gb_ref, cos_ref, sin_ref,
             h_ref, ga_ref, gb_out_ref, u0_ref, q_ref, sgc_ref, sgt_ref, k_ref, v_ref):
        h = _rms_fwd(x_ref[...], g_ref[...]).astype(BF16)
        h_ref[...] = h
        cos, sin = cos_ref[...], sin_ref[...]
        ga = _nt(h, w_ref[pl.ds(o_ga, D), :])
        gb = _nt(h, w_ref[pl.ds(o_gb, D), :])
        ga_ref[...] = ga.astype(BF16)
        gb_out_ref[...] = gb.astype(BF16)
        u0_ref[...] = (ga * _sigmoid(gb)).astype(BF16)
        q = _nt(h, wq_ref[...])
        q_ref[...] = _rope_fwd(q, cos, sin).astype(BF16)
        gc = _nt(h, w_ref[pl.ds(o_gc, D), :]) + gb_ref[:, pl.ds(0, D)]
        sgc_ref[...] = _sigmoid(gc).astype(BF16)
        gt = _nt(h, w_ref[pl.ds(o_gt, D), :]) + gb_ref[:, pl.ds(D, D)]
        sgt_ref[...] = _sigmoid(gt).astype(BF16)
        k = _nt(h, w_ref[pl.ds(o_k, KV), :])
        k_ref[...] = _rope_fwd(k, cos, sin).astype(BF16)
        v_ref[...] = _nt(h, w_ref[pl.ds(o_v, KV), :]).astype(BF16)

    big = jax.ShapeDtypeStruct((T, D), BF16)
    small = jax.ShapeDtypeStruct((T, KV), BF16)
    return _call(
        body, name=name, grid=(T // tm,),
        out_shape=(big, big, big, big, big, big, big, small, small),
        in_specs=[_rows(tm, D), _resident((1, D)), _resident(wint.shape), _resident(wq.shape), _resident((1, 2 * D)),
                  _rows(tm, LANES), _rows(tm, LANES)],
        out_specs=(_rows(tm, D),) * 7 + (_rows(tm, KV),) * 2,
        args=(x, gain, wint, wq, gate_b, cos, sin_signed), carry=carry)


def _from_prev(rows):
    qi = lax.broadcasted_iota(jnp.int32, (rows, WINDOW), 0) % WINDOW
    return lax.broadcasted_iota(jnp.int32, (rows, WINDOW), 1) > qi


def _fold(x, g, from_prev):
    lo = 2 * WINDOW * g
    return jnp.where(from_prev, x[:, lo:lo + WINDOW], x[:, lo + WINDOW:lo + 2 * WINDOW])


def _unfold(folded, from_prev):
    zero = jnp.zeros_like(folded[0])
    parts = []
    for x in folded:
        parts += [jnp.where(from_prev, x, zero), jnp.where(from_prev, zero, x)]
    return jnp.concatenate(parts, axis=1)


def _kv_lane_head(rows, width):
    return lax.broadcasted_iota(jnp.int32, (rows, width), 1) // HEAD_DIM


def _block_diag(win):
    head = _kv_lane_head(*win.shape)
    zero = jnp.zeros_like(win)
    return jnp.concatenate([jnp.where(head == g, win, zero) for g in range(N_KV_HEADS)], axis=0)


def _diag_blocks_sum(bd, keys):
    head = _kv_lane_head(keys, bd.shape[1])
    out = jnp.zeros((keys, bd.shape[1]), F32)
    for g in range(N_KV_HEADS):
        out = jnp.where(head == g, bd[g * keys:(g + 1) * keys], out)
    return out


def _kv_windows(k_ref, kh_ref, v_ref, vh_ref, j):
    rows = pl.ds(j * WINDOW, WINDOW)
    if j == 0:
        kprev, vprev = kh_ref[...], vh_ref[...]
    else:
        prev = pl.ds((j - 1) * WINDOW, WINDOW)
        kprev, vprev = k_ref[prev, :], v_ref[prev, :]
    return jnp.concatenate([kprev, k_ref[rows, :]], axis=0), jnp.concatenate([vprev, v_ref[rows, :]], axis=0)


def _stack_slots(ref, j, group, KV):
    rows = pl.ds(j * WINDOW, WINDOW)
    return jnp.concatenate([ref[rows, pl.ds(KV * hh, KV)] for hh in range(group)], axis=0)


def _attn_exp(qs, kbd, sink_ref, from_prev, no_prev):
    s = _nt(qs, kbd)
    if no_prev is not None:
        qi = lax.broadcasted_iota(jnp.int32, from_prev.shape, 0) % WINDOW
        absent = lax.broadcasted_iota(jnp.int32, from_prev.shape, 1) > jnp.where(no_prev, qi, WINDOW)
    out = []
    for g in range(N_KV_HEADS):
        sg = _fold(s, g, from_prev) * (HEAD_DIM ** -0.5)
        if no_prev is not None:
            sg = jnp.where(absent, NEG_INF, sg)
        sink = sink_ref[g]
        m = jnp.maximum(jnp.max(sg, axis=-1, keepdims=True), sink)
        out.append((jnp.exp(sg - m), jnp.exp(sink - m)))
    return out


def _spread_over_heads(cols, rows, KV):
    head = _kv_lane_head(rows, KV)
    out = jnp.zeros((rows, KV), F32)
    for g, col in enumerate(cols):
        out = jnp.where(head == g, col, out)
    return out


def _halo_rows_spec(tq, width, sub):
    return pl.BlockSpec((sub, width), lambda i: (jnp.maximum(i * (tq // sub) - 1, 0), 0))


def attn_fwd(q, k, v, sink_col, name, carry=None):
    T, D = q.shape
    KV = k.shape[1]
    group = D // KV
    tq = min(ROW_TILE, T)
    nsub = tq // WINDOW
    rows, wide = group * WINDOW, N_KV_HEADS * 2 * WINDOW

    def body(q_ref, k_ref, kh_ref, v_ref, vh_ref, sink_ref, o_ref):
        from_prev = _from_prev(rows)
        head = _kv_lane_head(wide, KV)
        block = lax.broadcasted_iota(jnp.int32, head.shape, 0) // (2 * WINDOW)
        ones_bd = jnp.where(head == block, 1.0, 0.0).astype(BF16)
        for j in range(nsub):
            k_win, v_win = _kv_windows(k_ref, kh_ref, v_ref, vh_ref, j)
            parts = _attn_exp(_stack_slots(q_ref, j, group, KV), _block_diag(k_win), sink_ref, from_prev,
                              pl.program_id(0) == 0 if j == 0 else None)
            p = _unfold([pg.astype(BF16) for pg, _ in parts], from_prev)
            both = _dot(p, jnp.concatenate([_block_diag(v_win), ones_bd], axis=1))
            denom = both[:, KV:] + _spread_over_heads([es for _, es in parts], rows, KV)
            out = (both[:, :KV] / denom).astype(BF16)
            for hh in range(group):
                o_ref[pl.ds(j * WINDOW, WINDOW), pl.ds(KV * hh, KV)] = out[hh * WINDOW:(hh + 1) * WINDOW]

    (o,), carried = _call(
        body, name=name, grid=(T // tq,),
        out_shape=(jax.ShapeDtypeStruct((T, D), BF16),),
        in_specs=[_rows(tq, D), _rows(tq, KV), _halo_rows_spec(tq, KV, WINDOW), _rows(tq, KV),
                  _halo_rows_spec(tq, KV, WINDOW), _resident(sink_col.shape)],
        out_specs=(_rows(tq, D),), args=(q, k, k, v, v, sink_col), carry=carry)
    return o, carried


def attn_bwd(q, k, v, do, sink_col, cos, sin_signed, name):
    T, D = q.shape
    KV = k.shape[1]
    group = D // KV
    tq = min(ROW_TILE, T)
    nsub = tq // WINDOW
    nt = T // tq
    scale = HEAD_DIM ** -0.5
    rows, wide = group * WINDOW, N_KV_HEADS * 2 * WINDOW

    def rev(i):
        return nt - 1 - i

    def body(q_ref, k_ref, kh_ref, v_ref, vh_ref, do_ref, sink_ref, cos_ref, sin_ref,
             dq_ref, dk_ref, dv_ref, dsink_ref, dq_acc, dk_acc, dv_acc, carry_k, carry_v):
        i = pl.program_id(0)

        @pl.when(i == 0)
        def _():
            carry_k[...] = jnp.zeros_like(carry_k)
            carry_v[...] = jnp.zeros_like(carry_v)
            dsink_ref[...] = jnp.zeros_like(dsink_ref)

        dk_acc[...] = jnp.zeros_like(dk_acc)
        dv_acc[...] = jnp.zeros_like(dv_acc)
        from_prev = _from_prev(rows)
        lane = lax.broadcasted_iota(jnp.int32, (1, LANES), 1)
        for j in range(nsub):
            k_win, v_win = _kv_windows(k_ref, kh_ref, v_ref, vh_ref, j)
            kbd, vbd = _block_diag(k_win), _block_diag(v_win)
            qs, dos = _stack_slots(q_ref, j, group, KV), _stack_slots(do_ref, j, group, KV)
            dp = _nt(dos, vbd)
            probs16, ds16 = [], []
            for g, (pg, es) in enumerate(_attn_exp(qs, kbd, sink_ref, from_prev, rev(i) == 0 if j == 0 else None)):
                inv = 1.0 / (jnp.sum(pg, axis=-1, keepdims=True) + es)
                probs = pg * inv
                dpg = _fold(dp, g, from_prev)
                delta = jnp.sum(probs * dpg, axis=-1, keepdims=True)
                probs16.append(probs.astype(BF16))
                ds16.append((probs * (dpg - delta) * scale).astype(BF16))
                dsk = -(es * inv * delta)
                for hh in range(group):
                    tot = jnp.sum(dsk[hh * WINDOW:(hh + 1) * WINDOW], axis=0, keepdims=True)
                    dsink_ref[pl.ds(hh, 1), :] += jnp.where(lane == g, tot, 0.0)
            ds = _unfold(ds16, from_prev)
            dqs = _dot(ds, kbd)
            for hh in range(group):
                dq_acc[pl.ds(j * WINDOW, WINDOW), pl.ds(KV * hh, KV)] = dqs[hh * WINDOW:(hh + 1) * WINDOW]
            keys = pl.ds(j * WINDOW, 2 * WINDOW)
            dk_acc[keys, :] += _diag_blocks_sum(_tn(ds, qs), 2 * WINDOW)
            dv_acc[keys, :] += _diag_blocks_sum(_tn(_unfold(probs16, from_prev), dos), 2 * WINDOW)

        tail = pl.ds(tq, WINDOW)
        dk_acc[tail, :] += carry_k[...]
        dv_acc[tail, :] += carry_v[...]
        carry_k[...] = dk_acc[pl.ds(0, WINDOW), :]
        carry_v[...] = dv_acc[pl.ds(0, WINDOW), :]
        cos, sin = cos_ref[...], sin_ref[...]
        dq_ref[...] = _rope_bwd(dq_acc[...], cos, sin).astype(BF16)
        dk_ref[...] = _rope_bwd(dk_acc[pl.ds(WINDOW, tq), :], cos, sin).astype(BF16)
        dv_ref[...] = dv_acc[pl.ds(WINDOW, tq), :].astype(BF16)

    def rrows(n):
        return pl.BlockSpec((tq, n), lambda i: (rev(i), 0))

    def rhalo(n):
        return pl.BlockSpec((WINDOW, n), lambda i: (jnp.maximum(rev(i) * nsub - 1, 0), 0))

    return pl.pallas_call(
        body, name=name, grid=(nt,),
        out_shape=(jax.ShapeDtypeStruct((T, D), BF16), jax.ShapeDtypeStruct((T, KV), BF16),
                   jax.ShapeDtypeStruct((T, KV), BF16), jax.ShapeDtypeStruct((SUBLANES, LANES), F32)),
        in_specs=[rrows(D), rrows(KV), rhalo(KV), rrows(KV), rhalo(KV), rrows(D), _resident(sink_col.shape),
                  rrows(LANES), rrows(LANES)],
        out_specs=(rrows(D), rrows(KV), rrows(KV), _acc_spec((SUBLANES, LANES))),
        scratch_shapes=[pltpu.VMEM((tq, D), F32), pltpu.VMEM((WINDOW + tq, KV), F32), pltpu.VMEM((WINDOW + tq, KV), F32),
                        pltpu.VMEM((WINDOW, KV), F32), pltpu.VMEM((WINDOW, KV), F32)],
        compiler_params=_params("arbitrary"),
    )(q, k, k, v, v, do, sink_col, cos, sin_signed)


def _ln_stats(u):
    mu = jnp.mean(u, axis=-1, keepdims=True)
    d = u - mu
    rstd = lax.rsqrt(jnp.mean(d * d, axis=-1, keepdims=True) + LN_EPS)
    return d * rstd, rstd


def _lag_taps(b, K):
    return [(a, K - 1 - (SUBLANES * a + b)) for a in range(-(-K // SUBLANES)) if SUBLANES * a + b <= K - 1]


def _conv_chunks(tm, D, chunk):
    def rows(c, carry):
        r0 = pl.multiple_of(c * CONV_ROWS, CONV_ROWS)
        for l0 in range(0, D, CONV_LANES):
            chunk(r0, pl.ds(l0, CONV_LANES))
        return carry
    lax.fori_loop(0, tm // CONV_ROWS, rows, 0)


def _conv_causal(buf, w_ref, bias_ref, out_ref, tm, D, K):
    def chunk(r0, lanes):
        acc = jnp.broadcast_to(bias_ref[:, lanes], (CONV_ROWS, CONV_LANES))
        for b in range(SUBLANES):
            y = None
            for a, k in _lag_taps(b, K):
                start = pl.multiple_of(r0 + CONV_HALO - SUBLANES * (a + 1), SUBLANES)
                t = buf[pl.ds(start, CONV_ROWS + SUBLANES), lanes] * w_ref[pl.ds(k, 1), lanes]
                y = t if y is None else y + t
            acc = acc + y[SUBLANES - b:SUBLANES - b + CONV_ROWS]
        out_ref[pl.ds(r0, CONV_ROWS), lanes] = acc
    _conv_chunks(tm, D, chunk)


def _conv_anticausal(dbuf, w_ref, out_ref, tm, D, K):
    def chunk(r0, lanes):
        acc = jnp.zeros((CONV_ROWS, CONV_LANES), F32)
        for b in range(SUBLANES):
            y = None
            for a, k in _lag_taps(b, K):
                start = pl.multiple_of(r0 + SUBLANES * a, SUBLANES)
                t = dbuf[pl.ds(start, CONV_ROWS + SUBLANES), lanes] * w_ref[pl.ds(k, 1), lanes]
                y = t if y is None else y + t
            acc = acc + y[b:b + CONV_ROWS]
        out_ref[pl.ds(r0, CONV_ROWS), lanes] = acc
    _conv_chunks(tm, D, chunk)


def _conv_tap_grads(dbuf, ubuf, acc_ref, tm, D, K):
    reach = SUBLANES * (-(-K // SUBLANES) - 1)

    def chunk(r0, lanes):
        d = dbuf[pl.ds(r0, CONV_ROWS), lanes]
        around = ubuf[pl.ds(pl.multiple_of(r0 + CONV_HALO - reach - SUBLANES, SUBLANES), CONV_ROWS + reach + SUBLANES), lanes]
        for b in range(SUBLANES):
            shifted = around[SUBLANES - b:SUBLANES - b + CONV_ROWS + reach]
            for a, k in _lag_taps(b, K):
                prod = d * shifted[reach - SUBLANES * a:reach - SUBLANES * a + CONV_ROWS]
                part = prod[0:SUBLANES]
                for i in range(1, CONV_ROWS // SUBLANES):
                    part = part + prod[SUBLANES * i:SUBLANES * (i + 1)]
                acc_ref[k, :, lanes] += part
    _conv_chunks(tm, D, chunk)


def mix_out_fwd(x, u0, o, sgc, sgt, dw_w, dw_b, ln_g, ln_b, wcp, wo, wout, name):
    T, D = x.shape
    tm = min(ROW_TILE, T)
    K = dw_w.shape[0]

    def body(x_ref, u_ref, uh_ref, o_ref, sgc_ref, sgt_ref, w_ref, b_ref, lg_ref, lb_ref, wcp_ref, wo_ref, wout_ref,
             x2_ref, u1_ref, co_ref, ao_ref, mg_ref, buf, conv):
        keep = (pl.program_id(0) > 0).astype(F32)
        buf[pl.ds(0, CONV_HALO), :] = uh_ref[...].astype(F32) * keep
        buf[pl.ds(CONV_HALO, tm), :] = u_ref[...].astype(F32)
        _conv_causal(buf, w_ref, b_ref, conv, tm, D, K)
        acc = conv[...]
        u1_ref[...] = acc.astype(BF16)
        xhat, _ = _ln_stats(acc)
        u2 = xhat * lg_ref[...] + lb_ref[...]
        u3 = (u2 * _sigmoid(u2)).astype(BF16)
        co = _dot(u3, wcp_ref[...])
        ao = _dot(o_ref[...], wo_ref[...])
        co_ref[...] = co.astype(BF16)
        ao_ref[...] = ao.astype(BF16)
        merged = (sgc_ref[...].astype(F32) * co + sgt_ref[...].astype(F32) * ao).astype(BF16)
        mg_ref[...] = merged
        x2_ref[...] = x_ref[...] + _dot(merged, wout_ref[...])

    big = jax.ShapeDtypeStruct((T, D), BF16)
    vec = _resident((1, D))
    return pl.pallas_call(
        body, name=name, grid=(T // tm,),
        out_shape=(jax.ShapeDtypeStruct((T, D), F32), big, big, big, big),
        in_specs=[_rows(tm, D), _rows(tm, D), _halo_rows_spec(tm, D, CONV_HALO), _rows(tm, D), _rows(tm, D), _rows(tm, D),
                  _resident((K, D)), vec, vec, vec, _resident((D, D)), _resident((D, D)), _resident((D, D))],
        out_specs=(_rows(tm, D),) * 5,
        scratch_shapes=[pltpu.VMEM((CONV_HALO + tm, D), F32), pltpu.VMEM((tm, D), F32)],
        compiler_params=_params("arbitrary"),
    )(x, u0, u0, o, sgc, sgt, dw_w, dw_b, ln_g, ln_b, wcp, wo, wout)


def mix_out_bwd(dx2, u1, co, ao, sgc, sgt, merged, ln_g, ln_b, wcp, wo, wout, name, carry=None):
    T, D = dx2.shape
    tm = min(ROW_TILE, T)
    nt = T // tm

    def body(dx_ref, u1_ref, co_ref, ao_ref, sgc_ref, sgt_ref, mg_ref, lg_ref, lb_ref, wcp_ref, wo_ref, wout_ref,
             dgc_ref, dgt_ref, do_ref, du1_ref, dao_ref, sums_ref, gwout_ref, gwcp_ref, acc_out, acc_cp):
        i = pl.program_id(0)

        @pl.when(i == 0)
        def _():
            sums_ref[...] = jnp.zeros_like(sums_ref)
            acc_out[...] = jnp.zeros_like(acc_out)
            acc_cp[...] = jnp.zeros_like(acc_cp)

        dx16 = dx_ref[...].astype(BF16)
        acc_out[...] += _tn(mg_ref[...], dx16)
        dm = _nt(dx16, wout_ref[...])
        sgc, sgt = sgc_ref[...].astype(F32), sgt_ref[...].astype(F32)
        dco = (dm * sgc).astype(BF16)
        dao = (dm * sgt).astype(BF16)
        dgc = dm * co_ref[...].astype(F32) * sgc * (1.0 - sgc)
        dgt = dm * ao_ref[...].astype(F32) * sgt * (1.0 - sgt)
        dao_ref[...] = dao
        dgc_ref[...] = dgc.astype(BF16)
        dgt_ref[...] = dgt.astype(BF16)
        do_ref[...] = _nt(dao, wo_ref[...]).astype(BF16)
        du3 = _nt(dco, wcp_ref[...])
        xhat, rstd = _ln_stats(u1_ref[...].astype(F32))
        g = lg_ref[...]
        u2 = xhat * g + lb_ref[...]
        su = _sigmoid(u2)
        acc_cp[...] += _tn((u2 * su).astype(BF16), dco)
        du2 = du3 * (su * (1.0 + u2 * (1.0 - su)))
        dxh = du2 * g
        du1 = rstd * (dxh - jnp.mean(dxh, axis=-1, keepdims=True) - xhat * jnp.mean(dxh * xhat, axis=-1, keepdims=True))
        du1_ref[...] = du1.astype(BF16)
        for r, val in enumerate((dgc, dgt, du2 * xhat, du2, du1)):
            sums_ref[pl.ds(r, 1), :] += jnp.sum(val, axis=0, keepdims=True)

        @pl.when(i == nt - 1)
        def _():
            gwout_ref[...] = acc_out[...].astype(BF16)
            gwcp_ref[...] = acc_cp[...].astype(BF16)

    big = jax.ShapeDtypeStruct((T, D), BF16)
    square = jax.ShapeDtypeStruct((D, D), BF16)
    vec = _resident((1, D))
    return _call(
        body, name=name, grid=(nt,),
        out_shape=(big,) * 5 + (jax.ShapeDtypeStruct((8, D), F32), square, square),
        in_specs=[_rows(tm, D)] * 7 + [vec, vec, _resident((D, D)), _resident((D, D)), _resident((D, D))],
        out_specs=(_rows(tm, D),) * 5 + (_acc_spec((8, D)), _resident((D, D)), _resident((D, D))),
        scratch_shapes=[pltpu.VMEM((D, D), F32), pltpu.VMEM((D, D), F32)],
        args=(dx2, u1, co, ao, sgc, sgt, merged, ln_g, ln_b, wcp, wo, wout), carry=carry)


def conv_bwd(du1, u0, ga, gb, dw_w, name, carry=None):
    T, D = du1.shape
    tm = min(ROW_TILE, T)
    nt = T // tm
    K = dw_w.shape[0]
    per = tm // CONV_HALO

    def body(d_ref, dn_ref, u_ref, uh_ref, ga_ref, gb_ref, w_ref, dga_ref, dgb_ref, dw_ref, dbuf, ubuf, du0_buf, taps):
        i = pl.program_id(0)
        dbuf[pl.ds(0, tm), :] = d_ref[...].astype(F32)
        dbuf[pl.ds(tm, CONV_HALO), :] = dn_ref[...].astype(F32) * (i < nt - 1).astype(F32)
        ubuf[pl.ds(0, CONV_HALO), :] = uh_ref[...].astype(F32) * (i > 0).astype(F32)
        ubuf[pl.ds(CONV_HALO, tm), :] = u_ref[...].astype(F32)

        @pl.when(i == 0)
        def _():
            taps[...] = jnp.zeros_like(taps)

        _conv_anticausal(dbuf, w_ref, du0_buf, tm, D, K)
        _conv_tap_grads(dbuf, ubuf, taps, tm, D, K)
        du0 = du0_buf[...]
        ga, gb = ga_ref[...].astype(F32), gb_ref[...].astype(F32)
        sg = _sigmoid(gb)
        dga_ref[...] = (du0 * sg).astype(BF16)
        dgb_ref[...] = (du0 * ga * sg * (1.0 - sg)).astype(BF16)

        @pl.when(i == nt - 1)
        def _():
            for k in range(K):
                dw_ref[pl.ds(k, 1), :] = jnp.sum(taps[k], axis=0, keepdims=True)

    nxt = pl.BlockSpec((CONV_HALO, D), lambda i: (jnp.minimum((i + 1) * per, nt * per - 1), 0))
    big = jax.ShapeDtypeStruct((T, D), BF16)
    return _call(
        body, name=name, grid=(nt,),
        out_shape=(big, big, jax.ShapeDtypeStruct((K, D), F32)),
        in_specs=[_rows(tm, D), nxt, _rows(tm, D), _halo_rows_spec(tm, D, CONV_HALO), _rows(tm, D), _rows(tm, D),
                  _resident((K, D))],
        out_specs=(_rows(tm, D), _rows(tm, D), _acc_spec((K, D))),
        scratch_shapes=[pltpu.VMEM((tm + CONV_HALO, D), F32), pltpu.VMEM((CONV_HALO + tm, D), F32), pltpu.VMEM((tm, D), F32),
                        pltpu.VMEM((K, SUBLANES, D), F32)],
        args=(du1, du1, u0, u0, ga, gb, dw_w), carry=carry)


def mix_in_bwd(dx2, x, gain, wint, wq, pieces, name):
    T, D = x.shape
    tm = min(ROW_TILE, T)
    widths = [p.shape[1] for p in pieces]
    offs = _w_in_rows(D)

    def body(dx2_ref, x_ref, g_ref, w_ref, wq_ref, *rest):
        piece_refs, (dx_ref, dg_ref) = rest[:len(pieces)], rest[len(pieces):]
        dh = None
        for n, (p_ref, off, w) in enumerate(zip(piece_refs, offs, widths)):
            t = _dot(p_ref[...], wq_ref[...] if n == 2 else w_ref[pl.ds(off, w), :])
            dh = t if dh is None else dh + t
        dx, dgt = _rms_bwd(x_ref[...], g_ref[...], dh)
        dx_ref[...] = dx2_ref[...] + dx

        @pl.when(pl.program_id(0) == 0)
        def _():
            dg_ref[...] = jnp.zeros_like(dg_ref)
        dg_ref[...] += jnp.sum(dgt, axis=0, keepdims=True)

    return pl.pallas_call(
        body, name=name, grid=(T // tm,),
        out_shape=(jax.ShapeDtypeStruct((T, D), F32), jax.ShapeDtypeStruct((1, D), F32)),
        in_specs=[_rows(tm, D), _rows(tm, D), _resident((1, D)), _resident(wint.shape), _resident(wq.shape)]
        + [_rows(tm, w) for w in widths],
        out_specs=(_rows(tm, D), _acc_spec((1, D))),
        compiler_params=_params("arbitrary"),
    )(dx2, x, gain, wint, wq, *pieces)


def rope_tables(positions):
    half = HEAD_DIM // 2
    inv_freq = ROPE_THETA ** (-jnp.arange(half, dtype=F32) / half)
    ang = positions.astype(F32)[:, None] * inv_freq
    cos, sin = jnp.cos(ang), jnp.sin(ang)
    reps = LANES // HEAD_DIM
    return jnp.tile(jnp.concatenate([cos, cos], axis=-1), (1, reps)), jnp.tile(jnp.concatenate([-sin, sin], axis=-1), (1, reps))


def _place():
    return lax.axis_index("x"), lax.axis_index("y"), lax.axis_index("c")


def all_gather(blocks, name):
    n = len(blocks)
    send = gather_send(blocks)
    forward = gather_forward(send.out_shapes)
    n_sems = len(send.sems)

    def body(*refs):
        ins, outs, sems = refs[:n], refs[n:2 * n], refs[2 * n:]
        send.start(ins, outs, sems[:n_sems])
        send.finish(ins, outs, sems[:n_sems])
        forward.start((), outs, sems[n_sems:])
        forward.finish((), outs, sems[n_sems:])

    return pl.pallas_call(
        body, name=name, out_shape=tuple(send.out_shapes), in_specs=[_ANY] * n, out_specs=(_ANY,) * n,
        scratch_shapes=list(send.sems) + list(forward.sems),
    )(*blocks)


def _chips_across(x, y):
    return [(1 - x, y), (x, 1 - y), (1 - x, 1 - y)]


def _dev_index(x, y, c):
    return 4 * x + 2 * y + c


def gather_send(blocks):
    n = len(blocks)

    def copies(in_refs, out_refs, sems):
        send, recv, local = sems
        x, y, c = _place()
        targets = [(x, y, 1 - c)] + [(*chip, c) for chip in _chips_across(x, y)]
        outgoing, incoming, mine = [], [], []
        for i, (x_ref, out_ref) in enumerate(zip(in_refs, out_refs)):
            for k, t in enumerate(targets):
                pair = dict(send_sem=send.at[4 * i + k], recv_sem=recv.at[4 * i + k], device_id=t, device_id_type=MESH)
                outgoing.append(pltpu.make_async_remote_copy(src_ref=x_ref, dst_ref=out_ref.at[_dev_index(x, y, c)], **pair))
                incoming.append(pltpu.make_async_remote_copy(src_ref=x_ref, dst_ref=out_ref.at[_dev_index(*t)], **pair))
            mine.append(pltpu.make_async_copy(x_ref, out_ref.at[_dev_index(x, y, c)], local.at[i]))
        return outgoing, incoming, mine

    def start(*refs):
        outgoing, _, mine = copies(*refs)
        for cp in mine + outgoing:
            cp.start()

    def finish(*refs):
        outgoing, incoming, mine = copies(*refs)
        for cp in incoming:
            cp.wait_recv()
        for cp in outgoing:
            cp.wait_send()
        for cp in mine:
            cp.wait()

    return Carry(ins=tuple(blocks), out_shapes=tuple(jax.ShapeDtypeStruct((N_DEV,) + b.shape, b.dtype) for b in blocks),
                 aliases={}, sems=(pltpu.SemaphoreType.DMA((4 * n,)), pltpu.SemaphoreType.DMA((4 * n,)),
                                   pltpu.SemaphoreType.DMA((n,))), start=start, finish=finish)


def gather_forward(gathered):
    n = len(gathered)

    def copies(in_refs, out_refs, sems):
        send, recv = sems
        x, y, c = _place()
        outgoing, incoming = [], []
        for i, buf in enumerate(out_refs):
            for k, chip in enumerate(_chips_across(x, y)):
                pair = dict(send_sem=send.at[3 * i + k], recv_sem=recv.at[3 * i + k], device_id=(x, y, 1 - c),
                            device_id_type=MESH)
                rows = buf.at[_dev_index(*chip, c)]
                outgoing.append(pltpu.make_async_remote_copy(src_ref=rows, dst_ref=rows, **pair))
                theirs = buf.at[_dev_index(*chip, 1 - c)]
                incoming.append(pltpu.make_async_remote_copy(src_ref=theirs, dst_ref=theirs, **pair))
        return outgoing, incoming

    def start(*refs):
        for cp in copies(*refs)[0]:
            cp.start()

    def finish(*refs):
        outgoing, incoming = copies(*refs)
        for cp in incoming:
            cp.wait_recv()
        for cp in outgoing:
            cp.wait_send()

    return Carry(ins=tuple(gathered), out_shapes=tuple(jax.ShapeDtypeStruct(g.shape, g.dtype) for g in gathered),
                 aliases={i: i for i in range(n)},
                 sems=(pltpu.SemaphoreType.DMA((3 * n,)), pltpu.SemaphoreType.DMA((3 * n,))), start=start, finish=finish)


def gather_whole(blocks):
    send = gather_send(blocks)
    forward = gather_forward(send.out_shapes)
    n = len(send.sems)

    def relay(ins, outs, sems):
        send.finish(ins, outs, sems[:n])
        forward.start((), outs, sems[n:])

    return Carry(ins=send.ins, out_shapes=send.out_shapes, aliases={}, sems=send.sems + forward.sems,
                 start=lambda ins, outs, sems: send.start(ins, outs, sems[:n]), relay=relay,
                 finish=lambda ins, outs, sems: forward.finish((), outs, sems[n:]))


def compose(*carries):
    def split(refs, count):
        out, at = [], 0
        for c in carries:
            n = count(c)
            out.append(refs[at:at + n])
            at += n
        return out

    def each(stage):
        def run(ins, outs, sems):
            parts = zip(carries, split(ins, lambda c: len(c.ins)), split(outs, lambda c: len(c.out_shapes)),
                        split(sems, lambda c: len(c.sems)))
            for c, i, o, s in parts:
                if getattr(c, stage) is not None:
                    getattr(c, stage)(i, o, s)
        return run

    aliases, n_in, n_out = {}, 0, 0
    for c in carries:
        aliases.update({n_in + i: n_out + o for i, o in c.aliases.items()})
        n_in += len(c.ins)
        n_out += len(c.out_shapes)
    return Carry(ins=sum((tuple(c.ins) for c in carries), ()), out_shapes=sum((tuple(c.out_shapes) for c in carries), ()),
                 aliases=aliases, sems=sum((tuple(c.sems) for c in carries), ()), start=each("start"), finish=each("finish"),
                 relay=each("relay") if any(c.relay is not None for c in carries) else None)


def swap_halves(by_core):
    n = len(by_core)

    def copies(in_refs, out_refs, sems):
        send, recv = sems
        x, y, c = _place()
        return [pltpu.make_async_remote_copy(src_ref=a.at[:, 1 - c], dst_ref=r, send_sem=send.at[i], recv_sem=recv.at[i],
                                             device_id=(x, y, 1 - c), device_id_type=MESH)
                for i, (a, r) in enumerate(zip(in_refs, out_refs))]

    def start(*refs):
        for cp in copies(*refs):
            cp.start()

    def finish(*refs):
        for cp in copies(*refs):
            cp.wait()

    shapes = tuple(jax.ShapeDtypeStruct((a.shape[0],) + a.shape[2:], a.dtype) for a in by_core)
    return Carry(ins=tuple(by_core), out_shapes=shapes, aliases={},
                 sems=(pltpu.SemaphoreType.DMA((n,)), pltpu.SemaphoreType.DMA((n,))), start=start, finish=finish)


def exchange_between_chips(by_chip):
    n = len(by_chip)

    def copies(in_refs, out_refs, sems):
        send, recv = sems
        x, y, c = _place()
        out = []
        for i, (s, r) in enumerate(zip(in_refs, out_refs)):
            for k, (tx, ty) in enumerate(_chips_across(x, y)):
                out.append(pltpu.make_async_remote_copy(
                    src_ref=s.at[2 * tx + ty], dst_ref=r.at[k], send_sem=send.at[3 * i + k], recv_sem=recv.at[3 * i + k],
                    device_id=(tx, ty, c), device_id_type=MESH))
        return out

    def start(*refs):
        for cp in copies(*refs):
            cp.start()

    def finish(*refs):
        for cp in copies(*refs):
            cp.wait()

    shapes = tuple(jax.ShapeDtypeStruct((3,) + a.shape[1:], a.dtype) for a in by_chip)
    return Carry(ins=tuple(by_chip), out_shapes=shapes, aliases={},
                 sems=(pltpu.SemaphoreType.DMA((3 * n,)), pltpu.SemaphoreType.DMA((3 * n,))), start=start, finish=finish)


def run_exchange(carry, name):
    n_in = len(carry.ins)
    n_out = len(carry.out_shapes)

    def body(*refs):
        parts = refs[:n_in], refs[n_in:n_in + n_out], refs[n_in + n_out:]
        carry.start(*parts)
        carry.finish(*parts)

    return pl.pallas_call(
        body, name=name, out_shape=tuple(carry.out_shapes), in_specs=[_ANY] * n_in, out_specs=(_ANY,) * n_out,
        scratch_shapes=list(carry.sems), input_output_aliases=dict(carry.aliases),
    )(*carry.ins)


def pair_sum(my_core, by_core, received, name):
    n = len(by_core)

    def body(core_ref, *refs):
        for a_ref, b_ref, o_ref in zip(refs[:n], refs[n:2 * n], refs[2 * n:]):
            o_ref[0] = (a_ref[0, 0].astype(F32) + b_ref[0].astype(F32)).astype(BF16)

    mine = [pl.BlockSpec((1, 1) + a.shape[2:], lambda j, core: (j, core[0], 0, 0)) for a in by_core]
    theirs = [pl.BlockSpec((1,) + r.shape[1:], lambda j, core: (j, 0, 0)) for r in received]
    return pl.pallas_call(
        body, name=name, out_shape=tuple(jax.ShapeDtypeStruct(r.shape, BF16) for r in received),
        grid_spec=pltpu.PrefetchScalarGridSpec(num_scalar_prefetch=1, grid=(by_core[0].shape[0],), in_specs=mine + theirs,
                                               out_specs=tuple(theirs)),
        compiler_params=_params("arbitrary"),
    )(my_core, *by_core, *received)


def _adamw_math(w, g, m, v):
    m = ADAM_B1 * m + (1.0 - ADAM_B1) * g
    v = ADAM_B2 * v + (1.0 - ADAM_B2) * (g * g)
    m_hat = m / (1.0 - ADAM_B1 ** ADAM_STEP)
    v_hat = v / (1.0 - ADAM_B2 ** ADAM_STEP)
    delta = -ADAM_LR * (m_hat / (jnp.sqrt(v_hat) + ADAM_EPS) + ADAM_WD * w)
    return delta, m, v


def adamw(my_chip, ws, gs, ms, vs, name, carry=None):
    n = len(ws)
    flat, widths = [], []
    for g in gs:
        parts = list(g) if isinstance(g, (tuple, list)) else [g]
        flat += parts
        widths.append(len(parts))
    c_ins = list(carry.ins) if carry else []
    c_outs = list(carry.out_shapes) if carry else []

    def body(chip_ref, *refs):
        w_refs, refs = refs[:n], refs[n:]
        g_refs, refs = refs[:len(flat)], refs[len(flat):]
        m_refs, v_refs, refs = refs[:n], refs[n:2 * n], refs[2 * n:]
        carried = (refs[:len(c_ins)], refs[len(c_ins) + 4 * n:len(c_ins) + 4 * n + len(c_outs)],
                   refs[len(c_ins) + 4 * n + len(c_outs):])
        outs = refs[len(c_ins):]
        if carry:
            carry.start(*carried)
        at = 0
        for t in range(n):
            if widths[t] == 1:
                g = g_refs[at][...]
            else:
                g = g_refs[at][0].astype(F32)
                for k in range(3):
                    g = g + g_refs[at + 1][k].astype(F32)
            at += widths[t]
            outs[4 * t][...] = g
            outs[4 * t + 1][...], outs[4 * t + 2][...], outs[4 * t + 3][...] = _adamw_math(
                w_refs[t][...], g, m_refs[t][...], v_refs[t][...])
        if carry:
            if carry.relay is not None:
                carry.relay(*carried)
            carry.finish(*carried)

    def whole(a):
        zeros = (0,) * a.ndim
        return pl.BlockSpec(a.shape, lambda i, chip: zeros, pipeline_mode=pl.Buffered(1))

    g_specs = []
    for g in gs:
        if isinstance(g, (tuple, list)):
            g_specs += [pl.BlockSpec((1,) + g[0].shape[1:], lambda i, chip: (chip[0], 0, 0), pipeline_mode=pl.Buffered(1)),
                        whole(g[1])]
        else:
            g_specs.append(whole(g))
    shapes, out_specs = [], []
    for w in ws:
        shapes += [jax.ShapeDtypeStruct(w.shape, F32)] * 4
        out_specs += [whole(w)] * 4
    res = pl.pallas_call(
        body, name=name, out_shape=tuple(shapes) + tuple(c_outs),
        grid_spec=pltpu.PrefetchScalarGridSpec(
            num_scalar_prefetch=1, grid=(1,),
            in_specs=[whole(w) for w in ws] + g_specs + [whole(a) for a in ms + vs] + [_ANY] * len(c_ins),
            out_specs=tuple(out_specs) + (_ANY,) * len(c_outs), scratch_shapes=list(carry.sems) if carry else []),
        compiler_params=_params("arbitrary"),
    )(my_chip, *ws, *flat, *ms, *vs, *c_ins)
    return [tuple(res[4 * t:4 * t + 4]) for t in range(n)], tuple(res[4 * n:])


def adamw_replicated(w, partials, m, v, name):
    def body(w_ref, p_ref, m_ref, v_ref, g_ref, d_ref, mo_ref, vo_ref):
        g = p_ref[0]
        for k in range(1, N_DEV):
            g = g + p_ref[k]
        g_ref[...] = g
        d_ref[...], mo_ref[...], vo_ref[...] = _adamw_math(w_ref[...], g, m_ref[...], v_ref[...])

    shape = jax.ShapeDtypeStruct(w.shape, F32)
    return pl.pallas_call(body, name=name, out_shape=(shape,) * 4, compiler_params=_params())(w, partials, m, v)


PACK_COLS = 1024
PACK_ROW_ALIGN = 16

REPLICATED = ("ffn1_norm", "mix_norm", "conv_dw_b", "conv_ln_g", "conv_ln_b", "ffn2_norm", "final_norm", "gate_b", "attn_sinks")
WEIGHT_ORDER = ("ffn1_norm", "ffn1_w_gate", "ffn1_w_up", "ffn1_w_down", "mix_norm", "w_in", "conv_dw_w", "conv_dw_b", "conv_ln_g",
                "conv_ln_b", "conv_w_proj", "attn_sinks", "attn_w_o", "gate_b", "w_out", "ffn2_norm", "ffn2_w_gate", "ffn2_w_up",
                "ffn2_w_down", "final_norm")


def _to_rows(flat, lead):
    n = flat.shape[-1]
    rows = -(-n // PACK_COLS)
    flat = jnp.pad(flat, [(0, 0)] * lead + [(0, rows * PACK_COLS - n)])
    return flat.reshape(flat.shape[:lead] + (rows, PACK_COLS))


def _pad_rows(a, axis):
    rows = a.shape[axis]
    pad = -rows % PACK_ROW_ALIGN
    widths = [(0, 0)] * a.ndim
    widths[axis] = (0, pad)
    return jnp.pad(a, widths)


def _from_rows(rows, shape):
    n = 1
    for s in shape:
        n *= s
    return rows.reshape(rows.shape[:-2] + (-1,))[..., :n].reshape(rows.shape[:-2] + tuple(shape))


def _heads_slot_major(rows):
    group = rows.shape[0] // (N_KV_HEADS * HEAD_DIM)
    return rows.reshape(N_KV_HEADS, group, HEAD_DIM, rows.shape[1]).transpose(1, 0, 2, 3).reshape(rows.shape)


def _heads_kv_major(rows):
    group = rows.shape[0] // (N_KV_HEADS * HEAD_DIM)
    return rows.reshape(group, N_KV_HEADS, HEAD_DIM, rows.shape[1]).transpose(1, 0, 2, 3).reshape(rows.shape)


def _by_core(full_rows):
    return full_rows.reshape((N_DEV // 2, 2, full_rows.shape[0] // N_DEV, full_rows.shape[1]))


def kernel(x, positions, ffn1_norm, ffn1_w_gate, ffn1_w_up, ffn1_w_down, mix_norm, w_in, conv_dw_w, conv_dw_b, conv_ln_g, conv_ln_b, conv_w_proj, attn_sinks, attn_w_o, gate_b, w_out, ffn2_norm, ffn2_w_gate, ffn2_w_up, ffn2_w_down, final_norm, loss_target, m_ffn1_norm, m_ffn1_w_gate, m_ffn1_w_up, m_ffn1_w_down, m_mix_norm, m_w_in, m_conv_dw_w, m_conv_dw_b, m_conv_ln_g, m_conv_ln_b, m_conv_w_proj, m_attn_sinks, m_attn_w_o, m_gate_b, m_w_out, m_ffn2_norm, m_ffn2_w_gate, m_ffn2_w_up, m_ffn2_w_down, m_final_norm, v_ffn1_norm, v_ffn1_w_gate, v_ffn1_w_up, v_ffn1_w_down, v_mix_norm, v_w_in, v_conv_dw_w, v_conv_dw_b, v_conv_ln_g, v_conv_ln_b, v_conv_w_proj, v_attn_sinks, v_attn_w_o, v_gate_b, v_w_out, v_ffn2_norm, v_ffn2_w_gate, v_ffn2_w_up, v_ffn2_w_down, v_final_norm):
    given = dict(locals())
    shapes = {n: given[n].shape for n in WEIGHT_ORDER}
    w = {n: given[n].reshape(given[n].shape[-2:]) if given[n].ndim == 3 else given[n].reshape(1, -1) for n in WEIGHT_ORDER}
    m = {n: given["m_" + n].reshape(w[n].shape) for n in WEIGHT_ORDER}
    v = {n: given["v_" + n].reshape(w[n].shape) for n in WEIGHT_ORDER}
    my_x, my_y, my_c = _place()
    my_core = my_c.astype(jnp.int32).reshape(1)
    my_chip = (2 * my_x + my_y).astype(jnp.int32).reshape(1)
    xs, target = x[0], loss_target[0]
    T, D = xs.shape
    KV = N_KV_HEADS * HEAD_DIM
    K = w["conv_dw_w"].shape[0]

    def t16(n):
        return w[n].T.astype(BF16)

    def r16(n):
        return w[n].astype(BF16)

    blocks1 = [t16("ffn1_w_gate"), t16("ffn1_w_up"), r16("ffn1_w_down")]
    dw_bits = _pad_rows(_to_rows(lax.bitcast_convert_type(w["conv_dw_w"], BF16).reshape(-1), 0), 0)
    blocks2 = [t16("w_in"), r16("conv_w_proj"), r16("attn_w_o"), r16("w_out"), dw_bits]
    blocks3 = [t16("ffn2_w_gate"), t16("ffn2_w_up"), r16("ffn2_w_down")]
    cos, sin = rope_tables(positions[0])
    sink_col = jnp.repeat(w["attn_sinks"].reshape(-1), WINDOW).reshape(N_KV_HEADS, (D // KV) * WINDOW, 1)

    def full(gathered):
        return gathered.reshape(-1, gathered.shape[2])

    wgt1, wut1 = (full(g) for g in all_gather(blocks1[:2], "gather_ffn1_up"))
    (h1, a1, b1, s1), got = ffn_up(xs, w["ffn1_norm"], wgt1, wut1, "ffn1_up",
                                   carry=compose(gather_whole(blocks1[2:]), gather_send(blocks2[:1])))
    wd1 = full(got[0])
    (x1,), got = ffn_down(xs, s1, wd1, "ffn1_down", carry=compose(gather_forward(got[1:]), gather_send(blocks2[1:])))
    wint = full(got[0])
    wq = _heads_slot_major(wint[2 * D:3 * D])
    (h2, ga, gb, u0, q, sgc, sgt, kk, vv), got = mix_in_fwd(x1, w["mix_norm"], wint, wq, w["gate_b"], cos, sin, "mix_in_fwd",
                                                            carry=compose(gather_forward(got[1:]), gather_send(blocks3)))
    wcp, wo, wout = (full(g) for g in got[:3])
    wo = _heads_slot_major(wo)
    dw_full = lax.bitcast_convert_type(_from_rows(got[3], w["conv_dw_w"].shape + (2,)), F32)
    dw_full = dw_full.transpose(1, 0, 2).reshape(K, D)
    o, gath3 = attn_fwd(q, kk, vv, sink_col, "attn_fwd", carry=gather_forward(got[4:]))
    x2, u1, co, ao, merged = mix_out_fwd(x1, u0, o, sgc, sgt, dw_full, w["conv_dw_b"], w["conv_ln_g"], w["conv_ln_b"],
                                         wcp, wo, wout, "mix_out_fwd")
    wgt2, wut2, wd2 = (full(g) for g in gath3)
    loss, dx3, d_final, h3, a2, b2, s2 = ffn_fwd_loss(x2, w["ffn2_norm"], wgt2, wut2, wd2, w["final_norm"], target,
                                                      "ffn2_fwd_loss")

    small = {"final_norm": d_final}
    (da2, db2), _ = ffn_bwd_hidden(dx3, a2, b2, wd2, "ffn2_bwd_hidden")
    (dx2, small["ffn2_norm"]), _ = ffn_bwd_input(dx3, x2, w["ffn2_norm"], da2, db2, wgt2, wut2, "ffn2_bwd_input")
    core2 = [_by_core(g) for g in (wgrad(da2, h3, "ffn2_dwg")[0], wgrad(db2, h3, "ffn2_dwu")[0],
                                   wgrad(s2, dx3, "ffn2_dwd", b_scale=FFN_SCALE)[0])]
    (dgc, dgt, do, du1, dao, sums, g_wout, g_wcp), recv2 = mix_out_bwd(
        dx2, u1, co, ao, sgc, sgt, merged, w["conv_ln_g"], w["conv_ln_b"], wcp, wo, wout, "mix_out_bwd",
        carry=swap_halves(core2))
    chip2 = pair_sum(my_core, core2, recv2, "ffn2_grads_pair_sum")
    small["gate_b"] = jnp.concatenate([sums[0:1], sums[1:2]], axis=1)
    small["conv_ln_g"], small["conv_ln_b"], small["conv_dw_b"] = sums[2:3], sums[3:4], sums[4:5]
    g_wo =_heads_kv_major(wgrad(o, dao, "dw_attn_o")[0])
    (dga, dgb, g_dw), got2 = conv_bwd(du1, u0, ga, gb, dw_full, "conv_bwd", carry=exchange_between_chips(chip2))
    dq, dk, dv, dsink = attn_bwd(q, kk, vv, do, sink_col, cos, sin, "attn_bwd")
    small["attn_sinks"] = dsink[:D // KV, :N_KV_HEADS].T.reshape(1, -1)
    pieces = [dga, dgb, dq, dk, dv, dgc, dgt]
    dx1, small["mix_norm"] = mix_in_bwd(dx2, x1, w["mix_norm"], wint, wq, pieces, "mix_in_bwd")
    group = D // KV
    q_moves = [(2 * D + HEAD_DIM * (group * g + hh), 2 * D + HEAD_DIM * (N_KV_HEADS * hh + g), HEAD_DIM)
               for g in range(N_KV_HEADS) for hh in range(group)]
    g_wint = wgrad_stacked(pieces[:3], h2, "dw_in_a", wint.shape[0], 0, moves=[(0, 0, 2 * D)] + q_moves)
    g_wint = wgrad_stacked(pieces[3:], h2, "dw_in_b", wint.shape[0], 3 * D, into=g_wint)
    corem = [_by_core(a) for a in (g_wint, g_wcp, g_wo, g_wout)]

    (da1, db1), recvm = ffn_bwd_hidden(dx1, a1, b1, wd1, "ffn1_bwd_hidden", carry=swap_halves(corem))
    chipm = pair_sum(my_core, corem, recvm, "mix_grads_pair_sum")
    g1c, gotm_a = wgrad(s1, dx1, "ffn1_dwd", carry=exchange_between_chips(chipm[:1]), b_scale=FFN_SCALE)
    g1a, gotm_b = wgrad(da1, h1, "ffn1_dwg", carry=exchange_between_chips(chipm[1:]))
    g1b = wgrad(db1, h1, "ffn1_dwu")[0]
    core1 = [_by_core(a) for a in (g1a, g1b, g1c)]
    chip1 = pair_sum(my_core, core1, run_exchange(swap_halves(core1), "ffn1_grads_swap"), "ffn1_grads_pair_sum")
    (grad_x, small["ffn1_norm"]), got1 = ffn_bwd_input(dx1, xs, w["ffn1_norm"], da1, db1, wgt1, wut1, "ffn1_bwd_input",
                                                       carry=exchange_between_chips(chip1))

    gotm = gotm_a + gotm_b
    grad_src = {"ffn1_w_gate": (chip1[0], got1[0]), "ffn1_w_up": (chip1[1], got1[1]), "ffn1_w_down": (chip1[2], got1[2]),
                "ffn2_w_gate": (chip2[0], got2[0]), "ffn2_w_up": (chip2[1], got2[1]), "ffn2_w_down": (chip2[2], got2[2]),
                "w_in": (chipm[0], gotm[0]), "conv_w_proj": (chipm[1], gotm[1]), "attn_w_o": (chipm[2], gotm[2]),
                "w_out": (chipm[3], gotm[3])}
    grads = {}

    def pack_small(d, taps, extra):
        rows = [_to_rows(d[n].reshape(-1), 0) for n in REPLICATED] + [taps, _to_rows(extra.reshape(-1), 0)]
        return _pad_rows(jnp.concatenate(rows, axis=0), 0)

    def like(a, ref):
        return a if a.shape == ref.shape else a.T

    def adamw_group(names, name, carry=None):
        refs = [grad_src[n][0][0] if isinstance(grad_src[n], tuple) else grad_src[n] for n in names]
        res, carried = adamw(my_chip, [like(w[n], r) for n, r in zip(names, refs)], [grad_src[n] for n in names],
                             [like(m[n], r) for n, r in zip(names, refs)], [like(v[n], r) for n, r in zip(names, refs)], name,
                             carry=carry)
        for n, outs in zip(names, res):
            grads[n], delta[n], new_m[n], new_v[n] = (like(a, w[n]) for a in outs)
        return carried

    delta, new_m, new_v = {}, {}, {}
    zero, no_taps = jnp.zeros((1, LANES), F32), jnp.zeros((K, D), F32)
    (shares,) = adamw_group(("ffn2_w_gate", "ffn2_w_up", "ffn2_w_down"), "adamw_ffn2",
                            carry=gather_whole([pack_small(small, g_dw, loss)]))
    g_s, d_s, m_s, v_s = adamw_replicated(pack_small(w, no_taps, zero), shares, pack_small(m, no_taps, zero),
                                          pack_small(v, no_taps, zero), "adamw_replicated")
    off = 0
    for n in REPLICATED:
        r = -(-w[n].shape[1] // PACK_COLS)
        grads[n], delta[n], new_m[n], new_v[n] = (_from_rows(a[off:off + r], w[n].shape) for a in (g_s, d_s, m_s, v_s))
        off += r
    shard_cols = w["conv_dw_w"].shape[1]
    grad_src["conv_dw_w"] = lax.dynamic_slice_in_dim(g_s[off:off + K], _dev_index(my_x, my_y, my_c) * shard_cols, shard_cols,
                                                    axis=1)
    total_loss = g_s[off + K, 0]
    adamw_group(("ffn1_w_gate", "ffn1_w_up", "ffn1_w_down"), "adamw_ffn1")
    adamw_group(("w_in", "conv_dw_w", "conv_w_proj", "attn_w_o", "w_out"), "adamw_mix")

    out = [total_loss, grad_x[None]]
    for d in (grads, delta, new_m, new_v):
        out += [d[n].reshape(shapes[n]) for n in WEIGHT_ORDER]
    return tuple(out)
```

```python
import functools
from typing import Callable, NamedTuple

import jax
import jax.numpy as jnp
from jax import lax
from jax.experimental import pallas as pl
from jax.experimental.pallas import tpu as pltpu

F32, BF16 = jnp.float32, jnp.bfloat16

HEAD_DIM = 64
N_KV_HEADS = 4
WINDOW = 128
ROPE_THETA = 10000.0
EPS = 1e-6
LN_EPS = 1e-5
NEG_INF = -1e30
ADAM_LR, ADAM_B1, ADAM_B2, ADAM_EPS, ADAM_WD, ADAM_STEP = 0.001, 0.9, 0.999, 1e-08, 0.01, 10

N_DEV = 8
LANES = 128
SUBLANES = 8
CONV_HALO = 32
CONV_ROWS, CONV_LANES = 64, 256
ROW_TILE = 512
FFN_CHUNK = 256
FFN_SCALE = 0.5
WGRAD_TILE_ELEMS = 2 ** 22
WGRAD_TILE_ROWS = 2048
WGRAD_VMEM = 40 * 2 ** 20
VMEM_LIMIT = 56 * 2 ** 20
MESH = pl.DeviceIdType.MESH


def _params(*sem):
    return pltpu.CompilerParams(dimension_semantics=sem or None, vmem_limit_bytes=VMEM_LIMIT)


def _resident(shape):
    zeros = (0,) * len(shape)
    return pl.BlockSpec(shape, lambda *_: zeros, pipeline_mode=pl.Buffered(1))


def _rows(tm, n):
    return pl.BlockSpec((tm, n), lambda i: (i, 0))


def _acc_spec(shape):
    zeros = (0,) * len(shape)
    return pl.BlockSpec(shape, lambda *_: zeros)


_ANY = pl.BlockSpec(memory_space=pl.ANY)


class Carry(NamedTuple):
    ins: tuple
    out_shapes: tuple
    aliases: dict
    sems: tuple
    start: Callable
    finish: Callable
    relay: Callable = None


def _call(body, *, name, grid, in_specs, out_specs, out_shape, args, scratch_shapes=(), carry=None):
    n_in, n_out, n_scr = len(in_specs), len(out_specs), len(scratch_shapes)
    params = _params(*(("arbitrary",) * len(grid)))
    if carry is None:
        res = pl.pallas_call(body, name=name, grid=grid, in_specs=list(in_specs), out_specs=tuple(out_specs),
                             out_shape=tuple(out_shape), scratch_shapes=list(scratch_shapes), compiler_params=params)(*args)
        return tuple(res), ()
    c_in, c_out = len(carry.ins), len(carry.out_shapes)

    def wrapped(*refs):
        ins, c_ins = refs[:n_in], refs[n_in:n_in + c_in]
        p = n_in + c_in
        outs, c_outs = refs[p:p + n_out], refs[p + n_out:p + n_out + c_out]
        p += n_out + c_out
        scr, c_sems = refs[p:p + n_scr], refs[p + n_scr:]
        ids = [pl.program_id(d) for d in range(len(grid))]
        first = functools.reduce(jnp.logical_and, [i == 0 for i in ids])
        last = functools.reduce(jnp.logical_and, [i == n - 1 for i, n in zip(ids, grid)])

        @pl.when(first)
        def _():
            carry.start(c_ins, c_outs, c_sems)

        body(*ins, *outs, *scr)

        if carry.relay is not None:
            @pl.when(ids[0] == (3 * grid[0]) // 4)
            def _():
                carry.relay(c_ins, c_outs, c_sems)

        @pl.when(last)
        def _():
            carry.finish(c_ins, c_outs, c_sems)

    res = pl.pallas_call(
        wrapped, name=name, grid=grid, in_specs=list(in_specs) + [_ANY] * c_in, out_specs=tuple(out_specs) + (_ANY,) * c_out,
        out_shape=tuple(out_shape) + tuple(carry.out_shapes), scratch_shapes=list(scratch_shapes) + list(carry.sems),
        input_output_aliases={n_in + i: n_out + o for i, o in carry.aliases.items()}, compiler_params=params,
    )(*args, *carry.ins)
    return tuple(res[:n_out]), tuple(res[n_out:])


def _nt(a, b):
    return lax.dot_general(a, b, (((1,), (1,)), ((), ())), preferred_element_type=F32)


def _tn(a, b):
    return lax.dot_general(a, b, (((0,), (0,)), ((), ())), preferred_element_type=F32)


def _dot(a, b):
    return jnp.dot(a, b, preferred_element_type=F32)


def _sigmoid(x):
    return 1.0 / (1.0 + jnp.exp(-x))


def _rms_fwd(x, g):
    r = lax.rsqrt(jnp.mean(x * x, axis=-1, keepdims=True) + EPS)
    return (x * r) * g


def _rms_bwd(x, g, dy):
    r = lax.rsqrt(jnp.mean(x * x, axis=-1, keepdims=True) + EPS)
    xhat = x * r
    dyg = dy * g
    dx = r * (dyg - xhat * jnp.mean(dyg * xhat, axis=-1, keepdims=True))
    return dx, dy * xhat


def _rot_half(x):
    lane = lax.broadcasted_iota(jnp.int32, (x.shape[0], LANES), 1)
    first = (lane % HEAD_DIM) < (HEAD_DIM // 2)
    out = []
    for s in range(x.shape[1] // LANES):
        xs = x[:, LANES * s:LANES * (s + 1)]
        out.append(jnp.where(first, pltpu.roll(xs, LANES - HEAD_DIM // 2, 1), pltpu.roll(xs, HEAD_DIM // 2, 1)))
    return out[0] if len(out) == 1 else jnp.concatenate(out, axis=1)


def _tile_lanes(t, width):
    return t if width == LANES else jnp.concatenate([t] * (width // LANES), axis=1)


def _rope_fwd(x, cos, sin_signed):
    w = x.shape[1]
    return x * _tile_lanes(cos, w) + _rot_half(x) * _tile_lanes(sin_signed, w)


def _rope_bwd(dy, cos, sin_signed):
    w = dy.shape[1]
    return dy * _tile_lanes(cos, w) + _rot_half(dy * _tile_lanes(sin_signed, w))


def _ffn_rows(x, g_ref, wg_ref, wu_ref, wd_ref, h_ref, a_ref, b_ref, s_ref, acc_ref):
    F = wg_ref.shape[0]
    h = _rms_fwd(x, g_ref[...]).astype(BF16)
    h_ref[...] = h
    for c in range(F // FFN_CHUNK):
        cs = pl.ds(c * FFN_CHUNK, FFN_CHUNK)
        a = _nt(h, wg_ref[cs, :])
        b = _nt(h, wu_ref[cs, :])
        a_ref[:, cs] = a.astype(BF16)
        b_ref[:, cs] = b.astype(BF16)
        s = (a * _sigmoid(a) * b).astype(BF16)
        s_ref[:, cs] = s
        y = _dot(s, wd_ref[cs, :])
        if c == 0:
            acc_ref[...] = y
        else:
            acc_ref[...] += y
    return x + FFN_SCALE * acc_ref[...]


def ffn_up(x, gain, wgt, wut, name, carry=None):
    T, D = x.shape
    F = wgt.shape[0]
    tm = min(ROW_TILE, T)

    def body(x_ref, g_ref, wg_ref, wu_ref, h_ref, a_ref, b_ref, s_ref):
        h = _rms_fwd(x_ref[...], g_ref[...]).astype(BF16)
        h_ref[...] = h
        for c in range(F // FFN_CHUNK):
            cs = pl.ds(c * FFN_CHUNK, FFN_CHUNK)
            a = _nt(h, wg_ref[cs, :])
            b = _nt(h, wu_ref[cs, :])
            a_ref[:, cs] = a.astype(BF16)
            b_ref[:, cs] = b.astype(BF16)
            s_ref[:, cs] = (a * _sigmoid(a) * b).astype(BF16)

    wide = jax.ShapeDtypeStruct((T, F), BF16)
    return _call(
        body, name=name, grid=(T // tm,), out_shape=(jax.ShapeDtypeStruct((T, D), BF16), wide, wide, wide),
        in_specs=[_rows(tm, D), _resident((1, D)), _resident((F, D)), _resident((F, D))],
        out_specs=(_rows(tm, D), _rows(tm, F), _rows(tm, F), _rows(tm, F)), args=(x, gain, wgt, wut), carry=carry)


def ffn_down(x, s, wd, name, carry=None):
    T, D = x.shape
    F = wd.shape[0]
    tm = min(2 * ROW_TILE, T)

    def body(x_ref, s_ref, wd_ref, xo_ref):
        xo_ref[...] = x_ref[...] + FFN_SCALE * _dot(s_ref[...], wd_ref[...])

    return _call(
        body, name=name, grid=(T // tm,), out_shape=(jax.ShapeDtypeStruct((T, D), F32),),
        in_specs=[_rows(tm, D), _rows(tm, F), _resident((F, D))], out_specs=(_rows(tm, D),), args=(x, s, wd), carry=carry)


def ffn_fwd_loss(x, gain, wgt, wut, wd, final_gain, target, name):
    T, D = x.shape
    F = wgt.shape[0]
    tm = min(ROW_TILE, T)

    def body(x_ref, g_ref, wg_ref, wu_ref, wd_ref, gf_ref, t_ref,
             loss_ref, dx_ref, dg_ref, h_ref, a_ref, b_ref, s_ref, acc_ref):
        xo = _ffn_rows(x_ref[...], g_ref, wg_ref, wu_ref, wd_ref, h_ref, a_ref, b_ref, s_ref, acc_ref)
        gf = gf_ref[...]
        err = _rms_fwd(xo, gf) - t_ref[...]
        dx, dgt = _rms_bwd(xo, gf, err * (1.0 / D))
        dx_ref[...] = dx

        @pl.when(pl.program_id(0) == 0)
        def _():
            dg_ref[...] = jnp.zeros_like(dg_ref)
            loss_ref[...] = jnp.zeros_like(loss_ref)
        dg_ref[...] += jnp.sum(dgt, axis=0, keepdims=True)
        per_token = jnp.sum(err * err, axis=-1, keepdims=True) * (0.5 / D)
        loss_ref[...] += jnp.broadcast_to(jnp.sum(per_token, axis=0, keepdims=True), (1, LANES))

    return pl.pallas_call(
        body, name=name, grid=(T // tm,),
        out_shape=(jax.ShapeDtypeStruct((1, LANES), F32), jax.ShapeDtypeStruct((T, D), F32), jax.ShapeDtypeStruct((1, D), F32),
                   jax.ShapeDtypeStruct((T, D), BF16), jax.ShapeDtypeStruct((T, F), BF16), jax.ShapeDtypeStruct((T, F), BF16),
                   jax.ShapeDtypeStruct((T, F), BF16)),
        in_specs=[_rows(tm, D), _resident((1, D)), _resident((F, D)), _resident((F, D)), _resident((F, D)), _resident((1, D)),
                  _rows(tm, D)],
        out_specs=(_acc_spec((1, LANES)), _rows(tm, D), _acc_spec((1, D)), _rows(tm, D), _rows(tm, F), _rows(tm, F),
                   _rows(tm, F)),
        scratch_shapes=[pltpu.VMEM((tm, D), F32)], compiler_params=_params("arbitrary"),
    )(x, gain, wgt, wut, wd, final_gain, target)


def ffn_bwd_hidden(dxo, a, b, s, wd, name, carry=None):
    T, D = dxo.shape
    F = wd.shape[0]
    tm = min(ROW_TILE // 2, T)
    nt = T // tm
    fc = FFN_CHUNK

    def hidden_body(dxo_ref, a_ref, b_ref, s_ref, wd_ref, da_ref, db_ref, gwd_ref, acc_ref):
        i = pl.program_id(0)
        g0 = (FFN_SCALE * dxo_ref[...]).astype(BF16)
        part = _tn(s_ref[...], g0)

        @pl.when(i == 0)
        def _():
            acc_ref[...] = part

        @pl.when(i > 0)
        def _():
            acc_ref[...] += part

        @pl.when(i == nt - 1)
        def _():
            gwd_ref[...] = acc_ref[...].astype(BF16)

        for c in range(F // fc):
            cs = pl.ds(c * fc, fc)
            ds = _nt(g0, wd_ref[cs, :])
            a = a_ref[:, cs].astype(F32)
            bb = b_ref[:, cs].astype(F32)
            sa = _sigmoid(a)
            da_ref[:, cs] = (ds * bb * (sa * (1.0 + a * (1.0 - sa)))).astype(BF16)
            db_ref[:, cs] = (ds * (a * sa)).astype(BF16)

    wide = jax.ShapeDtypeStruct((T, F), BF16)
    return _call(
        hidden_body, name=name, grid=(nt,), out_shape=(wide, wide, jax.ShapeDtypeStruct((F, D), BF16)),
        in_specs=[_rows(tm, D), _rows(tm, F), _rows(tm, F), _rows(tm, F), _resident((F, D))],
        out_specs=(_rows(tm, F), _rows(tm, F), _resident((F, D))), scratch_shapes=[pltpu.VMEM((F, D), F32)],
        args=(dxo, a, b, s, wd), carry=carry)


def ffn_bwd_input(dxo, x, gain, da, db, wg, wu, name, carry=None):
    T, D = x.shape
    F = wg.shape[0]
    tm = min(ROW_TILE, T)

    def input_body(dxo_ref, x_ref, g_ref, da_ref, db_ref, wg_ref, wu_ref, dx_ref, dg_ref):
        dh = _dot(da_ref[...], wg_ref[...]) + _dot(db_ref[...], wu_ref[...])
        dx, dgt = _rms_bwd(x_ref[...], g_ref[...], dh)
        dx_ref[...] = dxo_ref[...] + dx

        @pl.when(pl.program_id(0) == 0)
        def _():
            dg_ref[...] = jnp.zeros_like(dg_ref)
        dg_ref[...] += jnp.sum(dgt, axis=0, keepdims=True)

    return _call(
        input_body, name=name, grid=(T // tm,),
        out_shape=(jax.ShapeDtypeStruct((T, D), F32), jax.ShapeDtypeStruct((1, D), F32)),
        in_specs=[_rows(tm, D), _rows(tm, D), _resident((1, D)), _rows(tm, F), _rows(tm, F), _resident((F, D)),
                  _resident((F, D))],
        out_specs=(_rows(tm, D), _acc_spec((1, D))), args=(dxo, x, gain, da, db, wg, wu), carry=carry)


def wgrad(a, b, name, carry=None, b_scale=None):
    T, M = a.shape
    N = b.shape[1]
    fixed = M * N * (4 + 2)
    per_row = 2 * (M * a.dtype.itemsize + N * b.dtype.itemsize)
    tk = ROW_TILE
    while fixed + 2 * tk * per_row <= WGRAD_VMEM and 2 * tk <= WGRAD_TILE_ROWS:
        tk *= 2
    tk = min(tk, T)
    nk = T // tk

    def body(a_ref, b_ref, o_ref, acc_ref):
        k = pl.program_id(0)
        bt = b_ref[...] if b_scale is None else b_scale * b_ref[...]
        part = _tn(a_ref[...].astype(BF16), bt.astype(BF16))

        @pl.when(k == 0)
        def _():
            acc_ref[...] = part

        @pl.when(k > 0)
        def _():
            acc_ref[...] += part

        @pl.when(k == nk - 1)
        def _():
            o_ref[...] = acc_ref[...].astype(BF16)

    (out,), carried = _call(
        body, name=name, grid=(nk,), out_shape=(jax.ShapeDtypeStruct((M, N), BF16),),
        in_specs=[_rows(tk, M), _rows(tk, N)], out_specs=(_resident((M, N)),),
        scratch_shapes=[pltpu.VMEM((M, N), F32)], args=(a, b), carry=carry)
    return out, carried


def wgrad_stacked(pieces, b, name, total_rows, at, into=None, moves=None):
    T, N = b.shape
    n = len(pieces)
    widths = [p.shape[1] for p in pieces]
    offs = [sum(widths[:i]) for i in range(n)]
    M = sum(widths)
    tk = ROW_TILE
    while 2 * tk * M <= WGRAD_TILE_ELEMS and 2 * tk <= WGRAD_TILE_ROWS:
        tk *= 2
    tk = min(tk, T)
    nk = T // tk
    moves = moves or [(0, 0, M)]
    n_in = n + 1 + (into is not None)

    def body(*refs):
        a_refs, b_ref = refs[:n], refs[n]
        o_ref, acc_ref, stage, sem = refs[n_in:]
        k = pl.program_id(0)

        @pl.when(k == 0)
        def _():
            acc_ref[...] = jnp.zeros_like(acc_ref)
        bt = b_ref[...].astype(BF16)
        for a_ref, off, width in zip(a_refs, offs, widths):
            acc_ref[pl.ds(off, width), :] += _tn(a_ref[...].astype(BF16), bt)

        @pl.when(k == nk - 1)
        def _():
            for to, start, rows in moves:
                stage[pl.ds(to, rows), :] = acc_ref[pl.ds(start, rows), :].astype(BF16)
            cp = pltpu.make_async_copy(stage, o_ref.at[pl.ds(at, M)], sem)
            cp.start()
            cp.wait()

    return pl.pallas_call(
        body, name=name, grid=(nk,), out_shape=jax.ShapeDtypeStruct((total_rows, N), BF16),
        in_specs=[_rows(tk, width) for width in widths] + [_rows(tk, N)] + [_ANY] * (into is not None), out_specs=_ANY,
        scratch_shapes=[pltpu.VMEM((M, N), F32), pltpu.VMEM((M, N), BF16), pltpu.SemaphoreType.DMA],
        input_output_aliases={n + 1: 0} if into is not None else {}, compiler_params=_params("arbitrary"),
    )(*pieces, b, *([into] if into is not None else []))


def _w_in_rows(D):
    KV = N_KV_HEADS * HEAD_DIM
    return 0, D, 2 * D, 3 * D, 3 * D + KV, 3 * D + 2 * KV, 4 * D + 2 * KV


def mix_in_fwd(x, gain, wint, wq, gate_b, cos, sin_signed, name, carry=None):
    T, D = x.shape
    KV = N_KV_HEADS * HEAD_DIM
    tm = min(ROW_TILE, T)
    o_ga, o_gb, _, o_k, o_v, o_gc, o_gt = _w_in_rows(D)

    def body(x_ref, g_ref, w_ref, wq_ref, gb_ref, cos_ref, sin_ref,
             h_ref, ga_ref, gb_out_ref, u0_ref, q_ref, sgc_ref, sgt_ref, k_ref, v_ref):
        h = _rms_fwd(x_ref[...], g_ref[...]).astype(BF16)
        h_ref[...] = h
        cos, sin = cos_ref[...], sin_ref[...]
        ga = _nt(h, w_ref[pl.ds(o_ga, D), :])
        gb = _nt(h, w_ref[pl.ds(o_gb, D), :])
        ga_ref[...] = ga.astype(BF16)
        gb_out_ref[...] = gb.astype(BF16)
        u0_ref[...] = (ga * _sigmoid(gb)).astype(BF16)
        q = _nt(h, wq_ref[...])
        q_ref[...] = _rope_fwd(q, cos, sin).astype(BF16)
        gc = _nt(h, w_ref[pl.ds(o_gc, D), :]) + gb_ref[:, pl.ds(0, D)]
        sgc_ref[...] = _sigmoid(gc).astype(BF16)
        gt = _nt(h, w_ref[pl.ds(o_gt, D), :]) + gb_ref[:, pl.ds(D, D)]
        sgt_ref[...] = _sigmoid(gt).astype(BF16)
        k = _nt(h, w_ref[pl.ds(o_k, KV), :])
        k_ref[...] = _rope_fwd(k, cos, sin).astype(BF16)
        v_ref[...] = _nt(h, w_ref[pl.ds(o_v, KV), :]).astype(BF16)

    big = jax.ShapeDtypeStruct((T, D), BF16)
    small = jax.ShapeDtypeStruct((T, KV), BF16)
    return _call(
        body, name=name, grid=(T // tm,),
        out_shape=(big, big, big, big, big, big, big, small, small),
        in_specs=[_rows(tm, D), _resident((1, D)), _resident(wint.shape), _resident(wq.shape), _resident((1, 2 * D)),
                  _rows(tm, LANES), _rows(tm, LANES)],
        out_specs=(_rows(tm, D),) * 7 + (_rows(tm, KV),) * 2,
        args=(x, gain, wint, wq, gate_b, cos, sin_signed), carry=carry)


def _from_prev(rows):
    qi = lax.broadcasted_iota(jnp.int32, (rows, WINDOW), 0) % WINDOW
    return lax.broadcasted_iota(jnp.int32, (rows, WINDOW), 1) > qi


def _fold(x, g, from_prev):
    lo = 2 * WINDOW * g
    return jnp.where(from_prev, x[:, lo:lo + WINDOW], x[:, lo + WINDOW:lo + 2 * WINDOW])


def _unfold(folded, from_prev):
    zero = jnp.zeros_like(folded[0])
    parts = []
    for x in folded:
        parts += [jnp.where(from_prev, x, zero), jnp.where(from_prev, zero, x)]
    return jnp.concatenate(parts, axis=1)


def _kv_lane_head(rows, width):
    return lax.broadcasted_iota(jnp.int32, (rows, width), 1) // HEAD_DIM


def _block_diag(win):
    head = _kv_lane_head(*win.shape)
    zero = jnp.zeros_like(win)
    return jnp.concatenate([jnp.where(head == g, win, zero) for g in range(N_KV_HEADS)], axis=0)


def _diag_blocks_sum(bd, keys):
    head = _kv_lane_head(keys, bd.shape[1])
    out = jnp.zeros((keys, bd.shape[1]), F32)
    for g in range(N_KV_HEADS):
        out = jnp.where(head == g, bd[g * keys:(g + 1) * keys], out)
    return out


def _kv_windows(k_ref, kh_ref, v_ref, vh_ref, j):
    rows = pl.ds(j * WINDOW, WINDOW)
    if j == 0:
        kprev, vprev = kh_ref[...], vh_ref[...]
    else:
        prev = pl.ds((j - 1) * WINDOW, WINDOW)
        kprev, vprev = k_ref[prev, :], v_ref[prev, :]
    return jnp.concatenate([kprev, k_ref[rows, :]], axis=0), jnp.concatenate([vprev, v_ref[rows, :]], axis=0)


def _stack_slots(ref, j, group, KV):
    rows = pl.ds(j * WINDOW, WINDOW)
    return jnp.concatenate([ref[rows, pl.ds(KV * hh, KV)] for hh in range(group)], axis=0)


def _attn_exp(qs, kbd, sink_ref, from_prev, no_prev):
    s = _nt(qs, kbd)
    if no_prev is not None:
        qi = lax.broadcasted_iota(jnp.int32, from_prev.shape, 0) % WINDOW
        absent = lax.broadcasted_iota(jnp.int32, from_prev.shape, 1) > jnp.where(no_prev, qi, WINDOW)
    out = []
    for g in range(N_KV_HEADS):
        sg = _fold(s, g, from_prev) * (HEAD_DIM ** -0.5)
        if no_prev is not None:
            sg = jnp.where(absent, NEG_INF, sg)
        sink = sink_ref[g]
        m = jnp.maximum(jnp.max(sg, axis=-1, keepdims=True), sink)
        out.append((jnp.exp(sg - m), jnp.exp(sink - m)))
    return out


def _spread_over_heads(cols, rows, KV):
    head = _kv_lane_head(rows, KV)
    out = jnp.zeros((rows, KV), F32)
    for g, col in enumerate(cols):
        out = jnp.where(head == g, col, out)
    return out


def _halo_rows_spec(tq, width, sub):
    return pl.BlockSpec((sub, width), lambda i: (jnp.maximum(i * (tq // sub) - 1, 0), 0))


def attn_fwd(q, k, v, sink_col, name, carry=None):
    T, D = q.shape
    KV = k.shape[1]
    group = D // KV
    tq = min(ROW_TILE, T)
    nsub = tq // WINDOW
    rows, wide = group * WINDOW, N_KV_HEADS * 2 * WINDOW

    def body(q_ref, k_ref, kh_ref, v_ref, vh_ref, sink_ref, o_ref):
        from_prev = _from_prev(rows)
        head = _kv_lane_head(wide, KV)
        block = lax.broadcasted_iota(jnp.int32, head.shape, 0) // (2 * WINDOW)
        ones_bd = jnp.where(head == block, 1.0, 0.0).astype(BF16)
        for j in range(nsub):
            k_win, v_win = _kv_windows(k_ref, kh_ref, v_ref, vh_ref, j)
            parts = _attn_exp(_stack_slots(q_ref, j, group, KV), _block_diag(k_win), sink_ref, from_prev,
                              pl.program_id(0) == 0 if j == 0 else None)
            p = _unfold([pg.astype(BF16) for pg, _ in parts], from_prev)
            both = _dot(p, jnp.concatenate([_block_diag(v_win), ones_bd], axis=1))
            denom = both[:, KV:] + _spread_over_heads([es for _, es in parts], rows, KV)
            out = (both[:, :KV] / denom).astype(BF16)
            for hh in range(group):
                o_ref[pl.ds(j * WINDOW, WINDOW), pl.ds(KV * hh, KV)] = out[hh * WINDOW:(hh + 1) * WINDOW]

    (o,), carried = _call(
        body, name=name, grid=(T // tq,),
        out_shape=(jax.ShapeDtypeStruct((T, D), BF16),),
        in_specs=[_rows(tq, D), _rows(tq, KV), _halo_rows_spec(tq, KV, WINDOW), _rows(tq, KV),
                  _halo_rows_spec(tq, KV, WINDOW), _resident(sink_col.shape)],
        out_specs=(_rows(tq, D),), args=(q, k, k, v, v, sink_col), carry=carry)
    return o, carried


def attn_bwd(q, k, v, do, sink_col, cos, sin_signed, name):
    T, D = q.shape
    KV = k.shape[1]
    group = D // KV
    tq = min(ROW_TILE, T)
    nsub = tq // WINDOW
    nt = T // tq
    scale = HEAD_DIM ** -0.5
    rows, wide = group * WINDOW, N_KV_HEADS * 2 * WINDOW

    def rev(i):
        return nt - 1 - i

    def body(q_ref, k_ref, kh_ref, v_ref, vh_ref, do_ref, sink_ref, cos_ref, sin_ref,
             dq_ref, dk_ref, dv_ref, dsink_ref, dq_acc, dk_acc, dv_acc, carry_k, carry_v):
        i = pl.program_id(0)

        @pl.when(i == 0)
        def _():
            carry_k[...] = jnp.zeros_like(carry_k)
            carry_v[...] = jnp.zeros_like(carry_v)
            dsink_ref[...] = jnp.zeros_like(dsink_ref)

        dk_acc[...] = jnp.zeros_like(dk_acc)
        dv_acc[...] = jnp.zeros_like(dv_acc)
        from_prev = _from_prev(rows)
        lane = lax.broadcasted_iota(jnp.int32, (1, LANES), 1)
        for j in range(nsub):
            k_win, v_win = _kv_windows(k_ref, kh_ref, v_ref, vh_ref, j)
            kbd, vbd = _block_diag(k_win), _block_diag(v_win)
            qs, dos = _stack_slots(q_ref, j, group, KV), _stack_slots(do_ref, j, group, KV)
            dp = _nt(dos, vbd)
            probs16, ds16 = [], []
            for g, (pg, es) in enumerate(_attn_exp(qs, kbd, sink_ref, from_prev, rev(i) == 0 if j == 0 else None)):
                inv = 1.0 / (jnp.sum(pg, axis=-1, keepdims=True) + es)
                probs = pg * inv
                dpg = _fold(dp, g, from_prev)
                delta = jnp.sum(probs * dpg, axis=-1, keepdims=True)
                probs16.append(probs.astype(BF16))
                ds16.append((probs * (dpg - delta) * scale).astype(BF16))
                dsk = -(es * inv * delta)
                for hh in range(group):
                    tot = jnp.sum(dsk[hh * WINDOW:(hh + 1) * WINDOW], axis=0, keepdims=True)
                    dsink_ref[pl.ds(hh, 1), :] += jnp.where(lane == g, tot, 0.0)
            ds = _unfold(ds16, from_prev)
            dqs = _dot(ds, kbd)
            for hh in range(group):
                dq_acc[pl.ds(j * WINDOW, WINDOW), pl.ds(KV * hh, KV)] = dqs[hh * WINDOW:(hh + 1) * WINDOW]
            keys = pl.ds(j * WINDOW, 2 * WINDOW)
            dk_acc[keys, :] += _diag_blocks_sum(_tn(ds, qs), 2 * WINDOW)
            dv_acc[keys, :] += _diag_blocks_sum(_tn(_unfold(probs16, from_prev), dos), 2 * WINDOW)

        tail = pl.ds(tq, WINDOW)
        dk_acc[tail, :] += carry_k[...]
        dv_acc[tail, :] += carry_v[...]
        carry_k[...] = dk_acc[pl.ds(0, WINDOW), :]
        carry_v[...] = dv_acc[pl.ds(0, WINDOW), :]
        cos, sin = cos_ref[...], sin_ref[...]
        dq_ref[...] = _rope_bwd(dq_acc[...], cos, sin).astype(BF16)
        dk_ref[...] = _rope_bwd(dk_acc[pl.ds(WINDOW, tq), :], cos, sin).astype(BF16)
        dv_ref[...] = dv_acc[pl.ds(WINDOW, tq), :].astype(BF16)

    def rrows(n):
        return pl.BlockSpec((tq, n), lambda i: (rev(i), 0))

    def rhalo(n):
        return pl.BlockSpec((WINDOW, n), lambda i: (jnp.maximum(rev(i) * nsub - 1, 0), 0))

    return pl.pallas_call(
        body, name=name, grid=(nt,),
        out_shape=(jax.ShapeDtypeStruct((T, D), BF16), jax.ShapeDtypeStruct((T, KV), BF16),
                   jax.ShapeDtypeStruct((T, KV), BF16), jax.ShapeDtypeStruct((SUBLANES, LANES), F32)),
        in_specs=[rrows(D), rrows(KV), rhalo(KV), rrows(KV), rhalo(KV), rrows(D), _resident(sink_col.shape),
                  rrows(LANES), rrows(LANES)],
        out_specs=(rrows(D), rrows(KV), rrows(KV), _acc_spec((SUBLANES, LANES))),
        scratch_shapes=[pltpu.VMEM((tq, D), F32), pltpu.VMEM((WINDOW + tq, KV), F32), pltpu.VMEM((WINDOW + tq, KV), F32),
                        pltpu.VMEM((WINDOW, KV), F32), pltpu.VMEM((WINDOW, KV), F32)],
        compiler_params=_params("arbitrary"),
    )(q, k, k, v, v, do, sink_col, cos, sin_signed)


def _ln_stats(u):
    mu = jnp.mean(u, axis=-1, keepdims=True)
    d = u - mu
    rstd = lax.rsqrt(jnp.mean(d * d, axis=-1, keepdims=True) + LN_EPS)
    return d * rstd, rstd


def _lag_taps(b, K):
    return [(a, K - 1 - (SUBLANES * a + b)) for a in range(-(-K // SUBLANES)) if SUBLANES * a + b <= K - 1]


def _conv_chunks(tm, D, chunk):
    def rows(c, carry):
        r0 = pl.multiple_of(c * CONV_ROWS, CONV_ROWS)
        for l0 in range(0, D, CONV_LANES):
            chunk(r0, pl.ds(l0, CONV_LANES))
        return carry
    lax.fori_loop(0, tm // CONV_ROWS, rows, 0)


def _conv_causal(buf, w_ref, bias_ref, out_ref, tm, D, K):
    def chunk(r0, lanes):
        acc = jnp.broadcast_to(bias_ref[:, lanes], (CONV_ROWS, CONV_LANES))
        for b in range(SUBLANES):
            y = None
            for a, k in _lag_taps(b, K):
                start = pl.multiple_of(r0 + CONV_HALO - SUBLANES * (a + 1), SUBLANES)
                t = buf[pl.ds(start, CONV_ROWS + SUBLANES), lanes] * w_ref[pl.ds(k, 1), lanes]
                y = t if y is None else y + t
            acc = acc + y[SUBLANES - b:SUBLANES - b + CONV_ROWS]
        out_ref[pl.ds(r0, CONV_ROWS), lanes] = acc
    _conv_chunks(tm, D, chunk)


def _conv_anticausal(dbuf, w_ref, out_ref, tm, D, K):
    def chunk(r0, lanes):
        acc = jnp.zeros((CONV_ROWS, CONV_LANES), F32)
        for b in range(SUBLANES):
            y = None
            for a, k in _lag_taps(b, K):
                start = pl.multiple_of(r0 + SUBLANES * a, SUBLANES)
                t = dbuf[pl.ds(start, CONV_ROWS + SUBLANES), lanes] * w_ref[pl.ds(k, 1), lanes]
                y = t if y is None else y + t
            acc = acc + y[b:b + CONV_ROWS]
        out_ref[pl.ds(r0, CONV_ROWS), lanes] = acc
    _conv_chunks(tm, D, chunk)


def _conv_tap_grads(dbuf, ubuf, acc_ref, tm, D, K):
    reach = SUBLANES * (-(-K // SUBLANES) - 1)

    def chunk(r0, lanes):
        d = dbuf[pl.ds(r0, CONV_ROWS), lanes]
        around = ubuf[pl.ds(pl.multiple_of(r0 + CONV_HALO - reach - SUBLANES, SUBLANES), CONV_ROWS + reach + SUBLANES), lanes]
        for b in range(SUBLANES):
            shifted = around[SUBLANES - b:SUBLANES - b + CONV_ROWS + reach]
            for a, k in _lag_taps(b, K):
                prod = d * shifted[reach - SUBLANES * a:reach - SUBLANES * a + CONV_ROWS]
                part = prod[0:SUBLANES]
                for i in range(1, CONV_ROWS // SUBLANES):
                    part = part + prod[SUBLANES * i:SUBLANES * (i + 1)]
                acc_ref[k, :, lanes] += part
    _conv_chunks(tm, D, chunk)


def mix_out_fwd(x, u0, o, sgc, sgt, dw_w, dw_b, ln_g, ln_b, wcp, wo, wout, name):
    T, D = x.shape
    tm = min(ROW_TILE, T)
    K = dw_w.shape[0]

    def body(x_ref, u_ref, uh_ref, o_ref, sgc_ref, sgt_ref, w_ref, b_ref, lg_ref, lb_ref, wcp_ref, wo_ref, wout_ref,
             x2_ref, u1_ref, co_ref, ao_ref, mg_ref, buf, conv):
        keep = (pl.program_id(0) > 0).astype(F32)
        buf[pl.ds(0, CONV_HALO), :] = uh_ref[...].astype(F32) * keep
        buf[pl.ds(CONV_HALO, tm), :] = u_ref[...].astype(F32)
        _conv_causal(buf, w_ref, b_ref, conv, tm, D, K)
        acc = conv[...]
        u1_ref[...] = acc.astype(BF16)
        xhat, _ = _ln_stats(acc)
        u2 = xhat * lg_ref[...] + lb_ref[...]
        u3 = (u2 * _sigmoid(u2)).astype(BF16)
        co = _dot(u3, wcp_ref[...])
        ao = _dot(o_ref[...], wo_ref[...])
        co_ref[...] = co.astype(BF16)
        ao_ref[...] = ao.astype(BF16)
        merged = (sgc_ref[...].astype(F32) * co + sgt_ref[...].astype(F32) * ao).astype(BF16)
        mg_ref[...] = merged
        x2_ref[...] = x_ref[...] + _dot(merged, wout_ref[...])

    big = jax.ShapeDtypeStruct((T, D), BF16)
    vec = _resident((1, D))
    return pl.pallas_call(
        body, name=name, grid=(T // tm,),
        out_shape=(jax.ShapeDtypeStruct((T, D), F32), big, big, big, big),
        in_specs=[_rows(tm, D), _rows(tm, D), _halo_rows_spec(tm, D, CONV_HALO), _rows(tm, D), _rows(tm, D), _rows(tm, D),
                  _resident((K, D)), vec, vec, vec, _resident((D, D)), _resident((D, D)), _resident((D, D))],
        out_specs=(_rows(tm, D),) * 5,
        scratch_shapes=[pltpu.VMEM((CONV_HALO + tm, D), F32), pltpu.VMEM((tm, D), F32)],
        compiler_params=_params("arbitrary"),
    )(x, u0, u0, o, sgc, sgt, dw_w, dw_b, ln_g, ln_b, wcp, wo, wout)


def mix_out_bwd(dx2, u1, co, ao, sgc, sgt, merged, ln_g, ln_b, wcp, wo, wout, name, carry=None):
    T, D = dx2.shape
    tm = min(ROW_TILE, T)
    nt = T // tm

    def body(dx_ref, u1_ref, co_ref, ao_ref, sgc_ref, sgt_ref, mg_ref, lg_ref, lb_ref, wcp_ref, wo_ref, wout_ref,
             dgc_ref, dgt_ref, do_ref, du1_ref, dao_ref, sums_ref, gwout_ref, gwcp_ref, acc_out, acc_cp):
        i = pl.program_id(0)

        @pl.when(i == 0)
        def _():
            sums_ref[...] = jnp.zeros_like(sums_ref)
            acc_out[...] = jnp.zeros_like(acc_out)
            acc_cp[...] = jnp.zeros_like(acc_cp)

        dx16 = dx_ref[...].astype(BF16)
        acc_out[...] += _tn(mg_ref[...], dx16)
        dm = _nt(dx16, wout_ref[...])
        sgc, sgt = sgc_ref[...].astype(F32), sgt_ref[...].astype(F32)
        dco = (dm * sgc).astype(BF16)
        dao = (dm * sgt).astype(BF16)
        dgc = dm * co_ref[...].astype(F32) * sgc * (1.0 - sgc)
        dgt = dm * ao_ref[...].astype(F32) * sgt * (1.0 - sgt)
        dao_ref[...] = dao
        dgc_ref[...] = dgc.astype(BF16)
        dgt_ref[...] = dgt.astype(BF16)
        do_ref[...] = _nt(dao, wo_ref[...]).astype(BF16)
        du3 = _nt(dco, wcp_ref[...])
        xhat, rstd = _ln_stats(u1_ref[...].astype(F32))
        g = lg_ref[...]
        u2 = xhat * g + lb_ref[...]
        su = _sigmoid(u2)
        acc_cp[...] += _tn((u2 * su).astype(BF16), dco)
        du2 = du3 * (su * (1.0 + u2 * (1.0 - su)))
        dxh = du2 * g
        du1 = rstd * (dxh - jnp.mean(dxh, axis=-1, keepdims=True) - xhat * jnp.mean(dxh * xhat, axis=-1, keepdims=True))
        du1_ref[...] = du1.astype(BF16)
        for r, val in enumerate((dgc, dgt, du2 * xhat, du2, du1)):
            sums_ref[pl.ds(r, 1), :] += jnp.sum(val, axis=0, keepdims=True)

        @pl.when(i == nt - 1)
        def _():
            gwout_ref[...] = acc_out[...].astype(BF16)
            gwcp_ref[...] = acc_cp[...].astype(BF16)

    big = jax.ShapeDtypeStruct((T, D), BF16)
    square = jax.ShapeDtypeStruct((D, D), BF16)
    vec = _resident((1, D))
    return _call(
        body, name=name, grid=(nt,),
        out_shape=(big,) * 5 + (jax.ShapeDtypeStruct((8, D), F32), square, square),
        in_specs=[_rows(tm, D)] * 7 + [vec, vec, _resident((D, D)), _resident((D, D)), _resident((D, D))],
        out_specs=(_rows(tm, D),) * 5 + (_acc_spec((8, D)), _resident((D, D)), _resident((D, D))),
        scratch_shapes=[pltpu.VMEM((D, D), F32), pltpu.VMEM((D, D), F32)],
        args=(dx2, u1, co, ao, sgc, sgt, merged, ln_g, ln_b, wcp, wo, wout), carry=carry)


def conv_bwd(du1, u0, ga, gb, dw_w, name, carry=None):
    T, D = du1.shape
    tm = min(ROW_TILE, T)
    nt = T // tm
    K = dw_w.shape[0]
    per = tm // CONV_HALO

    def body(d_ref, dn_ref, u_ref, uh_ref, ga_ref, gb_ref, w_ref, dga_ref, dgb_ref, dw_ref, dbuf, ubuf, du0_buf, taps):
        i = pl.program_id(0)
        dbuf[pl.ds(0, tm), :] = d_ref[...].astype(F32)
        dbuf[pl.ds(tm, CONV_HALO), :] = dn_ref[...].astype(F32) * (i < nt - 1).astype(F32)
        ubuf[pl.ds(0, CONV_HALO), :] = uh_ref[...].astype(F32) * (i > 0).astype(F32)
        ubuf[pl.ds(CONV_HALO, tm), :] = u_ref[...].astype(F32)

        @pl.when(i == 0)
        def _():
            taps[...] = jnp.zeros_like(taps)

        _conv_anticausal(dbuf, w_ref, du0_buf, tm, D, K)
        _conv_tap_grads(dbuf, ubuf, taps, tm, D, K)
        du0 = du0_buf[...]
        ga, gb = ga_ref[...].astype(F32), gb_ref[...].astype(F32)
        sg = _sigmoid(gb)
        dga_ref[...] = (du0 * sg).astype(BF16)
        dgb_ref[...] = (du0 * ga * sg * (1.0 - sg)).astype(BF16)

        @pl.when(i == nt - 1)
        def _():
            for k in range(K):
                dw_ref[pl.ds(k, 1), :] = jnp.sum(taps[k], axis=0, keepdims=True)

    nxt = pl.BlockSpec((CONV_HALO, D), lambda i: (jnp.minimum((i + 1) * per, nt * per - 1), 0))
    big = jax.ShapeDtypeStruct((T, D), BF16)
    return _call(
        body, name=name, grid=(nt,),
        out_shape=(big, big, jax.ShapeDtypeStruct((K, D), F32)),
        in_specs=[_rows(tm, D), nxt, _rows(tm, D), _halo_rows_spec(tm, D, CONV_HALO), _rows(tm, D), _rows(tm, D),
                  _resident((K, D))],
        out_specs=(_rows(tm, D), _rows(tm, D), _acc_spec((K, D))),
        scratch_shapes=[pltpu.VMEM((tm + CONV_HALO, D), F32), pltpu.VMEM((CONV_HALO + tm, D), F32), pltpu.VMEM((tm, D), F32),
                        pltpu.VMEM((K, SUBLANES, D), F32)],
        args=(du1, du1, u0, u0, ga, gb, dw_w), carry=carry)


def mix_in_bwd(dx2, x, gain, wint, wq, pieces, name):
    T, D = x.shape
    tm = min(ROW_TILE, T)
    widths = [p.shape[1] for p in pieces]
    offs = _w_in_rows(D)

    def body(dx2_ref, x_ref, g_ref, w_ref, wq_ref, *rest):
        piece_refs, (dx_ref, dg_ref) = rest[:len(pieces)], rest[len(pieces):]
        dh = None
        for n, (p_ref, off, w) in enumerate(zip(piece_refs, offs, widths)):
            t = _dot(p_ref[...], wq_ref[...] if n == 2 else w_ref[pl.ds(off, w), :])
            dh = t if dh is None else dh + t
        dx, dgt = _rms_bwd(x_ref[...], g_ref[...], dh)
        dx_ref[...] = dx2_ref[...] + dx

        @pl.when(pl.program_id(0) == 0)
        def _():
            dg_ref[...] = jnp.zeros_like(dg_ref)
        dg_ref[...] += jnp.sum(dgt, axis=0, keepdims=True)

    return pl.pallas_call(
        body, name=name, grid=(T // tm,),
        out_shape=(jax.ShapeDtypeStruct((T, D), F32), jax.ShapeDtypeStruct((1, D), F32)),
        in_specs=[_rows(tm, D), _rows(tm, D), _resident((1, D)), _resident(wint.shape), _resident(wq.shape)]
        + [_rows(tm, w) for w in widths],
        out_specs=(_rows(tm, D), _acc_spec((1, D))),
        compiler_params=_params("arbitrary"),
    )(dx2, x, gain, wint, wq, *pieces)


def rope_tables(positions):
    half = HEAD_DIM // 2
    inv_freq = ROPE_THETA ** (-jnp.arange(half, dtype=F32) / half)
    ang = positions.astype(F32)[:, None] * inv_freq
    cos, sin = jnp.cos(ang), jnp.sin(ang)
    reps = LANES // HEAD_DIM
    return jnp.tile(jnp.concatenate([cos, cos], axis=-1), (1, reps)), jnp.tile(jnp.concatenate([-sin, sin], axis=-1), (1, reps))


def _place():
    return lax.axis_index("x"), lax.axis_index("y"), lax.axis_index("c")


def all_gather(blocks, name):
    n = len(blocks)
    send = gather_send(blocks)
    forward = gather_forward(send.out_shapes)
    n_sems = len(send.sems)

    def body(*refs):
        ins, outs, sems = refs[:n], refs[n:2 * n], refs[2 * n:]
        send.start(ins, outs, sems[:n_sems])
        send.finish(ins, outs, sems[:n_sems])
        forward.start((), outs, sems[n_sems:])
        forward.finish((), outs, sems[n_sems:])

    return pl.pallas_call(
        body, name=name, out_shape=tuple(send.out_shapes), in_specs=[_ANY] * n, out_specs=(_ANY,) * n,
        scratch_shapes=list(send.sems) + list(forward.sems),
    )(*blocks)


def _chips_across(x, y):
    return [(1 - x, y), (x, 1 - y), (1 - x, 1 - y)]


def _dev_index(x, y, c):
    return 4 * x + 2 * y + c


def gather_send(blocks):
    n = len(blocks)

    def copies(in_refs, out_refs, sems):
        send, recv, local = sems
        x, y, c = _place()
        targets = [(x, y, 1 - c)] + [(*chip, c) for chip in _chips_across(x, y)]
        outgoing, incoming, mine = [], [], []
        for i, (x_ref, out_ref) in enumerate(zip(in_refs, out_refs)):
            for k, t in enumerate(targets):
                pair = dict(send_sem=send.at[4 * i + k], recv_sem=recv.at[4 * i + k], device_id=t, device_id_type=MESH)
                outgoing.append(pltpu.make_async_remote_copy(src_ref=x_ref, dst_ref=out_ref.at[_dev_index(x, y, c)], **pair))
                incoming.append(pltpu.make_async_remote_copy(src_ref=x_ref, dst_ref=out_ref.at[_dev_index(*t)], **pair))
            mine.append(pltpu.make_async_copy(x_ref, out_ref.at[_dev_index(x, y, c)], local.at[i]))
        return outgoing, incoming, mine

    def start(*refs):
        outgoing, _, mine = copies(*refs)
        for cp in mine + outgoing:
            cp.start()

    def finish(*refs):
        outgoing, incoming, mine = copies(*refs)
        for cp in incoming:
            cp.wait_recv()
        for cp in outgoing:
            cp.wait_send()
        for cp in mine:
            cp.wait()

    return Carry(ins=tuple(blocks), out_shapes=tuple(jax.ShapeDtypeStruct((N_DEV,) + b.shape, b.dtype) for b in blocks),
                 aliases={}, sems=(pltpu.SemaphoreType.DMA((4 * n,)), pltpu.SemaphoreType.DMA((4 * n,)),
                                   pltpu.SemaphoreType.DMA((n,))), start=start, finish=finish)


def gather_forward(gathered):
    n = len(gathered)

    def copies(in_refs, out_refs, sems):
        send, recv = sems
        x, y, c = _place()
        outgoing, incoming = [], []
        for i, buf in enumerate(out_refs):
            for k, chip in enumerate(_chips_across(x, y)):
                pair = dict(send_sem=send.at[3 * i + k], recv_sem=recv.at[3 * i + k], device_id=(x, y, 1 - c),
                            device_id_type=MESH)
                rows = buf.at[_dev_index(*chip, c)]
                outgoing.append(pltpu.make_async_remote_copy(src_ref=rows, dst_ref=rows, **pair))
                theirs = buf.at[_dev_index(*chip, 1 - c)]
                incoming.append(pltpu.make_async_remote_copy(src_ref=theirs, dst_ref=theirs, **pair))
        return outgoing, incoming

    def start(*refs):
        for cp in copies(*refs)[0]:
            cp.start()

    def finish(*refs):
        outgoing, incoming = copies(*refs)
        for cp in incoming:
            cp.wait_recv()
        for cp in outgoing:
            cp.wait_send()

    return Carry(ins=tuple(gathered), out_shapes=tuple(jax.ShapeDtypeStruct(g.shape, g.dtype) for g in gathered),
                 aliases={i: i for i in range(n)},
                 sems=(pltpu.SemaphoreType.DMA((3 * n,)), pltpu.SemaphoreType.DMA((3 * n,))), start=start, finish=finish)


def gather_whole(blocks):
    send = gather_send(blocks)
    forward = gather_forward(send.out_shapes)
    n = len(send.sems)

    def relay(ins, outs, sems):
        send.finish(ins, outs, sems[:n])
        forward.start((), outs, sems[n:])

    return Carry(ins=send.ins, out_shapes=send.out_shapes, aliases={}, sems=send.sems + forward.sems,
                 start=lambda ins, outs, sems: send.start(ins, outs, sems[:n]), relay=relay,
                 finish=lambda ins, outs, sems: forward.finish((), outs, sems[n:]))


def compose(*carries):
    def split(refs, count):
        out, at = [], 0
        for c in carries:
            n = count(c)
            out.append(refs[at:at + n])
            at += n
        return out

    def each(stage):
        def run(ins, outs, sems):
            parts = zip(carries, split(ins, lambda c: len(c.ins)), split(outs, lambda c: len(c.out_shapes)),
                        split(sems, lambda c: len(c.sems)))
            for c, i, o, s in parts:
                if getattr(c, stage) is not None:
                    getattr(c, stage)(i, o, s)
        return run

    aliases, n_in, n_out = {}, 0, 0
    for c in carries:
        aliases.update({n_in + i: n_out + o for i, o in c.aliases.items()})
        n_in += len(c.ins)
        n_out += len(c.out_shapes)
    return Carry(ins=sum((tuple(c.ins) for c in carries), ()), out_shapes=sum((tuple(c.out_shapes) for c in carries), ()),
                 aliases=aliases, sems=sum((tuple(c.sems) for c in carries), ()), start=each("start"), finish=each("finish"),
                 relay=each("relay") if any(c.relay is not None for c in carries) else None)


def swap_halves(by_core):
    n = len(by_core)

    def copies(in_refs, out_refs, sems):
        send, recv = sems
        x, y, c = _place()
        return [pltpu.make_async_remote_copy(src_ref=a.at[:, 1 - c], dst_ref=r, send_sem=send.at[i], recv_sem=recv.at[i],
                                             device_id=(x, y, 1 - c), device_id_type=MESH)
                for i, (a, r) in enumerate(zip(in_refs, out_refs))]

    def start(*refs):
        for cp in copies(*refs):
            cp.start()

    def finish(*refs):
        for cp in copies(*refs):
            cp.wait()

    shapes = tuple(jax.ShapeDtypeStruct((a.shape[0],) + a.shape[2:], a.dtype) for a in by_core)
    return Carry(ins=tuple(by_core), out_shapes=shapes, aliases={},
                 sems=(pltpu.SemaphoreType.DMA((n,)), pltpu.SemaphoreType.DMA((n,))), start=start, finish=finish)


def exchange_between_chips(by_chip):
    n = len(by_chip)

    def copies(in_refs, out_refs, sems):
        send, recv = sems
        x, y, c = _place()
        out = []
        for i, (s, r) in enumerate(zip(in_refs, out_refs)):
            for k, (tx, ty) in enumerate(_chips_across(x, y)):
                out.append(pltpu.make_async_remote_copy(
                    src_ref=s.at[2 * tx + ty], dst_ref=r.at[k], send_sem=send.at[3 * i + k], recv_sem=recv.at[3 * i + k],
                    device_id=(tx, ty, c), device_id_type=MESH))
        return out

    def start(*refs):
        for cp in copies(*refs):
            cp.start()

    def finish(*refs):
        for cp in copies(*refs):
            cp.wait()

    shapes = tuple(jax.ShapeDtypeStruct((3,) + a.shape[1:], a.dtype) for a in by_chip)
    return Carry(ins=tuple(by_chip), out_shapes=shapes, aliases={},
                 sems=(pltpu.SemaphoreType.DMA((3 * n,)), pltpu.SemaphoreType.DMA((3 * n,))), start=start, finish=finish)


def run_exchange(carry, name):
    n_in = len(carry.ins)
    n_out = len(carry.out_shapes)

    def body(*refs):
        parts = refs[:n_in], refs[n_in:n_in + n_out], refs[n_in + n_out:]
        carry.start(*parts)
        carry.finish(*parts)

    return pl.pallas_call(
        body, name=name, out_shape=tuple(carry.out_shapes), in_specs=[_ANY] * n_in, out_specs=(_ANY,) * n_out,
        scratch_shapes=list(carry.sems), input_output_aliases=dict(carry.aliases),
    )(*carry.ins)


def pair_sum(my_core, by_core, received, name):
    n = len(by_core)

    def body(core_ref, *refs):
        for a_ref, b_ref, o_ref in zip(refs[:n], refs[n:2 * n], refs[2 * n:]):
            o_ref[0] = (a_ref[0, 0].astype(F32) + b_ref[0].astype(F32)).astype(BF16)

    mine = [pl.BlockSpec((1, 1) + a.shape[2:], lambda j, core: (j, core[0], 0, 0)) for a in by_core]
    theirs = [pl.BlockSpec((1,) + r.shape[1:], lambda j, core: (j, 0, 0)) for r in received]
    return pl.pallas_call(
        body, name=name, out_shape=tuple(jax.ShapeDtypeStruct(r.shape, BF16) for r in received),
        grid_spec=pltpu.PrefetchScalarGridSpec(num_scalar_prefetch=1, grid=(by_core[0].shape[0],), in_specs=mine + theirs,
                                               out_specs=tuple(theirs)),
        compiler_params=_params("arbitrary"),
    )(my_core, *by_core, *received)


def _adamw_math(w, g, m, v):
    m = ADAM_B1 * m + (1.0 - ADAM_B1) * g
    v = ADAM_B2 * v + (1.0 - ADAM_B2) * (g * g)
    m_hat = m / (1.0 - ADAM_B1 ** ADAM_STEP)
    v_hat = v / (1.0 - ADAM_B2 ** ADAM_STEP)
    delta = -ADAM_LR * (m_hat / (jnp.sqrt(v_hat) + ADAM_EPS) + ADAM_WD * w)
    return delta, m, v


def adamw(my_chip, ws, gs, ms, vs, name, carry=None):
    n = len(ws)
    flat, widths = [], []
    for g in gs:
        parts = list(g) if isinstance(g, (tuple, list)) else [g]
        flat += parts
        widths.append(len(parts))
    c_ins = list(carry.ins) if carry else []
    c_outs = list(carry.out_shapes) if carry else []

    def body(chip_ref, *refs):
        w_refs, refs = refs[:n], refs[n:]
        g_refs, refs = refs[:len(flat)], refs[len(flat):]
        m_refs, v_refs, refs = refs[:n], refs[n:2 * n], refs[2 * n:]
        carried = (refs[:len(c_ins)], refs[len(c_ins) + 4 * n:len(c_ins) + 4 * n + len(c_outs)],
                   refs[len(c_ins) + 4 * n + len(c_outs):])
        outs = refs[len(c_ins):]
        if carry:
            carry.start(*carried)
        at = 0
        for t in range(n):
            if widths[t] == 1:
                g = g_refs[at][...]
            else:
                g = g_refs[at][0].astype(F32)
                for k in range(3):
                    g = g + g_refs[at + 1][k].astype(F32)
            at += widths[t]
            outs[4 * t][...] = g
            outs[4 * t + 1][...], outs[4 * t + 2][...], outs[4 * t + 3][...] = _adamw_math(
                w_refs[t][...], g, m_refs[t][...], v_refs[t][...])
        if carry:
            if carry.relay is not None:
                carry.relay(*carried)
            carry.finish(*carried)

    def whole(a):
        zeros = (0,) * a.ndim
        return pl.BlockSpec(a.shape, lambda i, chip: zeros, pipeline_mode=pl.Buffered(1))

    g_specs = []
    for g in gs:
        if isinstance(g, (tuple, list)):
            g_specs += [pl.BlockSpec((1,) + g[0].shape[1:], lambda i, chip: (chip[0], 0, 0), pipeline_mode=pl.Buffered(1)),
                        whole(g[1])]
        else:
            g_specs.append(whole(g))
    shapes, out_specs = [], []
    for w in ws:
        shapes += [jax.ShapeDtypeStruct(w.shape, F32)] * 4
        out_specs += [whole(w)] * 4
    res = pl.pallas_call(
        body, name=name, out_shape=tuple(shapes) + tuple(c_outs),
        grid_spec=pltpu.PrefetchScalarGridSpec(
            num_scalar_prefetch=1, grid=(1,),
            in_specs=[whole(w) for w in ws] + g_specs + [whole(a) for a in ms + vs] + [_ANY] * len(c_ins),
            out_specs=tuple(out_specs) + (_ANY,) * len(c_outs), scratch_shapes=list(carry.sems) if carry else []),
        compiler_params=_params("arbitrary"),
    )(my_chip, *ws, *flat, *ms, *vs, *c_ins)
    return [tuple(res[4 * t:4 * t + 4]) for t in range(n)], tuple(res[4 * n:])


def adamw_replicated(w, partials, m, v, name):
    def body(w_ref, p_ref, m_ref, v_ref, g_ref, d_ref, mo_ref, vo_ref):
        g = p_ref[0]
        for k in range(1, N_DEV):
            g = g + p_ref[k]
        g_ref[...] = g
        d_ref[...], mo_ref[...], vo_ref[...] = _adamw_math(w_ref[...], g, m_ref[...], v_ref[...])

    shape = jax.ShapeDtypeStruct(w.shape, F32)
    return pl.pallas_call(body, name=name, out_shape=(shape,) * 4, compiler_params=_params())(w, partials, m, v)


PACK_COLS = 1024
PACK_ROW_ALIGN = 16

REPLICATED = ("ffn1_norm", "mix_norm", "conv_dw_b", "conv_ln_g", "conv_ln_b", "ffn2_norm", "final_norm", "gate_b", "attn_sinks")
WEIGHT_ORDER = ("ffn1_norm", "ffn1_w_gate", "ffn1_w_up", "ffn1_w_down", "mix_norm", "w_in", "conv_dw_w", "conv_dw_b", "conv_ln_g",
                "conv_ln_b", "conv_w_proj", "attn_sinks", "attn_w_o", "gate_b", "w_out", "ffn2_norm", "ffn2_w_gate", "ffn2_w_up",
                "ffn2_w_down", "final_norm")


def _to_rows(flat, lead):
    n = flat.shape[-1]
    rows = -(-n // PACK_COLS)
    flat = jnp.pad(flat, [(0, 0)] * lead + [(0, rows * PACK_COLS - n)])
    return flat.reshape(flat.shape[:lead] + (rows, PACK_COLS))


def _pad_rows(a, axis):
    rows = a.shape[axis]
    pad = -rows % PACK_ROW_ALIGN
    widths = [(0, 0)] * a.ndim
    widths[axis] = (0, pad)
    return jnp.pad(a, widths)


def _from_rows(rows, shape):
    n = 1
    for s in shape:
        n *= s
    return rows.reshape(rows.shape[:-2] + (-1,))[..., :n].reshape(rows.shape[:-2] + tuple(shape))


def _heads_slot_major(rows):
    group = rows.shape[0] // (N_KV_HEADS * HEAD_DIM)
    return rows.reshape(N_KV_HEADS, group, HEAD_DIM, rows.shape[1]).transpose(1, 0, 2, 3).reshape(rows.shape)


def _heads_kv_major(rows):
    group = rows.shape[0] // (N_KV_HEADS * HEAD_DIM)
    return rows.reshape(group, N_KV_HEADS, HEAD_DIM, rows.shape[1]).transpose(1, 0, 2, 3).reshape(rows.shape)


def _by_core(full_rows):
    return full_rows.reshape((N_DEV // 2, 2, full_rows.shape[0] // N_DEV, full_rows.shape[1]))


def kernel(x, positions, ffn1_norm, ffn1_w_gate, ffn1_w_up, ffn1_w_down, mix_norm, w_in, conv_dw_w, conv_dw_b, conv_ln_g, conv_ln_b, conv_w_proj, attn_sinks, attn_w_o, gate_b, w_out, ffn2_norm, ffn2_w_gate, ffn2_w_up, ffn2_w_down, final_norm, loss_target, m_ffn1_norm, m_ffn1_w_gate, m_ffn1_w_up, m_ffn1_w_down, m_mix_norm, m_w_in, m_conv_dw_w, m_conv_dw_b, m_conv_ln_g, m_conv_ln_b, m_conv_w_proj, m_attn_sinks, m_attn_w_o, m_gate_b, m_w_out, m_ffn2_norm, m_ffn2_w_gate, m_ffn2_w_up, m_ffn2_w_down, m_final_norm, v_ffn1_norm, v_ffn1_w_gate, v_ffn1_w_up, v_ffn1_w_down, v_mix_norm, v_w_in, v_conv_dw_w, v_conv_dw_b, v_conv_ln_g, v_conv_ln_b, v_conv_w_proj, v_attn_sinks, v_attn_w_o, v_gate_b, v_w_out, v_ffn2_norm, v_ffn2_w_gate, v_ffn2_w_up, v_ffn2_w_down, v_final_norm):
    given = dict(locals())
    shapes = {n: given[n].shape for n in WEIGHT_ORDER}
    w = {n: given[n].reshape(given[n].shape[-2:]) if given[n].ndim == 3 else given[n].reshape(1, -1) for n in WEIGHT_ORDER}
    m = {n: given["m_" + n].reshape(w[n].shape) for n in WEIGHT_ORDER}
    v = {n: given["v_" + n].reshape(w[n].shape) for n in WEIGHT_ORDER}
    my_x, my_y, my_c = _place()
    my_core = my_c.astype(jnp.int32).reshape(1)
    my_chip = (2 * my_x + my_y).astype(jnp.int32).reshape(1)
    xs, target = x[0], loss_target[0]
    T, D = xs.shape
    KV = N_KV_HEADS * HEAD_DIM
    K = w["conv_dw_w"].shape[0]

    def t16(n):
        return w[n].T.astype(BF16)

    def r16(n):
        return w[n].astype(BF16)

    blocks1 = [t16("ffn1_w_gate"), t16("ffn1_w_up"), r16("ffn1_w_down")]
    dw_bits = _pad_rows(_to_rows(lax.bitcast_convert_type(w["conv_dw_w"], BF16).reshape(-1), 0), 0)
    blocks2 = [t16("w_in"), r16("conv_w_proj"), r16("attn_w_o"), r16("w_out"), dw_bits]
    blocks3 = [t16("ffn2_w_gate"), t16("ffn2_w_up"), r16("ffn2_w_down")]
    cos, sin = rope_tables(positions[0])
    sink_col = jnp.repeat(w["attn_sinks"].reshape(-1), WINDOW).reshape(N_KV_HEADS, (D // KV) * WINDOW, 1)

    def full(gathered):
        return gathered.reshape(-1, gathered.shape[2])

    wgt1, wut1 = (full(g) for g in all_gather(blocks1[:2], "gather_ffn1_up"))
    (h1, a1, b1, s1), got = ffn_up(xs, w["ffn1_norm"], wgt1, wut1, "ffn1_up",
                                   carry=compose(gather_whole(blocks1[2:]), gather_send(blocks2[:1])))
    wd1 = full(got[0])
    (x1,), got = ffn_down(xs, s1, wd1, "ffn1_down", carry=compose(gather_forward(got[1:]), gather_send(blocks2[1:])))
    wint = full(got[0])
    wq = _heads_slot_major(wint[2 * D:3 * D])
    (h2, ga, gb, u0, q, sgc, sgt, kk, vv), got = mix_in_fwd(x1, w["mix_norm"], wint, wq, w["gate_b"], cos, sin, "mix_in_fwd",
                                                            carry=compose(gather_forward(got[1:]), gather_send(blocks3)))
    wcp, wo, wout = (full(g) for g in got[:3])
    wo = _heads_slot_major(wo)
    dw_full = lax.bitcast_convert_type(_from_rows(got[3], w["conv_dw_w"].shape + (2,)), F32)
    dw_full = dw_full.transpose(1, 0, 2).reshape(K, D)
    o, gath3 = attn_fwd(q, kk, vv, sink_col, "attn_fwd", carry=gather_forward(got[4:]))
    x2, u1, co, ao, merged = mix_out_fwd(x1, u0, o, sgc, sgt, dw_full, w["conv_dw_b"], w["conv_ln_g"], w["conv_ln_b"],
                                         wcp, wo, wout, "mix_out_fwd")
    wgt2, wut2, wd2 = (full(g) for g in gath3)
    loss, dx3, d_final, h3, a2, b2, s2 = ffn_fwd_loss(x2, w["ffn2_norm"], wgt2, wut2, wd2, w["final_norm"], target,
                                                      "ffn2_fwd_loss")

    small = {"final_norm": d_final}
    (da2, db2, g2c), _ = ffn_bwd_hidden(dx3, a2, b2, s2, wd2, "ffn2_bwd_hidden")
    (dx2, small["ffn2_norm"]), _ = ffn_bwd_input(dx3, x2, w["ffn2_norm"], da2, db2, wgt2, wut2, "ffn2_bwd_input")
    core2 = [_by_core(g) for g in (wgrad(da2, h3, "ffn2_dwg")[0], wgrad(db2, h3, "ffn2_dwu")[0], g2c)]
    (dgc, dgt, do, du1, dao, sums, g_wout, g_wcp), recv2 = mix_out_bwd(
        dx2, u1, co, ao, sgc, sgt, merged, w["conv_ln_g"], w["conv_ln_b"], wcp, wo, wout, "mix_out_bwd",
        carry=swap_halves(core2))
    chip2 = pair_sum(my_core, core2, recv2, "ffn2_grads_pair_sum")
    small["gate_b"] = jnp.concatenate([sums[0:1], sums[1:2]], axis=1)
    small["conv_ln_g"], small["conv_ln_b"], small["conv_dw_b"] = sums[2:3], sums[3:4], sums[4:5]
    g_wo =_heads_kv_major(wgrad(o, dao, "dw_attn_o")[0])
    (dga, dgb, g_dw), got2 = conv_bwd(du1, u0, ga, gb, dw_full, "conv_bwd", carry=exchange_between_chips(chip2))
    dq, dk, dv, dsink = attn_bwd(q, kk, vv, do, sink_col, cos, sin, "attn_bwd")
    small["attn_sinks"] = dsink[:D // KV, :N_KV_HEADS].T.reshape(1, -1)
    pieces = [dga, dgb, dq, dk, dv, dgc, dgt]
    dx1, small["mix_norm"] = mix_in_bwd(dx2, x1, w["mix_norm"], wint, wq, pieces, "mix_in_bwd")
    group = D // KV
    q_moves = [(2 * D + HEAD_DIM * (group * g + hh), 2 * D + HEAD_DIM * (N_KV_HEADS * hh + g), HEAD_DIM)
               for g in range(N_KV_HEADS) for hh in range(group)]
    g_wint = wgrad_stacked(pieces[:3], h2, "dw_in_a", wint.shape[0], 0, moves=[(0, 0, 2 * D)] + q_moves)
    g_wint = wgrad_stacked(pieces[3:], h2, "dw_in_b", wint.shape[0], 3 * D, into=g_wint)
    corem = [_by_core(a) for a in (g_wint, g_wcp, g_wo, g_wout)]

    (da1, db1, g1c), recvm = ffn_bwd_hidden(dx1, a1, b1, s1, wd1, "ffn1_bwd_hidden", carry=swap_halves(corem))
    chipm = pair_sum(my_core, corem, recvm, "mix_grads_pair_sum")
    g1a, gotm_a = wgrad(da1, h1, "ffn1_dwg", carry=exchange_between_chips(chipm[:1]))
    g1b, gotm_b = wgrad(db1, h1, "ffn1_dwu", carry=exchange_between_chips(chipm[1:]))
    core1 = [_by_core(a) for a in (g1a, g1b, g1c)]
    chip1 = pair_sum(my_core, core1, run_exchange(swap_halves(core1), "ffn1_grads_swap"), "ffn1_grads_pair_sum")
    (grad_x, small["ffn1_norm"]), got1 = ffn_bwd_input(dx1, xs, w["ffn1_norm"], da1, db1, wgt1, wut1, "ffn1_bwd_input",
                                                       carry=exchange_between_chips(chip1))

    gotm = gotm_a + gotm_b
    grad_src = {"ffn1_w_gate": (chip1[0], got1[0]), "ffn1_w_up": (chip1[1], got1[1]), "ffn1_w_down": (chip1[2], got1[2]),
                "ffn2_w_gate": (chip2[0], got2[0]), "ffn2_w_up": (chip2[1], got2[1]), "ffn2_w_down": (chip2[2], got2[2]),
                "w_in": (chipm[0], gotm[0]), "conv_w_proj": (chipm[1], gotm[1]), "attn_w_o": (chipm[2], gotm[2]),
                "w_out": (chipm[3], gotm[3])}
    grads = {}

    def pack_small(d, taps, extra):
        rows = [_to_rows(d[n].reshape(-1), 0) for n in REPLICATED] + [taps, _to_rows(extra.reshape(-1), 0)]
        return _pad_rows(jnp.concatenate(rows, axis=0), 0)

    def like(a, ref):
        return a if a.shape == ref.shape else a.T

    def adamw_group(names, name, carry=None):
        refs = [grad_src[n][0][0] if isinstance(grad_src[n], tuple) else grad_src[n] for n in names]
        res, carried = adamw(my_chip, [like(w[n], r) for n, r in zip(names, refs)], [grad_src[n] for n in names],
                             [like(m[n], r) for n, r in zip(names, refs)], [like(v[n], r) for n, r in zip(names, refs)], name,
                             carry=carry)
        for n, outs in zip(names, res):
            grads[n], delta[n], new_m[n], new_v[n] = (like(a, w[n]) for a in outs)
        return carried

    delta, new_m, new_v = {}, {}, {}
    zero, no_taps = jnp.zeros((1, LANES), F32), jnp.zeros((K, D), F32)
    (shares,) = adamw_group(("ffn2_w_gate", "ffn2_w_up", "ffn2_w_down"), "adamw_ffn2",
                            carry=gather_whole([pack_small(small, g_dw, loss)]))
    g_s, d_s, m_s, v_s = adamw_replicated(pack_small(w, no_taps, zero), shares, pack_small(m, no_taps, zero),
                                          pack_small(v, no_taps, zero), "adamw_replicated")
    off = 0
    for n in REPLICATED:
        r = -(-w[n].shape[1] // PACK_COLS)
        grads[n], delta[n], new_m[n], new_v[n] = (_from_rows(a[off:off + r], w[n].shape) for a in (g_s, d_s, m_s, v_s))
        off += r
    shard_cols = w["conv_dw_w"].shape[1]
    grad_src["conv_dw_w"] = lax.dynamic_slice_in_dim(g_s[off:off + K], _dev_index(my_x, my_y, my_c) * shard_cols, shard_cols,
                                                    axis=1)
    total_loss = g_s[off + K, 0]
    adamw_group(("ffn1_w_gate", "ffn1_w_up", "ffn1_w_down"), "adamw_ffn1")
    adamw_group(("w_in", "conv_dw_w", "conv_w_proj", "attn_w_o", "w_out"), "adamw_mix")

    out = [total_loss, grad_x[None]]
    for d in (grads, delta, new_m, new_v):
        out += [d[n].reshape(shapes[n]) for n in WEIGHT_ORDER]
    return tuple(out)
```

```python
import functools
from typing import Callable, NamedTuple

import jax
import jax.numpy as jnp
from jax import lax
from jax.experimental import pallas as pl
from jax.experimental.pallas import tpu as pltpu

F32, BF16 = jnp.float32, jnp.bfloat16

HEAD_DIM = 64
N_KV_HEADS = 4
WINDOW = 128
ROPE_THETA = 10000.0
EPS = 1e-6
LN_EPS = 1e-5
NEG_INF = -1e30
ADAM_LR, ADAM_B1, ADAM_B2, ADAM_EPS, ADAM_WD, ADAM_STEP = 0.001, 0.9, 0.999, 1e-08, 0.01, 10

N_DEV = 8
LANES = 128
SUBLANES = 8
CONV_HALO = 32
CONV_ROWS, CONV_LANES = 64, 256
ROW_TILE = 512
FFN_CHUNK = 256
FFN_SCALE = 0.5
WGRAD_TILE_ELEMS = 2 ** 22
WGRAD_TILE_ROWS = 2048
WGRAD_VMEM = 40 * 2 ** 20
VMEM_LIMIT = 56 * 2 ** 20
MESH = pl.DeviceIdType.MESH


def _params(*sem):
    return pltpu.CompilerParams(dimension_semantics=sem or None, vmem_limit_bytes=VMEM_LIMIT)


def _resident(shape):
    zeros = (0,) * len(shape)
    return pl.BlockSpec(shape, lambda *_: zeros, pipeline_mode=pl.Buffered(1))


def _rows(tm, n):
    return pl.BlockSpec((tm, n), lambda i: (i, 0))


def _acc_spec(shape):
    zeros = (0,) * len(shape)
    return pl.BlockSpec(shape, lambda *_: zeros)


_ANY = pl.BlockSpec(memory_space=pl.ANY)


class Carry(NamedTuple):
    ins: tuple
    out_shapes: tuple
    aliases: dict
    sems: tuple
    start: Callable
    finish: Callable
    relay: Callable = None


def _call(body, *, name, grid, in_specs, out_specs, out_shape, args, scratch_shapes=(), carry=None):
    n_in, n_out, n_scr = len(in_specs), len(out_specs), len(scratch_shapes)
    params = _params(*(("arbitrary",) * len(grid)))
    if carry is None:
        res = pl.pallas_call(body, name=name, grid=grid, in_specs=list(in_specs), out_specs=tuple(out_specs),
                             out_shape=tuple(out_shape), scratch_shapes=list(scratch_shapes), compiler_params=params)(*args)
        return tuple(res), ()
    c_in, c_out = len(carry.ins), len(carry.out_shapes)

    def wrapped(*refs):
        ins, c_ins = refs[:n_in], refs[n_in:n_in + c_in]
        p = n_in + c_in
        outs, c_outs = refs[p:p + n_out], refs[p + n_out:p + n_out + c_out]
        p += n_out + c_out
        scr, c_sems = refs[p:p + n_scr], refs[p + n_scr:]
        ids = [pl.program_id(d) for d in range(len(grid))]
        first = functools.reduce(jnp.logical_and, [i == 0 for i in ids])
        last = functools.reduce(jnp.logical_and, [i == n - 1 for i, n in zip(ids, grid)])

        @pl.when(first)
        def _():
            carry.start(c_ins, c_outs, c_sems)

        body(*ins, *outs, *scr)

        if carry.relay is not None:
            @pl.when(ids[0] == (3 * grid[0]) // 4)
            def _():
                carry.relay(c_ins, c_outs, c_sems)

        @pl.when(last)
        def _():
            carry.finish(c_ins, c_outs, c_sems)

    res = pl.pallas_call(
        wrapped, name=name, grid=grid, in_specs=list(in_specs) + [_ANY] * c_in, out_specs=tuple(out_specs) + (_ANY,) * c_out,
        out_shape=tuple(out_shape) + tuple(carry.out_shapes), scratch_shapes=list(scratch_shapes) + list(carry.sems),
        input_output_aliases={n_in + i: n_out + o for i, o in carry.aliases.items()}, compiler_params=params,
    )(*args, *carry.ins)
    return tuple(res[:n_out]), tuple(res[n_out:])


def _nt(a, b):
    return lax.dot_general(a, b, (((1,), (1,)), ((), ())), preferred_element_type=F32)


def _tn(a, b):
    return lax.dot_general(a, b, (((0,), (0,)), ((), ())), preferred_element_type=F32)


def _dot(a, b):
    return jnp.dot(a, b, preferred_element_type=F32)


def _sigmoid(x):
    return 1.0 / (1.0 + jnp.exp(-x))


def _rms_fwd(x, g):
    r = lax.rsqrt(jnp.mean(x * x, axis=-1, keepdims=True) + EPS)
    return (x * r) * g


def _rms_bwd(x, g, dy):
    r = lax.rsqrt(jnp.mean(x * x, axis=-1, keepdims=True) + EPS)
    xhat = x * r
    dyg = dy * g
    dx = r * (dyg - xhat * jnp.mean(dyg * xhat, axis=-1, keepdims=True))
    return dx, dy * xhat


def _rot_half(x):
    lane = lax.broadcasted_iota(jnp.int32, (x.shape[0], LANES), 1)
    first = (lane % HEAD_DIM) < (HEAD_DIM // 2)
    out = []
    for s in range(x.shape[1] // LANES):
        xs = x[:, LANES * s:LANES * (s + 1)]
        out.append(jnp.where(first, pltpu.roll(xs, LANES - HEAD_DIM // 2, 1), pltpu.roll(xs, HEAD_DIM // 2, 1)))
    return out[0] if len(out) == 1 else jnp.concatenate(out, axis=1)


def _tile_lanes(t, width):
    return t if width == LANES else jnp.concatenate([t] * (width // LANES), axis=1)


def _rope_fwd(x, cos, sin_signed):
    w = x.shape[1]
    return x * _tile_lanes(cos, w) + _rot_half(x) * _tile_lanes(sin_signed, w)


def _rope_bwd(dy, cos, sin_signed):
    w = dy.shape[1]
    return dy * _tile_lanes(cos, w) + _rot_half(dy * _tile_lanes(sin_signed, w))


def _ffn_rows(x, g_ref, wg_ref, wu_ref, wd_ref, h_ref, a_ref, b_ref, s_ref, acc_ref):
    F = wg_ref.shape[0]
    h = _rms_fwd(x, g_ref[...]).astype(BF16)
    h_ref[...] = h
    for c in range(F // FFN_CHUNK):
        cs = pl.ds(c * FFN_CHUNK, FFN_CHUNK)
        a = _nt(h, wg_ref[cs, :])
        b = _nt(h, wu_ref[cs, :])
        a_ref[:, cs] = a.astype(BF16)
        b_ref[:, cs] = b.astype(BF16)
        s = (a * _sigmoid(a) * b).astype(BF16)
        s_ref[:, cs] = s
        y = _dot(s, wd_ref[cs, :])
        if c == 0:
            acc_ref[...] = y
        else:
            acc_ref[...] += y
    return x + FFN_SCALE * acc_ref[...]


def ffn_up(x, gain, wgt, wut, name, carry=None):
    T, D = x.shape
    F = wgt.shape[0]
    tm = min(ROW_TILE, T)

    def body(x_ref, g_ref, wg_ref, wu_ref, h_ref, a_ref, b_ref, s_ref):
        h = _rms_fwd(x_ref[...], g_ref[...]).astype(BF16)
        h_ref[...] = h
        for c in range(F // FFN_CHUNK):
            cs = pl.ds(c * FFN_CHUNK, FFN_CHUNK)
            a = _nt(h, wg_ref[cs, :])
            b = _nt(h, wu_ref[cs, :])
            a_ref[:, cs] = a.astype(BF16)
            b_ref[:, cs] = b.astype(BF16)
            s_ref[:, cs] = (a * _sigmoid(a) * b).astype(BF16)

    wide = jax.ShapeDtypeStruct((T, F), BF16)
    return _call(
        body, name=name, grid=(T // tm,), out_shape=(jax.ShapeDtypeStruct((T, D), BF16), wide, wide, wide),
        in_specs=[_rows(tm, D), _resident((1, D)), _resident((F, D)), _resident((F, D))],
        out_specs=(_rows(tm, D), _rows(tm, F), _rows(tm, F), _rows(tm, F)), args=(x, gain, wgt, wut), carry=carry)


def ffn_down(x, s, wd, name, carry=None):
    T, D = x.shape
    F = wd.shape[0]
    tm = min(2 * ROW_TILE, T)

    def body(x_ref, s_ref, wd_ref, xo_ref):
        xo_ref[...] = x_ref[...] + FFN_SCALE * _dot(s_ref[...], wd_ref[...])

    return _call(
        body, name=name, grid=(T // tm,), out_shape=(jax.ShapeDtypeStruct((T, D), F32),),
        in_specs=[_rows(tm, D), _rows(tm, F), _resident((F, D))], out_specs=(_rows(tm, D),), args=(x, s, wd), carry=carry)


def ffn_fwd_loss(x, gain, wgt, wut, wd, final_gain, target, name):
    T, D = x.shape
    F = wgt.shape[0]
    tm = min(ROW_TILE, T)

    def body(x_ref, g_ref, wg_ref, wu_ref, wd_ref, gf_ref, t_ref,
             loss_ref, dx_ref, dg_ref, h_ref, a_ref, b_ref, s_ref, acc_ref):
        xo = _ffn_rows(x_ref[...], g_ref, wg_ref, wu_ref, wd_ref, h_ref, a_ref, b_ref, s_ref, acc_ref)
        gf = gf_ref[...]
        err = _rms_fwd(xo, gf) - t_ref[...]
        dx, dgt = _rms_bwd(xo, gf, err * (1.0 / D))
        dx_ref[...] = dx

        @pl.when(pl.program_id(0) == 0)
        def _():
            dg_ref[...] = jnp.zeros_like(dg_ref)
            loss_ref[...] = jnp.zeros_like(loss_ref)
        dg_ref[...] += jnp.sum(dgt, axis=0, keepdims=True)
        per_token = jnp.sum(err * err, axis=-1, keepdims=True) * (0.5 / D)
        loss_ref[...] += jnp.broadcast_to(jnp.sum(per_token, axis=0, keepdims=True), (1, LANES))

    return pl.pallas_call(
        body, name=name, grid=(T // tm,),
        out_shape=(jax.ShapeDtypeStruct((1, LANES), F32), jax.ShapeDtypeStruct((T, D), F32), jax.ShapeDtypeStruct((1, D), F32),
                   jax.ShapeDtypeStruct((T, D), BF16), jax.ShapeDtypeStruct((T, F), BF16), jax.ShapeDtypeStruct((T, F), BF16),
                   jax.ShapeDtypeStruct((T, F), BF16)),
        in_specs=[_rows(tm, D), _resident((1, D)), _resident((F, D)), _resident((F, D)), _resident((F, D)), _resident((1, D)),
                  _rows(tm, D)],
        out_specs=(_acc_spec((1, LANES)), _rows(tm, D), _acc_spec((1, D)), _rows(tm, D), _rows(tm, F), _rows(tm, F),
                   _rows(tm, F)),
        scratch_shapes=[pltpu.VMEM((tm, D), F32)], compiler_params=_params("arbitrary"),
    )(x, gain, wgt, wut, wd, final_gain, target)


def ffn_bwd_hidden(dxo, a, b, wd, name, carry=None):
    T, D = dxo.shape
    F = wd.shape[0]
    tm = min(ROW_TILE, T)
    fc = FFN_CHUNK

    def hidden_body(dxo_ref, a_ref, b_ref, wd_ref, da_ref, db_ref):
        g0 = (FFN_SCALE * dxo_ref[...]).astype(BF16)
        for c in range(F // fc):
            cs = pl.ds(c * fc, fc)
            ds = _nt(g0, wd_ref[cs, :])
            a = a_ref[:, cs].astype(F32)
            bb = b_ref[:, cs].astype(F32)
            sa = _sigmoid(a)
            da_ref[:, cs] = (ds * bb * (sa * (1.0 + a * (1.0 - sa)))).astype(BF16)
            db_ref[:, cs] = (ds * (a * sa)).astype(BF16)

    wide = jax.ShapeDtypeStruct((T, F), BF16)
    return _call(
        hidden_body, name=name, grid=(T // tm,), out_shape=(wide, wide),
        in_specs=[_rows(tm, D), _rows(tm, F), _rows(tm, F), _resident((F, D))],
        out_specs=(_rows(tm, F), _rows(tm, F)), args=(dxo, a, b, wd), carry=carry)


def ffn_bwd_input(dxo, x, gain, da, db, wg, wu, name, carry=None):
    T, D = x.shape
    F = wg.shape[0]
    tm = min(ROW_TILE, T)

    def input_body(dxo_ref, x_ref, g_ref, da_ref, db_ref, wg_ref, wu_ref, dx_ref, dg_ref):
        dh = _dot(da_ref[...], wg_ref[...]) + _dot(db_ref[...], wu_ref[...])
        dx, dgt = _rms_bwd(x_ref[...], g_ref[...], dh)
        dx_ref[...] = dxo_ref[...] + dx

        @pl.when(pl.program_id(0) == 0)
        def _():
            dg_ref[...] = jnp.zeros_like(dg_ref)
        dg_ref[...] += jnp.sum(dgt, axis=0, keepdims=True)

    return _call(
        input_body, name=name, grid=(T // tm,),
        out_shape=(jax.ShapeDtypeStruct((T, D), F32), jax.ShapeDtypeStruct((1, D), F32)),
        in_specs=[_rows(tm, D), _rows(tm, D), _resident((1, D)), _rows(tm, F), _rows(tm, F), _resident((F, D)),
                  _resident((F, D))],
        out_specs=(_rows(tm, D), _acc_spec((1, D))), args=(dxo, x, gain, da, db, wg, wu), carry=carry)


def wgrad(a, b, name, carry=None, b_scale=None):
    T, M = a.shape
    N = b.shape[1]
    fixed = M * N * (4 + 2)
    per_row = 2 * (M * a.dtype.itemsize + N * b.dtype.itemsize)
    tk = ROW_TILE
    while fixed + 2 * tk * per_row <= WGRAD_VMEM and 2 * tk <= WGRAD_TILE_ROWS:
        tk *= 2
    tk = min(tk, T)
    nk = T // tk
    flipped = M > N

    def body(a_ref, b_ref, o_ref, acc_ref):
        k = pl.program_id(0)
        at = a_ref[...].astype(BF16)
        bt = (b_ref[...] if b_scale is None else b_scale * b_ref[...]).astype(BF16)
        part = _tn(bt, at) if flipped else _tn(at, bt)

        @pl.when(k == 0)
        def _():
            acc_ref[...] = part

        @pl.when(k > 0)
        def _():
            acc_ref[...] += part

        @pl.when(k == nk - 1)
        def _():
            total = acc_ref[...]
            o_ref[...] = (total.T if flipped else total).astype(BF16)

    (out,), carried = _call(
        body, name=name, grid=(nk,), out_shape=(jax.ShapeDtypeStruct((M, N), BF16),),
        in_specs=[_rows(tk, M), _rows(tk, N)], out_specs=(_resident((M, N)),),
        scratch_shapes=[pltpu.VMEM((N, M) if flipped else (M, N), F32)], args=(a, b), carry=carry)
    return out, carried


def wgrad_stacked(pieces, b, name, total_rows, at, into=None, moves=None):
    T, N = b.shape
    n = len(pieces)
    widths = [p.shape[1] for p in pieces]
    offs = [sum(widths[:i]) for i in range(n)]
    M = sum(widths)
    tk = ROW_TILE
    while 2 * tk * M <= WGRAD_TILE_ELEMS and 2 * tk <= WGRAD_TILE_ROWS:
        tk *= 2
    tk = min(tk, T)
    nk = T // tk
    moves = moves or [(0, 0, M)]
    n_in = n + 1 + (into is not None)

    def body(*refs):
        a_refs, b_ref = refs[:n], refs[n]
        o_ref, acc_ref, stage, sem = refs[n_in:]
        k = pl.program_id(0)

        @pl.when(k == 0)
        def _():
            acc_ref[...] = jnp.zeros_like(acc_ref)
        bt = b_ref[...].astype(BF16)
        for a_ref, off, width in zip(a_refs, offs, widths):
            acc_ref[pl.ds(off, width), :] += _tn(a_ref[...].astype(BF16), bt)

        @pl.when(k == nk - 1)
        def _():
            for to, start, rows in moves:
                stage[pl.ds(to, rows), :] = acc_ref[pl.ds(start, rows), :].astype(BF16)
            cp = pltpu.make_async_copy(stage, o_ref.at[pl.ds(at, M)], sem)
            cp.start()
            cp.wait()

    return pl.pallas_call(
        body, name=name, grid=(nk,), out_shape=jax.ShapeDtypeStruct((total_rows, N), BF16),
        in_specs=[_rows(tk, width) for width in widths] + [_rows(tk, N)] + [_ANY] * (into is not None), out_specs=_ANY,
        scratch_shapes=[pltpu.VMEM((M, N), F32), pltpu.VMEM((M, N), BF16), pltpu.SemaphoreType.DMA],
        input_output_aliases={n + 1: 0} if into is not None else {}, compiler_params=_params("arbitrary"),
    )(*pieces, b, *([into] if into is not None else []))


def _w_in_rows(D):
    KV = N_KV_HEADS * HEAD_DIM
    return 0, D, 2 * D, 3 * D, 3 * D + KV, 3 * D + 2 * KV, 4 * D + 2 * KV


def mix_in_fwd(x, gain, wint, wq, gate_b, cos, sin_signed, name, carry=None):
    T, D = x.shape
    KV = N_KV_HEADS * HEAD_DIM
    tm = min(ROW_TILE, T)
    o_ga, o_gb, _, o_k, o_v, o_gc, o_gt = _w_in_rows(D)

    def body(x_ref, g_ref, w_ref, wq_ref, gb_ref, cos_ref, sin_ref,
             h_ref, ga_ref, gb_out_ref, u0_ref, q_ref, sgc_ref, sgt_ref, k_ref, v_ref):
        h = _rms_fwd(x_ref[...], g_ref[...]).astype(BF16)
        h_ref[...] = h
        cos, sin = cos_ref[...], sin_ref[...]
        ga = _nt(h, w_ref[pl.ds(o_ga, D), :])
        gb = _nt(h, w_ref[pl.ds(o_gb, D), :])
        ga_ref[...] = ga.astype(BF16)
        gb_out_ref[...] = gb.astype(BF16)
        u0_ref[...] = (ga * _sigmoid(gb)).astype(BF16)
        q = _nt(h, wq_ref[...])
        q_ref[...] = _rope_fwd(q, cos, sin).astype(BF16)
        gc = _nt(h, w_ref[pl.ds(o_gc, D), :]) + gb_ref[:, pl.ds(0, D)]
        sgc_ref[...] = _sigmoid(gc).astype(BF16)
        gt = _nt(h, w_ref[pl.ds(o_gt, D), :]) + gb_ref[:, pl.ds(D, D)]
        sgt_ref[...] = _sigmoid(gt).astype(BF16)
        k = _nt(h, w_ref[pl.ds(o_k, KV), :])
        k_ref[...] = _rope_fwd(k, cos, sin).astype(BF16)
        v_ref[...] = _nt(h, w_ref[pl.ds(o_v, KV), :]).astype(BF16)

    big = jax.ShapeDtypeStruct((T, D), BF16)
    small = jax.ShapeDtypeStruct((T, KV), BF16)
    return _call(
        body, name=name, grid=(T // tm,),
        out_shape=(big, big, big, big, big, big, big, small, small),
        in_specs=[_rows(tm, D), _resident((1, D)), _resident(wint.shape), _resident(wq.shape), _resident((1, 2 * D)),
                  _rows(tm, LANES), _rows(tm, LANES)],
        out_specs=(_rows(tm, D),) * 7 + (_rows(tm, KV),) * 2,
        args=(x, gain, wint, wq, gate_b, cos, sin_signed), carry=carry)


def _from_prev(rows):
    qi = lax.broadcasted_iota(jnp.int32, (rows, WINDOW), 0) % WINDOW
    return lax.broadcasted_iota(jnp.int32, (rows, WINDOW), 1) > qi


def _fold(x, g, from_prev):
    lo = 2 * WINDOW * g
    return jnp.where(from_prev, x[:, lo:lo + WINDOW], x[:, lo + WINDOW:lo + 2 * WINDOW])


def _unfold(folded, from_prev):
    zero = jnp.zeros_like(folded[0])
    parts = []
    for x in folded:
        parts += [jnp.where(from_prev, x, zero), jnp.where(from_prev, zero, x)]
    return jnp.concatenate(parts, axis=1)


def _kv_lane_head(rows, width):
    return lax.broadcasted_iota(jnp.int32, (rows, width), 1) // HEAD_DIM


def _block_diag(win):
    head = _kv_lane_head(*win.shape)
    zero = jnp.zeros_like(win)
    return jnp.concatenate([jnp.where(head == g, win, zero) for g in range(N_KV_HEADS)], axis=0)


def _diag_blocks_sum(bd, keys):
    head = _kv_lane_head(keys, bd.shape[1])
    out = jnp.zeros((keys, bd.shape[1]), F32)
    for g in range(N_KV_HEADS):
        out = jnp.where(head == g, bd[g * keys:(g + 1) * keys], out)
    return out


def _kv_windows(k_ref, kh_ref, v_ref, vh_ref, j):
    rows = pl.ds(j * WINDOW, WINDOW)
    if j == 0:
        kprev, vprev = kh_ref[...], vh_ref[...]
    else:
        prev = pl.ds((j - 1) * WINDOW, WINDOW)
        kprev, vprev = k_ref[prev, :], v_ref[prev, :]
    return jnp.concatenate([kprev, k_ref[rows, :]], axis=0), jnp.concatenate([vprev, v_ref[rows, :]], axis=0)


def _stack_slots(ref, j, group, KV):
    rows = pl.ds(j * WINDOW, WINDOW)
    return jnp.concatenate([ref[rows, pl.ds(KV * hh, KV)] for hh in range(group)], axis=0)


def _attn_exp(qs, kbd, sink_ref, from_prev, no_prev):
    s = _nt(qs, kbd)
    if no_prev is not None:
        qi = lax.broadcasted_iota(jnp.int32, from_prev.shape, 0) % WINDOW
        absent = lax.broadcasted_iota(jnp.int32, from_prev.shape, 1) > jnp.where(no_prev, qi, WINDOW)
    out = []
    for g in range(N_KV_HEADS):
        sg = _fold(s, g, from_prev) * (HEAD_DIM ** -0.5)
        if no_prev is not None:
            sg = jnp.where(absent, NEG_INF, sg)
        sink = sink_ref[g]
        m = jnp.maximum(jnp.max(sg, axis=-1, keepdims=True), sink)
        out.append((jnp.exp(sg - m), jnp.exp(sink - m)))
    return out


def _spread_over_heads(cols, rows, KV):
    head = _kv_lane_head(rows, KV)
    out = jnp.zeros((rows, KV), F32)
    for g, col in enumerate(cols):
        out = jnp.where(head == g, col, out)
    return out


def _halo_rows_spec(tq, width, sub):
    return pl.BlockSpec((sub, width), lambda i: (jnp.maximum(i * (tq // sub) - 1, 0), 0))


def attn_fwd(q, k, v, sink_col, name, carry=None):
    T, D = q.shape
    KV = k.shape[1]
    group = D // KV
    tq = min(ROW_TILE, T)
    nsub = tq // WINDOW
    rows, wide = group * WINDOW, N_KV_HEADS * 2 * WINDOW

    def body(q_ref, k_ref, kh_ref, v_ref, vh_ref, sink_ref, o_ref):
        from_prev = _from_prev(rows)
        head = _kv_lane_head(wide, KV)
        block = lax.broadcasted_iota(jnp.int32, head.shape, 0) // (2 * WINDOW)
        ones_bd = jnp.where(head == block, 1.0, 0.0).astype(BF16)
        for j in range(nsub):
            k_win, v_win = _kv_windows(k_ref, kh_ref, v_ref, vh_ref, j)
            parts = _attn_exp(_stack_slots(q_ref, j, group, KV), _block_diag(k_win), sink_ref, from_prev,
                              pl.program_id(0) == 0 if j == 0 else None)
            p = _unfold([pg.astype(BF16) for pg, _ in parts], from_prev)
            both = _dot(p, jnp.concatenate([_block_diag(v_win), ones_bd], axis=1))
            denom = both[:, KV:] + _spread_over_heads([es for _, es in parts], rows, KV)
            out = (both[:, :KV] / denom).astype(BF16)
            for hh in range(group):
                o_ref[pl.ds(j * WINDOW, WINDOW), pl.ds(KV * hh, KV)] = out[hh * WINDOW:(hh + 1) * WINDOW]

    (o,), carried = _call(
        body, name=name, grid=(T // tq,),
        out_shape=(jax.ShapeDtypeStruct((T, D), BF16),),
        in_specs=[_rows(tq, D), _rows(tq, KV), _halo_rows_spec(tq, KV, WINDOW), _rows(tq, KV),
                  _halo_rows_spec(tq, KV, WINDOW), _resident(sink_col.shape)],
        out_specs=(_rows(tq, D),), args=(q, k, k, v, v, sink_col), carry=carry)
    return o, carried


def attn_bwd(q, k, v, do, sink_col, cos, sin_signed, name):
    T, D = q.shape
    KV = k.shape[1]
    group = D // KV
    tq = min(ROW_TILE, T)
    nsub = tq // WINDOW
    nt = T // tq
    scale = HEAD_DIM ** -0.5
    rows, wide = group * WINDOW, N_KV_HEADS * 2 * WINDOW

    def rev(i):
        return nt - 1 - i

    def body(q_ref, k_ref, kh_ref, v_ref, vh_ref, do_ref, sink_ref, cos_ref, sin_ref,
             dq_ref, dk_ref, dv_ref, dsink_ref, dq_acc, dk_acc, dv_acc, carry_k, carry_v):
        i = pl.program_id(0)

        @pl.when(i == 0)
        def _():
            carry_k[...] = jnp.zeros_like(carry_k)
            carry_v[...] = jnp.zeros_like(carry_v)
            dsink_ref[...] = jnp.zeros_like(dsink_ref)

        dk_acc[...] = jnp.zeros_like(dk_acc)
        dv_acc[...] = jnp.zeros_like(dv_acc)
        from_prev = _from_prev(rows)
        lane = lax.broadcasted_iota(jnp.int32, (1, LANES), 1)
        for j in range(nsub):
            k_win, v_win = _kv_windows(k_ref, kh_ref, v_ref, vh_ref, j)
            kbd, vbd = _block_diag(k_win), _block_diag(v_win)
            qs, dos = _stack_slots(q_ref, j, group, KV), _stack_slots(do_ref, j, group, KV)
            dp = _nt(dos, vbd)
            probs16, ds16 = [], []
            for g, (pg, es) in enumerate(_attn_exp(qs, kbd, sink_ref, from_prev, rev(i) == 0 if j == 0 else None)):
                inv = 1.0 / (jnp.sum(pg, axis=-1, keepdims=True) + es)
                probs = pg * inv
                dpg = _fold(dp, g, from_prev)
                delta = jnp.sum(probs * dpg, axis=-1, keepdims=True)
                probs16.append(probs.astype(BF16))
                ds16.append((probs * (dpg - delta) * scale).astype(BF16))
                dsk = -(es * inv * delta)
                for hh in range(group):
                    tot = jnp.sum(dsk[hh * WINDOW:(hh + 1) * WINDOW], axis=0, keepdims=True)
                    dsink_ref[pl.ds(hh, 1), :] += jnp.where(lane == g, tot, 0.0)
            ds = _unfold(ds16, from_prev)
            dqs = _dot(ds, kbd)
            for hh in range(group):
                dq_acc[pl.ds(j * WINDOW, WINDOW), pl.ds(KV * hh, KV)] = dqs[hh * WINDOW:(hh + 1) * WINDOW]
            keys = pl.ds(j * WINDOW, 2 * WINDOW)
            dk_acc[keys, :] += _diag_blocks_sum(_tn(ds, qs), 2 * WINDOW)
            dv_acc[keys, :] += _diag_blocks_sum(_tn(_unfold(probs16, from_prev), dos), 2 * WINDOW)

        tail = pl.ds(tq, WINDOW)
        dk_acc[tail, :] += carry_k[...]
        dv_acc[tail, :] += carry_v[...]
        carry_k[...] = dk_acc[pl.ds(0, WINDOW), :]
        carry_v[...] = dv_acc[pl.ds(0, WINDOW), :]
        cos, sin = cos_ref[...], sin_ref[...]
        dq_ref[...] = _rope_bwd(dq_acc[...], cos, sin).astype(BF16)
        dk_ref[...] = _rope_bwd(dk_acc[pl.ds(WINDOW, tq), :], cos, sin).astype(BF16)
        dv_ref[...] = dv_acc[pl.ds(WINDOW, tq), :].astype(BF16)

    def rrows(n):
        return pl.BlockSpec((tq, n), lambda i: (rev(i), 0))

    def rhalo(n):
        return pl.BlockSpec((WINDOW, n), lambda i: (jnp.maximum(rev(i) * nsub - 1, 0), 0))

    return pl.pallas_call(
        body, name=name, grid=(nt,),
        out_shape=(jax.ShapeDtypeStruct((T, D), BF16), jax.ShapeDtypeStruct((T, KV), BF16),
                   jax.ShapeDtypeStruct((T, KV), BF16), jax.ShapeDtypeStruct((SUBLANES, LANES), F32)),
        in_specs=[rrows(D), rrows(KV), rhalo(KV), rrows(KV), rhalo(KV), rrows(D), _resident(sink_col.shape),
                  rrows(LANES), rrows(LANES)],
        out_specs=(rrows(D), rrows(KV), rrows(KV), _acc_spec((SUBLANES, LANES))),
        scratch_shapes=[pltpu.VMEM((tq, D), F32), pltpu.VMEM((WINDOW + tq, KV), F32), pltpu.VMEM((WINDOW + tq, KV), F32),
                        pltpu.VMEM((WINDOW, KV), F32), pltpu.VMEM((WINDOW, KV), F32)],
        compiler_params=_params("arbitrary"),
    )(q, k, k, v, v, do, sink_col, cos, sin_signed)


def _ln_stats(u):
    mu = jnp.mean(u, axis=-1, keepdims=True)
    d = u - mu
    rstd = lax.rsqrt(jnp.mean(d * d, axis=-1, keepdims=True) + LN_EPS)
    return d * rstd, rstd


def _lag_taps(b, K):
    return [(a, K - 1 - (SUBLANES * a + b)) for a in range(-(-K // SUBLANES)) if SUBLANES * a + b <= K - 1]


def _conv_chunks(tm, D, chunk):
    def rows(c, carry):
        r0 = pl.multiple_of(c * CONV_ROWS, CONV_ROWS)
        for l0 in range(0, D, CONV_LANES):
            chunk(r0, pl.ds(l0, CONV_LANES))
        return carry
    lax.fori_loop(0, tm // CONV_ROWS, rows, 0)


def _conv_causal(buf, w_ref, bias_ref, out_ref, tm, D, K):
    def chunk(r0, lanes):
        acc = jnp.broadcast_to(bias_ref[:, lanes], (CONV_ROWS, CONV_LANES))
        for b in range(SUBLANES):
            y = None
            for a, k in _lag_taps(b, K):
                start = pl.multiple_of(r0 + CONV_HALO - SUBLANES * (a + 1), SUBLANES)
                t = buf[pl.ds(start, CONV_ROWS + SUBLANES), lanes] * w_ref[pl.ds(k, 1), lanes]
                y = t if y is None else y + t
            acc = acc + y[SUBLANES - b:SUBLANES - b + CONV_ROWS]
        out_ref[pl.ds(r0, CONV_ROWS), lanes] = acc
    _conv_chunks(tm, D, chunk)


def _conv_anticausal(dbuf, w_ref, out_ref, tm, D, K):
    def chunk(r0, lanes):
        acc = jnp.zeros((CONV_ROWS, CONV_LANES), F32)
        for b in range(SUBLANES):
            y = None
            for a, k in _lag_taps(b, K):
                start = pl.multiple_of(r0 + SUBLANES * a, SUBLANES)
                t = dbuf[pl.ds(start, CONV_ROWS + SUBLANES), lanes] * w_ref[pl.ds(k, 1), lanes]
                y = t if y is None else y + t
            acc = acc + y[b:b + CONV_ROWS]
        out_ref[pl.ds(r0, CONV_ROWS), lanes] = acc
    _conv_chunks(tm, D, chunk)


def _conv_tap_grads(dbuf, ubuf, acc_ref, tm, D, K):
    reach = SUBLANES * (-(-K // SUBLANES) - 1)

    def chunk(r0, lanes):
        d = dbuf[pl.ds(r0, CONV_ROWS), lanes]
        around = ubuf[pl.ds(pl.multiple_of(r0 + CONV_HALO - reach - SUBLANES, SUBLANES), CONV_ROWS + reach + SUBLANES), lanes]
        for b in range(SUBLANES):
            shifted = around[SUBLANES - b:SUBLANES - b + CONV_ROWS + reach]
            for a, k in _lag_taps(b, K):
                prod = d * shifted[reach - SUBLANES * a:reach - SUBLANES * a + CONV_ROWS]
                part = prod[0:SUBLANES]
                for i in range(1, CONV_ROWS // SUBLANES):
                    part = part + prod[SUBLANES * i:SUBLANES * (i + 1)]
                acc_ref[k, :, lanes] += part
    _conv_chunks(tm, D, chunk)


def mix_out_fwd(x, u0, o, sgc, sgt, dw_w, dw_b, ln_g, ln_b, wcp, wo, wout, name):
    T, D = x.shape
    tm = min(ROW_TILE, T)
    K = dw_w.shape[0]

    def body(x_ref, u_ref, uh_ref, o_ref, sgc_ref, sgt_ref, w_ref, b_ref, lg_ref, lb_ref, wcp_ref, wo_ref, wout_ref,
             x2_ref, u1_ref, co_ref, ao_ref, mg_ref, buf, conv):
        keep = (pl.program_id(0) > 0).astype(F32)
        buf[pl.ds(0, CONV_HALO), :] = uh_ref[...].astype(F32) * keep
        buf[pl.ds(CONV_HALO, tm), :] = u_ref[...].astype(F32)
        _conv_causal(buf, w_ref, b_ref, conv, tm, D, K)
        acc = conv[...]
        u1_ref[...] = acc.astype(BF16)
        xhat, _ = _ln_stats(acc)
        u2 = xhat * lg_ref[...] + lb_ref[...]
        u3 = (u2 * _sigmoid(u2)).astype(BF16)
        co = _dot(u3, wcp_ref[...])
        ao = _dot(o_ref[...], wo_ref[...])
        co_ref[...] = co.astype(BF16)
        ao_ref[...] = ao.astype(BF16)
        merged = (sgc_ref[...].astype(F32) * co + sgt_ref[...].astype(F32) * ao).astype(BF16)
        mg_ref[...] = merged
        x2_ref[...] = x_ref[...] + _dot(merged, wout_ref[...])

    big = jax.ShapeDtypeStruct((T, D), BF16)
    vec = _resident((1, D))
    return pl.pallas_call(
        body, name=name, grid=(T // tm,),
        out_shape=(jax.ShapeDtypeStruct((T, D), F32), big, big, big, big),
        in_specs=[_rows(tm, D), _rows(tm, D), _halo_rows_spec(tm, D, CONV_HALO), _rows(tm, D), _rows(tm, D), _rows(tm, D),
                  _resident((K, D)), vec, vec, vec, _resident((D, D)), _resident((D, D)), _resident((D, D))],
        out_specs=(_rows(tm, D),) * 5,
        scratch_shapes=[pltpu.VMEM((CONV_HALO + tm, D), F32), pltpu.VMEM((tm, D), F32)],
        compiler_params=_params("arbitrary"),
    )(x, u0, u0, o, sgc, sgt, dw_w, dw_b, ln_g, ln_b, wcp, wo, wout)


def mix_out_bwd(dx2, u1, co, ao, sgc, sgt, merged, ln_g, ln_b, wcp, wo, wout, name, carry=None):
    T, D = dx2.shape
    tm = min(ROW_TILE, T)
    nt = T // tm

    def body(dx_ref, u1_ref, co_ref, ao_ref, sgc_ref, sgt_ref, mg_ref, lg_ref, lb_ref, wcp_ref, wo_ref, wout_ref,
             dgc_ref, dgt_ref, do_ref, du1_ref, dao_ref, sums_ref, gwout_ref, gwcp_ref, acc_out, acc_cp):
        i = pl.program_id(0)

        @pl.when(i == 0)
        def _():
            sums_ref[...] = jnp.zeros_like(sums_ref)
            acc_out[...] = jnp.zeros_like(acc_out)
            acc_cp[...] = jnp.zeros_like(acc_cp)

        dx16 = dx_ref[...].astype(BF16)
        acc_out[...] += _tn(mg_ref[...], dx16)
        dm = _nt(dx16, wout_ref[...])
        sgc, sgt = sgc_ref[...].astype(F32), sgt_ref[...].astype(F32)
        dco = (dm * sgc).astype(BF16)
        dao = (dm * sgt).astype(BF16)
        dgc = dm * co_ref[...].astype(F32) * sgc * (1.0 - sgc)
        dgt = dm * ao_ref[...].astype(F32) * sgt * (1.0 - sgt)
        dao_ref[...] = dao
        dgc_ref[...] = dgc.astype(BF16)
        dgt_ref[...] = dgt.astype(BF16)
        do_ref[...] = _nt(dao, wo_ref[...]).astype(BF16)
        du3 = _nt(dco, wcp_ref[...])
        xhat, rstd = _ln_stats(u1_ref[...].astype(F32))
        g = lg_ref[...]
        u2 = xhat * g + lb_ref[...]
        su = _sigmoid(u2)
        acc_cp[...] += _tn((u2 * su).astype(BF16), dco)
        du2 = du3 * (su * (1.0 + u2 * (1.0 - su)))
        dxh = du2 * g
        du1 = rstd * (dxh - jnp.mean(dxh, axis=-1, keepdims=True) - xhat * jnp.mean(dxh * xhat, axis=-1, keepdims=True))
        du1_ref[...] = du1.astype(BF16)
        for r, val in enumerate((dgc, dgt, du2 * xhat, du2, du1)):
            sums_ref[pl.ds(r, 1), :] += jnp.sum(val, axis=0, keepdims=True)

        @pl.when(i == nt - 1)
        def _():
            gwout_ref[...] = acc_out[...].astype(BF16)
            gwcp_ref[...] = acc_cp[...].astype(BF16)

    big = jax.ShapeDtypeStruct((T, D), BF16)
    square = jax.ShapeDtypeStruct((D, D), BF16)
    vec = _resident((1, D))
    return _call(
        body, name=name, grid=(nt,),
        out_shape=(big,) * 5 + (jax.ShapeDtypeStruct((8, D), F32), square, square),
        in_specs=[_rows(tm, D)] * 7 + [vec, vec, _resident((D, D)), _resident((D, D)), _resident((D, D))],
        out_specs=(_rows(tm, D),) * 5 + (_acc_spec((8, D)), _resident((D, D)), _resident((D, D))),
        scratch_shapes=[pltpu.VMEM((D, D), F32), pltpu.VMEM((D, D), F32)],
        args=(dx2, u1, co, ao, sgc, sgt, merged, ln_g, ln_b, wcp, wo, wout), carry=carry)


def conv_bwd(du1, u0, ga, gb, dw_w, name, carry=None):
    T, D = du1.shape
    tm = min(ROW_TILE, T)
    nt = T // tm
    K = dw_w.shape[0]
    per = tm // CONV_HALO

    def body(d_ref, dn_ref, u_ref, uh_ref, ga_ref, gb_ref, w_ref, dga_ref, dgb_ref, dw_ref, dbuf, ubuf, du0_buf, taps):
        i = pl.program_id(0)
        dbuf[pl.ds(0, tm), :] = d_ref[...].astype(F32)
        dbuf[pl.ds(tm, CONV_HALO), :] = dn_ref[...].astype(F32) * (i < nt - 1).astype(F32)
        ubuf[pl.ds(0, CONV_HALO), :] = uh_ref[...].astype(F32) * (i > 0).astype(F32)
        ubuf[pl.ds(CONV_HALO, tm), :] = u_ref[...].astype(F32)

        @pl.when(i == 0)
        def _():
            taps[...] = jnp.zeros_like(taps)

        _conv_anticausal(dbuf, w_ref, du0_buf, tm, D, K)
        _conv_tap_grads(dbuf, ubuf, taps, tm, D, K)
        du0 = du0_buf[...]
        ga, gb = ga_ref[...].astype(F32), gb_ref[...].astype(F32)
        sg = _sigmoid(gb)
        dga_ref[...] = (du0 * sg).astype(BF16)
        dgb_ref[...] = (du0 * ga * sg * (1.0 - sg)).astype(BF16)

        @pl.when(i == nt - 1)
        def _():
            for k in range(K):
                dw_ref[pl.ds(k, 1), :] = jnp.sum(taps[k], axis=0, keepdims=True)

    nxt = pl.BlockSpec((CONV_HALO, D), lambda i: (jnp.minimum((i + 1) * per, nt * per - 1), 0))
    big = jax.ShapeDtypeStruct((T, D), BF16)
    return _call(
        body, name=name, grid=(nt,),
        out_shape=(big, big, jax.ShapeDtypeStruct((K, D), F32)),
        in_specs=[_rows(tm, D), nxt, _rows(tm, D), _halo_rows_spec(tm, D, CONV_HALO), _rows(tm, D), _rows(tm, D),
                  _resident((K, D))],
        out_specs=(_rows(tm, D), _rows(tm, D), _acc_spec((K, D))),
        scratch_shapes=[pltpu.VMEM((tm + CONV_HALO, D), F32), pltpu.VMEM((CONV_HALO + tm, D), F32), pltpu.VMEM((tm, D), F32),
                        pltpu.VMEM((K, SUBLANES, D), F32)],
        args=(du1, du1, u0, u0, ga, gb, dw_w), carry=carry)


def mix_in_bwd(dx2, x, gain, wint, wq, pieces, name):
    T, D = x.shape
    tm = min(ROW_TILE, T)
    widths = [p.shape[1] for p in pieces]
    offs = _w_in_rows(D)

    def body(dx2_ref, x_ref, g_ref, w_ref, wq_ref, *rest):
        piece_refs, (dx_ref, dg_ref) = rest[:len(pieces)], rest[len(pieces):]
        dh = None
        for n, (p_ref, off, w) in enumerate(zip(piece_refs, offs, widths)):
            t = _dot(p_ref[...], wq_ref[...] if n == 2 else w_ref[pl.ds(off, w), :])
            dh = t if dh is None else dh + t
        dx, dgt = _rms_bwd(x_ref[...], g_ref[...], dh)
        dx_ref[...] = dx2_ref[...] + dx

        @pl.when(pl.program_id(0) == 0)
        def _():
            dg_ref[...] = jnp.zeros_like(dg_ref)
        dg_ref[...] += jnp.sum(dgt, axis=0, keepdims=True)

    return pl.pallas_call(
        body, name=name, grid=(T // tm,),
        out_shape=(jax.ShapeDtypeStruct((T, D), F32), jax.ShapeDtypeStruct((1, D), F32)),
        in_specs=[_rows(tm, D), _rows(tm, D), _resident((1, D)), _resident(wint.shape), _resident(wq.shape)]
        + [_rows(tm, w) for w in widths],
        out_specs=(_rows(tm, D), _acc_spec((1, D))),
        compiler_params=_params("arbitrary"),
    )(dx2, x, gain, wint, wq, *pieces)


def rope_tables(positions):
    half = HEAD_DIM // 2
    inv_freq = ROPE_THETA ** (-jnp.arange(half, dtype=F32) / half)
    ang = positions.astype(F32)[:, None] * inv_freq
    cos, sin = jnp.cos(ang), jnp.sin(ang)
    reps = LANES // HEAD_DIM
    return jnp.tile(jnp.concatenate([cos, cos], axis=-1), (1, reps)), jnp.tile(jnp.concatenate([-sin, sin], axis=-1), (1, reps))


def _place():
    return lax.axis_index("x"), lax.axis_index("y"), lax.axis_index("c")


def all_gather(blocks, name):
    n = len(blocks)
    send = gather_send(blocks)
    forward = gather_forward(send.out_shapes)
    n_sems = len(send.sems)

    def body(*refs):
        ins, outs, sems = refs[:n], refs[n:2 * n], refs[2 * n:]
        send.start(ins, outs, sems[:n_sems])
        send.finish(ins, outs, sems[:n_sems])
        forward.start((), outs, sems[n_sems:])
        forward.finish((), outs, sems[n_sems:])

    return pl.pallas_call(
        body, name=name, out_shape=tuple(send.out_shapes), in_specs=[_ANY] * n, out_specs=(_ANY,) * n,
        scratch_shapes=list(send.sems) + list(forward.sems),
    )(*blocks)


def _chips_across(x, y):
    return [(1 - x, y), (x, 1 - y), (1 - x, 1 - y)]


def _dev_index(x, y, c):
    return 4 * x + 2 * y + c


def gather_send(blocks):
    n = len(blocks)

    def copies(in_refs, out_refs, sems):
        send, recv, local = sems
        x, y, c = _place()
        targets = [(x, y, 1 - c)] + [(*chip, c) for chip in _chips_across(x, y)]
        outgoing, incoming, mine = [], [], []
        for i, (x_ref, out_ref) in enumerate(zip(in_refs, out_refs)):
            for k, t in enumerate(targets):
                pair = dict(send_sem=send.at[4 * i + k], recv_sem=recv.at[4 * i + k], device_id=t, device_id_type=MESH)
                outgoing.append(pltpu.make_async_remote_copy(src_ref=x_ref, dst_ref=out_ref.at[_dev_index(x, y, c)], **pair))
                incoming.append(pltpu.make_async_remote_copy(src_ref=x_ref, dst_ref=out_ref.at[_dev_index(*t)], **pair))
            mine.append(pltpu.make_async_copy(x_ref, out_ref.at[_dev_index(x, y, c)], local.at[i]))
        return outgoing, incoming, mine

    def start(*refs):
        outgoing, _, mine = copies(*refs)
        for cp in mine + outgoing:
            cp.start()

    def finish(*refs):
        outgoing, incoming, mine = copies(*refs)
        for cp in incoming:
            cp.wait_recv()
        for cp in outgoing:
            cp.wait_send()
        for cp in mine:
            cp.wait()

    return Carry(ins=tuple(blocks), out_shapes=tuple(jax.ShapeDtypeStruct((N_DEV,) + b.shape, b.dtype) for b in blocks),
                 aliases={}, sems=(pltpu.SemaphoreType.DMA((4 * n,)), pltpu.SemaphoreType.DMA((4 * n,)),
                                   pltpu.SemaphoreType.DMA((n,))), start=start, finish=finish)


def gather_forward(gathered):
    n = len(gathered)

    def copies(in_refs, out_refs, sems):
        send, recv = sems
        x, y, c = _place()
        outgoing, incoming = [], []
        for i, buf in enumerate(out_refs):
            for k, chip in enumerate(_chips_across(x, y)):
                pair = dict(send_sem=send.at[3 * i + k], recv_sem=recv.at[3 * i + k], device_id=(x, y, 1 - c),
                            device_id_type=MESH)
                rows = buf.at[_dev_index(*chip, c)]
                outgoing.append(pltpu.make_async_remote_copy(src_ref=rows, dst_ref=rows, **pair))
                theirs = buf.at[_dev_index(*chip, 1 - c)]
                incoming.append(pltpu.make_async_remote_copy(src_ref=theirs, dst_ref=theirs, **pair))
        return outgoing, incoming

    def start(*refs):
        for cp in copies(*refs)[0]:
            cp.start()

    def finish(*refs):
        outgoing, incoming = copies(*refs)
        for cp in incoming:
            cp.wait_recv()
        for cp in outgoing:
            cp.wait_send()

    return Carry(ins=tuple(gathered), out_shapes=tuple(jax.ShapeDtypeStruct(g.shape, g.dtype) for g in gathered),
                 aliases={i: i for i in range(n)},
                 sems=(pltpu.SemaphoreType.DMA((3 * n,)), pltpu.SemaphoreType.DMA((3 * n,))), start=start, finish=finish)


def gather_whole(blocks):
    send = gather_send(blocks)
    forward = gather_forward(send.out_shapes)
    n = len(send.sems)

    def relay(ins, outs, sems):
        send.finish(ins, outs, sems[:n])
        forward.start((), outs, sems[n:])

    return Carry(ins=send.ins, out_shapes=send.out_shapes, aliases={}, sems=send.sems + forward.sems,
                 start=lambda ins, outs, sems: send.start(ins, outs, sems[:n]), relay=relay,
                 finish=lambda ins, outs, sems: forward.finish((), outs, sems[n:]))


def compose(*carries):
    def split(refs, count):
        out, at = [], 0
        for c in carries:
            n = count(c)
            out.append(refs[at:at + n])
            at += n
        return out

    def each(stage):
        def run(ins, outs, sems):
            parts = zip(carries, split(ins, lambda c: len(c.ins)), split(outs, lambda c: len(c.out_shapes)),
                        split(sems, lambda c: len(c.sems)))
            for c, i, o, s in parts:
                if getattr(c, stage) is not None:
                    getattr(c, stage)(i, o, s)
        return run

    aliases, n_in, n_out = {}, 0, 0
    for c in carries:
        aliases.update({n_in + i: n_out + o for i, o in c.aliases.items()})
        n_in += len(c.ins)
        n_out += len(c.out_shapes)
    return Carry(ins=sum((tuple(c.ins) for c in carries), ()), out_shapes=sum((tuple(c.out_shapes) for c in carries), ()),
                 aliases=aliases, sems=sum((tuple(c.sems) for c in carries), ()), start=each("start"), finish=each("finish"),
                 relay=each("relay") if any(c.relay is not None for c in carries) else None)


def swap_halves(by_core):
    n = len(by_core)

    def copies(in_refs, out_refs, sems):
        send, recv = sems
        x, y, c = _place()
        return [pltpu.make_async_remote_copy(src_ref=a.at[:, 1 - c], dst_ref=r, send_sem=send.at[i], recv_sem=recv.at[i],
                                             device_id=(x, y, 1 - c), device_id_type=MESH)
                for i, (a, r) in enumerate(zip(in_refs, out_refs))]

    def start(*refs):
        for cp in copies(*refs):
            cp.start()

    def finish(*refs):
        for cp in copies(*refs):
            cp.wait()

    shapes = tuple(jax.ShapeDtypeStruct((a.shape[0],) + a.shape[2:], a.dtype) for a in by_core)
    return Carry(ins=tuple(by_core), out_shapes=shapes, aliases={},
                 sems=(pltpu.SemaphoreType.DMA((n,)), pltpu.SemaphoreType.DMA((n,))), start=start, finish=finish)


def exchange_between_chips(by_chip):
    n = len(by_chip)

    def copies(in_refs, out_refs, sems):
        send, recv = sems
        x, y, c = _place()
        out = []
        for i, (s, r) in enumerate(zip(in_refs, out_refs)):
            for k, (tx, ty) in enumerate(_chips_across(x, y)):
                out.append(pltpu.make_async_remote_copy(
                    src_ref=s.at[2 * tx + ty], dst_ref=r.at[k], send_sem=send.at[3 * i + k], recv_sem=recv.at[3 * i + k],
                    device_id=(tx, ty, c), device_id_type=MESH))
        return out

    def start(*refs):
        for cp in copies(*refs):
            cp.start()

    def finish(*refs):
        for cp in copies(*refs):
            cp.wait()

    shapes = tuple(jax.ShapeDtypeStruct((3,) + a.shape[1:], a.dtype) for a in by_chip)
    return Carry(ins=tuple(by_chip), out_shapes=shapes, aliases={},
                 sems=(pltpu.SemaphoreType.DMA((3 * n,)), pltpu.SemaphoreType.DMA((3 * n,))), start=start, finish=finish)


def run_exchange(carry, name):
    n_in = len(carry.ins)
    n_out = len(carry.out_shapes)

    def body(*refs):
        parts = refs[:n_in], refs[n_in:n_in + n_out], refs[n_in + n_out:]
        carry.start(*parts)
        carry.finish(*parts)

    return pl.pallas_call(
        body, name=name, out_shape=tuple(carry.out_shapes), in_specs=[_ANY] * n_in, out_specs=(_ANY,) * n_out,
        scratch_shapes=list(carry.sems), input_output_aliases=dict(carry.aliases),
    )(*carry.ins)


def pair_sum(my_core, by_core, received, name):
    n = len(by_core)

    def body(core_ref, *refs):
        for a_ref, b_ref, o_ref in zip(refs[:n], refs[n:2 * n], refs[2 * n:]):
            o_ref[0] = (a_ref[0, 0].astype(F32) + b_ref[0].astype(F32)).astype(BF16)

    mine = [pl.BlockSpec((1, 1) + a.shape[2:], lambda j, core: (j, core[0], 0, 0)) for a in by_core]
    theirs = [pl.BlockSpec((1,) + r.shape[1:], lambda j, core: (j, 0, 0)) for r in received]
    return pl.pallas_call(
        body, name=name, out_shape=tuple(jax.ShapeDtypeStruct(r.shape, BF16) for r in received),
        grid_spec=pltpu.PrefetchScalarGridSpec(num_scalar_prefetch=1, grid=(by_core[0].shape[0],), in_specs=mine + theirs,
                                               out_specs=tuple(theirs)),
        compiler_params=_params("arbitrary"),
    )(my_core, *by_core, *received)


def _adamw_math(w, g, m, v):
    m = ADAM_B1 * m + (1.0 - ADAM_B1) * g
    v = ADAM_B2 * v + (1.0 - ADAM_B2) * (g * g)
    m_hat = m / (1.0 - ADAM_B1 ** ADAM_STEP)
    v_hat = v / (1.0 - ADAM_B2 ** ADAM_STEP)
    delta = -ADAM_LR * (m_hat / (jnp.sqrt(v_hat) + ADAM_EPS) + ADAM_WD * w)
    return delta, m, v


def adamw(my_chip, ws, gs, ms, vs, name, carry=None):
    n = len(ws)
    flat, widths = [], []
    for g in gs:
        parts = list(g) if isinstance(g, (tuple, list)) else [g]
        flat += parts
        widths.append(len(parts))
    c_ins = list(carry.ins) if carry else []
    c_outs = list(carry.out_shapes) if carry else []

    def body(chip_ref, *refs):
        w_refs, refs = refs[:n], refs[n:]
        g_refs, refs = refs[:len(flat)], refs[len(flat):]
        m_refs, v_refs, refs = refs[:n], refs[n:2 * n], refs[2 * n:]
        carried = (refs[:len(c_ins)], refs[len(c_ins) + 4 * n:len(c_ins) + 4 * n + len(c_outs)],
                   refs[len(c_ins) + 4 * n + len(c_outs):])
        outs = refs[len(c_ins):]
        if carry:
            carry.start(*carried)
        at = 0
        for t in range(n):
            if widths[t] == 1:
                g = g_refs[at][...]
            else:
                g = g_refs[at][0].astype(F32)
                for k in range(3):
                    g = g + g_refs[at + 1][k].astype(F32)
            at += widths[t]
            outs[4 * t][...] = g
            outs[4 * t + 1][...], outs[4 * t + 2][...], outs[4 * t + 3][...] = _adamw_math(
                w_refs[t][...], g, m_refs[t][...], v_refs[t][...])
        if carry:
            if carry.relay is not None:
                carry.relay(*carried)
            carry.finish(*carried)

    def whole(a):
        zeros = (0,) * a.ndim
        return pl.BlockSpec(a.shape, lambda i, chip: zeros, pipeline_mode=pl.Buffered(1))

    g_specs = []
    for g in gs:
        if isinstance(g, (tuple, list)):
            g_specs += [pl.BlockSpec((1,) + g[0].shape[1:], lambda i, chip: (chip[0], 0, 0), pipeline_mode=pl.Buffered(1)),
                        whole(g[1])]
        else:
            g_specs.append(whole(g))
    shapes, out_specs = [], []
    for w in ws:
        shapes += [jax.ShapeDtypeStruct(w.shape, F32)] * 4
        out_specs += [whole(w)] * 4
    res = pl.pallas_call(
        body, name=name, out_shape=tuple(shapes) + tuple(c_outs),
        grid_spec=pltpu.PrefetchScalarGridSpec(
            num_scalar_prefetch=1, grid=(1,),
            in_specs=[whole(w) for w in ws] + g_specs + [whole(a) for a in ms + vs] + [_ANY] * len(c_ins),
            out_specs=tuple(out_specs) + (_ANY,) * len(c_outs), scratch_shapes=list(carry.sems) if carry else []),
        compiler_params=_params("arbitrary"),
    )(my_chip, *ws, *flat, *ms, *vs, *c_ins)
    return [tuple(res[4 * t:4 * t + 4]) for t in range(n)], tuple(res[4 * n:])


def adamw_replicated(w, partials, m, v, name):
    def body(w_ref, p_ref, m_ref, v_ref, g_ref, d_ref, mo_ref, vo_ref):
        g = p_ref[0]
        for k in range(1, N_DEV):
            g = g + p_ref[k]
        g_ref[...] = g
        d_ref[...], mo_ref[...], vo_ref[...] = _adamw_math(w_ref[...], g, m_ref[...], v_ref[...])

    shape = jax.ShapeDtypeStruct(w.shape, F32)
    return pl.pallas_call(body, name=name, out_shape=(shape,) * 4, compiler_params=_params())(w, partials, m, v)


PACK_COLS = 1024
PACK_ROW_ALIGN = 16

REPLICATED = ("ffn1_norm", "mix_norm", "conv_dw_b", "conv_ln_g", "conv_ln_b", "ffn2_norm", "final_norm", "gate_b", "attn_sinks")
WEIGHT_ORDER = ("ffn1_norm", "ffn1_w_gate", "ffn1_w_up", "ffn1_w_down", "mix_norm", "w_in", "conv_dw_w", "conv_dw_b", "conv_ln_g",
                "conv_ln_b", "conv_w_proj", "attn_sinks", "attn_w_o", "gate_b", "w_out", "ffn2_norm", "ffn2_w_gate", "ffn2_w_up",
                "ffn2_w_down", "final_norm")


def _to_rows(flat, lead):
    n = flat.shape[-1]
    rows = -(-n // PACK_COLS)
    flat = jnp.pad(flat, [(0, 0)] * lead + [(0, rows * PACK_COLS - n)])
    return flat.reshape(flat.shape[:lead] + (rows, PACK_COLS))


def _pad_rows(a, axis):
    rows = a.shape[axis]
    pad = -rows % PACK_ROW_ALIGN
    widths = [(0, 0)] * a.ndim
    widths[axis] = (0, pad)
    return jnp.pad(a, widths)


def _from_rows(rows, shape):
    n = 1
    for s in shape:
        n *= s
    return rows.reshape(rows.shape[:-2] + (-1,))[..., :n].reshape(rows.shape[:-2] + tuple(shape))


def _heads_slot_major(rows):
    group = rows.shape[0] // (N_KV_HEADS * HEAD_DIM)
    return rows.reshape(N_KV_HEADS, group, HEAD_DIM, rows.shape[1]).transpose(1, 0, 2, 3).reshape(rows.shape)


def _heads_kv_major(rows):
    group = rows.shape[0] // (N_KV_HEADS * HEAD_DIM)
    return rows.reshape(group, N_KV_HEADS, HEAD_DIM, rows.shape[1]).transpose(1, 0, 2, 3).reshape(rows.shape)


def _by_core(full_rows):
    return full_rows.reshape((N_DEV // 2, 2, full_rows.shape[0] // N_DEV, full_rows.shape[1]))


def kernel(x, positions, ffn1_norm, ffn1_w_gate, ffn1_w_up, ffn1_w_down, mix_norm, w_in, conv_dw_w, conv_dw_b, conv_ln_g, conv_ln_b, conv_w_proj, attn_sinks, attn_w_o, gate_b, w_out, ffn2_norm, ffn2_w_gate, ffn2_w_up, ffn2_w_down, final_norm, loss_target, m_ffn1_norm, m_ffn1_w_gate, m_ffn1_w_up, m_ffn1_w_down, m_mix_norm, m_w_in, m_conv_dw_w, m_conv_dw_b, m_conv_ln_g, m_conv_ln_b, m_conv_w_proj, m_attn_sinks, m_attn_w_o, m_gate_b, m_w_out, m_ffn2_norm, m_ffn2_w_gate, m_ffn2_w_up, m_ffn2_w_down, m_final_norm, v_ffn1_norm, v_ffn1_w_gate, v_ffn1_w_up, v_ffn1_w_down, v_mix_norm, v_w_in, v_conv_dw_w, v_conv_dw_b, v_conv_ln_g, v_conv_ln_b, v_conv_w_proj, v_attn_sinks, v_attn_w_o, v_gate_b, v_w_out, v_ffn2_norm, v_ffn2_w_gate, v_ffn2_w_up, v_ffn2_w_down, v_final_norm):
    given = dict(locals())
    shapes = {n: given[n].shape for n in WEIGHT_ORDER}
    w = {n: given[n].reshape(given[n].shape[-2:]) if given[n].ndim == 3 else given[n].reshape(1, -1) for n in WEIGHT_ORDER}
    m = {n: given["m_" + n].reshape(w[n].shape) for n in WEIGHT_ORDER}
    v = {n: given["v_" + n].reshape(w[n].shape) for n in WEIGHT_ORDER}
    my_x, my_y, my_c = _place()
    my_core = my_c.astype(jnp.int32).reshape(1)
    my_chip = (2 * my_x + my_y).astype(jnp.int32).reshape(1)
    xs, target = x[0], loss_target[0]
    T, D = xs.shape
    KV = N_KV_HEADS * HEAD_DIM
    K = w["conv_dw_w"].shape[0]

    def t16(n):
        return w[n].T.astype(BF16)

    def r16(n):
        return w[n].astype(BF16)

    blocks1 = [t16("ffn1_w_gate"), t16("ffn1_w_up"), r16("ffn1_w_down")]
    dw_bits = _pad_rows(_to_rows(lax.bitcast_convert_type(w["conv_dw_w"], BF16).reshape(-1), 0), 0)
    blocks2 = [t16("w_in"), r16("conv_w_proj"), r16("attn_w_o"), r16("w_out"), dw_bits]
    blocks3 = [t16("ffn2_w_gate"), t16("ffn2_w_up"), r16("ffn2_w_down")]
    cos, sin = rope_tables(positions[0])
    sink_col = jnp.repeat(w["attn_sinks"].reshape(-1), WINDOW).reshape(N_KV_HEADS, (D // KV) * WINDOW, 1)

    def full(gathered):
        return gathered.reshape(-1, gathered.shape[2])

    wgt1, wut1 = (full(g) for g in all_gather(blocks1[:2], "gather_ffn1_up"))
    (h1, a1, b1, s1), got = ffn_up(xs, w["ffn1_norm"], wgt1, wut1, "ffn1_up",
                                   carry=compose(gather_whole(blocks1[2:]), gather_send(blocks2[:1])))
    wd1 = full(got[0])
    (x1,), got = ffn_down(xs, s1, wd1, "ffn1_down", carry=compose(gather_forward(got[1:]), gather_send(blocks2[1:])))
    wint = full(got[0])
    wq = _heads_slot_major(wint[2 * D:3 * D])
    (h2, ga, gb, u0, q, sgc, sgt, kk, vv), got = mix_in_fwd(x1, w["mix_norm"], wint, wq, w["gate_b"], cos, sin, "mix_in_fwd",
                                                            carry=compose(gather_forward(got[1:]), gather_send(blocks3)))
    wcp, wo, wout = (full(g) for g in got[:3])
    wo = _heads_slot_major(wo)
    dw_full = lax.bitcast_convert_type(_from_rows(got[3], w["conv_dw_w"].shape + (2,)), F32)
    dw_full = dw_full.transpose(1, 0, 2).reshape(K, D)
    o, gath3 = attn_fwd(q, kk, vv, sink_col, "attn_fwd", carry=gather_forward(got[4:]))
    x2, u1, co, ao, merged = mix_out_fwd(x1, u0, o, sgc, sgt, dw_full, w["conv_dw_b"], w["conv_ln_g"], w["conv_ln_b"],
                                         wcp, wo, wout, "mix_out_fwd")
    wgt2, wut2, wd2 = (full(g) for g in gath3)
    loss, dx3, d_final, h3, a2, b2, s2 = ffn_fwd_loss(x2, w["ffn2_norm"], wgt2, wut2, wd2, w["final_norm"], target,
                                                      "ffn2_fwd_loss")

    small = {"final_norm": d_final}
    (da2, db2), _ = ffn_bwd_hidden(dx3, a2, b2, wd2, "ffn2_bwd_hidden")
    (dx2, small["ffn2_norm"]), _ = ffn_bwd_input(dx3, x2, w["ffn2_norm"], da2, db2, wgt2, wut2, "ffn2_bwd_input")
    core2 = [_by_core(g) for g in (wgrad(da2, h3, "ffn2_dwg")[0], wgrad(db2, h3, "ffn2_dwu")[0],
                                   wgrad(s2, dx3, "ffn2_dwd", b_scale=FFN_SCALE)[0])]
    (dgc, dgt, do, du1, dao, sums, g_wout, g_wcp), recv2 = mix_out_bwd(
        dx2, u1, co, ao, sgc, sgt, merged, w["conv_ln_g"], w["conv_ln_b"], wcp, wo, wout, "mix_out_bwd",
        carry=swap_halves(core2))
    chip2 = pair_sum(my_core, core2, recv2, "ffn2_grads_pair_sum")
    small["gate_b"] = jnp.concatenate([sums[0:1], sums[1:2]], axis=1)
    small["conv_ln_g"], small["conv_ln_b"], small["conv_dw_b"] = sums[2:3], sums[3:4], sums[4:5]
    g_wo =_heads_kv_major(wgrad(o, dao, "dw_attn_o")[0])
    (dga, dgb, g_dw), got2 = conv_bwd(du1, u0, ga, gb, dw_full, "conv_bwd", carry=exchange_between_chips(chip2))
    dq, dk, dv, dsink = attn_bwd(q, kk, vv, do, sink_col, cos, sin, "attn_bwd")
    small["attn_sinks"] = dsink[:D // KV, :N_KV_HEADS].T.reshape(1, -1)
    pieces = [dga, dgb, dq, dk, dv, dgc, dgt]
    dx1, small["mix_norm"] = mix_in_bwd(dx2, x1, w["mix_norm"], wint, wq, pieces, "mix_in_bwd")
    group = D // KV
    q_moves = [(2 * D + HEAD_DIM * (group * g + hh), 2 * D + HEAD_DIM * (N_KV_HEADS * hh + g), HEAD_DIM)
               for g in range(N_KV_HEADS) for hh in range(group)]
    g_wint = wgrad_stacked(pieces[:3], h2, "dw_in_a", wint.shape[0], 0, moves=[(0, 0, 2 * D)] + q_moves)
    g_wint = wgrad_stacked(pieces[3:], h2, "dw_in_b", wint.shape[0], 3 * D, into=g_wint)
    corem = [_by_core(a) for a in (g_wint, g_wcp, g_wo, g_wout)]

    (da1, db1), recvm = ffn_bwd_hidden(dx1, a1, b1, wd1, "ffn1_bwd_hidden", carry=swap_halves(corem))
    chipm = pair_sum(my_core, corem, recvm, "mix_grads_pair_sum")
    g1c, gotm_a = wgrad(s1, dx1, "ffn1_dwd", carry=exchange_between_chips(chipm[:1]), b_scale=FFN_SCALE)
    g1a, gotm_b = wgrad(da1, h1, "ffn1_dwg", carry=exchange_between_chips(chipm[1:]))
    g1b = wgrad(db1, h1, "ffn1_dwu")[0]
    core1 = [_by_core(a) for a in (g1a, g1b, g1c)]
    chip1 = pair_sum(my_core, core1, run_exchange(swap_halves(core1), "ffn1_grads_swap"), "ffn1_grads_pair_sum")
    (grad_x, small["ffn1_norm"]), got1 = ffn_bwd_input(dx1, xs, w["ffn1_norm"], da1, db1, wgt1, wut1, "ffn1_bwd_input",
                                                       carry=exchange_between_chips(chip1))

    gotm = gotm_a + gotm_b
    grad_src = {"ffn1_w_gate": (chip1[0], got1[0]), "ffn1_w_up": (chip1[1], got1[1]), "ffn1_w_down": (chip1[2], got1[2]),
                "ffn2_w_gate": (chip2[0], got2[0]), "ffn2_w_up": (chip2[1], got2[1]), "ffn2_w_down": (chip2[2], got2[2]),
                "w_in": (chipm[0], gotm[0]), "conv_w_proj": (chipm[1], gotm[1]), "attn_w_o": (chipm[2], gotm[2]),
                "w_out": (chipm[3], gotm[3])}
    grads = {}

    def pack_small(d, taps, extra):
        rows = [_to_rows(d[n].reshape(-1), 0) for n in REPLICATED] + [taps, _to_rows(extra.reshape(-1), 0)]
        return _pad_rows(jnp.concatenate(rows, axis=0), 0)

    def like(a, ref):
        return a if a.shape == ref.shape else a.T

    def adamw_group(names, name, carry=None):
        refs = [grad_src[n][0][0] if isinstance(grad_src[n], tuple) else grad_src[n] for n in names]
        res, carried = adamw(my_chip, [like(w[n], r) for n, r in zip(names, refs)], [grad_src[n] for n in names],
                             [like(m[n], r) for n, r in zip(names, refs)], [like(v[n], r) for n, r in zip(names, refs)], name,
                             carry=carry)
        for n, outs in zip(names, res):
            grads[n], delta[n], new_m[n], new_v[n] = (like(a, w[n]) for a in outs)
        return carried

    delta, new_m, new_v = {}, {}, {}
    zero, no_taps = jnp.zeros((1, LANES), F32), jnp.zeros((K, D), F32)
    (shares,) = adamw_group(("ffn2_w_gate", "ffn2_w_up", "ffn2_w_down"), "adamw_ffn2",
                            carry=gather_whole([pack_small(small, g_dw, loss)]))
    g_s, d_s, m_s, v_s = adamw_replicated(pack_small(w, no_taps, zero), shares, pack_small(m, no_taps, zero),
                                          pack_small(v, no_taps, zero), "adamw_replicated")
    off = 0
    for n in REPLICATED:
        r = -(-w[n].shape[1] // PACK_COLS)
        grads[n], delta[n], new_m[n], new_v[n] = (_from_rows(a[off:off + r], w[n].shape) for a in (g_s, d_s, m_s, v_s))
        off += r
    shard_cols = w["conv_dw_w"].shape[1]
    grad_src["conv_dw_w"] = lax.dynamic_slice_in_dim(g_s[off:off + K], _dev_index(my_x, my_y, my_c) * shard_cols, shard_cols,
                                                    axis=1)
    total_loss = g_s[off + K, 0]
    adamw_group(("ffn1_w_gate", "ffn1_w_up", "ffn1_w_down"), "adamw_ffn1")
    adamw_group(("w_in", "conv_dw_w", "conv_w_proj", "attn_w_o", "w_out"), "adamw_mix")

    out = [total_loss, grad_x[None]]
    for d in (grads, delta, new_m, new_v):
        out += [d[n].reshape(shapes[n]) for n in WEIGHT_ORDER]
    return tuple(out)
```

```python
import functools
from typing import Callable, NamedTuple

import jax
import jax.numpy as jnp
from jax import lax
from jax.experimental import pallas as pl
from jax.experimental.pallas import tpu as pltpu

F32, BF16 = jnp.float32, jnp.bfloat16

HEAD_DIM = 64
N_KV_HEADS = 4
WINDOW = 128
ROPE_THETA = 10000.0
EPS = 1e-6
LN_EPS = 1e-5
NEG_INF = -1e30
ADAM_LR, ADAM_B1, ADAM_B2, ADAM_EPS, ADAM_WD, ADAM_STEP = 0.001, 0.9, 0.999, 1e-08, 0.01, 10

N_DEV = 8
LANES = 128
SUBLANES = 8
CONV_HALO = 32
CONV_ROWS, CONV_LANES = 128, 128
ROW_TILE = 512
FFN_CHUNK = 256
FFN_SCALE = 0.5
WGRAD_TILE_ELEMS = 2 ** 22
WGRAD_TILE_ROWS = 2048
WGRAD_VMEM = 40 * 2 ** 20
VMEM_LIMIT = 56 * 2 ** 20
MESH = pl.DeviceIdType.MESH


def _params(*sem):
    return pltpu.CompilerParams(dimension_semantics=sem or None, vmem_limit_bytes=VMEM_LIMIT)


def _resident(shape):
    zeros = (0,) * len(shape)
    return pl.BlockSpec(shape, lambda *_: zeros, pipeline_mode=pl.Buffered(1))


def _rows(tm, n):
    return pl.BlockSpec((tm, n), lambda i: (i, 0))


def _acc_spec(shape):
    zeros = (0,) * len(shape)
    return pl.BlockSpec(shape, lambda *_: zeros)


_ANY = pl.BlockSpec(memory_space=pl.ANY)


class Carry(NamedTuple):
    ins: tuple
    out_shapes: tuple
    aliases: dict
    sems: tuple
    start: Callable
    finish: Callable
    relay: Callable = None


def _call(body, *, name, grid, in_specs, out_specs, out_shape, args, scratch_shapes=(), carry=None):
    n_in, n_out, n_scr = len(in_specs), len(out_specs), len(scratch_shapes)
    params = _params(*(("arbitrary",) * len(grid)))
    if carry is None:
        res = pl.pallas_call(body, name=name, grid=grid, in_specs=list(in_specs), out_specs=tuple(out_specs),
                             out_shape=tuple(out_shape), scratch_shapes=list(scratch_shapes), compiler_params=params)(*args)
        return tuple(res), ()
    c_in, c_out = len(carry.ins), len(carry.out_shapes)

    def wrapped(*refs):
        ins, c_ins = refs[:n_in], refs[n_in:n_in + c_in]
        p = n_in + c_in
        outs, c_outs = refs[p:p + n_out], refs[p + n_out:p + n_out + c_out]
        p += n_out + c_out
        scr, c_sems = refs[p:p + n_scr], refs[p + n_scr:]
        ids = [pl.program_id(d) for d in range(len(grid))]
        first = functools.reduce(jnp.logical_and, [i == 0 for i in ids])
        last = functools.reduce(jnp.logical_and, [i == n - 1 for i, n in zip(ids, grid)])

        @pl.when(first)
        def _():
            carry.start(c_ins, c_outs, c_sems)

        body(*ins, *outs, *scr)

        if carry.relay is not None:
            @pl.when(ids[0] == (3 * grid[0]) // 4)
            def _():
                carry.relay(c_ins, c_outs, c_sems)

        @pl.when(last)
        def _():
            carry.finish(c_ins, c_outs, c_sems)

    res = pl.pallas_call(
        wrapped, name=name, grid=grid, in_specs=list(in_specs) + [_ANY] * c_in, out_specs=tuple(out_specs) + (_ANY,) * c_out,
        out_shape=tuple(out_shape) + tuple(carry.out_shapes), scratch_shapes=list(scratch_shapes) + list(carry.sems),
        input_output_aliases={n_in + i: n_out + o for i, o in carry.aliases.items()}, compiler_params=params,
    )(*args, *carry.ins)
    return tuple(res[:n_out]), tuple(res[n_out:])


def _nt(a, b):
    return lax.dot_general(a, b, (((1,), (1,)), ((), ())), preferred_element_type=F32)


def _tn(a, b):
    return lax.dot_general(a, b, (((0,), (0,)), ((), ())), preferred_element_type=F32)


def _dot(a, b):
    return jnp.dot(a, b, preferred_element_type=F32)


def _sigmoid(x):
    return 1.0 / (1.0 + jnp.exp(-x))


def _rms_fwd(x, g):
    r = lax.rsqrt(jnp.mean(x * x, axis=-1, keepdims=True) + EPS)
    return (x * r) * g


def _rms_bwd(x, g, dy):
    r = lax.rsqrt(jnp.mean(x * x, axis=-1, keepdims=True) + EPS)
    xhat = x * r
    dyg = dy * g
    dx = r * (dyg - xhat * jnp.mean(dyg * xhat, axis=-1, keepdims=True))
    return dx, dy * xhat


def _rot_half(x):
    lane = lax.broadcasted_iota(jnp.int32, (x.shape[0], LANES), 1)
    first = (lane % HEAD_DIM) < (HEAD_DIM // 2)
    out = []
    for s in range(x.shape[1] // LANES):
        xs = x[:, LANES * s:LANES * (s + 1)]
        out.append(jnp.where(first, pltpu.roll(xs, LANES - HEAD_DIM // 2, 1), pltpu.roll(xs, HEAD_DIM // 2, 1)))
    return out[0] if len(out) == 1 else jnp.concatenate(out, axis=1)


def _tile_lanes(t, width):
    return t if width == LANES else jnp.concatenate([t] * (width // LANES), axis=1)


def _rope_fwd(x, cos, sin_signed):
    w = x.shape[1]
    return x * _tile_lanes(cos, w) + _rot_half(x) * _tile_lanes(sin_signed, w)


def _rope_bwd(dy, cos, sin_signed):
    w = dy.shape[1]
    return dy * _tile_lanes(cos, w) + _rot_half(dy * _tile_lanes(sin_signed, w))


def _ffn_rows(x, g_ref, wg_ref, wu_ref, wd_ref, h_ref, a_ref, b_ref, s_ref, acc_ref):
    F = wg_ref.shape[0]
    h = _rms_fwd(x, g_ref[...]).astype(BF16)
    h_ref[...] = h
    for c in range(F // FFN_CHUNK):
        cs = pl.ds(c * FFN_CHUNK, FFN_CHUNK)
        a = _nt(h, wg_ref[cs, :])
        b = _nt(h, wu_ref[cs, :])
        a_ref[:, cs] = a.astype(BF16)
        b_ref[:, cs] = b.astype(BF16)
        s = (a * _sigmoid(a) * b).astype(BF16)
        s_ref[:, cs] = s
        y = _dot(s, wd_ref[cs, :])
        if c == 0:
            acc_ref[...] = y
        else:
            acc_ref[...] += y
    return x + FFN_SCALE * acc_ref[...]


def ffn_up(x, gain, wgt, wut, name, carry=None):
    T, D = x.shape
    F = wgt.shape[0]
    tm = min(ROW_TILE, T)

    def body(x_ref, g_ref, wg_ref, wu_ref, h_ref, a_ref, b_ref, s_ref):
        h = _rms_fwd(x_ref[...], g_ref[...]).astype(BF16)
        h_ref[...] = h
        for c in range(F // FFN_CHUNK):
            cs = pl.ds(c * FFN_CHUNK, FFN_CHUNK)
            a = _nt(h, wg_ref[cs, :])
            b = _nt(h, wu_ref[cs, :])
            a_ref[:, cs] = a.astype(BF16)
            b_ref[:, cs] = b.astype(BF16)
            s_ref[:, cs] = (a * _sigmoid(a) * b).astype(BF16)

    wide = jax.ShapeDtypeStruct((T, F), BF16)
    return _call(
        body, name=name, grid=(T // tm,), out_shape=(jax.ShapeDtypeStruct((T, D), BF16), wide, wide, wide),
        in_specs=[_rows(tm, D), _resident((1, D)), _resident((F, D)), _resident((F, D))],
        out_specs=(_rows(tm, D), _rows(tm, F), _rows(tm, F), _rows(tm, F)), args=(x, gain, wgt, wut), carry=carry)


def ffn_down(x, s, wd, name, carry=None):
    T, D = x.shape
    F = wd.shape[0]
    tm = min(2 * ROW_TILE, T)

    def body(x_ref, s_ref, wd_ref, xo_ref):
        xo_ref[...] = x_ref[...] + FFN_SCALE * _dot(s_ref[...], wd_ref[...])

    return _call(
        body, name=name, grid=(T // tm,), out_shape=(jax.ShapeDtypeStruct((T, D), F32),),
        in_specs=[_rows(tm, D), _rows(tm, F), _resident((F, D))], out_specs=(_rows(tm, D),), args=(x, s, wd), carry=carry)


def ffn_fwd_loss(x, gain, wgt, wut, wd, final_gain, target, name):
    T, D = x.shape
    F = wgt.shape[0]
    tm = min(ROW_TILE, T)

    def body(x_ref, g_ref, wg_ref, wu_ref, wd_ref, gf_ref, t_ref,
             loss_ref, dx_ref, dg_ref, h_ref, a_ref, b_ref, s_ref, acc_ref):
        xo = _ffn_rows(x_ref[...], g_ref, wg_ref, wu_ref, wd_ref, h_ref, a_ref, b_ref, s_ref, acc_ref)
        gf = gf_ref[...]
        err = _rms_fwd(xo, gf) - t_ref[...]
        dx, dgt = _rms_bwd(xo, gf, err * (1.0 / D))
        dx_ref[...] = dx

        @pl.when(pl.program_id(0) == 0)
        def _():
            dg_ref[...] = jnp.zeros_like(dg_ref)
            loss_ref[...] = jnp.zeros_like(loss_ref)
        dg_ref[...] += jnp.sum(dgt, axis=0, keepdims=True)
        per_token = jnp.sum(err * err, axis=-1, keepdims=True) * (0.5 / D)
        loss_ref[...] += jnp.broadcast_to(jnp.sum(per_token, axis=0, keepdims=True), (1, LANES))

    return pl.pallas_call(
        body, name=name, grid=(T // tm,),
        out_shape=(jax.ShapeDtypeStruct((1, LANES), F32), jax.ShapeDtypeStruct((T, D), F32), jax.ShapeDtypeStruct((1, D), F32),
                   jax.ShapeDtypeStruct((T, D), BF16), jax.ShapeDtypeStruct((T, F), BF16), jax.ShapeDtypeStruct((T, F), BF16),
                   jax.ShapeDtypeStruct((T, F), BF16)),
        in_specs=[_rows(tm, D), _resident((1, D)), _resident((F, D)), _resident((F, D)), _resident((F, D)), _resident((1, D)),
                  _rows(tm, D)],
        out_specs=(_acc_spec((1, LANES)), _rows(tm, D), _acc_spec((1, D)), _rows(tm, D), _rows(tm, F), _rows(tm, F),
                   _rows(tm, F)),
        scratch_shapes=[pltpu.VMEM((tm, D), F32)], compiler_params=_params("arbitrary"),
    )(x, gain, wgt, wut, wd, final_gain, target)


def ffn_bwd_hidden(dxo, a, b, wd, name, carry=None):
    T, D = dxo.shape
    F = wd.shape[0]
    tm = min(ROW_TILE, T)
    fc = FFN_CHUNK

    def hidden_body(dxo_ref, a_ref, b_ref, wd_ref, da_ref, db_ref):
        g0 = (FFN_SCALE * dxo_ref[...]).astype(BF16)
        for c in range(F // fc):
            cs = pl.ds(c * fc, fc)
            ds = _nt(g0, wd_ref[cs, :])
            a = a_ref[:, cs].astype(F32)
            bb = b_ref[:, cs].astype(F32)
            sa = _sigmoid(a)
            da_ref[:, cs] = (ds * bb * (sa * (1.0 + a * (1.0 - sa)))).astype(BF16)
            db_ref[:, cs] = (ds * (a * sa)).astype(BF16)

    wide = jax.ShapeDtypeStruct((T, F), BF16)
    return _call(
        hidden_body, name=name, grid=(T // tm,), out_shape=(wide, wide),
        in_specs=[_rows(tm, D), _rows(tm, F), _rows(tm, F), _resident((F, D))],
        out_specs=(_rows(tm, F), _rows(tm, F)), args=(dxo, a, b, wd), carry=carry)


def ffn_bwd_input(dxo, x, gain, da, db, wg, wu, name, carry=None):
    T, D = x.shape
    F = wg.shape[0]
    tm = min(ROW_TILE, T)

    def input_body(dxo_ref, x_ref, g_ref, da_ref, db_ref, wg_ref, wu_ref, dx_ref, dg_ref):
        dh = _dot(da_ref[...], wg_ref[...]) + _dot(db_ref[...], wu_ref[...])
        dx, dgt = _rms_bwd(x_ref[...], g_ref[...], dh)
        dx_ref[...] = dxo_ref[...] + dx

        @pl.when(pl.program_id(0) == 0)
        def _():
            dg_ref[...] = jnp.zeros_like(dg_ref)
        dg_ref[...] += jnp.sum(dgt, axis=0, keepdims=True)

    return _call(
        input_body, name=name, grid=(T // tm,),
        out_shape=(jax.ShapeDtypeStruct((T, D), F32), jax.ShapeDtypeStruct((1, D), F32)),
        in_specs=[_rows(tm, D), _rows(tm, D), _resident((1, D)), _rows(tm, F), _rows(tm, F), _resident((F, D)),
                  _resident((F, D))],
        out_specs=(_rows(tm, D), _acc_spec((1, D))), args=(dxo, x, gain, da, db, wg, wu), carry=carry)


def wgrad(a, b, name, carry=None, b_scale=None):
    T, M = a.shape
    N = b.shape[1]
    fixed = M * N * (4 + 2)
    per_row = 2 * (M * a.dtype.itemsize + N * b.dtype.itemsize)
    tk = ROW_TILE
    while fixed + 2 * tk * per_row <= WGRAD_VMEM and 2 * tk <= WGRAD_TILE_ROWS:
        tk *= 2
    tk = min(tk, T)
    nk = T // tk

    def body(a_ref, b_ref, o_ref, acc_ref):
        k = pl.program_id(0)
        bt = b_ref[...] if b_scale is None else b_scale * b_ref[...]
        part = _tn(a_ref[...].astype(BF16), bt.astype(BF16))

        @pl.when(k == 0)
        def _():
            acc_ref[...] = part

        @pl.when(k > 0)
        def _():
            acc_ref[...] += part

        @pl.when(k == nk - 1)
        def _():
            o_ref[...] = acc_ref[...].astype(BF16)

    (out,), carried = _call(
        body, name=name, grid=(nk,), out_shape=(jax.ShapeDtypeStruct((M, N), BF16),),
        in_specs=[_rows(tk, M), _rows(tk, N)], out_specs=(_resident((M, N)),),
        scratch_shapes=[pltpu.VMEM((M, N), F32)], args=(a, b), carry=carry)
    return out, carried


def wgrad_stacked(pieces, b, name, total_rows, at, into=None, moves=None):
    T, N = b.shape
    n = len(pieces)
    widths = [p.shape[1] for p in pieces]
    offs = [sum(widths[:i]) for i in range(n)]
    M = sum(widths)
    tk = ROW_TILE
    while 2 * tk * M <= WGRAD_TILE_ELEMS and 2 * tk <= WGRAD_TILE_ROWS:
        tk *= 2
    tk = min(tk, T)
    nk = T // tk
    moves = moves or [(0, 0, M)]
    n_in = n + 1 + (into is not None)

    def body(*refs):
        a_refs, b_ref = refs[:n], refs[n]
        o_ref, acc_ref, stage, sem = refs[n_in:]
        k = pl.program_id(0)

        @pl.when(k == 0)
        def _():
            acc_ref[...] = jnp.zeros_like(acc_ref)
        bt = b_ref[...].astype(BF16)
        for a_ref, off, width in zip(a_refs, offs, widths):
            acc_ref[pl.ds(off, width), :] += _tn(a_ref[...].astype(BF16), bt)

        @pl.when(k == nk - 1)
        def _():
            for to, start, rows in moves:
                stage[pl.ds(to, rows), :] = acc_ref[pl.ds(start, rows), :].astype(BF16)
            cp = pltpu.make_async_copy(stage, o_ref.at[pl.ds(at, M)], sem)
            cp.start()
            cp.wait()

    return pl.pallas_call(
        body, name=name, grid=(nk,), out_shape=jax.ShapeDtypeStruct((total_rows, N), BF16),
        in_specs=[_rows(tk, width) for width in widths] + [_rows(tk, N)] + [_ANY] * (into is not None), out_specs=_ANY,
        scratch_shapes=[pltpu.VMEM((M, N), F32), pltpu.VMEM((M, N), BF16), pltpu.SemaphoreType.DMA],
        input_output_aliases={n + 1: 0} if into is not None else {}, compiler_params=_params("arbitrary"),
    )(*pieces, b, *([into] if into is not None else []))


def _w_in_rows(D):
    KV = N_KV_HEADS * HEAD_DIM
    return 0, D, 2 * D, 3 * D, 3 * D + KV, 3 * D + 2 * KV, 4 * D + 2 * KV


def mix_in_fwd(x, gain, wint, wq, gate_b, cos, sin_signed, name, carry=None):
    T, D = x.shape
    KV = N_KV_HEADS * HEAD_DIM
    tm = min(ROW_TILE, T)
    o_ga, o_gb, _, o_k, o_v, o_gc, o_gt = _w_in_rows(D)

    def body(x_ref, g_ref, w_ref, wq_ref, gb_ref, cos_ref, sin_ref,
             h_ref, ga_ref, gb_out_ref, u0_ref, q_ref, sgc_ref, sgt_ref, k_ref, v_ref):
        h = _rms_fwd(x_ref[...], g_ref[...]).astype(BF16)
        h_ref[...] = h
        cos, sin = cos_ref[...], sin_ref[...]
        ga = _nt(h, w_ref[pl.ds(o_ga, D), :])
        gb = _nt(h, w_ref[pl.ds(o_gb, D), :])
        ga_ref[...] = ga.astype(BF16)
        gb_out_ref[...] = gb.astype(BF16)
        u0_ref[...] = (ga * _sigmoid(gb)).astype(BF16)
        q = _nt(h, wq_ref[...])
        q_ref[...] = _rope_fwd(q, cos, sin).astype(BF16)
        gc = _nt(h, w_ref[pl.ds(o_gc, D), :]) + gb_ref[:, pl.ds(0, D)]
        sgc_ref[...] = _sigmoid(gc).astype(BF16)
        gt = _nt(h, w_ref[pl.ds(o_gt, D), :]) + gb_ref[:, pl.ds(D, D)]
        sgt_ref[...] = _sigmoid(gt).astype(BF16)
        k = _nt(h, w_ref[pl.ds(o_k, KV), :])
        k_ref[...] = _rope_fwd(k, cos, sin).astype(BF16)
        v_ref[...] = _nt(h, w_ref[pl.ds(o_v, KV), :]).astype(BF16)

    big = jax.ShapeDtypeStruct((T, D), BF16)
    small = jax.ShapeDtypeStruct((T, KV), BF16)
    return _call(
        body, name=name, grid=(T // tm,),
        out_shape=(big, big, big, big, big, big, big, small, small),
        in_specs=[_rows(tm, D), _resident((1, D)), _resident(wint.shape), _resident(wq.shape), _resident((1, 2 * D)),
                  _rows(tm, LANES), _rows(tm, LANES)],
        out_specs=(_rows(tm, D),) * 7 + (_rows(tm, KV),) * 2,
        args=(x, gain, wint, wq, gate_b, cos, sin_signed), carry=carry)


def _from_prev(rows):
    qi = lax.broadcasted_iota(jnp.int32, (rows, WINDOW), 0) % WINDOW
    return lax.broadcasted_iota(jnp.int32, (rows, WINDOW), 1) > qi


def _fold(x, g, from_prev):
    lo = 2 * WINDOW * g
    return jnp.where(from_prev, x[:, lo:lo + WINDOW], x[:, lo + WINDOW:lo + 2 * WINDOW])


def _unfold(folded, from_prev):
    zero = jnp.zeros_like(folded[0])
    parts = []
    for x in folded:
        parts += [jnp.where(from_prev, x, zero), jnp.where(from_prev, zero, x)]
    return jnp.concatenate(parts, axis=1)


def _kv_lane_head(rows, width):
    return lax.broadcasted_iota(jnp.int32, (rows, width), 1) // HEAD_DIM


def _block_diag(win):
    head = _kv_lane_head(*win.shape)
    zero = jnp.zeros_like(win)
    return jnp.concatenate([jnp.where(head == g, win, zero) for g in range(N_KV_HEADS)], axis=0)


def _diag_blocks_sum(bd, keys):
    head = _kv_lane_head(keys, bd.shape[1])
    out = jnp.zeros((keys, bd.shape[1]), F32)
    for g in range(N_KV_HEADS):
        out = jnp.where(head == g, bd[g * keys:(g + 1) * keys], out)
    return out


def _kv_windows(k_ref, kh_ref, v_ref, vh_ref, j):
    rows = pl.ds(j * WINDOW, WINDOW)
    if j == 0:
        kprev, vprev = kh_ref[...], vh_ref[...]
    else:
        prev = pl.ds((j - 1) * WINDOW, WINDOW)
        kprev, vprev = k_ref[prev, :], v_ref[prev, :]
    return jnp.concatenate([kprev, k_ref[rows, :]], axis=0), jnp.concatenate([vprev, v_ref[rows, :]], axis=0)


def _stack_slots(ref, j, group, KV):
    rows = pl.ds(j * WINDOW, WINDOW)
    return jnp.concatenate([ref[rows, pl.ds(KV * hh, KV)] for hh in range(group)], axis=0)


def _attn_exp(qs, kbd, sink_ref, from_prev, no_prev):
    s = _nt(qs, kbd)
    if no_prev is not None:
        qi = lax.broadcasted_iota(jnp.int32, from_prev.shape, 0) % WINDOW
        absent = lax.broadcasted_iota(jnp.int32, from_prev.shape, 1) > jnp.where(no_prev, qi, WINDOW)
    out = []
    for g in range(N_KV_HEADS):
        sg = _fold(s, g, from_prev) * (HEAD_DIM ** -0.5)
        if no_prev is not None:
            sg = jnp.where(absent, NEG_INF, sg)
        sink = sink_ref[g]
        m = jnp.maximum(jnp.max(sg, axis=-1, keepdims=True), sink)
        out.append((jnp.exp(sg - m), jnp.exp(sink - m)))
    return out


def _spread_over_heads(cols, rows, KV):
    head = _kv_lane_head(rows, KV)
    out = jnp.zeros((rows, KV), F32)
    for g, col in enumerate(cols):
        out = jnp.where(head == g, col, out)
    return out


def _halo_rows_spec(tq, width, sub):
    return pl.BlockSpec((sub, width), lambda i: (jnp.maximum(i * (tq // sub) - 1, 0), 0))


def attn_fwd(q, k, v, sink_col, name, carry=None):
    T, D = q.shape
    KV = k.shape[1]
    group = D // KV
    tq = min(ROW_TILE, T)
    nsub = tq // WINDOW
    rows, wide = group * WINDOW, N_KV_HEADS * 2 * WINDOW

    def body(q_ref, k_ref, kh_ref, v_ref, vh_ref, sink_ref, o_ref):
        from_prev = _from_prev(rows)
        head = _kv_lane_head(wide, KV)
        block = lax.broadcasted_iota(jnp.int32, head.shape, 0) // (2 * WINDOW)
        ones_bd = jnp.where(head == block, 1.0, 0.0).astype(BF16)
        for j in range(nsub):
            k_win, v_win = _kv_windows(k_ref, kh_ref, v_ref, vh_ref, j)
            parts = _attn_exp(_stack_slots(q_ref, j, group, KV), _block_diag(k_win), sink_ref, from_prev,
                              pl.program_id(0) == 0 if j == 0 else None)
            p = _unfold([pg.astype(BF16) for pg, _ in parts], from_prev)
            both = _dot(p, jnp.concatenate([_block_diag(v_win), ones_bd], axis=1))
            denom = both[:, KV:] + _spread_over_heads([es for _, es in parts], rows, KV)
            out = (both[:, :KV] / denom).astype(BF16)
            for hh in range(group):
                o_ref[pl.ds(j * WINDOW, WINDOW), pl.ds(KV * hh, KV)] = out[hh * WINDOW:(hh + 1) * WINDOW]

    (o,), carried = _call(
        body, name=name, grid=(T // tq,),
        out_shape=(jax.ShapeDtypeStruct((T, D), BF16),),
        in_specs=[_rows(tq, D), _rows(tq, KV), _halo_rows_spec(tq, KV, WINDOW), _rows(tq, KV),
                  _halo_rows_spec(tq, KV, WINDOW), _resident(sink_col.shape)],
        out_specs=(_rows(tq, D),), args=(q, k, k, v, v, sink_col), carry=carry)
    return o, carried


def attn_bwd(q, k, v, do, sink_col, cos, sin_signed, name):
    T, D = q.shape
    KV = k.shape[1]
    group = D // KV
    tq = min(ROW_TILE, T)
    nsub = tq // WINDOW
    nt = T // tq
    scale = HEAD_DIM ** -0.5
    rows, wide = group * WINDOW, N_KV_HEADS * 2 * WINDOW

    def rev(i):
        return nt - 1 - i

    def body(q_ref, k_ref, kh_ref, v_ref, vh_ref, do_ref, sink_ref, cos_ref, sin_ref,
             dq_ref, dk_ref, dv_ref, dsink_ref, dq_acc, dk_acc, dv_acc, carry_k, carry_v):
        i = pl.program_id(0)

        @pl.when(i == 0)
        def _():
            carry_k[...] = jnp.zeros_like(carry_k)
            carry_v[...] = jnp.zeros_like(carry_v)
            dsink_ref[...] = jnp.zeros_like(dsink_ref)

        dk_acc[...] = jnp.zeros_like(dk_acc)
        dv_acc[...] = jnp.zeros_like(dv_acc)
        from_prev = _from_prev(rows)
        lane = lax.broadcasted_iota(jnp.int32, (1, LANES), 1)
        for j in range(nsub):
            k_win, v_win = _kv_windows(k_ref, kh_ref, v_ref, vh_ref, j)
            kbd, vbd = _block_diag(k_win), _block_diag(v_win)
            qs, dos = _stack_slots(q_ref, j, group, KV), _stack_slots(do_ref, j, group, KV)
            dp = _nt(dos, vbd)
            probs16, ds16 = [], []
            for g, (pg, es) in enumerate(_attn_exp(qs, kbd, sink_ref, from_prev, rev(i) == 0 if j == 0 else None)):
                inv = 1.0 / (jnp.sum(pg, axis=-1, keepdims=True) + es)
                probs = pg * inv
                dpg = _fold(dp, g, from_prev)
                delta = jnp.sum(probs * dpg, axis=-1, keepdims=True)
                probs16.append(probs.astype(BF16))
                ds16.append((probs * (dpg - delta) * scale).astype(BF16))
                dsk = -(es * inv * delta)
                for hh in range(group):
                    tot = jnp.sum(dsk[hh * WINDOW:(hh + 1) * WINDOW], axis=0, keepdims=True)
                    dsink_ref[pl.ds(hh, 1), :] += jnp.where(lane == g, tot, 0.0)
            ds = _unfold(ds16, from_prev)
            dqs = _dot(ds, kbd)
            for hh in range(group):
                dq_acc[pl.ds(j * WINDOW, WINDOW), pl.ds(KV * hh, KV)] = dqs[hh * WINDOW:(hh + 1) * WINDOW]
            keys = pl.ds(j * WINDOW, 2 * WINDOW)
            dk_acc[keys, :] += _diag_blocks_sum(_tn(ds, qs), 2 * WINDOW)
            dv_acc[keys, :] += _diag_blocks_sum(_tn(_unfold(probs16, from_prev), dos), 2 * WINDOW)

        tail = pl.ds(tq, WINDOW)
        dk_acc[tail, :] += carry_k[...]
        dv_acc[tail, :] += carry_v[...]
        carry_k[...] = dk_acc[pl.ds(0, WINDOW), :]
        carry_v[...] = dv_acc[pl.ds(0, WINDOW), :]
        cos, sin = cos_ref[...], sin_ref[...]
        dq_ref[...] = _rope_bwd(dq_acc[...], cos, sin).astype(BF16)
        dk_ref[...] = _rope_bwd(dk_acc[pl.ds(WINDOW, tq), :], cos, sin).astype(BF16)
        dv_ref[...] = dv_acc[pl.ds(WINDOW, tq), :].astype(BF16)

    def rrows(n):
        return pl.BlockSpec((tq, n), lambda i: (rev(i), 0))

    def rhalo(n):
        return pl.BlockSpec((WINDOW, n), lambda i: (jnp.maximum(rev(i) * nsub - 1, 0), 0))

    return pl.pallas_call(
        body, name=name, grid=(nt,),
        out_shape=(jax.ShapeDtypeStruct((T, D), BF16), jax.ShapeDtypeStruct((T, KV), BF16),
                   jax.ShapeDtypeStruct((T, KV), BF16), jax.ShapeDtypeStruct((SUBLANES, LANES), F32)),
        in_specs=[rrows(D), rrows(KV), rhalo(KV), rrows(KV), rhalo(KV), rrows(D), _resident(sink_col.shape),
                  rrows(LANES), rrows(LANES)],
        out_specs=(rrows(D), rrows(KV), rrows(KV), _acc_spec((SUBLANES, LANES))),
        scratch_shapes=[pltpu.VMEM((tq, D), F32), pltpu.VMEM((WINDOW + tq, KV), F32), pltpu.VMEM((WINDOW + tq, KV), F32),
                        pltpu.VMEM((WINDOW, KV), F32), pltpu.VMEM((WINDOW, KV), F32)],
        compiler_params=_params("arbitrary"),
    )(q, k, k, v, v, do, sink_col, cos, sin_signed)


def _ln_stats(u):
    mu = jnp.mean(u, axis=-1, keepdims=True)
    d = u - mu
    rstd = lax.rsqrt(jnp.mean(d * d, axis=-1, keepdims=True) + LN_EPS)
    return d * rstd, rstd


def _lag_taps(b, K):
    return [(a, K - 1 - (SUBLANES * a + b)) for a in range(-(-K // SUBLANES)) if SUBLANES * a + b <= K - 1]


def _conv_chunks(tm, D, chunk):
    def rows(c, carry):
        r0 = pl.multiple_of(c * CONV_ROWS, CONV_ROWS)
        for l0 in range(0, D, CONV_LANES):
            chunk(r0, pl.ds(l0, CONV_LANES))
        return carry
    lax.fori_loop(0, tm // CONV_ROWS, rows, 0)


def _conv_causal(buf, w_ref, bias_ref, out_ref, tm, D, K):
    def chunk(r0, lanes):
        acc = jnp.broadcast_to(bias_ref[:, lanes], (CONV_ROWS, CONV_LANES))
        for b in range(SUBLANES):
            y = None
            for a, k in _lag_taps(b, K):
                start = pl.multiple_of(r0 + CONV_HALO - SUBLANES * (a + 1), SUBLANES)
                t = buf[pl.ds(start, CONV_ROWS + SUBLANES), lanes] * w_ref[pl.ds(k, 1), lanes]
                y = t if y is None else y + t
            acc = acc + y[SUBLANES - b:SUBLANES - b + CONV_ROWS]
        out_ref[pl.ds(r0, CONV_ROWS), lanes] = acc
    _conv_chunks(tm, D, chunk)


def _conv_anticausal(dbuf, w_ref, out_ref, tm, D, K):
    def chunk(r0, lanes):
        acc = jnp.zeros((CONV_ROWS, CONV_LANES), F32)
        for b in range(SUBLANES):
            y = None
            for a, k in _lag_taps(b, K):
                start = pl.multiple_of(r0 + SUBLANES * a, SUBLANES)
                t = dbuf[pl.ds(start, CONV_ROWS + SUBLANES), lanes] * w_ref[pl.ds(k, 1), lanes]
                y = t if y is None else y + t
            acc = acc + y[b:b + CONV_ROWS]
        out_ref[pl.ds(r0, CONV_ROWS), lanes] = acc
    _conv_chunks(tm, D, chunk)


def _conv_tap_grads(dbuf, ubuf, acc_ref, tm, D, K):
    reach = SUBLANES * (-(-K // SUBLANES) - 1)

    def chunk(r0, lanes):
        d = dbuf[pl.ds(r0, CONV_ROWS), lanes]
        around = ubuf[pl.ds(pl.multiple_of(r0 + CONV_HALO - reach - SUBLANES, SUBLANES), CONV_ROWS + reach + SUBLANES), lanes]
        for b in range(SUBLANES):
            shifted = around[SUBLANES - b:SUBLANES - b + CONV_ROWS + reach]
            for a, k in _lag_taps(b, K):
                prod = d * shifted[reach - SUBLANES * a:reach - SUBLANES * a + CONV_ROWS]
                part = prod[0:SUBLANES]
                for i in range(1, CONV_ROWS // SUBLANES):
                    part = part + prod[SUBLANES * i:SUBLANES * (i + 1)]
                acc_ref[k, :, lanes] += part
    _conv_chunks(tm, D, chunk)


def mix_out_fwd(x, u0, o, sgc, sgt, dw_w, dw_b, ln_g, ln_b, wcp, wo, wout, name):
    T, D = x.shape
    tm = min(ROW_TILE, T)
    K = dw_w.shape[0]

    def body(x_ref, u_ref, uh_ref, o_ref, sgc_ref, sgt_ref, w_ref, b_ref, lg_ref, lb_ref, wcp_ref, wo_ref, wout_ref,
             x2_ref, u1_ref, co_ref, ao_ref, mg_ref, buf, conv):
        keep = (pl.program_id(0) > 0).astype(F32)
        buf[pl.ds(0, CONV_HALO), :] = uh_ref[...].astype(F32) * keep
        buf[pl.ds(CONV_HALO, tm), :] = u_ref[...].astype(F32)
        _conv_causal(buf, w_ref, b_ref, conv, tm, D, K)
        acc = conv[...]
        u1_ref[...] = acc.astype(BF16)
        xhat, _ = _ln_stats(acc)
        u2 = xhat * lg_ref[...] + lb_ref[...]
        u3 = (u2 * _sigmoid(u2)).astype(BF16)
        co = _dot(u3, wcp_ref[...])
        ao = _dot(o_ref[...], wo_ref[...])
        co_ref[...] = co.astype(BF16)
        ao_ref[...] = ao.astype(BF16)
        merged = (sgc_ref[...].astype(F32) * co + sgt_ref[...].astype(F32) * ao).astype(BF16)
        mg_ref[...] = merged
        x2_ref[...] = x_ref[...] + _dot(merged, wout_ref[...])

    big = jax.ShapeDtypeStruct((T, D), BF16)
    vec = _resident((1, D))
    return pl.pallas_call(
        body, name=name, grid=(T // tm,),
        out_shape=(jax.ShapeDtypeStruct((T, D), F32), big, big, big, big),
        in_specs=[_rows(tm, D), _rows(tm, D), _halo_rows_spec(tm, D, CONV_HALO), _rows(tm, D), _rows(tm, D), _rows(tm, D),
                  _resident((K, D)), vec, vec, vec, _resident((D, D)), _resident((D, D)), _resident((D, D))],
        out_specs=(_rows(tm, D),) * 5,
        scratch_shapes=[pltpu.VMEM((CONV_HALO + tm, D), F32), pltpu.VMEM((tm, D), F32)],
        compiler_params=_params("arbitrary"),
    )(x, u0, u0, o, sgc, sgt, dw_w, dw_b, ln_g, ln_b, wcp, wo, wout)


def mix_out_bwd(dx2, u1, co, ao, sgc, sgt, merged, ln_g, ln_b, wcp, wo, wout, name, carry=None):
    T, D = dx2.shape
    tm = min(ROW_TILE, T)
    nt = T // tm

    def body(dx_ref, u1_ref, co_ref, ao_ref, sgc_ref, sgt_ref, mg_ref, lg_ref, lb_ref, wcp_ref, wo_ref, wout_ref,
             dgc_ref, dgt_ref, do_ref, du1_ref, dao_ref, sums_ref, gwout_ref, gwcp_ref, acc_out, acc_cp):
        i = pl.program_id(0)

        @pl.when(i == 0)
        def _():
            sums_ref[...] = jnp.zeros_like(sums_ref)
            acc_out[...] = jnp.zeros_like(acc_out)
            acc_cp[...] = jnp.zeros_like(acc_cp)

        dx16 = dx_ref[...].astype(BF16)
        acc_out[...] += _tn(mg_ref[...], dx16)
        dm = _nt(dx16, wout_ref[...])
        sgc, sgt = sgc_ref[...].astype(F32), sgt_ref[...].astype(F32)
        dco = (dm * sgc).astype(BF16)
        dao = (dm * sgt).astype(BF16)
        dgc = dm * co_ref[...].astype(F32) * sgc * (1.0 - sgc)
        dgt = dm * ao_ref[...].astype(F32) * sgt * (1.0 - sgt)
        dao_ref[...] = dao
        dgc_ref[...] = dgc.astype(BF16)
        dgt_ref[...] = dgt.astype(BF16)
        do_ref[...] = _nt(dao, wo_ref[...]).astype(BF16)
        du3 = _nt(dco, wcp_ref[...])
        xhat, rstd = _ln_stats(u1_ref[...].astype(F32))
        g = lg_ref[...]
        u2 = xhat * g + lb_ref[...]
        su = _sigmoid(u2)
        acc_cp[...] += _tn((u2 * su).astype(BF16), dco)
        du2 = du3 * (su * (1.0 + u2 * (1.0 - su)))
        dxh = du2 * g
        du1 = rstd * (dxh - jnp.mean(dxh, axis=-1, keepdims=True) - xhat * jnp.mean(dxh * xhat, axis=-1, keepdims=True))
        du1_ref[...] = du1.astype(BF16)
        for r, val in enumerate((dgc, dgt, du2 * xhat, du2, du1)):
            sums_ref[pl.ds(r, 1), :] += jnp.sum(val, axis=0, keepdims=True)

        @pl.when(i == nt - 1)
        def _():
            gwout_ref[...] = acc_out[...].astype(BF16)
            gwcp_ref[...] = acc_cp[...].astype(BF16)

    big = jax.ShapeDtypeStruct((T, D), BF16)
    square = jax.ShapeDtypeStruct((D, D), BF16)
    vec = _resident((1, D))
    return _call(
        body, name=name, grid=(nt,),
        out_shape=(big,) * 5 + (jax.ShapeDtypeStruct((8, D), F32), square, square),
        in_specs=[_rows(tm, D)] * 7 + [vec, vec, _resident((D, D)), _resident((D, D)), _resident((D, D))],
        out_specs=(_rows(tm, D),) * 5 + (_acc_spec((8, D)), _resident((D, D)), _resident((D, D))),
        scratch_shapes=[pltpu.VMEM((D, D), F32), pltpu.VMEM((D, D), F32)],
        args=(dx2, u1, co, ao, sgc, sgt, merged, ln_g, ln_b, wcp, wo, wout), carry=carry)


def conv_bwd(du1, u0, ga, gb, dw_w, name, carry=None):
    T, D = du1.shape
    tm = min(ROW_TILE, T)
    nt = T // tm
    K = dw_w.shape[0]
    per = tm // CONV_HALO

    def body(d_ref, dn_ref, u_ref, uh_ref, ga_ref, gb_ref, w_ref, dga_ref, dgb_ref, dw_ref, dbuf, ubuf, du0_buf, taps):
        i = pl.program_id(0)
        dbuf[pl.ds(0, tm), :] = d_ref[...].astype(F32)
        dbuf[pl.ds(tm, CONV_HALO), :] = dn_ref[...].astype(F32) * (i < nt - 1).astype(F32)
        ubuf[pl.ds(0, CONV_HALO), :] = uh_ref[...].astype(F32) * (i > 0).astype(F32)
        ubuf[pl.ds(CONV_HALO, tm), :] = u_ref[...].astype(F32)

        @pl.when(i == 0)
        def _():
            taps[...] = jnp.zeros_like(taps)

        _conv_anticausal(dbuf, w_ref, du0_buf, tm, D, K)
        _conv_tap_grads(dbuf, ubuf, taps, tm, D, K)
        du0 = du0_buf[...]
        ga, gb = ga_ref[...].astype(F32), gb_ref[...].astype(F32)
        sg = _sigmoid(gb)
        dga_ref[...] = (du0 * sg).astype(BF16)
        dgb_ref[...] = (du0 * ga * sg * (1.0 - sg)).astype(BF16)

        @pl.when(i == nt - 1)
        def _():
            for k in range(K):
                dw_ref[pl.ds(k, 1), :] = jnp.sum(taps[k], axis=0, keepdims=True)

    nxt = pl.BlockSpec((CONV_HALO, D), lambda i: (jnp.minimum((i + 1) * per, nt * per - 1), 0))
    big = jax.ShapeDtypeStruct((T, D), BF16)
    return _call(
        body, name=name, grid=(nt,),
        out_shape=(big, big, jax.ShapeDtypeStruct((K, D), F32)),
        in_specs=[_rows(tm, D), nxt, _rows(tm, D), _halo_rows_spec(tm, D, CONV_HALO), _rows(tm, D), _rows(tm, D),
                  _resident((K, D))],
        out_specs=(_rows(tm, D), _rows(tm, D), _acc_spec((K, D))),
        scratch_shapes=[pltpu.VMEM((tm + CONV_HALO, D), F32), pltpu.VMEM((CONV_HALO + tm, D), F32), pltpu.VMEM((tm, D), F32),
                        pltpu.VMEM((K, SUBLANES, D), F32)],
        args=(du1, du1, u0, u0, ga, gb, dw_w), carry=carry)


def mix_in_bwd(dx2, x, gain, wint, wq, pieces, name):
    T, D = x.shape
    tm = min(ROW_TILE, T)
    widths = [p.shape[1] for p in pieces]
    offs = _w_in_rows(D)

    def body(dx2_ref, x_ref, g_ref, w_ref, wq_ref, *rest):
        piece_refs, (dx_ref, dg_ref) = rest[:len(pieces)], rest[len(pieces):]
        dh = None
        for n, (p_ref, off, w) in enumerate(zip(piece_refs, offs, widths)):
            t = _dot(p_ref[...], wq_ref[...] if n == 2 else w_ref[pl.ds(off, w), :])
            dh = t if dh is None else dh + t
        dx, dgt = _rms_bwd(x_ref[...], g_ref[...], dh)
        dx_ref[...] = dx2_ref[...] + dx

        @pl.when(pl.program_id(0) == 0)
        def _():
            dg_ref[...] = jnp.zeros_like(dg_ref)
        dg_ref[...] += jnp.sum(dgt, axis=0, keepdims=True)

    return pl.pallas_call(
        body, name=name, grid=(T // tm,),
        out_shape=(jax.ShapeDtypeStruct((T, D), F32), jax.ShapeDtypeStruct((1, D), F32)),
        in_specs=[_rows(tm, D), _rows(tm, D), _resident((1, D)), _resident(wint.shape), _resident(wq.shape)]
        + [_rows(tm, w) for w in widths],
        out_specs=(_rows(tm, D), _acc_spec((1, D))),
        compiler_params=_params("arbitrary"),
    )(dx2, x, gain, wint, wq, *pieces)


def rope_tables(positions):
    half = HEAD_DIM // 2
    inv_freq = ROPE_THETA ** (-jnp.arange(half, dtype=F32) / half)
    ang = positions.astype(F32)[:, None] * inv_freq
    cos, sin = jnp.cos(ang), jnp.sin(ang)
    reps = LANES // HEAD_DIM
    return jnp.tile(jnp.concatenate([cos, cos], axis=-1), (1, reps)), jnp.tile(jnp.concatenate([-sin, sin], axis=-1), (1, reps))


def _place():
    return lax.axis_index("x"), lax.axis_index("y"), lax.axis_index("c")


def all_gather(blocks, name):
    n = len(blocks)
    send = gather_send(blocks)
    forward = gather_forward(send.out_shapes)
    n_sems = len(send.sems)

    def body(*refs):
        ins, outs, sems = refs[:n], refs[n:2 * n], refs[2 * n:]
        send.start(ins, outs, sems[:n_sems])
        send.finish(ins, outs, sems[:n_sems])
        forward.start((), outs, sems[n_sems:])
        forward.finish((), outs, sems[n_sems:])

    return pl.pallas_call(
        body, name=name, out_shape=tuple(send.out_shapes), in_specs=[_ANY] * n, out_specs=(_ANY,) * n,
        scratch_shapes=list(send.sems) + list(forward.sems),
    )(*blocks)


def _chips_across(x, y):
    return [(1 - x, y), (x, 1 - y), (1 - x, 1 - y)]


def _dev_index(x, y, c):
    return 4 * x + 2 * y + c


def gather_send(blocks):
    n = len(blocks)

    def copies(in_refs, out_refs, sems):
        send, recv, local = sems
        x, y, c = _place()
        targets = [(x, y, 1 - c)] + [(*chip, c) for chip in _chips_across(x, y)]
        outgoing, incoming, mine = [], [], []
        for i, (x_ref, out_ref) in enumerate(zip(in_refs, out_refs)):
            for k, t in enumerate(targets):
                pair = dict(send_sem=send.at[4 * i + k], recv_sem=recv.at[4 * i + k], device_id=t, device_id_type=MESH)
                outgoing.append(pltpu.make_async_remote_copy(src_ref=x_ref, dst_ref=out_ref.at[_dev_index(x, y, c)], **pair))
                incoming.append(pltpu.make_async_remote_copy(src_ref=x_ref, dst_ref=out_ref.at[_dev_index(*t)], **pair))
            mine.append(pltpu.make_async_copy(x_ref, out_ref.at[_dev_index(x, y, c)], local.at[i]))
        return outgoing, incoming, mine

    def start(*refs):
        outgoing, _, mine = copies(*refs)
        for cp in mine + outgoing:
            cp.start()

    def finish(*refs):
        outgoing, incoming, mine = copies(*refs)
        for cp in incoming:
            cp.wait_recv()
        for cp in outgoing:
            cp.wait_send()
        for cp in mine:
            cp.wait()

    return Carry(ins=tuple(blocks), out_shapes=tuple(jax.ShapeDtypeStruct((N_DEV,) + b.shape, b.dtype) for b in blocks),
                 aliases={}, sems=(pltpu.SemaphoreType.DMA((4 * n,)), pltpu.SemaphoreType.DMA((4 * n,)),
                                   pltpu.SemaphoreType.DMA((n,))), start=start, finish=finish)


def gather_forward(gathered):
    n = len(gathered)

    def copies(in_refs, out_refs, sems):
        send, recv = sems
        x, y, c = _place()
        outgoing, incoming = [], []
        for i, buf in enumerate(out_refs):
            for k, chip in enumerate(_chips_across(x, y)):
                pair = dict(send_sem=send.at[3 * i + k], recv_sem=recv.at[3 * i + k], device_id=(x, y, 1 - c),
                            device_id_type=MESH)
                rows = buf.at[_dev_index(*chip, c)]
                outgoing.append(pltpu.make_async_remote_copy(src_ref=rows, dst_ref=rows, **pair))
                theirs = buf.at[_dev_index(*chip, 1 - c)]
                incoming.append(pltpu.make_async_remote_copy(src_ref=theirs, dst_ref=theirs, **pair))
        return outgoing, incoming

    def start(*refs):
        for cp in copies(*refs)[0]:
            cp.start()

    def finish(*refs):
        outgoing, incoming = copies(*refs)
        for cp in incoming:
            cp.wait_recv()
        for cp in outgoing:
            cp.wait_send()

    return Carry(ins=tuple(gathered), out_shapes=tuple(jax.ShapeDtypeStruct(g.shape, g.dtype) for g in gathered),
                 aliases={i: i for i in range(n)},
                 sems=(pltpu.SemaphoreType.DMA((3 * n,)), pltpu.SemaphoreType.DMA((3 * n,))), start=start, finish=finish)


def gather_whole(blocks):
    send = gather_send(blocks)
    forward = gather_forward(send.out_shapes)
    n = len(send.sems)

    def relay(ins, outs, sems):
        send.finish(ins, outs, sems[:n])
        forward.start((), outs, sems[n:])

    return Carry(ins=send.ins, out_shapes=send.out_shapes, aliases={}, sems=send.sems + forward.sems,
                 start=lambda ins, outs, sems: send.start(ins, outs, sems[:n]), relay=relay,
                 finish=lambda ins, outs, sems: forward.finish((), outs, sems[n:]))


def compose(*carries):
    def split(refs, count):
        out, at = [], 0
        for c in carries:
            n = count(c)
            out.append(refs[at:at + n])
            at += n
        return out

    def each(stage):
        def run(ins, outs, sems):
            parts = zip(carries, split(ins, lambda c: len(c.ins)), split(outs, lambda c: len(c.out_shapes)),
                        split(sems, lambda c: len(c.sems)))
            for c, i, o, s in parts:
                if getattr(c, stage) is not None:
                    getattr(c, stage)(i, o, s)
        return run

    aliases, n_in, n_out = {}, 0, 0
    for c in carries:
        aliases.update({n_in + i: n_out + o for i, o in c.aliases.items()})
        n_in += len(c.ins)
        n_out += len(c.out_shapes)
    return Carry(ins=sum((tuple(c.ins) for c in carries), ()), out_shapes=sum((tuple(c.out_shapes) for c in carries), ()),
                 aliases=aliases, sems=sum((tuple(c.sems) for c in carries), ()), start=each("start"), finish=each("finish"),
                 relay=each("relay") if any(c.relay is not None for c in carries) else None)


def swap_halves(by_core):
    n = len(by_core)

    def copies(in_refs, out_refs, sems):
        send, recv = sems
        x, y, c = _place()
        return [pltpu.make_async_remote_copy(src_ref=a.at[:, 1 - c], dst_ref=r, send_sem=send.at[i], recv_sem=recv.at[i],
                                             device_id=(x, y, 1 - c), device_id_type=MESH)
                for i, (a, r) in enumerate(zip(in_refs, out_refs))]

    def start(*refs):
        for cp in copies(*refs):
            cp.start()

    def finish(*refs):
        for cp in copies(*refs):
            cp.wait()

    shapes = tuple(jax.ShapeDtypeStruct((a.shape[0],) + a.shape[2:], a.dtype) for a in by_core)
    return Carry(ins=tuple(by_core), out_shapes=shapes, aliases={},
                 sems=(pltpu.SemaphoreType.DMA((n,)), pltpu.SemaphoreType.DMA((n,))), start=start, finish=finish)


def exchange_between_chips(by_chip):
    n = len(by_chip)

    def copies(in_refs, out_refs, sems):
        send, recv = sems
        x, y, c = _place()
        out = []
        for i, (s, r) in enumerate(zip(in_refs, out_refs)):
            for k, (tx, ty) in enumerate(_chips_across(x, y)):
                out.append(pltpu.make_async_remote_copy(
                    src_ref=s.at[2 * tx + ty], dst_ref=r.at[k], send_sem=send.at[3 * i + k], recv_sem=recv.at[3 * i + k],
                    device_id=(tx, ty, c), device_id_type=MESH))
        return out

    def start(*refs):
        for cp in copies(*refs):
            cp.start()

    def finish(*refs):
        for cp in copies(*refs):
            cp.wait()

    shapes = tuple(jax.ShapeDtypeStruct((3,) + a.shape[1:], a.dtype) for a in by_chip)
    return Carry(ins=tuple(by_chip), out_shapes=shapes, aliases={},
                 sems=(pltpu.SemaphoreType.DMA((3 * n,)), pltpu.SemaphoreType.DMA((3 * n,))), start=start, finish=finish)


def run_exchange(carry, name):
    n_in = len(carry.ins)
    n_out = len(carry.out_shapes)

    def body(*refs):
        parts = refs[:n_in], refs[n_in:n_in + n_out], refs[n_in + n_out:]
        carry.start(*parts)
        carry.finish(*parts)

    return pl.pallas_call(
        body, name=name, out_shape=tuple(carry.out_shapes), in_specs=[_ANY] * n_in, out_specs=(_ANY,) * n_out,
        scratch_shapes=list(carry.sems), input_output_aliases=dict(carry.aliases),
    )(*carry.ins)


def pair_sum(my_core, by_core, received, name):
    n = len(by_core)

    def body(core_ref, *refs):
        for a_ref, b_ref, o_ref in zip(refs[:n], refs[n:2 * n], refs[2 * n:]):
            o_ref[0] = (a_ref[0, 0].astype(F32) + b_ref[0].astype(F32)).astype(BF16)

    mine = [pl.BlockSpec((1, 1) + a.shape[2:], lambda j, core: (j, core[0], 0, 0)) for a in by_core]
    theirs = [pl.BlockSpec((1,) + r.shape[1:], lambda j, core: (j, 0, 0)) for r in received]
    return pl.pallas_call(
        body, name=name, out_shape=tuple(jax.ShapeDtypeStruct(r.shape, BF16) for r in received),
        grid_spec=pltpu.PrefetchScalarGridSpec(num_scalar_prefetch=1, grid=(by_core[0].shape[0],), in_specs=mine + theirs,
                                               out_specs=tuple(theirs)),
        compiler_params=_params("arbitrary"),
    )(my_core, *by_core, *received)


def _adamw_math(w, g, m, v):
    m = ADAM_B1 * m + (1.0 - ADAM_B1) * g
    v = ADAM_B2 * v + (1.0 - ADAM_B2) * (g * g)
    m_hat = m / (1.0 - ADAM_B1 ** ADAM_STEP)
    v_hat = v / (1.0 - ADAM_B2 ** ADAM_STEP)
    delta = -ADAM_LR * (m_hat / (jnp.sqrt(v_hat) + ADAM_EPS) + ADAM_WD * w)
    return delta, m, v


def adamw(my_chip, ws, gs, ms, vs, name, carry=None):
    n = len(ws)
    flat, widths = [], []
    for g in gs:
        parts = list(g) if isinstance(g, (tuple, list)) else [g]
        flat += parts
        widths.append(len(parts))
    c_ins = list(carry.ins) if carry else []
    c_outs = list(carry.out_shapes) if carry else []

    def body(chip_ref, *refs):
        w_refs, refs = refs[:n], refs[n:]
        g_refs, refs = refs[:len(flat)], refs[len(flat):]
        m_refs, v_refs, refs = refs[:n], refs[n:2 * n], refs[2 * n:]
        carried = (refs[:len(c_ins)], refs[len(c_ins) + 4 * n:len(c_ins) + 4 * n + len(c_outs)],
                   refs[len(c_ins) + 4 * n + len(c_outs):])
        outs = refs[len(c_ins):]
        if carry:
            carry.start(*carried)
        at = 0
        for t in range(n):
            if widths[t] == 1:
                g = g_refs[at][...]
            else:
                g = g_refs[at][0].astype(F32)
                for k in range(3):
                    g = g + g_refs[at + 1][k].astype(F32)
            at += widths[t]
            outs[4 * t][...] = g
            outs[4 * t + 1][...], outs[4 * t + 2][...], outs[4 * t + 3][...] = _adamw_math(
                w_refs[t][...], g, m_refs[t][...], v_refs[t][...])
        if carry:
            if carry.relay is not None:
                carry.relay(*carried)
            carry.finish(*carried)

    def whole(a):
        zeros = (0,) * a.ndim
        return pl.BlockSpec(a.shape, lambda i, chip: zeros, pipeline_mode=pl.Buffered(1))

    g_specs = []
    for g in gs:
        if isinstance(g, (tuple, list)):
            g_specs += [pl.BlockSpec((1,) + g[0].shape[1:], lambda i, chip: (chip[0], 0, 0), pipeline_mode=pl.Buffered(1)),
                        whole(g[1])]
        else:
            g_specs.append(whole(g))
    shapes, out_specs = [], []
    for w in ws:
        shapes += [jax.ShapeDtypeStruct(w.shape, F32)] * 4
        out_specs += [whole(w)] * 4
    res = pl.pallas_call(
        body, name=name, out_shape=tuple(shapes) + tuple(c_outs),
        grid_spec=pltpu.PrefetchScalarGridSpec(
            num_scalar_prefetch=1, grid=(1,),
            in_specs=[whole(w) for w in ws] + g_specs + [whole(a) for a in ms + vs] + [_ANY] * len(c_ins),
            out_specs=tuple(out_specs) + (_ANY,) * len(c_outs), scratch_shapes=list(carry.sems) if carry else []),
        compiler_params=_params("arbitrary"),
    )(my_chip, *ws, *flat, *ms, *vs, *c_ins)
    return [tuple(res[4 * t:4 * t + 4]) for t in range(n)], tuple(res[4 * n:])


def adamw_replicated(w, partials, m, v, name):
    def body(w_ref, p_ref, m_ref, v_ref, g_ref, d_ref, mo_ref, vo_ref):
        g = p_ref[0]
        for k in range(1, N_DEV):
            g = g + p_ref[k]
        g_ref[...] = g
        d_ref[...], mo_ref[...], vo_ref[...] = _adamw_math(w_ref[...], g, m_ref[...], v_ref[...])

    shape = jax.ShapeDtypeStruct(w.shape, F32)
    return pl.pallas_call(body, name=name, out_shape=(shape,) * 4, compiler_params=_params())(w, partials, m, v)


PACK_COLS = 1024
PACK_ROW_ALIGN = 16

REPLICATED = ("ffn1_norm", "mix_norm", "conv_dw_b", "conv_ln_g", "conv_ln_b", "ffn2_norm", "final_norm", "gate_b", "attn_sinks")
WEIGHT_ORDER = ("ffn1_norm", "ffn1_w_gate", "ffn1_w_up", "ffn1_w_down", "mix_norm", "w_in", "conv_dw_w", "conv_dw_b", "conv_ln_g",
                "conv_ln_b", "conv_w_proj", "attn_sinks", "attn_w_o", "gate_b", "w_out", "ffn2_norm", "ffn2_w_gate", "ffn2_w_up",
                "ffn2_w_down", "final_norm")


def _to_rows(flat, lead):
    n = flat.shape[-1]
    rows = -(-n // PACK_COLS)
    flat = jnp.pad(flat, [(0, 0)] * lead + [(0, rows * PACK_COLS - n)])
    return flat.reshape(flat.shape[:lead] + (rows, PACK_COLS))


def _pad_rows(a, axis):
    rows = a.shape[axis]
    pad = -rows % PACK_ROW_ALIGN
    widths = [(0, 0)] * a.ndim
    widths[axis] = (0, pad)
    return jnp.pad(a, widths)


def _from_rows(rows, shape):
    n = 1
    for s in shape:
        n *= s
    return rows.reshape(rows.shape[:-2] + (-1,))[..., :n].reshape(rows.shape[:-2] + tuple(shape))


def _heads_slot_major(rows):
    group = rows.shape[0] // (N_KV_HEADS * HEAD_DIM)
    return rows.reshape(N_KV_HEADS, group, HEAD_DIM, rows.shape[1]).transpose(1, 0, 2, 3).reshape(rows.shape)


def _heads_kv_major(rows):
    group = rows.shape[0] // (N_KV_HEADS * HEAD_DIM)
    return rows.reshape(group, N_KV_HEADS, HEAD_DIM, rows.shape[1]).transpose(1, 0, 2, 3).reshape(rows.shape)


def _by_core(full_rows):
    return full_rows.reshape((N_DEV // 2, 2, full_rows.shape[0] // N_DEV, full_rows.shape[1]))


def kernel(x, positions, ffn1_norm, ffn1_w_gate, ffn1_w_up, ffn1_w_down, mix_norm, w_in, conv_dw_w, conv_dw_b, conv_ln_g, conv_ln_b, conv_w_proj, attn_sinks, attn_w_o, gate_b, w_out, ffn2_norm, ffn2_w_gate, ffn2_w_up, ffn2_w_down, final_norm, loss_target, m_ffn1_norm, m_ffn1_w_gate, m_ffn1_w_up, m_ffn1_w_down, m_mix_norm, m_w_in, m_conv_dw_w, m_conv_dw_b, m_conv_ln_g, m_conv_ln_b, m_conv_w_proj, m_attn_sinks, m_attn_w_o, m_gate_b, m_w_out, m_ffn2_norm, m_ffn2_w_gate, m_ffn2_w_up, m_ffn2_w_down, m_final_norm, v_ffn1_norm, v_ffn1_w_gate, v_ffn1_w_up, v_ffn1_w_down, v_mix_norm, v_w_in, v_conv_dw_w, v_conv_dw_b, v_conv_ln_g, v_conv_ln_b, v_conv_w_proj, v_attn_sinks, v_attn_w_o, v_gate_b, v_w_out, v_ffn2_norm, v_ffn2_w_gate, v_ffn2_w_up, v_ffn2_w_down, v_final_norm):
    given = dict(locals())
    shapes = {n: given[n].shape for n in WEIGHT_ORDER}
    w = {n: given[n].reshape(given[n].shape[-2:]) if given[n].ndim == 3 else given[n].reshape(1, -1) for n in WEIGHT_ORDER}
    m = {n: given["m_" + n].reshape(w[n].shape) for n in WEIGHT_ORDER}
    v = {n: given["v_" + n].reshape(w[n].shape) for n in WEIGHT_ORDER}
    my_x, my_y, my_c = _place()
    my_core = my_c.astype(jnp.int32).reshape(1)
    my_chip = (2 * my_x + my_y).astype(jnp.int32).reshape(1)
    xs, target = x[0], loss_target[0]
    T, D = xs.shape
    KV = N_KV_HEADS * HEAD_DIM
    K = w["conv_dw_w"].shape[0]

    def t16(n):
        return w[n].T.astype(BF16)

    def r16(n):
        return w[n].astype(BF16)

    blocks1 = [t16("ffn1_w_gate"), t16("ffn1_w_up"), r16("ffn1_w_down")]
    dw_bits = _pad_rows(_to_rows(lax.bitcast_convert_type(w["conv_dw_w"], BF16).reshape(-1), 0), 0)
    blocks2 = [t16("w_in"), r16("conv_w_proj"), r16("attn_w_o"), r16("w_out"), dw_bits]
    blocks3 = [t16("ffn2_w_gate"), t16("ffn2_w_up"), r16("ffn2_w_down")]
    cos, sin = rope_tables(positions[0])
    sink_col = jnp.repeat(w["attn_sinks"].reshape(-1), WINDOW).reshape(N_KV_HEADS, (D // KV) * WINDOW, 1)

    def full(gathered):
        return gathered.reshape(-1, gathered.shape[2])

    wgt1, wut1 = (full(g) for g in all_gather(blocks1[:2], "gather_ffn1_up"))
    (h1, a1, b1, s1), got = ffn_up(xs, w["ffn1_norm"], wgt1, wut1, "ffn1_up",
                                   carry=compose(gather_whole(blocks1[2:]), gather_send(blocks2[:1])))
    wd1 = full(got[0])
    (x1,), got = ffn_down(xs, s1, wd1, "ffn1_down", carry=compose(gather_forward(got[1:]), gather_send(blocks2[1:])))
    wint = full(got[0])
    wq = _heads_slot_major(wint[2 * D:3 * D])
    (h2, ga, gb, u0, q, sgc, sgt, kk, vv), got = mix_in_fwd(x1, w["mix_norm"], wint, wq, w["gate_b"], cos, sin, "mix_in_fwd",
                                                            carry=compose(gather_forward(got[1:]), gather_send(blocks3)))
    wcp, wo, wout = (full(g) for g in got[:3])
    wo = _heads_slot_major(wo)
    dw_full = lax.bitcast_convert_type(_from_rows(got[3], w["conv_dw_w"].shape + (2,)), F32)
    dw_full = dw_full.transpose(1, 0, 2).reshape(K, D)
    o, gath3 = attn_fwd(q, kk, vv, sink_col, "attn_fwd", carry=gather_forward(got[4:]))
    x2, u1, co, ao, merged = mix_out_fwd(x1, u0, o, sgc, sgt, dw_full, w["conv_dw_b"], w["conv_ln_g"], w["conv_ln_b"],
                                         wcp, wo, wout, "mix_out_fwd")
    wgt2, wut2, wd2 = (full(g) for g in gath3)
    loss, dx3, d_final, h3, a2, b2, s2 = ffn_fwd_loss(x2, w["ffn2_norm"], wgt2, wut2, wd2, w["final_norm"], target,
                                                      "ffn2_fwd_loss")

    small = {"final_norm": d_final}
    (da2, db2), _ = ffn_bwd_hidden(dx3, a2, b2, wd2, "ffn2_bwd_hidden")
    (dx2, small["ffn2_norm"]), _ = ffn_bwd_input(dx3, x2, w["ffn2_norm"], da2, db2, wgt2, wut2, "ffn2_bwd_input")
    core2 = [_by_core(g) for g in (wgrad(da2, h3, "ffn2_dwg")[0], wgrad(db2, h3, "ffn2_dwu")[0],
                                   wgrad(s2, dx3, "ffn2_dwd", b_scale=FFN_SCALE)[0])]
    (dgc, dgt, do, du1, dao, sums, g_wout, g_wcp), recv2 = mix_out_bwd(
        dx2, u1, co, ao, sgc, sgt, merged, w["conv_ln_g"], w["conv_ln_b"], wcp, wo, wout, "mix_out_bwd",
        carry=swap_halves(core2))
    chip2 = pair_sum(my_core, core2, recv2, "ffn2_grads_pair_sum")
    small["gate_b"] = jnp.concatenate([sums[0:1], sums[1:2]], axis=1)
    small["conv_ln_g"], small["conv_ln_b"], small["conv_dw_b"] = sums[2:3], sums[3:4], sums[4:5]
    g_wo =_heads_kv_major(wgrad(o, dao, "dw_attn_o")[0])
    (dga, dgb, g_dw), got2 = conv_bwd(du1, u0, ga, gb, dw_full, "conv_bwd", carry=exchange_between_chips(chip2))
    dq, dk, dv, dsink = attn_bwd(q, kk, vv, do, sink_col, cos, sin, "attn_bwd")
    small["attn_sinks"] = dsink[:D // KV, :N_KV_HEADS].T.reshape(1, -1)
    pieces = [dga, dgb, dq, dk, dv, dgc, dgt]
    dx1, small["mix_norm"] = mix_in_bwd(dx2, x1, w["mix_norm"], wint, wq, pieces, "mix_in_bwd")
    group = D // KV
    q_moves = [(2 * D + HEAD_DIM * (group * g + hh), 2 * D + HEAD_DIM * (N_KV_HEADS * hh + g), HEAD_DIM)
               for g in range(N_KV_HEADS) for hh in range(group)]
    g_wint = wgrad_stacked(pieces[:3], h2, "dw_in_a", wint.shape[0], 0, moves=[(0, 0, 2 * D)] + q_moves)
    g_wint = wgrad_stacked(pieces[3:], h2, "dw_in_b", wint.shape[0], 3 * D, into=g_wint)
    corem = [_by_core(a) for a in (g_wint, g_wcp, g_wo, g_wout)]

    (da1, db1), recvm = ffn_bwd_hidden(dx1, a1, b1, wd1, "ffn1_bwd_hidden", carry=swap_halves(corem))
    chipm = pair_sum(my_core, corem, recvm, "mix_grads_pair_sum")
    g1c, gotm_a = wgrad(s1, dx1, "ffn1_dwd", carry=exchange_between_chips(chipm[:1]), b_scale=FFN_SCALE)
    g1a, gotm_b = wgrad(da1, h1, "ffn1_dwg", carry=exchange_between_chips(chipm[1:]))
    early = [_by_core(g1a), _by_core(g1c)]
    g1b, recv_early = wgrad(db1, h1, "ffn1_dwu", carry=swap_halves(early))
    (recv_late,) = run_exchange(swap_halves([_by_core(g1b)]), "ffn1_grads_swap")
    core1 = [early[0], _by_core(g1b), early[1]]
    chip1 = pair_sum(my_core, core1, [recv_early[0], recv_late, recv_early[1]], "ffn1_grads_pair_sum")
    (grad_x, small["ffn1_norm"]), got1 = ffn_bwd_input(dx1, xs, w["ffn1_norm"], da1, db1, wgt1, wut1, "ffn1_bwd_input",
                                                       carry=exchange_between_chips(chip1))

    gotm = gotm_a + gotm_b
    grad_src = {"ffn1_w_gate": (chip1[0], got1[0]), "ffn1_w_up": (chip1[1], got1[1]), "ffn1_w_down": (chip1[2], got1[2]),
                "ffn2_w_gate": (chip2[0], got2[0]), "ffn2_w_up": (chip2[1], got2[1]), "ffn2_w_down": (chip2[2], got2[2]),
                "w_in": (chipm[0], gotm[0]), "conv_w_proj": (chipm[1], gotm[1]), "attn_w_o": (chipm[2], gotm[2]),
                "w_out": (chipm[3], gotm[3])}
    grads = {}

    def pack_small(d, taps, extra):
        rows = [_to_rows(d[n].reshape(-1), 0) for n in REPLICATED] + [taps, _to_rows(extra.reshape(-1), 0)]
        return _pad_rows(jnp.concatenate(rows, axis=0), 0)

    def like(a, ref):
        return a if a.shape == ref.shape else a.T

    def adamw_group(names, name, carry=None):
        refs = [grad_src[n][0][0] if isinstance(grad_src[n], tuple) else grad_src[n] for n in names]
        res, carried = adamw(my_chip, [like(w[n], r) for n, r in zip(names, refs)], [grad_src[n] for n in names],
                             [like(m[n], r) for n, r in zip(names, refs)], [like(v[n], r) for n, r in zip(names, refs)], name,
                             carry=carry)
        for n, outs in zip(names, res):
            grads[n], delta[n], new_m[n], new_v[n] = (like(a, w[n]) for a in outs)
        return carried

    delta, new_m, new_v = {}, {}, {}
    zero, no_taps = jnp.zeros((1, LANES), F32), jnp.zeros((K, D), F32)
    (shares,) = adamw_group(("ffn2_w_gate", "ffn2_w_up", "ffn2_w_down"), "adamw_ffn2",
                            carry=gather_whole([pack_small(small, g_dw, loss)]))
    g_s, d_s, m_s, v_s = adamw_replicated(pack_small(w, no_taps, zero), shares, pack_small(m, no_taps, zero),
                                          pack_small(v, no_taps, zero), "adamw_replicated")
    off = 0
    for n in REPLICATED:
        r = -(-w[n].shape[1] // PACK_COLS)
        grads[n], delta[n], new_m[n], new_v[n] = (_from_rows(a[off:off + r], w[n].shape) for a in (g_s, d_s, m_s, v_s))
        off += r
    shard_cols = w["conv_dw_w"].shape[1]
    grad_src["conv_dw_w"] = lax.dynamic_slice_in_dim(g_s[off:off + K], _dev_index(my_x, my_y, my_c) * shard_cols, shard_cols,
                                                    axis=1)
    total_loss = g_s[off + K, 0]
    adamw_group(("ffn1_w_gate", "ffn1_w_up", "ffn1_w_down"), "adamw_ffn1")
    adamw_group(("w_in", "conv_dw_w", "conv_w_proj", "attn_w_o", "w_out"), "adamw_mix")

    out = [total_loss, grad_x[None]]
    for d in (grads, delta, new_m, new_v):
        out += [d[n].reshape(shapes[n]) for n in WEIGHT_ORDER]
    return tuple(out)
```

```python
import functools
from typing import Callable, NamedTuple

import jax
import jax.numpy as jnp
from jax import lax
from jax.experimental import pallas as pl
from jax.experimental.pallas import tpu as pltpu

F32, BF16 = jnp.float32, jnp.bfloat16

HEAD_DIM = 64
N_KV_HEADS = 4
WINDOW = 128
ROPE_THETA = 10000.0
EPS = 1e-6
LN_EPS = 1e-5
NEG_INF = -1e30
ADAM_LR, ADAM_B1, ADAM_B2, ADAM_EPS, ADAM_WD, ADAM_STEP = 0.001, 0.9, 0.999, 1e-08, 0.01, 10
ADAMW_STEPS = 4

N_DEV = 8
LANES = 128
SUBLANES = 8
CONV_HALO = 32
CONV_ROWS, CONV_LANES = 128, 128
ROW_TILE = 512
FFN_CHUNK = 256
FFN_SCALE = 0.5
WGRAD_TILE_ELEMS = 2 ** 22
WGRAD_TILE_ROWS = 2048
WGRAD_VMEM = 40 * 2 ** 20
VMEM_LIMIT = 56 * 2 ** 20
MESH = pl.DeviceIdType.MESH


def _params(*sem):
    return pltpu.CompilerParams(dimension_semantics=sem or None, vmem_limit_bytes=VMEM_LIMIT)


def _resident(shape):
    zeros = (0,) * len(shape)
    return pl.BlockSpec(shape, lambda *_: zeros, pipeline_mode=pl.Buffered(1))


def _rows(tm, n):
    return pl.BlockSpec((tm, n), lambda i: (i, 0))


def _acc_spec(shape):
    zeros = (0,) * len(shape)
    return pl.BlockSpec(shape, lambda *_: zeros)


_ANY = pl.BlockSpec(memory_space=pl.ANY)


class Carry(NamedTuple):
    ins: tuple
    out_shapes: tuple
    aliases: dict
    sems: tuple
    start: Callable
    finish: Callable
    relay: Callable = None
    copies: Callable = None


def _call(body, *, name, grid, in_specs, out_specs, out_shape, args, scratch_shapes=(), carry=None):
    n_in, n_out, n_scr = len(in_specs), len(out_specs), len(scratch_shapes)
    params = _params(*(("arbitrary",) * len(grid)))
    if carry is None:
        res = pl.pallas_call(body, name=name, grid=grid, in_specs=list(in_specs), out_specs=tuple(out_specs),
                             out_shape=tuple(out_shape), scratch_shapes=list(scratch_shapes), compiler_params=params)(*args)
        return tuple(res), ()
    c_in, c_out = len(carry.ins), len(carry.out_shapes)

    def wrapped(*refs):
        ins, c_ins = refs[:n_in], refs[n_in:n_in + c_in]
        p = n_in + c_in
        outs, c_outs = refs[p:p + n_out], refs[p + n_out:p + n_out + c_out]
        p += n_out + c_out
        scr, c_sems = refs[p:p + n_scr], refs[p + n_scr:]
        ids = [pl.program_id(d) for d in range(len(grid))]
        first = functools.reduce(jnp.logical_and, [i == 0 for i in ids])
        last = functools.reduce(jnp.logical_and, [i == n - 1 for i, n in zip(ids, grid)])

        @pl.when(first)
        def _():
            carry.start(c_ins, c_outs, c_sems)

        body(*ins, *outs, *scr)

        if carry.relay is not None:
            @pl.when(ids[0] == (3 * grid[0]) // 4)
            def _():
                carry.relay(c_ins, c_outs, c_sems)

        @pl.when(last)
        def _():
            carry.finish(c_ins, c_outs, c_sems)

    res = pl.pallas_call(
        wrapped, name=name, grid=grid, in_specs=list(in_specs) + [_ANY] * c_in, out_specs=tuple(out_specs) + (_ANY,) * c_out,
        out_shape=tuple(out_shape) + tuple(carry.out_shapes), scratch_shapes=list(scratch_shapes) + list(carry.sems),
        input_output_aliases={n_in + i: n_out + o for i, o in carry.aliases.items()}, compiler_params=params,
    )(*args, *carry.ins)
    return tuple(res[:n_out]), tuple(res[n_out:])


def _nt(a, b):
    return lax.dot_general(a, b, (((1,), (1,)), ((), ())), preferred_element_type=F32)


def _tn(a, b):
    return lax.dot_general(a, b, (((0,), (0,)), ((), ())), preferred_element_type=F32)


def _dot(a, b):
    return jnp.dot(a, b, preferred_element_type=F32)


def _sigmoid(x):
    return 1.0 / (1.0 + jnp.exp(-x))


def _rms_fwd(x, g):
    r = lax.rsqrt(jnp.mean(x * x, axis=-1, keepdims=True) + EPS)
    return (x * r) * g


def _rms_bwd(x, g, dy):
    r = lax.rsqrt(jnp.mean(x * x, axis=-1, keepdims=True) + EPS)
    xhat = x * r
    dyg = dy * g
    dx = r * (dyg - xhat * jnp.mean(dyg * xhat, axis=-1, keepdims=True))
    return dx, dy * xhat


def _rot_half(x):
    lane = lax.broadcasted_iota(jnp.int32, (x.shape[0], LANES), 1)
    first = (lane % HEAD_DIM) < (HEAD_DIM // 2)
    out = []
    for s in range(x.shape[1] // LANES):
        xs = x[:, LANES * s:LANES * (s + 1)]
        out.append(jnp.where(first, pltpu.roll(xs, LANES - HEAD_DIM // 2, 1), pltpu.roll(xs, HEAD_DIM // 2, 1)))
    return out[0] if len(out) == 1 else jnp.concatenate(out, axis=1)


def _tile_lanes(t, width):
    return t if width == LANES else jnp.concatenate([t] * (width // LANES), axis=1)


def _rope_fwd(x, cos, sin_signed):
    w = x.shape[1]
    return x * _tile_lanes(cos, w) + _rot_half(x) * _tile_lanes(sin_signed, w)


def _rope_bwd(dy, cos, sin_signed):
    w = dy.shape[1]
    return dy * _tile_lanes(cos, w) + _rot_half(dy * _tile_lanes(sin_signed, w))


def _ffn_rows(x, g_ref, wg_ref, wu_ref, wd_ref, h_ref, a_ref, b_ref, s_ref, acc_ref):
    F = wg_ref.shape[0]
    h = _rms_fwd(x, g_ref[...]).astype(BF16)
    h_ref[...] = h
    for c in range(F // FFN_CHUNK):
        cs = pl.ds(c * FFN_CHUNK, FFN_CHUNK)
        a = _nt(h, wg_ref[cs, :])
        b = _nt(h, wu_ref[cs, :])
        a_ref[:, cs] = a.astype(BF16)
        b_ref[:, cs] = b.astype(BF16)
        s = (a * _sigmoid(a) * b).astype(BF16)
        s_ref[:, cs] = s
        y = _dot(s, wd_ref[cs, :])
        if c == 0:
            acc_ref[...] = y
        else:
            acc_ref[...] += y
    return x + FFN_SCALE * acc_ref[...]


def ffn_up(x, gain, wgt, wut, name, carry=None):
    T, D = x.shape
    F = wgt.shape[0]
    tm = min(ROW_TILE, T)

    def body(x_ref, g_ref, wg_ref, wu_ref, h_ref, a_ref, b_ref, s_ref):
        h = _rms_fwd(x_ref[...], g_ref[...]).astype(BF16)
        h_ref[...] = h
        for c in range(F // FFN_CHUNK):
            cs = pl.ds(c * FFN_CHUNK, FFN_CHUNK)
            a = _nt(h, wg_ref[cs, :])
            b = _nt(h, wu_ref[cs, :])
            a_ref[:, cs] = a.astype(BF16)
            b_ref[:, cs] = b.astype(BF16)
            s_ref[:, cs] = (a * _sigmoid(a) * b).astype(BF16)

    wide = jax.ShapeDtypeStruct((T, F), BF16)
    return _call(
        body, name=name, grid=(T // tm,), out_shape=(jax.ShapeDtypeStruct((T, D), BF16), wide, wide, wide),
        in_specs=[_rows(tm, D), _resident((1, D)), _resident((F, D)), _resident((F, D))],
        out_specs=(_rows(tm, D), _rows(tm, F), _rows(tm, F), _rows(tm, F)), args=(x, gain, wgt, wut), carry=carry)


def ffn_down(x, s, wd, name, carry=None):
    T, D = x.shape
    F = wd.shape[0]
    tm = min(2 * ROW_TILE, T)

    def body(x_ref, s_ref, wd_ref, xo_ref):
        xo_ref[...] = x_ref[...] + FFN_SCALE * _dot(s_ref[...], wd_ref[...])

    return _call(
        body, name=name, grid=(T // tm,), out_shape=(jax.ShapeDtypeStruct((T, D), F32),),
        in_specs=[_rows(tm, D), _rows(tm, F), _resident((F, D))], out_specs=(_rows(tm, D),), args=(x, s, wd), carry=carry)


def ffn_fwd_loss(x, gain, wgt, wut, wd, final_gain, target, name):
    T, D = x.shape
    F = wgt.shape[0]
    tm = min(ROW_TILE, T)

    def body(x_ref, g_ref, wg_ref, wu_ref, wd_ref, gf_ref, t_ref,
             loss_ref, dx_ref, dg_ref, h_ref, a_ref, b_ref, s_ref, acc_ref):
        xo = _ffn_rows(x_ref[...], g_ref, wg_ref, wu_ref, wd_ref, h_ref, a_ref, b_ref, s_ref, acc_ref)
        gf = gf_ref[...]
        err = _rms_fwd(xo, gf) - t_ref[...]
        dx, dgt = _rms_bwd(xo, gf, err * (1.0 / D))
        dx_ref[...] = dx

        @pl.when(pl.program_id(0) == 0)
        def _():
            dg_ref[...] = jnp.zeros_like(dg_ref)
            loss_ref[...] = jnp.zeros_like(loss_ref)
        dg_ref[...] += jnp.sum(dgt, axis=0, keepdims=True)
        per_token = jnp.sum(err * err, axis=-1, keepdims=True) * (0.5 / D)
        loss_ref[...] += jnp.broadcast_to(jnp.sum(per_token, axis=0, keepdims=True), (1, LANES))

    return pl.pallas_call(
        body, name=name, grid=(T // tm,),
        out_shape=(jax.ShapeDtypeStruct((1, LANES), F32), jax.ShapeDtypeStruct((T, D), F32), jax.ShapeDtypeStruct((1, D), F32),
                   jax.ShapeDtypeStruct((T, D), BF16), jax.ShapeDtypeStruct((T, F), BF16), jax.ShapeDtypeStruct((T, F), BF16),
                   jax.ShapeDtypeStruct((T, F), BF16)),
        in_specs=[_rows(tm, D), _resident((1, D)), _resident((F, D)), _resident((F, D)), _resident((F, D)), _resident((1, D)),
                  _rows(tm, D)],
        out_specs=(_acc_spec((1, LANES)), _rows(tm, D), _acc_spec((1, D)), _rows(tm, D), _rows(tm, F), _rows(tm, F),
                   _rows(tm, F)),
        scratch_shapes=[pltpu.VMEM((tm, D), F32)], compiler_params=_params("arbitrary"),
    )(x, gain, wgt, wut, wd, final_gain, target)


def ffn_bwd_hidden(dxo, a, b, wd, name, carry=None):
    T, D = dxo.shape
    F = wd.shape[0]
    tm = min(ROW_TILE, T)
    fc = FFN_CHUNK

    def hidden_body(dxo_ref, a_ref, b_ref, wd_ref, da_ref, db_ref):
        g0 = (FFN_SCALE * dxo_ref[...]).astype(BF16)
        for c in range(F // fc):
            cs = pl.ds(c * fc, fc)
            ds = _nt(g0, wd_ref[cs, :])
            a = a_ref[:, cs].astype(F32)
            bb = b_ref[:, cs].astype(F32)
            sa = _sigmoid(a)
            da_ref[:, cs] = (ds * bb * (sa * (1.0 + a * (1.0 - sa)))).astype(BF16)
            db_ref[:, cs] = (ds * (a * sa)).astype(BF16)

    wide = jax.ShapeDtypeStruct((T, F), BF16)
    return _call(
        hidden_body, name=name, grid=(T // tm,), out_shape=(wide, wide),
        in_specs=[_rows(tm, D), _rows(tm, F), _rows(tm, F), _resident((F, D))],
        out_specs=(_rows(tm, F), _rows(tm, F)), args=(dxo, a, b, wd), carry=carry)


def ffn_bwd_input(dxo, x, gain, da, db, wg, wu, name, carry=None):
    T, D = x.shape
    F = wg.shape[0]
    tm = min(ROW_TILE, T)

    def input_body(dxo_ref, x_ref, g_ref, da_ref, db_ref, wg_ref, wu_ref, dx_ref, dg_ref):
        dh = _dot(da_ref[...], wg_ref[...]) + _dot(db_ref[...], wu_ref[...])
        dx, dgt = _rms_bwd(x_ref[...], g_ref[...], dh)
        dx_ref[...] = dxo_ref[...] + dx

        @pl.when(pl.program_id(0) == 0)
        def _():
            dg_ref[...] = jnp.zeros_like(dg_ref)
        dg_ref[...] += jnp.sum(dgt, axis=0, keepdims=True)

    return _call(
        input_body, name=name, grid=(T // tm,),
        out_shape=(jax.ShapeDtypeStruct((T, D), F32), jax.ShapeDtypeStruct((1, D), F32)),
        in_specs=[_rows(tm, D), _rows(tm, D), _resident((1, D)), _rows(tm, F), _rows(tm, F), _resident((F, D)),
                  _resident((F, D))],
        out_specs=(_rows(tm, D), _acc_spec((1, D))), args=(dxo, x, gain, da, db, wg, wu), carry=carry)


def wgrad(a, b, name, carry=None, b_scale=None):
    T, M = a.shape
    N = b.shape[1]
    fixed = M * N * (4 + 2)
    per_row = 2 * (M * a.dtype.itemsize + N * b.dtype.itemsize)
    tk = ROW_TILE
    while fixed + 2 * tk * per_row <= WGRAD_VMEM and 2 * tk <= WGRAD_TILE_ROWS:
        tk *= 2
    tk = min(tk, T)
    nk = T // tk

    def body(a_ref, b_ref, o_ref, acc_ref):
        k = pl.program_id(0)
        bt = b_ref[...] if b_scale is None else b_scale * b_ref[...]
        part = _tn(a_ref[...].astype(BF16), bt.astype(BF16))

        @pl.when(k == 0)
        def _():
            acc_ref[...] = part

        @pl.when(k > 0)
        def _():
            acc_ref[...] += part

        @pl.when(k == nk - 1)
        def _():
            o_ref[...] = acc_ref[...].astype(BF16)

    (out,), carried = _call(
        body, name=name, grid=(nk,), out_shape=(jax.ShapeDtypeStruct((M, N), BF16),),
        in_specs=[_rows(tk, M), _rows(tk, N)], out_specs=(_resident((M, N)),),
        scratch_shapes=[pltpu.VMEM((M, N), F32)], args=(a, b), carry=carry)
    return out, carried


def wgrad_stacked(pieces, b, name, total_rows, at, into=None, moves=None):
    T, N = b.shape
    n = len(pieces)
    widths = [p.shape[1] for p in pieces]
    offs = [sum(widths[:i]) for i in range(n)]
    M = sum(widths)
    tk = ROW_TILE
    while 2 * tk * M <= WGRAD_TILE_ELEMS and 2 * tk <= WGRAD_TILE_ROWS:
        tk *= 2
    tk = min(tk, T)
    nk = T // tk
    moves = moves or [(0, 0, M)]
    n_in = n + 1 + (into is not None)

    def body(*refs):
        a_refs, b_ref = refs[:n], refs[n]
        o_ref, acc_ref, stage, sem = refs[n_in:]
        k = pl.program_id(0)

        @pl.when(k == 0)
        def _():
            acc_ref[...] = jnp.zeros_like(acc_ref)
        bt = b_ref[...].astype(BF16)
        for a_ref, off, width in zip(a_refs, offs, widths):
            acc_ref[pl.ds(off, width), :] += _tn(a_ref[...].astype(BF16), bt)

        @pl.when(k == nk - 1)
        def _():
            for to, start, rows in moves:
                stage[pl.ds(to, rows), :] = acc_ref[pl.ds(start, rows), :].astype(BF16)
            cp = pltpu.make_async_copy(stage, o_ref.at[pl.ds(at, M)], sem)
            cp.start()
            cp.wait()

    return pl.pallas_call(
        body, name=name, grid=(nk,), out_shape=jax.ShapeDtypeStruct((total_rows, N), BF16),
        in_specs=[_rows(tk, width) for width in widths] + [_rows(tk, N)] + [_ANY] * (into is not None), out_specs=_ANY,
        scratch_shapes=[pltpu.VMEM((M, N), F32), pltpu.VMEM((M, N), BF16), pltpu.SemaphoreType.DMA],
        input_output_aliases={n + 1: 0} if into is not None else {}, compiler_params=_params("arbitrary"),
    )(*pieces, b, *([into] if into is not None else []))


def _w_in_rows(D):
    KV = N_KV_HEADS * HEAD_DIM
    return 0, D, 2 * D, 3 * D, 3 * D + KV, 3 * D + 2 * KV, 4 * D + 2 * KV


def mix_in_fwd(x, gain, wint, wq, gate_b, cos, sin_signed, name, carry=None):
    T, D = x.shape
    KV = N_KV_HEADS * HEAD_DIM
    tm = min(ROW_TILE, T)
    o_ga, o_gb, _, o_k, o_v, o_gc, o_gt = _w_in_rows(D)

    def body(x_ref, g_ref, w_ref, wq_ref, gb_ref, cos_ref, sin_ref,
             h_ref, ga_ref, gb_out_ref, u0_ref, q_ref, sgc_ref, sgt_ref, k_ref, v_ref):
        h = _rms_fwd(x_ref[...], g_ref[...]).astype(BF16)
        h_ref[...] = h
        cos, sin = cos_ref[...], sin_ref[...]
        ga = _nt(h, w_ref[pl.ds(o_ga, D), :])
        gb = _nt(h, w_ref[pl.ds(o_gb, D), :])
        ga_ref[...] = ga.astype(BF16)
        gb_out_ref[...] = gb.astype(BF16)
        u0_ref[...] = (ga * _sigmoid(gb)).astype(BF16)
        q = _nt(h, wq_ref[...])
        q_ref[...] = _rope_fwd(q, cos, sin).astype(BF16)
        gc = _nt(h, w_ref[pl.ds(o_gc, D), :]) + gb_ref[:, pl.ds(0, D)]
        sgc_ref[...] = _sigmoid(gc).astype(BF16)
        gt = _nt(h, w_ref[pl.ds(o_gt, D), :]) + gb_ref[:, pl.ds(D, D)]
        sgt_ref[...] = _sigmoid(gt).astype(BF16)
        k = _nt(h, w_ref[pl.ds(o_k, KV), :])
        k_ref[...] = _rope_fwd(k, cos, sin).astype(BF16)
        v_ref[...] = _nt(h, w_ref[pl.ds(o_v, KV), :]).astype(BF16)

    big = jax.ShapeDtypeStruct((T, D), BF16)
    small = jax.ShapeDtypeStruct((T, KV), BF16)
    return _call(
        body, name=name, grid=(T // tm,),
        out_shape=(big, big, big, big, big, big, big, small, small),
        in_specs=[_rows(tm, D), _resident((1, D)), _resident(wint.shape), _resident(wq.shape), _resident((1, 2 * D)),
                  _rows(tm, LANES), _rows(tm, LANES)],
        out_specs=(_rows(tm, D),) * 7 + (_rows(tm, KV),) * 2,
        args=(x, gain, wint, wq, gate_b, cos, sin_signed), carry=carry)


def _from_prev(rows):
    qi = lax.broadcasted_iota(jnp.int32, (rows, WINDOW), 0) % WINDOW
    return lax.broadcasted_iota(jnp.int32, (rows, WINDOW), 1) > qi


def _fold(x, g, from_prev):
    lo = 2 * WINDOW * g
    return jnp.where(from_prev, x[:, lo:lo + WINDOW], x[:, lo + WINDOW:lo + 2 * WINDOW])


def _unfold(folded, from_prev):
    zero = jnp.zeros_like(folded[0])
    parts = []
    for x in folded:
        parts += [jnp.where(from_prev, x, zero), jnp.where(from_prev, zero, x)]
    return jnp.concatenate(parts, axis=1)


def _kv_lane_head(rows, width):
    return lax.broadcasted_iota(jnp.int32, (rows, width), 1) // HEAD_DIM


def _block_diag(win):
    head = _kv_lane_head(*win.shape)
    zero = jnp.zeros_like(win)
    return jnp.concatenate([jnp.where(head == g, win, zero) for g in range(N_KV_HEADS)], axis=0)


def _diag_blocks_sum(bd, keys):
    head = _kv_lane_head(keys, bd.shape[1])
    out = jnp.zeros((keys, bd.shape[1]), F32)
    for g in range(N_KV_HEADS):
        out = jnp.where(head == g, bd[g * keys:(g + 1) * keys], out)
    return out


def _kv_windows(k_ref, kh_ref, v_ref, vh_ref, j):
    rows = pl.ds(j * WINDOW, WINDOW)
    if j == 0:
        kprev, vprev = kh_ref[...], vh_ref[...]
    else:
        prev = pl.ds((j - 1) * WINDOW, WINDOW)
        kprev, vprev = k_ref[prev, :], v_ref[prev, :]
    return jnp.concatenate([kprev, k_ref[rows, :]], axis=0), jnp.concatenate([vprev, v_ref[rows, :]], axis=0)


def _stack_slots(ref, j, group, KV):
    rows = pl.ds(j * WINDOW, WINDOW)
    return jnp.concatenate([ref[rows, pl.ds(KV * hh, KV)] for hh in range(group)], axis=0)


def _attn_exp(qs, kbd, sink_ref, from_prev, no_prev):
    s = _nt(qs, kbd)
    if no_prev is not None:
        qi = lax.broadcasted_iota(jnp.int32, from_prev.shape, 0) % WINDOW
        absent = lax.broadcasted_iota(jnp.int32, from_prev.shape, 1) > jnp.where(no_prev, qi, WINDOW)
    out = []
    for g in range(N_KV_HEADS):
        sg = _fold(s, g, from_prev) * (HEAD_DIM ** -0.5)
        if no_prev is not None:
            sg = jnp.where(absent, NEG_INF, sg)
        sink = sink_ref[g]
        m = jnp.maximum(jnp.max(sg, axis=-1, keepdims=True), sink)
        out.append((jnp.exp(sg - m), jnp.exp(sink - m)))
    return out


def _spread_over_heads(cols, rows, KV):
    head = _kv_lane_head(rows, KV)
    out = jnp.zeros((rows, KV), F32)
    for g, col in enumerate(cols):
        out = jnp.where(head == g, col, out)
    return out


def _halo_rows_spec(tq, width, sub):
    return pl.BlockSpec((sub, width), lambda i: (jnp.maximum(i * (tq // sub) - 1, 0), 0))


def attn_fwd(q, k, v, sink_col, name, carry=None):
    T, D = q.shape
    KV = k.shape[1]
    group = D // KV
    tq = min(ROW_TILE, T)
    nsub = tq // WINDOW
    rows, wide = group * WINDOW, N_KV_HEADS * 2 * WINDOW

    def body(q_ref, k_ref, kh_ref, v_ref, vh_ref, sink_ref, o_ref):
        from_prev = _from_prev(rows)
        head = _kv_lane_head(wide, KV)
        block = lax.broadcasted_iota(jnp.int32, head.shape, 0) // (2 * WINDOW)
        ones_bd = jnp.where(head == block, 1.0, 0.0).astype(BF16)
        for j in range(nsub):
            k_win, v_win = _kv_windows(k_ref, kh_ref, v_ref, vh_ref, j)
            parts = _attn_exp(_stack_slots(q_ref, j, group, KV), _block_diag(k_win), sink_ref, from_prev,
                              pl.program_id(0) == 0 if j == 0 else None)
            p = _unfold([pg.astype(BF16) for pg, _ in parts], from_prev)
            both = _dot(p, jnp.concatenate([_block_diag(v_win), ones_bd], axis=1))
            denom = both[:, KV:] + _spread_over_heads([es for _, es in parts], rows, KV)
            out = (both[:, :KV] / denom).astype(BF16)
            for hh in range(group):
                o_ref[pl.ds(j * WINDOW, WINDOW), pl.ds(KV * hh, KV)] = out[hh * WINDOW:(hh + 1) * WINDOW]

    (o,), carried = _call(
        body, name=name, grid=(T // tq,),
        out_shape=(jax.ShapeDtypeStruct((T, D), BF16),),
        in_specs=[_rows(tq, D), _rows(tq, KV), _halo_rows_spec(tq, KV, WINDOW), _rows(tq, KV),
                  _halo_rows_spec(tq, KV, WINDOW), _resident(sink_col.shape)],
        out_specs=(_rows(tq, D),), args=(q, k, k, v, v, sink_col), carry=carry)
    return o, carried


def attn_bwd(q, k, v, do, sink_col, cos, sin_signed, name):
    T, D = q.shape
    KV = k.shape[1]
    group = D // KV
    tq = min(ROW_TILE, T)
    nsub = tq // WINDOW
    nt = T // tq
    scale = HEAD_DIM ** -0.5
    rows, wide = group * WINDOW, N_KV_HEADS * 2 * WINDOW

    def rev(i):
        return nt - 1 - i

    def body(q_ref, k_ref, kh_ref, v_ref, vh_ref, do_ref, sink_ref, cos_ref, sin_ref,
             dq_ref, dk_ref, dv_ref, dsink_ref, dq_acc, dk_acc, dv_acc, carry_k, carry_v):
        i = pl.program_id(0)

        @pl.when(i == 0)
        def _():
            carry_k[...] = jnp.zeros_like(carry_k)
            carry_v[...] = jnp.zeros_like(carry_v)
            dsink_ref[...] = jnp.zeros_like(dsink_ref)

        dk_acc[...] = jnp.zeros_like(dk_acc)
        dv_acc[...] = jnp.zeros_like(dv_acc)
        from_prev = _from_prev(rows)
        lane = lax.broadcasted_iota(jnp.int32, (1, LANES), 1)
        for j in range(nsub):
            k_win, v_win = _kv_windows(k_ref, kh_ref, v_ref, vh_ref, j)
            kbd, vbd = _block_diag(k_win), _block_diag(v_win)
            qs, dos = _stack_slots(q_ref, j, group, KV), _stack_slots(do_ref, j, group, KV)
            dp = _nt(dos, vbd)
            probs16, ds16 = [], []
            for g, (pg, es) in enumerate(_attn_exp(qs, kbd, sink_ref, from_prev, rev(i) == 0 if j == 0 else None)):
                inv = 1.0 / (jnp.sum(pg, axis=-1, keepdims=True) + es)
                probs = pg * inv
                dpg = _fold(dp, g, from_prev)
                delta = jnp.sum(probs * dpg, axis=-1, keepdims=True)
                probs16.append(probs.astype(BF16))
                ds16.append((probs * (dpg - delta) * scale).astype(BF16))
                dsk = -(es * inv * delta)
                for hh in range(group):
                    tot = jnp.sum(dsk[hh * WINDOW:(hh + 1) * WINDOW], axis=0, keepdims=True)
                    dsink_ref[pl.ds(hh, 1), :] += jnp.where(lane == g, tot, 0.0)
            ds = _unfold(ds16, from_prev)
            dqs = _dot(ds, kbd)
            for hh in range(group):
                dq_acc[pl.ds(j * WINDOW, WINDOW), pl.ds(KV * hh, KV)] = dqs[hh * WINDOW:(hh + 1) * WINDOW]
            keys = pl.ds(j * WINDOW, 2 * WINDOW)
            dk_acc[keys, :] += _diag_blocks_sum(_tn(ds, qs), 2 * WINDOW)
            dv_acc[keys, :] += _diag_blocks_sum(_tn(_unfold(probs16, from_prev), dos), 2 * WINDOW)

        tail = pl.ds(tq, WINDOW)
        dk_acc[tail, :] += carry_k[...]
        dv_acc[tail, :] += carry_v[...]
        carry_k[...] = dk_acc[pl.ds(0, WINDOW), :]
        carry_v[...] = dv_acc[pl.ds(0, WINDOW), :]
        cos, sin = cos_ref[...], sin_ref[...]
        dq_ref[...] = _rope_bwd(dq_acc[...], cos, sin).astype(BF16)
        dk_ref[...] = _rope_bwd(dk_acc[pl.ds(WINDOW, tq), :], cos, sin).astype(BF16)
        dv_ref[...] = dv_acc[pl.ds(WINDOW, tq), :].astype(BF16)

    def rrows(n):
        return pl.BlockSpec((tq, n), lambda i: (rev(i), 0))

    def rhalo(n):
        return pl.BlockSpec((WINDOW, n), lambda i: (jnp.maximum(rev(i) * nsub - 1, 0), 0))

    return pl.pallas_call(
        body, name=name, grid=(nt,),
        out_shape=(jax.ShapeDtypeStruct((T, D), BF16), jax.ShapeDtypeStruct((T, KV), BF16),
                   jax.ShapeDtypeStruct((T, KV), BF16), jax.ShapeDtypeStruct((SUBLANES, LANES), F32)),
        in_specs=[rrows(D), rrows(KV), rhalo(KV), rrows(KV), rhalo(KV), rrows(D), _resident(sink_col.shape),
                  rrows(LANES), rrows(LANES)],
        out_specs=(rrows(D), rrows(KV), rrows(KV), _acc_spec((SUBLANES, LANES))),
        scratch_shapes=[pltpu.VMEM((tq, D), F32), pltpu.VMEM((WINDOW + tq, KV), F32), pltpu.VMEM((WINDOW + tq, KV), F32),
                        pltpu.VMEM((WINDOW, KV), F32), pltpu.VMEM((WINDOW, KV), F32)],
        compiler_params=_params("arbitrary"),
    )(q, k, k, v, v, do, sink_col, cos, sin_signed)


def _ln_stats(u):
    mu = jnp.mean(u, axis=-1, keepdims=True)
    d = u - mu
    rstd = lax.rsqrt(jnp.mean(d * d, axis=-1, keepdims=True) + LN_EPS)
    return d * rstd, rstd


def _lag_taps(b, K):
    return [(a, K - 1 - (SUBLANES * a + b)) for a in range(-(-K // SUBLANES)) if SUBLANES * a + b <= K - 1]


def _conv_chunks(tm, D, chunk):
    def rows(c, carry):
        r0 = pl.multiple_of(c * CONV_ROWS, CONV_ROWS)
        for l0 in range(0, D, CONV_LANES):
            chunk(r0, pl.ds(l0, CONV_LANES))
        return carry
    lax.fori_loop(0, tm // CONV_ROWS, rows, 0)


def _conv_causal(buf, w_ref, bias_ref, out_ref, tm, D, K):
    def chunk(r0, lanes):
        acc = jnp.broadcast_to(bias_ref[:, lanes], (CONV_ROWS, CONV_LANES))
        for b in range(SUBLANES):
            y = None
            for a, k in _lag_taps(b, K):
                start = pl.multiple_of(r0 + CONV_HALO - SUBLANES * (a + 1), SUBLANES)
                t = buf[pl.ds(start, CONV_ROWS + SUBLANES), lanes] * w_ref[pl.ds(k, 1), lanes]
                y = t if y is None else y + t
            acc = acc + y[SUBLANES - b:SUBLANES - b + CONV_ROWS]
        out_ref[pl.ds(r0, CONV_ROWS), lanes] = acc
    _conv_chunks(tm, D, chunk)


def _conv_anticausal(dbuf, w_ref, out_ref, tm, D, K):
    def chunk(r0, lanes):
        acc = jnp.zeros((CONV_ROWS, CONV_LANES), F32)
        for b in range(SUBLANES):
            y = None
            for a, k in _lag_taps(b, K):
                start = pl.multiple_of(r0 + SUBLANES * a, SUBLANES)
                t = dbuf[pl.ds(start, CONV_ROWS + SUBLANES), lanes] * w_ref[pl.ds(k, 1), lanes]
                y = t if y is None else y + t
            acc = acc + y[b:b + CONV_ROWS]
        out_ref[pl.ds(r0, CONV_ROWS), lanes] = acc
    _conv_chunks(tm, D, chunk)


def _conv_tap_grads(dbuf, ubuf, acc_ref, tm, D, K):
    reach = SUBLANES * (-(-K // SUBLANES) - 1)

    def chunk(r0, lanes):
        d = dbuf[pl.ds(r0, CONV_ROWS), lanes]
        around = ubuf[pl.ds(pl.multiple_of(r0 + CONV_HALO - reach - SUBLANES, SUBLANES), CONV_ROWS + reach + SUBLANES), lanes]
        for b in range(SUBLANES):
            shifted = around[SUBLANES - b:SUBLANES - b + CONV_ROWS + reach]
            for a, k in _lag_taps(b, K):
                prod = d * shifted[reach - SUBLANES * a:reach - SUBLANES * a + CONV_ROWS]
                part = prod[0:SUBLANES]
                for i in range(1, CONV_ROWS // SUBLANES):
                    part = part + prod[SUBLANES * i:SUBLANES * (i + 1)]
                acc_ref[k, :, lanes] += part
    _conv_chunks(tm, D, chunk)


def mix_out_fwd(x, u0, o, sgc, sgt, dw_w, dw_b, ln_g, ln_b, wcp, wo, wout, name):
    T, D = x.shape
    tm = min(ROW_TILE, T)
    K = dw_w.shape[0]

    def body(x_ref, u_ref, uh_ref, o_ref, sgc_ref, sgt_ref, w_ref, b_ref, lg_ref, lb_ref, wcp_ref, wo_ref, wout_ref,
             x2_ref, u1_ref, co_ref, ao_ref, mg_ref, buf, conv):
        keep = (pl.program_id(0) > 0).astype(F32)
        buf[pl.ds(0, CONV_HALO), :] = uh_ref[...].astype(F32) * keep
        buf[pl.ds(CONV_HALO, tm), :] = u_ref[...].astype(F32)
        _conv_causal(buf, w_ref, b_ref, conv, tm, D, K)
        acc = conv[...]
        u1_ref[...] = acc.astype(BF16)
        xhat, _ = _ln_stats(acc)
        u2 = xhat * lg_ref[...] + lb_ref[...]
        u3 = (u2 * _sigmoid(u2)).astype(BF16)
        co = _dot(u3, wcp_ref[...])
        ao = _dot(o_ref[...], wo_ref[...])
        co_ref[...] = co.astype(BF16)
        ao_ref[...] = ao.astype(BF16)
        merged = (sgc_ref[...].astype(F32) * co + sgt_ref[...].astype(F32) * ao).astype(BF16)
        mg_ref[...] = merged
        x2_ref[...] = x_ref[...] + _dot(merged, wout_ref[...])

    big = jax.ShapeDtypeStruct((T, D), BF16)
    vec = _resident((1, D))
    return pl.pallas_call(
        body, name=name, grid=(T // tm,),
        out_shape=(jax.ShapeDtypeStruct((T, D), F32), big, big, big, big),
        in_specs=[_rows(tm, D), _rows(tm, D), _halo_rows_spec(tm, D, CONV_HALO), _rows(tm, D), _rows(tm, D), _rows(tm, D),
                  _resident((K, D)), vec, vec, vec, _resident((D, D)), _resident((D, D)), _resident((D, D))],
        out_specs=(_rows(tm, D),) * 5,
        scratch_shapes=[pltpu.VMEM((CONV_HALO + tm, D), F32), pltpu.VMEM((tm, D), F32)],
        compiler_params=_params("arbitrary"),
    )(x, u0, u0, o, sgc, sgt, dw_w, dw_b, ln_g, ln_b, wcp, wo, wout)


def mix_out_bwd(dx2, u1, co, ao, sgc, sgt, merged, ln_g, ln_b, wcp, wo, wout, name, carry=None):
    T, D = dx2.shape
    tm = min(ROW_TILE, T)
    nt = T // tm

    def body(dx_ref, u1_ref, co_ref, ao_ref, sgc_ref, sgt_ref, mg_ref, lg_ref, lb_ref, wcp_ref, wo_ref, wout_ref,
             dgc_ref, dgt_ref, do_ref, du1_ref, dao_ref, sums_ref, gwout_ref, gwcp_ref, acc_out, acc_cp):
        i = pl.program_id(0)

        @pl.when(i == 0)
        def _():
            sums_ref[...] = jnp.zeros_like(sums_ref)
            acc_out[...] = jnp.zeros_like(acc_out)
            acc_cp[...] = jnp.zeros_like(acc_cp)

        dx16 = dx_ref[...].astype(BF16)
        acc_out[...] += _tn(mg_ref[...], dx16)
        dm = _nt(dx16, wout_ref[...])
        sgc, sgt = sgc_ref[...].astype(F32), sgt_ref[...].astype(F32)
        dco = (dm * sgc).astype(BF16)
        dao = (dm * sgt).astype(BF16)
        dgc = dm * co_ref[...].astype(F32) * sgc * (1.0 - sgc)
        dgt = dm * ao_ref[...].astype(F32) * sgt * (1.0 - sgt)
        dao_ref[...] = dao
        dgc_ref[...] = dgc.astype(BF16)
        dgt_ref[...] = dgt.astype(BF16)
        do_ref[...] = _nt(dao, wo_ref[...]).astype(BF16)
        du3 = _nt(dco, wcp_ref[...])
        xhat, rstd = _ln_stats(u1_ref[...].astype(F32))
        g = lg_ref[...]
        u2 = xhat * g + lb_ref[...]
        su = _sigmoid(u2)
        acc_cp[...] += _tn((u2 * su).astype(BF16), dco)
        du2 = du3 * (su * (1.0 + u2 * (1.0 - su)))
        dxh = du2 * g
        du1 = rstd * (dxh - jnp.mean(dxh, axis=-1, keepdims=True) - xhat * jnp.mean(dxh * xhat, axis=-1, keepdims=True))
        du1_ref[...] = du1.astype(BF16)
        for r, val in enumerate((dgc, dgt, du2 * xhat, du2, du1)):
            sums_ref[pl.ds(r, 1), :] += jnp.sum(val, axis=0, keepdims=True)

        @pl.when(i == nt - 1)
        def _():
            gwout_ref[...] = acc_out[...].astype(BF16)
            gwcp_ref[...] = acc_cp[...].astype(BF16)

    big = jax.ShapeDtypeStruct((T, D), BF16)
    square = jax.ShapeDtypeStruct((D, D), BF16)
    vec = _resident((1, D))
    return _call(
        body, name=name, grid=(nt,),
        out_shape=(big,) * 5 + (jax.ShapeDtypeStruct((8, D), F32), square, square),
        in_specs=[_rows(tm, D)] * 7 + [vec, vec, _resident((D, D)), _resident((D, D)), _resident((D, D))],
        out_specs=(_rows(tm, D),) * 5 + (_acc_spec((8, D)), _resident((D, D)), _resident((D, D))),
        scratch_shapes=[pltpu.VMEM((D, D), F32), pltpu.VMEM((D, D), F32)],
        args=(dx2, u1, co, ao, sgc, sgt, merged, ln_g, ln_b, wcp, wo, wout), carry=carry)


def conv_bwd(du1, u0, ga, gb, dw_w, name, carry=None):
    T, D = du1.shape
    tm = min(ROW_TILE, T)
    nt = T // tm
    K = dw_w.shape[0]
    per = tm // CONV_HALO

    def body(d_ref, dn_ref, u_ref, uh_ref, ga_ref, gb_ref, w_ref, dga_ref, dgb_ref, dw_ref, dbuf, ubuf, du0_buf, taps):
        i = pl.program_id(0)
        dbuf[pl.ds(0, tm), :] = d_ref[...].astype(F32)
        dbuf[pl.ds(tm, CONV_HALO), :] = dn_ref[...].astype(F32) * (i < nt - 1).astype(F32)
        ubuf[pl.ds(0, CONV_HALO), :] = uh_ref[...].astype(F32) * (i > 0).astype(F32)
        ubuf[pl.ds(CONV_HALO, tm), :] = u_ref[...].astype(F32)

        @pl.when(i == 0)
        def _():
            taps[...] = jnp.zeros_like(taps)

        _conv_anticausal(dbuf, w_ref, du0_buf, tm, D, K)
        _conv_tap_grads(dbuf, ubuf, taps, tm, D, K)
        du0 = du0_buf[...]
        ga, gb = ga_ref[...].astype(F32), gb_ref[...].astype(F32)
        sg = _sigmoid(gb)
        dga_ref[...] = (du0 * sg).astype(BF16)
        dgb_ref[...] = (du0 * ga * sg * (1.0 - sg)).astype(BF16)

        @pl.when(i == nt - 1)
        def _():
            for k in range(K):
                dw_ref[pl.ds(k, 1), :] = jnp.sum(taps[k], axis=0, keepdims=True)

    nxt = pl.BlockSpec((CONV_HALO, D), lambda i: (jnp.minimum((i + 1) * per, nt * per - 1), 0))
    big = jax.ShapeDtypeStruct((T, D), BF16)
    return _call(
        body, name=name, grid=(nt,),
        out_shape=(big, big, jax.ShapeDtypeStruct((K, D), F32)),
        in_specs=[_rows(tm, D), nxt, _rows(tm, D), _halo_rows_spec(tm, D, CONV_HALO), _rows(tm, D), _rows(tm, D),
                  _resident((K, D))],
        out_specs=(_rows(tm, D), _rows(tm, D), _acc_spec((K, D))),
        scratch_shapes=[pltpu.VMEM((tm + CONV_HALO, D), F32), pltpu.VMEM((CONV_HALO + tm, D), F32), pltpu.VMEM((tm, D), F32),
                        pltpu.VMEM((K, SUBLANES, D), F32)],
        args=(du1, du1, u0, u0, ga, gb, dw_w), carry=carry)


def mix_in_bwd(dx2, x, gain, wint, wq, pieces, name):
    T, D = x.shape
    tm = min(ROW_TILE, T)
    widths = [p.shape[1] for p in pieces]
    offs = _w_in_rows(D)

    def body(dx2_ref, x_ref, g_ref, w_ref, wq_ref, *rest):
        piece_refs, (dx_ref, dg_ref) = rest[:len(pieces)], rest[len(pieces):]
        dh = None
        for n, (p_ref, off, w) in enumerate(zip(piece_refs, offs, widths)):
            t = _dot(p_ref[...], wq_ref[...] if n == 2 else w_ref[pl.ds(off, w), :])
            dh = t if dh is None else dh + t
        dx, dgt = _rms_bwd(x_ref[...], g_ref[...], dh)
        dx_ref[...] = dx2_ref[...] + dx

        @pl.when(pl.program_id(0) == 0)
        def _():
            dg_ref[...] = jnp.zeros_like(dg_ref)
        dg_ref[...] += jnp.sum(dgt, axis=0, keepdims=True)

    return pl.pallas_call(
        body, name=name, grid=(T // tm,),
        out_shape=(jax.ShapeDtypeStruct((T, D), F32), jax.ShapeDtypeStruct((1, D), F32)),
        in_specs=[_rows(tm, D), _rows(tm, D), _resident((1, D)), _resident(wint.shape), _resident(wq.shape)]
        + [_rows(tm, w) for w in widths],
        out_specs=(_rows(tm, D), _acc_spec((1, D))),
        compiler_params=_params("arbitrary"),
    )(dx2, x, gain, wint, wq, *pieces)


def rope_tables(positions):
    half = HEAD_DIM // 2
    inv_freq = ROPE_THETA ** (-jnp.arange(half, dtype=F32) / half)
    ang = positions.astype(F32)[:, None] * inv_freq
    cos, sin = jnp.cos(ang), jnp.sin(ang)
    reps = LANES // HEAD_DIM
    return jnp.tile(jnp.concatenate([cos, cos], axis=-1), (1, reps)), jnp.tile(jnp.concatenate([-sin, sin], axis=-1), (1, reps))


def _place():
    return lax.axis_index("x"), lax.axis_index("y"), lax.axis_index("c")


def all_gather(blocks, name):
    n = len(blocks)
    send = gather_send(blocks)
    forward = gather_forward(send.out_shapes)
    n_sems = len(send.sems)

    def body(*refs):
        ins, outs, sems = refs[:n], refs[n:2 * n], refs[2 * n:]
        sent, landing, mine = send.copies(ins, outs, sems[:n_sems])
        passed, relanding = forward.copies((), outs, sems[n_sems:])
        for cp in mine + sent:
            cp.start()
        for j in range(3):
            for i in range(n):
                landing[4 * i + 1 + j].wait_recv()
                passed[3 * i + j].start()
        for i in range(n):
            landing[4 * i].wait_recv()
        for cp in relanding:
            cp.wait_recv()
        for cp in sent + passed:
            cp.wait_send()
        for cp in mine:
            cp.wait()

    return pl.pallas_call(
        body, name=name, out_shape=tuple(send.out_shapes), in_specs=[_ANY] * n, out_specs=(_ANY,) * n,
        scratch_shapes=list(send.sems) + list(forward.sems),
    )(*blocks)


def _chips_across(x, y):
    return [(1 - x, y), (x, 1 - y), (1 - x, 1 - y)]


def _dev_index(x, y, c):
    return 4 * x + 2 * y + c


def gather_send(blocks):
    n = len(blocks)

    def copies(in_refs, out_refs, sems):
        send, recv, local = sems
        x, y, c = _place()
        targets = [(x, y, 1 - c)] + [(*chip, c) for chip in _chips_across(x, y)]
        outgoing, incoming, mine = [], [], []
        for i, (x_ref, out_ref) in enumerate(zip(in_refs, out_refs)):
            for k, t in enumerate(targets):
                pair = dict(send_sem=send.at[4 * i + k], recv_sem=recv.at[4 * i + k], device_id=t, device_id_type=MESH)
                outgoing.append(pltpu.make_async_remote_copy(src_ref=x_ref, dst_ref=out_ref.at[_dev_index(x, y, c)], **pair))
                incoming.append(pltpu.make_async_remote_copy(src_ref=x_ref, dst_ref=out_ref.at[_dev_index(*t)], **pair))
            mine.append(pltpu.make_async_copy(x_ref, out_ref.at[_dev_index(x, y, c)], local.at[i]))
        return outgoing, incoming, mine

    def start(*refs):
        outgoing, _, mine = copies(*refs)
        for cp in mine + outgoing:
            cp.start()

    def finish(*refs):
        outgoing, incoming, mine = copies(*refs)
        for cp in incoming:
            cp.wait_recv()
        for cp in outgoing:
            cp.wait_send()
        for cp in mine:
            cp.wait()

    return Carry(ins=tuple(blocks), out_shapes=tuple(jax.ShapeDtypeStruct((N_DEV,) + b.shape, b.dtype) for b in blocks),
                 aliases={}, sems=(pltpu.SemaphoreType.DMA((4 * n,)), pltpu.SemaphoreType.DMA((4 * n,)),
                                   pltpu.SemaphoreType.DMA((n,))), start=start, finish=finish, copies=copies)


def gather_forward(gathered):
    n = len(gathered)

    def copies(in_refs, out_refs, sems):
        send, recv = sems
        x, y, c = _place()
        outgoing, incoming = [], []
        for i, buf in enumerate(out_refs):
            for k, chip in enumerate(_chips_across(x, y)):
                pair = dict(send_sem=send.at[3 * i + k], recv_sem=recv.at[3 * i + k], device_id=(x, y, 1 - c),
                            device_id_type=MESH)
                rows = buf.at[_dev_index(*chip, c)]
                outgoing.append(pltpu.make_async_remote_copy(src_ref=rows, dst_ref=rows, **pair))
                theirs = buf.at[_dev_index(*chip, 1 - c)]
                incoming.append(pltpu.make_async_remote_copy(src_ref=theirs, dst_ref=theirs, **pair))
        return outgoing, incoming

    def start(*refs):
        for cp in copies(*refs)[0]:
            cp.start()

    def finish(*refs):
        outgoing, incoming = copies(*refs)
        for cp in incoming:
            cp.wait_recv()
        for cp in outgoing:
            cp.wait_send()

    return Carry(ins=tuple(gathered), out_shapes=tuple(jax.ShapeDtypeStruct(g.shape, g.dtype) for g in gathered),
                 aliases={i: i for i in range(n)},
                 sems=(pltpu.SemaphoreType.DMA((3 * n,)), pltpu.SemaphoreType.DMA((3 * n,))), start=start, finish=finish,
                 copies=copies)


def gather_whole(blocks):
    send = gather_send(blocks)
    forward = gather_forward(send.out_shapes)
    n = len(send.sems)

    def relay(ins, outs, sems):
        send.finish(ins, outs, sems[:n])
        forward.start((), outs, sems[n:])

    return Carry(ins=send.ins, out_shapes=send.out_shapes, aliases={}, sems=send.sems + forward.sems,
                 start=lambda ins, outs, sems: send.start(ins, outs, sems[:n]), relay=relay,
                 finish=lambda ins, outs, sems: forward.finish((), outs, sems[n:]))


def compose(*carries):
    def split(refs, count):
        out, at = [], 0
        for c in carries:
            n = count(c)
            out.append(refs[at:at + n])
            at += n
        return out

    def each(stage):
        def run(ins, outs, sems):
            parts = zip(carries, split(ins, lambda c: len(c.ins)), split(outs, lambda c: len(c.out_shapes)),
                        split(sems, lambda c: len(c.sems)))
            for c, i, o, s in parts:
                if getattr(c, stage) is not None:
                    getattr(c, stage)(i, o, s)
        return run

    aliases, n_in, n_out = {}, 0, 0
    for c in carries:
        aliases.update({n_in + i: n_out + o for i, o in c.aliases.items()})
        n_in += len(c.ins)
        n_out += len(c.out_shapes)
    return Carry(ins=sum((tuple(c.ins) for c in carries), ()), out_shapes=sum((tuple(c.out_shapes) for c in carries), ()),
                 aliases=aliases, sems=sum((tuple(c.sems) for c in carries), ()), start=each("start"), finish=each("finish"),
                 relay=each("relay") if any(c.relay is not None for c in carries) else None)


def swap_halves(by_core):
    n = len(by_core)

    def copies(in_refs, out_refs, sems):
        send, recv = sems
        x, y, c = _place()
        return [pltpu.make_async_remote_copy(src_ref=a.at[:, 1 - c], dst_ref=r, send_sem=send.at[i], recv_sem=recv.at[i],
                                             device_id=(x, y, 1 - c), device_id_type=MESH)
                for i, (a, r) in enumerate(zip(in_refs, out_refs))]

    def start(*refs):
        for cp in copies(*refs):
            cp.start()

    def finish(*refs):
        for cp in copies(*refs):
            cp.wait()

    shapes = tuple(jax.ShapeDtypeStruct((a.shape[0],) + a.shape[2:], a.dtype) for a in by_core)
    return Carry(ins=tuple(by_core), out_shapes=shapes, aliases={},
                 sems=(pltpu.SemaphoreType.DMA((n,)), pltpu.SemaphoreType.DMA((n,))), start=start, finish=finish)


def exchange_between_chips(by_chip):
    n = len(by_chip)

    def copies(in_refs, out_refs, sems):
        send, recv = sems
        x, y, c = _place()
        out = []
        for i, (s, r) in enumerate(zip(in_refs, out_refs)):
            for k, (tx, ty) in enumerate(_chips_across(x, y)):
                out.append(pltpu.make_async_remote_copy(
                    src_ref=s.at[2 * tx + ty], dst_ref=r.at[k], send_sem=send.at[3 * i + k], recv_sem=recv.at[3 * i + k],
                    device_id=(tx, ty, c), device_id_type=MESH))
        return out

    def start(*refs):
        for cp in copies(*refs):
            cp.start()

    def finish(*refs):
        for cp in copies(*refs):
            cp.wait()

    shapes = tuple(jax.ShapeDtypeStruct((3,) + a.shape[1:], a.dtype) for a in by_chip)
    return Carry(ins=tuple(by_chip), out_shapes=shapes, aliases={},
                 sems=(pltpu.SemaphoreType.DMA((3 * n,)), pltpu.SemaphoreType.DMA((3 * n,))), start=start, finish=finish)


def run_exchange(carry, name):
    n_in = len(carry.ins)
    n_out = len(carry.out_shapes)

    def body(*refs):
        parts = refs[:n_in], refs[n_in:n_in + n_out], refs[n_in + n_out:]
        carry.start(*parts)
        carry.finish(*parts)

    return pl.pallas_call(
        body, name=name, out_shape=tuple(carry.out_shapes), in_specs=[_ANY] * n_in, out_specs=(_ANY,) * n_out,
        scratch_shapes=list(carry.sems), input_output_aliases=dict(carry.aliases),
    )(*carry.ins)


def pair_sum(my_core, by_core, received, name):
    n = len(by_core)

    def body(core_ref, *refs):
        for a_ref, b_ref, o_ref in zip(refs[:n], refs[n:2 * n], refs[2 * n:]):
            o_ref[0] = (a_ref[0, 0].astype(F32) + b_ref[0].astype(F32)).astype(BF16)

    mine = [pl.BlockSpec((1, 1) + a.shape[2:], lambda j, core: (j, core[0], 0, 0)) for a in by_core]
    theirs = [pl.BlockSpec((1,) + r.shape[1:], lambda j, core: (j, 0, 0)) for r in received]
    return pl.pallas_call(
        body, name=name, out_shape=tuple(jax.ShapeDtypeStruct(r.shape, BF16) for r in received),
        grid_spec=pltpu.PrefetchScalarGridSpec(num_scalar_prefetch=1, grid=(by_core[0].shape[0],), in_specs=mine + theirs,
                                               out_specs=tuple(theirs)),
        compiler_params=_params("arbitrary"),
    )(my_core, *by_core, *received)


def _adamw_math(w, g, m, v):
    m = ADAM_B1 * m + (1.0 - ADAM_B1) * g
    v = ADAM_B2 * v + (1.0 - ADAM_B2) * (g * g)
    m_hat = m / (1.0 - ADAM_B1 ** ADAM_STEP)
    v_hat = v / (1.0 - ADAM_B2 ** ADAM_STEP)
    delta = -ADAM_LR * (m_hat / (jnp.sqrt(v_hat) + ADAM_EPS) + ADAM_WD * w)
    return delta, m, v


def adamw(my_chip, ws, gs, ms, vs, name, carry=None):
    n = len(ws)
    flat, widths = [], []
    for g in gs:
        parts = list(g) if isinstance(g, (tuple, list)) else [g]
        flat += parts
        widths.append(len(parts))
    c_ins = list(carry.ins) if carry else []
    c_outs = list(carry.out_shapes) if carry else []

    def body(chip_ref, *refs):
        w_refs, refs = refs[:n], refs[n:]
        g_refs, refs = refs[:len(flat)], refs[len(flat):]
        m_refs, v_refs, refs = refs[:n], refs[n:2 * n], refs[2 * n:]
        carried = (refs[:len(c_ins)], refs[len(c_ins) + 4 * n:len(c_ins) + 4 * n + len(c_outs)],
                   refs[len(c_ins) + 4 * n + len(c_outs):])
        outs = refs[len(c_ins):]
        if carry:
            @pl.when(pl.program_id(0) == 0)
            def _():
                carry.start(*carried)
        at = 0
        for t in range(n):
            if widths[t] == 1:
                g = g_refs[at][...]
            else:
                g = g_refs[at][0].astype(F32)
                for k in range(3):
                    g = g + g_refs[at + 1][k].astype(F32)
            at += widths[t]
            outs[4 * t][...] = g
            outs[4 * t + 1][...], outs[4 * t + 2][...], outs[4 * t + 3][...] = _adamw_math(
                w_refs[t][...], g, m_refs[t][...], v_refs[t][...])
        if carry:
            @pl.when(pl.program_id(0) == ADAMW_STEPS - 1)
            def _():
                if carry.relay is not None:
                    carry.relay(*carried)
                carry.finish(*carried)

    def rows_of(a):
        return a.shape[-2] // ADAMW_STEPS if a.shape[-2] % (ADAMW_STEPS * SUBLANES) == 0 else None

    def whole(a):
        r = rows_of(a)
        lead = (0,) * (a.ndim - 2)
        if r is None:
            zeros = (0,) * a.ndim
            return pl.BlockSpec(a.shape, lambda i, chip: zeros, pipeline_mode=pl.Buffered(1))
        return pl.BlockSpec(a.shape[:-2] + (r, a.shape[-1]), lambda i, chip: lead + (i, 0))

    g_specs = []
    for g in gs:
        if isinstance(g, (tuple, list)):
            r = rows_of(g[0])
            g_specs += [pl.BlockSpec((1, r, g[0].shape[2]), lambda i, chip: (chip[0], i, 0)), whole(g[1])]
        else:
            g_specs.append(whole(g))
    shapes, out_specs = [], []
    for w in ws:
        shapes += [jax.ShapeDtypeStruct(w.shape, F32)] * 4
        out_specs += [whole(w)] * 4
    res = pl.pallas_call(
        body, name=name, out_shape=tuple(shapes) + tuple(c_outs),
        grid_spec=pltpu.PrefetchScalarGridSpec(
            num_scalar_prefetch=1, grid=(ADAMW_STEPS,),
            in_specs=[whole(w) for w in ws] + g_specs + [whole(a) for a in ms + vs] + [_ANY] * len(c_ins),
            out_specs=tuple(out_specs) + (_ANY,) * len(c_outs), scratch_shapes=list(carry.sems) if carry else []),
        compiler_params=_params("arbitrary"),
    )(my_chip, *ws, *flat, *ms, *vs, *c_ins)
    return [tuple(res[4 * t:4 * t + 4]) for t in range(n)], tuple(res[4 * n:])


def adamw_replicated(w, partials, m, v, name):
    def body(w_ref, p_ref, m_ref, v_ref, g_ref, d_ref, mo_ref, vo_ref):
        g = p_ref[0]
        for k in range(1, N_DEV):
            g = g + p_ref[k]
        g_ref[...] = g
        d_ref[...], mo_ref[...], vo_ref[...] = _adamw_math(w_ref[...], g, m_ref[...], v_ref[...])

    shape = jax.ShapeDtypeStruct(w.shape, F32)
    return pl.pallas_call(body, name=name, out_shape=(shape,) * 4, compiler_params=_params())(w, partials, m, v)


PACK_COLS = 1024
PACK_ROW_ALIGN = 16

REPLICATED = ("ffn1_norm", "mix_norm", "conv_dw_b", "conv_ln_g", "conv_ln_b", "ffn2_norm", "final_norm", "gate_b", "attn_sinks")
WEIGHT_ORDER = ("ffn1_norm", "ffn1_w_gate", "ffn1_w_up", "ffn1_w_down", "mix_norm", "w_in", "conv_dw_w", "conv_dw_b", "conv_ln_g",
                "conv_ln_b", "conv_w_proj", "attn_sinks", "attn_w_o", "gate_b", "w_out", "ffn2_norm", "ffn2_w_gate", "ffn2_w_up",
                "ffn2_w_down", "final_norm")


def _to_rows(flat, lead):
    n = flat.shape[-1]
    rows = -(-n // PACK_COLS)
    flat = jnp.pad(flat, [(0, 0)] * lead + [(0, rows * PACK_COLS - n)])
    return flat.reshape(flat.shape[:lead] + (rows, PACK_COLS))


def _pad_rows(a, axis):
    rows = a.shape[axis]
    pad = -rows % PACK_ROW_ALIGN
    widths = [(0, 0)] * a.ndim
    widths[axis] = (0, pad)
    return jnp.pad(a, widths)


def _from_rows(rows, shape):
    n = 1
    for s in shape:
        n *= s
    return rows.reshape(rows.shape[:-2] + (-1,))[..., :n].reshape(rows.shape[:-2] + tuple(shape))


def _heads_slot_major(rows):
    group = rows.shape[0] // (N_KV_HEADS * HEAD_DIM)
    return rows.reshape(N_KV_HEADS, group, HEAD_DIM, rows.shape[1]).transpose(1, 0, 2, 3).reshape(rows.shape)


def _heads_kv_major(rows):
    group = rows.shape[0] // (N_KV_HEADS * HEAD_DIM)
    return rows.reshape(group, N_KV_HEADS, HEAD_DIM, rows.shape[1]).transpose(1, 0, 2, 3).reshape(rows.shape)


def _by_core(full_rows):
    return full_rows.reshape((N_DEV // 2, 2, full_rows.shape[0] // N_DEV, full_rows.shape[1]))


def kernel(x, positions, ffn1_norm, ffn1_w_gate, ffn1_w_up, ffn1_w_down, mix_norm, w_in, conv_dw_w, conv_dw_b, conv_ln_g, conv_ln_b, conv_w_proj, attn_sinks, attn_w_o, gate_b, w_out, ffn2_norm, ffn2_w_gate, ffn2_w_up, ffn2_w_down, final_norm, loss_target, m_ffn1_norm, m_ffn1_w_gate, m_ffn1_w_up, m_ffn1_w_down, m_mix_norm, m_w_in, m_conv_dw_w, m_conv_dw_b, m_conv_ln_g, m_conv_ln_b, m_conv_w_proj, m_attn_sinks, m_attn_w_o, m_gate_b, m_w_out, m_ffn2_norm, m_ffn2_w_gate, m_ffn2_w_up, m_ffn2_w_down, m_final_norm, v_ffn1_norm, v_ffn1_w_gate, v_ffn1_w_up, v_ffn1_w_down, v_mix_norm, v_w_in, v_conv_dw_w, v_conv_dw_b, v_conv_ln_g, v_conv_ln_b, v_conv_w_proj, v_attn_sinks, v_attn_w_o, v_gate_b, v_w_out, v_ffn2_norm, v_ffn2_w_gate, v_ffn2_w_up, v_ffn2_w_down, v_final_norm):
    given = dict(locals())
    shapes = {n: given[n].shape for n in WEIGHT_ORDER}
    w = {n: given[n].reshape(given[n].shape[-2:]) if given[n].ndim == 3 else given[n].reshape(1, -1) for n in WEIGHT_ORDER}
    m = {n: given["m_" + n].reshape(w[n].shape) for n in WEIGHT_ORDER}
    v = {n: given["v_" + n].reshape(w[n].shape) for n in WEIGHT_ORDER}
    my_x, my_y, my_c = _place()
    my_core = my_c.astype(jnp.int32).reshape(1)
    my_chip = (2 * my_x + my_y).astype(jnp.int32).reshape(1)
    xs, target = x[0], loss_target[0]
    T, D = xs.shape
    KV = N_KV_HEADS * HEAD_DIM
    K = w["conv_dw_w"].shape[0]

    def t16(n):
        return w[n].T.astype(BF16)

    def r16(n):
        return w[n].astype(BF16)

    blocks1 = [t16("ffn1_w_gate"), t16("ffn1_w_up"), r16("ffn1_w_down")]
    dw_bits = _pad_rows(_to_rows(lax.bitcast_convert_type(w["conv_dw_w"], BF16).reshape(-1), 0), 0)
    blocks2 = [t16("w_in"), r16("conv_w_proj"), r16("attn_w_o"), r16("w_out"), dw_bits]
    blocks3 = [t16("ffn2_w_gate"), t16("ffn2_w_up"), r16("ffn2_w_down")]
    cos, sin = rope_tables(positions[0])
    sink_col = jnp.repeat(w["attn_sinks"].reshape(-1), WINDOW).reshape(N_KV_HEADS, (D // KV) * WINDOW, 1)

    def full(gathered):
        return gathered.reshape(-1, gathered.shape[2])

    wgt1, wut1 = (full(g) for g in all_gather(blocks1[:2], "gather_ffn1_up"))
    (h1, a1, b1, s1), got = ffn_up(xs, w["ffn1_norm"], wgt1, wut1, "ffn1_up",
                                   carry=compose(gather_whole(blocks1[2:]), gather_send(blocks2[:1])))
    wd1 = full(got[0])
    (x1,), got = ffn_down(xs, s1, wd1, "ffn1_down", carry=compose(gather_forward(got[1:]), gather_send(blocks2[1:])))
    wint = full(got[0])
    wq = _heads_slot_major(wint[2 * D:3 * D])
    (h2, ga, gb, u0, q, sgc, sgt, kk, vv), got = mix_in_fwd(x1, w["mix_norm"], wint, wq, w["gate_b"], cos, sin, "mix_in_fwd",
                                                            carry=compose(gather_forward(got[1:]), gather_send(blocks3)))
    wcp, wo, wout = (full(g) for g in got[:3])
    wo = _heads_slot_major(wo)
    dw_full = lax.bitcast_convert_type(_from_rows(got[3], w["conv_dw_w"].shape + (2,)), F32)
    dw_full = dw_full.transpose(1, 0, 2).reshape(K, D)
    o, gath3 = attn_fwd(q, kk, vv, sink_col, "attn_fwd", carry=gather_forward(got[4:]))
    x2, u1, co, ao, merged = mix_out_fwd(x1, u0, o, sgc, sgt, dw_full, w["conv_dw_b"], w["conv_ln_g"], w["conv_ln_b"],
                                         wcp, wo, wout, "mix_out_fwd")
    wgt2, wut2, wd2 = (full(g) for g in gath3)
    loss, dx3, d_final, h3, a2, b2, s2 = ffn_fwd_loss(x2, w["ffn2_norm"], wgt2, wut2, wd2, w["final_norm"], target,
                                                      "ffn2_fwd_loss")

    small = {"final_norm": d_final}
    (da2, db2), _ = ffn_bwd_hidden(dx3, a2, b2, wd2, "ffn2_bwd_hidden")
    (dx2, small["ffn2_norm"]), _ = ffn_bwd_input(dx3, x2, w["ffn2_norm"], da2, db2, wgt2, wut2, "ffn2_bwd_input")
    core2 = [_by_core(g) for g in (wgrad(da2, h3, "ffn2_dwg")[0], wgrad(db2, h3, "ffn2_dwu")[0],
                                   wgrad(s2, dx3, "ffn2_dwd", b_scale=FFN_SCALE)[0])]
    (dgc, dgt, do, du1, dao, sums, g_wout, g_wcp), recv2 = mix_out_bwd(
        dx2, u1, co, ao, sgc, sgt, merged, w["conv_ln_g"], w["conv_ln_b"], wcp, wo, wout, "mix_out_bwd",
        carry=swap_halves(core2))
    chip2 = pair_sum(my_core, core2, recv2, "ffn2_grads_pair_sum")
    small["gate_b"] = jnp.concatenate([sums[0:1], sums[1:2]], axis=1)
    small["conv_ln_g"], small["conv_ln_b"], small["conv_dw_b"] = sums[2:3], sums[3:4], sums[4:5]
    g_wo =_heads_kv_major(wgrad(o, dao, "dw_attn_o")[0])
    (dga, dgb, g_dw), got2 = conv_bwd(du1, u0, ga, gb, dw_full, "conv_bwd", carry=exchange_between_chips(chip2))
    dq, dk, dv, dsink = attn_bwd(q, kk, vv, do, sink_col, cos, sin, "attn_bwd")
    small["attn_sinks"] = dsink[:D // KV, :N_KV_HEADS].T.reshape(1, -1)
    pieces = [dga, dgb, dq, dk, dv, dgc, dgt]
    dx1, small["mix_norm"] = mix_in_bwd(dx2, x1, w["mix_norm"], wint, wq, pieces, "mix_in_bwd")
    group = D // KV
    q_moves = [(2 * D + HEAD_DIM * (group * g + hh), 2 * D + HEAD_DIM * (N_KV_HEADS * hh + g), HEAD_DIM)
               for g in range(N_KV_HEADS) for hh in range(group)]
    g_wint = wgrad_stacked(pieces[:3], h2, "dw_in_a", wint.shape[0], 0, moves=[(0, 0, 2 * D)] + q_moves)
    g_wint = wgrad_stacked(pieces[3:], h2, "dw_in_b", wint.shape[0], 3 * D, into=g_wint)
    corem = [_by_core(a) for a in (g_wint, g_wcp, g_wo, g_wout)]

    (da1, db1), recvm = ffn_bwd_hidden(dx1, a1, b1, wd1, "ffn1_bwd_hidden", carry=swap_halves(corem))
    chipm = pair_sum(my_core, corem, recvm, "mix_grads_pair_sum")
    g1c, gotm_a = wgrad(s1, dx1, "ffn1_dwd", carry=exchange_between_chips(chipm[:1]), b_scale=FFN_SCALE)
    g1a, gotm_b = wgrad(da1, h1, "ffn1_dwg", carry=exchange_between_chips(chipm[1:]))
    early = [_by_core(g1a), _by_core(g1c)]
    g1b, recv_early = wgrad(db1, h1, "ffn1_dwu", carry=swap_halves(early))
    (recv_late,) = run_exchange(swap_halves([_by_core(g1b)]), "ffn1_grads_swap")
    core1 = [early[0], _by_core(g1b), early[1]]
    chip1 = pair_sum(my_core, core1, [recv_early[0], recv_late, recv_early[1]], "ffn1_grads_pair_sum")
    (grad_x, small["ffn1_norm"]), got1 = ffn_bwd_input(dx1, xs, w["ffn1_norm"], da1, db1, wgt1, wut1, "ffn1_bwd_input",
                                                       carry=exchange_between_chips(chip1))

    gotm = gotm_a + gotm_b
    grad_src = {"ffn1_w_gate": (chip1[0], got1[0]), "ffn1_w_up": (chip1[1], got1[1]), "ffn1_w_down": (chip1[2], got1[2]),
                "ffn2_w_gate": (chip2[0], got2[0]), "ffn2_w_up": (chip2[1], got2[1]), "ffn2_w_down": (chip2[2], got2[2]),
                "w_in": (chipm[0], gotm[0]), "conv_w_proj": (chipm[1], gotm[1]), "attn_w_o": (chipm[2], gotm[2]),
                "w_out": (chipm[3], gotm[3])}
    grads = {}

    def pack_small(d, taps, extra):
        rows = [_to_rows(d[n].reshape(-1), 0) for n in REPLICATED] + [taps, _to_rows(extra.reshape(-1), 0)]
        return _pad_rows(jnp.concatenate(rows, axis=0), 0)

    def like(a, ref):
        return a if a.shape == ref.shape else a.T

    def adamw_group(names, name, carry=None):
        refs = [grad_src[n][0][0] if isinstance(grad_src[n], tuple) else grad_src[n] for n in names]
        res, carried = adamw(my_chip, [like(w[n], r) for n, r in zip(names, refs)], [grad_src[n] for n in names],
                             [like(m[n], r) for n, r in zip(names, refs)], [like(v[n], r) for n, r in zip(names, refs)], name,
                             carry=carry)
        for n, outs in zip(names, res):
            grads[n], delta[n], new_m[n], new_v[n] = (like(a, w[n]) for a in outs)
        return carried

    delta, new_m, new_v = {}, {}, {}
    zero, no_taps = jnp.zeros((1, LANES), F32), jnp.zeros((K, D), F32)
    (shares,) = adamw_group(("ffn2_w_gate", "ffn2_w_up", "ffn2_w_down"), "adamw_ffn2",
                            carry=gather_whole([pack_small(small, g_dw, loss)]))
    g_s, d_s, m_s, v_s = adamw_replicated(pack_small(w, no_taps, zero), shares, pack_small(m, no_taps, zero),
                                          pack_small(v, no_taps, zero), "adamw_replicated")
    off = 0
    for n in REPLICATED:
        r = -(-w[n].shape[1] // PACK_COLS)
        grads[n], delta[n], new_m[n], new_v[n] = (_from_rows(a[off:off + r], w[n].shape) for a in (g_s, d_s, m_s, v_s))
        off += r
    shard_cols = w["conv_dw_w"].shape[1]
    grad_src["conv_dw_w"] = lax.dynamic_slice_in_dim(g_s[off:off + K], _dev_index(my_x, my_y, my_c) * shard_cols, shard_cols,
                                                    axis=1)
    total_loss = g_s[off + K, 0]
    adamw_group(("ffn1_w_gate", "ffn1_w_up", "ffn1_w_down"), "adamw_ffn1")
    adamw_group(("w_in", "conv_dw_w", "conv_w_proj", "attn_w_o", "w_out"), "adamw_mix")

    out = [total_loss, grad_x[None]]
    for d in (grads, delta, new_m, new_v):
        out += [d[n].reshape(shapes[n]) for n in WEIGHT_ORDER]
    return tuple(out)
```

```python
import functools
from typing import Callable, NamedTuple

import jax
import jax.numpy as jnp
from jax import lax
from jax.experimental import pallas as pl
from jax.experimental.pallas import tpu as pltpu

F32, BF16 = jnp.float32, jnp.bfloat16

HEAD_DIM = 64
N_KV_HEADS = 4
WINDOW = 128
ROPE_THETA = 10000.0
EPS = 1e-6
LN_EPS = 1e-5
NEG_INF = -1e30
ADAM_LR, ADAM_B1, ADAM_B2, ADAM_EPS, ADAM_WD, ADAM_STEP = 0.001, 0.9, 0.999, 1e-08, 0.01, 10
ADAMW_STEPS = 4

N_DEV = 8
LANES = 128
SUBLANES = 8
CONV_HALO = 32
CONV_ROWS, CONV_LANES = 128, 128
ROW_TILE = 512
FFN_CHUNK = 256
FFN_SCALE = 0.5
RING_SLOTS = 3
WGRAD_TILE_ELEMS = 2 ** 22
WGRAD_TILE_ROWS = 2048
WGRAD_VMEM = 40 * 2 ** 20
VMEM_LIMIT = 56 * 2 ** 20
MESH = pl.DeviceIdType.MESH


def _params(*sem):
    return pltpu.CompilerParams(dimension_semantics=sem or None, vmem_limit_bytes=VMEM_LIMIT)


def _resident(shape):
    zeros = (0,) * len(shape)
    return pl.BlockSpec(shape, lambda *_: zeros, pipeline_mode=pl.Buffered(1))


def _rows(tm, n):
    return pl.BlockSpec((tm, n), lambda i: (i, 0))


def _acc_spec(shape):
    zeros = (0,) * len(shape)
    return pl.BlockSpec(shape, lambda *_: zeros)


_ANY = pl.BlockSpec(memory_space=pl.ANY)


class Carry(NamedTuple):
    ins: tuple
    out_shapes: tuple
    aliases: dict
    sems: tuple
    start: Callable
    finish: Callable
    relay: Callable = None
    copies: Callable = None


def _call(body, *, name, grid, in_specs, out_specs, out_shape, args, scratch_shapes=(), carry=None):
    n_in, n_out, n_scr = len(in_specs), len(out_specs), len(scratch_shapes)
    params = _params(*(("arbitrary",) * len(grid)))
    if carry is None:
        res = pl.pallas_call(body, name=name, grid=grid, in_specs=list(in_specs), out_specs=tuple(out_specs),
                             out_shape=tuple(out_shape), scratch_shapes=list(scratch_shapes), compiler_params=params)(*args)
        return tuple(res), ()
    c_in, c_out = len(carry.ins), len(carry.out_shapes)

    def wrapped(*refs):
        ins, c_ins = refs[:n_in], refs[n_in:n_in + c_in]
        p = n_in + c_in
        outs, c_outs = refs[p:p + n_out], refs[p + n_out:p + n_out + c_out]
        p += n_out + c_out
        scr, c_sems = refs[p:p + n_scr], refs[p + n_scr:]
        ids = [pl.program_id(d) for d in range(len(grid))]
        first = functools.reduce(jnp.logical_and, [i == 0 for i in ids])
        last = functools.reduce(jnp.logical_and, [i == n - 1 for i, n in zip(ids, grid)])

        @pl.when(first)
        def _():
            carry.start(c_ins, c_outs, c_sems)

        body(*ins, *outs, *scr)

        if carry.relay is not None:
            @pl.when(ids[0] == (3 * grid[0]) // 4)
            def _():
                carry.relay(c_ins, c_outs, c_sems)

        @pl.when(last)
        def _():
            carry.finish(c_ins, c_outs, c_sems)

    res = pl.pallas_call(
        wrapped, name=name, grid=grid, in_specs=list(in_specs) + [_ANY] * c_in, out_specs=tuple(out_specs) + (_ANY,) * c_out,
        out_shape=tuple(out_shape) + tuple(carry.out_shapes), scratch_shapes=list(scratch_shapes) + list(carry.sems),
        input_output_aliases={n_in + i: n_out + o for i, o in carry.aliases.items()}, compiler_params=params,
    )(*args, *carry.ins)
    return tuple(res[:n_out]), tuple(res[n_out:])


def _nt(a, b):
    return lax.dot_general(a, b, (((1,), (1,)), ((), ())), preferred_element_type=F32)


def _tn(a, b):
    return lax.dot_general(a, b, (((0,), (0,)), ((), ())), preferred_element_type=F32)


def _dot(a, b):
    return jnp.dot(a, b, preferred_element_type=F32)


def _sigmoid(x):
    return 1.0 / (1.0 + jnp.exp(-x))


def _rms_fwd(x, g):
    r = lax.rsqrt(jnp.mean(x * x, axis=-1, keepdims=True) + EPS)
    return (x * r) * g


def _rms_bwd(x, g, dy):
    r = lax.rsqrt(jnp.mean(x * x, axis=-1, keepdims=True) + EPS)
    xhat = x * r
    dyg = dy * g
    dx = r * (dyg - xhat * jnp.mean(dyg * xhat, axis=-1, keepdims=True))
    return dx, dy * xhat


def _rot_half(x):
    lane = lax.broadcasted_iota(jnp.int32, (x.shape[0], LANES), 1)
    first = (lane % HEAD_DIM) < (HEAD_DIM // 2)
    out = []
    for s in range(x.shape[1] // LANES):
        xs = x[:, LANES * s:LANES * (s + 1)]
        out.append(jnp.where(first, pltpu.roll(xs, LANES - HEAD_DIM // 2, 1), pltpu.roll(xs, HEAD_DIM // 2, 1)))
    return out[0] if len(out) == 1 else jnp.concatenate(out, axis=1)


def _tile_lanes(t, width):
    return t if width == LANES else jnp.concatenate([t] * (width // LANES), axis=1)


def _rope_fwd(x, cos, sin_signed):
    w = x.shape[1]
    return x * _tile_lanes(cos, w) + _rot_half(x) * _tile_lanes(sin_signed, w)


def _rope_bwd(dy, cos, sin_signed):
    w = dy.shape[1]
    return dy * _tile_lanes(cos, w) + _rot_half(dy * _tile_lanes(sin_signed, w))


def _ffn_rows(x, g_ref, wg_ref, wu_ref, wd_ref, h_ref, a_ref, b_ref, s_ref, acc_ref):
    F = wg_ref.shape[0]
    h = _rms_fwd(x, g_ref[...]).astype(BF16)
    h_ref[...] = h
    for c in range(F // FFN_CHUNK):
        cs = pl.ds(c * FFN_CHUNK, FFN_CHUNK)
        a = _nt(h, wg_ref[cs, :])
        b = _nt(h, wu_ref[cs, :])
        a_ref[:, cs] = a.astype(BF16)
        b_ref[:, cs] = b.astype(BF16)
        s = (a * _sigmoid(a) * b).astype(BF16)
        s_ref[:, cs] = s
        y = _dot(s, wd_ref[cs, :])
        if c == 0:
            acc_ref[...] = y
        else:
            acc_ref[...] += y
    return x + FFN_SCALE * acc_ref[...]


def ffn_up(x, gain, wgt, wut, name, carry=None):
    T, D = x.shape
    F = wgt.shape[0]
    tm = min(ROW_TILE, T)

    def body(x_ref, g_ref, wg_ref, wu_ref, h_ref, a_ref, b_ref, s_ref):
        h = _rms_fwd(x_ref[...], g_ref[...]).astype(BF16)
        h_ref[...] = h
        for c in range(F // FFN_CHUNK):
            cs = pl.ds(c * FFN_CHUNK, FFN_CHUNK)
            a = _nt(h, wg_ref[cs, :])
            b = _nt(h, wu_ref[cs, :])
            a_ref[:, cs] = a.astype(BF16)
            b_ref[:, cs] = b.astype(BF16)
            s_ref[:, cs] = (a * _sigmoid(a) * b).astype(BF16)

    wide = jax.ShapeDtypeStruct((T, F), BF16)
    return _call(
        body, name=name, grid=(T // tm,), out_shape=(jax.ShapeDtypeStruct((T, D), BF16), wide, wide, wide),
        in_specs=[_rows(tm, D), _resident((1, D)), _resident((F, D)), _resident((F, D))],
        out_specs=(_rows(tm, D), _rows(tm, F), _rows(tm, F), _rows(tm, F)), args=(x, gain, wgt, wut), carry=carry)


def ffn_down(x, s, wd, name, carry=None):
    T, D = x.shape
    F = wd.shape[0]
    tm = min(2 * ROW_TILE, T)

    def body(x_ref, s_ref, wd_ref, xo_ref):
        xo_ref[...] = x_ref[...] + FFN_SCALE * _dot(s_ref[...], wd_ref[...])

    return _call(
        body, name=name, grid=(T // tm,), out_shape=(jax.ShapeDtypeStruct((T, D), F32),),
        in_specs=[_rows(tm, D), _rows(tm, F), _resident((F, D))], out_specs=(_rows(tm, D),), args=(x, s, wd), carry=carry)


def ffn_fwd_loss(x, gain, wgt, wut, wd, final_gain, target, name):
    T, D = x.shape
    F = wgt.shape[0]
    tm = min(ROW_TILE, T)

    def body(x_ref, g_ref, wg_ref, wu_ref, wd_ref, gf_ref, t_ref,
             loss_ref, dx_ref, dg_ref, h_ref, a_ref, b_ref, s_ref, acc_ref):
        xo = _ffn_rows(x_ref[...], g_ref, wg_ref, wu_ref, wd_ref, h_ref, a_ref, b_ref, s_ref, acc_ref)
        gf = gf_ref[...]
        err = _rms_fwd(xo, gf) - t_ref[...]
        dx, dgt = _rms_bwd(xo, gf, err * (1.0 / D))
        dx_ref[...] = dx

        @pl.when(pl.program_id(0) == 0)
        def _():
            dg_ref[...] = jnp.zeros_like(dg_ref)
            loss_ref[...] = jnp.zeros_like(loss_ref)
        dg_ref[...] += jnp.sum(dgt, axis=0, keepdims=True)
        per_token = jnp.sum(err * err, axis=-1, keepdims=True) * (0.5 / D)
        loss_ref[...] += jnp.broadcast_to(jnp.sum(per_token, axis=0, keepdims=True), (1, LANES))

    return pl.pallas_call(
        body, name=name, grid=(T // tm,),
        out_shape=(jax.ShapeDtypeStruct((1, LANES), F32), jax.ShapeDtypeStruct((T, D), F32), jax.ShapeDtypeStruct((1, D), F32),
                   jax.ShapeDtypeStruct((T, D), BF16), jax.ShapeDtypeStruct((T, F), BF16), jax.ShapeDtypeStruct((T, F), BF16),
                   jax.ShapeDtypeStruct((T, F), BF16)),
        in_specs=[_rows(tm, D), _resident((1, D)), _resident((F, D)), _resident((F, D)), _resident((F, D)), _resident((1, D)),
                  _rows(tm, D)],
        out_specs=(_acc_spec((1, LANES)), _rows(tm, D), _acc_spec((1, D)), _rows(tm, D), _rows(tm, F), _rows(tm, F),
                   _rows(tm, F)),
        scratch_shapes=[pltpu.VMEM((tm, D), F32)], compiler_params=_params("arbitrary"),
    )(x, gain, wgt, wut, wd, final_gain, target)


def ffn_bwd_hidden(dxo, a, b, wd, name, carry=None):
    T, D = dxo.shape
    F = wd.shape[0]
    tm = min(ROW_TILE, T)
    nt = T // tm
    fc = FFN_CHUNK

    def hidden_body(dxo_ref, a_hbm, b_hbm, wd_ref, da_ref, db_ref, a_buf, b_buf, a_sems, b_sems):
        i = pl.program_id(0)

        def fetch(step, slot):
            rows = pl.ds(pl.multiple_of(step * tm, tm), tm)
            return (pltpu.make_async_copy(a_hbm.at[rows, :], a_buf.at[slot], a_sems.at[slot]),
                    pltpu.make_async_copy(b_hbm.at[rows, :], b_buf.at[slot], b_sems.at[slot]))

        @pl.when(i == 0)
        def _():
            for step in range(min(RING_SLOTS - 1, nt)):
                for cp in fetch(step, step):
                    cp.start()

        @pl.when(i + RING_SLOTS - 1 < nt)
        def _():
            for cp in fetch(i + RING_SLOTS - 1, (i + RING_SLOTS - 1) % RING_SLOTS):
                cp.start()

        slot = i % RING_SLOTS
        for cp in fetch(i, slot):
            cp.wait()
        g0 = (FFN_SCALE * dxo_ref[...]).astype(BF16)
        for c in range(F // fc):
            cs = pl.ds(c * fc, fc)
            ds = _nt(g0, wd_ref[cs, :])
            a = a_buf[slot, :, cs].astype(F32)
            bb = b_buf[slot, :, cs].astype(F32)
            sa = _sigmoid(a)
            da_ref[:, cs] = (ds * bb * (sa * (1.0 + a * (1.0 - sa)))).astype(BF16)
            db_ref[:, cs] = (ds * (a * sa)).astype(BF16)

    wide = jax.ShapeDtypeStruct((T, F), BF16)
    return _call(
        hidden_body, name=name, grid=(nt,), out_shape=(wide, wide),
        in_specs=[_rows(tm, D), _ANY, _ANY, _resident((F, D))],
        out_specs=(_rows(tm, F), _rows(tm, F)),
        scratch_shapes=[pltpu.VMEM((RING_SLOTS, tm, F), BF16), pltpu.VMEM((RING_SLOTS, tm, F), BF16),
                        pltpu.SemaphoreType.DMA((RING_SLOTS,)), pltpu.SemaphoreType.DMA((RING_SLOTS,))],
        args=(dxo, a, b, wd), carry=carry)


def ffn_bwd_input(dxo, x, gain, da, db, wg, wu, name, carry=None):
    T, D = x.shape
    F = wg.shape[0]
    tm = min(ROW_TILE, T)

    def input_body(dxo_ref, x_ref, g_ref, da_ref, db_ref, wg_ref, wu_ref, dx_ref, dg_ref):
        dh = _dot(da_ref[...], wg_ref[...]) + _dot(db_ref[...], wu_ref[...])
        dx, dgt = _rms_bwd(x_ref[...], g_ref[...], dh)
        dx_ref[...] = dxo_ref[...] + dx

        @pl.when(pl.program_id(0) == 0)
        def _():
            dg_ref[...] = jnp.zeros_like(dg_ref)
        dg_ref[...] += jnp.sum(dgt, axis=0, keepdims=True)

    return _call(
        input_body, name=name, grid=(T // tm,),
        out_shape=(jax.ShapeDtypeStruct((T, D), F32), jax.ShapeDtypeStruct((1, D), F32)),
        in_specs=[_rows(tm, D), _rows(tm, D), _resident((1, D)), _rows(tm, F), _rows(tm, F), _resident((F, D)),
                  _resident((F, D))],
        out_specs=(_rows(tm, D), _acc_spec((1, D))), args=(dxo, x, gain, da, db, wg, wu), carry=carry)


def wgrad(a, b, name, carry=None, b_scale=None):
    T, M = a.shape
    N = b.shape[1]
    fixed = M * N * (4 + 2)
    per_row = 2 * (M * a.dtype.itemsize + N * b.dtype.itemsize)
    tk = ROW_TILE
    while fixed + 2 * tk * per_row <= WGRAD_VMEM and 2 * tk <= WGRAD_TILE_ROWS:
        tk *= 2
    tk = min(tk, T)
    nk = T // tk

    def body(a_ref, b_ref, o_ref, acc_ref):
        k = pl.program_id(0)
        bt = b_ref[...] if b_scale is None else b_scale * b_ref[...]
        part = _tn(a_ref[...].astype(BF16), bt.astype(BF16))

        @pl.when(k == 0)
        def _():
            acc_ref[...] = part

        @pl.when(k > 0)
        def _():
            acc_ref[...] += part

        @pl.when(k == nk - 1)
        def _():
            o_ref[...] = acc_ref[...].astype(BF16)

    (out,), carried = _call(
        body, name=name, grid=(nk,), out_shape=(jax.ShapeDtypeStruct((M, N), BF16),),
        in_specs=[_rows(tk, M), _rows(tk, N)], out_specs=(_resident((M, N)),),
        scratch_shapes=[pltpu.VMEM((M, N), F32)], args=(a, b), carry=carry)
    return out, carried


def wgrad_stacked(pieces, b, name, total_rows, at, into=None, moves=None):
    T, N = b.shape
    n = len(pieces)
    widths = [p.shape[1] for p in pieces]
    offs = [sum(widths[:i]) for i in range(n)]
    M = sum(widths)
    tk = ROW_TILE
    while 2 * tk * M <= WGRAD_TILE_ELEMS and 2 * tk <= WGRAD_TILE_ROWS:
        tk *= 2
    tk = min(tk, T)
    nk = T // tk
    moves = moves or [(0, 0, M)]
    n_in = n + 1 + (into is not None)

    def body(*refs):
        a_refs, b_ref = refs[:n], refs[n]
        o_ref, acc_ref, stage, sem = refs[n_in:]
        k = pl.program_id(0)

        @pl.when(k == 0)
        def _():
            acc_ref[...] = jnp.zeros_like(acc_ref)
        bt = b_ref[...].astype(BF16)
        for a_ref, off, width in zip(a_refs, offs, widths):
            acc_ref[pl.ds(off, width), :] += _tn(a_ref[...].astype(BF16), bt)

        @pl.when(k == nk - 1)
        def _():
            for to, start, rows in moves:
                stage[pl.ds(to, rows), :] = acc_ref[pl.ds(start, rows), :].astype(BF16)
            cp = pltpu.make_async_copy(stage, o_ref.at[pl.ds(at, M)], sem)
            cp.start()
            cp.wait()

    return pl.pallas_call(
        body, name=name, grid=(nk,), out_shape=jax.ShapeDtypeStruct((total_rows, N), BF16),
        in_specs=[_rows(tk, width) for width in widths] + [_rows(tk, N)] + [_ANY] * (into is not None), out_specs=_ANY,
        scratch_shapes=[pltpu.VMEM((M, N), F32), pltpu.VMEM((M, N), BF16), pltpu.SemaphoreType.DMA],
        input_output_aliases={n + 1: 0} if into is not None else {}, compiler_params=_params("arbitrary"),
    )(*pieces, b, *([into] if into is not None else []))


def _w_in_rows(D):
    KV = N_KV_HEADS * HEAD_DIM
    return 0, D, 2 * D, 3 * D, 3 * D + KV, 3 * D + 2 * KV, 4 * D + 2 * KV


def mix_in_fwd(x, gain, wint, wq, gate_b, cos, sin_signed, name, carry=None):
    T, D = x.shape
    KV = N_KV_HEADS * HEAD_DIM
    tm = min(ROW_TILE, T)
    o_ga, o_gb, _, o_k, o_v, o_gc, o_gt = _w_in_rows(D)

    def body(x_ref, g_ref, w_ref, wq_ref, gb_ref, cos_ref, sin_ref,
             h_ref, ga_ref, gb_out_ref, u0_ref, q_ref, sgc_ref, sgt_ref, k_ref, v_ref):
        h = _rms_fwd(x_ref[...], g_ref[...]).astype(BF16)
        h_ref[...] = h
        cos, sin = cos_ref[...], sin_ref[...]
        ga = _nt(h, w_ref[pl.ds(o_ga, D), :])
        gb = _nt(h, w_ref[pl.ds(o_gb, D), :])
        ga_ref[...] = ga.astype(BF16)
        gb_out_ref[...] = gb.astype(BF16)
        u0_ref[...] = (ga * _sigmoid(gb)).astype(BF16)
        q = _nt(h, wq_ref[...])
        q_ref[...] = _rope_fwd(q, cos, sin).astype(BF16)
        gc = _nt(h, w_ref[pl.ds(o_gc, D), :]) + gb_ref[:, pl.ds(0, D)]
        sgc_ref[...] = _sigmoid(gc).astype(BF16)
        gt = _nt(h, w_ref[pl.ds(o_gt, D), :]) + gb_ref[:, pl.ds(D, D)]
        sgt_ref[...] = _sigmoid(gt).astype(BF16)
        k = _nt(h, w_ref[pl.ds(o_k, KV), :])
        k_ref[...] = _rope_fwd(k, cos, sin).astype(BF16)
        v_ref[...] = _nt(h, w_ref[pl.ds(o_v, KV), :]).astype(BF16)

    big = jax.ShapeDtypeStruct((T, D), BF16)
    small = jax.ShapeDtypeStruct((T, KV), BF16)
    return _call(
        body, name=name, grid=(T // tm,),
        out_shape=(big, big, big, big, big, big, big, small, small),
        in_specs=[_rows(tm, D), _resident((1, D)), _resident(wint.shape), _resident(wq.shape), _resident((1, 2 * D)),
                  _rows(tm, LANES), _rows(tm, LANES)],
        out_specs=(_rows(tm, D),) * 7 + (_rows(tm, KV),) * 2,
        args=(x, gain, wint, wq, gate_b, cos, sin_signed), carry=carry)


def _from_prev(rows):
    qi = lax.broadcasted_iota(jnp.int32, (rows, WINDOW), 0) % WINDOW
    return lax.broadcasted_iota(jnp.int32, (rows, WINDOW), 1) > qi


def _fold(x, g, from_prev):
    lo = 2 * WINDOW * g
    return jnp.where(from_prev, x[:, lo:lo + WINDOW], x[:, lo + WINDOW:lo + 2 * WINDOW])


def _unfold(folded, from_prev):
    zero = jnp.zeros_like(folded[0])
    parts = []
    for x in folded:
        parts += [jnp.where(from_prev, x, zero), jnp.where(from_prev, zero, x)]
    return jnp.concatenate(parts, axis=1)


def _kv_lane_head(rows, width):
    return lax.broadcasted_iota(jnp.int32, (rows, width), 1) // HEAD_DIM


def _block_diag(win):
    head = _kv_lane_head(*win.shape)
    zero = jnp.zeros_like(win)
    return jnp.concatenate([jnp.where(head == g, win, zero) for g in range(N_KV_HEADS)], axis=0)


def _diag_blocks_sum(bd, keys):
    head = _kv_lane_head(keys, bd.shape[1])
    out = jnp.zeros((keys, bd.shape[1]), F32)
    for g in range(N_KV_HEADS):
        out = jnp.where(head == g, bd[g * keys:(g + 1) * keys], out)
    return out


def _kv_windows(k_ref, kh_ref, v_ref, vh_ref, j):
    rows = pl.ds(j * WINDOW, WINDOW)
    if j == 0:
        kprev, vprev = kh_ref[...], vh_ref[...]
    else:
        prev = pl.ds((j - 1) * WINDOW, WINDOW)
        kprev, vprev = k_ref[prev, :], v_ref[prev, :]
    return jnp.concatenate([kprev, k_ref[rows, :]], axis=0), jnp.concatenate([vprev, v_ref[rows, :]], axis=0)


def _stack_slots(ref, j, group, KV):
    rows = pl.ds(j * WINDOW, WINDOW)
    return jnp.concatenate([ref[rows, pl.ds(KV * hh, KV)] for hh in range(group)], axis=0)


def _attn_exp(qs, kbd, sink_ref, from_prev, no_prev):
    s = _nt(qs, kbd)
    if no_prev is not None:
        qi = lax.broadcasted_iota(jnp.int32, from_prev.shape, 0) % WINDOW
        absent = lax.broadcasted_iota(jnp.int32, from_prev.shape, 1) > jnp.where(no_prev, qi, WINDOW)
    out = []
    for g in range(N_KV_HEADS):
        sg = _fold(s, g, from_prev) * (HEAD_DIM ** -0.5)
        if no_prev is not None:
            sg = jnp.where(absent, NEG_INF, sg)
        sink = sink_ref[g]
        m = jnp.maximum(jnp.max(sg, axis=-1, keepdims=True), sink)
        out.append((jnp.exp(sg - m), jnp.exp(sink - m)))
    return out


def _spread_over_heads(cols, rows, KV):
    head = _kv_lane_head(rows, KV)
    out = jnp.zeros((rows, KV), F32)
    for g, col in enumerate(cols):
        out = jnp.where(head == g, col, out)
    return out


def _halo_rows_spec(tq, width, sub):
    return pl.BlockSpec((sub, width), lambda i: (jnp.maximum(i * (tq // sub) - 1, 0), 0))


def attn_fwd(q, k, v, sink_col, name, carry=None):
    T, D = q.shape
    KV = k.shape[1]
    group = D // KV
    tq = min(ROW_TILE, T)
    nsub = tq // WINDOW
    rows, wide = group * WINDOW, N_KV_HEADS * 2 * WINDOW

    def body(q_ref, k_ref, kh_ref, v_ref, vh_ref, sink_ref, o_ref):
        from_prev = _from_prev(rows)
        head = _kv_lane_head(wide, KV)
        block = lax.broadcasted_iota(jnp.int32, head.shape, 0) // (2 * WINDOW)
        ones_bd = jnp.where(head == block, 1.0, 0.0).astype(BF16)
        for j in range(nsub):
            k_win, v_win = _kv_windows(k_ref, kh_ref, v_ref, vh_ref, j)
            parts = _attn_exp(_stack_slots(q_ref, j, group, KV), _block_diag(k_win), sink_ref, from_prev,
                              pl.program_id(0) == 0 if j == 0 else None)
            p = _unfold([pg.astype(BF16) for pg, _ in parts], from_prev)
            both = _dot(p, jnp.concatenate([_block_diag(v_win), ones_bd], axis=1))
            denom = both[:, KV:] + _spread_over_heads([es for _, es in parts], rows, KV)
            out = (both[:, :KV] / denom).astype(BF16)
            for hh in range(group):
                o_ref[pl.ds(j * WINDOW, WINDOW), pl.ds(KV * hh, KV)] = out[hh * WINDOW:(hh + 1) * WINDOW]

    (o,), carried = _call(
        body, name=name, grid=(T // tq,),
        out_shape=(jax.ShapeDtypeStruct((T, D), BF16),),
        in_specs=[_rows(tq, D), _rows(tq, KV), _halo_rows_spec(tq, KV, WINDOW), _rows(tq, KV),
                  _halo_rows_spec(tq, KV, WINDOW), _resident(sink_col.shape)],
        out_specs=(_rows(tq, D),), args=(q, k, k, v, v, sink_col), carry=carry)
    return o, carried


def attn_bwd(q, k, v, do, sink_col, cos, sin_signed, name):
    T, D = q.shape
    KV = k.shape[1]
    group = D // KV
    tq = min(ROW_TILE, T)
    nsub = tq // WINDOW
    nt = T // tq
    scale = HEAD_DIM ** -0.5
    rows, wide = group * WINDOW, N_KV_HEADS * 2 * WINDOW

    def rev(i):
        return nt - 1 - i

    def body(q_ref, k_ref, kh_ref, v_ref, vh_ref, do_ref, sink_ref, cos_ref, sin_ref,
             dq_ref, dk_ref, dv_ref, dsink_ref, dq_acc, dk_acc, dv_acc, carry_k, carry_v):
        i = pl.program_id(0)

        @pl.when(i == 0)
        def _():
            carry_k[...] = jnp.zeros_like(carry_k)
            carry_v[...] = jnp.zeros_like(carry_v)
            dsink_ref[...] = jnp.zeros_like(dsink_ref)

        dk_acc[...] = jnp.zeros_like(dk_acc)
        dv_acc[...] = jnp.zeros_like(dv_acc)
        from_prev = _from_prev(rows)
        lane = lax.broadcasted_iota(jnp.int32, (1, LANES), 1)
        for j in range(nsub):
            k_win, v_win = _kv_windows(k_ref, kh_ref, v_ref, vh_ref, j)
            kbd, vbd = _block_diag(k_win), _block_diag(v_win)
            qs, dos = _stack_slots(q_ref, j, group, KV), _stack_slots(do_ref, j, group, KV)
            dp = _nt(dos, vbd)
            probs16, ds16 = [], []
            for g, (pg, es) in enumerate(_attn_exp(qs, kbd, sink_ref, from_prev, rev(i) == 0 if j == 0 else None)):
                inv = 1.0 / (jnp.sum(pg, axis=-1, keepdims=True) + es)
                probs = pg * inv
                dpg = _fold(dp, g, from_prev)
                delta = jnp.sum(probs * dpg, axis=-1, keepdims=True)
                probs16.append(probs.astype(BF16))
                ds16.append((probs * (dpg - delta) * scale).astype(BF16))
                dsk = -(es * inv * delta)
                for hh in range(group):
                    tot = jnp.sum(dsk[hh * WINDOW:(hh + 1) * WINDOW], axis=0, keepdims=True)
                    dsink_ref[pl.ds(hh, 1), :] += jnp.where(lane == g, tot, 0.0)
            ds = _unfold(ds16, from_prev)
            dqs = _dot(ds, kbd)
            for hh in range(group):
                dq_acc[pl.ds(j * WINDOW, WINDOW), pl.ds(KV * hh, KV)] = dqs[hh * WINDOW:(hh + 1) * WINDOW]
            keys = pl.ds(j * WINDOW, 2 * WINDOW)
            dk_acc[keys, :] += _diag_blocks_sum(_tn(ds, qs), 2 * WINDOW)
            dv_acc[keys, :] += _diag_blocks_sum(_tn(_unfold(probs16, from_prev), dos), 2 * WINDOW)

        tail = pl.ds(tq, WINDOW)
        dk_acc[tail, :] += carry_k[...]
        dv_acc[tail, :] += carry_v[...]
        carry_k[...] = dk_acc[pl.ds(0, WINDOW), :]
        carry_v[...] = dv_acc[pl.ds(0, WINDOW), :]
        cos, sin = cos_ref[...], sin_ref[...]
        dq_ref[...] = _rope_bwd(dq_acc[...], cos, sin).astype(BF16)
        dk_ref[...] = _rope_bwd(dk_acc[pl.ds(WINDOW, tq), :], cos, sin).astype(BF16)
        dv_ref[...] = dv_acc[pl.ds(WINDOW, tq), :].astype(BF16)

    def rrows(n):
        return pl.BlockSpec((tq, n), lambda i: (rev(i), 0))

    def rhalo(n):
        return pl.BlockSpec((WINDOW, n), lambda i: (jnp.maximum(rev(i) * nsub - 1, 0), 0))

    return pl.pallas_call(
        body, name=name, grid=(nt,),
        out_shape=(jax.ShapeDtypeStruct((T, D), BF16), jax.ShapeDtypeStruct((T, KV), BF16),
                   jax.ShapeDtypeStruct((T, KV), BF16), jax.ShapeDtypeStruct((SUBLANES, LANES), F32)),
        in_specs=[rrows(D), rrows(KV), rhalo(KV), rrows(KV), rhalo(KV), rrows(D), _resident(sink_col.shape),
                  rrows(LANES), rrows(LANES)],
        out_specs=(rrows(D), rrows(KV), rrows(KV), _acc_spec((SUBLANES, LANES))),
        scratch_shapes=[pltpu.VMEM((tq, D), F32), pltpu.VMEM((WINDOW + tq, KV), F32), pltpu.VMEM((WINDOW + tq, KV), F32),
                        pltpu.VMEM((WINDOW, KV), F32), pltpu.VMEM((WINDOW, KV), F32)],
        compiler_params=_params("arbitrary"),
    )(q, k, k, v, v, do, sink_col, cos, sin_signed)


def _ln_stats(u):
    mu = jnp.mean(u, axis=-1, keepdims=True)
    d = u - mu
    rstd = lax.rsqrt(jnp.mean(d * d, axis=-1, keepdims=True) + LN_EPS)
    return d * rstd, rstd


def _lag_taps(b, K):
    return [(a, K - 1 - (SUBLANES * a + b)) for a in range(-(-K // SUBLANES)) if SUBLANES * a + b <= K - 1]


def _conv_chunks(tm, D, chunk):
    def rows(c, carry):
        r0 = pl.multiple_of(c * CONV_ROWS, CONV_ROWS)
        for l0 in range(0, D, CONV_LANES):
            chunk(r0, pl.ds(l0, CONV_LANES))
        return carry
    lax.fori_loop(0, tm // CONV_ROWS, rows, 0)


def _conv_causal(buf, w_ref, bias_ref, out_ref, tm, D, K):
    def chunk(r0, lanes):
        acc = jnp.broadcast_to(bias_ref[:, lanes], (CONV_ROWS, CONV_LANES))
        for b in range(SUBLANES):
            y = None
            for a, k in _lag_taps(b, K):
                start = pl.multiple_of(r0 + CONV_HALO - SUBLANES * (a + 1), SUBLANES)
                t = buf[pl.ds(start, CONV_ROWS + SUBLANES), lanes] * w_ref[pl.ds(k, 1), lanes]
                y = t if y is None else y + t
            acc = acc + y[SUBLANES - b:SUBLANES - b + CONV_ROWS]
        out_ref[pl.ds(r0, CONV_ROWS), lanes] = acc
    _conv_chunks(tm, D, chunk)


def _conv_anticausal(dbuf, w_ref, out_ref, tm, D, K):
    def chunk(r0, lanes):
        acc = jnp.zeros((CONV_ROWS, CONV_LANES), F32)
        for b in range(SUBLANES):
            y = None
            for a, k in _lag_taps(b, K):
                start = pl.multiple_of(r0 + SUBLANES * a, SUBLANES)
                t = dbuf[pl.ds(start, CONV_ROWS + SUBLANES), lanes] * w_ref[pl.ds(k, 1), lanes]
                y = t if y is None else y + t
            acc = acc + y[b:b + CONV_ROWS]
        out_ref[pl.ds(r0, CONV_ROWS), lanes] = acc
    _conv_chunks(tm, D, chunk)


def _conv_tap_grads(dbuf, ubuf, acc_ref, tm, D, K):
    reach = SUBLANES * (-(-K // SUBLANES) - 1)

    def chunk(r0, lanes):
        d = dbuf[pl.ds(r0, CONV_ROWS), lanes]
        around = ubuf[pl.ds(pl.multiple_of(r0 + CONV_HALO - reach - SUBLANES, SUBLANES), CONV_ROWS + reach + SUBLANES), lanes]
        for b in range(SUBLANES):
            shifted = around[SUBLANES - b:SUBLANES - b + CONV_ROWS + reach]
            for a, k in _lag_taps(b, K):
                prod = d * shifted[reach - SUBLANES * a:reach - SUBLANES * a + CONV_ROWS]
                part = prod[0:SUBLANES]
                for i in range(1, CONV_ROWS // SUBLANES):
                    part = part + prod[SUBLANES * i:SUBLANES * (i + 1)]
                acc_ref[k, :, lanes] += part
    _conv_chunks(tm, D, chunk)


def mix_out_fwd(x, u0, o, sgc, sgt, dw_w, dw_b, ln_g, ln_b, wcp, wo, wout, name):
    T, D = x.shape
    tm = min(ROW_TILE, T)
    K = dw_w.shape[0]

    def body(x_ref, u_ref, uh_ref, o_ref, sgc_ref, sgt_ref, w_ref, b_ref, lg_ref, lb_ref, wcp_ref, wo_ref, wout_ref,
             x2_ref, u1_ref, co_ref, ao_ref, mg_ref, buf, conv):
        keep = (pl.program_id(0) > 0).astype(F32)
        buf[pl.ds(0, CONV_HALO), :] = uh_ref[...].astype(F32) * keep
        buf[pl.ds(CONV_HALO, tm), :] = u_ref[...].astype(F32)
        _conv_causal(buf, w_ref, b_ref, conv, tm, D, K)
        acc = conv[...]
        u1_ref[...] = acc.astype(BF16)
        xhat, _ = _ln_stats(acc)
        u2 = xhat * lg_ref[...] + lb_ref[...]
        u3 = (u2 * _sigmoid(u2)).astype(BF16)
        co = _dot(u3, wcp_ref[...])
        ao = _dot(o_ref[...], wo_ref[...])
        co_ref[...] = co.astype(BF16)
        ao_ref[...] = ao.astype(BF16)
        merged = (sgc_ref[...].astype(F32) * co + sgt_ref[...].astype(F32) * ao).astype(BF16)
        mg_ref[...] = merged
        x2_ref[...] = x_ref[...] + _dot(merged, wout_ref[...])

    big = jax.ShapeDtypeStruct((T, D), BF16)
    vec = _resident((1, D))
    return pl.pallas_call(
        body, name=name, grid=(T // tm,),
        out_shape=(jax.ShapeDtypeStruct((T, D), F32), big, big, big, big),
        in_specs=[_rows(tm, D), _rows(tm, D), _halo_rows_spec(tm, D, CONV_HALO), _rows(tm, D), _rows(tm, D), _rows(tm, D),
                  _resident((K, D)), vec, vec, vec, _resident((D, D)), _resident((D, D)), _resident((D, D))],
        out_specs=(_rows(tm, D),) * 5,
        scratch_shapes=[pltpu.VMEM((CONV_HALO + tm, D), F32), pltpu.VMEM((tm, D), F32)],
        compiler_params=_params("arbitrary"),
    )(x, u0, u0, o, sgc, sgt, dw_w, dw_b, ln_g, ln_b, wcp, wo, wout)


def mix_out_bwd(dx2, u1, co, ao, sgc, sgt, merged, ln_g, ln_b, wcp, wo, wout, name, carry=None):
    T, D = dx2.shape
    tm = min(ROW_TILE, T)
    nt = T // tm

    def body(dx_ref, u1_ref, co_ref, ao_ref, sgc_ref, sgt_ref, mg_ref, lg_ref, lb_ref, wcp_ref, wo_ref, wout_ref,
             dgc_ref, dgt_ref, do_ref, du1_ref, dao_ref, sums_ref, gwout_ref, gwcp_ref, acc_out, acc_cp):
        i = pl.program_id(0)

        @pl.when(i == 0)
        def _():
            sums_ref[...] = jnp.zeros_like(sums_ref)
            acc_out[...] = jnp.zeros_like(acc_out)
            acc_cp[...] = jnp.zeros_like(acc_cp)

        dx16 = dx_ref[...].astype(BF16)
        acc_out[...] += _tn(mg_ref[...], dx16)
        dm = _nt(dx16, wout_ref[...])
        sgc, sgt = sgc_ref[...].astype(F32), sgt_ref[...].astype(F32)
        dco = (dm * sgc).astype(BF16)
        dao = (dm * sgt).astype(BF16)
        dgc = dm * co_ref[...].astype(F32) * sgc * (1.0 - sgc)
        dgt = dm * ao_ref[...].astype(F32) * sgt * (1.0 - sgt)
        dao_ref[...] = dao
        dgc_ref[...] = dgc.astype(BF16)
        dgt_ref[...] = dgt.astype(BF16)
        do_ref[...] = _nt(dao, wo_ref[...]).astype(BF16)
        du3 = _nt(dco, wcp_ref[...])
        xhat, rstd = _ln_stats(u1_ref[...].astype(F32))
        g = lg_ref[...]
        u2 = xhat * g + lb_ref[...]
        su = _sigmoid(u2)
        acc_cp[...] += _tn((u2 * su).astype(BF16), dco)
        du2 = du3 * (su * (1.0 + u2 * (1.0 - su)))
        dxh = du2 * g
        du1 = rstd * (dxh - jnp.mean(dxh, axis=-1, keepdims=True) - xhat * jnp.mean(dxh * xhat, axis=-1, keepdims=True))
        du1_ref[...] = du1.astype(BF16)
        for r, val in enumerate((dgc, dgt, du2 * xhat, du2, du1)):
            sums_ref[pl.ds(r, 1), :] += jnp.sum(val, axis=0, keepdims=True)

        @pl.when(i == nt - 1)
        def _():
            gwout_ref[...] = acc_out[...].astype(BF16)
            gwcp_ref[...] = acc_cp[...].astype(BF16)

    big = jax.ShapeDtypeStruct((T, D), BF16)
    square = jax.ShapeDtypeStruct((D, D), BF16)
    vec = _resident((1, D))
    return _call(
        body, name=name, grid=(nt,),
        out_shape=(big,) * 5 + (jax.ShapeDtypeStruct((8, D), F32), square, square),
        in_specs=[_rows(tm, D)] * 7 + [vec, vec, _resident((D, D)), _resident((D, D)), _resident((D, D))],
        out_specs=(_rows(tm, D),) * 5 + (_acc_spec((8, D)), _resident((D, D)), _resident((D, D))),
        scratch_shapes=[pltpu.VMEM((D, D), F32), pltpu.VMEM((D, D), F32)],
        args=(dx2, u1, co, ao, sgc, sgt, merged, ln_g, ln_b, wcp, wo, wout), carry=carry)


def conv_bwd(du1, u0, ga, gb, dw_w, name, carry=None):
    T, D = du1.shape
    tm = min(ROW_TILE, T)
    nt = T // tm
    K = dw_w.shape[0]
    per = tm // CONV_HALO

    def body(d_ref, dn_ref, u_ref, uh_ref, ga_ref, gb_ref, w_ref, dga_ref, dgb_ref, dw_ref, dbuf, ubuf, du0_buf, taps):
        i = pl.program_id(0)
        dbuf[pl.ds(0, tm), :] = d_ref[...].astype(F32)
        dbuf[pl.ds(tm, CONV_HALO), :] = dn_ref[...].astype(F32) * (i < nt - 1).astype(F32)
        ubuf[pl.ds(0, CONV_HALO), :] = uh_ref[...].astype(F32) * (i > 0).astype(F32)
        ubuf[pl.ds(CONV_HALO, tm), :] = u_ref[...].astype(F32)

        @pl.when(i == 0)
        def _():
            taps[...] = jnp.zeros_like(taps)

        _conv_anticausal(dbuf, w_ref, du0_buf, tm, D, K)
        _conv_tap_grads(dbuf, ubuf, taps, tm, D, K)
        du0 = du0_buf[...]
        ga, gb = ga_ref[...].astype(F32), gb_ref[...].astype(F32)
        sg = _sigmoid(gb)
        dga_ref[...] = (du0 * sg).astype(BF16)
        dgb_ref[...] = (du0 * ga * sg * (1.0 - sg)).astype(BF16)

        @pl.when(i == nt - 1)
        def _():
            for k in range(K):
                dw_ref[pl.ds(k, 1), :] = jnp.sum(taps[k], axis=0, keepdims=True)

    nxt = pl.BlockSpec((CONV_HALO, D), lambda i: (jnp.minimum((i + 1) * per, nt * per - 1), 0))
    big = jax.ShapeDtypeStruct((T, D), BF16)
    return _call(
        body, name=name, grid=(nt,),
        out_shape=(big, big, jax.ShapeDtypeStruct((K, D), F32)),
        in_specs=[_rows(tm, D), nxt, _rows(tm, D), _halo_rows_spec(tm, D, CONV_HALO), _rows(tm, D), _rows(tm, D),
                  _resident((K, D))],
        out_specs=(_rows(tm, D), _rows(tm, D), _acc_spec((K, D))),
        scratch_shapes=[pltpu.VMEM((tm + CONV_HALO, D), F32), pltpu.VMEM((CONV_HALO + tm, D), F32), pltpu.VMEM((tm, D), F32),
                        pltpu.VMEM((K, SUBLANES, D), F32)],
        args=(du1, du1, u0, u0, ga, gb, dw_w), carry=carry)


def mix_in_bwd(dx2, x, gain, wint, wq, pieces, name):
    T, D = x.shape
    tm = min(ROW_TILE, T)
    widths = [p.shape[1] for p in pieces]
    offs = _w_in_rows(D)

    def body(dx2_ref, x_ref, g_ref, w_ref, wq_ref, *rest):
        piece_refs, (dx_ref, dg_ref) = rest[:len(pieces)], rest[len(pieces):]
        dh = None
        for n, (p_ref, off, w) in enumerate(zip(piece_refs, offs, widths)):
            t = _dot(p_ref[...], wq_ref[...] if n == 2 else w_ref[pl.ds(off, w), :])
            dh = t if dh is None else dh + t
        dx, dgt = _rms_bwd(x_ref[...], g_ref[...], dh)
        dx_ref[...] = dx2_ref[...] + dx

        @pl.when(pl.program_id(0) == 0)
        def _():
            dg_ref[...] = jnp.zeros_like(dg_ref)
        dg_ref[...] += jnp.sum(dgt, axis=0, keepdims=True)

    return pl.pallas_call(
        body, name=name, grid=(T // tm,),
        out_shape=(jax.ShapeDtypeStruct((T, D), F32), jax.ShapeDtypeStruct((1, D), F32)),
        in_specs=[_rows(tm, D), _rows(tm, D), _resident((1, D)), _resident(wint.shape), _resident(wq.shape)]
        + [_rows(tm, w) for w in widths],
        out_specs=(_rows(tm, D), _acc_spec((1, D))),
        compiler_params=_params("arbitrary"),
    )(dx2, x, gain, wint, wq, *pieces)


def rope_tables(positions):
    half = HEAD_DIM // 2
    inv_freq = ROPE_THETA ** (-jnp.arange(half, dtype=F32) / half)
    ang = positions.astype(F32)[:, None] * inv_freq
    cos, sin = jnp.cos(ang), jnp.sin(ang)
    reps = LANES // HEAD_DIM
    return jnp.tile(jnp.concatenate([cos, cos], axis=-1), (1, reps)), jnp.tile(jnp.concatenate([-sin, sin], axis=-1), (1, reps))


def _place():
    return lax.axis_index("x"), lax.axis_index("y"), lax.axis_index("c")


def all_gather(blocks, name):
    n = len(blocks)
    send = gather_send(blocks)
    forward = gather_forward(send.out_shapes)
    n_sems = len(send.sems)

    def body(*refs):
        ins, outs, sems = refs[:n], refs[n:2 * n], refs[2 * n:]
        sent, landing, mine = send.copies(ins, outs, sems[:n_sems])
        passed, relanding = forward.copies((), outs, sems[n_sems:])
        for cp in mine + sent:
            cp.start()
        for j in range(3):
            for i in range(n):
                landing[4 * i + 1 + j].wait_recv()
                passed[3 * i + j].start()
        for i in range(n):
            landing[4 * i].wait_recv()
        for cp in relanding:
            cp.wait_recv()
        for cp in sent + passed:
            cp.wait_send()
        for cp in mine:
            cp.wait()

    return pl.pallas_call(
        body, name=name, out_shape=tuple(send.out_shapes), in_specs=[_ANY] * n, out_specs=(_ANY,) * n,
        scratch_shapes=list(send.sems) + list(forward.sems),
    )(*blocks)


def _chips_across(x, y):
    return [(1 - x, y), (x, 1 - y), (1 - x, 1 - y)]


def _dev_index(x, y, c):
    return 4 * x + 2 * y + c


def gather_send(blocks):
    n = len(blocks)

    def copies(in_refs, out_refs, sems):
        send, recv, local = sems
        x, y, c = _place()
        targets = [(x, y, 1 - c)] + [(*chip, c) for chip in _chips_across(x, y)]
        outgoing, incoming, mine = [], [], []
        for i, (x_ref, out_ref) in enumerate(zip(in_refs, out_refs)):
            for k, t in enumerate(targets):
                pair = dict(send_sem=send.at[4 * i + k], recv_sem=recv.at[4 * i + k], device_id=t, device_id_type=MESH)
                outgoing.append(pltpu.make_async_remote_copy(src_ref=x_ref, dst_ref=out_ref.at[_dev_index(x, y, c)], **pair))
                incoming.append(pltpu.make_async_remote_copy(src_ref=x_ref, dst_ref=out_ref.at[_dev_index(*t)], **pair))
            mine.append(pltpu.make_async_copy(x_ref, out_ref.at[_dev_index(x, y, c)], local.at[i]))
        return outgoing, incoming, mine

    def start(*refs):
        outgoing, _, mine = copies(*refs)
        for cp in mine + outgoing:
            cp.start()

    def finish(*refs):
        outgoing, incoming, mine = copies(*refs)
        for cp in incoming:
            cp.wait_recv()
        for cp in outgoing:
            cp.wait_send()
        for cp in mine:
            cp.wait()

    return Carry(ins=tuple(blocks), out_shapes=tuple(jax.ShapeDtypeStruct((N_DEV,) + b.shape, b.dtype) for b in blocks),
                 aliases={}, sems=(pltpu.SemaphoreType.DMA((4 * n,)), pltpu.SemaphoreType.DMA((4 * n,)),
                                   pltpu.SemaphoreType.DMA((n,))), start=start, finish=finish, copies=copies)


def gather_forward(gathered):
    n = len(gathered)

    def copies(in_refs, out_refs, sems):
        send, recv = sems
        x, y, c = _place()
        outgoing, incoming = [], []
        for i, buf in enumerate(out_refs):
            for k, chip in enumerate(_chips_across(x, y)):
                pair = dict(send_sem=send.at[3 * i + k], recv_sem=recv.at[3 * i + k], device_id=(x, y, 1 - c),
                            device_id_type=MESH)
                rows = buf.at[_dev_index(*chip, c)]
                outgoing.append(pltpu.make_async_remote_copy(src_ref=rows, dst_ref=rows, **pair))
                theirs = buf.at[_dev_index(*chip, 1 - c)]
                incoming.append(pltpu.make_async_remote_copy(src_ref=theirs, dst_ref=theirs, **pair))
        return outgoing, incoming

    def start(*refs):
        for cp in copies(*refs)[0]:
            cp.start()

    def finish(*refs):
        outgoing, incoming = copies(*refs)
        for cp in incoming:
            cp.wait_recv()
        for cp in outgoing:
            cp.wait_send()

    return Carry(ins=tuple(gathered), out_shapes=tuple(jax.ShapeDtypeStruct(g.shape, g.dtype) for g in gathered),
                 aliases={i: i for i in range(n)},
                 sems=(pltpu.SemaphoreType.DMA((3 * n,)), pltpu.SemaphoreType.DMA((3 * n,))), start=start, finish=finish,
                 copies=copies)


def gather_whole(blocks):
    send = gather_send(blocks)
    forward = gather_forward(send.out_shapes)
    n = len(send.sems)

    def relay(ins, outs, sems):
        send.finish(ins, outs, sems[:n])
        forward.start((), outs, sems[n:])

    return Carry(ins=send.ins, out_shapes=send.out_shapes, aliases={}, sems=send.sems + forward.sems,
                 start=lambda ins, outs, sems: send.start(ins, outs, sems[:n]), relay=relay,
                 finish=lambda ins, outs, sems: forward.finish((), outs, sems[n:]))


def compose(*carries):
    def split(refs, count):
        out, at = [], 0
        for c in carries:
            n = count(c)
            out.append(refs[at:at + n])
            at += n
        return out

    def each(stage):
        def run(ins, outs, sems):
            parts = zip(carries, split(ins, lambda c: len(c.ins)), split(outs, lambda c: len(c.out_shapes)),
                        split(sems, lambda c: len(c.sems)))
            for c, i, o, s in parts:
                if getattr(c, stage) is not None:
                    getattr(c, stage)(i, o, s)
        return run

    aliases, n_in, n_out = {}, 0, 0
    for c in carries:
        aliases.update({n_in + i: n_out + o for i, o in c.aliases.items()})
        n_in += len(c.ins)
        n_out += len(c.out_shapes)
    return Carry(ins=sum((tuple(c.ins) for c in carries), ()), out_shapes=sum((tuple(c.out_shapes) for c in carries), ()),
                 aliases=aliases, sems=sum((tuple(c.sems) for c in carries), ()), start=each("start"), finish=each("finish"),
                 relay=each("relay") if any(c.relay is not None for c in carries) else None)


def swap_halves(by_core):
    n = len(by_core)

    def copies(in_refs, out_refs, sems):
        send, recv = sems
        x, y, c = _place()
        return [pltpu.make_async_remote_copy(src_ref=a.at[:, 1 - c], dst_ref=r, send_sem=send.at[i], recv_sem=recv.at[i],
                                             device_id=(x, y, 1 - c), device_id_type=MESH)
                for i, (a, r) in enumerate(zip(in_refs, out_refs))]

    def start(*refs):
        for cp in copies(*refs):
            cp.start()

    def finish(*refs):
        for cp in copies(*refs):
            cp.wait()

    shapes = tuple(jax.ShapeDtypeStruct((a.shape[0],) + a.shape[2:], a.dtype) for a in by_core)
    return Carry(ins=tuple(by_core), out_shapes=shapes, aliases={},
                 sems=(pltpu.SemaphoreType.DMA((n,)), pltpu.SemaphoreType.DMA((n,))), start=start, finish=finish)


def exchange_between_chips(by_chip):
    n = len(by_chip)

    def copies(in_refs, out_refs, sems):
        send, recv = sems
        x, y, c = _place()
        out = []
        for i, (s, r) in enumerate(zip(in_refs, out_refs)):
            for k, (tx, ty) in enumerate(_chips_across(x, y)):
                out.append(pltpu.make_async_remote_copy(
                    src_ref=s.at[2 * tx + ty], dst_ref=r.at[k], send_sem=send.at[3 * i + k], recv_sem=recv.at[3 * i + k],
                    device_id=(tx, ty, c), device_id_type=MESH))
        return out

    def start(*refs):
        for cp in copies(*refs):
            cp.start()

    def finish(*refs):
        for cp in copies(*refs):
            cp.wait()

    shapes = tuple(jax.ShapeDtypeStruct((3,) + a.shape[1:], a.dtype) for a in by_chip)
    return Carry(ins=tuple(by_chip), out_shapes=shapes, aliases={},
                 sems=(pltpu.SemaphoreType.DMA((3 * n,)), pltpu.SemaphoreType.DMA((3 * n,))), start=start, finish=finish)


def run_exchange(carry, name):
    n_in = len(carry.ins)
    n_out = len(carry.out_shapes)

    def body(*refs):
        parts = refs[:n_in], refs[n_in:n_in + n_out], refs[n_in + n_out:]
        carry.start(*parts)
        carry.finish(*parts)

    return pl.pallas_call(
        body, name=name, out_shape=tuple(carry.out_shapes), in_specs=[_ANY] * n_in, out_specs=(_ANY,) * n_out,
        scratch_shapes=list(carry.sems), input_output_aliases=dict(carry.aliases),
    )(*carry.ins)


def pair_sum(my_core, by_core, received, name):
    n = len(by_core)

    def body(core_ref, *refs):
        for a_ref, b_ref, o_ref in zip(refs[:n], refs[n:2 * n], refs[2 * n:]):
            o_ref[0] = (a_ref[0, 0].astype(F32) + b_ref[0].astype(F32)).astype(BF16)

    mine = [pl.BlockSpec((1, 1) + a.shape[2:], lambda j, core: (j, core[0], 0, 0)) for a in by_core]
    theirs = [pl.BlockSpec((1,) + r.shape[1:], lambda j, core: (j, 0, 0)) for r in received]
    return pl.pallas_call(
        body, name=name, out_shape=tuple(jax.ShapeDtypeStruct(r.shape, BF16) for r in received),
        grid_spec=pltpu.PrefetchScalarGridSpec(num_scalar_prefetch=1, grid=(by_core[0].shape[0],), in_specs=mine + theirs,
                                               out_specs=tuple(theirs)),
        compiler_params=_params("arbitrary"),
    )(my_core, *by_core, *received)


def _adamw_math(w, g, m, v):
    m = ADAM_B1 * m + (1.0 - ADAM_B1) * g
    v = ADAM_B2 * v + (1.0 - ADAM_B2) * (g * g)
    m_hat = m / (1.0 - ADAM_B1 ** ADAM_STEP)
    v_hat = v / (1.0 - ADAM_B2 ** ADAM_STEP)
    delta = -ADAM_LR * (m_hat / (jnp.sqrt(v_hat) + ADAM_EPS) + ADAM_WD * w)
    return delta, m, v


def adamw(my_chip, ws, gs, ms, vs, name, carry=None):
    n = len(ws)
    flat, widths = [], []
    for g in gs:
        parts = list(g) if isinstance(g, (tuple, list)) else [g]
        flat += parts
        widths.append(len(parts))
    c_ins = list(carry.ins) if carry else []
    c_outs = list(carry.out_shapes) if carry else []

    def body(chip_ref, *refs):
        w_refs, refs = refs[:n], refs[n:]
        g_refs, refs = refs[:len(flat)], refs[len(flat):]
        m_refs, v_refs, refs = refs[:n], refs[n:2 * n], refs[2 * n:]
        carried = (refs[:len(c_ins)], refs[len(c_ins) + 4 * n:len(c_ins) + 4 * n + len(c_outs)],
                   refs[len(c_ins) + 4 * n + len(c_outs):])
        outs = refs[len(c_ins):]
        if carry:
            @pl.when(pl.program_id(0) == 0)
            def _():
                carry.start(*carried)
        at = 0
        for t in range(n):
            if widths[t] == 1:
                g = g_refs[at][...]
            else:
                g = g_refs[at][0].astype(F32)
                for k in range(3):
                    g = g + g_refs[at + 1][k].astype(F32)
            at += widths[t]
            outs[4 * t][...] = g
            outs[4 * t + 1][...], outs[4 * t + 2][...], outs[4 * t + 3][...] = _adamw_math(
                w_refs[t][...], g, m_refs[t][...], v_refs[t][...])
        if carry:
            @pl.when(pl.program_id(0) == ADAMW_STEPS - 1)
            def _():
                if carry.relay is not None:
                    carry.relay(*carried)
                carry.finish(*carried)

    def rows_of(a):
        return a.shape[-2] // ADAMW_STEPS if a.shape[-2] % (ADAMW_STEPS * SUBLANES) == 0 else None

    def whole(a):
        r = rows_of(a)
        lead = (0,) * (a.ndim - 2)
        if r is None:
            zeros = (0,) * a.ndim
            return pl.BlockSpec(a.shape, lambda i, chip: zeros, pipeline_mode=pl.Buffered(1))
        return pl.BlockSpec(a.shape[:-2] + (r, a.shape[-1]), lambda i, chip: lead + (i, 0))

    g_specs = []
    for g in gs:
        if isinstance(g, (tuple, list)):
            r = rows_of(g[0])
            g_specs += [pl.BlockSpec((1, r, g[0].shape[2]), lambda i, chip: (chip[0], i, 0)), whole(g[1])]
        else:
            g_specs.append(whole(g))
    shapes, out_specs = [], []
    for w in ws:
        shapes += [jax.ShapeDtypeStruct(w.shape, F32)] * 4
        out_specs += [whole(w)] * 4
    res = pl.pallas_call(
        body, name=name, out_shape=tuple(shapes) + tuple(c_outs),
        grid_spec=pltpu.PrefetchScalarGridSpec(
            num_scalar_prefetch=1, grid=(ADAMW_STEPS,),
            in_specs=[whole(w) for w in ws] + g_specs + [whole(a) for a in ms + vs] + [_ANY] * len(c_ins),
            out_specs=tuple(out_specs) + (_ANY,) * len(c_outs), scratch_shapes=list(carry.sems) if carry else []),
        compiler_params=_params("arbitrary"),
    )(my_chip, *ws, *flat, *ms, *vs, *c_ins)
    return [tuple(res[4 * t:4 * t + 4]) for t in range(n)], tuple(res[4 * n:])


def adamw_replicated(w, partials, m, v, name):
    def body(w_ref, p_ref, m_ref, v_ref, g_ref, d_ref, mo_ref, vo_ref):
        g = p_ref[0]
        for k in range(1, N_DEV):
            g = g + p_ref[k]
        g_ref[...] = g
        d_ref[...], mo_ref[...], vo_ref[...] = _adamw_math(w_ref[...], g, m_ref[...], v_ref[...])

    shape = jax.ShapeDtypeStruct(w.shape, F32)
    return pl.pallas_call(body, name=name, out_shape=(shape,) * 4, compiler_params=_params())(w, partials, m, v)


PACK_COLS = 1024
PACK_ROW_ALIGN = 16

REPLICATED = ("ffn1_norm", "mix_norm", "conv_dw_b", "conv_ln_g", "conv_ln_b", "ffn2_norm", "final_norm", "gate_b", "attn_sinks")
WEIGHT_ORDER = ("ffn1_norm", "ffn1_w_gate", "ffn1_w_up", "ffn1_w_down", "mix_norm", "w_in", "conv_dw_w", "conv_dw_b", "conv_ln_g",
                "conv_ln_b", "conv_w_proj", "attn_sinks", "attn_w_o", "gate_b", "w_out", "ffn2_norm", "ffn2_w_gate", "ffn2_w_up",
                "ffn2_w_down", "final_norm")


def _to_rows(flat, lead):
    n = flat.shape[-1]
    rows = -(-n // PACK_COLS)
    flat = jnp.pad(flat, [(0, 0)] * lead + [(0, rows * PACK_COLS - n)])
    return flat.reshape(flat.shape[:lead] + (rows, PACK_COLS))


def _pad_rows(a, axis):
    rows = a.shape[axis]
    pad = -rows % PACK_ROW_ALIGN
    widths = [(0, 0)] * a.ndim
    widths[axis] = (0, pad)
    return jnp.pad(a, widths)


def _from_rows(rows, shape):
    n = 1
    for s in shape:
        n *= s
    return rows.reshape(rows.shape[:-2] + (-1,))[..., :n].reshape(rows.shape[:-2] + tuple(shape))


def _heads_slot_major(rows):
    group = rows.shape[0] // (N_KV_HEADS * HEAD_DIM)
    return rows.reshape(N_KV_HEADS, group, HEAD_DIM, rows.shape[1]).transpose(1, 0, 2, 3).reshape(rows.shape)


def _heads_kv_major(rows):
    group = rows.shape[0] // (N_KV_HEADS * HEAD_DIM)
    return rows.reshape(group, N_KV_HEADS, HEAD_DIM, rows.shape[1]).transpose(1, 0, 2, 3).reshape(rows.shape)


def _by_core(full_rows):
    return full_rows.reshape((N_DEV // 2, 2, full_rows.shape[0] // N_DEV, full_rows.shape[1]))


def kernel(x, positions, ffn1_norm, ffn1_w_gate, ffn1_w_up, ffn1_w_down, mix_norm, w_in, conv_dw_w, conv_dw_b, conv_ln_g, conv_ln_b, conv_w_proj, attn_sinks, attn_w_o, gate_b, w_out, ffn2_norm, ffn2_w_gate, ffn2_w_up, ffn2_w_down, final_norm, loss_target, m_ffn1_norm, m_ffn1_w_gate, m_ffn1_w_up, m_ffn1_w_down, m_mix_norm, m_w_in, m_conv_dw_w, m_conv_dw_b, m_conv_ln_g, m_conv_ln_b, m_conv_w_proj, m_attn_sinks, m_attn_w_o, m_gate_b, m_w_out, m_ffn2_norm, m_ffn2_w_gate, m_ffn2_w_up, m_ffn2_w_down, m_final_norm, v_ffn1_norm, v_ffn1_w_gate, v_ffn1_w_up, v_ffn1_w_down, v_mix_norm, v_w_in, v_conv_dw_w, v_conv_dw_b, v_conv_ln_g, v_conv_ln_b, v_conv_w_proj, v_attn_sinks, v_attn_w_o, v_gate_b, v_w_out, v_ffn2_norm, v_ffn2_w_gate, v_ffn2_w_up, v_ffn2_w_down, v_final_norm):
    given = dict(locals())
    shapes = {n: given[n].shape for n in WEIGHT_ORDER}
    w = {n: given[n].reshape(given[n].shape[-2:]) if given[n].ndim == 3 else given[n].reshape(1, -1) for n in WEIGHT_ORDER}
    m = {n: given["m_" + n].reshape(w[n].shape) for n in WEIGHT_ORDER}
    v = {n: given["v_" + n].reshape(w[n].shape) for n in WEIGHT_ORDER}
    my_x, my_y, my_c = _place()
    my_core = my_c.astype(jnp.int32).reshape(1)
    my_chip = (2 * my_x + my_y).astype(jnp.int32).reshape(1)
    xs, target = x[0], loss_target[0]
    T, D = xs.shape
    KV = N_KV_HEADS * HEAD_DIM
    K = w["conv_dw_w"].shape[0]

    def t16(n):
        return w[n].T.astype(BF16)

    def r16(n):
        return w[n].astype(BF16)

    blocks1 = [t16("ffn1_w_gate"), t16("ffn1_w_up"), r16("ffn1_w_down")]
    dw_bits = _pad_rows(_to_rows(lax.bitcast_convert_type(w["conv_dw_w"], BF16).reshape(-1), 0), 0)
    blocks2 = [t16("w_in"), r16("conv_w_proj"), r16("attn_w_o"), r16("w_out"), dw_bits]
    blocks3 = [t16("ffn2_w_gate"), t16("ffn2_w_up"), r16("ffn2_w_down")]
    cos, sin = rope_tables(positions[0])
    sink_col = jnp.repeat(w["attn_sinks"].reshape(-1), WINDOW).reshape(N_KV_HEADS, (D // KV) * WINDOW, 1)

    def full(gathered):
        return gathered.reshape(-1, gathered.shape[2])

    wgt1, wut1 = (full(g) for g in all_gather(blocks1[:2], "gather_ffn1_up"))
    (h1, a1, b1, s1), got = ffn_up(xs, w["ffn1_norm"], wgt1, wut1, "ffn1_up",
                                   carry=compose(gather_whole(blocks1[2:]), gather_send(blocks2[:1])))
    wd1 = full(got[0])
    (x1,), got = ffn_down(xs, s1, wd1, "ffn1_down", carry=compose(gather_forward(got[1:]), gather_send(blocks2[1:])))
    wint = full(got[0])
    wq = _heads_slot_major(wint[2 * D:3 * D])
    (h2, ga, gb, u0, q, sgc, sgt, kk, vv), got = mix_in_fwd(x1, w["mix_norm"], wint, wq, w["gate_b"], cos, sin, "mix_in_fwd",
                                                            carry=compose(gather_forward(got[1:]), gather_send(blocks3)))
    wcp, wo, wout = (full(g) for g in got[:3])
    wo = _heads_slot_major(wo)
    dw_full = lax.bitcast_convert_type(_from_rows(got[3], w["conv_dw_w"].shape + (2,)), F32)
    dw_full = dw_full.transpose(1, 0, 2).reshape(K, D)
    o, gath3 = attn_fwd(q, kk, vv, sink_col, "attn_fwd", carry=gather_forward(got[4:]))
    x2, u1, co, ao, merged = mix_out_fwd(x1, u0, o, sgc, sgt, dw_full, w["conv_dw_b"], w["conv_ln_g"], w["conv_ln_b"],
                                         wcp, wo, wout, "mix_out_fwd")
    wgt2, wut2, wd2 = (full(g) for g in gath3)
    loss, dx3, d_final, h3, a2, b2, s2 = ffn_fwd_loss(x2, w["ffn2_norm"], wgt2, wut2, wd2, w["final_norm"], target,
                                                      "ffn2_fwd_loss")

    small = {"final_norm": d_final}
    (da2, db2), _ = ffn_bwd_hidden(dx3, a2, b2, wd2, "ffn2_bwd_hidden")
    (dx2, small["ffn2_norm"]), _ = ffn_bwd_input(dx3, x2, w["ffn2_norm"], da2, db2, wgt2, wut2, "ffn2_bwd_input")
    core2 = [_by_core(g) for g in (wgrad(da2, h3, "ffn2_dwg")[0], wgrad(db2, h3, "ffn2_dwu")[0],
                                   wgrad(s2, dx3, "ffn2_dwd", b_scale=FFN_SCALE)[0])]
    (dgc, dgt, do, du1, dao, sums, g_wout, g_wcp), recv2 = mix_out_bwd(
        dx2, u1, co, ao, sgc, sgt, merged, w["conv_ln_g"], w["conv_ln_b"], wcp, wo, wout, "mix_out_bwd",
        carry=swap_halves(core2))
    chip2 = pair_sum(my_core, core2, recv2, "ffn2_grads_pair_sum")
    small["gate_b"] = jnp.concatenate([sums[0:1], sums[1:2]], axis=1)
    small["conv_ln_g"], small["conv_ln_b"], small["conv_dw_b"] = sums[2:3], sums[3:4], sums[4:5]
    g_wo =_heads_kv_major(wgrad(o, dao, "dw_attn_o")[0])
    (dga, dgb, g_dw), got2 = conv_bwd(du1, u0, ga, gb, dw_full, "conv_bwd", carry=exchange_between_chips(chip2))
    dq, dk, dv, dsink = attn_bwd(q, kk, vv, do, sink_col, cos, sin, "attn_bwd")
    small["attn_sinks"] = dsink[:D // KV, :N_KV_HEADS].T.reshape(1, -1)
    pieces = [dga, dgb, dq, dk, dv, dgc, dgt]
    dx1, small["mix_norm"] = mix_in_bwd(dx2, x1, w["mix_norm"], wint, wq, pieces, "mix_in_bwd")
    group = D // KV
    q_moves = [(2 * D + HEAD_DIM * (group * g + hh), 2 * D + HEAD_DIM * (N_KV_HEADS * hh + g), HEAD_DIM)
               for g in range(N_KV_HEADS) for hh in range(group)]
    g_wint = wgrad_stacked(pieces[:3], h2, "dw_in_a", wint.shape[0], 0, moves=[(0, 0, 2 * D)] + q_moves)
    g_wint = wgrad_stacked(pieces[3:], h2, "dw_in_b", wint.shape[0], 3 * D, into=g_wint)
    corem = [_by_core(a) for a in (g_wint, g_wcp, g_wo, g_wout)]

    (da1, db1), recvm = ffn_bwd_hidden(dx1, a1, b1, wd1, "ffn1_bwd_hidden", carry=swap_halves(corem))
    chipm = pair_sum(my_core, corem, recvm, "mix_grads_pair_sum")
    g1c, gotm_a = wgrad(s1, dx1, "ffn1_dwd", carry=exchange_between_chips(chipm[:1]), b_scale=FFN_SCALE)
    g1a, gotm_b = wgrad(da1, h1, "ffn1_dwg", carry=exchange_between_chips(chipm[1:]))
    early = [_by_core(g1a), _by_core(g1c)]
    g1b, recv_early = wgrad(db1, h1, "ffn1_dwu", carry=swap_halves(early))
    (recv_late,) = run_exchange(swap_halves([_by_core(g1b)]), "ffn1_grads_swap")
    core1 = [early[0], _by_core(g1b), early[1]]
    chip1 = pair_sum(my_core, core1, [recv_early[0], recv_late, recv_early[1]], "ffn1_grads_pair_sum")
    (grad_x, small["ffn1_norm"]), got1 = ffn_bwd_input(dx1, xs, w["ffn1_norm"], da1, db1, wgt1, wut1, "ffn1_bwd_input",
                                                       carry=exchange_between_chips(chip1))

    gotm = gotm_a + gotm_b
    grad_src = {"ffn1_w_gate": (chip1[0], got1[0]), "ffn1_w_up": (chip1[1], got1[1]), "ffn1_w_down": (chip1[2], got1[2]),
                "ffn2_w_gate": (chip2[0], got2[0]), "ffn2_w_up": (chip2[1], got2[1]), "ffn2_w_down": (chip2[2], got2[2]),
                "w_in": (chipm[0], gotm[0]), "conv_w_proj": (chipm[1], gotm[1]), "attn_w_o": (chipm[2], gotm[2]),
                "w_out": (chipm[3], gotm[3])}
    grads = {}

    def pack_small(d, taps, extra):
        rows = [_to_rows(d[n].reshape(-1), 0) for n in REPLICATED] + [taps, _to_rows(extra.reshape(-1), 0)]
        return _pad_rows(jnp.concatenate(rows, axis=0), 0)

    def like(a, ref):
        return a if a.shape == ref.shape else a.T

    def adamw_group(names, name, carry=None):
        refs = [grad_src[n][0][0] if isinstance(grad_src[n], tuple) else grad_src[n] for n in names]
        res, carried = adamw(my_chip, [like(w[n], r) for n, r in zip(names, refs)], [grad_src[n] for n in names],
                             [like(m[n], r) for n, r in zip(names, refs)], [like(v[n], r) for n, r in zip(names, refs)], name,
                             carry=carry)
        for n, outs in zip(names, res):
            grads[n], delta[n], new_m[n], new_v[n] = (like(a, w[n]) for a in outs)
        return carried

    delta, new_m, new_v = {}, {}, {}
    zero, no_taps = jnp.zeros((1, LANES), F32), jnp.zeros((K, D), F32)
    (shares,) = adamw_group(("ffn2_w_gate", "ffn2_w_up", "ffn2_w_down"), "adamw_ffn2",
                            carry=gather_whole([pack_small(small, g_dw, loss)]))
    g_s, d_s, m_s, v_s = adamw_replicated(pack_small(w, no_taps, zero), shares, pack_small(m, no_taps, zero),
                                          pack_small(v, no_taps, zero), "adamw_replicated")
    off = 0
    for n in REPLICATED:
        r = -(-w[n].shape[1] // PACK_COLS)
        grads[n], delta[n], new_m[n], new_v[n] = (_from_rows(a[off:off + r], w[n].shape) for a in (g_s, d_s, m_s, v_s))
        off += r
    shard_cols = w["conv_dw_w"].shape[1]
    grad_src["conv_dw_w"] = lax.dynamic_slice_in_dim(g_s[off:off + K], _dev_index(my_x, my_y, my_c) * shard_cols, shard_cols,
                                                    axis=1)
    total_loss = g_s[off + K, 0]
    adamw_group(("ffn1_w_gate", "ffn1_w_up", "ffn1_w_down"), "adamw_ffn1")
    adamw_group(("w_in", "conv_dw_w", "conv_w_proj", "attn_w_o", "w_out"), "adamw_mix")

    out = [total_loss, grad_x[None]]
    for d in (grads, delta, new_m, new_v):
        out += [d[n].reshape(shapes[n]) for n in WEIGHT_ORDER]
    return tuple(out)
```

```python
import functools
from typing import Callable, NamedTuple

import jax
import jax.numpy as jnp
from jax import lax
from jax.experimental import pallas as pl
from jax.experimental.pallas import tpu as pltpu

F32, BF16 = jnp.float32, jnp.bfloat16

HEAD_DIM = 64
N_KV_HEADS = 4
WINDOW = 128
ROPE_THETA = 10000.0
EPS = 1e-6
LN_EPS = 1e-5
NEG_INF = -1e30
ADAM_LR, ADAM_B1, ADAM_B2, ADAM_EPS, ADAM_WD, ADAM_STEP = 0.001, 0.9, 0.999, 1e-08, 0.01, 10
ADAMW_STEPS = 4

N_DEV = 8
LANES = 128
SUBLANES = 8
CONV_HALO = 32
CONV_ROWS, CONV_LANES = 128, 128
ROW_TILE = 512
FFN_CHUNK = 256
FFN_SCALE = 0.5
RING_SLOTS = 3
WGRAD_TILE_ELEMS = 2 ** 22
WGRAD_TILE_ROWS = 2048
WGRAD_VMEM = 40 * 2 ** 20
VMEM_LIMIT = 56 * 2 ** 20
MESH = pl.DeviceIdType.MESH


def _params(*sem):
    return pltpu.CompilerParams(dimension_semantics=sem or None, vmem_limit_bytes=VMEM_LIMIT)


def _resident(shape):
    zeros = (0,) * len(shape)
    return pl.BlockSpec(shape, lambda *_: zeros, pipeline_mode=pl.Buffered(1))


def _rows(tm, n):
    return pl.BlockSpec((tm, n), lambda i: (i, 0))


def _acc_spec(shape):
    zeros = (0,) * len(shape)
    return pl.BlockSpec(shape, lambda *_: zeros)


_ANY = pl.BlockSpec(memory_space=pl.ANY)


class Carry(NamedTuple):
    ins: tuple
    out_shapes: tuple
    aliases: dict
    sems: tuple
    start: Callable
    finish: Callable
    relay: Callable = None
    copies: Callable = None


def _call(body, *, name, grid, in_specs, out_specs, out_shape, args, scratch_shapes=(), carry=None):
    n_in, n_out, n_scr = len(in_specs), len(out_specs), len(scratch_shapes)
    params = _params(*(("arbitrary",) * len(grid)))
    if carry is None:
        res = pl.pallas_call(body, name=name, grid=grid, in_specs=list(in_specs), out_specs=tuple(out_specs),
                             out_shape=tuple(out_shape), scratch_shapes=list(scratch_shapes), compiler_params=params)(*args)
        return tuple(res), ()
    c_in, c_out = len(carry.ins), len(carry.out_shapes)

    def wrapped(*refs):
        ins, c_ins = refs[:n_in], refs[n_in:n_in + c_in]
        p = n_in + c_in
        outs, c_outs = refs[p:p + n_out], refs[p + n_out:p + n_out + c_out]
        p += n_out + c_out
        scr, c_sems = refs[p:p + n_scr], refs[p + n_scr:]
        ids = [pl.program_id(d) for d in range(len(grid))]
        first = functools.reduce(jnp.logical_and, [i == 0 for i in ids])
        last = functools.reduce(jnp.logical_and, [i == n - 1 for i, n in zip(ids, grid)])

        @pl.when(first)
        def _():
            carry.start(c_ins, c_outs, c_sems)

        body(*ins, *outs, *scr)

        if carry.relay is not None:
            @pl.when(ids[0] == (3 * grid[0]) // 4)
            def _():
                carry.relay(c_ins, c_outs, c_sems)

        @pl.when(last)
        def _():
            carry.finish(c_ins, c_outs, c_sems)

    res = pl.pallas_call(
        wrapped, name=name, grid=grid, in_specs=list(in_specs) + [_ANY] * c_in, out_specs=tuple(out_specs) + (_ANY,) * c_out,
        out_shape=tuple(out_shape) + tuple(carry.out_shapes), scratch_shapes=list(scratch_shapes) + list(carry.sems),
        input_output_aliases={n_in + i: n_out + o for i, o in carry.aliases.items()}, compiler_params=params,
    )(*args, *carry.ins)
    return tuple(res[:n_out]), tuple(res[n_out:])


def _nt(a, b):
    return lax.dot_general(a, b, (((1,), (1,)), ((), ())), preferred_element_type=F32)


def _tn(a, b):
    return lax.dot_general(a, b, (((0,), (0,)), ((), ())), preferred_element_type=F32)


def _dot(a, b):
    return jnp.dot(a, b, preferred_element_type=F32)


def _sigmoid(x):
    return 1.0 / (1.0 + jnp.exp(-x))


def _rms_fwd(x, g):
    r = lax.rsqrt(jnp.mean(x * x, axis=-1, keepdims=True) + EPS)
    return (x * r) * g


def _rms_bwd(x, g, dy):
    r = lax.rsqrt(jnp.mean(x * x, axis=-1, keepdims=True) + EPS)
    xhat = x * r
    dyg = dy * g
    dx = r * (dyg - xhat * jnp.mean(dyg * xhat, axis=-1, keepdims=True))
    return dx, dy * xhat


def _rot_half(x):
    lane = lax.broadcasted_iota(jnp.int32, (x.shape[0], LANES), 1)
    first = (lane % HEAD_DIM) < (HEAD_DIM // 2)
    out = []
    for s in range(x.shape[1] // LANES):
        xs = x[:, LANES * s:LANES * (s + 1)]
        out.append(jnp.where(first, pltpu.roll(xs, LANES - HEAD_DIM // 2, 1), pltpu.roll(xs, HEAD_DIM // 2, 1)))
    return out[0] if len(out) == 1 else jnp.concatenate(out, axis=1)


def _tile_lanes(t, width):
    return t if width == LANES else jnp.concatenate([t] * (width // LANES), axis=1)


def _rope_fwd(x, cos, sin_signed):
    w = x.shape[1]
    return x * _tile_lanes(cos, w) + _rot_half(x) * _tile_lanes(sin_signed, w)


def _rope_bwd(dy, cos, sin_signed):
    w = dy.shape[1]
    return dy * _tile_lanes(cos, w) + _rot_half(dy * _tile_lanes(sin_signed, w))


def _ffn_rows(x, g_ref, wg_ref, wu_ref, wd_ref, h_ref, a_ref, b_ref, s_ref, acc_ref):
    F = wg_ref.shape[0]
    h = _rms_fwd(x, g_ref[...]).astype(BF16)
    h_ref[...] = h
    for c in range(F // FFN_CHUNK):
        cs = pl.ds(c * FFN_CHUNK, FFN_CHUNK)
        a = _nt(h, wg_ref[cs, :])
        b = _nt(h, wu_ref[cs, :])
        a_ref[:, cs] = a.astype(BF16)
        b_ref[:, cs] = b.astype(BF16)
        s = (a * _sigmoid(a) * b).astype(BF16)
        s_ref[:, cs] = s
        y = _dot(s, wd_ref[cs, :])
        if c == 0:
            acc_ref[...] = y
        else:
            acc_ref[...] += y
    return x + FFN_SCALE * acc_ref[...]


def ffn_up(x, gain, wgt, wut, name, carry=None):
    T, D = x.shape
    F = wgt.shape[0]
    tm = min(ROW_TILE, T)

    def body(x_ref, g_ref, wg_ref, wu_ref, h_ref, a_ref, b_ref, s_ref):
        h = _rms_fwd(x_ref[...], g_ref[...]).astype(BF16)
        h_ref[...] = h
        for c in range(F // FFN_CHUNK):
            cs = pl.ds(c * FFN_CHUNK, FFN_CHUNK)
            a = _nt(h, wg_ref[cs, :])
            b = _nt(h, wu_ref[cs, :])
            a_ref[:, cs] = a.astype(BF16)
            b_ref[:, cs] = b.astype(BF16)
            s_ref[:, cs] = (a * _sigmoid(a) * b).astype(BF16)

    wide = jax.ShapeDtypeStruct((T, F), BF16)
    return _call(
        body, name=name, grid=(T // tm,), out_shape=(jax.ShapeDtypeStruct((T, D), BF16), wide, wide, wide),
        in_specs=[_rows(tm, D), _resident((1, D)), _resident((F, D)), _resident((F, D))],
        out_specs=(_rows(tm, D), _rows(tm, F), _rows(tm, F), _rows(tm, F)), args=(x, gain, wgt, wut), carry=carry)


def ffn_down(x, s, wd, name, carry=None):
    T, D = x.shape
    F = wd.shape[0]
    tm = min(2 * ROW_TILE, T)

    def body(x_ref, s_ref, wd_ref, xo_ref):
        xo_ref[...] = x_ref[...] + FFN_SCALE * _dot(s_ref[...], wd_ref[...])

    return _call(
        body, name=name, grid=(T // tm,), out_shape=(jax.ShapeDtypeStruct((T, D), F32),),
        in_specs=[_rows(tm, D), _rows(tm, F), _resident((F, D))], out_specs=(_rows(tm, D),), args=(x, s, wd), carry=carry)


def ffn_fwd_loss(x, gain, wgt, wut, wd, final_gain, target, name):
    T, D = x.shape
    F = wgt.shape[0]
    tm = min(ROW_TILE, T)

    def body(x_ref, g_ref, wg_ref, wu_ref, wd_ref, gf_ref, t_ref,
             loss_ref, dx_ref, dg_ref, h_ref, a_ref, b_ref, s_ref, acc_ref):
        xo = _ffn_rows(x_ref[...], g_ref, wg_ref, wu_ref, wd_ref, h_ref, a_ref, b_ref, s_ref, acc_ref)
        gf = gf_ref[...]
        err = _rms_fwd(xo, gf) - t_ref[...]
        dx, dgt = _rms_bwd(xo, gf, err * (1.0 / D))
        dx_ref[...] = dx

        @pl.when(pl.program_id(0) == 0)
        def _():
            dg_ref[...] = jnp.zeros_like(dg_ref)
            loss_ref[...] = jnp.zeros_like(loss_ref)
        dg_ref[...] += jnp.sum(dgt, axis=0, keepdims=True)
        per_token = jnp.sum(err * err, axis=-1, keepdims=True) * (0.5 / D)
        loss_ref[...] += jnp.broadcast_to(jnp.sum(per_token, axis=0, keepdims=True), (1, LANES))

    return pl.pallas_call(
        body, name=name, grid=(T // tm,),
        out_shape=(jax.ShapeDtypeStruct((1, LANES), F32), jax.ShapeDtypeStruct((T, D), F32), jax.ShapeDtypeStruct((1, D), F32),
                   jax.ShapeDtypeStruct((T, D), BF16), jax.ShapeDtypeStruct((T, F), BF16), jax.ShapeDtypeStruct((T, F), BF16),
                   jax.ShapeDtypeStruct((T, F), BF16)),
        in_specs=[_rows(tm, D), _resident((1, D)), _resident((F, D)), _resident((F, D)), _resident((F, D)), _resident((1, D)),
                  _rows(tm, D)],
        out_specs=(_acc_spec((1, LANES)), _rows(tm, D), _acc_spec((1, D)), _rows(tm, D), _rows(tm, F), _rows(tm, F),
                   _rows(tm, F)),
        scratch_shapes=[pltpu.VMEM((tm, D), F32)], compiler_params=_params("arbitrary"),
    )(x, gain, wgt, wut, wd, final_gain, target)


def ffn_bwd_hidden(dxo, a, b, wd, name, carry=None):
    T, D = dxo.shape
    F = wd.shape[0]
    tm = min(ROW_TILE, T)
    nt = T // tm
    fc = FFN_CHUNK

    def hidden_body(dxo_ref, a_hbm, b_hbm, wd_ref, da_ref, db_ref, a_buf, b_buf, a_sems, b_sems):
        i = pl.program_id(0)

        def fetch(step, slot):
            rows = pl.ds(pl.multiple_of(step * tm, tm), tm)
            return (pltpu.make_async_copy(a_hbm.at[rows, :], a_buf.at[slot], a_sems.at[slot]),
                    pltpu.make_async_copy(b_hbm.at[rows, :], b_buf.at[slot], b_sems.at[slot]))

        @pl.when(i == 0)
        def _():
            for step in range(min(RING_SLOTS - 1, nt)):
                for cp in fetch(step, step):
                    cp.start()

        @pl.when(i + RING_SLOTS - 1 < nt)
        def _():
            for cp in fetch(i + RING_SLOTS - 1, (i + RING_SLOTS - 1) % RING_SLOTS):
                cp.start()

        slot = i % RING_SLOTS
        for cp in fetch(i, slot):
            cp.wait()
        g0 = (FFN_SCALE * dxo_ref[...]).astype(BF16)
        for c in range(F // fc):
            cs = pl.ds(c * fc, fc)
            ds = _nt(g0, wd_ref[cs, :])
            a = a_buf[slot, :, cs].astype(F32)
            bb = b_buf[slot, :, cs].astype(F32)
            sa = _sigmoid(a)
            da_ref[:, cs] = (ds * bb * (sa * (1.0 + a * (1.0 - sa)))).astype(BF16)
            db_ref[:, cs] = (ds * (a * sa)).astype(BF16)

    wide = jax.ShapeDtypeStruct((T, F), BF16)
    return _call(
        hidden_body, name=name, grid=(nt,), out_shape=(wide, wide),
        in_specs=[_rows(tm, D), _ANY, _ANY, _resident((F, D))],
        out_specs=(_rows(tm, F), _rows(tm, F)),
        scratch_shapes=[pltpu.VMEM((RING_SLOTS, tm, F), BF16), pltpu.VMEM((RING_SLOTS, tm, F), BF16),
                        pltpu.SemaphoreType.DMA((RING_SLOTS,)), pltpu.SemaphoreType.DMA((RING_SLOTS,))],
        args=(dxo, a, b, wd), carry=carry)


def ffn_bwd_input(dxo, x, gain, da, db, wg, wu, name, carry=None):
    T, D = x.shape
    F = wg.shape[0]
    tm = min(ROW_TILE, T)

    def input_body(dxo_ref, x_ref, g_ref, da_ref, db_ref, wg_ref, wu_ref, dx_ref, dg_ref):
        dh = _dot(da_ref[...], wg_ref[...]) + _dot(db_ref[...], wu_ref[...])
        dx, dgt = _rms_bwd(x_ref[...], g_ref[...], dh)
        dx_ref[...] = dxo_ref[...] + dx

        @pl.when(pl.program_id(0) == 0)
        def _():
            dg_ref[...] = jnp.zeros_like(dg_ref)
        dg_ref[...] += jnp.sum(dgt, axis=0, keepdims=True)

    return _call(
        input_body, name=name, grid=(T // tm,),
        out_shape=(jax.ShapeDtypeStruct((T, D), F32), jax.ShapeDtypeStruct((1, D), F32)),
        in_specs=[_rows(tm, D), _rows(tm, D), _resident((1, D)), _rows(tm, F), _rows(tm, F), _resident((F, D)),
                  _resident((F, D))],
        out_specs=(_rows(tm, D), _acc_spec((1, D))), args=(dxo, x, gain, da, db, wg, wu), carry=carry)


def wgrad(a, b, name, carry=None, b_scale=None):
    T, M = a.shape
    N = b.shape[1]
    fixed = M * N * (4 + 2)
    per_row = 2 * (M * a.dtype.itemsize + N * b.dtype.itemsize)
    tk = ROW_TILE
    while fixed + 2 * tk * per_row <= WGRAD_VMEM and 2 * tk <= WGRAD_TILE_ROWS:
        tk *= 2
    tk = min(tk, T)
    nk = T // tk

    def body(a_ref, b_ref, o_ref, acc_ref):
        k = pl.program_id(0)
        bt = b_ref[...] if b_scale is None else b_scale * b_ref[...]
        part = _tn(a_ref[...].astype(BF16), bt.astype(BF16))

        @pl.when(k == 0)
        def _():
            acc_ref[...] = part

        @pl.when(k > 0)
        def _():
            acc_ref[...] += part

        @pl.when(k == nk - 1)
        def _():
            o_ref[...] = acc_ref[...].astype(BF16)

    (out,), carried = _call(
        body, name=name, grid=(nk,), out_shape=(jax.ShapeDtypeStruct((M, N), BF16),),
        in_specs=[_rows(tk, M), _rows(tk, N)], out_specs=(_resident((M, N)),),
        scratch_shapes=[pltpu.VMEM((M, N), F32)], args=(a, b), carry=carry)
    return out, carried


def wgrad_stacked(pieces, b, name, total_rows, at, into=None, moves=None):
    T, N = b.shape
    n = len(pieces)
    widths = [p.shape[1] for p in pieces]
    offs = [sum(widths[:i]) for i in range(n)]
    M = sum(widths)
    tk = ROW_TILE
    while 2 * tk * M <= WGRAD_TILE_ELEMS and 2 * tk <= WGRAD_TILE_ROWS:
        tk *= 2
    tk = min(tk, T)
    nk = T // tk
    moves = moves or [(0, 0, M)]
    n_in = n + 1 + (into is not None)

    def body(*refs):
        a_refs, b_ref = refs[:n], refs[n]
        o_ref, acc_ref, stage, sem = refs[n_in:]
        k = pl.program_id(0)

        @pl.when(k == 0)
        def _():
            acc_ref[...] = jnp.zeros_like(acc_ref)
        bt = b_ref[...].astype(BF16)
        for a_ref, off, width in zip(a_refs, offs, widths):
            acc_ref[pl.ds(off, width), :] += _tn(a_ref[...].astype(BF16), bt)

        @pl.when(k == nk - 1)
        def _():
            for to, start, rows in moves:
                stage[pl.ds(to, rows), :] = acc_ref[pl.ds(start, rows), :].astype(BF16)
            cp = pltpu.make_async_copy(stage, o_ref.at[pl.ds(at, M)], sem)
            cp.start()
            cp.wait()

    return pl.pallas_call(
        body, name=name, grid=(nk,), out_shape=jax.ShapeDtypeStruct((total_rows, N), BF16),
        in_specs=[_rows(tk, width) for width in widths] + [_rows(tk, N)] + [_ANY] * (into is not None), out_specs=_ANY,
        scratch_shapes=[pltpu.VMEM((M, N), F32), pltpu.VMEM((M, N), BF16), pltpu.SemaphoreType.DMA],
        input_output_aliases={n + 1: 0} if into is not None else {}, compiler_params=_params("arbitrary"),
    )(*pieces, b, *([into] if into is not None else []))


def _w_in_rows(D):
    KV = N_KV_HEADS * HEAD_DIM
    return 0, D, 2 * D, 3 * D, 3 * D + KV, 3 * D + 2 * KV, 4 * D + 2 * KV


def mix_in_fwd(x, gain, wint, wq, gate_b, cos, sin_signed, name, carry=None):
    T, D = x.shape
    KV = N_KV_HEADS * HEAD_DIM
    tm = min(ROW_TILE, T)
    o_ga, o_gb, _, o_k, o_v, o_gc, o_gt = _w_in_rows(D)

    def body(x_ref, g_ref, w_ref, wq_ref, gb_ref, cos_ref, sin_ref,
             h_ref, ga_ref, gb_out_ref, u0_ref, q_ref, sgc_ref, sgt_ref, k_ref, v_ref):
        h = _rms_fwd(x_ref[...], g_ref[...]).astype(BF16)
        h_ref[...] = h
        cos, sin = cos_ref[...], sin_ref[...]
        ga = _nt(h, w_ref[pl.ds(o_ga, D), :])
        gb = _nt(h, w_ref[pl.ds(o_gb, D), :])
        ga_ref[...] = ga.astype(BF16)
        gb_out_ref[...] = gb.astype(BF16)
        u0_ref[...] = (ga * _sigmoid(gb)).astype(BF16)
        q = _nt(h, wq_ref[...])
        q_ref[...] = _rope_fwd(q, cos, sin).astype(BF16)
        gc = _nt(h, w_ref[pl.ds(o_gc, D), :]) + gb_ref[:, pl.ds(0, D)]
        sgc_ref[...] = _sigmoid(gc).astype(BF16)
        gt = _nt(h, w_ref[pl.ds(o_gt, D), :]) + gb_ref[:, pl.ds(D, D)]
        sgt_ref[...] = _sigmoid(gt).astype(BF16)
        k = _nt(h, w_ref[pl.ds(o_k, KV), :])
        k_ref[...] = _rope_fwd(k, cos, sin).astype(BF16)
        v_ref[...] = _nt(h, w_ref[pl.ds(o_v, KV), :]).astype(BF16)

    big = jax.ShapeDtypeStruct((T, D), BF16)
    small = jax.ShapeDtypeStruct((T, KV), BF16)
    return _call(
        body, name=name, grid=(T // tm,),
        out_shape=(big, big, big, big, big, big, big, small, small),
        in_specs=[_rows(tm, D), _resident((1, D)), _resident(wint.shape), _resident(wq.shape), _resident((1, 2 * D)),
                  _rows(tm, LANES), _rows(tm, LANES)],
        out_specs=(_rows(tm, D),) * 7 + (_rows(tm, KV),) * 2,
        args=(x, gain, wint, wq, gate_b, cos, sin_signed), carry=carry)


def _from_prev(rows):
    qi = lax.broadcasted_iota(jnp.int32, (rows, WINDOW), 0) % WINDOW
    return lax.broadcasted_iota(jnp.int32, (rows, WINDOW), 1) > qi


def _fold(x, g, from_prev):
    lo = 2 * WINDOW * g
    return jnp.where(from_prev, x[:, lo:lo + WINDOW], x[:, lo + WINDOW:lo + 2 * WINDOW])


def _unfold(folded, from_prev):
    zero = jnp.zeros_like(folded[0])
    parts = []
    for x in folded:
        parts += [jnp.where(from_prev, x, zero), jnp.where(from_prev, zero, x)]
    return jnp.concatenate(parts, axis=1)


def _kv_lane_head(rows, width):
    return lax.broadcasted_iota(jnp.int32, (rows, width), 1) // HEAD_DIM


def _block_diag(win):
    head = _kv_lane_head(*win.shape)
    zero = jnp.zeros_like(win)
    return jnp.concatenate([jnp.where(head == g, win, zero) for g in range(N_KV_HEADS)], axis=0)


def _diag_blocks_sum(bd, keys):
    head = _kv_lane_head(keys, bd.shape[1])
    out = jnp.zeros((keys, bd.shape[1]), F32)
    for g in range(N_KV_HEADS):
        out = jnp.where(head == g, bd[g * keys:(g + 1) * keys], out)
    return out


def _kv_windows(k_ref, kh_ref, v_ref, vh_ref, j):
    rows = pl.ds(j * WINDOW, WINDOW)
    if j == 0:
        kprev, vprev = kh_ref[...], vh_ref[...]
    else:
        prev = pl.ds((j - 1) * WINDOW, WINDOW)
        kprev, vprev = k_ref[prev, :], v_ref[prev, :]
    return jnp.concatenate([kprev, k_ref[rows, :]], axis=0), jnp.concatenate([vprev, v_ref[rows, :]], axis=0)


def _stack_slots(ref, j, group, KV):
    rows = pl.ds(j * WINDOW, WINDOW)
    return jnp.concatenate([ref[rows, pl.ds(KV * hh, KV)] for hh in range(group)], axis=0)


def _attn_exp(qs, kbd, sink_ref, from_prev, no_prev):
    s = _nt(qs, kbd)
    if no_prev is not None:
        qi = lax.broadcasted_iota(jnp.int32, from_prev.shape, 0) % WINDOW
        absent = lax.broadcasted_iota(jnp.int32, from_prev.shape, 1) > jnp.where(no_prev, qi, WINDOW)
    out = []
    for g in range(N_KV_HEADS):
        sg = _fold(s, g, from_prev) * (HEAD_DIM ** -0.5)
        if no_prev is not None:
            sg = jnp.where(absent, NEG_INF, sg)
        sink = sink_ref[g]
        m = jnp.maximum(jnp.max(sg, axis=-1, keepdims=True), sink)
        out.append((jnp.exp(sg - m), jnp.exp(sink - m)))
    return out


def _spread_over_heads(cols, rows, KV):
    head = _kv_lane_head(rows, KV)
    out = jnp.zeros((rows, KV), F32)
    for g, col in enumerate(cols):
        out = jnp.where(head == g, col, out)
    return out


def _halo_rows_spec(tq, width, sub):
    return pl.BlockSpec((sub, width), lambda i: (jnp.maximum(i * (tq // sub) - 1, 0), 0))


def attn_fwd(q, k, v, sink_col, name, carry=None):
    T, D = q.shape
    KV = k.shape[1]
    group = D // KV
    tq = min(ROW_TILE, T)
    nsub = tq // WINDOW
    rows, wide = group * WINDOW, N_KV_HEADS * 2 * WINDOW

    def body(q_ref, k_ref, kh_ref, v_ref, vh_ref, sink_ref, o_ref):
        from_prev = _from_prev(rows)
        head = _kv_lane_head(wide, KV)
        block = lax.broadcasted_iota(jnp.int32, head.shape, 0) // (2 * WINDOW)
        ones_bd = jnp.where(head == block, 1.0, 0.0).astype(BF16)
        for j in range(nsub):
            k_win, v_win = _kv_windows(k_ref, kh_ref, v_ref, vh_ref, j)
            parts = _attn_exp(_stack_slots(q_ref, j, group, KV), _block_diag(k_win), sink_ref, from_prev,
                              pl.program_id(0) == 0 if j == 0 else None)
            p = _unfold([pg.astype(BF16) for pg, _ in parts], from_prev)
            both = _dot(p, jnp.concatenate([_block_diag(v_win), ones_bd], axis=1))
            denom = both[:, KV:] + _spread_over_heads([es for _, es in parts], rows, KV)
            out = (both[:, :KV] / denom).astype(BF16)
            for hh in range(group):
                o_ref[pl.ds(j * WINDOW, WINDOW), pl.ds(KV * hh, KV)] = out[hh * WINDOW:(hh + 1) * WINDOW]

    (o,), carried = _call(
        body, name=name, grid=(T // tq,),
        out_shape=(jax.ShapeDtypeStruct((T, D), BF16),),
        in_specs=[_rows(tq, D), _rows(tq, KV), _halo_rows_spec(tq, KV, WINDOW), _rows(tq, KV),
                  _halo_rows_spec(tq, KV, WINDOW), _resident(sink_col.shape)],
        out_specs=(_rows(tq, D),), args=(q, k, k, v, v, sink_col), carry=carry)
    return o, carried


def attn_bwd(q, k, v, do, sink_col, cos, sin_signed, name):
    T, D = q.shape
    KV = k.shape[1]
    group = D // KV
    tq = min(ROW_TILE, T)
    nsub = tq // WINDOW
    nt = T // tq
    scale = HEAD_DIM ** -0.5
    rows, wide = group * WINDOW, N_KV_HEADS * 2 * WINDOW

    def rev(i):
        return nt - 1 - i

    def body(q_ref, k_ref, kh_ref, v_ref, vh_ref, do_ref, sink_ref, cos_ref, sin_ref,
             dq_ref, dk_ref, dv_ref, dsink_ref, dq_acc, dk_acc, dv_acc, carry_k, carry_v):
        i = pl.program_id(0)

        @pl.when(i == 0)
        def _():
            carry_k[...] = jnp.zeros_like(carry_k)
            carry_v[...] = jnp.zeros_like(carry_v)
            dsink_ref[...] = jnp.zeros_like(dsink_ref)

        dk_acc[...] = jnp.zeros_like(dk_acc)
        dv_acc[...] = jnp.zeros_like(dv_acc)
        from_prev = _from_prev(rows)
        lane = lax.broadcasted_iota(jnp.int32, (1, LANES), 1)
        for j in range(nsub):
            k_win, v_win = _kv_windows(k_ref, kh_ref, v_ref, vh_ref, j)
            kbd, vbd = _block_diag(k_win), _block_diag(v_win)
            qs, dos = _stack_slots(q_ref, j, group, KV), _stack_slots(do_ref, j, group, KV)
            dp = _nt(dos, vbd)
            probs16, ds16 = [], []
            for g, (pg, es) in enumerate(_attn_exp(qs, kbd, sink_ref, from_prev, rev(i) == 0 if j == 0 else None)):
                inv = 1.0 / (jnp.sum(pg, axis=-1, keepdims=True) + es)
                probs = pg * inv
                dpg = _fold(dp, g, from_prev)
                delta = jnp.sum(probs * dpg, axis=-1, keepdims=True)
                probs16.append(probs.astype(BF16))
                ds16.append((probs * (dpg - delta) * scale).astype(BF16))
                dsk = -(es * inv * delta)
                for hh in range(group):
                    tot = jnp.sum(dsk[hh * WINDOW:(hh + 1) * WINDOW], axis=0, keepdims=True)
                    dsink_ref[pl.ds(hh, 1), :] += jnp.where(lane == g, tot, 0.0)
            ds = _unfold(ds16, from_prev)
            dqs = _dot(ds, kbd)
            for hh in range(group):
                dq_acc[pl.ds(j * WINDOW, WINDOW), pl.ds(KV * hh, KV)] = dqs[hh * WINDOW:(hh + 1) * WINDOW]
            keys = pl.ds(j * WINDOW, 2 * WINDOW)
            dk_acc[keys, :] += _diag_blocks_sum(_tn(ds, qs), 2 * WINDOW)
            dv_acc[keys, :] += _diag_blocks_sum(_tn(_unfold(probs16, from_prev), dos), 2 * WINDOW)

        tail = pl.ds(tq, WINDOW)
        dk_acc[tail, :] += carry_k[...]
        dv_acc[tail, :] += carry_v[...]
        carry_k[...] = dk_acc[pl.ds(0, WINDOW), :]
        carry_v[...] = dv_acc[pl.ds(0, WINDOW), :]
        cos, sin = cos_ref[...], sin_ref[...]
        dq_ref[...] = _rope_bwd(dq_acc[...], cos, sin).astype(BF16)
        dk_ref[...] = _rope_bwd(dk_acc[pl.ds(WINDOW, tq), :], cos, sin).astype(BF16)
        dv_ref[...] = dv_acc[pl.ds(WINDOW, tq), :].astype(BF16)

    def rrows(n):
        return pl.BlockSpec((tq, n), lambda i: (rev(i), 0))

    def rhalo(n):
        return pl.BlockSpec((WINDOW, n), lambda i: (jnp.maximum(rev(i) * nsub - 1, 0), 0))

    return pl.pallas_call(
        body, name=name, grid=(nt,),
        out_shape=(jax.ShapeDtypeStruct((T, D), BF16), jax.ShapeDtypeStruct((T, KV), BF16),
                   jax.ShapeDtypeStruct((T, KV), BF16), jax.ShapeDtypeStruct((SUBLANES, LANES), F32)),
        in_specs=[rrows(D), rrows(KV), rhalo(KV), rrows(KV), rhalo(KV), rrows(D), _resident(sink_col.shape),
                  rrows(LANES), rrows(LANES)],
        out_specs=(rrows(D), rrows(KV), rrows(KV), _acc_spec((SUBLANES, LANES))),
        scratch_shapes=[pltpu.VMEM((tq, D), F32), pltpu.VMEM((WINDOW + tq, KV), F32), pltpu.VMEM((WINDOW + tq, KV), F32),
                        pltpu.VMEM((WINDOW, KV), F32), pltpu.VMEM((WINDOW, KV), F32)],
        compiler_params=_params("arbitrary"),
    )(q, k, k, v, v, do, sink_col, cos, sin_signed)


def _ln_stats(u):
    mu = jnp.mean(u, axis=-1, keepdims=True)
    d = u - mu
    rstd = lax.rsqrt(jnp.mean(d * d, axis=-1, keepdims=True) + LN_EPS)
    return d * rstd, rstd


def _lag_taps(b, K):
    return [(a, K - 1 - (SUBLANES * a + b)) for a in range(-(-K // SUBLANES)) if SUBLANES * a + b <= K - 1]


def _conv_chunks(tm, D, chunk):
    def rows(c, carry):
        r0 = pl.multiple_of(c * CONV_ROWS, CONV_ROWS)
        for l0 in range(0, D, CONV_LANES):
            chunk(r0, pl.ds(l0, CONV_LANES))
        return carry
    lax.fori_loop(0, tm // CONV_ROWS, rows, 0)


def _conv_causal(buf, w_ref, bias_ref, out_ref, tm, D, K):
    def chunk(r0, lanes):
        acc = jnp.broadcast_to(bias_ref[:, lanes], (CONV_ROWS, CONV_LANES))
        for b in range(SUBLANES):
            y = None
            for a, k in _lag_taps(b, K):
                start = pl.multiple_of(r0 + CONV_HALO - SUBLANES * (a + 1), SUBLANES)
                t = buf[pl.ds(start, CONV_ROWS + SUBLANES), lanes] * w_ref[pl.ds(k, 1), lanes]
                y = t if y is None else y + t
            acc = acc + y[SUBLANES - b:SUBLANES - b + CONV_ROWS]
        out_ref[pl.ds(r0, CONV_ROWS), lanes] = acc
    _conv_chunks(tm, D, chunk)


def _conv_anticausal(dbuf, w_ref, out_ref, tm, D, K):
    def chunk(r0, lanes):
        acc = jnp.zeros((CONV_ROWS, CONV_LANES), F32)
        for b in range(SUBLANES):
            y = None
            for a, k in _lag_taps(b, K):
                start = pl.multiple_of(r0 + SUBLANES * a, SUBLANES)
                t = dbuf[pl.ds(start, CONV_ROWS + SUBLANES), lanes] * w_ref[pl.ds(k, 1), lanes]
                y = t if y is None else y + t
            acc = acc + y[b:b + CONV_ROWS]
        out_ref[pl.ds(r0, CONV_ROWS), lanes] = acc
    _conv_chunks(tm, D, chunk)


def _conv_tap_grads(dbuf, ubuf, acc_ref, tm, D, K):
    reach = SUBLANES * (-(-K // SUBLANES) - 1)

    def chunk(r0, lanes):
        d = dbuf[pl.ds(r0, CONV_ROWS), lanes]
        around = ubuf[pl.ds(pl.multiple_of(r0 + CONV_HALO - reach - SUBLANES, SUBLANES), CONV_ROWS + reach + SUBLANES), lanes]
        for b in range(SUBLANES):
            shifted = around[SUBLANES - b:SUBLANES - b + CONV_ROWS + reach]
            for a, k in _lag_taps(b, K):
                prod = d * shifted[reach - SUBLANES * a:reach - SUBLANES * a + CONV_ROWS]
                part = prod[0:SUBLANES]
                for i in range(1, CONV_ROWS // SUBLANES):
                    part = part + prod[SUBLANES * i:SUBLANES * (i + 1)]
                acc_ref[k, :, lanes] += part
    _conv_chunks(tm, D, chunk)


def mix_out_fwd(x, u0, o, sgc, sgt, dw_w, dw_b, ln_g, ln_b, wcp, wo, wout, name):
    T, D = x.shape
    tm = min(ROW_TILE, T)
    K = dw_w.shape[0]

    def body(x_ref, u_ref, uh_ref, o_ref, sgc_ref, sgt_ref, w_ref, b_ref, lg_ref, lb_ref, wcp_ref, wo_ref, wout_ref,
             x2_ref, u1_ref, co_ref, ao_ref, mg_ref, buf, conv):
        keep = (pl.program_id(0) > 0).astype(F32)
        buf[pl.ds(0, CONV_HALO), :] = uh_ref[...].astype(F32) * keep
        buf[pl.ds(CONV_HALO, tm), :] = u_ref[...].astype(F32)
        _conv_causal(buf, w_ref, b_ref, conv, tm, D, K)
        acc = conv[...]
        u1_ref[...] = acc.astype(BF16)
        xhat, _ = _ln_stats(acc)
        u2 = xhat * lg_ref[...] + lb_ref[...]
        u3 = (u2 * _sigmoid(u2)).astype(BF16)
        co = _dot(u3, wcp_ref[...])
        ao = _dot(o_ref[...], wo_ref[...])
        co_ref[...] = co.astype(BF16)
        ao_ref[...] = ao.astype(BF16)
        merged = (sgc_ref[...].astype(F32) * co + sgt_ref[...].astype(F32) * ao).astype(BF16)
        mg_ref[...] = merged
        x2_ref[...] = x_ref[...] + _dot(merged, wout_ref[...])

    big = jax.ShapeDtypeStruct((T, D), BF16)
    vec = _resident((1, D))
    return pl.pallas_call(
        body, name=name, grid=(T // tm,),
        out_shape=(jax.ShapeDtypeStruct((T, D), F32), big, big, big, big),
        in_specs=[_rows(tm, D), _rows(tm, D), _halo_rows_spec(tm, D, CONV_HALO), _rows(tm, D), _rows(tm, D), _rows(tm, D),
                  _resident((K, D)), vec, vec, vec, _resident((D, D)), _resident((D, D)), _resident((D, D))],
        out_specs=(_rows(tm, D),) * 5,
        scratch_shapes=[pltpu.VMEM((CONV_HALO + tm, D), F32), pltpu.VMEM((tm, D), F32)],
        compiler_params=_params("arbitrary"),
    )(x, u0, u0, o, sgc, sgt, dw_w, dw_b, ln_g, ln_b, wcp, wo, wout)


def mix_out_bwd(dx2, u1, co, ao, sgc, sgt, merged, ln_g, ln_b, wcp, wo, wout, name, carry=None):
    T, D = dx2.shape
    tm = min(ROW_TILE, T)
    nt = T // tm

    def body(dx_ref, u1_ref, co_ref, ao_ref, sgc_ref, sgt_ref, mg_ref, lg_ref, lb_ref, wcp_ref, wo_ref, wout_ref,
             dgc_ref, dgt_ref, do_ref, du1_ref, dao_ref, sums_ref, gwout_ref, gwcp_ref, acc_out, acc_cp):
        i = pl.program_id(0)

        @pl.when(i == 0)
        def _():
            sums_ref[...] = jnp.zeros_like(sums_ref)
            acc_out[...] = jnp.zeros_like(acc_out)
            acc_cp[...] = jnp.zeros_like(acc_cp)

        dx16 = dx_ref[...].astype(BF16)
        acc_out[...] += _tn(mg_ref[...], dx16)
        dm = _nt(dx16, wout_ref[...])
        sgc, sgt = sgc_ref[...].astype(F32), sgt_ref[...].astype(F32)
        dco = (dm * sgc).astype(BF16)
        dao = (dm * sgt).astype(BF16)
        dgc = dm * co_ref[...].astype(F32) * sgc * (1.0 - sgc)
        dgt = dm * ao_ref[...].astype(F32) * sgt * (1.0 - sgt)
        dao_ref[...] = dao
        dgc_ref[...] = dgc.astype(BF16)
        dgt_ref[...] = dgt.astype(BF16)
        do_ref[...] = _nt(dao, wo_ref[...]).astype(BF16)
        du3 = _nt(dco, wcp_ref[...])
        xhat, rstd = _ln_stats(u1_ref[...].astype(F32))
        g = lg_ref[...]
        u2 = xhat * g + lb_ref[...]
        su = _sigmoid(u2)
        acc_cp[...] += _tn((u2 * su).astype(BF16), dco)
        du2 = du3 * (su * (1.0 + u2 * (1.0 - su)))
        dxh = du2 * g
        du1 = rstd * (dxh - jnp.mean(dxh, axis=-1, keepdims=True) - xhat * jnp.mean(dxh * xhat, axis=-1, keepdims=True))
        du1_ref[...] = du1.astype(BF16)
        for r, val in enumerate((dgc, dgt, du2 * xhat, du2, du1)):
            sums_ref[pl.ds(r, 1), :] += jnp.sum(val, axis=0, keepdims=True)

        @pl.when(i == nt - 1)
        def _():
            gwout_ref[...] = acc_out[...].astype(BF16)
            gwcp_ref[...] = acc_cp[...].astype(BF16)

    big = jax.ShapeDtypeStruct((T, D), BF16)
    square = jax.ShapeDtypeStruct((D, D), BF16)
    vec = _resident((1, D))
    return _call(
        body, name=name, grid=(nt,),
        out_shape=(big,) * 5 + (jax.ShapeDtypeStruct((8, D), F32), square, square),
        in_specs=[_rows(tm, D)] * 7 + [vec, vec, _resident((D, D)), _resident((D, D)), _resident((D, D))],
        out_specs=(_rows(tm, D),) * 5 + (_acc_spec((8, D)), _resident((D, D)), _resident((D, D))),
        scratch_shapes=[pltpu.VMEM((D, D), F32), pltpu.VMEM((D, D), F32)],
        args=(dx2, u1, co, ao, sgc, sgt, merged, ln_g, ln_b, wcp, wo, wout), carry=carry)


def conv_bwd(du1, u0, ga, gb, dw_w, o, dao, name, carry=None):
    T, D = du1.shape
    tm = min(ROW_TILE, T)
    nt = T // tm
    K = dw_w.shape[0]
    per = tm // CONV_HALO

    def body(d_ref, dn_ref, u_ref, uh_ref, ga_ref, gb_ref, w_ref, o_ref, dao_ref,
             dga_ref, dgb_ref, dw_ref, gwo_ref, dbuf, ubuf, du0_buf, taps, wo_acc):
        i = pl.program_id(0)
        part = _tn(o_ref[...], dao_ref[...])

        @pl.when(i == 0)
        def _():
            wo_acc[...] = part

        @pl.when(i > 0)
        def _():
            wo_acc[...] += part

        @pl.when(i == nt - 1)
        def _():
            gwo_ref[...] = wo_acc[...].astype(BF16)

        dbuf[pl.ds(0, tm), :] = d_ref[...].astype(F32)
        dbuf[pl.ds(tm, CONV_HALO), :] = dn_ref[...].astype(F32) * (i < nt - 1).astype(F32)
        ubuf[pl.ds(0, CONV_HALO), :] = uh_ref[...].astype(F32) * (i > 0).astype(F32)
        ubuf[pl.ds(CONV_HALO, tm), :] = u_ref[...].astype(F32)

        @pl.when(i == 0)
        def _():
            taps[...] = jnp.zeros_like(taps)

        _conv_anticausal(dbuf, w_ref, du0_buf, tm, D, K)
        _conv_tap_grads(dbuf, ubuf, taps, tm, D, K)
        du0 = du0_buf[...]
        ga, gb = ga_ref[...].astype(F32), gb_ref[...].astype(F32)
        sg = _sigmoid(gb)
        dga_ref[...] = (du0 * sg).astype(BF16)
        dgb_ref[...] = (du0 * ga * sg * (1.0 - sg)).astype(BF16)

        @pl.when(i == nt - 1)
        def _():
            for k in range(K):
                dw_ref[pl.ds(k, 1), :] = jnp.sum(taps[k], axis=0, keepdims=True)

    nxt = pl.BlockSpec((CONV_HALO, D), lambda i: (jnp.minimum((i + 1) * per, nt * per - 1), 0))
    big = jax.ShapeDtypeStruct((T, D), BF16)
    return _call(
        body, name=name, grid=(nt,),
        out_shape=(big, big, jax.ShapeDtypeStruct((K, D), F32), jax.ShapeDtypeStruct((D, D), BF16)),
        in_specs=[_rows(tm, D), nxt, _rows(tm, D), _halo_rows_spec(tm, D, CONV_HALO), _rows(tm, D), _rows(tm, D),
                  _resident((K, D)), _rows(tm, D), _rows(tm, D)],
        out_specs=(_rows(tm, D), _rows(tm, D), _acc_spec((K, D)), _resident((D, D))),
        scratch_shapes=[pltpu.VMEM((tm + CONV_HALO, D), F32), pltpu.VMEM((CONV_HALO + tm, D), F32), pltpu.VMEM((tm, D), F32),
                        pltpu.VMEM((K, SUBLANES, D), F32), pltpu.VMEM((D, D), F32)],
        args=(du1, du1, u0, u0, ga, gb, dw_w, o, dao), carry=carry)


def mix_in_bwd(dx2, x, gain, wint, wq, pieces, name):
    T, D = x.shape
    tm = min(ROW_TILE, T)
    widths = [p.shape[1] for p in pieces]
    offs = _w_in_rows(D)

    def body(dx2_ref, x_ref, g_ref, w_ref, wq_ref, *rest):
        piece_refs, (dx_ref, dg_ref) = rest[:len(pieces)], rest[len(pieces):]
        dh = None
        for n, (p_ref, off, w) in enumerate(zip(piece_refs, offs, widths)):
            t = _dot(p_ref[...], wq_ref[...] if n == 2 else w_ref[pl.ds(off, w), :])
            dh = t if dh is None else dh + t
        dx, dgt = _rms_bwd(x_ref[...], g_ref[...], dh)
        dx_ref[...] = dx2_ref[...] + dx

        @pl.when(pl.program_id(0) == 0)
        def _():
            dg_ref[...] = jnp.zeros_like(dg_ref)
        dg_ref[...] += jnp.sum(dgt, axis=0, keepdims=True)

    return pl.pallas_call(
        body, name=name, grid=(T // tm,),
        out_shape=(jax.ShapeDtypeStruct((T, D), F32), jax.ShapeDtypeStruct((1, D), F32)),
        in_specs=[_rows(tm, D), _rows(tm, D), _resident((1, D)), _resident(wint.shape), _resident(wq.shape)]
        + [_rows(tm, w) for w in widths],
        out_specs=(_rows(tm, D), _acc_spec((1, D))),
        compiler_params=_params("arbitrary"),
    )(dx2, x, gain, wint, wq, *pieces)


def rope_tables(positions):
    half = HEAD_DIM // 2
    inv_freq = ROPE_THETA ** (-jnp.arange(half, dtype=F32) / half)
    ang = positions.astype(F32)[:, None] * inv_freq
    cos, sin = jnp.cos(ang), jnp.sin(ang)
    reps = LANES // HEAD_DIM
    return jnp.tile(jnp.concatenate([cos, cos], axis=-1), (1, reps)), jnp.tile(jnp.concatenate([-sin, sin], axis=-1), (1, reps))


def _place():
    return lax.axis_index("x"), lax.axis_index("y"), lax.axis_index("c")


def all_gather(blocks, name):
    n = len(blocks)
    send = gather_send(blocks)
    forward = gather_forward(send.out_shapes)
    n_sems = len(send.sems)

    def body(*refs):
        ins, outs, sems = refs[:n], refs[n:2 * n], refs[2 * n:]
        sent, landing, mine = send.copies(ins, outs, sems[:n_sems])
        passed, relanding = forward.copies((), outs, sems[n_sems:])
        for cp in mine + sent:
            cp.start()
        for j in range(3):
            for i in range(n):
                landing[4 * i + 1 + j].wait_recv()
                passed[3 * i + j].start()
        for i in range(n):
            landing[4 * i].wait_recv()
        for cp in relanding:
            cp.wait_recv()
        for cp in sent + passed:
            cp.wait_send()
        for cp in mine:
            cp.wait()

    return pl.pallas_call(
        body, name=name, out_shape=tuple(send.out_shapes), in_specs=[_ANY] * n, out_specs=(_ANY,) * n,
        scratch_shapes=list(send.sems) + list(forward.sems),
    )(*blocks)


def _chips_across(x, y):
    return [(1 - x, y), (x, 1 - y), (1 - x, 1 - y)]


def _dev_index(x, y, c):
    return 4 * x + 2 * y + c


def gather_send(blocks):
    n = len(blocks)

    def copies(in_refs, out_refs, sems):
        send, recv, local = sems
        x, y, c = _place()
        targets = [(x, y, 1 - c)] + [(*chip, c) for chip in _chips_across(x, y)]
        outgoing, incoming, mine = [], [], []
        for i, (x_ref, out_ref) in enumerate(zip(in_refs, out_refs)):
            for k, t in enumerate(targets):
                pair = dict(send_sem=send.at[4 * i + k], recv_sem=recv.at[4 * i + k], device_id=t, device_id_type=MESH)
                outgoing.append(pltpu.make_async_remote_copy(src_ref=x_ref, dst_ref=out_ref.at[_dev_index(x, y, c)], **pair))
                incoming.append(pltpu.make_async_remote_copy(src_ref=x_ref, dst_ref=out_ref.at[_dev_index(*t)], **pair))
            mine.append(pltpu.make_async_copy(x_ref, out_ref.at[_dev_index(x, y, c)], local.at[i]))
        return outgoing, incoming, mine

    def start(*refs):
        outgoing, _, mine = copies(*refs)
        for cp in mine + outgoing:
            cp.start()

    def finish(*refs):
        outgoing, incoming, mine = copies(*refs)
        for cp in incoming:
            cp.wait_recv()
        for cp in outgoing:
            cp.wait_send()
        for cp in mine:
            cp.wait()

    return Carry(ins=tuple(blocks), out_shapes=tuple(jax.ShapeDtypeStruct((N_DEV,) + b.shape, b.dtype) for b in blocks),
                 aliases={}, sems=(pltpu.SemaphoreType.DMA((4 * n,)), pltpu.SemaphoreType.DMA((4 * n,)),
                                   pltpu.SemaphoreType.DMA((n,))), start=start, finish=finish, copies=copies)


def gather_forward(gathered):
    n = len(gathered)

    def copies(in_refs, out_refs, sems):
        send, recv = sems
        x, y, c = _place()
        outgoing, incoming = [], []
        for i, buf in enumerate(out_refs):
            for k, chip in enumerate(_chips_across(x, y)):
                pair = dict(send_sem=send.at[3 * i + k], recv_sem=recv.at[3 * i + k], device_id=(x, y, 1 - c),
                            device_id_type=MESH)
                rows = buf.at[_dev_index(*chip, c)]
                outgoing.append(pltpu.make_async_remote_copy(src_ref=rows, dst_ref=rows, **pair))
                theirs = buf.at[_dev_index(*chip, 1 - c)]
                incoming.append(pltpu.make_async_remote_copy(src_ref=theirs, dst_ref=theirs, **pair))
        return outgoing, incoming

    def start(*refs):
        for cp in copies(*refs)[0]:
            cp.start()

    def finish(*refs):
        outgoing, incoming = copies(*refs)
        for cp in incoming:
            cp.wait_recv()
        for cp in outgoing:
            cp.wait_send()

    return Carry(ins=tuple(gathered), out_shapes=tuple(jax.ShapeDtypeStruct(g.shape, g.dtype) for g in gathered),
                 aliases={i: i for i in range(n)},
                 sems=(pltpu.SemaphoreType.DMA((3 * n,)), pltpu.SemaphoreType.DMA((3 * n,))), start=start, finish=finish,
                 copies=copies)


def gather_whole(blocks):
    send = gather_send(blocks)
    forward = gather_forward(send.out_shapes)
    n = len(send.sems)

    def relay(ins, outs, sems):
        send.finish(ins, outs, sems[:n])
        forward.start((), outs, sems[n:])

    return Carry(ins=send.ins, out_shapes=send.out_shapes, aliases={}, sems=send.sems + forward.sems,
                 start=lambda ins, outs, sems: send.start(ins, outs, sems[:n]), relay=relay,
                 finish=lambda ins, outs, sems: forward.finish((), outs, sems[n:]))


def compose(*carries):
    def split(refs, count):
        out, at = [], 0
        for c in carries:
            n = count(c)
            out.append(refs[at:at + n])
            at += n
        return out

    def each(stage):
        def run(ins, outs, sems):
            parts = zip(carries, split(ins, lambda c: len(c.ins)), split(outs, lambda c: len(c.out_shapes)),
                        split(sems, lambda c: len(c.sems)))
            for c, i, o, s in parts:
                if getattr(c, stage) is not None:
                    getattr(c, stage)(i, o, s)
        return run

    aliases, n_in, n_out = {}, 0, 0
    for c in carries:
        aliases.update({n_in + i: n_out + o for i, o in c.aliases.items()})
        n_in += len(c.ins)
        n_out += len(c.out_shapes)
    return Carry(ins=sum((tuple(c.ins) for c in carries), ()), out_shapes=sum((tuple(c.out_shapes) for c in carries), ()),
                 aliases=aliases, sems=sum((tuple(c.sems) for c in carries), ()), start=each("start"), finish=each("finish"),
                 relay=each("relay") if any(c.relay is not None for c in carries) else None)


def swap_halves(by_core):
    n = len(by_core)

    def copies(in_refs, out_refs, sems):
        send, recv = sems
        x, y, c = _place()
        return [pltpu.make_async_remote_copy(src_ref=a.at[:, 1 - c], dst_ref=r, send_sem=send.at[i], recv_sem=recv.at[i],
                                             device_id=(x, y, 1 - c), device_id_type=MESH)
                for i, (a, r) in enumerate(zip(in_refs, out_refs))]

    def start(*refs):
        for cp in copies(*refs):
            cp.start()

    def finish(*refs):
        for cp in copies(*refs):
            cp.wait()

    shapes = tuple(jax.ShapeDtypeStruct((a.shape[0],) + a.shape[2:], a.dtype) for a in by_core)
    return Carry(ins=tuple(by_core), out_shapes=shapes, aliases={},
                 sems=(pltpu.SemaphoreType.DMA((n,)), pltpu.SemaphoreType.DMA((n,))), start=start, finish=finish)


def exchange_between_chips(by_chip):
    n = len(by_chip)

    def copies(in_refs, out_refs, sems):
        send, recv = sems
        x, y, c = _place()
        out = []
        for i, (s, r) in enumerate(zip(in_refs, out_refs)):
            for k, (tx, ty) in enumerate(_chips_across(x, y)):
                out.append(pltpu.make_async_remote_copy(
                    src_ref=s.at[2 * tx + ty], dst_ref=r.at[k], send_sem=send.at[3 * i + k], recv_sem=recv.at[3 * i + k],
                    device_id=(tx, ty, c), device_id_type=MESH))
        return out

    def start(*refs):
        for cp in copies(*refs):
            cp.start()

    def finish(*refs):
        for cp in copies(*refs):
            cp.wait()

    shapes = tuple(jax.ShapeDtypeStruct((3,) + a.shape[1:], a.dtype) for a in by_chip)
    return Carry(ins=tuple(by_chip), out_shapes=shapes, aliases={},
                 sems=(pltpu.SemaphoreType.DMA((3 * n,)), pltpu.SemaphoreType.DMA((3 * n,))), start=start, finish=finish)


def run_exchange(carry, name):
    n_in = len(carry.ins)
    n_out = len(carry.out_shapes)

    def body(*refs):
        parts = refs[:n_in], refs[n_in:n_in + n_out], refs[n_in + n_out:]
        carry.start(*parts)
        carry.finish(*parts)

    return pl.pallas_call(
        body, name=name, out_shape=tuple(carry.out_shapes), in_specs=[_ANY] * n_in, out_specs=(_ANY,) * n_out,
        scratch_shapes=list(carry.sems), input_output_aliases=dict(carry.aliases),
    )(*carry.ins)


def pair_sum(my_core, by_core, received, name):
    n = len(by_core)

    def body(core_ref, *refs):
        for a_ref, b_ref, o_ref in zip(refs[:n], refs[n:2 * n], refs[2 * n:]):
            o_ref[0] = (a_ref[0, 0].astype(F32) + b_ref[0].astype(F32)).astype(BF16)

    mine = [pl.BlockSpec((1, 1) + a.shape[2:], lambda j, core: (j, core[0], 0, 0)) for a in by_core]
    theirs = [pl.BlockSpec((1,) + r.shape[1:], lambda j, core: (j, 0, 0)) for r in received]
    return pl.pallas_call(
        body, name=name, out_shape=tuple(jax.ShapeDtypeStruct(r.shape, BF16) for r in received),
        grid_spec=pltpu.PrefetchScalarGridSpec(num_scalar_prefetch=1, grid=(by_core[0].shape[0],), in_specs=mine + theirs,
                                               out_specs=tuple(theirs)),
        compiler_params=_params("arbitrary"),
    )(my_core, *by_core, *received)


def _adamw_math(w, g, m, v):
    m = ADAM_B1 * m + (1.0 - ADAM_B1) * g
    v = ADAM_B2 * v + (1.0 - ADAM_B2) * (g * g)
    m_hat = m / (1.0 - ADAM_B1 ** ADAM_STEP)
    v_hat = v / (1.0 - ADAM_B2 ** ADAM_STEP)
    delta = -ADAM_LR * (m_hat / (jnp.sqrt(v_hat) + ADAM_EPS) + ADAM_WD * w)
    return delta, m, v


def adamw(my_chip, ws, gs, ms, vs, name, carry=None):
    n = len(ws)
    flat, widths = [], []
    for g in gs:
        parts = list(g) if isinstance(g, (tuple, list)) else [g]
        flat += parts
        widths.append(len(parts))
    c_ins = list(carry.ins) if carry else []
    c_outs = list(carry.out_shapes) if carry else []

    def body(chip_ref, *refs):
        w_refs, refs = refs[:n], refs[n:]
        g_refs, refs = refs[:len(flat)], refs[len(flat):]
        m_refs, v_refs, refs = refs[:n], refs[n:2 * n], refs[2 * n:]
        carried = (refs[:len(c_ins)], refs[len(c_ins) + 4 * n:len(c_ins) + 4 * n + len(c_outs)],
                   refs[len(c_ins) + 4 * n + len(c_outs):])
        outs = refs[len(c_ins):]
        if carry:
            @pl.when(pl.program_id(0) == 0)
            def _():
                carry.start(*carried)
        at = 0
        for t in range(n):
            if widths[t] == 1:
                g = g_refs[at][...]
            else:
                g = g_refs[at][0].astype(F32)
                for k in range(3):
                    g = g + g_refs[at + 1][k].astype(F32)
            at += widths[t]
            outs[4 * t][...] = g
            outs[4 * t + 1][...], outs[4 * t + 2][...], outs[4 * t + 3][...] = _adamw_math(
                w_refs[t][...], g, m_refs[t][...], v_refs[t][...])
        if carry:
            @pl.when(pl.program_id(0) == ADAMW_STEPS - 1)
            def _():
                if carry.relay is not None:
                    carry.relay(*carried)
                carry.finish(*carried)

    def rows_of(a):
        return a.shape[-2] // ADAMW_STEPS if a.shape[-2] % (ADAMW_STEPS * SUBLANES) == 0 else None

    def whole(a):
        r = rows_of(a)
        lead = (0,) * (a.ndim - 2)
        if r is None:
            zeros = (0,) * a.ndim
            return pl.BlockSpec(a.shape, lambda i, chip: zeros, pipeline_mode=pl.Buffered(1))
        return pl.BlockSpec(a.shape[:-2] + (r, a.shape[-1]), lambda i, chip: lead + (i, 0))

    g_specs = []
    for g in gs:
        if isinstance(g, (tuple, list)):
            r = rows_of(g[0])
            g_specs += [pl.BlockSpec((1, r, g[0].shape[2]), lambda i, chip: (chip[0], i, 0)), whole(g[1])]
        else:
            g_specs.append(whole(g))
    shapes, out_specs = [], []
    for w in ws:
        shapes += [jax.ShapeDtypeStruct(w.shape, F32)] * 4
        out_specs += [whole(w)] * 4
    res = pl.pallas_call(
        body, name=name, out_shape=tuple(shapes) + tuple(c_outs),
        grid_spec=pltpu.PrefetchScalarGridSpec(
            num_scalar_prefetch=1, grid=(ADAMW_STEPS,),
            in_specs=[whole(w) for w in ws] + g_specs + [whole(a) for a in ms + vs] + [_ANY] * len(c_ins),
            out_specs=tuple(out_specs) + (_ANY,) * len(c_outs), scratch_shapes=list(carry.sems) if carry else []),
        compiler_params=_params("arbitrary"),
    )(my_chip, *ws, *flat, *ms, *vs, *c_ins)
    return [tuple(res[4 * t:4 * t + 4]) for t in range(n)], tuple(res[4 * n:])


def adamw_replicated(w, partials, m, v, name):
    def body(w_ref, p_ref, m_ref, v_ref, g_ref, d_ref, mo_ref, vo_ref):
        g = p_ref[0]
        for k in range(1, N_DEV):
            g = g + p_ref[k]
        g_ref[...] = g
        d_ref[...], mo_ref[...], vo_ref[...] = _adamw_math(w_ref[...], g, m_ref[...], v_ref[...])

    shape = jax.ShapeDtypeStruct(w.shape, F32)
    return pl.pallas_call(body, name=name, out_shape=(shape,) * 4, compiler_params=_params())(w, partials, m, v)


PACK_COLS = 1024
PACK_ROW_ALIGN = 16

REPLICATED = ("ffn1_norm", "mix_norm", "conv_dw_b", "conv_ln_g", "conv_ln_b", "ffn2_norm", "final_norm", "gate_b", "attn_sinks")
WEIGHT_ORDER = ("ffn1_norm", "ffn1_w_gate", "ffn1_w_up", "ffn1_w_down", "mix_norm", "w_in", "conv_dw_w", "conv_dw_b", "conv_ln_g",
                "conv_ln_b", "conv_w_proj", "attn_sinks", "attn_w_o", "gate_b", "w_out", "ffn2_norm", "ffn2_w_gate", "ffn2_w_up",
                "ffn2_w_down", "final_norm")


def _to_rows(flat, lead):
    n = flat.shape[-1]
    rows = -(-n // PACK_COLS)
    flat = jnp.pad(flat, [(0, 0)] * lead + [(0, rows * PACK_COLS - n)])
    return flat.reshape(flat.shape[:lead] + (rows, PACK_COLS))


def _pad_rows(a, axis):
    rows = a.shape[axis]
    pad = -rows % PACK_ROW_ALIGN
    widths = [(0, 0)] * a.ndim
    widths[axis] = (0, pad)
    return jnp.pad(a, widths)


def _from_rows(rows, shape):
    n = 1
    for s in shape:
        n *= s
    return rows.reshape(rows.shape[:-2] + (-1,))[..., :n].reshape(rows.shape[:-2] + tuple(shape))


def _heads_slot_major(rows):
    group = rows.shape[0] // (N_KV_HEADS * HEAD_DIM)
    return rows.reshape(N_KV_HEADS, group, HEAD_DIM, rows.shape[1]).transpose(1, 0, 2, 3).reshape(rows.shape)


def _heads_kv_major(rows):
    group = rows.shape[0] // (N_KV_HEADS * HEAD_DIM)
    return rows.reshape(group, N_KV_HEADS, HEAD_DIM, rows.shape[1]).transpose(1, 0, 2, 3).reshape(rows.shape)


def _by_core(full_rows):
    return full_rows.reshape((N_DEV // 2, 2, full_rows.shape[0] // N_DEV, full_rows.shape[1]))


def kernel(x, positions, ffn1_norm, ffn1_w_gate, ffn1_w_up, ffn1_w_down, mix_norm, w_in, conv_dw_w, conv_dw_b, conv_ln_g, conv_ln_b, conv_w_proj, attn_sinks, attn_w_o, gate_b, w_out, ffn2_norm, ffn2_w_gate, ffn2_w_up, ffn2_w_down, final_norm, loss_target, m_ffn1_norm, m_ffn1_w_gate, m_ffn1_w_up, m_ffn1_w_down, m_mix_norm, m_w_in, m_conv_dw_w, m_conv_dw_b, m_conv_ln_g, m_conv_ln_b, m_conv_w_proj, m_attn_sinks, m_attn_w_o, m_gate_b, m_w_out, m_ffn2_norm, m_ffn2_w_gate, m_ffn2_w_up, m_ffn2_w_down, m_final_norm, v_ffn1_norm, v_ffn1_w_gate, v_ffn1_w_up, v_ffn1_w_down, v_mix_norm, v_w_in, v_conv_dw_w, v_conv_dw_b, v_conv_ln_g, v_conv_ln_b, v_conv_w_proj, v_attn_sinks, v_attn_w_o, v_gate_b, v_w_out, v_ffn2_norm, v_ffn2_w_gate, v_ffn2_w_up, v_ffn2_w_down, v_final_norm):
    given = dict(locals())
    shapes = {n: given[n].shape for n in WEIGHT_ORDER}
    w = {n: given[n].reshape(given[n].shape[-2:]) if given[n].ndim == 3 else given[n].reshape(1, -1) for n in WEIGHT_ORDER}
    m = {n: given["m_" + n].reshape(w[n].shape) for n in WEIGHT_ORDER}
    v = {n: given["v_" + n].reshape(w[n].shape) for n in WEIGHT_ORDER}
    my_x, my_y, my_c = _place()
    my_core = my_c.astype(jnp.int32).reshape(1)
    my_chip = (2 * my_x + my_y).astype(jnp.int32).reshape(1)
    xs, target = x[0], loss_target[0]
    T, D = xs.shape
    KV = N_KV_HEADS * HEAD_DIM
    K = w["conv_dw_w"].shape[0]

    def t16(n):
        return w[n].T.astype(BF16)

    def r16(n):
        return w[n].astype(BF16)

    blocks1 = [t16("ffn1_w_gate"), t16("ffn1_w_up"), r16("ffn1_w_down")]
    dw_bits = _pad_rows(_to_rows(lax.bitcast_convert_type(w["conv_dw_w"], BF16).reshape(-1), 0), 0)
    blocks2 = [t16("w_in"), r16("conv_w_proj"), r16("attn_w_o"), r16("w_out"), dw_bits]
    blocks3 = [t16("ffn2_w_gate"), t16("ffn2_w_up"), r16("ffn2_w_down")]
    cos, sin = rope_tables(positions[0])
    sink_col = jnp.repeat(w["attn_sinks"].reshape(-1), WINDOW).reshape(N_KV_HEADS, (D // KV) * WINDOW, 1)

    def full(gathered):
        return gathered.reshape(-1, gathered.shape[2])

    wgt1, wut1 = (full(g) for g in all_gather(blocks1[:2], "gather_ffn1_up"))
    (h1, a1, b1, s1), got = ffn_up(xs, w["ffn1_norm"], wgt1, wut1, "ffn1_up",
                                   carry=compose(gather_whole(blocks1[2:]), gather_send(blocks2[:1])))
    wd1 = full(got[0])
    (x1,), got = ffn_down(xs, s1, wd1, "ffn1_down", carry=compose(gather_forward(got[1:]), gather_send(blocks2[1:])))
    wint = full(got[0])
    wq = _heads_slot_major(wint[2 * D:3 * D])
    (h2, ga, gb, u0, q, sgc, sgt, kk, vv), got = mix_in_fwd(x1, w["mix_norm"], wint, wq, w["gate_b"], cos, sin, "mix_in_fwd",
                                                            carry=compose(gather_forward(got[1:]), gather_send(blocks3)))
    wcp, wo, wout = (full(g) for g in got[:3])
    wo = _heads_slot_major(wo)
    dw_full = lax.bitcast_convert_type(_from_rows(got[3], w["conv_dw_w"].shape + (2,)), F32)
    dw_full = dw_full.transpose(1, 0, 2).reshape(K, D)
    o, gath3 = attn_fwd(q, kk, vv, sink_col, "attn_fwd", carry=gather_forward(got[4:]))
    x2, u1, co, ao, merged = mix_out_fwd(x1, u0, o, sgc, sgt, dw_full, w["conv_dw_b"], w["conv_ln_g"], w["conv_ln_b"],
                                         wcp, wo, wout, "mix_out_fwd")
    wgt2, wut2, wd2 = (full(g) for g in gath3)
    loss, dx3, d_final, h3, a2, b2, s2 = ffn_fwd_loss(x2, w["ffn2_norm"], wgt2, wut2, wd2, w["final_norm"], target,
                                                      "ffn2_fwd_loss")

    small = {"final_norm": d_final}
    (da2, db2), _ = ffn_bwd_hidden(dx3, a2, b2, wd2, "ffn2_bwd_hidden")
    (dx2, small["ffn2_norm"]), _ = ffn_bwd_input(dx3, x2, w["ffn2_norm"], da2, db2, wgt2, wut2, "ffn2_bwd_input")
    core2 = [_by_core(g) for g in (wgrad(da2, h3, "ffn2_dwg")[0], wgrad(db2, h3, "ffn2_dwu")[0],
                                   wgrad(s2, dx3, "ffn2_dwd", b_scale=FFN_SCALE)[0])]
    (dgc, dgt, do, du1, dao, sums, g_wout, g_wcp), recv2 = mix_out_bwd(
        dx2, u1, co, ao, sgc, sgt, merged, w["conv_ln_g"], w["conv_ln_b"], wcp, wo, wout, "mix_out_bwd",
        carry=swap_halves(core2))
    chip2 = pair_sum(my_core, core2, recv2, "ffn2_grads_pair_sum")
    small["gate_b"] = jnp.concatenate([sums[0:1], sums[1:2]], axis=1)
    small["conv_ln_g"], small["conv_ln_b"], small["conv_dw_b"] = sums[2:3], sums[3:4], sums[4:5]
    (dga, dgb, g_dw, g_wo), got2 = conv_bwd(du1, u0, ga, gb, dw_full, o, dao, "conv_bwd",
                                            carry=exchange_between_chips(chip2))
    g_wo = _heads_kv_major(g_wo)
    dq, dk, dv, dsink = attn_bwd(q, kk, vv, do, sink_col, cos, sin, "attn_bwd")
    small["attn_sinks"] = dsink[:D // KV, :N_KV_HEADS].T.reshape(1, -1)
    pieces = [dga, dgb, dq, dk, dv, dgc, dgt]
    dx1, small["mix_norm"] = mix_in_bwd(dx2, x1, w["mix_norm"], wint, wq, pieces, "mix_in_bwd")
    group = D // KV
    q_moves = [(2 * D + HEAD_DIM * (group * g + hh), 2 * D + HEAD_DIM * (N_KV_HEADS * hh + g), HEAD_DIM)
               for g in range(N_KV_HEADS) for hh in range(group)]
    g_wint = wgrad_stacked(pieces[:3], h2, "dw_in_a", wint.shape[0], 0, moves=[(0, 0, 2 * D)] + q_moves)
    g_wint = wgrad_stacked(pieces[3:], h2, "dw_in_b", wint.shape[0], 3 * D, into=g_wint)
    corem = [_by_core(a) for a in (g_wint, g_wcp, g_wo, g_wout)]

    (da1, db1), recvm = ffn_bwd_hidden(dx1, a1, b1, wd1, "ffn1_bwd_hidden", carry=swap_halves(corem))
    chipm = pair_sum(my_core, corem, recvm, "mix_grads_pair_sum")
    g1c, gotm_a = wgrad(s1, dx1, "ffn1_dwd", carry=exchange_between_chips(chipm[:1]), b_scale=FFN_SCALE)
    g1a, gotm_b = wgrad(da1, h1, "ffn1_dwg", carry=exchange_between_chips(chipm[1:]))
    early = [_by_core(g1a), _by_core(g1c)]
    g1b, recv_early = wgrad(db1, h1, "ffn1_dwu", carry=swap_halves(early))
    (recv_late,) = run_exchange(swap_halves([_by_core(g1b)]), "ffn1_grads_swap")
    core1 = [early[0], _by_core(g1b), early[1]]
    chip1 = pair_sum(my_core, core1, [recv_early[0], recv_late, recv_early[1]], "ffn1_grads_pair_sum")
    (grad_x, small["ffn1_norm"]), got1 = ffn_bwd_input(dx1, xs, w["ffn1_norm"], da1, db1, wgt1, wut1, "ffn1_bwd_input",
                                                       carry=exchange_between_chips(chip1))

    gotm = gotm_a + gotm_b
    grad_src = {"ffn1_w_gate": (chip1[0], got1[0]), "ffn1_w_up": (chip1[1], got1[1]), "ffn1_w_down": (chip1[2], got1[2]),
                "ffn2_w_gate": (chip2[0], got2[0]), "ffn2_w_up": (chip2[1], got2[1]), "ffn2_w_down": (chip2[2], got2[2]),
                "w_in": (chipm[0], gotm[0]), "conv_w_proj": (chipm[1], gotm[1]), "attn_w_o": (chipm[2], gotm[2]),
                "w_out": (chipm[3], gotm[3])}
    grads = {}

    def pack_small(d, taps, extra):
        rows = [_to_rows(d[n].reshape(-1), 0) for n in REPLICATED] + [taps, _to_rows(extra.reshape(-1), 0)]
        return _pad_rows(jnp.concatenate(rows, axis=0), 0)

    def like(a, ref):
        return a if a.shape == ref.shape else a.T

    def adamw_group(names, name, carry=None):
        refs = [grad_src[n][0][0] if isinstance(grad_src[n], tuple) else grad_src[n] for n in names]
        res, carried = adamw(my_chip, [like(w[n], r) for n, r in zip(names, refs)], [grad_src[n] for n in names],
                             [like(m[n], r) for n, r in zip(names, refs)], [like(v[n], r) for n, r in zip(names, refs)], name,
                             carry=carry)
        for n, outs in zip(names, res):
            grads[n], delta[n], new_m[n], new_v[n] = (like(a, w[n]) for a in outs)
        return carried

    delta, new_m, new_v = {}, {}, {}
    zero, no_taps = jnp.zeros((1, LANES), F32), jnp.zeros((K, D), F32)
    (shares,) = adamw_group(("ffn2_w_gate", "ffn2_w_up", "ffn2_w_down"), "adamw_ffn2",
                            carry=gather_whole([pack_small(small, g_dw, loss)]))
    g_s, d_s, m_s, v_s = adamw_replicated(pack_small(w, no_taps, zero), shares, pack_small(m, no_taps, zero),
                                          pack_small(v, no_taps, zero), "adamw_replicated")
    off = 0
    for n in REPLICATED:
        r = -(-w[n].shape[1] // PACK_COLS)
        grads[n], delta[n], new_m[n], new_v[n] = (_from_rows(a[off:off + r], w[n].shape) for a in (g_s, d_s, m_s, v_s))
        off += r
    shard_cols = w["conv_dw_w"].shape[1]
    grad_src["conv_dw_w"] = lax.dynamic_slice_in_dim(g_s[off:off + K], _dev_index(my_x, my_y, my_c) * shard_cols, shard_cols,
                                                    axis=1)
    total_loss = g_s[off + K, 0]
    adamw_group(("ffn1_w_gate", "ffn1_w_up", "ffn1_w_down"), "adamw_ffn1")
    adamw_group(("w_in", "conv_dw_w", "conv_w_proj", "attn_w_o", "w_out"), "adamw_mix")

    out = [total_loss, grad_x[None]]
    for d in (grads, delta, new_m, new_v):
        out += [d[n].reshape(shapes[n]) for n in WEIGHT_ORDER]
    return tuple(out)
```

```python
import functools
from typing import Callable, NamedTuple

import jax
import jax.numpy as jnp
from jax import lax
from jax.experimental import pallas as pl
from jax.experimental.pallas import tpu as pltpu

F32, BF16 = jnp.float32, jnp.bfloat16

HEAD_DIM = 64
N_KV_HEADS = 4
WINDOW = 128
ROPE_THETA = 10000.0
EPS = 1e-6
LN_EPS = 1e-5
NEG_INF = -1e30
ADAM_LR, ADAM_B1, ADAM_B2, ADAM_EPS, ADAM_WD, ADAM_STEP = 0.001, 0.9, 0.999, 1e-08, 0.01, 10
ADAMW_STEPS = 4

N_DEV = 8
LANES = 128
SUBLANES = 8
CONV_HALO = 32
CONV_ROWS, CONV_LANES = 128, 128
ROW_TILE = 512
FFN_CHUNK = 256
FFN_SCALE = 0.5
RING_SLOTS = 3
WGRAD_TILE_ELEMS = 2 ** 22
WGRAD_TILE_ROWS = 2048
WGRAD_VMEM = 40 * 2 ** 20
VMEM_LIMIT = 56 * 2 ** 20
MESH = pl.DeviceIdType.MESH


def _params(*sem):
    return pltpu.CompilerParams(dimension_semantics=sem or None, vmem_limit_bytes=VMEM_LIMIT)


def _resident(shape):
    zeros = (0,) * len(shape)
    return pl.BlockSpec(shape, lambda *_: zeros, pipeline_mode=pl.Buffered(1))


def _rows(tm, n):
    return pl.BlockSpec((tm, n), lambda i: (i, 0))


def _acc_spec(shape):
    zeros = (0,) * len(shape)
    return pl.BlockSpec(shape, lambda *_: zeros)


_ANY = pl.BlockSpec(memory_space=pl.ANY)


class Carry(NamedTuple):
    ins: tuple
    out_shapes: tuple
    aliases: dict
    sems: tuple
    start: Callable
    finish: Callable
    relay: Callable = None
    copies: Callable = None


def _call(body, *, name, grid, in_specs, out_specs, out_shape, args, scratch_shapes=(), carry=None):
    n_in, n_out, n_scr = len(in_specs), len(out_specs), len(scratch_shapes)
    params = _params(*(("arbitrary",) * len(grid)))
    if carry is None:
        res = pl.pallas_call(body, name=name, grid=grid, in_specs=list(in_specs), out_specs=tuple(out_specs),
                             out_shape=tuple(out_shape), scratch_shapes=list(scratch_shapes), compiler_params=params)(*args)
        return tuple(res), ()
    c_in, c_out = len(carry.ins), len(carry.out_shapes)

    def wrapped(*refs):
        ins, c_ins = refs[:n_in], refs[n_in:n_in + c_in]
        p = n_in + c_in
        outs, c_outs = refs[p:p + n_out], refs[p + n_out:p + n_out + c_out]
        p += n_out + c_out
        scr, c_sems = refs[p:p + n_scr], refs[p + n_scr:]
        ids = [pl.program_id(d) for d in range(len(grid))]
        first = functools.reduce(jnp.logical_and, [i == 0 for i in ids])
        last = functools.reduce(jnp.logical_and, [i == n - 1 for i, n in zip(ids, grid)])

        @pl.when(first)
        def _():
            carry.start(c_ins, c_outs, c_sems)

        body(*ins, *outs, *scr)

        if carry.relay is not None:
            @pl.when(ids[0] == (3 * grid[0]) // 4)
            def _():
                carry.relay(c_ins, c_outs, c_sems)

        @pl.when(last)
        def _():
            carry.finish(c_ins, c_outs, c_sems)

    res = pl.pallas_call(
        wrapped, name=name, grid=grid, in_specs=list(in_specs) + [_ANY] * c_in, out_specs=tuple(out_specs) + (_ANY,) * c_out,
        out_shape=tuple(out_shape) + tuple(carry.out_shapes), scratch_shapes=list(scratch_shapes) + list(carry.sems),
        input_output_aliases={n_in + i: n_out + o for i, o in carry.aliases.items()}, compiler_params=params,
    )(*args, *carry.ins)
    return tuple(res[:n_out]), tuple(res[n_out:])


def _nt(a, b):
    return lax.dot_general(a, b, (((1,), (1,)), ((), ())), preferred_element_type=F32)


def _tn(a, b):
    return lax.dot_general(a, b, (((0,), (0,)), ((), ())), preferred_element_type=F32)


def _dot(a, b):
    return jnp.dot(a, b, preferred_element_type=F32)


def _sigmoid(x):
    return 1.0 / (1.0 + jnp.exp(-x))


def _rms_fwd(x, g):
    r = lax.rsqrt(jnp.mean(x * x, axis=-1, keepdims=True) + EPS)
    return (x * r) * g


def _rms_bwd(x, g, dy):
    r = lax.rsqrt(jnp.mean(x * x, axis=-1, keepdims=True) + EPS)
    xhat = x * r
    dyg = dy * g
    dx = r * (dyg - xhat * jnp.mean(dyg * xhat, axis=-1, keepdims=True))
    return dx, dy * xhat


def _rot_half(x):
    lane = lax.broadcasted_iota(jnp.int32, (x.shape[0], LANES), 1)
    first = (lane % HEAD_DIM) < (HEAD_DIM // 2)
    out = []
    for s in range(x.shape[1] // LANES):
        xs = x[:, LANES * s:LANES * (s + 1)]
        out.append(jnp.where(first, pltpu.roll(xs, LANES - HEAD_DIM // 2, 1), pltpu.roll(xs, HEAD_DIM // 2, 1)))
    return out[0] if len(out) == 1 else jnp.concatenate(out, axis=1)


def _tile_lanes(t, width):
    return t if width == LANES else jnp.concatenate([t] * (width // LANES), axis=1)


def _rope_fwd(x, cos, sin_signed):
    w = x.shape[1]
    return x * _tile_lanes(cos, w) + _rot_half(x) * _tile_lanes(sin_signed, w)


def _rope_bwd(dy, cos, sin_signed):
    w = dy.shape[1]
    return dy * _tile_lanes(cos, w) + _rot_half(dy * _tile_lanes(sin_signed, w))


def _ffn_rows(x, g_ref, wg_ref, wu_ref, wd_ref, h_ref, a_ref, b_ref, s_ref, acc_ref):
    F = wg_ref.shape[0]
    h = _rms_fwd(x, g_ref[...]).astype(BF16)
    h_ref[...] = h
    for c in range(F // FFN_CHUNK):
        cs = pl.ds(c * FFN_CHUNK, FFN_CHUNK)
        a = _nt(h, wg_ref[cs, :])
        b = _nt(h, wu_ref[cs, :])
        a_ref[:, cs] = a.astype(BF16)
        b_ref[:, cs] = b.astype(BF16)
        s = (a * _sigmoid(a) * b).astype(BF16)
        s_ref[:, cs] = s
        y = _dot(s, wd_ref[cs, :])
        if c == 0:
            acc_ref[...] = y
        else:
            acc_ref[...] += y
    return x + FFN_SCALE * acc_ref[...]


def ffn_up(x, gain, wgt, wut, name, carry=None):
    T, D = x.shape
    F = wgt.shape[0]
    tm = min(ROW_TILE, T)

    def body(x_ref, g_ref, wg_ref, wu_ref, h_ref, a_ref, b_ref, s_ref):
        h = _rms_fwd(x_ref[...], g_ref[...]).astype(BF16)
        h_ref[...] = h
        for c in range(F // FFN_CHUNK):
            cs = pl.ds(c * FFN_CHUNK, FFN_CHUNK)
            a = _nt(h, wg_ref[cs, :])
            b = _nt(h, wu_ref[cs, :])
            a_ref[:, cs] = a.astype(BF16)
            b_ref[:, cs] = b.astype(BF16)
            s_ref[:, cs] = (a * _sigmoid(a) * b).astype(BF16)

    wide = jax.ShapeDtypeStruct((T, F), BF16)
    return _call(
        body, name=name, grid=(T // tm,), out_shape=(jax.ShapeDtypeStruct((T, D), BF16), wide, wide, wide),
        in_specs=[_rows(tm, D), _resident((1, D)), _resident((F, D)), _resident((F, D))],
        out_specs=(_rows(tm, D), _rows(tm, F), _rows(tm, F), _rows(tm, F)), args=(x, gain, wgt, wut), carry=carry)


def ffn_down(x, s, wd, name, carry=None):
    T, D = x.shape
    F = wd.shape[0]
    tm = min(2 * ROW_TILE, T)

    def body(x_ref, s_ref, wd_ref, xo_ref):
        xo_ref[...] = x_ref[...] + FFN_SCALE * _dot(s_ref[...], wd_ref[...])

    return _call(
        body, name=name, grid=(T // tm,), out_shape=(jax.ShapeDtypeStruct((T, D), F32),),
        in_specs=[_rows(tm, D), _rows(tm, F), _resident((F, D))], out_specs=(_rows(tm, D),), args=(x, s, wd), carry=carry)


def ffn_fwd_loss(x, gain, wgt, wut, wd, final_gain, target, name):
    T, D = x.shape
    F = wgt.shape[0]
    tm = min(ROW_TILE, T)

    def body(x_ref, g_ref, wg_ref, wu_ref, wd_ref, gf_ref, t_ref,
             loss_ref, dx_ref, dg_ref, h_ref, a_ref, b_ref, s_ref, acc_ref):
        xo = _ffn_rows(x_ref[...], g_ref, wg_ref, wu_ref, wd_ref, h_ref, a_ref, b_ref, s_ref, acc_ref)
        gf = gf_ref[...]
        err = _rms_fwd(xo, gf) - t_ref[...]
        dx, dgt = _rms_bwd(xo, gf, err * (1.0 / D))
        dx_ref[...] = dx

        @pl.when(pl.program_id(0) == 0)
        def _():
            dg_ref[...] = jnp.zeros_like(dg_ref)
            loss_ref[...] = jnp.zeros_like(loss_ref)
        dg_ref[...] += jnp.sum(dgt, axis=0, keepdims=True)
        per_token = jnp.sum(err * err, axis=-1, keepdims=True) * (0.5 / D)
        loss_ref[...] += jnp.broadcast_to(jnp.sum(per_token, axis=0, keepdims=True), (1, LANES))

    return pl.pallas_call(
        body, name=name, grid=(T // tm,),
        out_shape=(jax.ShapeDtypeStruct((1, LANES), F32), jax.ShapeDtypeStruct((T, D), F32), jax.ShapeDtypeStruct((1, D), F32),
                   jax.ShapeDtypeStruct((T, D), BF16), jax.ShapeDtypeStruct((T, F), BF16), jax.ShapeDtypeStruct((T, F), BF16),
                   jax.ShapeDtypeStruct((T, F), BF16)),
        in_specs=[_rows(tm, D), _resident((1, D)), _resident((F, D)), _resident((F, D)), _resident((F, D)), _resident((1, D)),
                  _rows(tm, D)],
        out_specs=(_acc_spec((1, LANES)), _rows(tm, D), _acc_spec((1, D)), _rows(tm, D), _rows(tm, F), _rows(tm, F),
                   _rows(tm, F)),
        scratch_shapes=[pltpu.VMEM((tm, D), F32)], compiler_params=_params("arbitrary"),
    )(x, gain, wgt, wut, wd, final_gain, target)


def ffn_bwd_hidden(dxo, a, b, wd, name, carry=None):
    T, D = dxo.shape
    F = wd.shape[0]
    tm = min(ROW_TILE, T)
    nt = T // tm
    fc = FFN_CHUNK

    def hidden_body(dxo_ref, a_hbm, b_hbm, wd_ref, da_ref, db_ref, a_buf, b_buf, a_sems, b_sems):
        i = pl.program_id(0)

        def fetch(step, slot):
            rows = pl.ds(pl.multiple_of(step * tm, tm), tm)
            return (pltpu.make_async_copy(a_hbm.at[rows, :], a_buf.at[slot], a_sems.at[slot]),
                    pltpu.make_async_copy(b_hbm.at[rows, :], b_buf.at[slot], b_sems.at[slot]))

        @pl.when(i == 0)
        def _():
            for step in range(min(RING_SLOTS - 1, nt)):
                for cp in fetch(step, step):
                    cp.start()

        @pl.when(i + RING_SLOTS - 1 < nt)
        def _():
            for cp in fetch(i + RING_SLOTS - 1, (i + RING_SLOTS - 1) % RING_SLOTS):
                cp.start()

        slot = i % RING_SLOTS
        for cp in fetch(i, slot):
            cp.wait()
        g0 = (FFN_SCALE * dxo_ref[...]).astype(BF16)
        for c in range(F // fc):
            cs = pl.ds(c * fc, fc)
            ds = _nt(g0, wd_ref[cs, :])
            a = a_buf[slot, :, cs].astype(F32)
            bb = b_buf[slot, :, cs].astype(F32)
            sa = _sigmoid(a)
            da_ref[:, cs] = (ds * bb * (sa * (1.0 + a * (1.0 - sa)))).astype(BF16)
            db_ref[:, cs] = (ds * (a * sa)).astype(BF16)

    wide = jax.ShapeDtypeStruct((T, F), BF16)
    return _call(
        hidden_body, name=name, grid=(nt,), out_shape=(wide, wide),
        in_specs=[_rows(tm, D), _ANY, _ANY, _resident((F, D))],
        out_specs=(_rows(tm, F), _rows(tm, F)),
        scratch_shapes=[pltpu.VMEM((RING_SLOTS, tm, F), BF16), pltpu.VMEM((RING_SLOTS, tm, F), BF16),
                        pltpu.SemaphoreType.DMA((RING_SLOTS,)), pltpu.SemaphoreType.DMA((RING_SLOTS,))],
        args=(dxo, a, b, wd), carry=carry)


def ffn_bwd_input(dxo, x, gain, da, db, wg, wu, name, carry=None):
    T, D = x.shape
    F = wg.shape[0]
    tm = min(ROW_TILE, T)

    def input_body(dxo_ref, x_ref, g_ref, da_ref, db_ref, wg_ref, wu_ref, dx_ref, dg_ref):
        dh = _dot(da_ref[...], wg_ref[...]) + _dot(db_ref[...], wu_ref[...])
        dx, dgt = _rms_bwd(x_ref[...], g_ref[...], dh)
        dx_ref[...] = dxo_ref[...] + dx

        @pl.when(pl.program_id(0) == 0)
        def _():
            dg_ref[...] = jnp.zeros_like(dg_ref)
        dg_ref[...] += jnp.sum(dgt, axis=0, keepdims=True)

    return _call(
        input_body, name=name, grid=(T // tm,),
        out_shape=(jax.ShapeDtypeStruct((T, D), F32), jax.ShapeDtypeStruct((1, D), F32)),
        in_specs=[_rows(tm, D), _rows(tm, D), _resident((1, D)), _rows(tm, F), _rows(tm, F), _resident((F, D)),
                  _resident((F, D))],
        out_specs=(_rows(tm, D), _acc_spec((1, D))), args=(dxo, x, gain, da, db, wg, wu), carry=carry)


def wgrad(a, b, name, carry=None, b_scale=None):
    T, M = a.shape
    N = b.shape[1]
    fixed = M * N * (4 + 2)
    per_row = 2 * (M * a.dtype.itemsize + N * b.dtype.itemsize)
    tk = ROW_TILE
    while fixed + 2 * tk * per_row <= WGRAD_VMEM and 2 * tk <= WGRAD_TILE_ROWS:
        tk *= 2
    tk = min(tk, T)
    nk = T // tk

    def body(a_ref, b_ref, o_ref, acc_ref):
        k = pl.program_id(0)
        bt = b_ref[...] if b_scale is None else b_scale * b_ref[...]
        part = _tn(a_ref[...].astype(BF16), bt.astype(BF16))

        @pl.when(k == 0)
        def _():
            acc_ref[...] = part

        @pl.when(k > 0)
        def _():
            acc_ref[...] += part

        @pl.when(k == nk - 1)
        def _():
            o_ref[...] = acc_ref[...].astype(BF16)

    (out,), carried = _call(
        body, name=name, grid=(nk,), out_shape=(jax.ShapeDtypeStruct((M, N), BF16),),
        in_specs=[_rows(tk, M), _rows(tk, N)], out_specs=(_resident((M, N)),),
        scratch_shapes=[pltpu.VMEM((M, N), F32)], args=(a, b), carry=carry)
    return out, carried


def wgrad_stacked(pieces, b, name, total_rows, at, into=None, moves=None):
    T, N = b.shape
    n = len(pieces)
    widths = [p.shape[1] for p in pieces]
    offs = [sum(widths[:i]) for i in range(n)]
    M = sum(widths)
    tk = ROW_TILE
    while 2 * tk * M <= WGRAD_TILE_ELEMS and 2 * tk <= WGRAD_TILE_ROWS:
        tk *= 2
    tk = min(tk, T)
    nk = T // tk
    moves = moves or [(0, 0, M)]
    n_in = n + 1 + (into is not None)

    def body(*refs):
        a_refs, b_ref = refs[:n], refs[n]
        o_ref, acc_ref, stage, sem = refs[n_in:]
        k = pl.program_id(0)

        @pl.when(k == 0)
        def _():
            acc_ref[...] = jnp.zeros_like(acc_ref)
        bt = b_ref[...].astype(BF16)
        for a_ref, off, width in zip(a_refs, offs, widths):
            acc_ref[pl.ds(off, width), :] += _tn(a_ref[...].astype(BF16), bt)

        @pl.when(k == nk - 1)
        def _():
            for to, start, rows in moves:
                stage[pl.ds(to, rows), :] = acc_ref[pl.ds(start, rows), :].astype(BF16)
            cp = pltpu.make_async_copy(stage, o_ref.at[pl.ds(at, M)], sem)
            cp.start()
            cp.wait()

    return pl.pallas_call(
        body, name=name, grid=(nk,), out_shape=jax.ShapeDtypeStruct((total_rows, N), BF16),
        in_specs=[_rows(tk, width) for width in widths] + [_rows(tk, N)] + [_ANY] * (into is not None), out_specs=_ANY,
        scratch_shapes=[pltpu.VMEM((M, N), F32), pltpu.VMEM((M, N), BF16), pltpu.SemaphoreType.DMA],
        input_output_aliases={n + 1: 0} if into is not None else {}, compiler_params=_params("arbitrary"),
    )(*pieces, b, *([into] if into is not None else []))


def _w_in_rows(D):
    KV = N_KV_HEADS * HEAD_DIM
    return 0, D, 2 * D, 3 * D, 3 * D + KV, 3 * D + 2 * KV, 4 * D + 2 * KV


def mix_in_fwd(x, gain, wint, wq, gate_b, cos, sin_signed, name, carry=None):
    T, D = x.shape
    KV = N_KV_HEADS * HEAD_DIM
    tm = min(ROW_TILE, T)
    o_ga, o_gb, _, o_k, o_v, o_gc, o_gt = _w_in_rows(D)

    def body(x_ref, g_ref, w_ref, wq_ref, gb_ref, cos_ref, sin_ref,
             h_ref, ga_ref, gb_out_ref, u0_ref, q_ref, sgc_ref, sgt_ref, k_ref, v_ref):
        h = _rms_fwd(x_ref[...], g_ref[...]).astype(BF16)
        h_ref[...] = h
        cos, sin = cos_ref[...], sin_ref[...]
        ga = _nt(h, w_ref[pl.ds(o_ga, D), :])
        gb = _nt(h, w_ref[pl.ds(o_gb, D), :])
        ga_ref[...] = ga.astype(BF16)
        gb_out_ref[...] = gb.astype(BF16)
        u0_ref[...] = (ga * _sigmoid(gb)).astype(BF16)
        q = _nt(h, wq_ref[...])
        q_ref[...] = _rope_fwd(q, cos, sin).astype(BF16)
        gc = _nt(h, w_ref[pl.ds(o_gc, D), :]) + gb_ref[:, pl.ds(0, D)]
        sgc_ref[...] = _sigmoid(gc).astype(BF16)
        gt = _nt(h, w_ref[pl.ds(o_gt, D), :]) + gb_ref[:, pl.ds(D, D)]
        sgt_ref[...] = _sigmoid(gt).astype(BF16)
        k = _nt(h, w_ref[pl.ds(o_k, KV), :])
        k_ref[...] = _rope_fwd(k, cos, sin).astype(BF16)
        v_ref[...] = _nt(h, w_ref[pl.ds(o_v, KV), :]).astype(BF16)

    big = jax.ShapeDtypeStruct((T, D), BF16)
    small = jax.ShapeDtypeStruct((T, KV), BF16)
    return _call(
        body, name=name, grid=(T // tm,),
        out_shape=(big, big, big, big, big, big, big, small, small),
        in_specs=[_rows(tm, D), _resident((1, D)), _resident(wint.shape), _resident(wq.shape), _resident((1, 2 * D)),
                  _rows(tm, LANES), _rows(tm, LANES)],
        out_specs=(_rows(tm, D),) * 7 + (_rows(tm, KV),) * 2,
        args=(x, gain, wint, wq, gate_b, cos, sin_signed), carry=carry)


def _from_prev(rows):
    qi = lax.broadcasted_iota(jnp.int32, (rows, WINDOW), 0) % WINDOW
    return lax.broadcasted_iota(jnp.int32, (rows, WINDOW), 1) > qi


def _fold(x, g, from_prev):
    lo = 2 * WINDOW * g
    return jnp.where(from_prev, x[:, lo:lo + WINDOW], x[:, lo + WINDOW:lo + 2 * WINDOW])


def _unfold(folded, from_prev):
    zero = jnp.zeros_like(folded[0])
    parts = []
    for x in folded:
        parts += [jnp.where(from_prev, x, zero), jnp.where(from_prev, zero, x)]
    return jnp.concatenate(parts, axis=1)


def _kv_lane_head(rows, width):
    return lax.broadcasted_iota(jnp.int32, (rows, width), 1) // HEAD_DIM


def _block_diag(win):
    head = _kv_lane_head(*win.shape)
    zero = jnp.zeros_like(win)
    return jnp.concatenate([jnp.where(head == g, win, zero) for g in range(N_KV_HEADS)], axis=0)


def _diag_blocks_sum(bd, keys):
    head = _kv_lane_head(keys, bd.shape[1])
    out = jnp.zeros((keys, bd.shape[1]), F32)
    for g in range(N_KV_HEADS):
        out = jnp.where(head == g, bd[g * keys:(g + 1) * keys], out)
    return out


def _kv_windows(k_ref, kh_ref, v_ref, vh_ref, j):
    rows = pl.ds(j * WINDOW, WINDOW)
    if j == 0:
        kprev, vprev = kh_ref[...], vh_ref[...]
    else:
        prev = pl.ds((j - 1) * WINDOW, WINDOW)
        kprev, vprev = k_ref[prev, :], v_ref[prev, :]
    return jnp.concatenate([kprev, k_ref[rows, :]], axis=0), jnp.concatenate([vprev, v_ref[rows, :]], axis=0)


def _stack_slots(ref, j, group, KV):
    rows = pl.ds(j * WINDOW, WINDOW)
    return jnp.concatenate([ref[rows, pl.ds(KV * hh, KV)] for hh in range(group)], axis=0)


def _attn_exp(qs, kbd, sink_ref, from_prev, no_prev):
    s = _nt(qs, kbd)
    if no_prev is not None:
        qi = lax.broadcasted_iota(jnp.int32, from_prev.shape, 0) % WINDOW
        absent = lax.broadcasted_iota(jnp.int32, from_prev.shape, 1) > jnp.where(no_prev, qi, WINDOW)
    out = []
    for g in range(N_KV_HEADS):
        sg = _fold(s, g, from_prev) * (HEAD_DIM ** -0.5)
        if no_prev is not None:
            sg = jnp.where(absent, NEG_INF, sg)
        sink = sink_ref[g]
        m = jnp.maximum(jnp.max(sg, axis=-1, keepdims=True), sink)
        out.append((jnp.exp(sg - m), jnp.exp(sink - m)))
    return out


def _spread_over_heads(cols, rows, KV):
    head = _kv_lane_head(rows, KV)
    out = jnp.zeros((rows, KV), F32)
    for g, col in enumerate(cols):
        out = jnp.where(head == g, col, out)
    return out


def _halo_rows_spec(tq, width, sub):
    return pl.BlockSpec((sub, width), lambda i: (jnp.maximum(i * (tq // sub) - 1, 0), 0))


def attn_fwd(q, k, v, sink_col, name, carry=None):
    T, D = q.shape
    KV = k.shape[1]
    group = D // KV
    tq = min(ROW_TILE, T)
    nsub = tq // WINDOW
    rows, wide = group * WINDOW, N_KV_HEADS * 2 * WINDOW

    def body(q_ref, k_ref, kh_ref, v_ref, vh_ref, sink_ref, o_ref):
        from_prev = _from_prev(rows)
        head = _kv_lane_head(wide, KV)
        block = lax.broadcasted_iota(jnp.int32, head.shape, 0) // (2 * WINDOW)
        ones_bd = jnp.where(head == block, 1.0, 0.0).astype(BF16)
        for j in range(nsub):
            k_win, v_win = _kv_windows(k_ref, kh_ref, v_ref, vh_ref, j)
            parts = _attn_exp(_stack_slots(q_ref, j, group, KV), _block_diag(k_win), sink_ref, from_prev,
                              pl.program_id(0) == 0 if j == 0 else None)
            p = _unfold([pg.astype(BF16) for pg, _ in parts], from_prev)
            both = _dot(p, jnp.concatenate([_block_diag(v_win), ones_bd], axis=1))
            denom = both[:, KV:] + _spread_over_heads([es for _, es in parts], rows, KV)
            out = (both[:, :KV] / denom).astype(BF16)
            for hh in range(group):
                o_ref[pl.ds(j * WINDOW, WINDOW), pl.ds(KV * hh, KV)] = out[hh * WINDOW:(hh + 1) * WINDOW]

    (o,), carried = _call(
        body, name=name, grid=(T // tq,),
        out_shape=(jax.ShapeDtypeStruct((T, D), BF16),),
        in_specs=[_rows(tq, D), _rows(tq, KV), _halo_rows_spec(tq, KV, WINDOW), _rows(tq, KV),
                  _halo_rows_spec(tq, KV, WINDOW), _resident(sink_col.shape)],
        out_specs=(_rows(tq, D),), args=(q, k, k, v, v, sink_col), carry=carry)
    return o, carried


def attn_bwd(q, k, v, do, sink_col, cos, sin_signed, name):
    T, D = q.shape
    KV = k.shape[1]
    group = D // KV
    tq = min(ROW_TILE, T)
    nsub = tq // WINDOW
    nt = T // tq
    scale = HEAD_DIM ** -0.5
    rows, wide = group * WINDOW, N_KV_HEADS * 2 * WINDOW

    def rev(i):
        return nt - 1 - i

    def body(q_ref, k_ref, kh_ref, v_ref, vh_ref, do_ref, sink_ref, cos_ref, sin_ref,
             dq_ref, dk_ref, dv_ref, dsink_ref, dq_acc, dk_acc, dv_acc, carry_k, carry_v):
        i = pl.program_id(0)

        @pl.when(i == 0)
        def _():
            carry_k[...] = jnp.zeros_like(carry_k)
            carry_v[...] = jnp.zeros_like(carry_v)
            dsink_ref[...] = jnp.zeros_like(dsink_ref)

        dk_acc[...] = jnp.zeros_like(dk_acc)
        dv_acc[...] = jnp.zeros_like(dv_acc)
        from_prev = _from_prev(rows)
        lane = lax.broadcasted_iota(jnp.int32, (1, LANES), 1)
        for j in range(nsub):
            k_win, v_win = _kv_windows(k_ref, kh_ref, v_ref, vh_ref, j)
            kbd, vbd = _block_diag(k_win), _block_diag(v_win)
            qs, dos = _stack_slots(q_ref, j, group, KV), _stack_slots(do_ref, j, group, KV)
            dp = _nt(dos, vbd)
            probs16, ds16 = [], []
            for g, (pg, es) in enumerate(_attn_exp(qs, kbd, sink_ref, from_prev, rev(i) == 0 if j == 0 else None)):
                inv = 1.0 / (jnp.sum(pg, axis=-1, keepdims=True) + es)
                probs = pg * inv
                dpg = _fold(dp, g, from_prev)
                delta = jnp.sum(probs * dpg, axis=-1, keepdims=True)
                probs16.append(probs.astype(BF16))
                ds16.append((probs * (dpg - delta) * scale).astype(BF16))
                dsk = -(es * inv * delta)
                for hh in range(group):
                    tot = jnp.sum(dsk[hh * WINDOW:(hh + 1) * WINDOW], axis=0, keepdims=True)
                    dsink_ref[pl.ds(hh, 1), :] += jnp.where(lane == g, tot, 0.0)
            ds = _unfold(ds16, from_prev)
            dqs = _dot(ds, kbd)
            for hh in range(group):
                dq_acc[pl.ds(j * WINDOW, WINDOW), pl.ds(KV * hh, KV)] = dqs[hh * WINDOW:(hh + 1) * WINDOW]
            keys = pl.ds(j * WINDOW, 2 * WINDOW)
            dk_acc[keys, :] += _diag_blocks_sum(_tn(ds, qs), 2 * WINDOW)
            dv_acc[keys, :] += _diag_blocks_sum(_tn(_unfold(probs16, from_prev), dos), 2 * WINDOW)

        tail = pl.ds(tq, WINDOW)
        dk_acc[tail, :] += carry_k[...]
        dv_acc[tail, :] += carry_v[...]
        carry_k[...] = dk_acc[pl.ds(0, WINDOW), :]
        carry_v[...] = dv_acc[pl.ds(0, WINDOW), :]
        cos, sin = cos_ref[...], sin_ref[...]
        dq_ref[...] = _rope_bwd(dq_acc[...], cos, sin).astype(BF16)
        dk_ref[...] = _rope_bwd(dk_acc[pl.ds(WINDOW, tq), :], cos, sin).astype(BF16)
        dv_ref[...] = dv_acc[pl.ds(WINDOW, tq), :].astype(BF16)

    def rrows(n):
        return pl.BlockSpec((tq, n), lambda i: (rev(i), 0))

    def rhalo(n):
        return pl.BlockSpec((WINDOW, n), lambda i: (jnp.maximum(rev(i) * nsub - 1, 0), 0))

    return pl.pallas_call(
        body, name=name, grid=(nt,),
        out_shape=(jax.ShapeDtypeStruct((T, D), BF16), jax.ShapeDtypeStruct((T, KV), BF16),
                   jax.ShapeDtypeStruct((T, KV), BF16), jax.ShapeDtypeStruct((SUBLANES, LANES), F32)),
        in_specs=[rrows(D), rrows(KV), rhalo(KV), rrows(KV), rhalo(KV), rrows(D), _resident(sink_col.shape),
                  rrows(LANES), rrows(LANES)],
        out_specs=(rrows(D), rrows(KV), rrows(KV), _acc_spec((SUBLANES, LANES))),
        scratch_shapes=[pltpu.VMEM((tq, D), F32), pltpu.VMEM((WINDOW + tq, KV), F32), pltpu.VMEM((WINDOW + tq, KV), F32),
                        pltpu.VMEM((WINDOW, KV), F32), pltpu.VMEM((WINDOW, KV), F32)],
        compiler_params=_params("arbitrary"),
    )(q, k, k, v, v, do, sink_col, cos, sin_signed)


def _ln_stats(u):
    mu = jnp.mean(u, axis=-1, keepdims=True)
    d = u - mu
    rstd = lax.rsqrt(jnp.mean(d * d, axis=-1, keepdims=True) + LN_EPS)
    return d * rstd, rstd


def _lag_taps(b, K):
    return [(a, K - 1 - (SUBLANES * a + b)) for a in range(-(-K // SUBLANES)) if SUBLANES * a + b <= K - 1]


def _conv_chunks(tm, D, chunk):
    def rows(c, carry):
        r0 = pl.multiple_of(c * CONV_ROWS, CONV_ROWS)
        for l0 in range(0, D, CONV_LANES):
            chunk(r0, pl.ds(l0, CONV_LANES))
        return carry
    lax.fori_loop(0, tm // CONV_ROWS, rows, 0)


def _conv_causal(buf, w_ref, bias_ref, out_ref, tm, D, K):
    def chunk(r0, lanes):
        acc = jnp.broadcast_to(bias_ref[:, lanes], (CONV_ROWS, CONV_LANES))
        for b in range(SUBLANES):
            y = None
            for a, k in _lag_taps(b, K):
                start = pl.multiple_of(r0 + CONV_HALO - SUBLANES * (a + 1), SUBLANES)
                t = buf[pl.ds(start, CONV_ROWS + SUBLANES), lanes] * w_ref[pl.ds(k, 1), lanes]
                y = t if y is None else y + t
            acc = acc + y[SUBLANES - b:SUBLANES - b + CONV_ROWS]
        out_ref[pl.ds(r0, CONV_ROWS), lanes] = acc
    _conv_chunks(tm, D, chunk)


def _conv_anticausal(dbuf, w_ref, out_ref, tm, D, K):
    def chunk(r0, lanes):
        acc = jnp.zeros((CONV_ROWS, CONV_LANES), F32)
        for b in range(SUBLANES):
            y = None
            for a, k in _lag_taps(b, K):
                start = pl.multiple_of(r0 + SUBLANES * a, SUBLANES)
                t = dbuf[pl.ds(start, CONV_ROWS + SUBLANES), lanes] * w_ref[pl.ds(k, 1), lanes]
                y = t if y is None else y + t
            acc = acc + y[b:b + CONV_ROWS]
        out_ref[pl.ds(r0, CONV_ROWS), lanes] = acc
    _conv_chunks(tm, D, chunk)


def _conv_tap_grads(dbuf, ubuf, acc_ref, tm, D, K):
    reach = SUBLANES * (-(-K // SUBLANES) - 1)

    def chunk(r0, lanes):
        d = dbuf[pl.ds(r0, CONV_ROWS), lanes]
        around = ubuf[pl.ds(pl.multiple_of(r0 + CONV_HALO - reach - SUBLANES, SUBLANES), CONV_ROWS + reach + SUBLANES), lanes]
        for b in range(SUBLANES):
            shifted = around[SUBLANES - b:SUBLANES - b + CONV_ROWS + reach]
            for a, k in _lag_taps(b, K):
                prod = d * shifted[reach - SUBLANES * a:reach - SUBLANES * a + CONV_ROWS]
                part = prod[0:SUBLANES]
                for i in range(1, CONV_ROWS // SUBLANES):
                    part = part + prod[SUBLANES * i:SUBLANES * (i + 1)]
                acc_ref[k, :, lanes] += part
    _conv_chunks(tm, D, chunk)


def mix_out_fwd(x, u0, o, sgc, sgt, dw_w, dw_b, ln_g, ln_b, wcp, wo, wout, name):
    T, D = x.shape
    tm = min(ROW_TILE, T)
    K = dw_w.shape[0]

    def body(x_ref, u_ref, uh_ref, o_ref, sgc_ref, sgt_ref, w_ref, b_ref, lg_ref, lb_ref, wcp_ref, wo_ref, wout_ref,
             x2_ref, u1_ref, co_ref, ao_ref, mg_ref, buf, conv):
        keep = (pl.program_id(0) > 0).astype(F32)
        buf[pl.ds(0, CONV_HALO), :] = uh_ref[...].astype(F32) * keep
        buf[pl.ds(CONV_HALO, tm), :] = u_ref[...].astype(F32)
        _conv_causal(buf, w_ref, b_ref, conv, tm, D, K)
        acc = conv[...]
        u1_ref[...] = acc.astype(BF16)
        xhat, _ = _ln_stats(acc)
        u2 = xhat * lg_ref[...] + lb_ref[...]
        u3 = (u2 * _sigmoid(u2)).astype(BF16)
        co = _dot(u3, wcp_ref[...])
        ao = _dot(o_ref[...], wo_ref[...])
        co_ref[...] = co.astype(BF16)
        ao_ref[...] = ao.astype(BF16)
        merged = (sgc_ref[...].astype(F32) * co + sgt_ref[...].astype(F32) * ao).astype(BF16)
        mg_ref[...] = merged
        x2_ref[...] = x_ref[...] + _dot(merged, wout_ref[...])

    big = jax.ShapeDtypeStruct((T, D), BF16)
    vec = _resident((1, D))
    return pl.pallas_call(
        body, name=name, grid=(T // tm,),
        out_shape=(jax.ShapeDtypeStruct((T, D), F32), big, big, big, big),
        in_specs=[_rows(tm, D), _rows(tm, D), _halo_rows_spec(tm, D, CONV_HALO), _rows(tm, D), _rows(tm, D), _rows(tm, D),
                  _resident((K, D)), vec, vec, vec, _resident((D, D)), _resident((D, D)), _resident((D, D))],
        out_specs=(_rows(tm, D),) * 5,
        scratch_shapes=[pltpu.VMEM((CONV_HALO + tm, D), F32), pltpu.VMEM((tm, D), F32)],
        compiler_params=_params("arbitrary"),
    )(x, u0, u0, o, sgc, sgt, dw_w, dw_b, ln_g, ln_b, wcp, wo, wout)


def mix_out_bwd(dx2, u1, co, ao, sgc, sgt, merged, ln_g, ln_b, wcp, wo, wout, name, carry=None):
    T, D = dx2.shape
    tm = min(ROW_TILE, T)
    nt = T // tm

    def body(dx_ref, u1_ref, co_ref, ao_ref, sgc_ref, sgt_ref, mg_ref, lg_ref, lb_ref, wcp_ref, wo_ref, wout_ref,
             dgc_ref, dgt_ref, do_ref, du1_ref, dao_ref, sums_ref, gwout_ref, gwcp_ref, acc_out, acc_cp):
        i = pl.program_id(0)

        @pl.when(i == 0)
        def _():
            sums_ref[...] = jnp.zeros_like(sums_ref)
            acc_out[...] = jnp.zeros_like(acc_out)
            acc_cp[...] = jnp.zeros_like(acc_cp)

        dx16 = dx_ref[...].astype(BF16)
        acc_out[...] += _tn(mg_ref[...], dx16)
        dm = _nt(dx16, wout_ref[...])
        sgc, sgt = sgc_ref[...].astype(F32), sgt_ref[...].astype(F32)
        dco = (dm * sgc).astype(BF16)
        dao = (dm * sgt).astype(BF16)
        dgc = dm * co_ref[...].astype(F32) * sgc * (1.0 - sgc)
        dgt = dm * ao_ref[...].astype(F32) * sgt * (1.0 - sgt)
        dao_ref[...] = dao
        dgc_ref[...] = dgc.astype(BF16)
        dgt_ref[...] = dgt.astype(BF16)
        do_ref[...] = _nt(dao, wo_ref[...]).astype(BF16)
        du3 = _nt(dco, wcp_ref[...])
        xhat, rstd = _ln_stats(u1_ref[...].astype(F32))
        g = lg_ref[...]
        u2 = xhat * g + lb_ref[...]
        su = _sigmoid(u2)
        acc_cp[...] += _tn((u2 * su).astype(BF16), dco)
        du2 = du3 * (su * (1.0 + u2 * (1.0 - su)))
        dxh = du2 * g
        du1 = rstd * (dxh - jnp.mean(dxh, axis=-1, keepdims=True) - xhat * jnp.mean(dxh * xhat, axis=-1, keepdims=True))
        du1_ref[...] = du1.astype(BF16)
        for r, val in enumerate((dgc, dgt, du2 * xhat, du2, du1)):
            sums_ref[pl.ds(r, 1), :] += jnp.sum(val, axis=0, keepdims=True)

        @pl.when(i == nt - 1)
        def _():
            gwout_ref[...] = acc_out[...].astype(BF16)
            gwcp_ref[...] = acc_cp[...].astype(BF16)

    big = jax.ShapeDtypeStruct((T, D), BF16)
    square = jax.ShapeDtypeStruct((D, D), BF16)
    vec = _resident((1, D))
    return _call(
        body, name=name, grid=(nt,),
        out_shape=(big,) * 5 + (jax.ShapeDtypeStruct((8, D), F32), square, square),
        in_specs=[_rows(tm, D)] * 7 + [vec, vec, _resident((D, D)), _resident((D, D)), _resident((D, D))],
        out_specs=(_rows(tm, D),) * 5 + (_acc_spec((8, D)), _resident((D, D)), _resident((D, D))),
        scratch_shapes=[pltpu.VMEM((D, D), F32), pltpu.VMEM((D, D), F32)],
        args=(dx2, u1, co, ao, sgc, sgt, merged, ln_g, ln_b, wcp, wo, wout), carry=carry)


def conv_bwd(du1, u0, ga, gb, dw_w, name, carry=None):
    T, D = du1.shape
    tm = min(ROW_TILE, T)
    nt = T // tm
    K = dw_w.shape[0]
    per = tm // CONV_HALO

    def body(d_ref, dn_ref, u_ref, uh_ref, ga_ref, gb_ref, w_ref, dga_ref, dgb_ref, dw_ref, dbuf, ubuf, du0_buf, taps):
        i = pl.program_id(0)
        dbuf[pl.ds(0, tm), :] = d_ref[...].astype(F32)
        dbuf[pl.ds(tm, CONV_HALO), :] = dn_ref[...].astype(F32) * (i < nt - 1).astype(F32)
        ubuf[pl.ds(0, CONV_HALO), :] = uh_ref[...].astype(F32) * (i > 0).astype(F32)
        ubuf[pl.ds(CONV_HALO, tm), :] = u_ref[...].astype(F32)

        @pl.when(i == 0)
        def _():
            taps[...] = jnp.zeros_like(taps)

        _conv_anticausal(dbuf, w_ref, du0_buf, tm, D, K)
        _conv_tap_grads(dbuf, ubuf, taps, tm, D, K)
        du0 = du0_buf[...]
        ga, gb = ga_ref[...].astype(F32), gb_ref[...].astype(F32)
        sg = _sigmoid(gb)
        dga_ref[...] = (du0 * sg).astype(BF16)
        dgb_ref[...] = (du0 * ga * sg * (1.0 - sg)).astype(BF16)

        @pl.when(i == nt - 1)
        def _():
            for k in range(K):
                dw_ref[pl.ds(k, 1), :] = jnp.sum(taps[k], axis=0, keepdims=True)

    nxt = pl.BlockSpec((CONV_HALO, D), lambda i: (jnp.minimum((i + 1) * per, nt * per - 1), 0))
    big = jax.ShapeDtypeStruct((T, D), BF16)
    return _call(
        body, name=name, grid=(nt,),
        out_shape=(big, big, jax.ShapeDtypeStruct((K, D), F32)),
        in_specs=[_rows(tm, D), nxt, _rows(tm, D), _halo_rows_spec(tm, D, CONV_HALO), _rows(tm, D), _rows(tm, D),
                  _resident((K, D))],
        out_specs=(_rows(tm, D), _rows(tm, D), _acc_spec((K, D))),
        scratch_shapes=[pltpu.VMEM((tm + CONV_HALO, D), F32), pltpu.VMEM((CONV_HALO + tm, D), F32), pltpu.VMEM((tm, D), F32),
                        pltpu.VMEM((K, SUBLANES, D), F32)],
        args=(du1, du1, u0, u0, ga, gb, dw_w), carry=carry)


def mix_in_bwd(dx2, x, gain, wint, wq, pieces, name):
    T, D = x.shape
    tm = min(ROW_TILE, T)
    widths = [p.shape[1] for p in pieces]
    offs = _w_in_rows(D)

    def body(dx2_ref, x_ref, g_ref, w_ref, wq_ref, *rest):
        piece_refs, (dx_ref, dg_ref) = rest[:len(pieces)], rest[len(pieces):]
        dh = None
        for n, (p_ref, off, w) in enumerate(zip(piece_refs, offs, widths)):
            t = _dot(p_ref[...], wq_ref[...] if n == 2 else w_ref[pl.ds(off, w), :])
            dh = t if dh is None else dh + t
        dx, dgt = _rms_bwd(x_ref[...], g_ref[...], dh)
        dx_ref[...] = dx2_ref[...] + dx

        @pl.when(pl.program_id(0) == 0)
        def _():
            dg_ref[...] = jnp.zeros_like(dg_ref)
        dg_ref[...] += jnp.sum(dgt, axis=0, keepdims=True)

    return pl.pallas_call(
        body, name=name, grid=(T // tm,),
        out_shape=(jax.ShapeDtypeStruct((T, D), F32), jax.ShapeDtypeStruct((1, D), F32)),
        in_specs=[_rows(tm, D), _rows(tm, D), _resident((1, D)), _resident(wint.shape), _resident(wq.shape)]
        + [_rows(tm, w) for w in widths],
        out_specs=(_rows(tm, D), _acc_spec((1, D))),
        compiler_params=_params("arbitrary"),
    )(dx2, x, gain, wint, wq, *pieces)


def rope_tables(positions):
    half = HEAD_DIM // 2
    inv_freq = ROPE_THETA ** (-jnp.arange(half, dtype=F32) / half)
    ang = positions.astype(F32)[:, None] * inv_freq
    cos, sin = jnp.cos(ang), jnp.sin(ang)
    reps = LANES // HEAD_DIM
    return jnp.tile(jnp.concatenate([cos, cos], axis=-1), (1, reps)), jnp.tile(jnp.concatenate([-sin, sin], axis=-1), (1, reps))


def _place():
    return lax.axis_index("x"), lax.axis_index("y"), lax.axis_index("c")


def all_gather(blocks, name):
    n = len(blocks)
    send = gather_send(blocks)
    forward = gather_forward(send.out_shapes)
    n_sems = len(send.sems)

    def body(*refs):
        ins, outs, sems = refs[:n], refs[n:2 * n], refs[2 * n:]
        sent, landing, mine = send.copies(ins, outs, sems[:n_sems])
        passed, relanding = forward.copies((), outs, sems[n_sems:])
        for cp in mine + sent:
            cp.start()
        for j in range(3):
            for i in range(n):
                landing[4 * i + 1 + j].wait_recv()
                passed[3 * i + j].start()
        for i in range(n):
            landing[4 * i].wait_recv()
        for cp in relanding:
            cp.wait_recv()
        for cp in sent + passed:
            cp.wait_send()
        for cp in mine:
            cp.wait()

    return pl.pallas_call(
        body, name=name, out_shape=tuple(send.out_shapes), in_specs=[_ANY] * n, out_specs=(_ANY,) * n,
        scratch_shapes=list(send.sems) + list(forward.sems),
    )(*blocks)


def _chips_across(x, y):
    return [(1 - x, y), (x, 1 - y), (1 - x, 1 - y)]


def _dev_index(x, y, c):
    return 4 * x + 2 * y + c


def gather_send(blocks):
    n = len(blocks)

    def copies(in_refs, out_refs, sems):
        send, recv, local = sems
        x, y, c = _place()
        targets = [(x, y, 1 - c)] + [(*chip, c) for chip in _chips_across(x, y)]
        outgoing, incoming, mine = [], [], []
        for i, (x_ref, out_ref) in enumerate(zip(in_refs, out_refs)):
            for k, t in enumerate(targets):
                pair = dict(send_sem=send.at[4 * i + k], recv_sem=recv.at[4 * i + k], device_id=t, device_id_type=MESH)
                outgoing.append(pltpu.make_async_remote_copy(src_ref=x_ref, dst_ref=out_ref.at[_dev_index(x, y, c)], **pair))
                incoming.append(pltpu.make_async_remote_copy(src_ref=x_ref, dst_ref=out_ref.at[_dev_index(*t)], **pair))
            mine.append(pltpu.make_async_copy(x_ref, out_ref.at[_dev_index(x, y, c)], local.at[i]))
        return outgoing, incoming, mine

    def start(*refs):
        outgoing, _, mine = copies(*refs)
        for cp in mine + outgoing:
            cp.start()

    def finish(*refs):
        outgoing, incoming, mine = copies(*refs)
        for cp in incoming:
            cp.wait_recv()
        for cp in outgoing:
            cp.wait_send()
        for cp in mine:
            cp.wait()

    return Carry(ins=tuple(blocks), out_shapes=tuple(jax.ShapeDtypeStruct((N_DEV,) + b.shape, b.dtype) for b in blocks),
                 aliases={}, sems=(pltpu.SemaphoreType.DMA((4 * n,)), pltpu.SemaphoreType.DMA((4 * n,)),
                                   pltpu.SemaphoreType.DMA((n,))), start=start, finish=finish, copies=copies)


def gather_forward(gathered):
    n = len(gathered)

    def copies(in_refs, out_refs, sems):
        send, recv = sems
        x, y, c = _place()
        outgoing, incoming = [], []
        for i, buf in enumerate(out_refs):
            for k, chip in enumerate(_chips_across(x, y)):
                pair = dict(send_sem=send.at[3 * i + k], recv_sem=recv.at[3 * i + k], device_id=(x, y, 1 - c),
                            device_id_type=MESH)
                rows = buf.at[_dev_index(*chip, c)]
                outgoing.append(pltpu.make_async_remote_copy(src_ref=rows, dst_ref=rows, **pair))
                theirs = buf.at[_dev_index(*chip, 1 - c)]
                incoming.append(pltpu.make_async_remote_copy(src_ref=theirs, dst_ref=theirs, **pair))
        return outgoing, incoming

    def start(*refs):
        for cp in copies(*refs)[0]:
            cp.start()

    def finish(*refs):
        outgoing, incoming = copies(*refs)
        for cp in incoming:
            cp.wait_recv()
        for cp in outgoing:
            cp.wait_send()

    return Carry(ins=tuple(gathered), out_shapes=tuple(jax.ShapeDtypeStruct(g.shape, g.dtype) for g in gathered),
                 aliases={i: i for i in range(n)},
                 sems=(pltpu.SemaphoreType.DMA((3 * n,)), pltpu.SemaphoreType.DMA((3 * n,))), start=start, finish=finish,
                 copies=copies)


def gather_whole(blocks):
    send = gather_send(blocks)
    forward = gather_forward(send.out_shapes)
    n = len(send.sems)

    def relay(ins, outs, sems):
        send.finish(ins, outs, sems[:n])
        forward.start((), outs, sems[n:])

    return Carry(ins=send.ins, out_shapes=send.out_shapes, aliases={}, sems=send.sems + forward.sems,
                 start=lambda ins, outs, sems: send.start(ins, outs, sems[:n]), relay=relay,
                 finish=lambda ins, outs, sems: forward.finish((), outs, sems[n:]))


def compose(*carries):
    def split(refs, count):
        out, at = [], 0
        for c in carries:
            n = count(c)
            out.append(refs[at:at + n])
            at += n
        return out

    def each(stage):
        def run(ins, outs, sems):
            parts = zip(carries, split(ins, lambda c: len(c.ins)), split(outs, lambda c: len(c.out_shapes)),
                        split(sems, lambda c: len(c.sems)))
            for c, i, o, s in parts:
                if getattr(c, stage) is not None:
                    getattr(c, stage)(i, o, s)
        return run

    aliases, n_in, n_out = {}, 0, 0
    for c in carries:
        aliases.update({n_in + i: n_out + o for i, o in c.aliases.items()})
        n_in += len(c.ins)
        n_out += len(c.out_shapes)
    return Carry(ins=sum((tuple(c.ins) for c in carries), ()), out_shapes=sum((tuple(c.out_shapes) for c in carries), ()),
                 aliases=aliases, sems=sum((tuple(c.sems) for c in carries), ()), start=each("start"), finish=each("finish"),
                 relay=each("relay") if any(c.relay is not None for c in carries) else None)


def swap_halves(by_core):
    n = len(by_core)

    def copies(in_refs, out_refs, sems):
        send, recv = sems
        x, y, c = _place()
        return [pltpu.make_async_remote_copy(src_ref=a.at[:, 1 - c], dst_ref=r, send_sem=send.at[i], recv_sem=recv.at[i],
                                             device_id=(x, y, 1 - c), device_id_type=MESH)
                for i, (a, r) in enumerate(zip(in_refs, out_refs))]

    def start(*refs):
        for cp in copies(*refs):
            cp.start()

    def finish(*refs):
        for cp in copies(*refs):
            cp.wait()

    shapes = tuple(jax.ShapeDtypeStruct((a.shape[0],) + a.shape[2:], a.dtype) for a in by_core)
    return Carry(ins=tuple(by_core), out_shapes=shapes, aliases={},
                 sems=(pltpu.SemaphoreType.DMA((n,)), pltpu.SemaphoreType.DMA((n,))), start=start, finish=finish)


def exchange_between_chips(by_chip):
    n = len(by_chip)

    def copies(in_refs, out_refs, sems):
        send, recv = sems
        x, y, c = _place()
        out = []
        for i, (s, r) in enumerate(zip(in_refs, out_refs)):
            for k, (tx, ty) in enumerate(_chips_across(x, y)):
                out.append(pltpu.make_async_remote_copy(
                    src_ref=s.at[2 * tx + ty], dst_ref=r.at[k], send_sem=send.at[3 * i + k], recv_sem=recv.at[3 * i + k],
                    device_id=(tx, ty, c), device_id_type=MESH))
        return out

    def start(*refs):
        for cp in copies(*refs):
            cp.start()

    def finish(*refs):
        for cp in copies(*refs):
            cp.wait()

    shapes = tuple(jax.ShapeDtypeStruct((3,) + a.shape[1:], a.dtype) for a in by_chip)
    return Carry(ins=tuple(by_chip), out_shapes=shapes, aliases={},
                 sems=(pltpu.SemaphoreType.DMA((3 * n,)), pltpu.SemaphoreType.DMA((3 * n,))), start=start, finish=finish)


def run_exchange(carry, name):
    n_in = len(carry.ins)
    n_out = len(carry.out_shapes)

    def body(*refs):
        parts = refs[:n_in], refs[n_in:n_in + n_out], refs[n_in + n_out:]
        carry.start(*parts)
        carry.finish(*parts)

    return pl.pallas_call(
        body, name=name, out_shape=tuple(carry.out_shapes), in_specs=[_ANY] * n_in, out_specs=(_ANY,) * n_out,
        scratch_shapes=list(carry.sems), input_output_aliases=dict(carry.aliases),
    )(*carry.ins)


def pair_sum(my_core, by_core, received, name):
    n = len(by_core)

    def body(core_ref, *refs):
        for a_ref, b_ref, o_ref in zip(refs[:n], refs[n:2 * n], refs[2 * n:]):
            o_ref[0] = (a_ref[0, 0].astype(F32) + b_ref[0].astype(F32)).astype(BF16)

    mine = [pl.BlockSpec((1, 1) + a.shape[2:], lambda j, core: (j, core[0], 0, 0)) for a in by_core]
    theirs = [pl.BlockSpec((1,) + r.shape[1:], lambda j, core: (j, 0, 0)) for r in received]
    return pl.pallas_call(
        body, name=name, out_shape=tuple(jax.ShapeDtypeStruct(r.shape, BF16) for r in received),
        grid_spec=pltpu.PrefetchScalarGridSpec(num_scalar_prefetch=1, grid=(by_core[0].shape[0],), in_specs=mine + theirs,
                                               out_specs=tuple(theirs)),
        compiler_params=_params("arbitrary"),
    )(my_core, *by_core, *received)


def _adamw_math(w, g, m, v):
    m = ADAM_B1 * m + (1.0 - ADAM_B1) * g
    v = ADAM_B2 * v + (1.0 - ADAM_B2) * (g * g)
    m_hat = m / (1.0 - ADAM_B1 ** ADAM_STEP)
    v_hat = v / (1.0 - ADAM_B2 ** ADAM_STEP)
    delta = -ADAM_LR * (m_hat / (jnp.sqrt(v_hat) + ADAM_EPS) + ADAM_WD * w)
    return delta, m, v


def adamw(my_chip, ws, gs, ms, vs, name, carry=None):
    n = len(ws)
    flat, widths = [], []
    for g in gs:
        parts = list(g) if isinstance(g, (tuple, list)) else [g]
        flat += parts
        widths.append(len(parts))
    c_ins = list(carry.ins) if carry else []
    c_outs = list(carry.out_shapes) if carry else []

    def body(chip_ref, *refs):
        w_refs, refs = refs[:n], refs[n:]
        g_refs, refs = refs[:len(flat)], refs[len(flat):]
        m_refs, v_refs, refs = refs[:n], refs[n:2 * n], refs[2 * n:]
        carried = (refs[:len(c_ins)], refs[len(c_ins) + 4 * n:len(c_ins) + 4 * n + len(c_outs)],
                   refs[len(c_ins) + 4 * n + len(c_outs):])
        outs = refs[len(c_ins):]
        if carry:
            @pl.when(pl.program_id(0) == 0)
            def _():
                carry.start(*carried)
        at = 0
        for t in range(n):
            if widths[t] == 1:
                g = g_refs[at][...]
            else:
                g = g_refs[at][0].astype(F32)
                for k in range(3):
                    g = g + g_refs[at + 1][k].astype(F32)
            at += widths[t]
            outs[4 * t][...] = g
            outs[4 * t + 1][...], outs[4 * t + 2][...], outs[4 * t + 3][...] = _adamw_math(
                w_refs[t][...], g, m_refs[t][...], v_refs[t][...])
        if carry:
            @pl.when(pl.program_id(0) == ADAMW_STEPS - 1)
            def _():
                if carry.relay is not None:
                    carry.relay(*carried)
                carry.finish(*carried)

    def rows_of(a):
        return a.shape[-2] // ADAMW_STEPS if a.shape[-2] % (ADAMW_STEPS * SUBLANES) == 0 else None

    def whole(a):
        r = rows_of(a)
        lead = (0,) * (a.ndim - 2)
        if r is None:
            zeros = (0,) * a.ndim
            return pl.BlockSpec(a.shape, lambda i, chip: zeros, pipeline_mode=pl.Buffered(1))
        return pl.BlockSpec(a.shape[:-2] + (r, a.shape[-1]), lambda i, chip: lead + (i, 0))

    g_specs = []
    for g in gs:
        if isinstance(g, (tuple, list)):
            r = rows_of(g[0])
            g_specs += [pl.BlockSpec((1, r, g[0].shape[2]), lambda i, chip: (chip[0], i, 0)), whole(g[1])]
        else:
            g_specs.append(whole(g))
    shapes, out_specs = [], []
    for w in ws:
        shapes += [jax.ShapeDtypeStruct(w.shape, F32)] * 4
        out_specs += [whole(w)] * 4
    res = pl.pallas_call(
        body, name=name, out_shape=tuple(shapes) + tuple(c_outs),
        grid_spec=pltpu.PrefetchScalarGridSpec(
            num_scalar_prefetch=1, grid=(ADAMW_STEPS,),
            in_specs=[whole(w) for w in ws] + g_specs + [whole(a) for a in ms + vs] + [_ANY] * len(c_ins),
            out_specs=tuple(out_specs) + (_ANY,) * len(c_outs), scratch_shapes=list(carry.sems) if carry else []),
        compiler_params=_params("arbitrary"),
    )(my_chip, *ws, *flat, *ms, *vs, *c_ins)
    return [tuple(res[4 * t:4 * t + 4]) for t in range(n)], tuple(res[4 * n:])


def adamw_replicated(w, partials, m, v, name):
    def body(w_ref, p_ref, m_ref, v_ref, g_ref, d_ref, mo_ref, vo_ref):
        g = p_ref[0]
        for k in range(1, N_DEV):
            g = g + p_ref[k]
        g_ref[...] = g
        d_ref[...], mo_ref[...], vo_ref[...] = _adamw_math(w_ref[...], g, m_ref[...], v_ref[...])

    shape = jax.ShapeDtypeStruct(w.shape, F32)
    return pl.pallas_call(body, name=name, out_shape=(shape,) * 4, compiler_params=_params())(w, partials, m, v)


PACK_COLS = 1024
PACK_ROW_ALIGN = 16

REPLICATED = ("ffn1_norm", "mix_norm", "conv_dw_b", "conv_ln_g", "conv_ln_b", "ffn2_norm", "final_norm", "gate_b", "attn_sinks")
WEIGHT_ORDER = ("ffn1_norm", "ffn1_w_gate", "ffn1_w_up", "ffn1_w_down", "mix_norm", "w_in", "conv_dw_w", "conv_dw_b", "conv_ln_g",
                "conv_ln_b", "conv_w_proj", "attn_sinks", "attn_w_o", "gate_b", "w_out", "ffn2_norm", "ffn2_w_gate", "ffn2_w_up",
                "ffn2_w_down", "final_norm")


def _to_rows(flat, lead):
    n = flat.shape[-1]
    rows = -(-n // PACK_COLS)
    flat = jnp.pad(flat, [(0, 0)] * lead + [(0, rows * PACK_COLS - n)])
    return flat.reshape(flat.shape[:lead] + (rows, PACK_COLS))


def _pad_rows(a, axis):
    rows = a.shape[axis]
    pad = -rows % PACK_ROW_ALIGN
    widths = [(0, 0)] * a.ndim
    widths[axis] = (0, pad)
    return jnp.pad(a, widths)


def _from_rows(rows, shape):
    n = 1
    for s in shape:
        n *= s
    return rows.reshape(rows.shape[:-2] + (-1,))[..., :n].reshape(rows.shape[:-2] + tuple(shape))


def _heads_slot_major(rows):
    group = rows.shape[0] // (N_KV_HEADS * HEAD_DIM)
    return rows.reshape(N_KV_HEADS, group, HEAD_DIM, rows.shape[1]).transpose(1, 0, 2, 3).reshape(rows.shape)


def _heads_kv_major(rows):
    group = rows.shape[0] // (N_KV_HEADS * HEAD_DIM)
    return rows.reshape(group, N_KV_HEADS, HEAD_DIM, rows.shape[1]).transpose(1, 0, 2, 3).reshape(rows.shape)


def _by_core(full_rows):
    return full_rows.reshape((N_DEV // 2, 2, full_rows.shape[0] // N_DEV, full_rows.shape[1]))


def kernel(x, positions, ffn1_norm, ffn1_w_gate, ffn1_w_up, ffn1_w_down, mix_norm, w_in, conv_dw_w, conv_dw_b, conv_ln_g, conv_ln_b, conv_w_proj, attn_sinks, attn_w_o, gate_b, w_out, ffn2_norm, ffn2_w_gate, ffn2_w_up, ffn2_w_down, final_norm, loss_target, m_ffn1_norm, m_ffn1_w_gate, m_ffn1_w_up, m_ffn1_w_down, m_mix_norm, m_w_in, m_conv_dw_w, m_conv_dw_b, m_conv_ln_g, m_conv_ln_b, m_conv_w_proj, m_attn_sinks, m_attn_w_o, m_gate_b, m_w_out, m_ffn2_norm, m_ffn2_w_gate, m_ffn2_w_up, m_ffn2_w_down, m_final_norm, v_ffn1_norm, v_ffn1_w_gate, v_ffn1_w_up, v_ffn1_w_down, v_mix_norm, v_w_in, v_conv_dw_w, v_conv_dw_b, v_conv_ln_g, v_conv_ln_b, v_conv_w_proj, v_attn_sinks, v_attn_w_o, v_gate_b, v_w_out, v_ffn2_norm, v_ffn2_w_gate, v_ffn2_w_up, v_ffn2_w_down, v_final_norm):
    given = dict(locals())
    shapes = {n: given[n].shape for n in WEIGHT_ORDER}
    w = {n: given[n].reshape(given[n].shape[-2:]) if given[n].ndim == 3 else given[n].reshape(1, -1) for n in WEIGHT_ORDER}
    m = {n: given["m_" + n].reshape(w[n].shape) for n in WEIGHT_ORDER}
    v = {n: given["v_" + n].reshape(w[n].shape) for n in WEIGHT_ORDER}
    my_x, my_y, my_c = _place()
    my_core = my_c.astype(jnp.int32).reshape(1)
    my_chip = (2 * my_x + my_y).astype(jnp.int32).reshape(1)
    xs, target = x[0], loss_target[0]
    T, D = xs.shape
    KV = N_KV_HEADS * HEAD_DIM
    K = w["conv_dw_w"].shape[0]

    def t16(n):
        return w[n].T.astype(BF16)

    def r16(n):
        return w[n].astype(BF16)

    blocks1 = [t16("ffn1_w_gate"), t16("ffn1_w_up"), r16("ffn1_w_down")]
    dw_bits = _pad_rows(_to_rows(lax.bitcast_convert_type(w["conv_dw_w"], BF16).reshape(-1), 0), 0)
    blocks2 = [t16("w_in"), r16("conv_w_proj"), r16("attn_w_o"), r16("w_out"), dw_bits]
    blocks3 = [t16("ffn2_w_gate"), t16("ffn2_w_up"), r16("ffn2_w_down")]
    cos, sin = rope_tables(positions[0])
    sink_col = jnp.repeat(w["attn_sinks"].reshape(-1), WINDOW).reshape(N_KV_HEADS, (D // KV) * WINDOW, 1)

    def full(gathered):
        return gathered.reshape(-1, gathered.shape[2])

    wgt1, wut1 = (full(g) for g in all_gather(blocks1[:2], "gather_ffn1_up"))
    (h1, a1, b1, s1), got = ffn_up(xs, w["ffn1_norm"], wgt1, wut1, "ffn1_up",
                                   carry=compose(gather_whole(blocks1[2:]), gather_send(blocks2[:1])))
    wd1 = full(got[0])
    (x1,), got = ffn_down(xs, s1, wd1, "ffn1_down", carry=compose(gather_forward(got[1:]), gather_send(blocks2[1:])))
    wint = full(got[0])
    wq = _heads_slot_major(wint[2 * D:3 * D])
    (h2, ga, gb, u0, q, sgc, sgt, kk, vv), got = mix_in_fwd(x1, w["mix_norm"], wint, wq, w["gate_b"], cos, sin, "mix_in_fwd",
                                                            carry=compose(gather_forward(got[1:]), gather_send(blocks3)))
    wcp, wo, wout = (full(g) for g in got[:3])
    wo = _heads_slot_major(wo)
    dw_full = lax.bitcast_convert_type(_from_rows(got[3], w["conv_dw_w"].shape + (2,)), F32)
    dw_full = dw_full.transpose(1, 0, 2).reshape(K, D)
    o, gath3 = attn_fwd(q, kk, vv, sink_col, "attn_fwd", carry=gather_forward(got[4:]))
    x2, u1, co, ao, merged = mix_out_fwd(x1, u0, o, sgc, sgt, dw_full, w["conv_dw_b"], w["conv_ln_g"], w["conv_ln_b"],
                                         wcp, wo, wout, "mix_out_fwd")
    wgt2, wut2, wd2 = (full(g) for g in gath3)
    loss, dx3, d_final, h3, a2, b2, s2 = ffn_fwd_loss(x2, w["ffn2_norm"], wgt2, wut2, wd2, w["final_norm"], target,
                                                      "ffn2_fwd_loss")

    small = {"final_norm": d_final}
    (da2, db2), _ = ffn_bwd_hidden(dx3, a2, b2, wd2, "ffn2_bwd_hidden")
    (dx2, small["ffn2_norm"]), _ = ffn_bwd_input(dx3, x2, w["ffn2_norm"], da2, db2, wgt2, wut2, "ffn2_bwd_input")
    core2 = [_by_core(g) for g in (wgrad(da2, h3, "ffn2_dwg")[0], wgrad(db2, h3, "ffn2_dwu")[0],
                                   wgrad(s2, dx3, "ffn2_dwd", b_scale=FFN_SCALE)[0])]
    (dgc, dgt, do, du1, dao, sums, g_wout, g_wcp), recv2 = mix_out_bwd(
        dx2, u1, co, ao, sgc, sgt, merged, w["conv_ln_g"], w["conv_ln_b"], wcp, wo, wout, "mix_out_bwd",
        carry=swap_halves(core2))
    chip2 = pair_sum(my_core, core2, recv2, "ffn2_grads_pair_sum")
    small["gate_b"] = jnp.concatenate([sums[0:1], sums[1:2]], axis=1)
    small["conv_ln_g"], small["conv_ln_b"], small["conv_dw_b"] = sums[2:3], sums[3:4], sums[4:5]
    g_wo =_heads_kv_major(wgrad(o, dao, "dw_attn_o")[0])
    (dga, dgb, g_dw), got2 = conv_bwd(du1, u0, ga, gb, dw_full, "conv_bwd", carry=exchange_between_chips(chip2))
    dq, dk, dv, dsink = attn_bwd(q, kk, vv, do, sink_col, cos, sin, "attn_bwd")
    small["attn_sinks"] = dsink[:D // KV, :N_KV_HEADS].T.reshape(1, -1)
    pieces = [dga, dgb, dq, dk, dv, dgc, dgt]
    dx1, small["mix_norm"] = mix_in_bwd(dx2, x1, w["mix_norm"], wint, wq, pieces, "mix_in_bwd")
    group = D // KV
    q_moves = [(2 * D + HEAD_DIM * (group * g + hh), 2 * D + HEAD_DIM * (N_KV_HEADS * hh + g), HEAD_DIM)
               for g in range(N_KV_HEADS) for hh in range(group)]
    g_wint = wgrad_stacked(pieces[:3], h2, "dw_in_a", wint.shape[0], 0, moves=[(0, 0, 2 * D)] + q_moves)
    g_wint = wgrad_stacked(pieces[3:], h2, "dw_in_b", wint.shape[0], 3 * D, into=g_wint)
    corem = [_by_core(a) for a in (g_wint, g_wcp, g_wo, g_wout)]

    (da1, db1), recvm = ffn_bwd_hidden(dx1, a1, b1, wd1, "ffn1_bwd_hidden", carry=swap_halves(corem))
    chipm = pair_sum(my_core, corem, recvm, "mix_grads_pair_sum")
    g1c, gotm_a = wgrad(s1, dx1, "ffn1_dwd", carry=exchange_between_chips(chipm[:1]), b_scale=FFN_SCALE)
    g1a, gotm_b = wgrad(da1, h1, "ffn1_dwg", carry=exchange_between_chips(chipm[1:]))
    early = [_by_core(g1a), _by_core(g1c)]
    g1b, recv_early = wgrad(db1, h1, "ffn1_dwu", carry=swap_halves(early))
    (recv_late,) = run_exchange(swap_halves([_by_core(g1b)]), "ffn1_grads_swap")
    core1 = [early[0], _by_core(g1b), early[1]]
    chip1 = pair_sum(my_core, core1, [recv_early[0], recv_late, recv_early[1]], "ffn1_grads_pair_sum")
    (grad_x, small["ffn1_norm"]), got1 = ffn_bwd_input(dx1, xs, w["ffn1_norm"], da1, db1, wgt1, wut1, "ffn1_bwd_input",
                                                       carry=exchange_between_chips(chip1))

    gotm = gotm_a + gotm_b
    grad_src = {"ffn1_w_gate": (chip1[0], got1[0]), "ffn1_w_up": (chip1[1], got1[1]), "ffn1_w_down": (chip1[2], got1[2]),
                "ffn2_w_gate": (chip2[0], got2[0]), "ffn2_w_up": (chip2[1], got2[1]), "ffn2_w_down": (chip2[2], got2[2]),
                "w_in": (chipm[0], gotm[0]), "conv_w_proj": (chipm[1], gotm[1]), "attn_w_o": (chipm[2], gotm[2]),
                "w_out": (chipm[3], gotm[3])}
    grads = {}

    def pack_small(d, taps, extra):
        rows = [_to_rows(d[n].reshape(-1), 0) for n in REPLICATED] + [taps, _to_rows(extra.reshape(-1), 0)]
        return _pad_rows(jnp.concatenate(rows, axis=0), 0)

    def like(a, ref):
        return a if a.shape == ref.shape else a.T

    def adamw_group(names, name, carry=None):
        refs = [grad_src[n][0][0] if isinstance(grad_src[n], tuple) else grad_src[n] for n in names]
        res, carried = adamw(my_chip, [like(w[n], r) for n, r in zip(names, refs)], [grad_src[n] for n in names],
                             [like(m[n], r) for n, r in zip(names, refs)], [like(v[n], r) for n, r in zip(names, refs)], name,
                             carry=carry)
        for n, outs in zip(names, res):
            grads[n], delta[n], new_m[n], new_v[n] = (like(a, w[n]) for a in outs)
        return carried

    delta, new_m, new_v = {}, {}, {}
    zero, no_taps = jnp.zeros((1, LANES), F32), jnp.zeros((K, D), F32)
    (shares,) = adamw_group(("ffn2_w_gate", "ffn2_w_up", "ffn2_w_down"), "adamw_ffn2",
                            carry=gather_whole([pack_small(small, g_dw, loss)]))
    g_s, d_s, m_s, v_s = adamw_replicated(pack_small(w, no_taps, zero), shares, pack_small(m, no_taps, zero),
                                          pack_small(v, no_taps, zero), "adamw_replicated")
    off = 0
    for n in REPLICATED:
        r = -(-w[n].shape[1] // PACK_COLS)
        grads[n], delta[n], new_m[n], new_v[n] = (_from_rows(a[off:off + r], w[n].shape) for a in (g_s, d_s, m_s, v_s))
        off += r
    shard_cols = w["conv_dw_w"].shape[1]
    grad_src["conv_dw_w"] = lax.dynamic_slice_in_dim(g_s[off:off + K], _dev_index(my_x, my_y, my_c) * shard_cols, shard_cols,
                                                    axis=1)
    total_loss = g_s[off + K, 0]
    adamw_group(("ffn1_w_gate", "ffn1_w_up", "ffn1_w_down", "w_in", "conv_dw_w", "conv_w_proj", "attn_w_o", "w_out"),
                "adamw_ffn1_mix")

    out = [total_loss, grad_x[None]]
    for d in (grads, delta, new_m, new_v):
        out += [d[n].reshape(shapes[n]) for n in WEIGHT_ORDER]
    return tuple(out)
```
